```python
import math
import jax, jax.numpy as jnp
from jax import lax
import numpy as np

D_MODEL = 1024
BATCH = 8
SEQ = 16384
DEPTH = 2

HEAD_DIM = 64
DILATED_GROUPS = ((128, 1), (512, 4), (2048, 16))
N_DIL = len(DILATED_GROUPS)
HEADS_PER_GROUP = 8
N_ATTN_HEADS = N_DIL * HEADS_PER_GROUP
ATTN_WIDTH = N_ATTN_HEADS * HEAD_DIM
ATTN_OUT_WIDTH = HEADS_PER_GROUP * HEAD_DIM
ATTN_BLOCK = 128
N_REL_BUCKETS = 32
REL_MAX_DISTANCE = 2048

D_INNER = 2 * D_MODEL
SSM_HEAD_DIM = 64
N_SSM_HEADS = D_INNER // SSM_HEAD_DIM
N_SSM_GROUPS = 4
HEADS_PER_SSM_GROUP = N_SSM_HEADS // N_SSM_GROUPS
D_STATE = 128
CONV_WIDTH = 4
SSD_CHUNK = 128
XBC_WIDTH = D_INNER + 2 * N_SSM_GROUPS * D_STATE

N_BRANCHES = 2
D_FF = -(-8 * D_MODEL // (3 * 256)) * 256
IN_PROJ_WIDTH = 3 * ATTN_WIDTH + D_INNER + XBC_WIDTH + N_SSM_HEADS + N_BRANCHES * D_MODEL
EPS = 1e-6

kernel_name = "hybrid_dilated_attn_ssd_gated_block"


def rmsnorm(x, w):
    xf = x.astype(jnp.float32)
    y = xf * lax.rsqrt(jnp.mean(xf * xf, axis=-1, keepdims=True) + EPS)
    return (y * w.astype(jnp.float32)).astype(x.dtype)


def t5_causal_bucket(dist):
    max_exact = N_REL_BUCKETS // 2
    d_f = jnp.maximum(dist, 1).astype(jnp.float32)
    large = max_exact + (jnp.log(d_f / max_exact) / math.log(REL_MAX_DISTANCE / max_exact)
                         * (N_REL_BUCKETS - max_exact)).astype(jnp.int32)
    large = jnp.minimum(large, N_REL_BUCKETS - 1)
    return jnp.where(dist < max_exact, dist, large)


def rel_bias_block(rel_bias_g, dilation, n_steps):
    qi = jnp.arange(ATTN_BLOCK)[:, None]
    kj = jnp.arange(2 * ATTN_BLOCK)[None, :]
    steps = jnp.clip(qi + ATTN_BLOCK - kj, 0, n_steps)
    bucket = t5_causal_bucket(steps * dilation)
    return jnp.transpose(rel_bias_g[bucket], (2, 0, 1)).astype(jnp.float32)


def dilated_window_attention(q, k, v, bias, dilation, n_steps):
    b, s, h, dh = q.shape
    seg = s // dilation
    nb = -(-seg // ATTN_BLOCK)
    segp = nb * ATTN_BLOCK

    def to_blocks(t):
        t = t.reshape(b, seg, dilation, h, dh).transpose(0, 2, 1, 3, 4).reshape(b * dilation, seg, h, dh)
        t = jnp.pad(t, ((0, 0), (0, segp - seg), (0, 0), (0, 0)))
        return t.reshape(b * dilation, nb, ATTN_BLOCK, h, dh)

    def with_prev(t):
        prev = jnp.pad(t[:, :-1], ((0, 0), (1, 0), (0, 0), (0, 0), (0, 0)))
        return jnp.concatenate([prev, t], axis=2)

    qb = to_blocks(q)
    kw = with_prev(to_blocks(k))
    vw = with_prev(to_blocks(v))

    qi = jnp.arange(ATTN_BLOCK)[:, None]
    kj = jnp.arange(2 * ATTN_BLOCK)[None, :]
    steps = qi + ATTN_BLOCK - kj
    blk = jnp.arange(nb)[:, None, None]
    valid = (steps >= 0) & (steps <= n_steps) & (blk * ATTN_BLOCK - ATTN_BLOCK + kj >= 0)

    logits = jnp.einsum('znqhd,znkhd->znhqk', qb, kw).astype(jnp.float32) * (HEAD_DIM ** -0.5) + bias
    logits = jnp.where(valid[None, :, None], logits, -jnp.inf)
    m = jnp.max(logits, axis=-1, keepdims=True)
    p = jnp.exp(logits - m)
    den = jnp.sum(p, axis=-1, keepdims=True)
    o = jnp.einsum('znhqk,znkhd->znqhd', p, vw.astype(jnp.float32)) / jnp.swapaxes(den, 2, 3)
    lse = jnp.swapaxes((m + jnp.log(den))[..., 0], 2, 3)

    o = o.reshape(b, dilation, segp, h, dh)[:, :, :seg]
    o = jnp.swapaxes(o, 1, 2).reshape(b, s, h, dh)
    lse = lse.reshape(b, dilation, segp, h)[:, :, :seg]
    lse = jnp.swapaxes(lse, 1, 2).reshape(b, s, h)
    return o, lse


def causal_depthwise_conv(x, w, bias):
    c = x.shape[-1]
    y = lax.conv_general_dilated(x, w[:, None, :], window_strides=(1,),
                                 padding=((CONV_WIDTH - 1, 0),),
                                 dimension_numbers=('NWC', 'WIO', 'NWC'),
                                 feature_group_count=c)
    return y + bias


def ssd_chunked_scan(xh, dt, a, bm, cm):
    b, s = xh.shape[:2]
    nc = s // SSD_CHUNK

    def chunks(t):
        return jnp.swapaxes(t.reshape(b, nc, SSD_CHUNK, *t.shape[2:]), 0, 1)

    causal = jnp.tril(jnp.ones((SSD_CHUNK, SSD_CHUNK), dtype=bool))[None, :, :, None, None]

    def step(state, inp):
        x_, dt_, b_, c_ = inp
        la = jnp.cumsum(dt_ * a, axis=1)
        seg = la[:, :, None] - la[:, None, :]
        decay = jnp.exp(jnp.where(causal, seg, -jnp.inf))
        cb = jnp.einsum('bign,bjgn->bijg', c_, b_)
        xdt = x_ * dt_[..., None]
        y_intra = jnp.einsum('bijg,bijgh,bjghp->bighp', cb, decay, xdt)
        y_inter = jnp.einsum('bign,bghpn->bighp', c_, state) * jnp.exp(la)[..., None]
        to_end = jnp.exp(la[:, -1:] - la)
        new_state = (state * jnp.exp(la[:, -1])[..., None, None]
                     + jnp.einsum('bjgn,bjgh,bjghp->bghpn', b_, to_end, xdt))
        return new_state, y_intra + y_inter

    state0 = jnp.zeros((b, N_SSM_GROUPS, HEADS_PER_SSM_GROUP, SSM_HEAD_DIM, D_STATE), jnp.float32)
    _, ys = lax.scan(step, state0, (chunks(xh), chunks(dt), chunks(bm), chunks(cm)))
    return jnp.swapaxes(ys, 0, 1).reshape(b, s, N_SSM_GROUPS, HEADS_PER_SSM_GROUP, SSM_HEAD_DIM)


def hybrid_mixer(xn, w_in, conv_w, conv_b, dt_bias, a_log, d_skip, ssm_norm_w,
                 w_attn_branch, w_ssm_branch, w_out, attn_biases):
    b, s, _ = xn.shape
    proj = xn @ w_in
    q, k, v, z, xbc, dt_raw, gate_logits = jnp.split(
        proj, np.cumsum([ATTN_WIDTH, ATTN_WIDTH, ATTN_WIDTH, D_INNER, XBC_WIDTH, N_SSM_HEADS]).tolist(), axis=-1)

    q = q.reshape(b, s, N_DIL, HEADS_PER_GROUP, HEAD_DIM)
    k = k.reshape(b, s, N_DIL, HEADS_PER_GROUP, HEAD_DIM)
    v = v.reshape(b, s, N_DIL, HEADS_PER_GROUP, HEAD_DIM)
    outs, lses = [], []
    for g, (window, dil) in enumerate(DILATED_GROUPS):
        o_g, lse_g = dilated_window_attention(q[:, :, g], k[:, :, g], v[:, :, g],
                                              attn_biases[g], dil, window // dil)
        outs.append(o_g)
        lses.append(lse_g)
    o = jnp.stack(outs, axis=2)
    alpha = jax.nn.softmax(jnp.stack(lses, axis=2), axis=2)
    attn = jnp.sum(alpha[..., None] * o, axis=2).reshape(b, s, ATTN_OUT_WIDTH).astype(xn.dtype)

    xbc = jax.nn.silu(causal_depthwise_conv(xbc, conv_w, conv_b))
    xs, bm, cm = jnp.split(xbc, [D_INNER, D_INNER + N_SSM_GROUPS * D_STATE], axis=-1)
    xh = xs.reshape(b, s, N_SSM_GROUPS, HEADS_PER_SSM_GROUP, SSM_HEAD_DIM).astype(jnp.float32)
    bm = bm.reshape(b, s, N_SSM_GROUPS, D_STATE).astype(jnp.float32)
    cm = cm.reshape(b, s, N_SSM_GROUPS, D_STATE).astype(jnp.float32)
    dt = jax.nn.softplus(dt_raw.astype(jnp.float32) + dt_bias.astype(jnp.float32))
    dt = dt.reshape(b, s, N_SSM_GROUPS, HEADS_PER_SSM_GROUP)
    a = -jnp.exp(a_log.astype(jnp.float32)).reshape(N_SSM_GROUPS, HEADS_PER_SSM_GROUP)
    y = ssd_chunked_scan(xh, dt, a, bm, cm)
    y = y + xh * d_skip.astype(jnp.float32).reshape(N_SSM_GROUPS, HEADS_PER_SSM_GROUP)[..., None]
    yg = (y.reshape(b, s, D_INNER) * jax.nn.silu(z.astype(jnp.float32))).reshape(b, s, N_SSM_GROUPS, D_INNER // N_SSM_GROUPS)
    yg = yg * lax.rsqrt(jnp.mean(yg * yg, axis=-1, keepdims=True) + EPS)
    ssm = (yg.reshape(b, s, D_INNER) * ssm_norm_w.astype(jnp.float32)).astype(xn.dtype)

    gates = jax.nn.sigmoid(gate_logits.astype(jnp.float32)).reshape(b, s, N_BRANCHES, D_MODEL)
    merged = (gates[:, :, 0] * (attn @ w_attn_branch).astype(jnp.float32)
              + gates[:, :, 1] * (ssm @ w_ssm_branch).astype(jnp.float32)).astype(xn.dtype)
    return merged @ w_out


def swiglu(xn, w_ffn_in, w_ffn_out):
    gate, up = jnp.split(xn @ w_ffn_in, 2, axis=-1)
    return (jax.nn.silu(gate) * up) @ w_ffn_out


def _fwd_setup_inputs(seed: int = 0) -> dict:
    key = jax.random.key(seed)
    ks = jax.random.split(key, 20)
    f32 = jnp.float32
    nrm = lambda k, shape, scale: jax.random.normal(k, shape, f32) * scale
    dt0 = jnp.exp(jax.random.uniform(ks[5], (DEPTH, N_SSM_HEADS), f32, math.log(1e-3), math.log(1e-1)))
    return {
        "x": nrm(ks[0], (BATCH, SEQ, D_MODEL), 1.0),
        "norm1_w": 1.0 + nrm(ks[1], (DEPTH, D_MODEL), 0.02),
        "w_in": nrm(ks[2], (DEPTH, D_MODEL, IN_PROJ_WIDTH), D_MODEL ** -0.5),
        "conv_w": nrm(ks[3], (DEPTH, CONV_WIDTH, XBC_WIDTH), CONV_WIDTH ** -0.5),
        "conv_b": nrm(ks[4], (DEPTH, XBC_WIDTH), 0.02),
        "dt_bias": dt0 + jnp.log(-jnp.expm1(-dt0)),
        "a_log": jnp.log(jax.random.uniform(ks[6], (DEPTH, N_SSM_HEADS), f32, 1.0, 16.0)),
        "d_skip": 1.0 + nrm(ks[7], (DEPTH, N_SSM_HEADS), 0.1),
        "ssm_norm_w": 1.0 + nrm(ks[8], (DEPTH, D_INNER), 0.02),
        "w_attn_branch": nrm(ks[9], (DEPTH, ATTN_OUT_WIDTH, D_MODEL), ATTN_OUT_WIDTH ** -0.5),
        "w_ssm_branch": nrm(ks[10], (DEPTH, D_INNER, D_MODEL), D_INNER ** -0.5),
        "w_out": nrm(ks[11], (DEPTH, D_MODEL, D_MODEL), D_MODEL ** -0.5),
        "norm2_w": 1.0 + nrm(ks[12], (DEPTH, D_MODEL), 0.02),
        "w_ffn_in": nrm(ks[13], (DEPTH, D_MODEL, 2 * D_FF), D_MODEL ** -0.5),
        "w_ffn_out": nrm(ks[14], (DEPTH, D_FF, D_MODEL), D_FF ** -0.5),
        "rel_bias": nrm(ks[15], (N_REL_BUCKETS, N_ATTN_HEADS), 0.5),
        "final_norm_w": 1.0 + nrm(ks[16], (D_MODEL,), 0.02),
    }


def _fwd_reference(x, norm1_w, w_in, conv_w, conv_b, dt_bias, a_log, d_skip, ssm_norm_w,
              w_attn_branch, w_ssm_branch, w_out, norm2_w, w_ffn_in, w_ffn_out,
              rel_bias, final_norm_w):
    attn_biases = [rel_bias_block(rel_bias[:, g * HEADS_PER_GROUP:(g + 1) * HEADS_PER_GROUP], dil, window // dil)
                   for g, (window, dil) in enumerate(DILATED_GROUPS)]
    h = x
    for layer in range(DEPTH):
        h = h + hybrid_mixer(rmsnorm(h, norm1_w[layer]), w_in[layer], conv_w[layer], conv_b[layer],
                             dt_bias[layer], a_log[layer], d_skip[layer], ssm_norm_w[layer],
                             w_attn_branch[layer], w_ssm_branch[layer], w_out[layer], attn_biases)
        h = h + swiglu(rmsnorm(h, norm2_w[layer]), w_ffn_in[layer], w_ffn_out[layer])
    return rmsnorm(h, final_norm_w)


import jax as _jax
import jax.numpy as _jnp

TWIN_FORMAT = 'train_step'
FWD_PARAMS = ['x', 'norm1_w', 'w_in', 'conv_w', 'conv_b', 'dt_bias', 'a_log', 'd_skip', 'ssm_norm_w', 'w_attn_branch', 'w_ssm_branch', 'w_out', 'norm2_w', 'w_ffn_in', 'w_ffn_out', 'rel_bias', 'final_norm_w']
TWIN_WEIGHTS = ['norm1_w', 'w_in', 'conv_w', 'conv_b', 'dt_bias', 'a_log', 'd_skip', 'ssm_norm_w', 'w_attn_branch', 'w_ssm_branch', 'w_out', 'norm2_w', 'w_ffn_in', 'w_ffn_out', 'rel_bias', 'final_norm_w']
TWIN_DIFF_INPUT = 'x'
TWIN_INPUTS = ['x', 'norm1_w', 'w_in', 'conv_w', 'conv_b', 'dt_bias', 'a_log', 'd_skip', 'ssm_norm_w', 'w_attn_branch', 'w_ssm_branch', 'w_out', 'norm2_w', 'w_ffn_in', 'w_ffn_out', 'rel_bias', 'final_norm_w', 'loss_target', 'm_norm1_w', 'm_w_in', 'm_conv_w', 'm_conv_b', 'm_dt_bias', 'm_a_log', 'm_d_skip', 'm_ssm_norm_w', 'm_w_attn_branch', 'm_w_ssm_branch', 'm_w_out', 'm_norm2_w', 'm_w_ffn_in', 'm_w_ffn_out', 'm_rel_bias', 'm_final_norm_w', 'v_norm1_w', 'v_w_in', 'v_conv_w', 'v_conv_b', 'v_dt_bias', 'v_a_log', 'v_d_skip', 'v_ssm_norm_w', 'v_w_attn_branch', 'v_w_ssm_branch', 'v_w_out', 'v_norm2_w', 'v_w_ffn_in', 'v_w_ffn_out', 'v_rel_bias', 'v_final_norm_w']
TWIN_OUTPUTS = ['loss', 'grad_x', 'grad_norm1_w', 'grad_w_in', 'grad_conv_w', 'grad_conv_b', 'grad_dt_bias', 'grad_a_log', 'grad_d_skip', 'grad_ssm_norm_w', 'grad_w_attn_branch', 'grad_w_ssm_branch', 'grad_w_out', 'grad_norm2_w', 'grad_w_ffn_in', 'grad_w_ffn_out', 'grad_rel_bias', 'grad_final_norm_w', 'delta_norm1_w', 'delta_w_in', 'delta_conv_w', 'delta_conv_b', 'delta_dt_bias', 'delta_a_log', 'delta_d_skip', 'delta_ssm_norm_w', 'delta_w_attn_branch', 'delta_w_ssm_branch', 'delta_w_out', 'delta_norm2_w', 'delta_w_ffn_in', 'delta_w_ffn_out', 'delta_rel_bias', 'delta_final_norm_w', 'new_m_norm1_w', 'new_m_w_in', 'new_m_conv_w', 'new_m_conv_b', 'new_m_dt_bias', 'new_m_a_log', 'new_m_d_skip', 'new_m_ssm_norm_w', 'new_m_w_attn_branch', 'new_m_w_ssm_branch', 'new_m_w_out', 'new_m_norm2_w', 'new_m_w_ffn_in', 'new_m_w_ffn_out', 'new_m_rel_bias', 'new_m_final_norm_w', 'new_v_norm1_w', 'new_v_w_in', 'new_v_conv_w', 'new_v_conv_b', 'new_v_dt_bias', 'new_v_a_log', 'new_v_d_skip', 'new_v_ssm_norm_w', 'new_v_w_attn_branch', 'new_v_w_ssm_branch', 'new_v_w_out', 'new_v_norm2_w', 'new_v_w_ffn_in', 'new_v_w_ffn_out', 'new_v_rel_bias', 'new_v_final_norm_w']
TWIN_LEAF_KINDS = {'loss': 'loss', 'grad_x': 'grad_x', 'grad_norm1_w': 'grad_w', 'grad_w_in': 'grad_w', 'grad_conv_w': 'grad_w', 'grad_conv_b': 'grad_w', 'grad_dt_bias': 'grad_w', 'grad_a_log': 'grad_w', 'grad_d_skip': 'grad_w', 'grad_ssm_norm_w': 'grad_w', 'grad_w_attn_branch': 'grad_w', 'grad_w_ssm_branch': 'grad_w', 'grad_w_out': 'grad_w', 'grad_norm2_w': 'grad_w', 'grad_w_ffn_in': 'grad_w', 'grad_w_ffn_out': 'grad_w', 'grad_rel_bias': 'grad_w', 'grad_final_norm_w': 'grad_w', 'delta_norm1_w': 'delta_w', 'delta_w_in': 'delta_w', 'delta_conv_w': 'delta_w', 'delta_conv_b': 'delta_w', 'delta_dt_bias': 'delta_w', 'delta_a_log': 'delta_w', 'delta_d_skip': 'delta_w', 'delta_ssm_norm_w': 'delta_w', 'delta_w_attn_branch': 'delta_w', 'delta_w_ssm_branch': 'delta_w', 'delta_w_out': 'delta_w', 'delta_norm2_w': 'delta_w', 'delta_w_ffn_in': 'delta_w', 'delta_w_ffn_out': 'delta_w', 'delta_rel_bias': 'delta_w', 'delta_final_norm_w': 'delta_w', 'new_m_norm1_w': 'new_m', 'new_m_w_in': 'new_m', 'new_m_conv_w': 'new_m', 'new_m_conv_b': 'new_m', 'new_m_dt_bias': 'new_m', 'new_m_a_log': 'new_m', 'new_m_d_skip': 'new_m', 'new_m_ssm_norm_w': 'new_m', 'new_m_w_attn_branch': 'new_m', 'new_m_w_ssm_branch': 'new_m', 'new_m_w_out': 'new_m', 'new_m_norm2_w': 'new_m', 'new_m_w_ffn_in': 'new_m', 'new_m_w_ffn_out': 'new_m', 'new_m_rel_bias': 'new_m', 'new_m_final_norm_w': 'new_m', 'new_v_norm1_w': 'new_v', 'new_v_w_in': 'new_v', 'new_v_conv_w': 'new_v', 'new_v_conv_b': 'new_v', 'new_v_dt_bias': 'new_v', 'new_v_a_log': 'new_v', 'new_v_d_skip': 'new_v', 'new_v_ssm_norm_w': 'new_v', 'new_v_w_attn_branch': 'new_v', 'new_v_w_ssm_branch': 'new_v', 'new_v_w_out': 'new_v', 'new_v_norm2_w': 'new_v', 'new_v_w_ffn_in': 'new_v', 'new_v_w_ffn_out': 'new_v', 'new_v_rel_bias': 'new_v', 'new_v_final_norm_w': 'new_v'}


def _forward(args):
    return _fwd_reference(*[args[k] for k in FWD_PARAMS])


def _output_shape():
    def fwd():
        inp = _fwd_setup_inputs(0)
        return _fwd_reference(*[inp[k] for k in FWD_PARAMS])
    out = _jax.eval_shape(fwd)
    return out.shape, out.dtype

N_MICROBATCH = 1
ADAM_LR = 0.001
ADAM_B1 = 0.9
ADAM_B2 = 0.999
ADAM_EPS = 1e-08
ADAM_WD = 0.01
ADAM_STEP = 10
PER_EXAMPLE_BATCH_AXIS = {'x': 0, 'loss_target': 0}
SHARED_INPUTS = []
_WEIGHT_DTYPES = {'norm1_w': _jnp.float32, 'w_in': _jnp.float32, 'conv_w': _jnp.float32, 'conv_b': _jnp.float32, 'dt_bias': _jnp.float32, 'a_log': _jnp.float32, 'd_skip': _jnp.float32, 'ssm_norm_w': _jnp.float32, 'w_attn_branch': _jnp.float32, 'w_ssm_branch': _jnp.float32, 'w_out': _jnp.float32, 'norm2_w': _jnp.float32, 'w_ffn_in': _jnp.float32, 'w_ffn_out': _jnp.float32, 'rel_bias': _jnp.float32, 'final_norm_w': _jnp.float32}
MOMENT_SCALE = {'norm1_w': 3.060895e-01, 'w_in': 8.961709e-02, 'conv_w': 1.177553e-01, 'conv_b': 1.791379e-01, 'dt_bias': 4.724946e-01, 'a_log': 3.742767e-01, 'd_skip': 7.148398e-01, 'ssm_norm_w': 1.455228e-01, 'w_attn_branch': 3.904214e-02, 'w_ssm_branch': 1.929223e-01, 'w_out': 1.932320e-01, 'norm2_w': 2.576042e-01, 'w_ffn_in': 1.048609e-01, 'w_ffn_out': 1.711346e-01, 'rel_bias': 4.904285e-02, 'final_norm_w': 1.279480e+02}


def _to_microbatches(a, axis):
    t = _jnp.moveaxis(a, axis, 0)
    t = t.reshape((N_MICROBATCH, t.shape[0] // N_MICROBATCH) + t.shape[1:])
    return _jnp.moveaxis(t, 1, axis + 1)


def setup_inputs(seed: int = 0) -> dict:
    inp = _fwd_setup_inputs(seed)
    key = _jax.random.fold_in(_jax.random.key(seed), 7919)
    shape, _ = _output_shape()
    out = dict(inp)
    out["loss_target"] = _jax.random.normal(_jax.random.fold_in(key, 0), shape, _jnp.float32)
    for i, name in enumerate(TWIN_WEIGHTS):
        w = inp[name].astype(_jnp.float32)
        if MOMENT_SCALE is None:
            s = _jnp.sqrt(_jnp.mean(_jnp.square(w)) + 1e-30)
        else:
            s = MOMENT_SCALE[name]
        km, kv = _jax.random.split(_jax.random.fold_in(key, i + 1))
        out[name] = w
        out["m_" + name] = s * _jax.random.normal(km, w.shape, _jnp.float32)
        out["v_" + name] = (s * s) * _jax.random.uniform(kv, w.shape, _jnp.float32, 0.5, 1.5)
    if N_MICROBATCH > 1:
        for name, axis in PER_EXAMPLE_BATCH_AXIS.items():
            out[name] = _to_microbatches(out[name], axis)
    return {'x': out['x'], 'norm1_w': out['norm1_w'], 'w_in': out['w_in'], 'conv_w': out['conv_w'], 'conv_b': out['conv_b'], 'dt_bias': out['dt_bias'], 'a_log': out['a_log'], 'd_skip': out['d_skip'], 'ssm_norm_w': out['ssm_norm_w'], 'w_attn_branch': out['w_attn_branch'], 'w_ssm_branch': out['w_ssm_branch'], 'w_out': out['w_out'], 'norm2_w': out['norm2_w'], 'w_ffn_in': out['w_ffn_in'], 'w_ffn_out': out['w_ffn_out'], 'rel_bias': out['rel_bias'], 'final_norm_w': out['final_norm_w'], 'loss_target': out['loss_target'], 'm_norm1_w': out['m_norm1_w'], 'm_w_in': out['m_w_in'], 'm_conv_w': out['m_conv_w'], 'm_conv_b': out['m_conv_b'], 'm_dt_bias': out['m_dt_bias'], 'm_a_log': out['m_a_log'], 'm_d_skip': out['m_d_skip'], 'm_ssm_norm_w': out['m_ssm_norm_w'], 'm_w_attn_branch': out['m_w_attn_branch'], 'm_w_ssm_branch': out['m_w_ssm_branch'], 'm_w_out': out['m_w_out'], 'm_norm2_w': out['m_norm2_w'], 'm_w_ffn_in': out['m_w_ffn_in'], 'm_w_ffn_out': out['m_w_ffn_out'], 'm_rel_bias': out['m_rel_bias'], 'm_final_norm_w': out['m_final_norm_w'], 'v_norm1_w': out['v_norm1_w'], 'v_w_in': out['v_w_in'], 'v_conv_w': out['v_conv_w'], 'v_conv_b': out['v_conv_b'], 'v_dt_bias': out['v_dt_bias'], 'v_a_log': out['v_a_log'], 'v_d_skip': out['v_d_skip'], 'v_ssm_norm_w': out['v_ssm_norm_w'], 'v_w_attn_branch': out['v_w_attn_branch'], 'v_w_ssm_branch': out['v_w_ssm_branch'], 'v_w_out': out['v_w_out'], 'v_norm2_w': out['v_norm2_w'], 'v_w_ffn_in': out['v_w_ffn_in'], 'v_w_ffn_out': out['v_w_ffn_out'], 'v_rel_bias': out['v_rel_bias'], 'v_final_norm_w': out['v_final_norm_w']}


def _loss(weights, diff, rest, loss_target):
    with _jax.named_scope("forward"):
        args = {**rest, TWIN_DIFF_INPUT: diff, **{k: w.astype(_WEIGHT_DTYPES[k]) for k, w in weights.items()}}
        y = _forward(args)
    with _jax.named_scope("loss_head"):
        err = _jnp.square(y.astype(_jnp.float32) - loss_target)
        return 0.5 * _jnp.sum(_jnp.mean(err, axis=-1)) if err.ndim else 0.5 * err


def _adamw(w, g, m, v):
    m = ADAM_B1 * m + (1.0 - ADAM_B1) * g
    v = ADAM_B2 * v + (1.0 - ADAM_B2) * _jnp.square(g)
    m_hat = m / (1.0 - ADAM_B1 ** ADAM_STEP)
    v_hat = v / (1.0 - ADAM_B2 ** ADAM_STEP)
    delta = -ADAM_LR * (m_hat / (_jnp.sqrt(v_hat) + ADAM_EPS) + ADAM_WD * w)
    return delta, m, v


def reference(x, norm1_w, w_in, conv_w, conv_b, dt_bias, a_log, d_skip, ssm_norm_w, w_attn_branch, w_ssm_branch, w_out, norm2_w, w_ffn_in, w_ffn_out, rel_bias, final_norm_w, loss_target, m_norm1_w, m_w_in, m_conv_w, m_conv_b, m_dt_bias, m_a_log, m_d_skip, m_ssm_norm_w, m_w_attn_branch, m_w_ssm_branch, m_w_out, m_norm2_w, m_w_ffn_in, m_w_ffn_out, m_rel_bias, m_final_norm_w, v_norm1_w, v_w_in, v_conv_w, v_conv_b, v_dt_bias, v_a_log, v_d_skip, v_ssm_norm_w, v_w_attn_branch, v_w_ssm_branch, v_w_out, v_norm2_w, v_w_ffn_in, v_w_ffn_out, v_rel_bias, v_final_norm_w):
    given = dict(x=x, norm1_w=norm1_w, w_in=w_in, conv_w=conv_w, conv_b=conv_b, dt_bias=dt_bias, a_log=a_log, d_skip=d_skip, ssm_norm_w=ssm_norm_w, w_attn_branch=w_attn_branch, w_ssm_branch=w_ssm_branch, w_out=w_out, norm2_w=norm2_w, w_ffn_in=w_ffn_in, w_ffn_out=w_ffn_out, rel_bias=rel_bias, final_norm_w=final_norm_w, loss_target=loss_target, m_norm1_w=m_norm1_w, m_w_in=m_w_in, m_conv_w=m_conv_w, m_conv_b=m_conv_b, m_dt_bias=m_dt_bias, m_a_log=m_a_log, m_d_skip=m_d_skip, m_ssm_norm_w=m_ssm_norm_w, m_w_attn_branch=m_w_attn_branch, m_w_ssm_branch=m_w_ssm_branch, m_w_out=m_w_out, m_norm2_w=m_norm2_w, m_w_ffn_in=m_w_ffn_in, m_w_ffn_out=m_w_ffn_out, m_rel_bias=m_rel_bias, m_final_norm_w=m_final_norm_w, v_norm1_w=v_norm1_w, v_w_in=v_w_in, v_conv_w=v_conv_w, v_conv_b=v_conv_b, v_dt_bias=v_dt_bias, v_a_log=v_a_log, v_d_skip=v_d_skip, v_ssm_norm_w=v_ssm_norm_w, v_w_attn_branch=v_w_attn_branch, v_w_ssm_branch=v_w_ssm_branch, v_w_out=v_w_out, v_norm2_w=v_norm2_w, v_w_ffn_in=v_w_ffn_in, v_w_ffn_out=v_w_ffn_out, v_rel_bias=v_rel_bias, v_final_norm_w=v_final_norm_w)
    weights = {n: given[n] for n in TWIN_WEIGHTS}
    shared = {n: given[n] for n in SHARED_INPUTS}
    per_example = {n: given[n] for n in ['x']}
    grad_fn = _jax.value_and_grad(_loss, argnums=(0, 1))

    def one_microbatch(ex, loss_target):
        ex = dict(ex)
        diff = ex.pop(TWIN_DIFF_INPUT)
        return grad_fn(weights, diff, {**shared, **ex}, loss_target)

    if N_MICROBATCH == 1:
        loss, (grad_w, grad_x) = one_microbatch(per_example, given["loss_target"])
    else:
        def body(carry, xs):
            loss_sum, grad_sum = carry
            l_k, (gw_k, gx_k) = one_microbatch(xs[0], xs[1])
            with _jax.named_scope("update"):
                return (loss_sum + l_k, _jax.tree.map(_jnp.add, grad_sum, gw_k)), gx_k

        init = (_jnp.zeros((), _jnp.float32), _jax.tree.map(_jnp.zeros_like, weights))
        (loss, grad_w), grad_x = _jax.lax.scan(body, init, (per_example, given["loss_target"]))
    with _jax.named_scope("update"):
        delta_w, new_m, new_v = {}, {}, {}
        for n in TWIN_WEIGHTS:
            delta_w[n], new_m[n], new_v[n] = _adamw(weights[n], grad_w[n], given["m_" + n], given["v_" + n])
    return (loss, grad_x, *[grad_w[n] for n in TWIN_WEIGHTS], *[delta_w[n] for n in TWIN_WEIGHTS],
            *[new_m[n] for n in TWIN_WEIGHTS], *[new_v[n] for n in TWIN_WEIGHTS])
```

```python
import functools
import math

import numpy as np
import jax
import jax.numpy as jnp
from jax import lax
from jax.experimental import pallas as pl
from jax.experimental.pallas import tpu as pltpu

F32, BF16 = jnp.float32, jnp.bfloat16
S = jax.ShapeDtypeStruct
MESH = pl.DeviceIdType.MESH

D_MODEL = 1024
DEPTH = 2
HEAD_DIM = 64
DILATED_GROUPS = ((128, 1), (512, 4), (2048, 16))
N_DIL = 3
HEADS_PER_GROUP = 8
ATTN_WIDTH = 1536
ATTN_OUT_WIDTH = 512
ATTN_BLOCK = 128
N_REL_BUCKETS = 32
REL_MAX_DISTANCE = 2048
D_INNER = 2048
N_SSM_HEADS = 32
N_SSM_GROUPS = 4
D_STATE = 128
CONV_WIDTH = 4
SSD_CHUNK = 128
XBC_WIDTH = 3072
D_FF = 2816
EPS = 1e-6
ADAM_LR, ADAM_B1, ADAM_B2, ADAM_EPS, ADAM_WD, ADAM_STEP = 0.001, 0.9, 0.999, 1e-08, 0.01, 10

N_DEV = 8
LANES = 128
VMEM_LIMIT = 56 * 1024 * 1024
NEG = -1e30
PACK_COLS = 1024
BIG = (("w_in", 1476), ("w_attn_branch", 64), ("w_ssm_branch", 256), ("w_out", 128), ("w_ffn_in", 704), ("w_ffn_out", 352))
ROWS_PER_LAYER = sum(r for _, r in BIG)
PACK_ROWS = DEPTH * ROWS_PER_LAYER
PACK_ROWS_PAD = 6016
AG_CHUNKS = 4

NT = (((1,), (1,)), ((), ()))
TN = (((0,), (0,)), ((), ()))


def _cparams(sem=None):
    return pltpu.CompilerParams(dimension_semantics=sem, vmem_limit_bytes=VMEM_LIMIT)


def _pick(n, target, mult=LANES):
    best = None
    for t in range(mult, min(n, target) + 1, mult):
        if n % t == 0:
            best = t
    return best or n


def _silu(x):
    return x * jax.nn.sigmoid(x)


def _dsilu(x):
    s = jax.nn.sigmoid(x)
    return s * (1.0 + x * (1.0 - s))


def _split2(x):
    hi = x.astype(BF16)
    lo = (x - hi.astype(F32)).astype(BF16)
    return hi, lo


def _split3(x):
    x1 = x.astype(BF16)
    r1 = x - x1.astype(F32)
    x2 = r1.astype(BF16)
    x3 = (r1 - x2.astype(F32)).astype(BF16)
    return x1, x2, x3


def _dotx_r(x, m, parts=3):
    xs = _split3(x) if parts == 3 else _split2(x)
    out = jnp.dot(xs[0], m, preferred_element_type=F32)
    for xi in xs[1:]:
        out = out + jnp.dot(xi, m, preferred_element_type=F32)
    return out


def _dotx_l(m, x, parts=3):
    xs = _split3(x) if parts == 3 else _split2(x)
    out = jnp.dot(m, xs[0], preferred_element_type=F32)
    for xi in xs[1:]:
        out = out + jnp.dot(m, xi, preferred_element_type=F32)
    return out


def _mm(a, b, *, ta=False, tb=False, out_dtype=F32, acc=None, name, tm=1024, tn=1536, tk=1024):
    M, K = (a.shape[1], a.shape[0]) if ta else a.shape
    N = b.shape[0] if tb else b.shape[1]
    tm, tn, tk = _pick(M, tm), _pick(N, tn), _pick(K, tk)
    nk = K // tk
    dims = (((0 if ta else 1,), (1 if tb else 0,)), ((), ()))
    has_acc = acc is not None

    def body(*refs):
        if has_acc:
            a_ref, b_ref, c_ref, o_ref, acc_ref = refs
        else:
            a_ref, b_ref, o_ref, acc_ref = refs
        k = pl.program_id(2)
        part = lax.dot_general(a_ref[...].astype(BF16), b_ref[...].astype(BF16), dims, preferred_element_type=F32)

        @pl.when(k == 0)
        def _():
            acc_ref[...] = part + c_ref[...].astype(F32) if has_acc else part

        @pl.when(k > 0)
        def _():
            acc_ref[...] += part

        @pl.when(k == nk - 1)
        def _():
            o_ref[...] = acc_ref[...].astype(o_ref.dtype)

    a_spec = pl.BlockSpec((tk, tm), lambda i, j, k: (k, i)) if ta else pl.BlockSpec((tm, tk), lambda i, j, k: (i, k))
    b_spec = pl.BlockSpec((tn, tk), lambda i, j, k: (j, k)) if tb else pl.BlockSpec((tk, tn), lambda i, j, k: (k, j))
    o_spec = pl.BlockSpec((tm, tn), lambda i, j, k: (i, j))
    in_specs, args = [a_spec, b_spec], [a, b]
    if has_acc:
        in_specs.append(o_spec)
        args.append(acc)
    return pl.pallas_call(
        body, grid=(M // tm, N // tn, nk), in_specs=in_specs, out_specs=o_spec,
        out_shape=S((M, N), out_dtype), scratch_shapes=[pltpu.VMEM((tm, tn), F32)],
        compiler_params=_cparams(("parallel", "parallel", "arbitrary")), name=name)(*args)


def _rowwise(fn, rows, fulls, outs, accs=(), *, tm, name):
    rows = [r if isinstance(r, tuple) else (r, r.shape[1], 0) for r in rows]
    T = rows[0][0].shape[0]
    tm = T if T <= tm else _pick(T, tm, 8)
    nr, nf, no, na = len(rows), len(fulls), len(outs), len(accs)

    def body(*refs):
        r, f = refs[:nr], refs[nr:nr + nf]
        o, a = refs[nr + nf:nr + nf + no], refs[nr + nf + no:]
        ro, ra = fn(*[x[...] for x in r], *[x[...] for x in f])
        for ref, val in zip(o, ro):
            ref[...] = val.astype(ref.dtype)
        if na:
            @pl.when(pl.program_id(0) == 0)
            def _():
                for ref in a:
                    ref[...] = jnp.zeros_like(ref)
            for ref, val in zip(a, ra):
                ref[...] += val

    in_specs = [pl.BlockSpec((tm, w), functools.partial(lambda i, cb: (i, cb), cb=cb)) for _, w, cb in rows]
    in_specs += [pl.BlockSpec(f.shape, lambda i: (0, 0)) for f in fulls]
    out_specs = [pl.BlockSpec((tm, c), lambda i: (i, 0)) for c, _ in outs] + [pl.BlockSpec(sh, lambda i: (0, 0)) for sh in accs]
    out_shape = [S((T, c), dt) for c, dt in outs] + [S(sh, F32) for sh in accs]
    return pl.pallas_call(
        body, grid=(T // tm,), in_specs=in_specs, out_specs=out_specs, out_shape=out_shape,
        compiler_params=_cparams(("arbitrary",)), name=name)(*[r[0] for r in rows], *fulls)


def _rmsnorm_fwd(h, w, name):
    def fn(h, w):
        r = lax.rsqrt(jnp.mean(h * h, axis=-1, keepdims=True) + EPS)
        return [h * r * w], []
    return _rowwise(fn, [h], [w.reshape(1, -1)], [(h.shape[1], BF16)], tm=512, name=name)[0]


def _rmsnorm_bwd(dxn, h, w, dres, name):
    def fn(dxn, h, dres, w):
        r = lax.rsqrt(jnp.mean(h * h, axis=-1, keepdims=True) + EPS)
        n = h * r
        dn = dxn * w
        dh = r * (dn - n * jnp.mean(dn * n, axis=-1, keepdims=True)) + dres
        return [dh], [jnp.sum(dxn * n, axis=0, keepdims=True)]
    D = h.shape[1]
    return _rowwise(fn, [dxn, h, dres], [w.reshape(1, -1)], [(D, F32)], [(1, D)], tm=256, name=name)


def _loss_head(h, w, tgt, name):
    D = h.shape[1]

    def fn(h, tgt, w):
        r = lax.rsqrt(jnp.mean(h * h, axis=-1, keepdims=True) + EPS)
        n = h * r
        e = n * w - tgt
        row_loss = 0.5 * jnp.mean(e * e, axis=-1, keepdims=True)
        dy = e * (1.0 / D)
        dn = dy * w
        dh = r * (dn - n * jnp.mean(dn * n, axis=-1, keepdims=True))
        return [dh], [jnp.sum(dy * n, axis=0, keepdims=True), jnp.broadcast_to(jnp.sum(row_loss, axis=0, keepdims=True), (1, LANES))]
    return _rowwise(fn, [h, tgt], [w.reshape(1, -1)], [(D, F32)], [(1, D), (1, LANES)], tm=256, name=name)


def _combine_fwd(os_, ls, name):
    def fn(o0, o1, o2, l0, l1, l2):
        m = jnp.maximum(jnp.maximum(l0, l1), l2)
        e0, e1, e2 = jnp.exp(l0 - m), jnp.exp(l1 - m), jnp.exp(l2 - m)
        attn = (e0 * o0 + e1 * o1 + e2 * o2) / (e0 + e1 + e2)
        return [attn, attn], []
    return _rowwise(fn, list(os_) + list(ls), [], [(ATTN_OUT_WIDTH, BF16), (ATTN_OUT_WIDTH, F32)], tm=512, name=name)


def _combine_bwd(dattn, attn, ls, head_ones, name):
    def fn(dattn, attn, l0, l1, l2, ones):
        m = jnp.maximum(jnp.maximum(l0, l1), l2)
        e0, e1, e2 = jnp.exp(l0 - m), jnp.exp(l1 - m), jnp.exp(l2 - m)
        inv = 1.0 / (e0 + e1 + e2)
        t = _dotx_r(dattn * attn, ones, parts=2)
        outs = []
        for e in (e0, e1, e2):
            al = e * inv
            outs += [al * dattn, al * t]
        return outs, []
    W = ATTN_OUT_WIDTH
    return _rowwise(fn, [dattn, attn] + list(ls), [head_ones], [(W, F32)] * 6, tm=512, name=name)


def _dt_fwd(dt_raw, dt_bias, name):
    def fn(raw, b):
        z = raw[:, :N_SSM_HEADS] + b
        return [jnp.maximum(z, 0.0) + jnp.log(1.0 + jnp.exp(-jnp.abs(z)))], []
    return _rowwise(fn, [dt_raw], [dt_bias.reshape(1, -1)], [(N_SSM_HEADS, F32)], tm=1024, name=name)[0]


def _dt_bwd(ddt_a, ddt_b, dt_raw, dt_bias, name):
    def fn(da, db, raw, b):
        g = (da + db) * jax.nn.sigmoid(raw[:, :N_SSM_HEADS] + b)
        pad = jnp.zeros((g.shape[0], LANES - N_SSM_HEADS), F32)
        return [jnp.concatenate([g, pad], axis=1)], [jnp.sum(g, axis=0, keepdims=True)]
    return _rowwise(fn, [ddt_a, ddt_b, dt_raw], [dt_bias.reshape(1, -1)], [(LANES, BF16)], [(1, N_SSM_HEADS)], tm=1024, name=name)


def _ssm_norm_fwd(y, z, w, name):
    G = D_INNER // N_SSM_GROUPS

    def fn(y, z, w):
        yg = y * _silu(z)
        outs = []
        for g in range(N_SSM_GROUPS):
            t = yg[:, g * G:(g + 1) * G]
            outs.append(t * lax.rsqrt(jnp.mean(t * t, axis=-1, keepdims=True) + EPS))
        return [jnp.concatenate(outs, axis=1) * w], []
    return _rowwise(fn, [y, z], [w.reshape(1, -1)], [(D_INNER, BF16)], tm=256, name=name)[0]


def _ssm_norm_bwd(dssm, y, z, w, name):
    G = D_INNER // N_SSM_GROUPS

    def fn(dssm, y, z, w):
        sz = _silu(z)
        yg = y * sz
        dn = dssm * w
        ns, dygs = [], []
        for g in range(N_SSM_GROUPS):
            t = yg[:, g * G:(g + 1) * G]
            r = lax.rsqrt(jnp.mean(t * t, axis=-1, keepdims=True) + EPS)
            n = t * r
            d = dn[:, g * G:(g + 1) * G]
            dygs.append(r * (d - n * jnp.mean(d * n, axis=-1, keepdims=True)))
            ns.append(n)
        n, dyg = jnp.concatenate(ns, axis=1), jnp.concatenate(dygs, axis=1)
        return [dyg * sz, dyg * y * _dsilu(z)], [jnp.sum(dssm * n, axis=0, keepdims=True)]
    return _rowwise(fn, [dssm, y, z], [w.reshape(1, -1)], [(D_INNER, F32), (D_INNER, BF16)], [(1, D_INNER)], tm=256, name=name)


def _gate_fwd(a, sb, gl, name):
    def fn(a, sb, gl):
        g = jax.nn.sigmoid(gl)
        return [g[:, :D_MODEL] * a + g[:, D_MODEL:] * sb], []
    return _rowwise(fn, [a, sb, gl], [], [(D_MODEL, BF16)], tm=512, name=name)[0]


def _gate_bwd(dm, a, sb, gl, name):
    def fn(dm, a, sb, gl):
        g = jax.nn.sigmoid(gl)
        g0, g1 = g[:, :D_MODEL], g[:, D_MODEL:]
        dgl = jnp.concatenate([dm * a * g0 * (1.0 - g0), dm * sb * g1 * (1.0 - g1)], axis=1)
        return [g0 * dm, g1 * dm, dgl], []
    return _rowwise(fn, [dm, a, sb, gl], [], [(D_MODEL, BF16), (D_MODEL, BF16), (2 * D_MODEL, BF16)], tm=512, name=name)


def _swiglu_fwd(u, name):
    def fn(u):
        return [_silu(u[:, :D_FF]) * u[:, D_FF:]], []
    return _rowwise(fn, [u], [], [(D_FF, BF16)], tm=256, name=name)[0]


def _swiglu_bwd(dact, u, name):
    def fn(dact, u):
        gate, up = u[:, :D_FF], u[:, D_FF:]
        return [jnp.concatenate([dact * up * _dsilu(gate), dact * _silu(gate)], axis=1)], []
    return _rowwise(fn, [dact, u], [], [(2 * D_FF, BF16)], tm=256, name=name)[0]


def _add_rows(a, b, name):
    def fn(a, b):
        return [a + b], []
    return _rowwise(fn, [a, b], [], [(a.shape[1], F32)], tm=512, name=name)[0]


def _add4_rows(a, b, c, d, name):
    def fn(a, b, c, d):
        return [((a + b) + c) + d], []
    return _rowwise(fn, [a, b, c, d], [], [(a.shape[1], F32)], tm=512, name=name)[0]


def _adamw(w, g, m, v, name):
    c1 = 1.0 - ADAM_B1 ** ADAM_STEP
    c2 = 1.0 - ADAM_B2 ** ADAM_STEP

    def fn(w, g, m, v):
        m = ADAM_B1 * m + (1.0 - ADAM_B1) * g
        v = ADAM_B2 * v + (1.0 - ADAM_B2) * (g * g)
        delta = -ADAM_LR * ((m / c1) / (jnp.sqrt(v / c2) + ADAM_EPS) + ADAM_WD * w)
        return [delta, m, v], []
    C = w.shape[1]
    return _rowwise(fn, [w, g, m, v], [], [(C, F32)] * 3, tm=256, name=name)


def _bias_consts(dilation, n_steps):
    qi = np.arange(ATTN_BLOCK)[:, None]
    kj = np.arange(2 * ATTN_BLOCK)[None, :]
    steps = qi + ATTN_BLOCK - kj
    valid = (steps >= 0) & (steps <= n_steps)
    dist = jnp.asarray(np.clip(steps, 0, n_steps) * dilation, jnp.int32)
    max_exact = N_REL_BUCKETS // 2
    d_f = jnp.maximum(dist, 1).astype(F32)
    large = max_exact + (jnp.log(d_f / max_exact) / math.log(REL_MAX_DISTANCE / max_exact)
                         * (N_REL_BUCKETS - max_exact)).astype(jnp.int32)
    large = jnp.minimum(large, N_REL_BUCKETS - 1)
    bucket = jnp.where(dist < max_exact, dist, large).reshape(-1)
    onehot = (bucket[None, :] == jnp.arange(N_REL_BUCKETS)[:, None]).astype(F32)
    return onehot, jnp.asarray(valid.reshape(1, -1), F32)


def _bias_gather(rel_g_t, onehot, valid, name):
    def body(r_ref, oh_ref, v_ref, o_ref):
        b = jnp.dot(r_ref[...], oh_ref[...], preferred_element_type=F32, precision=lax.Precision.HIGHEST)
        o_ref[...] = jnp.where(v_ref[...] > 0.5, b, NEG)
    return pl.pallas_call(body, out_shape=S((HEADS_PER_GROUP, onehot.shape[1]), F32), compiler_params=_cparams(), name=name)(rel_g_t, onehot, valid)


def _bias_scatter(dbias, onehot, name):
    def body(d_ref, oh_ref, o_ref):
        o_ref[...] = lax.dot_general(d_ref[...], oh_ref[...], NT, preferred_element_type=F32, precision=lax.Precision.HIGHEST)
    return pl.pallas_call(body, out_shape=S((HEADS_PER_GROUP, N_REL_BUCKETS), F32), compiler_params=_cparams(), name=name)(dbias, onehot)


def _attn_fwd(q, k, v, bias, g, d, name):
    T = q.shape[0]
    seg = T // d
    nb = seg // ATTN_BLOCK
    W = ATTN_OUT_WIDTH
    scale = HEAD_DIM ** -0.5

    def body(q_ref, kp_ref, kc_ref, vp_ref, vc_ref, b_ref, o_ref, l_ref):
        n = pl.program_id(1)
        qv = q_ref[...]
        kk = jnp.concatenate([kp_ref[...], kc_ref[...]], axis=0)
        vv = jnp.concatenate([vp_ref[...], vc_ref[...]], axis=0)
        col = lax.broadcasted_iota(jnp.int32, (ATTN_BLOCK, 2 * ATTN_BLOCK), 1)
        kill = jnp.logical_and(n == 0, col < ATTN_BLOCK)
        for h in range(HEADS_PER_GROUP):
            sl = slice(h * HEAD_DIM, (h + 1) * HEAD_DIM)
            s = lax.dot_general(qv[:, sl], kk[:, sl], NT, preferred_element_type=F32) * scale + b_ref[h]
            s = jnp.where(kill, NEG, s)
            m = jnp.max(s, axis=-1, keepdims=True)
            p = jnp.exp(s - m)
            den = jnp.sum(p, axis=-1, keepdims=True)
            o = jnp.dot(p.astype(BF16), vv[:, sl], preferred_element_type=F32) / den
            o_ref[:, sl] = o
            l_ref[:, sl] = jnp.broadcast_to(m + jnp.log(den), (ATTN_BLOCK, HEAD_DIM))

    def spec(off, prev):
        if prev:
            return pl.BlockSpec((ATTN_BLOCK, W), lambda r, n: (jnp.maximum(n - 1, 0), r * 3 + off))
        return pl.BlockSpec((ATTN_BLOCK, W), lambda r, n: (n, r * 3 + off))

    view = lambda t: t.reshape(seg, d * ATTN_WIDTH)
    o_spec = pl.BlockSpec((ATTN_BLOCK, W), lambda r, n: (n, r))
    o, l = pl.pallas_call(
        body, grid=(d, nb),
        in_specs=[spec(g, False), spec(g, True), spec(g, False), spec(g, True), spec(g, False),
                  pl.BlockSpec(bias.shape, lambda r, n: (0, 0, 0))],
        out_specs=[o_spec, o_spec], out_shape=[S((seg, d * W), F32)] * 2,
        compiler_params=_cparams(("parallel", "arbitrary")), name=name)(view(q), view(k), view(k), view(v), view(v), bias)
    return o.reshape(T, W), l.reshape(T, W)


def _attn_bwd(q, k, v, bias, lse, do, dd, dq_all, dk_all, dv_all, g, d, name):
    T = q.shape[0]
    seg = T // d
    nb = seg // ATTN_BLOCK
    W = ATTN_OUT_WIDTH
    scale = HEAD_DIM ** -0.5

    def body(q_ref, kp_ref, kc_ref, vp_ref, vc_ref, b_ref, l_ref, do_ref, dd_ref, _dq_in, _dk_in, _dv_in,
             dq_ref, dk_ref, dv_ref, db_ref, ck_ref, cv_ref):
        r, n = pl.program_id(0), pl.program_id(1)

        @pl.when(jnp.logical_and(r == 0, n == 0))
        def _():
            db_ref[...] = jnp.zeros_like(db_ref)

        @pl.when(n == 0)
        def _():
            ck_ref[...] = jnp.zeros_like(ck_ref)
            cv_ref[...] = jnp.zeros_like(cv_ref)

        @pl.when(n < nb)
        def _():
            qv = q_ref[...]
            kk = jnp.concatenate([kp_ref[...], kc_ref[...]], axis=0)
            vv = jnp.concatenate([vp_ref[...], vc_ref[...]], axis=0)
            lse_v, do_v, dd_v = l_ref[...], do_ref[...], dd_ref[...]
            col = lax.broadcasted_iota(jnp.int32, (ATTN_BLOCK, 2 * ATTN_BLOCK), 1)
            kill = jnp.logical_and(n == 0, col < ATTN_BLOCK)
            dqs, dks, dvs = [], [], []
            for h in range(HEADS_PER_GROUP):
                sl = slice(h * HEAD_DIM, (h + 1) * HEAD_DIM)
                c0 = slice(h * HEAD_DIM, h * HEAD_DIM + 1)
                qh, kh, vh = qv[:, sl], kk[:, sl], vv[:, sl]
                s = lax.dot_general(qh, kh, NT, preferred_element_type=F32) * scale + b_ref[h]
                s = jnp.where(kill, NEG, s)
                p = jnp.exp(s - lse_v[:, c0])
                doh = do_v[:, sl].astype(BF16)
                dvs.append(lax.dot_general(p.astype(BF16), doh, TN, preferred_element_type=F32))
                dp = lax.dot_general(doh, vh, NT, preferred_element_type=F32)
                ds = p * (dp - dd_v[:, c0])
                db_ref[h] += ds
                dsb = (ds * scale).astype(BF16)
                dqs.append(jnp.dot(dsb, kh, preferred_element_type=F32))
                dks.append(lax.dot_general(dsb, qh, TN, preferred_element_type=F32))
            dq_ref[...] = jnp.concatenate(dqs, axis=1).astype(dq_ref.dtype)
            dk = jnp.concatenate(dks, axis=1)
            dv = jnp.concatenate(dvs, axis=1)
            dk_ref[...] = (ck_ref[...] + dk[:ATTN_BLOCK]).astype(dk_ref.dtype)
            dv_ref[...] = (cv_ref[...] + dv[:ATTN_BLOCK]).astype(dv_ref.dtype)
            ck_ref[...] = dk[ATTN_BLOCK:]
            cv_ref[...] = dv[ATTN_BLOCK:]

        @pl.when(n == nb)
        def _():
            dk_ref[...] = ck_ref[...].astype(dk_ref.dtype)
            dv_ref[...] = cv_ref[...].astype(dv_ref.dtype)

    def cur(n):
        return jnp.minimum(n, nb - 1)

    def prev(n):
        return jnp.clip(n - 1, 0, nb - 1)

    blk = (ATTN_BLOCK, W)
    qs = pl.BlockSpec(blk, lambda r, n: (cur(n), r * 3 + g))
    ps = pl.BlockSpec(blk, lambda r, n: (prev(n), r * 3 + g))
    ls = pl.BlockSpec(blk, lambda r, n: (cur(n), r))
    anys = pl.BlockSpec(memory_space=pl.ANY)
    view = lambda t: t.reshape(seg, d * ATTN_WIDTH)
    view1 = lambda t: t.reshape(seg, d * W)
    dq_all, dk_all, dv_all, dbias = pl.pallas_call(
        body, grid=(d, nb + 1),
        in_specs=[qs, ps, qs, ps, qs, pl.BlockSpec(bias.shape, lambda r, n: (0, 0, 0)), ls, ls, ls, anys, anys, anys],
        out_specs=[qs, ps, ps, pl.BlockSpec(bias.shape, lambda r, n: (0, 0, 0))],
        out_shape=[S((seg, d * ATTN_WIDTH), BF16)] * 3 + [S(bias.shape, F32)],
        scratch_shapes=[pltpu.VMEM(blk, F32), pltpu.VMEM(blk, F32)],
        input_output_aliases={9: 0, 10: 1, 11: 2},
        compiler_params=_cparams(("arbitrary", "arbitrary")), name=name,
    )(view(q), view(k), view(k), view(v), view(v), bias, view1(lse), view1(do), view1(dd), view(dq_all), view(dk_all), view(dv_all))
    return dq_all.reshape(T, ATTN_WIDTH), dk_all.reshape(T, ATTN_WIDTH), dv_all.reshape(T, ATTN_WIDTH), dbias


CONV_TM, CONV_TC = 512, 1024


def _shift_down(x, halo8, s, row8):
    xr = pltpu.roll(x, s, 0)
    first = jnp.where(row8 < s, pltpu.roll(halo8, s, 0), xr[:8])
    return jnp.concatenate([first, xr[8:]], axis=0)


def _shift_up(x, halo8, s, row8):
    n = x.shape[0]
    xr = pltpu.roll(x, n - s, 0)
    last = jnp.where(row8 >= 8 - s, pltpu.roll(halo8, 8 - s, 0), xr[n - 8:])
    return jnp.concatenate([xr[:n - 8], last], axis=0)


def _conv_fwd(x, w, b, name):
    T, C = x.shape
    tm, tc = min(CONV_TM, T), CONV_TC

    def body(x_ref, p_ref, w_ref, b_ref, u_ref, a_ref):
        ti = pl.program_id(1)
        xv = x_ref[...]
        p8 = jnp.where(ti == 0, 0.0, p_ref[...])
        wv = w_ref[...]
        row8 = lax.broadcasted_iota(jnp.int32, (8, tc), 0)
        u = xv * wv[3:4] + b_ref[...]
        for s in (1, 2, 3):
            u = u + _shift_down(xv, p8, s, row8) * wv[3 - s:4 - s]
        u_ref[...] = u
        a_ref[...] = _silu(u)

    cur = pl.BlockSpec((tm, tc), lambda cj, ti: (ti, cj))
    halo = pl.BlockSpec((8, tc), lambda cj, ti: (jnp.maximum(ti * (tm // 8) - 1, 0), cj))
    return pl.pallas_call(
        body, grid=(C // tc, T // tm),
        in_specs=[cur, halo, pl.BlockSpec((CONV_WIDTH, tc), lambda cj, ti: (0, cj)), pl.BlockSpec((1, tc), lambda cj, ti: (0, cj))],
        out_specs=[cur, cur], out_shape=[S((T, C), F32)] * 2,
        compiler_params=_cparams(("parallel", "arbitrary")), name=name)(x, x, w, b)


def _conv_bwd(dact, u, x, w, name):
    T, C = x.shape
    tm, tc = min(CONV_TM, T), CONV_TC
    nt = T // tm

    def body(d_ref, dn_ref, u_ref, un_ref, x_ref, xp_ref, w_ref, dx_ref, dw_ref, db_ref):
        ti = pl.program_id(1)

        @pl.when(ti == 0)
        def _():
            dw_ref[...] = jnp.zeros_like(dw_ref)
            db_ref[...] = jnp.zeros_like(db_ref)

        du = d_ref[...] * _dsilu(u_ref[...])
        dun = jnp.where(ti == nt - 1, 0.0, dn_ref[...] * _dsilu(un_ref[...]))
        xv = x_ref[...]
        xp = jnp.where(ti == 0, 0.0, xp_ref[...])
        wv = w_ref[...]
        row8 = lax.broadcasted_iota(jnp.int32, (8, tc), 0)
        dx = du * wv[3:4]
        dws = [None] * CONV_WIDTH
        dws[3] = jnp.sum(du * xv, axis=0, keepdims=True)
        for s in (1, 2, 3):
            dx = dx + _shift_up(du, dun, s, row8) * wv[3 - s:4 - s]
            dws[3 - s] = jnp.sum(du * _shift_down(xv, xp, s, row8), axis=0, keepdims=True)
        dx_ref[...] = dx.astype(dx_ref.dtype)
        dw_ref[...] += jnp.concatenate(dws, axis=0)
        db_ref[...] += jnp.sum(du, axis=0, keepdims=True)

    cur = pl.BlockSpec((tm, tc), lambda cj, ti: (ti, cj))
    prev = pl.BlockSpec((8, tc), lambda cj, ti: (jnp.maximum(ti * (tm // 8) - 1, 0), cj))
    nxt = pl.BlockSpec((8, tc), lambda cj, ti: (jnp.minimum((ti + 1) * (tm // 8), T // 8 - 1), cj))
    return pl.pallas_call(
        body, grid=(C // tc, nt),
        in_specs=[cur, nxt, cur, nxt, cur, prev, pl.BlockSpec((CONV_WIDTH, tc), lambda cj, ti: (0, cj))],
        out_specs=[cur, pl.BlockSpec((CONV_WIDTH, tc), lambda cj, ti: (0, cj)), pl.BlockSpec((1, tc), lambda cj, ti: (0, cj))],
        out_shape=[S((T, C), BF16), S((CONV_WIDTH, C), F32), S((1, C), F32)],
        compiler_params=_cparams(("parallel", "arbitrary")), name=name)(dact, dact, u, u, x, x, w)


def _ssd_consts():
    i = np.arange(SSD_CHUNK)
    tril = (i[None, :] <= i[:, None]).astype(np.float32)
    trils = (i[None, :] < i[:, None]).astype(np.float32)
    head = np.repeat(np.arange(N_SSM_HEADS), D_INNER // N_SSM_HEADS)
    et = (head[None, :] == np.arange(N_SSM_HEADS)[:, None]).astype(np.float32)
    c = lambda a: jnp.asarray(a, BF16)
    return dict(tril=c(tril), triu=c(tril.T), trils=c(trils), trius=c(trils.T), et=c(et), e=c(et.T))


def _ssd_common(act_ref, dt_ref, dtT_ref, al_ref, alT_ref, tril_ref, triu_ref, et_ref):
    a_row = -jnp.exp(al_ref[...])
    a_col = -jnp.exp(alT_ref[...])
    dt, dtT = dt_ref[...], dtT_ref[...]
    la = _dotx_l(tril_ref[...], dt * a_row)
    laT = _dotx_r(dtT * a_col, triu_ref[...])
    et = et_ref[...]
    la_e = _dotx_r(la, et)
    dt_e = _dotx_r(dt, et)
    x = act_ref[:, :D_INNER]
    xdt = x * dt_e
    la_q = la_e[SSD_CHUNK - 1:SSD_CHUNK, :]
    return a_row, a_col, dt, dtT, la, laT, la_e, dt_e, x, xdt, la_q


def _decay(la, laT, h, causal):
    seg = la[:, h:h + 1] - laT[h:h + 1, :]
    return jnp.exp(jnp.where(causal, seg, NEG))


def _ssd_fwd(act, dt, dtT, alog, dskip_e, cs, name):
    T = act.shape[0]
    nc = T // SSD_CHUNK
    Q, G, GW = SSD_CHUNK, N_SSM_GROUPS, D_INNER // N_SSM_GROUPS

    def body(act_ref, dt_ref, dtT_ref, al_ref, alT_ref, dsk_ref, tril_ref, triu_ref, et_ref, y_ref, st_ref, scr):
        @pl.when(pl.program_id(0) == 0)
        def _():
            scr[...] = jnp.zeros_like(scr)
        st_ref[0] = scr[...]
        a_row, a_col, dtv, dtTv, la, laT, la_e, dt_e, x, xdt, la_q = _ssd_common(
            act_ref, dt_ref, dtT_ref, al_ref, alT_ref, tril_ref, triu_ref, et_ref)
        ela = jnp.exp(la_e)
        xdt_b = xdt.astype(BF16)
        xdte_b = (xdt * jnp.exp(la_q - la_e)).astype(BF16)
        ela_q = jnp.exp(la_q)
        causal = lax.broadcasted_iota(jnp.int32, (Q, Q), 0) >= lax.broadcasted_iota(jnp.int32, (Q, Q), 1)
        for g in range(G):
            gs = slice(g * GW, (g + 1) * GW)
            Bg = act_ref[:, D_INNER + g * D_STATE:D_INNER + (g + 1) * D_STATE].astype(BF16)
            Cg = act_ref[:, D_INNER + G * D_STATE + g * D_STATE:D_INNER + G * D_STATE + (g + 1) * D_STATE].astype(BF16)
            cb = lax.dot_general(Cg, Bg, NT, preferred_element_type=F32)
            st = scr[g]
            y_inter = jnp.dot(Cg, st.astype(BF16), preferred_element_type=F32) * ela[:, gs]
            ys = []
            for hh in range(HEADS_PER_GROUP):
                h = g * HEADS_PER_GROUP + hh
                m = (cb * _decay(la, laT, h, causal)).astype(BF16)
                ys.append(jnp.dot(m, xdt_b[:, h * HEAD_DIM:(h + 1) * HEAD_DIM], preferred_element_type=F32))
            y_ref[:, gs] = jnp.concatenate(ys, axis=1) + y_inter + x[:, gs] * dsk_ref[:, gs]
            scr[g] = st * ela_q[:, gs] + lax.dot_general(Bg, xdte_b[:, gs], TN, preferred_element_type=F32)

    full = lambda a: pl.BlockSpec(a.shape, lambda c: (0,) * a.ndim)
    al, alT = alog.reshape(1, -1), alog.reshape(-1, 1)
    return pl.pallas_call(
        body, grid=(nc,),
        in_specs=[pl.BlockSpec((Q, XBC_WIDTH), lambda c: (c, 0)), pl.BlockSpec((Q, N_SSM_HEADS), lambda c: (c, 0)),
                  pl.BlockSpec((N_SSM_HEADS, Q), lambda c: (0, c)), full(al), full(alT), full(dskip_e),
                  full(cs["tril"]), full(cs["triu"]), full(cs["et"])],
        out_specs=[pl.BlockSpec((Q, D_INNER), lambda c: (c, 0)), pl.BlockSpec((1, G, D_STATE, GW), lambda c: (c, 0, 0, 0))],
        out_shape=[S((T, D_INNER), F32), S((nc, G, D_STATE, GW), F32)],
        scratch_shapes=[pltpu.VMEM((G, D_STATE, GW), F32)],
        compiler_params=_cparams(("arbitrary",)), name=name)(act, dt, dtT, al, alT, dskip_e, cs["tril"], cs["triu"], cs["et"])


def _ssd_bwd(dy, act, dt, dtT, alog, dskip_e, states, cs, name):
    T = act.shape[0]
    nc = T // SSD_CHUNK
    Q, G, GW, H = SSD_CHUNK, N_SSM_GROUPS, D_INNER // N_SSM_GROUPS, N_SSM_HEADS

    def body(dy_ref, act_ref, dt_ref, dtT_ref, al_ref, alT_ref, dsk_ref, stp_ref, tril_ref, triu_ref, trils_ref, trius_ref,
             et_ref, e_ref, dact_ref, ddt_ref, ddtT_ref, da_ref, daT_ref, dsk_out_ref, dst, wbuf, ubuf, vbuf, sbuf):
        @pl.when(pl.program_id(0) == 0)
        def _():
            dst[...] = jnp.zeros_like(dst)
            da_ref[...] = jnp.zeros_like(da_ref)
            daT_ref[...] = jnp.zeros_like(daT_ref)
            dsk_out_ref[...] = jnp.zeros_like(dsk_out_ref)
        a_row, a_col, dtv, dtTv, la, laT, la_e, dt_e, x, xdt, la_q = _ssd_common(
            act_ref, dt_ref, dtT_ref, al_ref, alT_ref, tril_ref, triu_ref, et_ref)
        dyv = dy_ref[...]
        ela = jnp.exp(la_e)
        e_end = jnp.exp(la_q - la_e)
        ela_q = jnp.exp(la_q)
        dye_b = (dyv * ela).astype(BF16)
        dy_b = dyv.astype(BF16)
        xdt_b = xdt.astype(BF16)
        xdte_b = (xdt * e_end).astype(BF16)
        ri = lax.broadcasted_iota(jnp.int32, (Q, Q), 0)
        ci = lax.broadcasted_iota(jnp.int32, (Q, Q), 1)
        causal = ri >= ci
        trius = trius_ref[...]
        rows = []
        for g in range(G):
            gs = slice(g * GW, (g + 1) * GW)
            Bg = act_ref[:, D_INNER + g * D_STATE:D_INNER + (g + 1) * D_STATE].astype(BF16)
            Cg = act_ref[:, D_INNER + G * D_STATE + g * D_STATE:D_INNER + G * D_STATE + (g + 1) * D_STATE].astype(BF16)
            cb = lax.dot_general(Cg, Bg, NT, preferred_element_type=F32)
            stp = stp_ref[0, g]
            stp_b = stp.astype(BF16)
            dstv = dst[g]
            dst_b = dstv.astype(BF16)
            y_inter = jnp.dot(Cg, stp_b, preferred_element_type=F32) * ela[:, gs]
            wbuf[:, gs] = dyv[:, gs] * y_inter
            dxdt_state = jnp.dot(Bg, dst_b, preferred_element_type=F32) * e_end[:, gs]
            ubuf[:, gs] = dxdt_state * xdt[:, gs]
            dC = lax.dot_general(dye_b[:, gs], stp_b, NT, preferred_element_type=F32)
            dB = lax.dot_general(xdte_b[:, gs], dst_b, NT, preferred_element_type=F32)
            sbuf[:, gs] = jnp.broadcast_to(jnp.sum(dstv * stp, axis=0, keepdims=True), (8, GW))
            dst[g] = dstv * ela_q[:, gs] + lax.dot_general(Cg, dye_b[:, gs], TN, preferred_element_type=F32)
            dG = jnp.zeros((Q, Q), F32)
            dxs = []
            for hh in range(HEADS_PER_GROUP):
                h = g * HEADS_PER_GROUP + hh
                hs = slice(h * HEAD_DIM, (h + 1) * HEAD_DIM)
                L = _decay(la, laT, h, causal)
                M = cb * L
                dM = lax.dot_general(dy_b[:, hs], xdt_b[:, hs], NT, preferred_element_type=F32)
                dG = dG + dM * L
                R = _dotx_r(dM * M, trius, parts=2)
                rows.append(jnp.sum(jnp.where(causal, R, 0.0), axis=0, keepdims=True))
                dxs.append(lax.dot_general(M.astype(BF16), dy_b[:, hs], TN, preferred_element_type=F32))
            dG_b = dG.astype(BF16)
            dC = dC + jnp.dot(dG_b, Bg, preferred_element_type=F32)
            dB = dB + lax.dot_general(dG_b, Cg, TN, preferred_element_type=F32)
            dxdt = jnp.concatenate(dxs, axis=1) + dxdt_state
            vbuf[:, gs] = dxdt * x[:, gs]
            dact_ref[:, gs] = dxdt * dt_e[:, gs] + dyv[:, gs] * dsk_ref[:, gs]
            dact_ref[:, D_INNER + g * D_STATE:D_INNER + (g + 1) * D_STATE] = dB
            dact_ref[:, D_INNER + G * D_STATE + g * D_STATE:D_INNER + G * D_STATE + (g + 1) * D_STATE] = dC
        e = e_ref[...]
        w = _dotx_r(wbuf[...], e, parts=2)
        u = _dotx_r(ubuf[...], e, parts=2)
        vx = _dotx_r(vbuf[...], e, parts=2)
        dsk = _dotx_r(dyv * x, e, parts=2)
        s0 = _dotx_r(sbuf[...], e, parts=2)[0:1] * jnp.exp(la[Q - 1:Q, :])
        ddelta = _dotx_l(triu_ref[...], w) + _dotx_l(trils_ref[...], u) + s0
        ddt_ref[...] = ddelta * a_row + vx
        ddeltaT = jnp.concatenate(rows, axis=0)
        ddtT_ref[...] = ddeltaT * a_col
        da_ref[...] += jnp.sum(ddelta * dtv, axis=0, keepdims=True)
        daT_ref[...] += jnp.sum(ddeltaT * dtTv, axis=1, keepdims=True)
        dsk_out_ref[...] += jnp.sum(dsk, axis=0, keepdims=True)

    rev = lambda c: nc - 1 - c
    full = lambda a: pl.BlockSpec(a.shape, lambda c: (0,) * a.ndim)
    al, alT = alog.reshape(1, -1), alog.reshape(-1, 1)
    consts = [cs[k] for k in ("tril", "triu", "trils", "trius", "et", "e")]
    return pl.pallas_call(
        body, grid=(nc,),
        in_specs=[pl.BlockSpec((Q, D_INNER), lambda c: (rev(c), 0)), pl.BlockSpec((Q, XBC_WIDTH), lambda c: (rev(c), 0)),
                  pl.BlockSpec((Q, H), lambda c: (rev(c), 0)), pl.BlockSpec((H, Q), lambda c: (0, rev(c))),
                  full(al), full(alT), full(dskip_e), pl.BlockSpec((1, G, D_STATE, GW), lambda c: (rev(c), 0, 0, 0))]
                 + [full(a) for a in consts],
        out_specs=[pl.BlockSpec((Q, XBC_WIDTH), lambda c: (rev(c), 0)), pl.BlockSpec((Q, H), lambda c: (rev(c), 0)),
                   pl.BlockSpec((H, Q), lambda c: (0, rev(c))), pl.BlockSpec((1, H), lambda c: (0, 0)),
                   pl.BlockSpec((H, 1), lambda c: (0, 0)), pl.BlockSpec((1, H), lambda c: (0, 0))],
        out_shape=[S((T, XBC_WIDTH), F32), S((T, H), F32), S((H, T), F32), S((1, H), F32), S((H, 1), F32), S((1, H), F32)],
        scratch_shapes=[pltpu.VMEM((G, D_STATE, GW), F32), pltpu.VMEM((Q, D_INNER), F32), pltpu.VMEM((Q, D_INNER), F32),
                        pltpu.VMEM((Q, D_INNER), F32), pltpu.VMEM((8, D_INNER), F32)],
        compiler_params=_cparams(("arbitrary",)), name=name)(dy, act, dt, dtT, al, alT, dskip_e, states, *consts)


def _a_log_grad(da, daT_row, alog, name):
    def body(a_ref, b_ref, al_ref, o_ref):
        o_ref[...] = (a_ref[...] + b_ref[...]) * (-jnp.exp(al_ref[...]))
    return pl.pallas_call(body, out_shape=S((1, N_SSM_HEADS), F32), name=name)(da, daT_row, alog.reshape(1, -1))


def _place():
    x, y, c = lax.axis_index("x"), lax.axis_index("y"), lax.axis_index("c")
    return x, y, c, [(1 - x, y), (x, 1 - y), (1 - x, 1 - y)]


def _all_gather(xb, name):
    R, C = xb.shape
    rc = R // AG_CHUNKS

    def body(x_ref, out_ref, send_sems, recv_sems, local_sem):
        x, y, c, chips = _place()
        me, sib = (x, y, c), (x, y, 1 - c)

        def rows(dev, ch):
            return out_ref.at[4 * dev[0] + 2 * dev[1] + dev[2], pl.ds(ch * rc, rc), :]

        def copy(k, ch, block, to, src=None):
            return pltpu.make_async_remote_copy(
                src_ref=rows(block, ch) if src is None else src, dst_ref=rows(block, ch),
                send_sem=send_sems.at[k, ch], recv_sem=recv_sems.at[k, ch], device_id=to, device_id_type=MESH)

        mine = pltpu.make_async_copy(x_ref, out_ref.at[4 * x + 2 * y + c], local_sem)
        mine.start()
        first = []
        for ch in range(AG_CHUNKS):
            src = x_ref.at[pl.ds(ch * rc, rc), :]
            first.append(copy(0, ch, me, sib, src=src))
            first += [copy(1 + j, ch, me, (*chip, c), src=src) for j, chip in enumerate(chips)]
        for cp in first:
            cp.start()
        passed = []
        for ch in range(AG_CHUNKS):
            for j, chip in enumerate(chips):
                copy(1 + j, ch, (*chip, c), me).wait_recv()
                cp = copy(4 + j, ch, (*chip, c), sib)
                cp.start()
                passed.append(cp)
        for ch in range(AG_CHUNKS):
            copy(0, ch, sib, me).wait_recv()
            for j, chip in enumerate(chips):
                copy(4 + j, ch, (*chip, 1 - c), me).wait_recv()
        for cp in first + passed:
            cp.wait_send()
        mine.wait()

    anys = pl.BlockSpec(memory_space=pl.ANY)
    return pl.pallas_call(
        body, in_specs=[anys], out_specs=anys, out_shape=S((N_DEV, R, C), xb.dtype),
        scratch_shapes=[pltpu.SemaphoreType.DMA((7, AG_CHUNKS)), pltpu.SemaphoreType.DMA((7, AG_CHUNKS)), pltpu.SemaphoreType.DMA],
        name=name)(xb)


def _to_sibling(send, name):
    _, R, C = send.shape

    def body(s_ref, out_ref, send_sems, recv_sems):
        x, y, c, _ = _place()
        cps = [pltpu.make_async_remote_copy(
            src_ref=s_ref.at[2 * k + (1 - c)], dst_ref=out_ref.at[k], send_sem=send_sems.at[k], recv_sem=recv_sems.at[k],
            device_id=(x, y, 1 - c), device_id_type=MESH) for k in range(4)]
        for cp in cps:
            cp.start()
        for cp in cps:
            cp.wait()

    anys = pl.BlockSpec(memory_space=pl.ANY)
    return pl.pallas_call(
        body, in_specs=[anys], out_specs=anys, out_shape=S((4, R, C), send.dtype),
        scratch_shapes=[pltpu.SemaphoreType.DMA((4,)), pltpu.SemaphoreType.DMA((4,))],
        name=name)(send)


def _to_chips(part, name):
    _, R, C = part.shape

    def body(p_ref, out_ref, send_sems, recv_sems):
        x, y, c, chips = _place()
        cps = [pltpu.make_async_remote_copy(
            src_ref=p_ref.at[2 * chip[0] + chip[1]], dst_ref=out_ref.at[j], send_sem=send_sems.at[j], recv_sem=recv_sems.at[j],
            device_id=(*chip, c), device_id_type=MESH) for j, chip in enumerate(chips)]
        for cp in cps:
            cp.start()
        for cp in cps:
            cp.wait()

    anys = pl.BlockSpec(memory_space=pl.ANY)
    return pl.pallas_call(
        body, in_specs=[anys], out_specs=anys, out_shape=S((3, R, C), part.dtype),
        scratch_shapes=[pltpu.SemaphoreType.DMA((3,)), pltpu.SemaphoreType.DMA((3,))],
        name=name)(part)


def _reduce_scatter(send, name):
    _, R, C = send.shape
    c = lax.axis_index("c")
    x, y = lax.axis_index("x"), lax.axis_index("y")
    from_sib = _to_sibling(send, name + "_d2d")
    same_core = lax.dynamic_index_in_dim(send.reshape(4, 2, R, C), c, axis=1, keepdims=False)
    part = _add_rows(same_core.reshape(4 * R, C), from_sib.reshape(4 * R, C), name + "_add1").reshape(4, R, C)
    got = _to_chips(part, name + "_ici")
    own = lax.dynamic_index_in_dim(part, 2 * x + y, axis=0, keepdims=False)
    return _add4_rows(own, got[0], got[1], got[2], name + "_add2")


def _all_reduce_small(v, name):
    R, C = v.shape

    def body(x_ref, out_ref, buf, send_sems, recv_sems):
        x, y, c, chips = _place()
        me, sib = (x, y, c), (x, y, 1 - c)

        def rows(dev):
            return buf.at[4 * dev[0] + 2 * dev[1] + dev[2]]

        def copy(k, block, to, src=None):
            return pltpu.make_async_remote_copy(
                src_ref=rows(block) if src is None else src, dst_ref=rows(block),
                send_sem=send_sems.at[k], recv_sem=recv_sems.at[k], device_id=to, device_id_type=MESH)

        buf[4 * x + 2 * y + c] = x_ref[...]
        first = [copy(0, me, sib, src=x_ref)] + [copy(1 + j, me, (*chip, c), src=x_ref) for j, chip in enumerate(chips)]
        for cp in first:
            cp.start()
        passed = [copy(4 + j, (*chip, c), sib) for j, chip in enumerate(chips)]
        for j, chip in enumerate(chips):
            copy(1 + j, (*chip, c), me).wait_recv()
            passed[j].start()
        copy(0, sib, me).wait_recv()
        for j, chip in enumerate(chips):
            copy(4 + j, (*chip, 1 - c), me).wait_recv()
        for cp in first + passed:
            cp.wait_send()
        acc = buf[0]
        for j in range(1, N_DEV):
            acc = acc + buf[j]
        out_ref[...] = acc

    vm = pl.BlockSpec(memory_space=pltpu.VMEM)
    return pl.pallas_call(
        body, in_specs=[vm], out_specs=vm, out_shape=S((R, C), F32),
        scratch_shapes=[pltpu.VMEM((N_DEV, R, C), F32), pltpu.SemaphoreType.DMA((7,)), pltpu.SemaphoreType.DMA((7,))],
        compiler_params=pltpu.CompilerParams(vmem_limit_bytes=VMEM_LIMIT), name=name)(v)


SEG = (("q", 0, 1536), ("k", 1536, 1536), ("v", 3072, 1536), ("z", 4608, 2048), ("xbc", 6656, 3072), ("dt", 9728, 32), ("gl", 9760, 2048))


def _split_w_in(w_in_full):
    out = {}
    for nm, off, n in SEG:
        w = w_in_full[:, off:off + n]
        if nm == "dt":
            w = jnp.pad(w, ((0, 0), (0, LANES - n)))
        out[nm] = w
    return out


def _layer_fwd(h, p, W, biases, cs, l):
    T = h.shape[0]
    nm = lambda s: f"{s}_l{l}"
    sv = {"h_in": h}
    xn = _rmsnorm_fwd(h, p["norm1_w"], nm("norm1"))
    wi = W["w_in"]
    q = _mm(xn, wi["q"], out_dtype=BF16, name=nm("proj_q"))
    k = _mm(xn, wi["k"], out_dtype=BF16, name=nm("proj_k"))
    v = _mm(xn, wi["v"], out_dtype=BF16, name=nm("proj_v"))
    z = _mm(xn, wi["z"], name=nm("proj_z"))
    xbc = _mm(xn, wi["xbc"], name=nm("proj_xbc"))
    dt_raw = _mm(xn, wi["dt"], name=nm("proj_dt"))
    gl = _mm(xn, wi["gl"], name=nm("proj_gl"))
    os_, ls = [], []
    for g, (window, dil) in enumerate(DILATED_GROUPS):
        o, lse = _attn_fwd(q, k, v, biases[g], g, dil, nm(f"attn_fwd_g{g}"))
        os_.append(o)
        ls.append(lse)
    attn_b, attn_f = _combine_fwd(os_, ls, nm("combine"))
    u_conv, act = _conv_fwd(xbc, p["conv_w"], p["conv_b"].reshape(1, -1), nm("conv"))
    dt = _dt_fwd(dt_raw, p["dt_bias"], nm("dt"))
    dtT = dt.T
    dskip_e = jnp.repeat(p["d_skip"], D_INNER // N_SSM_HEADS).reshape(1, -1)
    y, states = _ssd_fwd(act, dt, dtT, p["a_log"], dskip_e, cs, nm("ssd_fwd"))
    ssm = _ssm_norm_fwd(y, z, p["ssm_norm_w"], nm("ssm_norm"))
    a_br = _mm(attn_b, W["w_attn_branch"], name=nm("attn_branch"))
    s_br = _mm(ssm, W["w_ssm_branch"], name=nm("ssm_branch"))
    merged = _gate_fwd(a_br, s_br, gl, nm("gate"))
    h_mid = _mm(merged, W["w_out"], acc=h, name=nm("out_proj"))
    xn2 = _rmsnorm_fwd(h_mid, p["norm2_w"], nm("norm2"))
    u_ffn = _mm(xn2, W["w_ffn_in"], name=nm("ffn_in"))
    ffn_act = _swiglu_fwd(u_ffn, nm("swiglu"))
    h_out = _mm(ffn_act, W["w_ffn_out"], acc=h_mid, name=nm("ffn_out"))
    sv.update(xn=xn, q=q, k=k, v=v, z=z, xbc=xbc, dt_raw=dt_raw, gl=gl, ls=ls, attn_b=attn_b, attn_f=attn_f, u_conv=u_conv,
              act=act, dt=dt, dtT=dtT, dskip_e=dskip_e, y=y, states=states, ssm=ssm, a_br=a_br, s_br=s_br, merged=merged,
              h_mid=h_mid, xn2=xn2, u_ffn=u_ffn, ffn_act=ffn_act)
    return h_out, sv


def _layer_bwd(dh, sv, p, W, biases, cs, head_ones, l):
    T = dh.shape[0]
    nm = lambda s: f"{s}_l{l}"
    gr = {}
    dact = _mm(dh, W["w_ffn_out"], tb=True, name=nm("d_ffn_act"))
    gr["w_ffn_out"] = _mm(sv["ffn_act"], dh, ta=True, name=nm("g_ffn_out"))
    du = _swiglu_bwd(dact, sv["u_ffn"], nm("d_swiglu"))
    dxn2 = _mm(du, W["w_ffn_in"], tb=True, name=nm("d_xn2"))
    gr["w_ffn_in"] = _mm(sv["xn2"], du, ta=True, name=nm("g_ffn_in"))
    dh_mid, gr["norm2_w"] = _rmsnorm_bwd(dxn2, sv["h_mid"], p["norm2_w"], dh, nm("d_norm2"))
    dmerged = _mm(dh_mid, W["w_out"], tb=True, name=nm("d_merged"))
    gr["w_out"] = _mm(sv["merged"], dh_mid, ta=True, name=nm("g_out"))
    d_a, d_s, dgl = _gate_bwd(dmerged, sv["a_br"], sv["s_br"], sv["gl"], nm("d_gate"))
    dattn = _mm(d_a, W["w_attn_branch"], tb=True, name=nm("d_attn"))
    gr["w_attn_branch"] = _mm(sv["attn_b"], d_a, ta=True, name=nm("g_attn_branch"))
    dssm = _mm(d_s, W["w_ssm_branch"], tb=True, name=nm("d_ssm"))
    gr["w_ssm_branch"] = _mm(sv["ssm"], d_s, ta=True, name=nm("g_ssm_branch"))
    dy, dz, gr["ssm_norm_w"] = _ssm_norm_bwd(dssm, sv["y"], sv["z"], p["ssm_norm_w"], nm("d_ssm_norm"))
    dact_c, ddt_a, ddt_bT, da, daT, dskip = _ssd_bwd(dy, sv["act"], sv["dt"], sv["dtT"], p["a_log"], sv["dskip_e"], sv["states"], cs, nm("ssd_bwd"))
    gr["a_log"] = _a_log_grad(da, daT.T, p["a_log"], nm("g_a_log")).reshape(-1)
    gr["d_skip"] = dskip.reshape(-1)
    ddt_raw, ddt_bias = _dt_bwd(ddt_a, ddt_bT.T, sv["dt_raw"], p["dt_bias"], nm("d_dt"))
    gr["dt_bias"] = ddt_bias.reshape(-1)
    dxbc, gr["conv_w"], dconv_b = _conv_bwd(dact_c, sv["u_conv"], sv["xbc"], p["conv_w"], nm("d_conv"))
    gr["conv_b"] = dconv_b.reshape(-1)
    outs = _combine_bwd(dattn, sv["attn_f"], sv["ls"], head_ones, nm("d_combine"))
    dq = jnp.zeros((T, ATTN_WIDTH), BF16)
    dk, dv = dq, dq
    dbias = []
    for g, (window, dil) in enumerate(DILATED_GROUPS):
        dq, dk, dv, db = _attn_bwd(sv["q"], sv["k"], sv["v"], biases[g], sv["ls"][g], outs[2 * g], outs[2 * g + 1],
                                   dq, dk, dv, g, dil, nm(f"attn_bwd_g{g}"))
        dbias.append(db)
    wi = W["w_in"]
    parts = (("q", dq), ("k", dk), ("v", dv), ("z", dz), ("xbc", dxbc), ("dt", ddt_raw), ("gl", dgl))
    dxn, gws = None, []
    for sname, dseg in parts:
        dxn = _mm(dseg, wi[sname], tb=True, acc=dxn, name=nm("d_xn_" + sname))
        gw = _mm(sv["xn"], dseg, ta=True, name=nm("g_in_" + sname))
        gws.append(gw[:, :N_SSM_HEADS] if sname == "dt" else gw)
    gr["w_in"] = jnp.concatenate(gws, axis=1)
    dh_in, gr["norm1_w"] = _rmsnorm_bwd(dxn, sv["h_in"], p["norm1_w"], dh_mid, nm("d_norm1"))
    return dh_in, gr, dbias


def _step_local(x, tgt, small, Wfull, rel_bias, final_norm_w):
    cs = _ssd_consts()
    head = np.repeat(np.arange(HEADS_PER_GROUP), HEAD_DIM)
    head_ones = jnp.asarray(head[:, None] == head[None, :], BF16)
    biases, onehots = [], []
    for g, (window, dil) in enumerate(DILATED_GROUPS):
        onehot, valid = _bias_consts(dil, window // dil)
        rel_g_t = rel_bias[:, g * HEADS_PER_GROUP:(g + 1) * HEADS_PER_GROUP].T
        b = _bias_gather(rel_g_t, onehot, valid, f"bias_gather_g{g}")
        biases.append(b.reshape(HEADS_PER_GROUP, ATTN_BLOCK, 2 * ATTN_BLOCK))
        onehots.append(onehot)
    h, saved = x, []
    for l in range(DEPTH):
        W = dict(Wfull[l])
        W["w_in"] = _split_w_in(W["w_in"])
        Wfull[l] = W
        h, sv = _layer_fwd(h, small[l], W, biases, cs, l)
        saved.append(sv)
    dh, g_final, loss = _loss_head(h, final_norm_w, tgt, "loss_head")
    grads = [None] * DEPTH
    dbias_tot = [None] * N_DIL
    for l in reversed(range(DEPTH)):
        dh, grads[l], dbias = _layer_bwd(dh, saved[l], small[l], Wfull[l], biases, cs, head_ones, l)
        for g in range(N_DIL):
            dbias_tot[g] = dbias[g] if dbias_tot[g] is None else dbias_tot[g] + dbias[g]
    d_rel = jnp.concatenate(
        [_bias_scatter(dbias_tot[g].reshape(HEADS_PER_GROUP, -1), onehots[g], f"bias_scatter_g{g}").T for g in range(N_DIL)], axis=1)
    return loss, dh, grads, d_rel, g_final


def _col_unshard(g, rows, cols):
    return g.reshape(N_DEV, rows, cols).transpose(1, 0, 2).reshape(rows, N_DEV * cols)


def _col_shard(w, rows, cols):
    return w.reshape(rows, N_DEV, cols).transpose(1, 0, 2).reshape(N_DEV, rows * cols // PACK_COLS, PACK_COLS)


UNSHARD = {
    "w_in": lambda g: _col_unshard(g, 1024, 1476), "w_attn_branch": lambda g: _col_unshard(g, 512, 128),
    "w_ssm_branch": lambda g: g.reshape(2048, 1024), "w_out": lambda g: g.reshape(1024, 1024),
    "w_ffn_in": lambda g: _col_unshard(g, 1024, 704), "w_ffn_out": lambda g: g.reshape(2816, 1024),
}
SHARD = {
    "w_in": lambda w: _col_shard(w, 1024, 1476), "w_attn_branch": lambda w: _col_shard(w, 512, 128),
    "w_ssm_branch": lambda w: w.reshape(N_DEV, 256, 1024), "w_out": lambda w: w.reshape(N_DEV, 128, 1024),
    "w_ffn_in": lambda w: _col_shard(w, 1024, 704), "w_ffn_out": lambda w: w.reshape(N_DEV, 352, 1024),
}
SMALL_LAYER = (("norm1_w", 1024), ("conv_w", 12288), ("conv_b", 3072), ("dt_bias", 32), ("a_log", 32), ("d_skip", 32),
               ("ssm_norm_w", 2048), ("norm2_w", 1024))
SMALL_GLOBAL = (("rel_bias", 768), ("final_norm_w", 1024), ("loss", 1))


def _pad128(v):
    n = v.shape[0]
    return jnp.pad(v, (0, -n % LANES))


def _pack_small(per_layer, glob):
    parts = [_pad128(per_layer[l][nm].reshape(-1)) for l in range(DEPTH) for nm, _ in SMALL_LAYER]
    parts += [_pad128(glob[nm].reshape(-1)) for nm, _ in SMALL_GLOBAL]
    flat = jnp.concatenate(parts)
    flat = jnp.pad(flat, (0, -flat.shape[0] % (8 * LANES)))
    return flat.reshape(-1, LANES)


def _unpack_small(packed):
    flat = packed.reshape(-1)
    per_layer, glob, off = [dict() for _ in range(DEPTH)], {}, 0
    for l in range(DEPTH):
        for nm, n in SMALL_LAYER:
            per_layer[l][nm] = flat[off:off + n]
            off += n + (-n % LANES)
    for nm, n in SMALL_GLOBAL:
        glob[nm] = flat[off:off + n]
        off += n + (-n % LANES)
    return per_layer, glob


def kernel(x, norm1_w, w_in, conv_w, conv_b, dt_bias, a_log, d_skip, ssm_norm_w, w_attn_branch, w_ssm_branch, w_out, norm2_w, w_ffn_in, w_ffn_out, rel_bias, final_norm_w, loss_target, m_norm1_w, m_w_in, m_conv_w, m_conv_b, m_dt_bias, m_a_log, m_d_skip, m_ssm_norm_w, m_w_attn_branch, m_w_ssm_branch, m_w_out, m_norm2_w, m_w_ffn_in, m_w_ffn_out, m_rel_bias, m_final_norm_w, v_norm1_w, v_w_in, v_conv_w, v_conv_b, v_dt_bias, v_a_log, v_d_skip, v_ssm_norm_w, v_w_attn_branch, v_w_ssm_branch, v_w_out, v_norm2_w, v_w_ffn_in, v_w_ffn_out, v_rel_bias, v_final_norm_w):
    big = dict(w_in=w_in, w_attn_branch=w_attn_branch, w_ssm_branch=w_ssm_branch, w_out=w_out, w_ffn_in=w_ffn_in, w_ffn_out=w_ffn_out)
    big_m = dict(w_in=m_w_in, w_attn_branch=m_w_attn_branch, w_ssm_branch=m_w_ssm_branch, w_out=m_w_out, w_ffn_in=m_w_ffn_in, w_ffn_out=m_w_ffn_out)
    big_v = dict(w_in=v_w_in, w_attn_branch=v_w_attn_branch, w_ssm_branch=v_w_ssm_branch, w_out=v_w_out, w_ffn_in=v_w_ffn_in, w_ffn_out=v_w_ffn_out)
    sm = dict(norm1_w=norm1_w, conv_w=conv_w, conv_b=conv_b, dt_bias=dt_bias, a_log=a_log, d_skip=d_skip, ssm_norm_w=ssm_norm_w, norm2_w=norm2_w)
    sm_m = dict(norm1_w=m_norm1_w, conv_w=m_conv_w, conv_b=m_conv_b, dt_bias=m_dt_bias, a_log=m_a_log, d_skip=m_d_skip, ssm_norm_w=m_ssm_norm_w, norm2_w=m_norm2_w)
    sm_v = dict(norm1_w=v_norm1_w, conv_w=v_conv_w, conv_b=v_conv_b, dt_bias=v_dt_bias, a_log=v_a_log, d_skip=v_d_skip, ssm_norm_w=v_ssm_norm_w, norm2_w=v_norm2_w)
    me = 4 * lax.axis_index("x") + 2 * lax.axis_index("y") + lax.axis_index("c")

    packed = jnp.concatenate([big[nm][l].reshape(-1, PACK_COLS) for l in range(DEPTH) for nm, _ in BIG], axis=0).astype(BF16)
    packed = jnp.pad(packed, ((0, PACK_ROWS_PAD - PACK_ROWS), (0, 0)))
    gathered = _all_gather(packed, "all_gather_weights")
    Wfull, off = [], 0
    for l in range(DEPTH):
        W = {}
        for nm, r in BIG:
            W[nm] = UNSHARD[nm](gathered[:, off:off + r])
            off += r
        Wfull.append(W)

    conv_full = []
    for l in range(DEPTH):
        z = jnp.zeros((N_DEV, CONV_WIDTH, XBC_WIDTH // N_DEV), F32)
        conv_full.append(lax.dynamic_update_index_in_dim(z, conv_w[l], me, axis=0))
    cw = jnp.stack(conv_full).reshape(-1, LANES)
    cw = _all_reduce_small(cw, "gather_conv_w").reshape(DEPTH, N_DEV, CONV_WIDTH, XBC_WIDTH // N_DEV)
    cw = cw.transpose(0, 2, 1, 3).reshape(DEPTH, CONV_WIDTH, XBC_WIDTH)

    small = [{nm: (cw[l] if nm == "conv_w" else a[l]) for nm, a in sm.items()} for l in range(DEPTH)]
    loss, dx, grads, d_rel, g_final = _step_local(x[0], loss_target[0], small, Wfull, rel_bias, final_norm_w)

    send = jnp.concatenate([SHARD[nm](grads[l][nm]) for l in range(DEPTH) for nm, _ in BIG], axis=1)
    send = jnp.pad(send, ((0, 0), (0, -PACK_ROWS % 8), (0, 0)))
    mine = _reduce_scatter(send, "reduce_scatter_grads")
    g_big, off = {nm: [] for nm, _ in BIG}, 0
    for l in range(DEPTH):
        for nm, r in BIG:
            g_big[nm].append(mine[off:off + r].reshape(big[nm].shape[1:]))
            off += r
    g_big = {nm: jnp.stack(v) for nm, v in g_big.items()}

    per_layer = [{nm: grads[l][nm] for nm, _ in SMALL_LAYER} for l in range(DEPTH)]
    packet = _pack_small(per_layer, dict(rel_bias=d_rel, final_norm_w=g_final, loss=loss[0, :1]))
    per_layer, glob = _unpack_small(_all_reduce_small(packet, "all_reduce_small"))
    g_small = {nm: jnp.stack([per_layer[l][nm] for l in range(DEPTH)]) for nm, _ in SMALL_LAYER}
    cwg = g_small["conv_w"].reshape(DEPTH, CONV_WIDTH, N_DEV, XBC_WIDTH // N_DEV)
    g_small["conv_w"] = lax.dynamic_index_in_dim(cwg, me, axis=2, keepdims=False)
    for nm in sm:
        g_small[nm] = g_small[nm].reshape(sm[nm].shape)
    g_rel = glob["rel_bias"].reshape(rel_bias.shape)
    g_fin = glob["final_norm_w"]
    loss_out = glob["loss"][0]

    def adam(w, g, m, v, name):
        shp = w.shape
        two = lambda a: a.reshape(-1, shp[-1]) if a.ndim > 1 else a.reshape(1, -1)
        d, nm_, nv = _adamw(two(w), two(g), two(m), two(v), name)
        return d.reshape(shp), nm_.reshape(shp), nv.reshape(shp)

    order = ["norm1_w", "w_in", "conv_w", "conv_b", "dt_bias", "a_log", "d_skip", "ssm_norm_w", "w_attn_branch", "w_ssm_branch",
             "w_out", "norm2_w", "w_ffn_in", "w_ffn_out", "rel_bias", "final_norm_w"]
    allw = {**big, **sm, "rel_bias": rel_bias, "final_norm_w": final_norm_w}
    allm = {**big_m, **sm_m, "rel_bias": m_rel_bias, "final_norm_w": m_final_norm_w}
    allv = {**big_v, **sm_v, "rel_bias": v_rel_bias, "final_norm_w": v_final_norm_w}
    allg = {**g_big, **g_small, "rel_bias": g_rel, "final_norm_w": g_fin}
    deltas, new_m, new_v = [], [], []
    for nm in order:
        d, a, b = adam(allw[nm], allg[nm], allm[nm], allv[nm], "adamw_" + nm)
        deltas.append(d)
        new_m.append(a)
        new_v.append(b)
    return (loss_out, dx[None], *[allg[nm] for nm in order], *deltas, *new_m, *new_v)
```

```python
import functools
import math

import numpy as np
import jax
import jax.numpy as jnp
from jax import lax
from jax.experimental import pallas as pl
from jax.experimental.pallas import tpu as pltpu

F32, BF16 = jnp.float32, jnp.bfloat16
S = jax.ShapeDtypeStruct
MESH = pl.DeviceIdType.MESH

D_MODEL = 1024
DEPTH = 2
HEAD_DIM = 64
DILATED_GROUPS = ((128, 1), (512, 4), (2048, 16))
N_DIL = 3
HEADS_PER_GROUP = 8
ATTN_WIDTH = 1536
ATTN_OUT_WIDTH = 512
ATTN_BLOCK = 128
N_REL_BUCKETS = 32
REL_MAX_DISTANCE = 2048
D_INNER = 2048
N_SSM_HEADS = 32
N_SSM_GROUPS = 4
D_STATE = 128
CONV_WIDTH = 4
SSD_CHUNK = 128
XBC_WIDTH = 3072
D_FF = 2816
EPS = 1e-6
ADAM_LR, ADAM_B1, ADAM_B2, ADAM_EPS, ADAM_WD, ADAM_STEP = 0.001, 0.9, 0.999, 1e-08, 0.01, 10

N_DEV = 8
LANES = 128
VMEM_LIMIT = 56 * 1024 * 1024
ROW_TILES_BYTES = 36 * 1024 * 1024
NEG = -1e30
BIG = ("w_in", "w_attn_branch", "w_ssm_branch", "w_out", "w_ffn_in", "w_ffn_out")
COL_SHARDED = ("w_in", "w_attn_branch", "w_ffn_in")

NT = (((1,), (1,)), ((), ()))
TN = (((0,), (0,)), ((), ()))


def _cparams(sem=None):
    return pltpu.CompilerParams(dimension_semantics=sem, vmem_limit_bytes=VMEM_LIMIT)


def _pick(n, target, mult=LANES):
    best = None
    for t in range(mult, min(n, target) + 1, mult):
        if n % t == 0:
            best = t
    return best or n


def _silu(x):
    return x * jax.nn.sigmoid(x)


def _dsilu(x):
    s = jax.nn.sigmoid(x)
    return s * (1.0 + x * (1.0 - s))


def _split2(x):
    hi = x.astype(BF16)
    lo = (x - hi.astype(F32)).astype(BF16)
    return hi, lo


def _split3(x):
    x1 = x.astype(BF16)
    r1 = x - x1.astype(F32)
    x2 = r1.astype(BF16)
    x3 = (r1 - x2.astype(F32)).astype(BF16)
    return x1, x2, x3


def _dotx_r(x, m, parts=3):
    xs = _split3(x) if parts == 3 else _split2(x)
    out = jnp.dot(xs[0], m, preferred_element_type=F32)
    for xi in xs[1:]:
        out = out + jnp.dot(xi, m, preferred_element_type=F32)
    return out


def _dotx_l(m, x, parts=3):
    xs = _split3(x) if parts == 3 else _split2(x)
    out = jnp.dot(m, xs[0], preferred_element_type=F32)
    for xi in xs[1:]:
        out = out + jnp.dot(m, xi, preferred_element_type=F32)
    return out


def _mm(a, b, *, ta=False, tb=False, out_dtype=F32, acc=None, name, tm=1024, tn=1536, tk=1024):
    M, K = (a.shape[1], a.shape[0]) if ta else a.shape
    N = b.shape[0] if tb else b.shape[1]
    tm, tn, tk = _pick(M, tm), _pick(N, tn), _pick(K, tk)
    nk = K // tk
    dims = (((0 if ta else 1,), (1 if tb else 0,)), ((), ()))
    has_acc = acc is not None

    def body(*refs):
        if has_acc:
            a_ref, b_ref, c_ref, o_ref, acc_ref = refs
        else:
            a_ref, b_ref, o_ref, acc_ref = refs
        k = pl.program_id(2)
        part = lax.dot_general(a_ref[...].astype(BF16), b_ref[...].astype(BF16), dims, preferred_element_type=F32)

        @pl.when(k == 0)
        def _():
            acc_ref[...] = part + c_ref[...].astype(F32) if has_acc else part

        @pl.when(k > 0)
        def _():
            acc_ref[...] += part

        @pl.when(k == nk - 1)
        def _():
            o_ref[...] = acc_ref[...].astype(o_ref.dtype)

    a_spec = pl.BlockSpec((tk, tm), lambda i, j, k: (k, i)) if ta else pl.BlockSpec((tm, tk), lambda i, j, k: (i, k))
    b_spec = pl.BlockSpec((tn, tk), lambda i, j, k: (j, k)) if tb else pl.BlockSpec((tk, tn), lambda i, j, k: (k, j))
    o_spec = pl.BlockSpec((tm, tn), lambda i, j, k: (i, j))
    in_specs, args = [a_spec, b_spec], [a, b]
    if has_acc:
        in_specs.append(o_spec)
        args.append(acc)
    return pl.pallas_call(
        body, grid=(M // tm, N // tn, nk), in_specs=in_specs, out_specs=o_spec,
        out_shape=S((M, N), out_dtype), scratch_shapes=[pltpu.VMEM((tm, tn), F32)],
        compiler_params=_cparams(("parallel", "parallel", "arbitrary")), name=name)(*args)


def _mm_dil(a_list, b_list, d, acc, name, tm=1024):
    T, K = a_list[0].shape
    N = b_list[0].shape[0]
    tm, tn = min(tm, T), _pick(N, 1024)
    na = len(a_list)
    has_acc = acc is not None

    def body(*refs):
        a_refs, b_refs, rest = refs[:na], refs[na:2 * na], refs[2 * na:]
        c_ref = rest[0] if has_acc else None
        o_ref, scr = rest[-2], rest[-1]
        out = c_ref[...] if has_acc else None
        for a_ref, b_ref in zip(a_refs, b_refs):
            a_tok = _dil_to_tok(scr, a_ref, d).astype(BF16) if d > 1 else a_ref[...]
            part = lax.dot_general(a_tok, b_ref[...], NT, preferred_element_type=F32)
            out = part if out is None else out + part
        o_ref[...] = out

    if d > 1:
        a_spec = pl.BlockSpec((d, tm // d, K), lambda i, j: (0, i, 0))
        a_args = [a.reshape(d, T // d, K) for a in a_list]
    else:
        a_spec = pl.BlockSpec((tm, K), lambda i, j: (i, 0))
        a_args = list(a_list)
    o_spec = pl.BlockSpec((tm, tn), lambda i, j: (i, j))
    in_specs = [a_spec] * na + [pl.BlockSpec((tn, K), lambda i, j: (j, 0))] * na + ([o_spec] if has_acc else [])
    return pl.pallas_call(
        body, grid=(T // tm, N // tn), in_specs=in_specs, out_specs=o_spec, out_shape=S((T, N), F32),
        scratch_shapes=[pltpu.VMEM((K // LANES, tm, LANES), F32)],
        compiler_params=_cparams(("parallel", "parallel")), name=name)(*a_args, *b_list, *([acc] if has_acc else []))


def _dil_to_tok(scr, ref, d):
    n, C = ref.shape[1], ref.shape[2]
    for r in range(d):
        v = ref[r].astype(F32)
        for cb in range(C // LANES):
            scr.at[cb][pl.ds(r, n, stride=d), :] = v[:, cb * LANES:(cb + 1) * LANES]
    return jnp.concatenate([scr[cb] for cb in range(C // LANES)], axis=1)


def _tok_to_dil(scr, val, ref, d):
    n, C = ref.shape[1], ref.shape[2]
    for cb in range(C // LANES):
        scr[cb] = val[:, cb * LANES:(cb + 1) * LANES].astype(F32)
    for r in range(d):
        ref[r] = jnp.concatenate([scr.at[cb][pl.ds(r, n, stride=d), :] for cb in range(C // LANES)], axis=1).astype(ref.dtype)


def _rowwise(fn, rows, fulls, outs, accs=(), *, tm, name):
    rows = [r if isinstance(r, tuple) else (r, r.shape[1], 0) for r in rows]
    first = rows[0]
    T = (first[1] if isinstance(first[0], str) else first[0]).shape[0]
    widest = max([r[1].shape[1] if isinstance(r[0], str) else r[1] for r in rows] + [o[0] for o in outs])
    tm = min(tm, max(8, ROW_TILES_BYTES // (2 * (len(rows) + len(outs))) // (4 * widest) // 8 * 8))
    tm = T if T <= tm else _pick(T, tm, 8)
    nr, nf, no, na = len(rows), len(fulls), len(outs), len(accs)
    dil_in = [i for i, r in enumerate(rows) if isinstance(r[0], str) and r[2] > 1]
    dil_out = [i for i, o in enumerate(outs) if len(o) == 3 and o[2] > 1]
    scr_cols = [rows[i][1].shape[1] for i in dil_in] + [outs[i][0] for i in dil_out]

    def body(*refs):
        r, f = refs[:nr], refs[nr:nr + nf]
        o, a = refs[nr + nf:nr + nf + no], refs[nr + nf + no:nr + nf + no + na]
        scr = refs[nr + nf + no + na:]
        tiles = []
        for i, x in enumerate(r):
            if i in dil_in:
                tiles.append(_dil_to_tok(scr[dil_in.index(i)], x, rows[i][2]))
            else:
                tiles.append(x[...])
        ro, ra = fn(*tiles, *[x[...] for x in f])
        for i, (ref, val) in enumerate(zip(o, ro)):
            if i in dil_out:
                _tok_to_dil(scr[len(dil_in) + dil_out.index(i)], val, ref, outs[i][2])
            else:
                ref[...] = val.astype(ref.dtype)
        if na:
            @pl.when(pl.program_id(0) == 0)
            def _():
                for ref in a:
                    ref[...] = jnp.zeros_like(ref)
            for ref, val in zip(a, ra):
                ref[...] += val

    in_specs, args = [], []
    for i, rr in enumerate(rows):
        if isinstance(rr[0], str):
            arr, d = rr[1], rr[2]
            if d > 1:
                in_specs.append(pl.BlockSpec((d, tm // d, arr.shape[1]), lambda i: (0, i, 0)))
                args.append(arr.reshape(d, T // d, arr.shape[1]))
            else:
                in_specs.append(pl.BlockSpec((tm, arr.shape[1]), lambda i: (i, 0)))
                args.append(arr)
        else:
            in_specs.append(pl.BlockSpec((tm, rr[1]), functools.partial(lambda i, cb: (i, cb), cb=rr[2])))
            args.append(rr[0])
    in_specs += [pl.BlockSpec(f.shape, lambda i: (0, 0)) for f in fulls]
    out_specs, out_shape = [], []
    for i, oo in enumerate(outs):
        if i in dil_out:
            d = oo[2]
            out_specs.append(pl.BlockSpec((d, tm // d, oo[0]), lambda i: (0, i, 0)))
            out_shape.append(S((d, T // d, oo[0]), oo[1]))
        else:
            out_specs.append(pl.BlockSpec((tm, oo[0]), lambda i: (i, 0)))
            out_shape.append(S((T, oo[0]), oo[1]))
    out_specs += [pl.BlockSpec(sh, lambda i: (0, 0)) for sh in accs]
    out_shape += [S(sh, F32) for sh in accs]
    res = pl.pallas_call(
        body, grid=(T // tm,), in_specs=in_specs, out_specs=out_specs, out_shape=out_shape,
        scratch_shapes=[pltpu.VMEM((c // LANES, tm, LANES), F32) for c in scr_cols],
        compiler_params=_cparams(("arbitrary",)), name=name)(*args, *fulls)
    return [x.reshape(T, x.shape[2]) if i in dil_out else x for i, x in enumerate(res)]


def _rmsnorm_fwd(h, w, name, dils=()):
    D = h.shape[1]

    def fn(h, w):
        r = lax.rsqrt(jnp.mean(h * h, axis=-1, keepdims=True) + EPS)
        xn = h * r * w
        return [xn] * (1 + len(dils)), []
    return _rowwise(fn, [h], [w.reshape(1, -1)], [(D, BF16)] + [(D, BF16, d) for d in dils], tm=512, name=name)


def _rmsnorm_bwd(dxn, h, w, dres, name):
    def fn(dxn, h, dres, w):
        r = lax.rsqrt(jnp.mean(h * h, axis=-1, keepdims=True) + EPS)
        n = h * r
        dn = dxn * w
        dh = r * (dn - n * jnp.mean(dn * n, axis=-1, keepdims=True)) + dres
        return [dh], [jnp.sum(dxn * n, axis=0, keepdims=True)]
    D = h.shape[1]
    return _rowwise(fn, [dxn, h, dres], [w.reshape(1, -1)], [(D, F32)], [(1, D)], tm=256, name=name)


def _loss_head(h, w, tgt, name):
    D = h.shape[1]

    def fn(h, tgt, w):
        r = lax.rsqrt(jnp.mean(h * h, axis=-1, keepdims=True) + EPS)
        n = h * r
        e = n * w - tgt
        row_loss = 0.5 * jnp.mean(e * e, axis=-1, keepdims=True)
        dy = e * (1.0 / D)
        dn = dy * w
        dh = r * (dn - n * jnp.mean(dn * n, axis=-1, keepdims=True))
        return [dh], [jnp.sum(dy * n, axis=0, keepdims=True), jnp.broadcast_to(jnp.sum(row_loss, axis=0, keepdims=True), (1, LANES))]
    return _rowwise(fn, [h, tgt], [w.reshape(1, -1)], [(D, F32)], [(1, D), (1, LANES)], tm=256, name=name)


def _combine_fwd(os_, ls, name):
    def fn(o0, o1, o2, l0, l1, l2):
        m = jnp.maximum(jnp.maximum(l0, l1), l2)
        e0, e1, e2 = jnp.exp(l0 - m), jnp.exp(l1 - m), jnp.exp(l2 - m)
        attn = (e0 * o0 + e1 * o1 + e2 * o2) / (e0 + e1 + e2)
        return [attn, attn], []
    dil = [("dil", t, d) for t, (_, d) in zip(list(os_) + list(ls), DILATED_GROUPS * 2)]
    return _rowwise(fn, dil, [], [(ATTN_OUT_WIDTH, BF16), (ATTN_OUT_WIDTH, F32)], tm=512, name=name)


def _combine_bwd(dattn, attn, ls, head_ones, name):
    def fn(dattn, attn, l0, l1, l2, ones):
        m = jnp.maximum(jnp.maximum(l0, l1), l2)
        e0, e1, e2 = jnp.exp(l0 - m), jnp.exp(l1 - m), jnp.exp(l2 - m)
        inv = 1.0 / (e0 + e1 + e2)
        t = _dotx_r(dattn * attn, ones, parts=2)
        outs = []
        for e in (e0, e1, e2):
            al = e * inv
            outs += [al * dattn, al * t]
        return outs, []
    W = ATTN_OUT_WIDTH
    dil = [("dil", t, d) for t, (_, d) in zip(ls, DILATED_GROUPS)]
    outs = [(W, F32, d) for _, d in DILATED_GROUPS for _ in range(2)]
    return _rowwise(fn, [dattn, attn] + dil, [head_ones], outs, tm=512, name=name)


def _dt_fwd(dt_raw, dt_bias, name):
    def fn(raw, b):
        z = raw[:, :N_SSM_HEADS] + b
        return [jnp.maximum(z, 0.0) + jnp.log(1.0 + jnp.exp(-jnp.abs(z)))], []
    return _rowwise(fn, [dt_raw], [dt_bias.reshape(1, -1)], [(N_SSM_HEADS, F32)], tm=1024, name=name)[0]


def _dt_bwd(ddt_a, ddt_b, dt_raw, dt_bias, name):
    def fn(da, db, raw, b):
        g = (da + db) * jax.nn.sigmoid(raw[:, :N_SSM_HEADS] + b)
        pad = jnp.zeros((g.shape[0], LANES - N_SSM_HEADS), F32)
        return [jnp.concatenate([g, pad], axis=1)], [jnp.sum(g, axis=0, keepdims=True)]
    return _rowwise(fn, [ddt_a, ddt_b, dt_raw], [dt_bias.reshape(1, -1)], [(LANES, BF16)], [(1, N_SSM_HEADS)], tm=1024, name=name)


def _ssm_norm_fwd(y, z, w, name):
    G = D_INNER // N_SSM_GROUPS

    def fn(y, z, w):
        yg = y * _silu(z)
        outs = []
        for g in range(N_SSM_GROUPS):
            t = yg[:, g * G:(g + 1) * G]
            outs.append(t * lax.rsqrt(jnp.mean(t * t, axis=-1, keepdims=True) + EPS))
        return [jnp.concatenate(outs, axis=1) * w], []
    return _rowwise(fn, [y, z], [w.reshape(1, -1)], [(D_INNER, BF16)], tm=256, name=name)[0]


def _ssm_norm_bwd(dssm, y, z, w, name):
    G = D_INNER // N_SSM_GROUPS

    def fn(dssm, y, z, w):
        sz = _silu(z)
        yg = y * sz
        dn = dssm * w
        ns, dygs = [], []
        for g in range(N_SSM_GROUPS):
            t = yg[:, g * G:(g + 1) * G]
            r = lax.rsqrt(jnp.mean(t * t, axis=-1, keepdims=True) + EPS)
            n = t * r
            d = dn[:, g * G:(g + 1) * G]
            dygs.append(r * (d - n * jnp.mean(d * n, axis=-1, keepdims=True)))
            ns.append(n)
        n, dyg = jnp.concatenate(ns, axis=1), jnp.concatenate(dygs, axis=1)
        return [dyg * sz, dyg * y * _dsilu(z)], [jnp.sum(dssm * n, axis=0, keepdims=True)]
    return _rowwise(fn, [dssm, y, z], [w.reshape(1, -1)], [(D_INNER, F32), (D_INNER, BF16)], [(1, D_INNER)], tm=256, name=name)


def _gate_fwd(a, sb, gl, name):
    def fn(a, sb, gl):
        g = jax.nn.sigmoid(gl)
        return [g[:, :D_MODEL] * a + g[:, D_MODEL:] * sb], []
    return _rowwise(fn, [a, sb, gl], [], [(D_MODEL, BF16)], tm=512, name=name)[0]


def _gate_bwd(dm, a, sb, gl, name):
    def fn(dm, a, sb, gl):
        g = jax.nn.sigmoid(gl)
        g0, g1 = g[:, :D_MODEL], g[:, D_MODEL:]
        dgl = jnp.concatenate([dm * a * g0 * (1.0 - g0), dm * sb * g1 * (1.0 - g1)], axis=1)
        return [g0 * dm, g1 * dm, dgl], []
    return _rowwise(fn, [dm, a, sb, gl], [], [(D_MODEL, BF16), (D_MODEL, BF16), (2 * D_MODEL, BF16)], tm=512, name=name)


def _swiglu_fwd(u, name):
    def fn(u):
        return [_silu(u[:, :D_FF]) * u[:, D_FF:]], []
    return _rowwise(fn, [u], [], [(D_FF, BF16)], tm=256, name=name)[0]


def _swiglu_bwd(dact, u, name):
    def fn(dact, u):
        gate, up = u[:, :D_FF], u[:, D_FF:]
        return [jnp.concatenate([dact * up * _dsilu(gate), dact * _silu(gate)], axis=1)], []
    return _rowwise(fn, [dact, u], [], [(2 * D_FF, BF16)], tm=256, name=name)[0]


def _adamw(w, g, m, v, name):
    c1 = 1.0 - ADAM_B1 ** ADAM_STEP
    c2 = 1.0 - ADAM_B2 ** ADAM_STEP

    def fn(w, g, m, v):
        m = ADAM_B1 * m + (1.0 - ADAM_B1) * g
        v = ADAM_B2 * v + (1.0 - ADAM_B2) * (g * g)
        delta = -ADAM_LR * ((m / c1) / (jnp.sqrt(v / c2) + ADAM_EPS) + ADAM_WD * w)
        return [delta, m, v], []
    C = w.shape[1]
    return _rowwise(fn, [w, g, m, v], [], [(C, F32)] * 3, tm=256, name=name)


def _bias_consts(dilation, n_steps):
    qi = np.arange(ATTN_BLOCK)[:, None]
    kj = np.arange(2 * ATTN_BLOCK)[None, :]
    steps = qi + ATTN_BLOCK - kj
    valid = (steps >= 0) & (steps <= n_steps)
    dist = jnp.asarray(np.clip(steps, 0, n_steps) * dilation, jnp.int32)
    max_exact = N_REL_BUCKETS // 2
    d_f = jnp.maximum(dist, 1).astype(F32)
    large = max_exact + (jnp.log(d_f / max_exact) / math.log(REL_MAX_DISTANCE / max_exact)
                         * (N_REL_BUCKETS - max_exact)).astype(jnp.int32)
    large = jnp.minimum(large, N_REL_BUCKETS - 1)
    bucket = jnp.where(dist < max_exact, dist, large).reshape(-1)
    onehot = (bucket[None, :] == jnp.arange(N_REL_BUCKETS)[:, None]).astype(F32)
    return onehot, jnp.asarray(valid.reshape(1, -1), F32)


def _bias_gather(rel_g_t, onehot, valid, name):
    def body(r_ref, oh_ref, v_ref, o_ref):
        b = jnp.dot(r_ref[...], oh_ref[...], preferred_element_type=F32, precision=lax.Precision.HIGHEST)
        o_ref[...] = jnp.where(v_ref[...] > 0.5, b, NEG)
    return pl.pallas_call(body, out_shape=S((HEADS_PER_GROUP, onehot.shape[1]), F32), compiler_params=_cparams(), name=name)(rel_g_t, onehot, valid)


def _bias_scatter(dbias, onehot, name):
    def body(d_ref, oh_ref, o_ref):
        o_ref[...] = lax.dot_general(d_ref[...], oh_ref[...], NT, preferred_element_type=F32, precision=lax.Precision.HIGHEST)
    return pl.pallas_call(body, out_shape=S((HEADS_PER_GROUP, N_REL_BUCKETS), F32), compiler_params=_cparams(), name=name)(dbias, onehot)


ATTN_QB = 4


def _attn_tiles(T, d):
    seg = T // d
    nqb = min(ATTN_QB, seg // ATTN_BLOCK)
    tq = nqb * ATTN_BLOCK
    return seg, nqb, tq, seg // tq


def _attn_fwd(qkv, bias, d, name):
    T = qkv.shape[0]
    seg, nqb, tq, ns = _attn_tiles(T, d)
    W = ATTN_OUT_WIDTH
    scale = HEAD_DIM ** -0.5

    def body(q_ref, kh_ref, kc_ref, vh_ref, vc_ref, b_ref, o_ref, l_ref):
        n = pl.program_id(1)
        qv = q_ref[...]
        kk = jnp.concatenate([kh_ref[...], kc_ref[...]], axis=0)
        vv = jnp.concatenate([vh_ref[...], vc_ref[...]], axis=0)
        col = lax.broadcasted_iota(jnp.int32, (ATTN_BLOCK, 2 * ATTN_BLOCK), 1)
        kill = jnp.logical_and(n == 0, col < ATTN_BLOCK)
        for j in range(nqb):
            rows = slice(j * ATTN_BLOCK, (j + 1) * ATTN_BLOCK)
            keys = slice(j * ATTN_BLOCK, (j + 2) * ATTN_BLOCK)
            for h in range(HEADS_PER_GROUP):
                sl = slice(h * HEAD_DIM, (h + 1) * HEAD_DIM)
                s = lax.dot_general(qv[rows, sl], kk[keys, sl], NT, preferred_element_type=F32) * scale + b_ref[h]
                if j == 0:
                    s = jnp.where(kill, NEG, s)
                m = jnp.max(s, axis=-1, keepdims=True)
                p = jnp.exp(s - m)
                den = jnp.sum(p, axis=-1, keepdims=True)
                o = jnp.dot(p.astype(BF16), vv[keys, sl], preferred_element_type=F32) / den
                o_ref[rows, sl] = o
                l_ref[rows, sl] = jnp.broadcast_to(m + jnp.log(den), (ATTN_BLOCK, HEAD_DIM))

    def cur(c):
        return pl.BlockSpec((tq, W), lambda r, n: (r * ns + n, c))

    def halo(c):
        return pl.BlockSpec((ATTN_BLOCK, W), lambda r, n: (jnp.maximum((r * ns + n) * nqb - 1, 0), c))

    return pl.pallas_call(
        body, grid=(d, ns),
        in_specs=[cur(0), halo(1), cur(1), halo(2), cur(2), pl.BlockSpec(bias.shape, lambda r, n: (0, 0, 0))],
        out_specs=[cur(0), cur(0)], out_shape=[S((T, W), F32)] * 2,
        compiler_params=_cparams(("parallel", "arbitrary")), name=name)(qkv, qkv, qkv, qkv, qkv, bias)


def _attn_bwd(qkv, bias, lse, do, dd, d, name):
    T = qkv.shape[0]
    seg, nqb, tq, ns = _attn_tiles(T, d)
    W = ATTN_OUT_WIDTH
    B = ATTN_BLOCK
    scale = HEAD_DIM ** -0.5

    def body(q_ref, kh_ref, kc_ref, vh_ref, vc_ref, b_ref, l_ref, do_ref, dd_ref, dq_ref, dk_ref, dv_ref, db_ref, pk_ref, pv_ref):
        r, n = pl.program_id(0), pl.program_id(1)

        @pl.when(jnp.logical_and(r == 0, n == 0))
        def _():
            db_ref[...] = jnp.zeros_like(db_ref)

        @pl.when(n == 0)
        def _():
            pk_ref[...] = jnp.zeros_like(pk_ref)
            pv_ref[...] = jnp.zeros_like(pv_ref)

        @pl.when(n < ns)
        def _():
            qv = q_ref[...]
            kk = jnp.concatenate([kh_ref[...], kc_ref[...]], axis=0)
            vv = jnp.concatenate([vh_ref[...], vc_ref[...]], axis=0)
            lse_v, do_v, dd_v = l_ref[...], do_ref[...], dd_ref[...]
            col = lax.broadcasted_iota(jnp.int32, (B, 2 * B), 1)
            kill = jnp.logical_and(n == 0, col < B)
            dqs = [[None] * HEADS_PER_GROUP for _ in range(nqb)]
            dks = [[None] * HEADS_PER_GROUP for _ in range(nqb)]
            dvs = [[None] * HEADS_PER_GROUP for _ in range(nqb)]
            for h in range(HEADS_PER_GROUP):
                sl = slice(h * HEAD_DIM, (h + 1) * HEAD_DIM)
                c0 = slice(h * HEAD_DIM, h * HEAD_DIM + 1)
                dsum = None
                for j in range(nqb):
                    rows = slice(j * B, (j + 1) * B)
                    keys = slice(j * B, (j + 2) * B)
                    qh, kh, vh = qv[rows, sl], kk[keys, sl], vv[keys, sl]
                    s = lax.dot_general(qh, kh, NT, preferred_element_type=F32) * scale + b_ref[h]
                    if j == 0:
                        s = jnp.where(kill, NEG, s)
                    p = jnp.exp(s - lse_v[rows, c0])
                    doh = do_v[rows, sl].astype(BF16)
                    dvs[j][h] = lax.dot_general(p.astype(BF16), doh, TN, preferred_element_type=F32)
                    dp = lax.dot_general(doh, vh, NT, preferred_element_type=F32)
                    ds = p * (dp - dd_v[rows, c0])
                    dsum = ds if dsum is None else dsum + ds
                    dsb = (ds * scale).astype(BF16)
                    dqs[j][h] = jnp.dot(dsb, kh, preferred_element_type=F32)
                    dks[j][h] = lax.dot_general(dsb, qh, TN, preferred_element_type=F32)
                db_ref[h] += dsum
            dq_ref[...] = jnp.concatenate([jnp.concatenate(dqs[j], axis=1) for j in range(nqb)], axis=0).astype(dq_ref.dtype)
            for parts, out_ref, pend in ((dks, dk_ref, pk_ref), (dvs, dv_ref, pv_ref)):
                full = [jnp.concatenate(parts[j], axis=1) for j in range(nqb)]
                if tq > B:
                    out_ref[:tq - B] = pend[:tq - B].astype(out_ref.dtype)
                out_ref[tq - B:] = (pend[tq - B:] + full[0][:B]).astype(out_ref.dtype)
                for j in range(nqb - 1):
                    pend[j * B:(j + 1) * B] = full[j][B:] + full[j + 1][:B]
                pend[tq - B:] = full[nqb - 1][B:]

        @pl.when(n == ns)
        def _():
            dk_ref[...] = pk_ref[...].astype(dk_ref.dtype)
            dv_ref[...] = pv_ref[...].astype(dv_ref.dtype)

    def cur(c):
        return pl.BlockSpec((tq, W), lambda r, n: (r * ns + jnp.minimum(n, ns - 1), c))

    def halo(c):
        return pl.BlockSpec((B, W), lambda r, n: (jnp.maximum((r * ns + jnp.minimum(n, ns - 1)) * nqb - 1, 0), c))

    late = pl.BlockSpec((tq, W), lambda r, n: (r * ns + jnp.clip(n - 1, 0, ns - 1), 0))
    bspec = pl.BlockSpec(bias.shape, lambda r, n: (0, 0, 0))
    return pl.pallas_call(
        body, grid=(d, ns + 1),
        in_specs=[cur(0), halo(1), cur(1), halo(2), cur(2), bspec, cur(0), cur(0), cur(0)],
        out_specs=[cur(0), late, late, bspec],
        out_shape=[S((T, W), BF16)] * 3 + [S(bias.shape, F32)],
        scratch_shapes=[pltpu.VMEM((tq, W), F32), pltpu.VMEM((tq, W), F32)],
        compiler_params=_cparams(("arbitrary", "arbitrary")), name=name,
    )(qkv, qkv, qkv, qkv, qkv, bias, lse, do, dd)


CONV_TM, CONV_TC = 512, 1024


def _shift_down(x, halo8, s, row8):
    xr = pltpu.roll(x, s, 0)
    first = jnp.where(row8 < s, pltpu.roll(halo8, s, 0), xr[:8])
    return jnp.concatenate([first, xr[8:]], axis=0)


def _shift_up(x, halo8, s, row8):
    n = x.shape[0]
    xr = pltpu.roll(x, n - s, 0)
    last = jnp.where(row8 >= 8 - s, pltpu.roll(halo8, 8 - s, 0), xr[n - 8:])
    return jnp.concatenate([xr[:n - 8], last], axis=0)


def _conv_fwd(x, w, b, name):
    T, C = x.shape
    tm, tc = min(CONV_TM, T), CONV_TC

    def body(x_ref, p_ref, w_ref, b_ref, u_ref, a_ref):
        ti = pl.program_id(1)
        xv = x_ref[...]
        p8 = jnp.where(ti == 0, 0.0, p_ref[...])
        wv = w_ref[...]
        row8 = lax.broadcasted_iota(jnp.int32, (8, tc), 0)
        u = xv * wv[3:4] + b_ref[...]
        for s in (1, 2, 3):
            u = u + _shift_down(xv, p8, s, row8) * wv[3 - s:4 - s]
        u_ref[...] = u
        a_ref[...] = _silu(u)

    cur = pl.BlockSpec((tm, tc), lambda cj, ti: (ti, cj))
    halo = pl.BlockSpec((8, tc), lambda cj, ti: (jnp.maximum(ti * (tm // 8) - 1, 0), cj))
    return pl.pallas_call(
        body, grid=(C // tc, T // tm),
        in_specs=[cur, halo, pl.BlockSpec((CONV_WIDTH, tc), lambda cj, ti: (0, cj)), pl.BlockSpec((1, tc), lambda cj, ti: (0, cj))],
        out_specs=[cur, cur], out_shape=[S((T, C), F32)] * 2,
        compiler_params=_cparams(("parallel", "arbitrary")), name=name)(x, x, w, b)


def _conv_bwd(dact, u, x, w, name):
    T, C = x.shape
    tm, tc = min(CONV_TM, T), CONV_TC
    nt = T // tm

    def body(d_ref, dn_ref, u_ref, un_ref, x_ref, xp_ref, w_ref, dx_ref, dw_ref, db_ref):
        ti = pl.program_id(1)

        @pl.when(ti == 0)
        def _():
            dw_ref[...] = jnp.zeros_like(dw_ref)
            db_ref[...] = jnp.zeros_like(db_ref)

        du = d_ref[...] * _dsilu(u_ref[...])
        dun = jnp.where(ti == nt - 1, 0.0, dn_ref[...] * _dsilu(un_ref[...]))
        xv = x_ref[...]
        xp = jnp.where(ti == 0, 0.0, xp_ref[...])
        wv = w_ref[...]
        row8 = lax.broadcasted_iota(jnp.int32, (8, tc), 0)
        dx = du * wv[3:4]
        dws = [None] * CONV_WIDTH
        dws[3] = jnp.sum(du * xv, axis=0, keepdims=True)
        for s in (1, 2, 3):
            dx = dx + _shift_up(du, dun, s, row8) * wv[3 - s:4 - s]
            dws[3 - s] = jnp.sum(du * _shift_down(xv, xp, s, row8), axis=0, keepdims=True)
        dx_ref[...] = dx.astype(dx_ref.dtype)
        dw_ref[...] += jnp.concatenate(dws, axis=0)
        db_ref[...] += jnp.sum(du, axis=0, keepdims=True)

    cur = pl.BlockSpec((tm, tc), lambda cj, ti: (ti, cj))
    prev = pl.BlockSpec((8, tc), lambda cj, ti: (jnp.maximum(ti * (tm // 8) - 1, 0), cj))
    nxt = pl.BlockSpec((8, tc), lambda cj, ti: (jnp.minimum((ti + 1) * (tm // 8), T // 8 - 1), cj))
    return pl.pallas_call(
        body, grid=(C // tc, nt),
        in_specs=[cur, nxt, cur, nxt, cur, prev, pl.BlockSpec((CONV_WIDTH, tc), lambda cj, ti: (0, cj))],
        out_specs=[cur, pl.BlockSpec((CONV_WIDTH, tc), lambda cj, ti: (0, cj)), pl.BlockSpec((1, tc), lambda cj, ti: (0, cj))],
        out_shape=[S((T, C), BF16), S((CONV_WIDTH, C), F32), S((1, C), F32)],
        compiler_params=_cparams(("parallel", "arbitrary")), name=name)(dact, dact, u, u, x, x, w)


def _ssd_consts():
    i = np.arange(SSD_CHUNK)
    tril = (i[None, :] <= i[:, None]).astype(np.float32)
    trils = (i[None, :] < i[:, None]).astype(np.float32)
    head = np.repeat(np.arange(N_SSM_HEADS), D_INNER // N_SSM_HEADS)
    et = (head[None, :] == np.arange(N_SSM_HEADS)[:, None]).astype(np.float32)
    c = lambda a: jnp.asarray(a, BF16)
    return dict(tril=c(tril), triu=c(tril.T), trils=c(trils), trius=c(trils.T), et=c(et), e=c(et.T))


def _ssd_common(act_ref, dt_ref, dtT_ref, al_ref, alT_ref, tril_ref, triu_ref, et_ref):
    a_row = -jnp.exp(al_ref[...])
    a_col = -jnp.exp(alT_ref[...])
    dt, dtT = dt_ref[...], dtT_ref[...]
    la = _dotx_l(tril_ref[...], dt * a_row)
    laT = _dotx_r(dtT * a_col, triu_ref[...])
    et = et_ref[...]
    la_e = _dotx_r(la, et)
    dt_e = _dotx_r(dt, et)
    x = act_ref[:, :D_INNER]
    xdt = x * dt_e
    la_q = la_e[SSD_CHUNK - 1:SSD_CHUNK, :]
    return a_row, a_col, dt, dtT, la, laT, la_e, dt_e, x, xdt, la_q


def _decay(la, laT, h, causal):
    seg = la[:, h:h + 1] - laT[h:h + 1, :]
    return jnp.exp(jnp.where(causal, seg, NEG))


def _ssd_fwd(act, dt, dtT, alog, dskip_e, cs, name):
    T = act.shape[0]
    nc = T // SSD_CHUNK
    Q, G, GW = SSD_CHUNK, N_SSM_GROUPS, D_INNER // N_SSM_GROUPS

    def body(act_ref, dt_ref, dtT_ref, al_ref, alT_ref, dsk_ref, tril_ref, triu_ref, et_ref, y_ref, st_ref, scr):
        @pl.when(pl.program_id(0) == 0)
        def _():
            scr[...] = jnp.zeros_like(scr)
        st_ref[0] = scr[...]
        a_row, a_col, dtv, dtTv, la, laT, la_e, dt_e, x, xdt, la_q = _ssd_common(
            act_ref, dt_ref, dtT_ref, al_ref, alT_ref, tril_ref, triu_ref, et_ref)
        ela = jnp.exp(la_e)
        xdt_b = xdt.astype(BF16)
        xdte_b = (xdt * jnp.exp(la_q - la_e)).astype(BF16)
        ela_q = jnp.exp(la_q)
        causal = lax.broadcasted_iota(jnp.int32, (Q, Q), 0) >= lax.broadcasted_iota(jnp.int32, (Q, Q), 1)
        for g in range(G):
            gs = slice(g * GW, (g + 1) * GW)
            Bg = act_ref[:, D_INNER + g * D_STATE:D_INNER + (g + 1) * D_STATE].astype(BF16)
            Cg = act_ref[:, D_INNER + G * D_STATE + g * D_STATE:D_INNER + G * D_STATE + (g + 1) * D_STATE].astype(BF16)
            cb = lax.dot_general(Cg, Bg, NT, preferred_element_type=F32)
            st = scr[g]
            y_inter = jnp.dot(Cg, st.astype(BF16), preferred_element_type=F32) * ela[:, gs]
            ys = []
            for hh in range(HEADS_PER_GROUP):
                h = g * HEADS_PER_GROUP + hh
                m = (cb * _decay(la, laT, h, causal)).astype(BF16)
                ys.append(jnp.dot(m, xdt_b[:, h * HEAD_DIM:(h + 1) * HEAD_DIM], preferred_element_type=F32))
            y_ref[:, gs] = jnp.concatenate(ys, axis=1) + y_inter + x[:, gs] * dsk_ref[:, gs]
            scr[g] = st * ela_q[:, gs] + lax.dot_general(Bg, xdte_b[:, gs], TN, preferred_element_type=F32)

    full = lambda a: pl.BlockSpec(a.shape, lambda c: (0,) * a.ndim)
    al, alT = alog.reshape(1, -1), alog.reshape(-1, 1)
    return pl.pallas_call(
        body, grid=(nc,),
        in_specs=[pl.BlockSpec((Q, XBC_WIDTH), lambda c: (c, 0)), pl.BlockSpec((Q, N_SSM_HEADS), lambda c: (c, 0)),
                  pl.BlockSpec((N_SSM_HEADS, Q), lambda c: (0, c)), full(al), full(alT), full(dskip_e),
                  full(cs["tril"]), full(cs["triu"]), full(cs["et"])],
        out_specs=[pl.BlockSpec((Q, D_INNER), lambda c: (c, 0)), pl.BlockSpec((1, G, D_STATE, GW), lambda c: (c, 0, 0, 0))],
        out_shape=[S((T, D_INNER), F32), S((nc, G, D_STATE, GW), F32)],
        scratch_shapes=[pltpu.VMEM((G, D_STATE, GW), F32)],
        compiler_params=_cparams(("arbitrary",)), name=name)(act, dt, dtT, al, alT, dskip_e, cs["tril"], cs["triu"], cs["et"])


def _ssd_bwd(dy, act, dt, dtT, alog, dskip_e, states, cs, name):
    T = act.shape[0]
    nc = T // SSD_CHUNK
    Q, G, GW, H = SSD_CHUNK, N_SSM_GROUPS, D_INNER // N_SSM_GROUPS, N_SSM_HEADS

    def body(dy_ref, act_ref, dt_ref, dtT_ref, al_ref, alT_ref, dsk_ref, stp_ref, tril_ref, triu_ref, trils_ref, trius_ref,
             et_ref, e_ref, dact_ref, ddt_ref, ddtT_ref, da_ref, daT_ref, dsk_out_ref, dst, wbuf, ubuf, vbuf, sbuf):
        @pl.when(pl.program_id(0) == 0)
        def _():
            dst[...] = jnp.zeros_like(dst)
            da_ref[...] = jnp.zeros_like(da_ref)
            daT_ref[...] = jnp.zeros_like(daT_ref)
            dsk_out_ref[...] = jnp.zeros_like(dsk_out_ref)
        a_row, a_col, dtv, dtTv, la, laT, la_e, dt_e, x, xdt, la_q = _ssd_common(
            act_ref, dt_ref, dtT_ref, al_ref, alT_ref, tril_ref, triu_ref, et_ref)
        dyv = dy_ref[...]
        ela = jnp.exp(la_e)
        e_end = jnp.exp(la_q - la_e)
        ela_q = jnp.exp(la_q)
        dye_b = (dyv * ela).astype(BF16)
        dy_b = dyv.astype(BF16)
        xdt_b = xdt.astype(BF16)
        xdte_b = (xdt * e_end).astype(BF16)
        ri = lax.broadcasted_iota(jnp.int32, (Q, Q), 0)
        ci = lax.broadcasted_iota(jnp.int32, (Q, Q), 1)
        causal = ri >= ci
        trius = trius_ref[...]
        rows = []
        for g in range(G):
            gs = slice(g * GW, (g + 1) * GW)
            Bg = act_ref[:, D_INNER + g * D_STATE:D_INNER + (g + 1) * D_STATE].astype(BF16)
            Cg = act_ref[:, D_INNER + G * D_STATE + g * D_STATE:D_INNER + G * D_STATE + (g + 1) * D_STATE].astype(BF16)
            cb = lax.dot_general(Cg, Bg, NT, preferred_element_type=F32)
            stp = stp_ref[0, g]
            stp_b = stp.astype(BF16)
            dstv = dst[g]
            dst_b = dstv.astype(BF16)
            y_inter = jnp.dot(Cg, stp_b, preferred_element_type=F32) * ela[:, gs]
            wbuf[:, gs] = dyv[:, gs] * y_inter
            dxdt_state = jnp.dot(Bg, dst_b, preferred_element_type=F32) * e_end[:, gs]
            ubuf[:, gs] = dxdt_state * xdt[:, gs]
            dC = lax.dot_general(dye_b[:, gs], stp_b, NT, preferred_element_type=F32)
            dB = lax.dot_general(xdte_b[:, gs], dst_b, NT, preferred_element_type=F32)
            sbuf[:, gs] = jnp.broadcast_to(jnp.sum(dstv * stp, axis=0, keepdims=True), (8, GW))
            dst[g] = dstv * ela_q[:, gs] + lax.dot_general(Cg, dye_b[:, gs], TN, preferred_element_type=F32)
            dG = jnp.zeros((Q, Q), F32)
            dxs = []
            for hh in range(HEADS_PER_GROUP):
                h = g * HEADS_PER_GROUP + hh
                hs = slice(h * HEAD_DIM, (h + 1) * HEAD_DIM)
                L = _decay(la, laT, h, causal)
                M = cb * L
                dM = lax.dot_general(dy_b[:, hs], xdt_b[:, hs], NT, preferred_element_type=F32)
                dG = dG + dM * L
                R = _dotx_r(dM * M, trius, parts=2)
                rows.append(jnp.sum(jnp.where(causal, R, 0.0), axis=0, keepdims=True))
                dxs.append(lax.dot_general(M.astype(BF16), dy_b[:, hs], TN, preferred_element_type=F32))
            dG_b = dG.astype(BF16)
            dC = dC + jnp.dot(dG_b, Bg, preferred_element_type=F32)
            dB = dB + lax.dot_general(dG_b, Cg, TN, preferred_element_type=F32)
            dxdt = jnp.concatenate(dxs, axis=1) + dxdt_state
            vbuf[:, gs] = dxdt * x[:, gs]
            dact_ref[:, gs] = dxdt * dt_e[:, gs] + dyv[:, gs] * dsk_ref[:, gs]
            dact_ref[:, D_INNER + g * D_STATE:D_INNER + (g + 1) * D_STATE] = dB
            dact_ref[:, D_INNER + G * D_STATE + g * D_STATE:D_INNER + G * D_STATE + (g + 1) * D_STATE] = dC
        e = e_ref[...]
        w = _dotx_r(wbuf[...], e, parts=2)
        u = _dotx_r(ubuf[...], e, parts=2)
        vx = _dotx_r(vbuf[...], e, parts=2)
        dsk = _dotx_r(dyv * x, e, parts=2)
        s0 = _dotx_r(sbuf[...], e, parts=2)[0:1] * jnp.exp(la[Q - 1:Q, :])
        ddelta = _dotx_l(triu_ref[...], w) + _dotx_l(trils_ref[...], u) + s0
        ddt_ref[...] = ddelta * a_row + vx
        ddeltaT = jnp.concatenate(rows, axis=0)
        ddtT_ref[...] = ddeltaT * a_col
        da_ref[...] += jnp.sum(ddelta * dtv, axis=0, keepdims=True)
        daT_ref[...] += jnp.sum(ddeltaT * dtTv, axis=1, keepdims=True)
        dsk_out_ref[...] += jnp.sum(dsk, axis=0, keepdims=True)

    rev = lambda c: nc - 1 - c
    full = lambda a: pl.BlockSpec(a.shape, lambda c: (0,) * a.ndim)
    al, alT = alog.reshape(1, -1), alog.reshape(-1, 1)
    consts = [cs[k] for k in ("tril", "triu", "trils", "trius", "et", "e")]
    return pl.pallas_call(
        body, grid=(nc,),
        in_specs=[pl.BlockSpec((Q, D_INNER), lambda c: (rev(c), 0)), pl.BlockSpec((Q, XBC_WIDTH), lambda c: (rev(c), 0)),
                  pl.BlockSpec((Q, H), lambda c: (rev(c), 0)), pl.BlockSpec((H, Q), lambda c: (0, rev(c))),
                  full(al), full(alT), full(dskip_e), pl.BlockSpec((1, G, D_STATE, GW), lambda c: (rev(c), 0, 0, 0))]
                 + [full(a) for a in consts],
        out_specs=[pl.BlockSpec((Q, XBC_WIDTH), lambda c: (rev(c), 0)), pl.BlockSpec((Q, H), lambda c: (rev(c), 0)),
                   pl.BlockSpec((H, Q), lambda c: (0, rev(c))), pl.BlockSpec((1, H), lambda c: (0, 0)),
                   pl.BlockSpec((H, 1), lambda c: (0, 0)), pl.BlockSpec((1, H), lambda c: (0, 0))],
        out_shape=[S((T, XBC_WIDTH), F32), S((T, H), F32), S((H, T), F32), S((1, H), F32), S((H, 1), F32), S((1, H), F32)],
        scratch_shapes=[pltpu.VMEM((G, D_STATE, GW), F32), pltpu.VMEM((Q, D_INNER), F32), pltpu.VMEM((Q, D_INNER), F32),
                        pltpu.VMEM((Q, D_INNER), F32), pltpu.VMEM((8, D_INNER), F32)],
        compiler_params=_cparams(("arbitrary",)), name=name)(dy, act, dt, dtT, al, alT, dskip_e, states, *consts)


def _a_log_grad(da, daT_row, alog, name):
    def body(a_ref, b_ref, al_ref, o_ref):
        o_ref[...] = (a_ref[...] + b_ref[...]) * (-jnp.exp(al_ref[...]))
    return pl.pallas_call(body, out_shape=S((1, N_SSM_HEADS), F32), name=name)(da, daT_row, alog.reshape(1, -1))


def _place():
    x, y, c = lax.axis_index("x"), lax.axis_index("y"), lax.axis_index("c")
    return x, y, c, [(1 - x, y), (x, 1 - y), (1 - x, 1 - y)]


def _all_gather(shards, name):
    npc = len(shards)

    def body(*refs):
        x_refs, o_refs = refs[:npc], refs[npc:2 * npc]
        send_sems, recv_sems, local_sems = refs[2 * npc:]
        x, y, c, chips = _place()
        me, sib = (x, y, c), (x, y, 1 - c)

        def rows(dev, i):
            return o_refs[i].at[4 * dev[0] + 2 * dev[1] + dev[2]]

        def copy(k, i, block, to, src=None):
            return pltpu.make_async_remote_copy(
                src_ref=rows(block, i) if src is None else src, dst_ref=rows(block, i),
                send_sem=send_sems.at[k, i], recv_sem=recv_sems.at[k, i], device_id=to, device_id_type=MESH)

        mine = [pltpu.make_async_copy(x_refs[i], rows(me, i), local_sems.at[i]) for i in range(npc)]
        for cp in mine:
            cp.start()
        first = []
        for i in range(npc):
            first.append(copy(0, i, me, sib, src=x_refs[i]))
            first += [copy(1 + j, i, me, (*chip, c), src=x_refs[i]) for j, chip in enumerate(chips)]
        for cp in first:
            cp.start()
        passed = []
        for i in range(npc):
            for j, chip in enumerate(chips):
                copy(1 + j, i, (*chip, c), me).wait_recv()
                cp = copy(4 + j, i, (*chip, c), sib)
                cp.start()
                passed.append(cp)
        for i in range(npc):
            copy(0, i, sib, me).wait_recv()
            for j, chip in enumerate(chips):
                copy(4 + j, i, (*chip, 1 - c), me).wait_recv()
        for cp in first + passed:
            cp.wait_send()
        for cp in mine:
            cp.wait()

    anys = pl.BlockSpec(memory_space=pl.ANY)
    return pl.pallas_call(
        body, in_specs=[anys] * npc, out_specs=[anys] * npc, out_shape=[S((N_DEV,) + s.shape, s.dtype) for s in shards],
        scratch_shapes=[pltpu.SemaphoreType.DMA((7, npc)), pltpu.SemaphoreType.DMA((7, npc)), pltpu.SemaphoreType.DMA((npc,))],
        name=name)(*shards)


def _to_sibling(to_sib, name):
    npc = len(to_sib)

    def body(*refs):
        s_refs, o_refs, send_sems, recv_sems = refs[:npc], refs[npc:2 * npc], refs[2 * npc], refs[2 * npc + 1]
        x, y, c, _ = _place()
        cps = [pltpu.make_async_remote_copy(
            src_ref=s_refs[i], dst_ref=o_refs[i], send_sem=send_sems.at[i], recv_sem=recv_sems.at[i],
            device_id=(x, y, 1 - c), device_id_type=MESH) for i in range(npc)]
        for cp in cps:
            cp.start()
        for cp in cps:
            cp.wait()

    anys = pl.BlockSpec(memory_space=pl.ANY)
    return pl.pallas_call(
        body, in_specs=[anys] * npc, out_specs=[anys] * npc, out_shape=[S(s.shape, s.dtype) for s in to_sib],
        scratch_shapes=[pltpu.SemaphoreType.DMA((npc,)), pltpu.SemaphoreType.DMA((npc,))],
        name=name)(*to_sib)


def _to_chips(parts, name):
    npc = len(parts)

    def body(*refs):
        p_refs, o_refs, send_sems, recv_sems = refs[:npc], refs[npc:2 * npc], refs[2 * npc], refs[2 * npc + 1]
        x, y, c, chips = _place()
        cps = [pltpu.make_async_remote_copy(
            src_ref=p_refs[i].at[2 * chip[0] + chip[1]], dst_ref=o_refs[i].at[j], send_sem=send_sems.at[j, i],
            recv_sem=recv_sems.at[j, i], device_id=(*chip, c), device_id_type=MESH)
            for i in range(npc) for j, chip in enumerate(chips)]
        for cp in cps:
            cp.start()
        for cp in cps:
            cp.wait()

    anys = pl.BlockSpec(memory_space=pl.ANY)
    return pl.pallas_call(
        body, in_specs=[anys] * npc, out_specs=[anys] * npc, out_shape=[S((3,) + p.shape[1:], p.dtype) for p in parts],
        scratch_shapes=[pltpu.SemaphoreType.DMA((3, npc)), pltpu.SemaphoreType.DMA((3, npc))],
        name=name)(*parts)


def _reduce_scatter(pieces, name):
    c = lax.axis_index("c")
    x, y = lax.axis_index("x"), lax.axis_index("y")
    by_core = [p.reshape(4, 2, p.shape[1], p.shape[2]) for p in pieces]
    to_sib = [lax.dynamic_index_in_dim(p, 1 - c, axis=1, keepdims=False).astype(BF16) for p in by_core]
    keep = [lax.dynamic_index_in_dim(p, c, axis=1, keepdims=False) for p in by_core]
    from_sib = _to_sibling(to_sib, name + "_d2d")

    def add1(a, b):
        s = a + b.astype(F32)
        return [s, s], []

    def add2(a, b, c_, d_):
        return [((a + b.astype(F32)) + c_.astype(F32)) + d_.astype(F32)], []

    parts, parts_b = [], []
    for i, (k, f) in enumerate(zip(keep, from_sib)):
        _, r, C = k.shape
        p, pb = _rowwise(add1, [k.reshape(4 * r, C), f.reshape(4 * r, C)], [], [(C, F32), (C, BF16)], tm=2048, name=f"{name}_add1_{i}")
        parts.append(p.reshape(4, r, C))
        parts_b.append(pb.reshape(4, r, C))
    got = _to_chips(parts_b, name + "_ici")
    outs = []
    for i, (p, g) in enumerate(zip(parts, got)):
        own = lax.dynamic_index_in_dim(p, 2 * x + y, axis=0, keepdims=False)
        outs.append(_rowwise(add2, [own, g[0], g[1], g[2]], [], [(p.shape[2], F32)], tm=2048, name=f"{name}_add2_{i}")[0])
    return outs


def _all_reduce_small(v, name):
    R, C = v.shape

    def body(x_ref, out_ref, buf, send_sems, recv_sems):
        x, y, c, chips = _place()
        me, sib = (x, y, c), (x, y, 1 - c)

        def rows(dev):
            return buf.at[4 * dev[0] + 2 * dev[1] + dev[2]]

        def copy(k, block, to, src=None):
            return pltpu.make_async_remote_copy(
                src_ref=rows(block) if src is None else src, dst_ref=rows(block),
                send_sem=send_sems.at[k], recv_sem=recv_sems.at[k], device_id=to, device_id_type=MESH)

        buf[4 * x + 2 * y + c] = x_ref[...]
        first = [copy(0, me, sib, src=x_ref)] + [copy(1 + j, me, (*chip, c), src=x_ref) for j, chip in enumerate(chips)]
        for cp in first:
            cp.start()
        passed = [copy(4 + j, (*chip, c), sib) for j, chip in enumerate(chips)]
        for j, chip in enumerate(chips):
            copy(1 + j, (*chip, c), me).wait_recv()
            passed[j].start()
        copy(0, sib, me).wait_recv()
        for j, chip in enumerate(chips):
            copy(4 + j, (*chip, 1 - c), me).wait_recv()
        for cp in first + passed:
            cp.wait_send()
        acc = buf[0]
        for j in range(1, N_DEV):
            acc = acc + buf[j]
        out_ref[...] = acc

    vm = pl.BlockSpec(memory_space=pltpu.VMEM)
    return pl.pallas_call(
        body, in_specs=[vm], out_specs=vm, out_shape=S((R, C), F32),
        scratch_shapes=[pltpu.VMEM((N_DEV, R, C), F32), pltpu.SemaphoreType.DMA((7,)), pltpu.SemaphoreType.DMA((7,))],
        compiler_params=pltpu.CompilerParams(vmem_limit_bytes=VMEM_LIMIT), name=name)(v)


SEG = (("q", 0, 1536), ("k", 1536, 1536), ("v", 3072, 1536), ("z", 4608, 2048), ("xbc", 6656, 3072), ("dt", 9728, 32), ("gl", 9760, 2048))


def _split_w_in(w_in_full):
    out = {}
    for nm, off, n in SEG:
        w = w_in_full[:, off:off + n]
        if nm == "dt":
            w = jnp.pad(w, ((0, 0), (0, LANES - n)))
        out[nm] = w
    W = ATTN_OUT_WIDTH
    out["qkv"] = [jnp.concatenate([out[s][:, g * W:(g + 1) * W] for s in ("q", "k", "v")], axis=1) for g in range(N_DIL)]
    out["qkv_t"] = [[out[s][:, g * W:(g + 1) * W] for s in ("q", "k", "v")] for g in range(N_DIL)]
    return out


def _layer_fwd(h, p, W, biases, cs, l):
    T = h.shape[0]
    nm = lambda s: f"{s}_l{l}"
    sv = {"h_in": h}
    xns = _rmsnorm_fwd(h, p["norm1_w"], nm("norm1"), dils=[d for _, d in DILATED_GROUPS[1:]])
    xn = xns[0]
    wi = W["w_in"]
    z = _mm(xn, wi["z"], name=nm("proj_z"))
    xbc = _mm(xn, wi["xbc"], name=nm("proj_xbc"))
    dt_raw = _mm(xn, wi["dt"], name=nm("proj_dt"))
    gl = _mm(xn, wi["gl"], name=nm("proj_gl"))
    os_, ls, qkvs = [], [], []
    for g, (window, dil) in enumerate(DILATED_GROUPS):
        qkv = _mm(xns[g], wi["qkv"][g], out_dtype=BF16, name=nm(f"proj_qkv_g{g}"))
        o, lse = _attn_fwd(qkv, biases[g], dil, nm(f"attn_fwd_g{g}"))
        os_.append(o)
        ls.append(lse)
        qkvs.append(qkv)
    attn_b, attn_f = _combine_fwd(os_, ls, nm("combine"))
    u_conv, act = _conv_fwd(xbc, p["conv_w"], p["conv_b"].reshape(1, -1), nm("conv"))
    dt = _dt_fwd(dt_raw, p["dt_bias"], nm("dt"))
    dtT = dt.T
    dskip_e = jnp.repeat(p["d_skip"], D_INNER // N_SSM_HEADS).reshape(1, -1)
    y, states = _ssd_fwd(act, dt, dtT, p["a_log"], dskip_e, cs, nm("ssd_fwd"))
    ssm = _ssm_norm_fwd(y, z, p["ssm_norm_w"], nm("ssm_norm"))
    a_br = _mm(attn_b, W["w_attn_branch"], name=nm("attn_branch"))
    s_br = _mm(ssm, W["w_ssm_branch"], name=nm("ssm_branch"))
    merged = _gate_fwd(a_br, s_br, gl, nm("gate"))
    h_mid = _mm(merged, W["w_out"], acc=h, name=nm("out_proj"))
    xn2 = _rmsnorm_fwd(h_mid, p["norm2_w"], nm("norm2"))[0]
    u_ffn = _mm(xn2, W["w_ffn_in"], name=nm("ffn_in"))
    ffn_act = _swiglu_fwd(u_ffn, nm("swiglu"))
    h_out = _mm(ffn_act, W["w_ffn_out"], acc=h_mid, name=nm("ffn_out"))
    sv.update(xn=xn, xns=xns, qkvs=qkvs, z=z, xbc=xbc, dt_raw=dt_raw, gl=gl, ls=ls, attn_b=attn_b, attn_f=attn_f, u_conv=u_conv,
              act=act, dt=dt, dtT=dtT, dskip_e=dskip_e, y=y, states=states, ssm=ssm, a_br=a_br, s_br=s_br, merged=merged,
              h_mid=h_mid, xn2=xn2, u_ffn=u_ffn, ffn_act=ffn_act)
    return h_out, sv


def _layer_bwd(dh, sv, p, W, biases, cs, head_ones, l):
    T = dh.shape[0]
    nm = lambda s: f"{s}_l{l}"
    gr = {}
    dact = _mm(dh, W["w_ffn_out"], tb=True, name=nm("d_ffn_act"))
    gr["w_ffn_out"] = _mm(sv["ffn_act"], dh, ta=True, name=nm("g_ffn_out"))
    du = _swiglu_bwd(dact, sv["u_ffn"], nm("d_swiglu"))
    dxn2 = _mm(du, W["w_ffn_in"], tb=True, name=nm("d_xn2"))
    gr["w_ffn_in"] = _mm(sv["xn2"], du, ta=True, name=nm("g_ffn_in"))
    dh_mid, gr["norm2_w"] = _rmsnorm_bwd(dxn2, sv["h_mid"], p["norm2_w"], dh, nm("d_norm2"))
    dmerged = _mm(dh_mid, W["w_out"], tb=True, name=nm("d_merged"))
    gr["w_out"] = _mm(sv["merged"], dh_mid, ta=True, name=nm("g_out"))
    d_a, d_s, dgl = _gate_bwd(dmerged, sv["a_br"], sv["s_br"], sv["gl"], nm("d_gate"))
    dattn = _mm(d_a, W["w_attn_branch"], tb=True, name=nm("d_attn"))
    gr["w_attn_branch"] = _mm(sv["attn_b"], d_a, ta=True, name=nm("g_attn_branch"))
    dssm = _mm(d_s, W["w_ssm_branch"], tb=True, name=nm("d_ssm"))
    gr["w_ssm_branch"] = _mm(sv["ssm"], d_s, ta=True, name=nm("g_ssm_branch"))
    dy, dz, gr["ssm_norm_w"] = _ssm_norm_bwd(dssm, sv["y"], sv["z"], p["ssm_norm_w"], nm("d_ssm_norm"))
    dact_c, ddt_a, ddt_bT, da, daT, dskip = _ssd_bwd(dy, sv["act"], sv["dt"], sv["dtT"], p["a_log"], sv["dskip_e"], sv["states"], cs, nm("ssd_bwd"))
    gr["a_log"] = _a_log_grad(da, daT.T, p["a_log"], nm("g_a_log")).reshape(-1)
    gr["d_skip"] = dskip.reshape(-1)
    ddt_raw, ddt_bias = _dt_bwd(ddt_a, ddt_bT.T, sv["dt_raw"], p["dt_bias"], nm("d_dt"))
    gr["dt_bias"] = ddt_bias.reshape(-1)
    dxbc, gr["conv_w"], dconv_b = _conv_bwd(dact_c, sv["u_conv"], sv["xbc"], p["conv_w"], nm("d_conv"))
    gr["conv_b"] = dconv_b.reshape(-1)
    outs = _combine_bwd(dattn, sv["attn_f"], sv["ls"], head_ones, nm("d_combine"))
    wi = W["w_in"]
    dbias, dxn = [], None
    gqkv = [[None] * N_DIL for _ in range(3)]
    for g, (window, dil) in enumerate(DILATED_GROUPS):
        dq, dk, dv, db = _attn_bwd(sv["qkvs"][g], biases[g], sv["ls"][g], outs[2 * g], outs[2 * g + 1], dil, nm(f"attn_bwd_g{g}"))
        dbias.append(db)
        dxn = _mm_dil([dq, dk, dv], wi["qkv_t"][g], dil, dxn, nm(f"d_xn_qkv_g{g}"))
        for i, dseg in enumerate((dq, dk, dv)):
            gqkv[i][g] = _mm(sv["xns"][g], dseg, ta=True, name=nm(f"g_in_{'qkv'[i]}_g{g}"))
    parts = (("z", dz), ("xbc", dxbc), ("dt", ddt_raw), ("gl", dgl))
    gws = gqkv[0] + gqkv[1] + gqkv[2]
    for sname, dseg in parts:
        dxn = _mm(dseg, wi[sname], tb=True, acc=dxn, name=nm("d_xn_" + sname))
        gw = _mm(sv["xn"], dseg, ta=True, name=nm("g_in_" + sname))
        gws.append(gw[:, :N_SSM_HEADS] if sname == "dt" else gw)
    gr["w_in"] = jnp.concatenate(gws, axis=1)
    dh_in, gr["norm1_w"] = _rmsnorm_bwd(dxn, sv["h_in"], p["norm1_w"], dh_mid, nm("d_norm1"))
    return dh_in, gr, dbias


def _step_local(x, tgt, small, Wfull, rel_bias, final_norm_w):
    cs = _ssd_consts()
    head = np.repeat(np.arange(HEADS_PER_GROUP), HEAD_DIM)
    head_ones = jnp.asarray(head[:, None] == head[None, :], BF16)
    biases, onehots = [], []
    for g, (window, dil) in enumerate(DILATED_GROUPS):
        onehot, valid = _bias_consts(dil, window // dil)
        rel_g_t = rel_bias[:, g * HEADS_PER_GROUP:(g + 1) * HEADS_PER_GROUP].T
        b = _bias_gather(rel_g_t, onehot, valid, f"bias_gather_g{g}")
        biases.append(b.reshape(HEADS_PER_GROUP, ATTN_BLOCK, 2 * ATTN_BLOCK))
        onehots.append(onehot)
    h, saved = x, []
    for l in range(DEPTH):
        W = dict(Wfull[l])
        W["w_in"] = _split_w_in(W["w_in"])
        Wfull[l] = W
        h, sv = _layer_fwd(h, small[l], W, biases, cs, l)
        saved.append(sv)
    dh, g_final, loss = _loss_head(h, final_norm_w, tgt, "loss_head")
    grads = [None] * DEPTH
    dbias_tot = [None] * N_DIL
    for l in reversed(range(DEPTH)):
        dh, grads[l], dbias = _layer_bwd(dh, saved[l], small[l], Wfull[l], biases, cs, head_ones, l)
        for g in range(N_DIL):
            dbias_tot[g] = dbias[g] if dbias_tot[g] is None else dbias_tot[g] + dbias[g]
    d_rel = jnp.concatenate(
        [_bias_scatter(dbias_tot[g].reshape(HEADS_PER_GROUP, -1), onehots[g], f"bias_scatter_g{g}").T for g in range(N_DIL)], axis=1)
    return loss, dh, grads, d_rel, g_final


def _unshard(nm, g):
    _, rows, cols = g.shape
    if nm in COL_SHARDED:
        return g.transpose(1, 0, 2).reshape(rows, N_DEV * cols)
    return g.reshape(N_DEV * rows, cols)


def _shard(nm, w):
    rows, cols = w.shape
    if nm in COL_SHARDED:
        return w.reshape(rows, N_DEV, cols // N_DEV).transpose(1, 0, 2)
    return w.reshape(N_DEV, rows // N_DEV, cols)


SMALL_LAYER = (("norm1_w", 1024), ("conv_w", 12288), ("conv_b", 3072), ("dt_bias", 32), ("a_log", 32), ("d_skip", 32),
               ("ssm_norm_w", 2048), ("norm2_w", 1024))
SMALL_GLOBAL = (("rel_bias", 768), ("final_norm_w", 1024), ("loss", 1))


def _pad128(v):
    n = v.shape[0]
    return jnp.pad(v, (0, -n % LANES))


def _pack_small(per_layer, glob):
    parts = [_pad128(per_layer[l][nm].reshape(-1)) for l in range(DEPTH) for nm, _ in SMALL_LAYER]
    parts += [_pad128(glob[nm].reshape(-1)) for nm, _ in SMALL_GLOBAL]
    flat = jnp.concatenate(parts)
    flat = jnp.pad(flat, (0, -flat.shape[0] % (8 * LANES)))
    return flat.reshape(-1, LANES)


def _unpack_small(packed):
    flat = packed.reshape(-1)
    per_layer, glob, off = [dict() for _ in range(DEPTH)], {}, 0
    for l in range(DEPTH):
        for nm, n in SMALL_LAYER:
            per_layer[l][nm] = flat[off:off + n]
            off += n + (-n % LANES)
    for nm, n in SMALL_GLOBAL:
        glob[nm] = flat[off:off + n]
        off += n + (-n % LANES)
    return per_layer, glob


def kernel(x, norm1_w, w_in, conv_w, conv_b, dt_bias, a_log, d_skip, ssm_norm_w, w_attn_branch, w_ssm_branch, w_out, norm2_w, w_ffn_in, w_ffn_out, rel_bias, final_norm_w, loss_target, m_norm1_w, m_w_in, m_conv_w, m_conv_b, m_dt_bias, m_a_log, m_d_skip, m_ssm_norm_w, m_w_attn_branch, m_w_ssm_branch, m_w_out, m_norm2_w, m_w_ffn_in, m_w_ffn_out, m_rel_bias, m_final_norm_w, v_norm1_w, v_w_in, v_conv_w, v_conv_b, v_dt_bias, v_a_log, v_d_skip, v_ssm_norm_w, v_w_attn_branch, v_w_ssm_branch, v_w_out, v_norm2_w, v_w_ffn_in, v_w_ffn_out, v_rel_bias, v_final_norm_w):
    big = dict(w_in=w_in, w_attn_branch=w_attn_branch, w_ssm_branch=w_ssm_branch, w_out=w_out, w_ffn_in=w_ffn_in, w_ffn_out=w_ffn_out)
    big_m = dict(w_in=m_w_in, w_attn_branch=m_w_attn_branch, w_ssm_branch=m_w_ssm_branch, w_out=m_w_out, w_ffn_in=m_w_ffn_in, w_ffn_out=m_w_ffn_out)
    big_v = dict(w_in=v_w_in, w_attn_branch=v_w_attn_branch, w_ssm_branch=v_w_ssm_branch, w_out=v_w_out, w_ffn_in=v_w_ffn_in, w_ffn_out=v_w_ffn_out)
    sm = dict(norm1_w=norm1_w, conv_w=conv_w, conv_b=conv_b, dt_bias=dt_bias, a_log=a_log, d_skip=d_skip, ssm_norm_w=ssm_norm_w, norm2_w=norm2_w)
    sm_m = dict(norm1_w=m_norm1_w, conv_w=m_conv_w, conv_b=m_conv_b, dt_bias=m_dt_bias, a_log=m_a_log, d_skip=m_d_skip, ssm_norm_w=m_ssm_norm_w, norm2_w=m_norm2_w)
    sm_v = dict(norm1_w=v_norm1_w, conv_w=v_conv_w, conv_b=v_conv_b, dt_bias=v_dt_bias, a_log=v_a_log, d_skip=v_d_skip, ssm_norm_w=v_ssm_norm_w, norm2_w=v_norm2_w)
    me = 4 * lax.axis_index("x") + 2 * lax.axis_index("y") + lax.axis_index("c")

    gathered = _all_gather([big[nm][l].astype(BF16) for l in range(DEPTH) for nm in BIG], "all_gather_weights")
    Wfull = [{nm: _unshard(nm, gathered[l * len(BIG) + i]) for i, nm in enumerate(BIG)} for l in range(DEPTH)]

    conv_full = []
    for l in range(DEPTH):
        z = jnp.zeros((N_DEV, CONV_WIDTH, XBC_WIDTH // N_DEV), F32)
        conv_full.append(lax.dynamic_update_index_in_dim(z, conv_w[l], me, axis=0))
    cw = jnp.stack(conv_full).reshape(-1, LANES)
    cw = _all_reduce_small(cw, "gather_conv_w").reshape(DEPTH, N_DEV, CONV_WIDTH, XBC_WIDTH // N_DEV)
    cw = cw.transpose(0, 2, 1, 3).reshape(DEPTH, CONV_WIDTH, XBC_WIDTH)

    small = [{nm: (cw[l] if nm == "conv_w" else a[l]) for nm, a in sm.items()} for l in range(DEPTH)]
    loss, dx, grads, d_rel, g_final = _step_local(x[0], loss_target[0], small, Wfull, rel_bias, final_norm_w)

    mine = _reduce_scatter([_shard(nm, grads[l][nm]) for l in range(DEPTH) for nm in BIG], "reduce_scatter_grads")
    g_big = {nm: jnp.stack([mine[l * len(BIG) + i] for l in range(DEPTH)]) for i, nm in enumerate(BIG)}

    per_layer = [{nm: grads[l][nm] for nm, _ in SMALL_LAYER} for l in range(DEPTH)]
    packet = _pack_small(per_layer, dict(rel_bias=d_rel, final_norm_w=g_final, loss=loss[0, :1]))
    per_layer, glob = _unpack_small(_all_reduce_small(packet, "all_reduce_small"))
    g_small = {nm: jnp.stack([per_layer[l][nm] for l in range(DEPTH)]) for nm, _ in SMALL_LAYER}
    cwg = g_small["conv_w"].reshape(DEPTH, CONV_WIDTH, N_DEV, XBC_WIDTH // N_DEV)
    g_small["conv_w"] = lax.dynamic_index_in_dim(cwg, me, axis=2, keepdims=False)
    for nm in sm:
        g_small[nm] = g_small[nm].reshape(sm[nm].shape)
    g_rel = glob["rel_bias"].reshape(rel_bias.shape)
    g_fin = glob["final_norm_w"]
    loss_out = glob["loss"][0]

    def adam(w, g, m, v, name):
        shp = w.shape
        two = lambda a: a.reshape(-1, shp[-1]) if a.ndim > 1 else a.reshape(1, -1)
        d, nm_, nv = _adamw(two(w), two(g), two(m), two(v), name)
        return d.reshape(shp), nm_.reshape(shp), nv.reshape(shp)

    order = ["norm1_w", "w_in", "conv_w", "conv_b", "dt_bias", "a_log", "d_skip", "ssm_norm_w", "w_attn_branch", "w_ssm_branch",
             "w_out", "norm2_w", "w_ffn_in", "w_ffn_out", "rel_bias", "final_norm_w"]
    allw = {**big, **sm, "rel_bias": rel_bias, "final_norm_w": final_norm_w}
    allm = {**big_m, **sm_m, "rel_bias": m_rel_bias, "final_norm_w": m_final_norm_w}
    allv = {**big_v, **sm_v, "rel_bias": v_rel_bias, "final_norm_w": v_final_norm_w}
    allg = {**g_big, **g_small, "rel_bias": g_rel, "final_norm_w": g_fin}
    deltas, new_m, new_v = [], [], []
    for nm in order:
        d, a, b = adam(allw[nm], allg[nm], allm[nm], allv[nm], "adamw_" + nm)
        deltas.append(d)
        new_m.append(a)
        new_v.append(b)
    return (loss_out, dx[None], *[allg[nm] for nm in order], *deltas, *new_m, *new_v)
```

```python
import functools
import math

import numpy as np
import jax
import jax.numpy as jnp
from jax import lax
from jax.experimental import pallas as pl
from jax.experimental.pallas import tpu as pltpu

F32, BF16 = jnp.float32, jnp.bfloat16
S = jax.ShapeDtypeStruct
MESH = pl.DeviceIdType.MESH

D_MODEL = 1024
DEPTH = 2
HEAD_DIM = 64
DILATED_GROUPS = ((128, 1), (512, 4), (2048, 16))
N_DIL = 3
HEADS_PER_GROUP = 8
ATTN_WIDTH = 1536
ATTN_OUT_WIDTH = 512
ATTN_BLOCK = 128
N_REL_BUCKETS = 32
REL_MAX_DISTANCE = 2048
D_INNER = 2048
N_SSM_HEADS = 32
N_SSM_GROUPS = 4
D_STATE = 128
CONV_WIDTH = 4
SSD_CHUNK = 128
XBC_WIDTH = 3072
D_FF = 2816
EPS = 1e-6
ADAM_LR, ADAM_B1, ADAM_B2, ADAM_EPS, ADAM_WD, ADAM_STEP = 0.001, 0.9, 0.999, 1e-08, 0.01, 10

N_DEV = 8
LANES = 128
VMEM_LIMIT = 56 * 1024 * 1024
ROW_TILES_BYTES = 36 * 1024 * 1024
NEG = -1e30
BIG = ("w_in", "w_attn_branch", "w_ssm_branch", "w_out", "w_ffn_in", "w_ffn_out")
COL_SHARDED = ("w_in", "w_attn_branch", "w_ffn_in")

NT = (((1,), (1,)), ((), ()))
TN = (((0,), (0,)), ((), ()))


def _cparams(sem=None):
    return pltpu.CompilerParams(dimension_semantics=sem, vmem_limit_bytes=VMEM_LIMIT)


def _pick(n, target, mult=LANES):
    best = None
    for t in range(mult, min(n, target) + 1, mult):
        if n % t == 0:
            best = t
    return best or n


def _silu(x):
    return x * jax.nn.sigmoid(x)


def _dsilu(x):
    s = jax.nn.sigmoid(x)
    return s * (1.0 + x * (1.0 - s))


def _split2(x):
    hi = x.astype(BF16)
    lo = (x - hi.astype(F32)).astype(BF16)
    return hi, lo


def _split3(x):
    x1 = x.astype(BF16)
    r1 = x - x1.astype(F32)
    x2 = r1.astype(BF16)
    x3 = (r1 - x2.astype(F32)).astype(BF16)
    return x1, x2, x3


def _dotx_r(x, m, parts=3):
    xs = _split3(x) if parts == 3 else _split2(x)
    out = jnp.dot(xs[0], m, preferred_element_type=F32)
    for xi in xs[1:]:
        out = out + jnp.dot(xi, m, preferred_element_type=F32)
    return out


def _dotx_l(m, x, parts=3):
    xs = _split3(x) if parts == 3 else _split2(x)
    out = jnp.dot(m, xs[0], preferred_element_type=F32)
    for xi in xs[1:]:
        out = out + jnp.dot(m, xi, preferred_element_type=F32)
    return out


def _mm(a, b, *, ta=False, tb=False, out_dtype=F32, acc=None, name, tm=1536, tn=1536, tk=1536):
    M, K = (a.shape[1], a.shape[0]) if ta else a.shape
    N = b.shape[0] if tb else b.shape[1]
    tm, tn, tk = _pick(M, tm), _pick(N, tn), _pick(K, tk)
    nk = K // tk
    dims = (((0 if ta else 1,), (1 if tb else 0,)), ((), ()))
    has_acc = acc is not None

    def body(*refs):
        if has_acc:
            a_ref, b_ref, c_ref, o_ref, acc_ref = refs
        else:
            a_ref, b_ref, o_ref, acc_ref = refs
        k = pl.program_id(2)
        part = lax.dot_general(a_ref[...].astype(BF16), b_ref[...].astype(BF16), dims, preferred_element_type=F32)

        @pl.when(k == 0)
        def _():
            acc_ref[...] = part + c_ref[...].astype(F32) if has_acc else part

        @pl.when(k > 0)
        def _():
            acc_ref[...] += part

        @pl.when(k == nk - 1)
        def _():
            o_ref[...] = acc_ref[...].astype(o_ref.dtype)

    a_spec = pl.BlockSpec((tk, tm), lambda i, j, k: (k, i)) if ta else pl.BlockSpec((tm, tk), lambda i, j, k: (i, k))
    b_spec = pl.BlockSpec((tn, tk), lambda i, j, k: (j, k)) if tb else pl.BlockSpec((tk, tn), lambda i, j, k: (k, j))
    o_spec = pl.BlockSpec((tm, tn), lambda i, j, k: (i, j))
    in_specs, args = [a_spec, b_spec], [a, b]
    if has_acc:
        in_specs.append(o_spec)
        args.append(acc)
    return pl.pallas_call(
        body, grid=(M // tm, N // tn, nk), in_specs=in_specs, out_specs=o_spec,
        out_shape=S((M, N), out_dtype), scratch_shapes=[pltpu.VMEM((tm, tn), F32)],
        compiler_params=_cparams(("parallel", "parallel", "arbitrary")), name=name)(*args)


def _mm_dil(a_list, b_list, d, acc, name, tm=1024):
    T, K = a_list[0].shape
    N = b_list[0].shape[0]
    tm, tn = min(tm, T), _pick(N, 1024)
    na = len(a_list)
    has_acc = acc is not None

    def body(*refs):
        a_refs, b_refs, rest = refs[:na], refs[na:2 * na], refs[2 * na:]
        c_ref = rest[0] if has_acc else None
        o_ref, scr = rest[-2], rest[-1]
        out = c_ref[...] if has_acc else None
        for a_ref, b_ref in zip(a_refs, b_refs):
            a_tok = _dil_to_tok(scr, a_ref, d).astype(BF16) if d > 1 else a_ref[...]
            part = lax.dot_general(a_tok, b_ref[...], NT, preferred_element_type=F32)
            out = part if out is None else out + part
        o_ref[...] = out

    if d > 1:
        a_spec = pl.BlockSpec((d, tm // d, K), lambda i, j: (0, i, 0))
        a_args = [a.reshape(d, T // d, K) for a in a_list]
    else:
        a_spec = pl.BlockSpec((tm, K), lambda i, j: (i, 0))
        a_args = list(a_list)
    o_spec = pl.BlockSpec((tm, tn), lambda i, j: (i, j))
    in_specs = [a_spec] * na + [pl.BlockSpec((tn, K), lambda i, j: (j, 0))] * na + ([o_spec] if has_acc else [])
    return pl.pallas_call(
        body, grid=(T // tm, N // tn), in_specs=in_specs, out_specs=o_spec, out_shape=S((T, N), F32),
        scratch_shapes=[pltpu.VMEM((K // LANES, tm, LANES), F32)],
        compiler_params=_cparams(("parallel", "parallel")), name=name)(*a_args, *b_list, *([acc] if has_acc else []))


def _dil_to_tok(scr, ref, d):
    n, C = ref.shape[1], ref.shape[2]
    for r in range(d):
        v = ref[r].astype(F32)
        for cb in range(C // LANES):
            scr.at[cb][pl.ds(r, n, stride=d), :] = v[:, cb * LANES:(cb + 1) * LANES]
    return jnp.concatenate([scr[cb] for cb in range(C // LANES)], axis=1)


def _tok_to_dil(scr, val, ref, d):
    n, C = ref.shape[1], ref.shape[2]
    for cb in range(C // LANES):
        scr[cb] = val[:, cb * LANES:(cb + 1) * LANES].astype(F32)
    for r in range(d):
        ref[r] = jnp.concatenate([scr.at[cb][pl.ds(r, n, stride=d), :] for cb in range(C // LANES)], axis=1).astype(ref.dtype)


def _rowwise(fn, rows, fulls, outs, accs=(), *, tm, name):
    rows = [r if isinstance(r, tuple) else (r, r.shape[1], 0) for r in rows]
    first = rows[0]
    T = (first[1] if isinstance(first[0], str) else first[0]).shape[0]
    widest = max([r[1].shape[1] if isinstance(r[0], str) else r[1] for r in rows] + [o[0] for o in outs])
    tm = min(tm, max(8, ROW_TILES_BYTES // (2 * (len(rows) + len(outs))) // (4 * widest) // 8 * 8))
    tm = T if T <= tm else _pick(T, tm, 8)
    nr, nf, no, na = len(rows), len(fulls), len(outs), len(accs)
    dil_in = [i for i, r in enumerate(rows) if isinstance(r[0], str) and r[2] > 1]
    dil_out = [i for i, o in enumerate(outs) if len(o) == 3 and o[2] > 1]
    scr_cols = [rows[i][1].shape[1] for i in dil_in] + [outs[i][0] for i in dil_out]

    def body(*refs):
        r, f = refs[:nr], refs[nr:nr + nf]
        o, a = refs[nr + nf:nr + nf + no], refs[nr + nf + no:nr + nf + no + na]
        scr = refs[nr + nf + no + na:]
        tiles = []
        for i, x in enumerate(r):
            if i in dil_in:
                tiles.append(_dil_to_tok(scr[dil_in.index(i)], x, rows[i][2]))
            else:
                tiles.append(x[...].astype(F32))
        ro, ra = fn(*tiles, *[x[...] for x in f])
        for i, (ref, val) in enumerate(zip(o, ro)):
            if i in dil_out:
                _tok_to_dil(scr[len(dil_in) + dil_out.index(i)], val, ref, outs[i][2])
            else:
                ref[...] = val.astype(ref.dtype)
        if na:
            @pl.when(pl.program_id(0) == 0)
            def _():
                for ref in a:
                    ref[...] = jnp.zeros_like(ref)
            for ref, val in zip(a, ra):
                ref[...] += val

    in_specs, args = [], []
    for i, rr in enumerate(rows):
        if isinstance(rr[0], str):
            arr, d = rr[1], rr[2]
            if d > 1:
                in_specs.append(pl.BlockSpec((d, tm // d, arr.shape[1]), lambda i: (0, i, 0)))
                args.append(arr.reshape(d, T // d, arr.shape[1]))
            else:
                in_specs.append(pl.BlockSpec((tm, arr.shape[1]), lambda i: (i, 0)))
                args.append(arr)
        else:
            in_specs.append(pl.BlockSpec((tm, rr[1]), functools.partial(lambda i, cb: (i, cb), cb=rr[2])))
            args.append(rr[0])
    in_specs += [pl.BlockSpec(f.shape, lambda i: (0, 0)) for f in fulls]
    out_specs, out_shape = [], []
    for i, oo in enumerate(outs):
        if i in dil_out:
            d = oo[2]
            out_specs.append(pl.BlockSpec((d, tm // d, oo[0]), lambda i: (0, i, 0)))
            out_shape.append(S((d, T // d, oo[0]), oo[1]))
        else:
            out_specs.append(pl.BlockSpec((tm, oo[0]), lambda i: (i, 0)))
            out_shape.append(S((T, oo[0]), oo[1]))
    out_specs += [pl.BlockSpec(sh, lambda i: (0, 0)) for sh in accs]
    out_shape += [S(sh, F32) for sh in accs]
    res = pl.pallas_call(
        body, grid=(T // tm,), in_specs=in_specs, out_specs=out_specs, out_shape=out_shape,
        scratch_shapes=[pltpu.VMEM((c // LANES, tm, LANES), F32) for c in scr_cols],
        compiler_params=_cparams(("arbitrary",)), name=name)(*args, *fulls)
    return [x.reshape(T, x.shape[2]) if i in dil_out else x for i, x in enumerate(res)]


def _rmsnorm_fwd(h, w, name, dils=()):
    D = h.shape[1]

    def fn(h, w):
        r = lax.rsqrt(jnp.mean(h * h, axis=-1, keepdims=True) + EPS)
        xn = h * r * w
        return [xn] * (1 + len(dils)), []
    return _rowwise(fn, [h], [w.reshape(1, -1)], [(D, BF16)] + [(D, BF16, d) for d in dils], tm=512, name=name)


def _rmsnorm_bwd(dxn, h, w, dres, name):
    def fn(dxn, h, dres, w):
        r = lax.rsqrt(jnp.mean(h * h, axis=-1, keepdims=True) + EPS)
        n = h * r
        dn = dxn * w
        dh = r * (dn - n * jnp.mean(dn * n, axis=-1, keepdims=True)) + dres
        return [dh], [jnp.sum(dxn * n, axis=0, keepdims=True)]
    D = h.shape[1]
    return _rowwise(fn, [dxn, h, dres], [w.reshape(1, -1)], [(D, F32)], [(1, D)], tm=256, name=name)


def _loss_head(h, w, tgt, name):
    D = h.shape[1]

    def fn(h, tgt, w):
        r = lax.rsqrt(jnp.mean(h * h, axis=-1, keepdims=True) + EPS)
        n = h * r
        e = n * w - tgt
        row_loss = 0.5 * jnp.mean(e * e, axis=-1, keepdims=True)
        dy = e * (1.0 / D)
        dn = dy * w
        dh = r * (dn - n * jnp.mean(dn * n, axis=-1, keepdims=True))
        return [dh], [jnp.sum(dy * n, axis=0, keepdims=True), jnp.broadcast_to(jnp.sum(row_loss, axis=0, keepdims=True), (1, LANES))]
    return _rowwise(fn, [h, tgt], [w.reshape(1, -1)], [(D, F32)], [(1, D), (1, LANES)], tm=256, name=name)


def _combine_fwd(os_, ls, name):
    def fn(o0, o1, o2, l0, l1, l2):
        m = jnp.maximum(jnp.maximum(l0, l1), l2)
        e0, e1, e2 = jnp.exp(l0 - m), jnp.exp(l1 - m), jnp.exp(l2 - m)
        attn = (e0 * o0 + e1 * o1 + e2 * o2) / (e0 + e1 + e2)
        return [attn, attn], []
    dil = [("dil", t, d) for t, (_, d) in zip(list(os_) + list(ls), DILATED_GROUPS * 2)]
    return _rowwise(fn, dil, [], [(ATTN_OUT_WIDTH, BF16), (ATTN_OUT_WIDTH, F32)], tm=512, name=name)


def _combine_bwd(dattn, attn, ls, head_ones, name):
    def fn(dattn, attn, l0, l1, l2, ones):
        m = jnp.maximum(jnp.maximum(l0, l1), l2)
        e0, e1, e2 = jnp.exp(l0 - m), jnp.exp(l1 - m), jnp.exp(l2 - m)
        inv = 1.0 / (e0 + e1 + e2)
        t = _dotx_r(dattn * attn, ones, parts=2)
        outs = []
        for e in (e0, e1, e2):
            al = e * inv
            outs += [al * dattn, al * t]
        return outs, []
    W = ATTN_OUT_WIDTH
    dil = [("dil", t, d) for t, (_, d) in zip(ls, DILATED_GROUPS)]
    outs = [(W, F32, d) for _, d in DILATED_GROUPS for _ in range(2)]
    return _rowwise(fn, [dattn, attn] + dil, [head_ones], outs, tm=512, name=name)


def _dt_fwd(dt_raw, dt_bias, name):
    def fn(raw, b):
        z = raw[:, :N_SSM_HEADS] + b
        return [jnp.maximum(z, 0.0) + jnp.log(1.0 + jnp.exp(-jnp.abs(z)))], []
    return _rowwise(fn, [dt_raw], [dt_bias.reshape(1, -1)], [(N_SSM_HEADS, F32)], tm=1024, name=name)[0]


def _dt_bwd(ddt_a, ddt_b, dt_raw, dt_bias, name):
    def fn(da, db, raw, b):
        g = (da + db) * jax.nn.sigmoid(raw[:, :N_SSM_HEADS] + b)
        pad = jnp.zeros((g.shape[0], LANES - N_SSM_HEADS), F32)
        return [jnp.concatenate([g, pad], axis=1)], [jnp.sum(g, axis=0, keepdims=True)]
    return _rowwise(fn, [ddt_a, ddt_b, dt_raw], [dt_bias.reshape(1, -1)], [(LANES, BF16)], [(1, N_SSM_HEADS)], tm=1024, name=name)


def _ssm_norm_fwd(y, z, w, name):
    G = D_INNER // N_SSM_GROUPS

    def fn(y, z, w):
        yg = y * _silu(z)
        outs = []
        for g in range(N_SSM_GROUPS):
            t = yg[:, g * G:(g + 1) * G]
            outs.append(t * lax.rsqrt(jnp.mean(t * t, axis=-1, keepdims=True) + EPS))
        return [jnp.concatenate(outs, axis=1) * w], []
    return _rowwise(fn, [y, z], [w.reshape(1, -1)], [(D_INNER, BF16)], tm=256, name=name)[0]


def _ssm_norm_bwd(dssm, y, z, w, name):
    G = D_INNER // N_SSM_GROUPS

    def fn(dssm, y, z, w):
        sz = _silu(z)
        yg = y * sz
        dn = dssm * w
        ns, dygs = [], []
        for g in range(N_SSM_GROUPS):
            t = yg[:, g * G:(g + 1) * G]
            r = lax.rsqrt(jnp.mean(t * t, axis=-1, keepdims=True) + EPS)
            n = t * r
            d = dn[:, g * G:(g + 1) * G]
            dygs.append(r * (d - n * jnp.mean(d * n, axis=-1, keepdims=True)))
            ns.append(n)
        n, dyg = jnp.concatenate(ns, axis=1), jnp.concatenate(dygs, axis=1)
        return [dyg * sz, dyg * y * _dsilu(z)], [jnp.sum(dssm * n, axis=0, keepdims=True)]
    return _rowwise(fn, [dssm, y, z], [w.reshape(1, -1)], [(D_INNER, F32), (D_INNER, BF16)], [(1, D_INNER)], tm=256, name=name)


def _gate_fwd(a, sb, gl, name):
    def fn(a, sb, gl):
        g = jax.nn.sigmoid(gl)
        return [g[:, :D_MODEL] * a + g[:, D_MODEL:] * sb], []
    return _rowwise(fn, [a, sb, gl], [], [(D_MODEL, BF16)], tm=512, name=name)[0]


def _gate_bwd(dm, a, sb, gl, name):
    def fn(dm, a, sb, gl):
        g = jax.nn.sigmoid(gl)
        g0, g1 = g[:, :D_MODEL], g[:, D_MODEL:]
        dgl = jnp.concatenate([dm * a * g0 * (1.0 - g0), dm * sb * g1 * (1.0 - g1)], axis=1)
        return [g0 * dm, g1 * dm, dgl], []
    return _rowwise(fn, [dm, a, sb, gl], [], [(D_MODEL, BF16), (D_MODEL, BF16), (2 * D_MODEL, BF16)], tm=512, name=name)


def _swiglu_fwd(u, name):
    def fn(u):
        return [_silu(u[:, :D_FF]) * u[:, D_FF:]], []
    return _rowwise(fn, [u], [], [(D_FF, BF16)], tm=256, name=name)[0]


def _swiglu_bwd(dact, u, name):
    def fn(dact, u):
        gate, up = u[:, :D_FF], u[:, D_FF:]
        return [jnp.concatenate([dact * up * _dsilu(gate), dact * _silu(gate)], axis=1)], []
    return _rowwise(fn, [dact, u], [], [(2 * D_FF, BF16)], tm=256, name=name)[0]


def _adamw(w, g, m, v, name):
    c1 = 1.0 - ADAM_B1 ** ADAM_STEP
    c2 = 1.0 - ADAM_B2 ** ADAM_STEP

    def fn(w, g, m, v):
        m = ADAM_B1 * m + (1.0 - ADAM_B1) * g
        v = ADAM_B2 * v + (1.0 - ADAM_B2) * (g * g)
        delta = -ADAM_LR * ((m / c1) / (jnp.sqrt(v / c2) + ADAM_EPS) + ADAM_WD * w)
        return [delta, m, v], []
    C = w.shape[1]
    return _rowwise(fn, [w, g, m, v], [], [(C, F32)] * 3, tm=256, name=name)


def _bias_consts(dilation, n_steps):
    qi = np.arange(ATTN_BLOCK)[:, None]
    kj = np.arange(2 * ATTN_BLOCK)[None, :]
    steps = qi + ATTN_BLOCK - kj
    valid = (steps >= 0) & (steps <= n_steps)
    dist = jnp.asarray(np.clip(steps, 0, n_steps) * dilation, jnp.int32)
    max_exact = N_REL_BUCKETS // 2
    d_f = jnp.maximum(dist, 1).astype(F32)
    large = max_exact + (jnp.log(d_f / max_exact) / math.log(REL_MAX_DISTANCE / max_exact)
                         * (N_REL_BUCKETS - max_exact)).astype(jnp.int32)
    large = jnp.minimum(large, N_REL_BUCKETS - 1)
    bucket = jnp.where(dist < max_exact, dist, large).reshape(-1)
    onehot = (bucket[None, :] == jnp.arange(N_REL_BUCKETS)[:, None]).astype(F32)
    return onehot, jnp.asarray(valid.reshape(1, -1), F32)


def _bias_gather(rel_g_t, onehot, valid, name):
    def body(r_ref, oh_ref, v_ref, o_ref):
        b = jnp.dot(r_ref[...], oh_ref[...], preferred_element_type=F32, precision=lax.Precision.HIGHEST)
        o_ref[...] = jnp.where(v_ref[...] > 0.5, b, NEG)
    return pl.pallas_call(body, out_shape=S((HEADS_PER_GROUP, onehot.shape[1]), F32), compiler_params=_cparams(), name=name)(rel_g_t, onehot, valid)


def _bias_scatter(dbias, onehot, name):
    def body(d_ref, oh_ref, o_ref):
        o_ref[...] = lax.dot_general(d_ref[...], oh_ref[...], NT, preferred_element_type=F32, precision=lax.Precision.HIGHEST)
    return pl.pallas_call(body, out_shape=S((HEADS_PER_GROUP, N_REL_BUCKETS), F32), compiler_params=_cparams(), name=name)(dbias, onehot)


ATTN_QB_FWD, ATTN_QB_BWD = 1, 4


def _attn_tiles(T, d, qb):
    seg = T // d
    nqb = min(qb, seg // ATTN_BLOCK)
    tq = nqb * ATTN_BLOCK
    return seg, nqb, tq, seg // tq


def _attn_fwd(qkv, bias, d, name):
    T = qkv.shape[0]
    seg, nqb, tq, ns = _attn_tiles(T, d, ATTN_QB_FWD)
    W = ATTN_OUT_WIDTH
    scale = HEAD_DIM ** -0.5

    def body(q_ref, kh_ref, kc_ref, vh_ref, vc_ref, b_ref, o_ref, l_ref):
        n = pl.program_id(1)
        qv = q_ref[...]
        kk = jnp.concatenate([kh_ref[...], kc_ref[...]], axis=0)
        vv = jnp.concatenate([vh_ref[...], vc_ref[...]], axis=0)
        col = lax.broadcasted_iota(jnp.int32, (ATTN_BLOCK, 2 * ATTN_BLOCK), 1)
        kill = jnp.logical_and(n == 0, col < ATTN_BLOCK)
        for j in range(nqb):
            rows = slice(j * ATTN_BLOCK, (j + 1) * ATTN_BLOCK)
            keys = slice(j * ATTN_BLOCK, (j + 2) * ATTN_BLOCK)
            for h in range(HEADS_PER_GROUP):
                sl = slice(h * HEAD_DIM, (h + 1) * HEAD_DIM)
                s = lax.dot_general(qv[rows, sl], kk[keys, sl], NT, preferred_element_type=F32) * scale + b_ref[h]
                if j == 0:
                    s = jnp.where(kill, NEG, s)
                m = jnp.max(s, axis=-1, keepdims=True)
                p = jnp.exp(s - m)
                den = jnp.sum(p, axis=-1, keepdims=True)
                o = jnp.dot(p.astype(BF16), vv[keys, sl], preferred_element_type=F32) / den
                o_ref[rows, sl] = o
                l_ref[rows, sl] = jnp.broadcast_to(m + jnp.log(den), (ATTN_BLOCK, HEAD_DIM))

    def cur(c):
        return pl.BlockSpec((tq, W), lambda r, n: (r * ns + n, c))

    def halo(c):
        return pl.BlockSpec((ATTN_BLOCK, W), lambda r, n: (jnp.maximum((r * ns + n) * nqb - 1, 0), c))

    return pl.pallas_call(
        body, grid=(d, ns),
        in_specs=[cur(0), halo(1), cur(1), halo(2), cur(2), pl.BlockSpec(bias.shape, lambda r, n: (0, 0, 0))],
        out_specs=[cur(0), cur(0)], out_shape=[S((T, W), F32)] * 2,
        compiler_params=_cparams(("parallel", "arbitrary")), name=name)(qkv, qkv, qkv, qkv, qkv, bias)


def _attn_bwd(qkv, bias, lse, do, dd, d, name):
    T = qkv.shape[0]
    seg, nqb, tq, ns = _attn_tiles(T, d, ATTN_QB_BWD)
    W = ATTN_OUT_WIDTH
    B = ATTN_BLOCK
    scale = HEAD_DIM ** -0.5

    def body(q_ref, kh_ref, kc_ref, vh_ref, vc_ref, b_ref, l_ref, do_ref, dd_ref, dq_ref, dk_ref, dv_ref, db_ref, pk_ref, pv_ref):
        r, n = pl.program_id(0), pl.program_id(1)

        @pl.when(jnp.logical_and(r == 0, n == 0))
        def _():
            db_ref[...] = jnp.zeros_like(db_ref)

        @pl.when(n == 0)
        def _():
            pk_ref[...] = jnp.zeros_like(pk_ref)
            pv_ref[...] = jnp.zeros_like(pv_ref)

        @pl.when(n < ns)
        def _():
            qv = q_ref[...]
            kk = jnp.concatenate([kh_ref[...], kc_ref[...]], axis=0)
            vv = jnp.concatenate([vh_ref[...], vc_ref[...]], axis=0)
            lse_v, do_v, dd_v = l_ref[...], do_ref[...], dd_ref[...]
            col = lax.broadcasted_iota(jnp.int32, (B, 2 * B), 1)
            kill = jnp.logical_and(n == 0, col < B)
            dqs = [[None] * HEADS_PER_GROUP for _ in range(nqb)]
            dks = [[None] * HEADS_PER_GROUP for _ in range(nqb)]
            dvs = [[None] * HEADS_PER_GROUP for _ in range(nqb)]
            for h in range(HEADS_PER_GROUP):
                sl = slice(h * HEAD_DIM, (h + 1) * HEAD_DIM)
                c0 = slice(h * HEAD_DIM, h * HEAD_DIM + 1)
                dsum = None
                for j in range(nqb):
                    rows = slice(j * B, (j + 1) * B)
                    keys = slice(j * B, (j + 2) * B)
                    qh, kh, vh = qv[rows, sl], kk[keys, sl], vv[keys, sl]
                    s = lax.dot_general(qh, kh, NT, preferred_element_type=F32) * scale + b_ref[h]
                    if j == 0:
                        s = jnp.where(kill, NEG, s)
                    p = jnp.exp(s - lse_v[rows, c0])
                    doh = do_v[rows, sl].astype(BF16)
                    dvs[j][h] = lax.dot_general(p.astype(BF16), doh, TN, preferred_element_type=F32)
                    dp = lax.dot_general(doh, vh, NT, preferred_element_type=F32)
                    ds = p * (dp - dd_v[rows, c0])
                    dsum = ds if dsum is None else dsum + ds
                    dsb = (ds * scale).astype(BF16)
                    dqs[j][h] = jnp.dot(dsb, kh, preferred_element_type=F32)
                    dks[j][h] = lax.dot_general(dsb, qh, TN, preferred_element_type=F32)
                db_ref[h] += dsum
            dq_ref[...] = jnp.concatenate([jnp.concatenate(dqs[j], axis=1) for j in range(nqb)], axis=0).astype(dq_ref.dtype)
            for parts, out_ref, pend in ((dks, dk_ref, pk_ref), (dvs, dv_ref, pv_ref)):
                full = [jnp.concatenate(parts[j], axis=1) for j in range(nqb)]
                if tq > B:
                    out_ref[:tq - B] = pend[:tq - B].astype(out_ref.dtype)
                out_ref[tq - B:] = (pend[tq - B:] + full[0][:B]).astype(out_ref.dtype)
                for j in range(nqb - 1):
                    pend[j * B:(j + 1) * B] = full[j][B:] + full[j + 1][:B]
                pend[tq - B:] = full[nqb - 1][B:]

        @pl.when(n == ns)
        def _():
            dk_ref[...] = pk_ref[...].astype(dk_ref.dtype)
            dv_ref[...] = pv_ref[...].astype(dv_ref.dtype)

    def cur(c):
        return pl.BlockSpec((tq, W), lambda r, n: (r * ns + jnp.minimum(n, ns - 1), c))

    def halo(c):
        return pl.BlockSpec((B, W), lambda r, n: (jnp.maximum((r * ns + jnp.minimum(n, ns - 1)) * nqb - 1, 0), c))

    late = pl.BlockSpec((tq, W), lambda r, n: (r * ns + jnp.clip(n - 1, 0, ns - 1), 0))
    bspec = pl.BlockSpec(bias.shape, lambda r, n: (0, 0, 0))
    return pl.pallas_call(
        body, grid=(d, ns + 1),
        in_specs=[cur(0), halo(1), cur(1), halo(2), cur(2), bspec, cur(0), cur(0), cur(0)],
        out_specs=[cur(0), late, late, bspec],
        out_shape=[S((T, W), BF16)] * 3 + [S(bias.shape, F32)],
        scratch_shapes=[pltpu.VMEM((tq, W), F32), pltpu.VMEM((tq, W), F32)],
        compiler_params=_cparams(("arbitrary", "arbitrary")), name=name,
    )(qkv, qkv, qkv, qkv, qkv, bias, lse, do, dd)


CONV_TM, CONV_TC = 512, 1024


def _shift_down(x, halo8, s, row8):
    xr = pltpu.roll(x, s, 0)
    first = jnp.where(row8 < s, pltpu.roll(halo8, s, 0), xr[:8])
    return jnp.concatenate([first, xr[8:]], axis=0)


def _shift_up(x, halo8, s, row8):
    n = x.shape[0]
    xr = pltpu.roll(x, n - s, 0)
    last = jnp.where(row8 >= 8 - s, pltpu.roll(halo8, 8 - s, 0), xr[n - 8:])
    return jnp.concatenate([xr[:n - 8], last], axis=0)


def _conv_fwd(x, w, b, name):
    T, C = x.shape
    tm, tc = min(CONV_TM, T), CONV_TC

    def body(x_ref, p_ref, w_ref, b_ref, u_ref, a_ref):
        ti = pl.program_id(1)
        xv = x_ref[...]
        p8 = jnp.where(ti == 0, 0.0, p_ref[...])
        wv = w_ref[...]
        row8 = lax.broadcasted_iota(jnp.int32, (8, tc), 0)
        u = xv * wv[3:4] + b_ref[...]
        for s in (1, 2, 3):
            u = u + _shift_down(xv, p8, s, row8) * wv[3 - s:4 - s]
        u_ref[...] = u
        a_ref[...] = _silu(u)

    cur = pl.BlockSpec((tm, tc), lambda cj, ti: (ti, cj))
    halo = pl.BlockSpec((8, tc), lambda cj, ti: (jnp.maximum(ti * (tm // 8) - 1, 0), cj))
    return pl.pallas_call(
        body, grid=(C // tc, T // tm),
        in_specs=[cur, halo, pl.BlockSpec((CONV_WIDTH, tc), lambda cj, ti: (0, cj)), pl.BlockSpec((1, tc), lambda cj, ti: (0, cj))],
        out_specs=[cur, cur], out_shape=[S((T, C), F32)] * 2,
        compiler_params=_cparams(("parallel", "arbitrary")), name=name)(x, x, w, b)


def _conv_bwd(dact, u, x, w, name):
    T, C = x.shape
    tm, tc = min(CONV_TM, T), CONV_TC
    nt = T // tm

    def body(d_ref, dn_ref, u_ref, un_ref, x_ref, w_ref, dx_ref, dw_ref, db_ref):
        ti = pl.program_id(1)

        @pl.when(ti == 0)
        def _():
            dw_ref[...] = jnp.zeros_like(dw_ref)
            db_ref[...] = jnp.zeros_like(db_ref)

        du = d_ref[...] * _dsilu(u_ref[...])
        dun = jnp.where(ti == nt - 1, 0.0, dn_ref[...] * _dsilu(un_ref[...]))
        xv = x_ref[...]
        wv = w_ref[...]
        row8 = lax.broadcasted_iota(jnp.int32, (8, tc), 0)
        dx = du * wv[3:4]
        dws = [None] * CONV_WIDTH
        dws[3] = jnp.sum(du * xv, axis=0, keepdims=True)
        for s in (1, 2, 3):
            up = _shift_up(du, dun, s, row8)
            dx = dx + up * wv[3 - s:4 - s]
            dws[3 - s] = jnp.sum(up * xv, axis=0, keepdims=True)
        dx_ref[...] = dx.astype(dx_ref.dtype)
        dw_ref[...] += jnp.concatenate(dws, axis=0)
        db_ref[...] += jnp.sum(du, axis=0, keepdims=True)

    cur = pl.BlockSpec((tm, tc), lambda cj, ti: (ti, cj))
    nxt = pl.BlockSpec((8, tc), lambda cj, ti: (jnp.minimum((ti + 1) * (tm // 8), T // 8 - 1), cj))
    return pl.pallas_call(
        body, grid=(C // tc, nt),
        in_specs=[cur, nxt, cur, nxt, cur, pl.BlockSpec((CONV_WIDTH, tc), lambda cj, ti: (0, cj))],
        out_specs=[cur, pl.BlockSpec((CONV_WIDTH, tc), lambda cj, ti: (0, cj)), pl.BlockSpec((1, tc), lambda cj, ti: (0, cj))],
        out_shape=[S((T, C), BF16), S((CONV_WIDTH, C), F32), S((1, C), F32)],
        compiler_params=_cparams(("parallel", "arbitrary")), name=name)(dact, dact, u, u, x, w)


def _ssd_consts():
    i = np.arange(SSD_CHUNK)
    tril = (i[None, :] <= i[:, None]).astype(np.float32)
    trils = (i[None, :] < i[:, None]).astype(np.float32)
    head = np.repeat(np.arange(N_SSM_HEADS), D_INNER // N_SSM_HEADS)
    et = (head[None, :] == np.arange(N_SSM_HEADS)[:, None]).astype(np.float32)
    c = lambda a: jnp.asarray(a, BF16)
    return dict(tril=c(tril), triu=c(tril.T), trils=c(trils), trius=c(trils.T), et=c(et), e=c(et.T))


def _ssd_common(act_ref, dt_ref, dtT_ref, al_ref, alT_ref, tril_ref, triu_ref, et_ref):
    a_row = -jnp.exp(al_ref[...])
    a_col = -jnp.exp(alT_ref[...])
    dt, dtT = dt_ref[...], dtT_ref[...]
    la = _dotx_l(tril_ref[...], dt * a_row)
    laT = _dotx_r(dtT * a_col, triu_ref[...])
    et = et_ref[...]
    la_e = _dotx_r(la, et)
    dt_e = _dotx_r(dt, et, parts=2)
    x = act_ref[:, :D_INNER]
    xdt = x * dt_e
    la_q = la_e[SSD_CHUNK - 1:SSD_CHUNK, :]
    return a_row, a_col, dt, dtT, la, laT, la_e, dt_e, x, xdt, la_q


def _decay(la, laT, h, causal):
    seg = la[:, h:h + 1] - laT[h:h + 1, :]
    return jnp.exp(jnp.where(causal, seg, NEG))


def _ssd_fwd(act, dt, dtT, alog, dskip_e, cs, name):
    T = act.shape[0]
    nc = T // SSD_CHUNK
    Q, G, GW = SSD_CHUNK, N_SSM_GROUPS, D_INNER // N_SSM_GROUPS

    def body(act_ref, dt_ref, dtT_ref, al_ref, alT_ref, dsk_ref, tril_ref, triu_ref, et_ref, y_ref, st_ref, scr):
        @pl.when(pl.program_id(0) == 0)
        def _():
            scr[...] = jnp.zeros_like(scr)
        st_ref[0] = scr[...]
        a_row, a_col, dtv, dtTv, la, laT, la_e, dt_e, x, xdt, la_q = _ssd_common(
            act_ref, dt_ref, dtT_ref, al_ref, alT_ref, tril_ref, triu_ref, et_ref)
        ela = jnp.exp(la_e)
        xdt_b = xdt.astype(BF16)
        xdte_b = (xdt * jnp.exp(la_q - la_e)).astype(BF16)
        ela_q = jnp.exp(la_q)
        causal = lax.broadcasted_iota(jnp.int32, (Q, Q), 0) >= lax.broadcasted_iota(jnp.int32, (Q, Q), 1)
        for g in range(G):
            gs = slice(g * GW, (g + 1) * GW)
            Bg = act_ref[:, D_INNER + g * D_STATE:D_INNER + (g + 1) * D_STATE].astype(BF16)
            Cg = act_ref[:, D_INNER + G * D_STATE + g * D_STATE:D_INNER + G * D_STATE + (g + 1) * D_STATE].astype(BF16)
            cb = lax.dot_general(Cg, Bg, NT, preferred_element_type=F32)
            st = scr[g]
            y_inter = jnp.dot(Cg, st.astype(BF16), preferred_element_type=F32) * ela[:, gs]
            ys = []
            for hh in range(HEADS_PER_GROUP):
                h = g * HEADS_PER_GROUP + hh
                m = (cb * _decay(la, laT, h, causal)).astype(BF16)
                ys.append(jnp.dot(m, xdt_b[:, h * HEAD_DIM:(h + 1) * HEAD_DIM], preferred_element_type=F32))
            y_ref[:, gs] = jnp.concatenate(ys, axis=1) + y_inter + x[:, gs] * dsk_ref[:, gs]
            scr[g] = st * ela_q[:, gs] + lax.dot_general(Bg, xdte_b[:, gs], TN, preferred_element_type=F32)

    full = lambda a: pl.BlockSpec(a.shape, lambda c: (0,) * a.ndim)
    al, alT = alog.reshape(1, -1), alog.reshape(-1, 1)
    return pl.pallas_call(
        body, grid=(nc,),
        in_specs=[pl.BlockSpec((Q, XBC_WIDTH), lambda c: (c, 0)), pl.BlockSpec((Q, N_SSM_HEADS), lambda c: (c, 0)),
                  pl.BlockSpec((N_SSM_HEADS, Q), lambda c: (0, c)), full(al), full(alT), full(dskip_e),
                  full(cs["tril"]), full(cs["triu"]), full(cs["et"])],
        out_specs=[pl.BlockSpec((Q, D_INNER), lambda c: (c, 0)), pl.BlockSpec((1, G, D_STATE, GW), lambda c: (c, 0, 0, 0))],
        out_shape=[S((T, D_INNER), F32), S((nc, G, D_STATE, GW), F32)],
        scratch_shapes=[pltpu.VMEM((G, D_STATE, GW), F32)],
        compiler_params=_cparams(("arbitrary",)), name=name)(act, dt, dtT, al, alT, dskip_e, cs["tril"], cs["triu"], cs["et"])


def _ssd_bwd(dy, act, dt, dtT, alog, dskip_e, states, cs, name):
    T = act.shape[0]
    nc = T // SSD_CHUNK
    Q, G, GW, H = SSD_CHUNK, N_SSM_GROUPS, D_INNER // N_SSM_GROUPS, N_SSM_HEADS

    def body(dy_ref, act_ref, dt_ref, dtT_ref, al_ref, alT_ref, dsk_ref, stp_ref, tril_ref, triu_ref, trils_ref, trius_ref,
             et_ref, e_ref, dact_ref, ddt_ref, ddtT_ref, da_ref, daT_ref, dsk_out_ref, dst, wbuf, ubuf, vbuf, sbuf):
        @pl.when(pl.program_id(0) == 0)
        def _():
            dst[...] = jnp.zeros_like(dst)
            da_ref[...] = jnp.zeros_like(da_ref)
            daT_ref[...] = jnp.zeros_like(daT_ref)
            dsk_out_ref[...] = jnp.zeros_like(dsk_out_ref)
        a_row, a_col, dtv, dtTv, la, laT, la_e, dt_e, x, xdt, la_q = _ssd_common(
            act_ref, dt_ref, dtT_ref, al_ref, alT_ref, tril_ref, triu_ref, et_ref)
        dyv = dy_ref[...]
        ela = jnp.exp(la_e)
        e_end = jnp.exp(la_q - la_e)
        ela_q = jnp.exp(la_q)
        dye_b = (dyv * ela).astype(BF16)
        dy_b = dyv.astype(BF16)
        xdt_b = xdt.astype(BF16)
        xdte_b = (xdt * e_end).astype(BF16)
        ri = lax.broadcasted_iota(jnp.int32, (Q, Q), 0)
        ci = lax.broadcasted_iota(jnp.int32, (Q, Q), 1)
        causal = ri >= ci
        rows = []
        for g in range(G):
            gs = slice(g * GW, (g + 1) * GW)
            Bg = act_ref[:, D_INNER + g * D_STATE:D_INNER + (g + 1) * D_STATE].astype(BF16)
            Cg = act_ref[:, D_INNER + G * D_STATE + g * D_STATE:D_INNER + G * D_STATE + (g + 1) * D_STATE].astype(BF16)
            cb = lax.dot_general(Cg, Bg, NT, preferred_element_type=F32)
            stp = stp_ref[0, g]
            stp_b = stp.astype(BF16)
            dstv = dst[g]
            dst_b = dstv.astype(BF16)
            y_inter = jnp.dot(Cg, stp_b, preferred_element_type=F32) * ela[:, gs]
            wbuf[:, gs] = dyv[:, gs] * y_inter
            dxdt_state = jnp.dot(Bg, dst_b, preferred_element_type=F32) * e_end[:, gs]
            ubuf[:, gs] = dxdt_state * xdt[:, gs]
            dC = lax.dot_general(dye_b[:, gs], stp_b, NT, preferred_element_type=F32)
            dB = lax.dot_general(xdte_b[:, gs], dst_b, NT, preferred_element_type=F32)
            sbuf[:, gs] = jnp.broadcast_to(jnp.sum(dstv * stp, axis=0, keepdims=True), (8, GW))
            dst[g] = dstv * ela_q[:, gs] + lax.dot_general(Cg, dye_b[:, gs], TN, preferred_element_type=F32)
            dG = jnp.zeros((Q, Q), F32)
            dxs = []
            for hh in range(HEADS_PER_GROUP):
                h = g * HEADS_PER_GROUP + hh
                hs = slice(h * HEAD_DIM, (h + 1) * HEAD_DIM)
                L = _decay(la, laT, h, causal)
                M = cb * L
                dM = lax.dot_general(dy_b[:, hs], xdt_b[:, hs], NT, preferred_element_type=F32)
                dG = dG + dM * L
                W = dM * M
                rows.append(jnp.sum(W.T, axis=0, keepdims=True) - jnp.sum(W, axis=0, keepdims=True))
                dxs.append(lax.dot_general(M.astype(BF16), dy_b[:, hs], TN, preferred_element_type=F32))
            dG_b = dG.astype(BF16)
            dC = dC + jnp.dot(dG_b, Bg, preferred_element_type=F32)
            dB = dB + lax.dot_general(dG_b, Cg, TN, preferred_element_type=F32)
            dxdt = jnp.concatenate(dxs, axis=1) + dxdt_state
            vbuf[:, gs] = dxdt * x[:, gs]
            dact_ref[:, gs] = dxdt * dt_e[:, gs] + dyv[:, gs] * dsk_ref[:, gs]
            dact_ref[:, D_INNER + g * D_STATE:D_INNER + (g + 1) * D_STATE] = dB
            dact_ref[:, D_INNER + G * D_STATE + g * D_STATE:D_INNER + G * D_STATE + (g + 1) * D_STATE] = dC
        e = e_ref[...]
        w = _dotx_r(wbuf[...], e, parts=2)
        u = _dotx_r(ubuf[...], e, parts=2)
        vx = _dotx_r(vbuf[...], e, parts=2)
        dsk = _dotx_r(jnp.broadcast_to(jnp.sum(dyv * x, axis=0, keepdims=True), (8, D_INNER)), e, parts=2)[0:1]
        s0 =_dotx_r(sbuf[...], e, parts=2)[0:1] * jnp.exp(la[Q - 1:Q, :])
        ddelta = _dotx_l(triu_ref[...], w) + _dotx_l(trils_ref[...], u) + s0
        ddt_ref[...] = ddelta * a_row + vx
        ddeltaT = _dotx_r(jnp.concatenate(rows, axis=0), tril_ref[...])
        ddtT_ref[...] = ddeltaT * a_col
        da_ref[...] += jnp.sum(ddelta * dtv, axis=0, keepdims=True)
        daT_ref[...] += jnp.sum(ddeltaT * dtTv, axis=1, keepdims=True)
        dsk_out_ref[...] += dsk

    rev = lambda c: nc - 1 - c
    full = lambda a: pl.BlockSpec(a.shape, lambda c: (0,) * a.ndim)
    al, alT = alog.reshape(1, -1), alog.reshape(-1, 1)
    consts = [cs[k] for k in ("tril", "triu", "trils", "trius", "et", "e")]
    return pl.pallas_call(
        body, grid=(nc,),
        in_specs=[pl.BlockSpec((Q, D_INNER), lambda c: (rev(c), 0)), pl.BlockSpec((Q, XBC_WIDTH), lambda c: (rev(c), 0)),
                  pl.BlockSpec((Q, H), lambda c: (rev(c), 0)), pl.BlockSpec((H, Q), lambda c: (0, rev(c))),
                  full(al), full(alT), full(dskip_e), pl.BlockSpec((1, G, D_STATE, GW), lambda c: (rev(c), 0, 0, 0))]
                 + [full(a) for a in consts],
        out_specs=[pl.BlockSpec((Q, XBC_WIDTH), lambda c: (rev(c), 0)), pl.BlockSpec((Q, H), lambda c: (rev(c), 0)),
                   pl.BlockSpec((H, Q), lambda c: (0, rev(c))), pl.BlockSpec((1, H), lambda c: (0, 0)),
                   pl.BlockSpec((H, 1), lambda c: (0, 0)), pl.BlockSpec((1, H), lambda c: (0, 0))],
        out_shape=[S((T, XBC_WIDTH), F32), S((T, H), F32), S((H, T), F32), S((1, H), F32), S((H, 1), F32), S((1, H), F32)],
        scratch_shapes=[pltpu.VMEM((G, D_STATE, GW), F32), pltpu.VMEM((Q, D_INNER), F32), pltpu.VMEM((Q, D_INNER), F32),
                        pltpu.VMEM((Q, D_INNER), F32), pltpu.VMEM((8, D_INNER), F32)],
        compiler_params=_cparams(("arbitrary",)), name=name)(dy, act, dt, dtT, al, alT, dskip_e, states, *consts)


def _a_log_grad(da, daT_row, alog, name):
    def body(a_ref, b_ref, al_ref, o_ref):
        o_ref[...] = (a_ref[...] + b_ref[...]) * (-jnp.exp(al_ref[...]))
    return pl.pallas_call(body, out_shape=S((1, N_SSM_HEADS), F32), name=name)(da, daT_row, alog.reshape(1, -1))


def _place():
    x, y, c = lax.axis_index("x"), lax.axis_index("y"), lax.axis_index("c")
    return x, y, c, [(1 - x, y), (x, 1 - y), (1 - x, 1 - y)]


def _all_gather(shards, name):
    npc = len(shards)

    def body(*refs):
        x_refs, o_refs = refs[:npc], refs[npc:2 * npc]
        send_sems, recv_sems, local_sems = refs[2 * npc:]
        x, y, c, chips = _place()
        me, sib = (x, y, c), (x, y, 1 - c)

        def rows(dev, i):
            return o_refs[i].at[4 * dev[0] + 2 * dev[1] + dev[2]]

        def copy(k, i, block, to, src=None):
            return pltpu.make_async_remote_copy(
                src_ref=rows(block, i) if src is None else src, dst_ref=rows(block, i),
                send_sem=send_sems.at[k, i], recv_sem=recv_sems.at[k, i], device_id=to, device_id_type=MESH)

        mine = [pltpu.make_async_copy(x_refs[i], rows(me, i), local_sems.at[i]) for i in range(npc)]
        for cp in mine:
            cp.start()
        first = []
        for i in range(npc):
            first.append(copy(0, i, me, sib, src=x_refs[i]))
            first += [copy(1 + j, i, me, (*chip, c), src=x_refs[i]) for j, chip in enumerate(chips)]
        for cp in first:
            cp.start()
        passed = []
        for i in range(npc):
            for j, chip in enumerate(chips):
                copy(1 + j, i, (*chip, c), me).wait_recv()
                cp = copy(4 + j, i, (*chip, c), sib)
                cp.start()
                passed.append(cp)
        for i in range(npc):
            copy(0, i, sib, me).wait_recv()
            for j, chip in enumerate(chips):
                copy(4 + j, i, (*chip, 1 - c), me).wait_recv()
        for cp in first + passed:
            cp.wait_send()
        for cp in mine:
            cp.wait()

    anys = pl.BlockSpec(memory_space=pl.ANY)
    return pl.pallas_call(
        body, in_specs=[anys] * npc, out_specs=[anys] * npc, out_shape=[S((N_DEV,) + s.shape, s.dtype) for s in shards],
        scratch_shapes=[pltpu.SemaphoreType.DMA((7, npc)), pltpu.SemaphoreType.DMA((7, npc)), pltpu.SemaphoreType.DMA((npc,))],
        name=name)(*shards)


def _to_sibling(to_sib, name):
    npc = len(to_sib)

    def body(*refs):
        s_refs, o_refs, send_sems, recv_sems = refs[:npc], refs[npc:2 * npc], refs[2 * npc], refs[2 * npc + 1]
        x, y, c, _ = _place()
        cps = [pltpu.make_async_remote_copy(
            src_ref=s_refs[i], dst_ref=o_refs[i], send_sem=send_sems.at[i], recv_sem=recv_sems.at[i],
            device_id=(x, y, 1 - c), device_id_type=MESH) for i in range(npc)]
        for cp in cps:
            cp.start()
        for cp in cps:
            cp.wait()

    anys = pl.BlockSpec(memory_space=pl.ANY)
    return pl.pallas_call(
        body, in_specs=[anys] * npc, out_specs=[anys] * npc, out_shape=[S(s.shape, s.dtype) for s in to_sib],
        scratch_shapes=[pltpu.SemaphoreType.DMA((npc,)), pltpu.SemaphoreType.DMA((npc,))],
        name=name)(*to_sib)


def _to_chips(parts, name):
    npc = len(parts)

    def body(*refs):
        p_refs, o_refs, send_sems, recv_sems = refs[:npc], refs[npc:2 * npc], refs[2 * npc], refs[2 * npc + 1]
        x, y, c, chips = _place()
        cps = [pltpu.make_async_remote_copy(
            src_ref=p_refs[i].at[2 * chip[0] + chip[1]], dst_ref=o_refs[i].at[j], send_sem=send_sems.at[j, i],
            recv_sem=recv_sems.at[j, i], device_id=(*chip, c), device_id_type=MESH)
            for i in range(npc) for j, chip in enumerate(chips)]
        for cp in cps:
            cp.start()
        for cp in cps:
            cp.wait()

    anys = pl.BlockSpec(memory_space=pl.ANY)
    return pl.pallas_call(
        body, in_specs=[anys] * npc, out_specs=[anys] * npc, out_shape=[S((3,) + p.shape[1:], p.dtype) for p in parts],
        scratch_shapes=[pltpu.SemaphoreType.DMA((3, npc)), pltpu.SemaphoreType.DMA((3, npc))],
        name=name)(*parts)


def _reduce_scatter(pieces, name):
    c = lax.axis_index("c")
    x, y = lax.axis_index("x"), lax.axis_index("y")
    by_core = [p.reshape(4, 2, p.shape[1], p.shape[2]) for p in pieces]
    to_sib = [lax.dynamic_index_in_dim(p, 1 - c, axis=1, keepdims=False).astype(BF16) for p in by_core]
    keep = [lax.dynamic_index_in_dim(p, c, axis=1, keepdims=False) for p in by_core]
    from_sib = _to_sibling(to_sib, name + "_d2d")

    def add1(a, b):
        s = a + b.astype(F32)
        return [s, s], []

    def add2(a, b, c_, d_):
        return [((a + b.astype(F32)) + c_.astype(F32)) + d_.astype(F32)], []

    parts, parts_b = [], []
    for i, (k, f) in enumerate(zip(keep, from_sib)):
        _, r, C = k.shape
        p, pb = _rowwise(add1, [k.reshape(4 * r, C), f.reshape(4 * r, C)], [], [(C, F32), (C, BF16)], tm=2048, name=f"{name}_add1_{i}")
        parts.append(p.reshape(4, r, C))
        parts_b.append(pb.reshape(4, r, C))
    got = _to_chips(parts_b, name + "_ici")
    outs = []
    for i, (p, g) in enumerate(zip(parts, got)):
        own = lax.dynamic_index_in_dim(p, 2 * x + y, axis=0, keepdims=False)
        outs.append(_rowwise(add2, [own, g[0], g[1], g[2]], [], [(p.shape[2], F32)], tm=2048, name=f"{name}_add2_{i}")[0])
    return outs


def _all_reduce_small(v, name):
    R, C = v.shape

    def body(x_ref, out_ref, buf, send_sems, recv_sems):
        x, y, c, chips = _place()
        me, sib = (x, y, c), (x, y, 1 - c)

        def rows(dev):
            return buf.at[4 * dev[0] + 2 * dev[1] + dev[2]]

        def copy(k, block, to, src=None):
            return pltpu.make_async_remote_copy(
                src_ref=rows(block) if src is None else src, dst_ref=rows(block),
                send_sem=send_sems.at[k], recv_sem=recv_sems.at[k], device_id=to, device_id_type=MESH)

        buf[4 * x + 2 * y + c] = x_ref[...]
        first = [copy(0, me, sib, src=x_ref)] + [copy(1 + j, me, (*chip, c), src=x_ref) for j, chip in enumerate(chips)]
        for cp in first:
            cp.start()
        passed = [copy(4 + j, (*chip, c), sib) for j, chip in enumerate(chips)]
        for j, chip in enumerate(chips):
            copy(1 + j, (*chip, c), me).wait_recv()
            passed[j].start()
        copy(0, sib, me).wait_recv()
        for j, chip in enumerate(chips):
            copy(4 + j, (*chip, 1 - c), me).wait_recv()
        for cp in first + passed:
            cp.wait_send()
        acc = buf[0]
        for j in range(1, N_DEV):
            acc = acc + buf[j]
        out_ref[...] = acc

    vm = pl.BlockSpec(memory_space=pltpu.VMEM)
    return pl.pallas_call(
        body, in_specs=[vm], out_specs=vm, out_shape=S((R, C), F32),
        scratch_shapes=[pltpu.VMEM((N_DEV, R, C), F32), pltpu.SemaphoreType.DMA((7,)), pltpu.SemaphoreType.DMA((7,))],
        compiler_params=pltpu.CompilerParams(vmem_limit_bytes=VMEM_LIMIT), name=name)(v)


SEG = (("q", 0, 1536), ("k", 1536, 1536), ("v", 3072, 1536), ("z", 4608, 2048), ("xbc", 6656, 3072), ("dt", 9728, 32), ("gl", 9760, 2048))


def _split_w_in(w_in_full):
    out = {}
    for nm, off, n in SEG:
        w = w_in_full[:, off:off + n]
        if nm == "dt":
            w = jnp.pad(w, ((0, 0), (0, LANES - n)))
        out[nm] = w
    W = ATTN_OUT_WIDTH
    out["qkv"] = [jnp.concatenate([out[s][:, g * W:(g + 1) * W] for s in ("q", "k", "v")], axis=1) for g in range(N_DIL)]
    out["qkv_t"] = [[out[s][:, g * W:(g + 1) * W] for s in ("q", "k", "v")] for g in range(N_DIL)]
    return out


def _layer_fwd(h, p, W, biases, cs, l):
    T = h.shape[0]
    nm = lambda s: f"{s}_l{l}"
    sv = {"h_in": h}
    xns = _rmsnorm_fwd(h, p["norm1_w"], nm("norm1"), dils=[d for _, d in DILATED_GROUPS[1:]])
    xn = xns[0]
    wi = W["w_in"]
    z = _mm(xn, wi["z"], out_dtype=BF16, name=nm("proj_z"))
    xbc = _mm(xn, wi["xbc"], name=nm("proj_xbc"))
    dt_raw = _mm(xn, wi["dt"], name=nm("proj_dt"))
    gl = _mm(xn, wi["gl"], out_dtype=BF16, name=nm("proj_gl"))
    os_, ls, qkvs = [], [], []
    for g, (window, dil) in enumerate(DILATED_GROUPS):
        qkv = _mm(xns[g], wi["qkv"][g], out_dtype=BF16, name=nm(f"proj_qkv_g{g}"))
        o, lse = _attn_fwd(qkv, biases[g], dil, nm(f"attn_fwd_g{g}"))
        os_.append(o)
        ls.append(lse)
        qkvs.append(qkv)
    attn_b, attn_f = _combine_fwd(os_, ls, nm("combine"))
    u_conv, act = _conv_fwd(xbc, p["conv_w"], p["conv_b"].reshape(1, -1), nm("conv"))
    dt = _dt_fwd(dt_raw, p["dt_bias"], nm("dt"))
    dtT = dt.T
    dskip_e = jnp.repeat(p["d_skip"], D_INNER // N_SSM_HEADS).reshape(1, -1)
    y, states = _ssd_fwd(act, dt, dtT, p["a_log"], dskip_e, cs, nm("ssd_fwd"))
    ssm = _ssm_norm_fwd(y, z, p["ssm_norm_w"], nm("ssm_norm"))
    a_br = _mm(attn_b, W["w_attn_branch"], out_dtype=BF16, name=nm("attn_branch"))
    s_br = _mm(ssm, W["w_ssm_branch"], out_dtype=BF16, name=nm("ssm_branch"))
    merged = _gate_fwd(a_br, s_br, gl, nm("gate"))
    h_mid = _mm(merged, W["w_out"], acc=h, name=nm("out_proj"))
    xn2 = _rmsnorm_fwd(h_mid, p["norm2_w"], nm("norm2"))[0]
    u_ffn = _mm(xn2, W["w_ffn_in"], out_dtype=BF16, name=nm("ffn_in"))
    ffn_act = _swiglu_fwd(u_ffn, nm("swiglu"))
    h_out = _mm(ffn_act, W["w_ffn_out"], acc=h_mid, name=nm("ffn_out"))
    sv.update(xn=xn, xns=xns, qkvs=qkvs, z=z, xbc=xbc, dt_raw=dt_raw, gl=gl, ls=ls, attn_b=attn_b, attn_f=attn_f, u_conv=u_conv,
              act=act, dt=dt, dtT=dtT, dskip_e=dskip_e, y=y, states=states, ssm=ssm, a_br=a_br, s_br=s_br, merged=merged,
              h_mid=h_mid, xn2=xn2, u_ffn=u_ffn, ffn_act=ffn_act)
    return h_out, sv


def _layer_bwd(dh, sv, p, W, biases, cs, head_ones, l):
    T = dh.shape[0]
    nm = lambda s: f"{s}_l{l}"
    gr = {}
    dact = _mm(dh, W["w_ffn_out"], tb=True, out_dtype=BF16, name=nm("d_ffn_act"))
    gr["w_ffn_out"] = _mm(sv["ffn_act"], dh, ta=True, name=nm("g_ffn_out"))
    du = _swiglu_bwd(dact, sv["u_ffn"], nm("d_swiglu"))
    dxn2 = _mm(du, W["w_ffn_in"], tb=True, name=nm("d_xn2"))
    gr["w_ffn_in"] = _mm(sv["xn2"], du, ta=True, name=nm("g_ffn_in"))
    dh_mid, gr["norm2_w"] = _rmsnorm_bwd(dxn2, sv["h_mid"], p["norm2_w"], dh, nm("d_norm2"))
    dmerged = _mm(dh_mid, W["w_out"], tb=True, out_dtype=BF16, name=nm("d_merged"))
    gr["w_out"] = _mm(sv["merged"], dh_mid, ta=True, name=nm("g_out"))
    d_a, d_s, dgl = _gate_bwd(dmerged, sv["a_br"], sv["s_br"], sv["gl"], nm("d_gate"))
    dattn = _mm(d_a, W["w_attn_branch"], tb=True, out_dtype=BF16, name=nm("d_attn"))
    gr["w_attn_branch"] = _mm(sv["attn_b"], d_a, ta=True, name=nm("g_attn_branch"))
    dssm = _mm(d_s, W["w_ssm_branch"], tb=True, out_dtype=BF16, name=nm("d_ssm"))
    gr["w_ssm_branch"] = _mm(sv["ssm"], d_s, ta=True, name=nm("g_ssm_branch"))
    dy, dz, gr["ssm_norm_w"] = _ssm_norm_bwd(dssm, sv["y"], sv["z"], p["ssm_norm_w"], nm("d_ssm_norm"))
    dact_c, ddt_a, ddt_bT, da, daT, dskip = _ssd_bwd(dy, sv["act"], sv["dt"], sv["dtT"], p["a_log"], sv["dskip_e"], sv["states"], cs, nm("ssd_bwd"))
    gr["a_log"] = _a_log_grad(da, daT.T, p["a_log"], nm("g_a_log")).reshape(-1)
    gr["d_skip"] = dskip.reshape(-1)
    ddt_raw, ddt_bias = _dt_bwd(ddt_a, ddt_bT.T, sv["dt_raw"], p["dt_bias"], nm("d_dt"))
    gr["dt_bias"] = ddt_bias.reshape(-1)
    dxbc, gr["conv_w"], dconv_b = _conv_bwd(dact_c, sv["u_conv"], sv["xbc"], p["conv_w"], nm("d_conv"))
    gr["conv_b"] = dconv_b.reshape(-1)
    outs = _combine_bwd(dattn, sv["attn_f"], sv["ls"], head_ones, nm("d_combine"))
    wi = W["w_in"]
    dbias, dxn = [], None
    gqkv = [[None] * N_DIL for _ in range(3)]
    for g, (window, dil) in enumerate(DILATED_GROUPS):
        dq, dk, dv, db = _attn_bwd(sv["qkvs"][g], biases[g], sv["ls"][g], outs[2 * g], outs[2 * g + 1], dil, nm(f"attn_bwd_g{g}"))
        dbias.append(db)
        dxn = _mm_dil([dq, dk, dv], wi["qkv_t"][g], dil, dxn, nm(f"d_xn_qkv_g{g}"))
        for i, dseg in enumerate((dq, dk, dv)):
            gqkv[i][g] = _mm(sv["xns"][g], dseg, ta=True, name=nm(f"g_in_{'qkv'[i]}_g{g}"))
    parts = (("z", dz), ("xbc", dxbc), ("dt", ddt_raw), ("gl", dgl))
    gws = gqkv[0] + gqkv[1] + gqkv[2]
    for sname, dseg in parts:
        dxn = _mm(dseg, wi[sname], tb=True, acc=dxn, name=nm("d_xn_" + sname))
        gw = _mm(sv["xn"], dseg, ta=True, name=nm("g_in_" + sname))
        gws.append(gw[:, :N_SSM_HEADS] if sname == "dt" else gw)
    gr["w_in"] = jnp.concatenate(gws, axis=1)
    dh_in, gr["norm1_w"] = _rmsnorm_bwd(dxn, sv["h_in"], p["norm1_w"], dh_mid, nm("d_norm1"))
    return dh_in, gr, dbias


def _step_local(x, tgt, small, Wfull, rel_bias, final_norm_w):
    cs = _ssd_consts()
    head = np.repeat(np.arange(HEADS_PER_GROUP), HEAD_DIM)
    head_ones = jnp.asarray(head[:, None] == head[None, :], BF16)
    biases, onehots = [], []
    for g, (window, dil) in enumerate(DILATED_GROUPS):
        onehot, valid = _bias_consts(dil, window // dil)
        rel_g_t = rel_bias[:, g * HEADS_PER_GROUP:(g + 1) * HEADS_PER_GROUP].T
        b = _bias_gather(rel_g_t, onehot, valid, f"bias_gather_g{g}")
        biases.append(b.reshape(HEADS_PER_GROUP, ATTN_BLOCK, 2 * ATTN_BLOCK))
        onehots.append(onehot)
    h, saved = x, []
    for l in range(DEPTH):
        W = dict(Wfull[l])
        W["w_in"] = _split_w_in(W["w_in"])
        Wfull[l] = W
        h, sv = _layer_fwd(h, small[l], W, biases, cs, l)
        saved.append(sv)
    dh, g_final, loss = _loss_head(h, final_norm_w, tgt, "loss_head")
    grads = [None] * DEPTH
    dbias_tot = [None] * N_DIL
    for l in reversed(range(DEPTH)):
        dh, grads[l], dbias = _layer_bwd(dh, saved[l], small[l], Wfull[l], biases, cs, head_ones, l)
        for g in range(N_DIL):
            dbias_tot[g] = dbias[g] if dbias_tot[g] is None else dbias_tot[g] + dbias[g]
    d_rel = jnp.concatenate(
        [_bias_scatter(dbias_tot[g].reshape(HEADS_PER_GROUP, -1), onehots[g], f"bias_scatter_g{g}").T for g in range(N_DIL)], axis=1)
    return loss, dh, grads, d_rel, g_final


def _unshard(nm, g):
    _, rows, cols = g.shape
    if nm in COL_SHARDED:
        return g.transpose(1, 0, 2).reshape(rows, N_DEV * cols)
    return g.reshape(N_DEV * rows, cols)


def _shard(nm, w):
    rows, cols = w.shape
    if nm in COL_SHARDED:
        return w.reshape(rows, N_DEV, cols // N_DEV).transpose(1, 0, 2)
    return w.reshape(N_DEV, rows // N_DEV, cols)


SMALL_LAYER = (("norm1_w", 1024), ("conv_w", 12288), ("conv_b", 3072), ("dt_bias", 32), ("a_log", 32), ("d_skip", 32),
               ("ssm_norm_w", 2048), ("norm2_w", 1024))
SMALL_GLOBAL = (("rel_bias", 768), ("final_norm_w", 1024), ("loss", 1))


def _pad128(v):
    n = v.shape[0]
    return jnp.pad(v, (0, -n % LANES))


def _pack_small(per_layer, glob):
    parts = [_pad128(per_layer[l][nm].reshape(-1)) for l in range(DEPTH) for nm, _ in SMALL_LAYER]
    parts += [_pad128(glob[nm].reshape(-1)) for nm, _ in SMALL_GLOBAL]
    flat = jnp.concatenate(parts)
    flat = jnp.pad(flat, (0, -flat.shape[0] % (8 * LANES)))
    return flat.reshape(-1, LANES)


def _unpack_small(packed):
    flat = packed.reshape(-1)
    per_layer, glob, off = [dict() for _ in range(DEPTH)], {}, 0
    for l in range(DEPTH):
        for nm, n in SMALL_LAYER:
            per_layer[l][nm] = flat[off:off + n]
            off += n + (-n % LANES)
    for nm, n in SMALL_GLOBAL:
        glob[nm] = flat[off:off + n]
        off += n + (-n % LANES)
    return per_layer, glob


def kernel(x, norm1_w, w_in, conv_w, conv_b, dt_bias, a_log, d_skip, ssm_norm_w, w_attn_branch, w_ssm_branch, w_out, norm2_w, w_ffn_in, w_ffn_out, rel_bias, final_norm_w, loss_target, m_norm1_w, m_w_in, m_conv_w, m_conv_b, m_dt_bias, m_a_log, m_d_skip, m_ssm_norm_w, m_w_attn_branch, m_w_ssm_branch, m_w_out, m_norm2_w, m_w_ffn_in, m_w_ffn_out, m_rel_bias, m_final_norm_w, v_norm1_w, v_w_in, v_conv_w, v_conv_b, v_dt_bias, v_a_log, v_d_skip, v_ssm_norm_w, v_w_attn_branch, v_w_ssm_branch, v_w_out, v_norm2_w, v_w_ffn_in, v_w_ffn_out, v_rel_bias, v_final_norm_w):
    big = dict(w_in=w_in, w_attn_branch=w_attn_branch, w_ssm_branch=w_ssm_branch, w_out=w_out, w_ffn_in=w_ffn_in, w_ffn_out=w_ffn_out)
    big_m = dict(w_in=m_w_in, w_attn_branch=m_w_attn_branch, w_ssm_branch=m_w_ssm_branch, w_out=m_w_out, w_ffn_in=m_w_ffn_in, w_ffn_out=m_w_ffn_out)
    big_v = dict(w_in=v_w_in, w_attn_branch=v_w_attn_branch, w_ssm_branch=v_w_ssm_branch, w_out=v_w_out, w_ffn_in=v_w_ffn_in, w_ffn_out=v_w_ffn_out)
    sm = dict(norm1_w=norm1_w, conv_w=conv_w, conv_b=conv_b, dt_bias=dt_bias, a_log=a_log, d_skip=d_skip, ssm_norm_w=ssm_norm_w, norm2_w=norm2_w)
    sm_m = dict(norm1_w=m_norm1_w, conv_w=m_conv_w, conv_b=m_conv_b, dt_bias=m_dt_bias, a_log=m_a_log, d_skip=m_d_skip, ssm_norm_w=m_ssm_norm_w, norm2_w=m_norm2_w)
    sm_v = dict(norm1_w=v_norm1_w, conv_w=v_conv_w, conv_b=v_conv_b, dt_bias=v_dt_bias, a_log=v_a_log, d_skip=v_d_skip, ssm_norm_w=v_ssm_norm_w, norm2_w=v_norm2_w)
    me = 4 * lax.axis_index("x") + 2 * lax.axis_index("y") + lax.axis_index("c")

    gathered = _all_gather([big[nm][l].astype(BF16) for l in range(DEPTH) for nm in BIG], "all_gather_weights")
    Wfull = [{nm: _unshard(nm, gathered[l * len(BIG) + i]) for i, nm in enumerate(BIG)} for l in range(DEPTH)]

    conv_full = []
    for l in range(DEPTH):
        z = jnp.zeros((N_DEV, CONV_WIDTH, XBC_WIDTH // N_DEV), F32)
        conv_full.append(lax.dynamic_update_index_in_dim(z, conv_w[l], me, axis=0))
    cw = jnp.stack(conv_full).reshape(-1, LANES)
    cw = _all_reduce_small(cw, "gather_conv_w").reshape(DEPTH, N_DEV, CONV_WIDTH, XBC_WIDTH // N_DEV)
    cw = cw.transpose(0, 2, 1, 3).reshape(DEPTH, CONV_WIDTH, XBC_WIDTH)

    small = [{nm: (cw[l] if nm == "conv_w" else a[l]) for nm, a in sm.items()} for l in range(DEPTH)]
    loss, dx, grads, d_rel, g_final = _step_local(x[0], loss_target[0], small, Wfull, rel_bias, final_norm_w)

    mine = _reduce_scatter([_shard(nm, grads[l][nm]) for l in range(DEPTH) for nm in BIG], "reduce_scatter_grads")
    g_big = {nm: jnp.stack([mine[l * len(BIG) + i] for l in range(DEPTH)]) for i, nm in enumerate(BIG)}

    per_layer = [{nm: grads[l][nm] for nm, _ in SMALL_LAYER} for l in range(DEPTH)]
    packet = _pack_small(per_layer, dict(rel_bias=d_rel, final_norm_w=g_final, loss=loss[0, :1]))
    per_layer, glob = _unpack_small(_all_reduce_small(packet, "all_reduce_small"))
    g_small = {nm: jnp.stack([per_layer[l][nm] for l in range(DEPTH)]) for nm, _ in SMALL_LAYER}
    cwg = g_small["conv_w"].reshape(DEPTH, CONV_WIDTH, N_DEV, XBC_WIDTH // N_DEV)
    g_small["conv_w"] = lax.dynamic_index_in_dim(cwg, me, axis=2, keepdims=False)
    for nm in sm:
        g_small[nm] = g_small[nm].reshape(sm[nm].shape)
    g_rel = glob["rel_bias"].reshape(rel_bias.shape)
    g_fin = glob["final_norm_w"]
    loss_out = glob["loss"][0]

    def adam(w, g, m, v, name):
        shp = w.shape
        two = lambda a: a.reshape(-1, shp[-1]) if a.ndim > 1 else a.reshape(1, -1)
        d, nm_, nv = _adamw(two(w), two(g), two(m), two(v), name)
        return d.reshape(shp), nm_.reshape(shp), nv.reshape(shp)

    order = ["norm1_w", "w_in", "conv_w", "conv_b", "dt_bias", "a_log", "d_skip", "ssm_norm_w", "w_attn_branch", "w_ssm_branch",
             "w_out", "norm2_w", "w_ffn_in", "w_ffn_out", "rel_bias", "final_norm_w"]
    allw = {**big, **sm, "rel_bias": rel_bias, "final_norm_w": final_norm_w}
    allm = {**big_m, **sm_m, "rel_bias": m_rel_bias, "final_norm_w": m_final_norm_w}
    allv = {**big_v, **sm_v, "rel_bias": v_rel_bias, "final_norm_w": v_final_norm_w}
    allg = {**g_big, **g_small, "rel_bias": g_rel, "final_norm_w": g_fin}
    deltas, new_m, new_v = [], [], []
    for nm in order:
        d, a, b = adam(allw[nm], allg[nm], allm[nm], allv[nm], "adamw_" + nm)
        deltas.append(d)
        new_m.append(a)
        new_v.append(b)
    return (loss_out, dx[None], *[allg[nm] for nm in order], *deltas, *new_m, *new_v)
```

```python
import functools
import math

import numpy as np
import jax
import jax.numpy as jnp
from jax import lax
from jax.experimental import pallas as pl
from jax.experimental.pallas import tpu as pltpu

F32, BF16 = jnp.float32, jnp.bfloat16
S = jax.ShapeDtypeStruct
MESH = pl.DeviceIdType.MESH

D_MODEL = 1024
DEPTH = 2
HEAD_DIM = 64
DILATED_GROUPS = ((128, 1), (512, 4), (2048, 16))
N_DIL = 3
HEADS_PER_GROUP = 8
ATTN_WIDTH = 1536
ATTN_OUT_WIDTH = 512
ATTN_BLOCK = 128
N_REL_BUCKETS = 32
REL_MAX_DISTANCE = 2048
D_INNER = 2048
N_SSM_HEADS = 32
N_SSM_GROUPS = 4
D_STATE = 128
CONV_WIDTH = 4
SSD_CHUNK = 128
XBC_WIDTH = 3072
D_FF = 2816
EPS = 1e-6
ADAM_LR, ADAM_B1, ADAM_B2, ADAM_EPS, ADAM_WD, ADAM_STEP = 0.001, 0.9, 0.999, 1e-08, 0.01, 10

N_DEV = 8
LANES = 128
VMEM_LIMIT = 56 * 1024 * 1024
ROW_TILES_BYTES = 36 * 1024 * 1024
NEG = -1e30
BIG = ("w_in", "w_attn_branch", "w_ssm_branch", "w_out", "w_ffn_in", "w_ffn_out")
COL_SHARDED = ("w_in", "w_attn_branch", "w_ffn_in")

NT = (((1,), (1,)), ((), ()))
TN = (((0,), (0,)), ((), ()))


def _cparams(sem=None):
    return pltpu.CompilerParams(dimension_semantics=sem, vmem_limit_bytes=VMEM_LIMIT)


def _pick(n, target, mult=LANES):
    best = None
    for t in range(mult, min(n, target) + 1, mult):
        if n % t == 0:
            best = t
    return best or n


def _silu(x):
    return x * jax.nn.sigmoid(x)


def _dsilu(x):
    s = jax.nn.sigmoid(x)
    return s * (1.0 + x * (1.0 - s))


def _split2(x):
    hi = x.astype(BF16)
    lo = (x - hi.astype(F32)).astype(BF16)
    return hi, lo


def _split3(x):
    x1 = x.astype(BF16)
    r1 = x - x1.astype(F32)
    x2 = r1.astype(BF16)
    x3 = (r1 - x2.astype(F32)).astype(BF16)
    return x1, x2, x3


def _dotx_r(x, m, parts=3):
    xs = _split3(x) if parts == 3 else _split2(x)
    out = jnp.dot(xs[0], m, preferred_element_type=F32)
    for xi in xs[1:]:
        out = out + jnp.dot(xi, m, preferred_element_type=F32)
    return out


def _dotx_l(m, x, parts=3):
    xs = _split3(x) if parts == 3 else _split2(x)
    out = jnp.dot(m, xs[0], preferred_element_type=F32)
    for xi in xs[1:]:
        out = out + jnp.dot(m, xi, preferred_element_type=F32)
    return out


def _mm(a, b, *, ta=False, tb=False, out_dtype=F32, acc=None, name, tm=1536, tn=1536, tk=1536, extras=(), epilogue=None, outs=None):
    M, K = (a.shape[1], a.shape[0]) if ta else a.shape
    N = b.shape[0] if tb else b.shape[1]
    tm, tn, tk = _pick(M, tm), _pick(N, tn), _pick(K, tk)
    nk = K // tk
    dims = (((0 if ta else 1,), (1 if tb else 0,)), ((), ()))
    has_acc = acc is not None
    outs = [(N, out_dtype)] if outs is None else outs
    ne, no = len(extras), len(outs)

    def body(*refs):
        a_ref, b_ref = refs[:2]
        c_ref = refs[2] if has_acc else None
        e_refs = refs[2 + has_acc:2 + has_acc + ne]
        o_refs = refs[2 + has_acc + ne:2 + has_acc + ne + no]
        acc_ref = refs[-1]
        k = pl.program_id(2)
        part = lax.dot_general(a_ref[...].astype(BF16), b_ref[...].astype(BF16), dims, preferred_element_type=F32)

        @pl.when(k == 0)
        def _():
            acc_ref[...] = part + c_ref[...].astype(F32) if has_acc else part

        @pl.when(k > 0)
        def _():
            acc_ref[...] += part

        @pl.when(k == nk - 1)
        def _():
            res = acc_ref[...]
            tiles = [res] if epilogue is None else epilogue(res, *[e[...] for e in e_refs])
            for o_ref, t in zip(o_refs, tiles):
                o_ref[...] = t.astype(o_ref.dtype)

    def cspec(cols):
        return pl.BlockSpec((tm, cols * tn // N), lambda i, j, k: (i, j))

    a_spec = pl.BlockSpec((tk, tm), lambda i, j, k: (k, i)) if ta else pl.BlockSpec((tm, tk), lambda i, j, k: (i, k))
    b_spec = pl.BlockSpec((tn, tk), lambda i, j, k: (j, k)) if tb else pl.BlockSpec((tk, tn), lambda i, j, k: (k, j))
    in_specs, args = [a_spec, b_spec], [a, b]
    if has_acc:
        in_specs.append(cspec(N))
        args.append(acc)
    in_specs += [cspec(e.shape[1]) for e in extras]
    args += list(extras)
    res = pl.pallas_call(
        body, grid=(M // tm, N // tn, nk), in_specs=in_specs, out_specs=[cspec(c) for c, _ in outs],
        out_shape=[S((M, c), dt) for c, dt in outs], scratch_shapes=[pltpu.VMEM((tm, tn), F32)],
        compiler_params=_cparams(("parallel", "parallel", "arbitrary")), name=name)(*args)
    return res[0] if len(outs) == 1 else res


def _mm_dil(a_list, b_list, d, acc, name, tm=1024):
    T, K = a_list[0].shape
    N = b_list[0].shape[0]
    tm, tn = min(tm, T), _pick(N, 1024)
    na = len(a_list)
    has_acc = acc is not None

    def body(*refs):
        a_refs, b_refs, rest = refs[:na], refs[na:2 * na], refs[2 * na:]
        c_ref = rest[0] if has_acc else None
        o_ref, scr = rest[-2], rest[-1]
        out = c_ref[...] if has_acc else None
        for a_ref, b_ref in zip(a_refs, b_refs):
            a_tok = _dil_to_tok(scr, a_ref, d).astype(BF16) if d > 1 else a_ref[...]
            part = lax.dot_general(a_tok, b_ref[...], NT, preferred_element_type=F32)
            out = part if out is None else out + part
        o_ref[...] = out

    if d > 1:
        a_spec = pl.BlockSpec((d, tm // d, K), lambda i, j: (0, i, 0))
        a_args = [a.reshape(d, T // d, K) for a in a_list]
    else:
        a_spec = pl.BlockSpec((tm, K), lambda i, j: (i, 0))
        a_args = list(a_list)
    o_spec = pl.BlockSpec((tm, tn), lambda i, j: (i, j))
    in_specs = [a_spec] * na + [pl.BlockSpec((tn, K), lambda i, j: (j, 0))] * na + ([o_spec] if has_acc else [])
    return pl.pallas_call(
        body, grid=(T // tm, N // tn), in_specs=in_specs, out_specs=o_spec, out_shape=S((T, N), F32),
        scratch_shapes=[pltpu.VMEM((K // LANES, tm, LANES), F32)],
        compiler_params=_cparams(("parallel", "parallel")), name=name)(*a_args, *b_list, *([acc] if has_acc else []))


def _dil_to_tok(scr, ref, d):
    n, C = ref.shape[1], ref.shape[2]
    for r in range(d):
        v = ref[r].astype(F32)
        for cb in range(C // LANES):
            scr.at[cb][pl.ds(r, n, stride=d), :] = v[:, cb * LANES:(cb + 1) * LANES]
    return jnp.concatenate([scr[cb] for cb in range(C // LANES)], axis=1)


def _tok_to_dil(scr, val, ref, d):
    n, C = ref.shape[1], ref.shape[2]
    for cb in range(C // LANES):
        scr[cb] = val[:, cb * LANES:(cb + 1) * LANES].astype(F32)
    for r in range(d):
        ref[r] = jnp.concatenate([scr.at[cb][pl.ds(r, n, stride=d), :] for cb in range(C // LANES)], axis=1).astype(ref.dtype)


def _rowwise(fn, rows, fulls, outs, accs=(), *, tm, name, cap=True):
    rows = [r if isinstance(r, tuple) else (r, r.shape[1], 0) for r in rows]
    first = rows[0]
    T = (first[1] if isinstance(first[0], str) else first[0]).shape[0]
    widest = max([r[1].shape[1] if isinstance(r[0], str) else r[1] for r in rows] + [o[0] for o in outs])
    if cap:
        tm = min(tm, max(8, ROW_TILES_BYTES // (2 * (len(rows) + len(outs))) // (4 * widest) // 8 * 8))
    tm = T if T <= tm else _pick(T, tm, 8)
    nr, nf, no, na = len(rows), len(fulls), len(outs), len(accs)
    dil_in = [i for i, r in enumerate(rows) if isinstance(r[0], str) and r[2] > 1]
    dil_out = [i for i, o in enumerate(outs) if len(o) == 3 and o[2] > 1]
    scr_cols = [rows[i][1].shape[1] for i in dil_in] + [outs[i][0] for i in dil_out]

    def body(*refs):
        r, f = refs[:nr], refs[nr:nr + nf]
        o, a = refs[nr + nf:nr + nf + no], refs[nr + nf + no:nr + nf + no + na]
        scr = refs[nr + nf + no + na:]
        tiles = []
        for i, x in enumerate(r):
            if i in dil_in:
                tiles.append(_dil_to_tok(scr[dil_in.index(i)], x, rows[i][2]))
            else:
                tiles.append(x[...].astype(F32))
        ro, ra = fn(*tiles, *[x[...] for x in f])
        for i, (ref, val) in enumerate(zip(o, ro)):
            if i in dil_out:
                _tok_to_dil(scr[len(dil_in) + dil_out.index(i)], val, ref, outs[i][2])
            else:
                ref[...] = val.astype(ref.dtype)
        if na:
            @pl.when(pl.program_id(0) == 0)
            def _():
                for ref in a:
                    ref[...] = jnp.zeros_like(ref)
            for ref, val in zip(a, ra):
                ref[...] += val

    in_specs, args = [], []
    for i, rr in enumerate(rows):
        if isinstance(rr[0], str):
            arr, d = rr[1], rr[2]
            if d > 1:
                in_specs.append(pl.BlockSpec((d, tm // d, arr.shape[1]), lambda i: (0, i, 0)))
                args.append(arr.reshape(d, T // d, arr.shape[1]))
            else:
                in_specs.append(pl.BlockSpec((tm, arr.shape[1]), lambda i: (i, 0)))
                args.append(arr)
        else:
            in_specs.append(pl.BlockSpec((tm, rr[1]), functools.partial(lambda i, cb: (i, cb), cb=rr[2])))
            args.append(rr[0])
    in_specs += [pl.BlockSpec(f.shape, lambda i: (0, 0)) for f in fulls]
    out_specs, out_shape = [], []
    for i, oo in enumerate(outs):
        if i in dil_out:
            d = oo[2]
            out_specs.append(pl.BlockSpec((d, tm // d, oo[0]), lambda i: (0, i, 0)))
            out_shape.append(S((d, T // d, oo[0]), oo[1]))
        else:
            out_specs.append(pl.BlockSpec((tm, oo[0]), lambda i: (i, 0)))
            out_shape.append(S((T, oo[0]), oo[1]))
    out_specs += [pl.BlockSpec(sh, lambda i: (0, 0)) for sh in accs]
    out_shape += [S(sh, F32) for sh in accs]
    res = pl.pallas_call(
        body, grid=(T // tm,), in_specs=in_specs, out_specs=out_specs, out_shape=out_shape,
        scratch_shapes=[pltpu.VMEM((c // LANES, tm, LANES), F32) for c in scr_cols],
        compiler_params=_cparams(("arbitrary",)), name=name)(*args, *fulls)
    return [x.reshape(T, x.shape[2]) if i in dil_out else x for i, x in enumerate(res)]


def _rmsnorm_fwd(h, w, name, dils=()):
    D = h.shape[1]

    def fn(h, w):
        r = lax.rsqrt(jnp.mean(h * h, axis=-1, keepdims=True) + EPS)
        xn = h * r * w
        return [xn] * (1 + len(dils)), []
    return _rowwise(fn, [h], [w.reshape(1, -1)], [(D, BF16)] + [(D, BF16, d) for d in dils], tm=512, name=name)


def _rmsnorm_bwd(dxn, h, w, dres, name):
    def fn(dxn, h, dres, w):
        r = lax.rsqrt(jnp.mean(h * h, axis=-1, keepdims=True) + EPS)
        n = h * r
        dn = dxn * w
        dh = r * (dn - n * jnp.mean(dn * n, axis=-1, keepdims=True)) + dres
        return [dh], [jnp.sum(dxn * n, axis=0, keepdims=True)]
    D = h.shape[1]
    return _rowwise(fn, [dxn, h, dres], [w.reshape(1, -1)], [(D, F32)], [(1, D)], tm=256, name=name)


def _loss_head(h, w, tgt, name):
    D = h.shape[1]

    def fn(h, tgt, w):
        r = lax.rsqrt(jnp.mean(h * h, axis=-1, keepdims=True) + EPS)
        n = h * r
        e = n * w - tgt
        row_loss = 0.5 * jnp.mean(e * e, axis=-1, keepdims=True)
        dy = e * (1.0 / D)
        dn = dy * w
        dh = r * (dn - n * jnp.mean(dn * n, axis=-1, keepdims=True))
        return [dh], [jnp.sum(dy * n, axis=0, keepdims=True), jnp.broadcast_to(jnp.sum(row_loss, axis=0, keepdims=True), (1, LANES))]
    return _rowwise(fn, [h, tgt], [w.reshape(1, -1)], [(D, F32)], [(1, D), (1, LANES)], tm=256, name=name)


def _combine_fwd(os_, ls, name):
    def fn(o0, o1, o2, l0, l1, l2):
        m = jnp.maximum(jnp.maximum(l0, l1), l2)
        e0, e1, e2 = jnp.exp(l0 - m), jnp.exp(l1 - m), jnp.exp(l2 - m)
        attn = (e0 * o0 + e1 * o1 + e2 * o2) / (e0 + e1 + e2)
        return [attn, attn], []
    dil = [("dil", t, d) for t, (_, d) in zip(list(os_) + list(ls), DILATED_GROUPS * 2)]
    return _rowwise(fn, dil, [], [(ATTN_OUT_WIDTH, BF16), (ATTN_OUT_WIDTH, F32)], tm=512, name=name)


def _combine_bwd(dattn, attn, ls, head_ones, name):
    def fn(dattn, attn, l0, l1, l2, ones):
        m = jnp.maximum(jnp.maximum(l0, l1), l2)
        e0, e1, e2 = jnp.exp(l0 - m), jnp.exp(l1 - m), jnp.exp(l2 - m)
        inv = 1.0 / (e0 + e1 + e2)
        t = _dotx_r(dattn * attn, ones, parts=2)
        outs = []
        for e in (e0, e1, e2):
            al = e * inv
            outs += [al * dattn, al * t]
        return outs, []
    W = ATTN_OUT_WIDTH
    dil = [("dil", t, d) for t, (_, d) in zip(ls, DILATED_GROUPS)]
    outs = [(W, F32, d) for _, d in DILATED_GROUPS for _ in range(2)]
    return _rowwise(fn, [dattn, attn] + dil, [head_ones], outs, tm=512, name=name)


def _dt_fwd(dt_raw, dt_bias, name):
    def fn(raw, b):
        z = raw[:, :N_SSM_HEADS] + b
        return [jnp.maximum(z, 0.0) + jnp.log(1.0 + jnp.exp(-jnp.abs(z)))], []
    return _rowwise(fn, [dt_raw], [dt_bias.reshape(1, -1)], [(N_SSM_HEADS, F32)], tm=1024, name=name)[0]


def _dt_bwd(ddt_a, ddt_b, dt_raw, dt_bias, name):
    def fn(da, db, raw, b):
        g = (da + db) * jax.nn.sigmoid(raw[:, :N_SSM_HEADS] + b)
        pad = jnp.zeros((g.shape[0], LANES - N_SSM_HEADS), F32)
        return [jnp.concatenate([g, pad], axis=1)], [jnp.sum(g, axis=0, keepdims=True)]
    return _rowwise(fn, [ddt_a, ddt_b, dt_raw], [dt_bias.reshape(1, -1)], [(LANES, BF16)], [(1, N_SSM_HEADS)], tm=1024, name=name)


def _ssm_norm_fwd(y, z, w, name):
    G = D_INNER // N_SSM_GROUPS

    def fn(y, z, w):
        yg = y * _silu(z)
        outs = []
        for g in range(N_SSM_GROUPS):
            t = yg[:, g * G:(g + 1) * G]
            outs.append(t * lax.rsqrt(jnp.mean(t * t, axis=-1, keepdims=True) + EPS))
        return [jnp.concatenate(outs, axis=1) * w], []
    return _rowwise(fn, [y, z], [w.reshape(1, -1)], [(D_INNER, BF16)], tm=256, name=name)[0]


def _ssm_norm_bwd(dssm, y, z, w, name):
    G = D_INNER // N_SSM_GROUPS

    def fn(dssm, y, z, w):
        sz = _silu(z)
        yg = y * sz
        dn = dssm * w
        ns, dygs = [], []
        for g in range(N_SSM_GROUPS):
            t = yg[:, g * G:(g + 1) * G]
            r = lax.rsqrt(jnp.mean(t * t, axis=-1, keepdims=True) + EPS)
            n = t * r
            d = dn[:, g * G:(g + 1) * G]
            dygs.append(r * (d - n * jnp.mean(d * n, axis=-1, keepdims=True)))
            ns.append(n)
        n, dyg = jnp.concatenate(ns, axis=1), jnp.concatenate(dygs, axis=1)
        return [dyg * sz, dyg * y * _dsilu(z)], [jnp.sum(dssm * n, axis=0, keepdims=True)]
    return _rowwise(fn, [dssm, y, z], [w.reshape(1, -1)], [(D_INNER, F32), (D_INNER, BF16)], [(1, D_INNER)], tm=256, name=name)


def _gate_out_proj(a, sb, gl, h, w_out, name):
    def fn(a, sb, gl, h, w):
        g = jax.nn.sigmoid(gl)
        merged = (g[:, :D_MODEL] * a + g[:, D_MODEL:] * sb).astype(BF16)
        return [h + jnp.dot(merged, w, preferred_element_type=F32), merged], []
    return _rowwise(fn, [a, sb, gl, h], [w_out], [(D_MODEL, F32), (D_MODEL, BF16)], tm=512, cap=False, name=name)


def _d_out_proj_gate(dh, a, sb, gl, w_out, name):
    def fn(dh, a, sb, gl, w):
        dm = lax.dot_general(dh.astype(BF16), w, NT, preferred_element_type=F32)
        g = jax.nn.sigmoid(gl)
        g0, g1 = g[:, :D_MODEL], g[:, D_MODEL:]
        dgl = jnp.concatenate([dm * a * g0 * (1.0 - g0), dm * sb * g1 * (1.0 - g1)], axis=1)
        return [g0 * dm, g1 * dm, dgl], []
    return _rowwise(fn, [dh, a, sb, gl], [w_out], [(D_MODEL, BF16), (D_MODEL, BF16), (2 * D_MODEL, BF16)], tm=512, cap=False, name=name)


FFN_HALF = D_FF // 2


def _ffn_perm(w):
    h = FFN_HALF
    return jnp.concatenate([w[:, 0:h], w[:, D_FF:D_FF + h], w[:, h:D_FF], w[:, D_FF + h:]], axis=1)


def _ffn_unperm(w):
    h = FFN_HALF
    return jnp.concatenate([w[:, 0:h], w[:, 2 * h:3 * h], w[:, h:2 * h], w[:, 3 * h:]], axis=1)


def _swiglu_epilogue(res):
    return [res, _silu(res[:, :FFN_HALF]) * res[:, FFN_HALF:]]


def _dswiglu_epilogue(dact, u):
    u = u.astype(F32)
    gate, up = u[:, :FFN_HALF], u[:, FFN_HALF:]
    return [jnp.concatenate([dact * up * _dsilu(gate), dact * _silu(gate)], axis=1)]


def _adamw(w, g, m, v, name):
    c1 = 1.0 - ADAM_B1 ** ADAM_STEP
    c2 = 1.0 - ADAM_B2 ** ADAM_STEP

    def fn(w, g, m, v):
        m = ADAM_B1 * m + (1.0 - ADAM_B1) * g
        v = ADAM_B2 * v + (1.0 - ADAM_B2) * (g * g)
        delta = -ADAM_LR * ((m / c1) / (jnp.sqrt(v / c2) + ADAM_EPS) + ADAM_WD * w)
        return [delta, m, v], []
    C = w.shape[1]
    return _rowwise(fn, [w, g, m, v], [], [(C, F32)] * 3, tm=256, name=name)


def _bias_consts(dilation, n_steps):
    qi = np.arange(ATTN_BLOCK)[:, None]
    kj = np.arange(2 * ATTN_BLOCK)[None, :]
    steps = qi + ATTN_BLOCK - kj
    valid = (steps >= 0) & (steps <= n_steps)
    dist = jnp.asarray(np.clip(steps, 0, n_steps) * dilation, jnp.int32)
    max_exact = N_REL_BUCKETS // 2
    d_f = jnp.maximum(dist, 1).astype(F32)
    large = max_exact + (jnp.log(d_f / max_exact) / math.log(REL_MAX_DISTANCE / max_exact)
                         * (N_REL_BUCKETS - max_exact)).astype(jnp.int32)
    large = jnp.minimum(large, N_REL_BUCKETS - 1)
    bucket = jnp.where(dist < max_exact, dist, large).reshape(-1)
    onehot = (bucket[None, :] == jnp.arange(N_REL_BUCKETS)[:, None]).astype(F32)
    return onehot, jnp.asarray(valid.reshape(1, -1), F32)


def _bias_gather(rel_g_t, onehot, valid, name):
    def body(r_ref, oh_ref, v_ref, o_ref):
        b = jnp.dot(r_ref[...], oh_ref[...], preferred_element_type=F32, precision=lax.Precision.HIGHEST)
        o_ref[...] = jnp.where(v_ref[...] > 0.5, b, NEG)
    return pl.pallas_call(body, out_shape=S((HEADS_PER_GROUP, onehot.shape[1]), F32), compiler_params=_cparams(), name=name)(rel_g_t, onehot, valid)


def _bias_scatter(dbias, onehot, name):
    def body(d_ref, oh_ref, o_ref):
        o_ref[...] = lax.dot_general(d_ref[...], oh_ref[...], NT, preferred_element_type=F32, precision=lax.Precision.HIGHEST)
    return pl.pallas_call(body, out_shape=S((HEADS_PER_GROUP, N_REL_BUCKETS), F32), compiler_params=_cparams(), name=name)(dbias, onehot)


ATTN_QB_FWD, ATTN_QB_BWD = 4, 4


def _attn_tiles(T, d, qb):
    seg = T // d
    nqb = min(qb, seg // ATTN_BLOCK)
    tq = nqb * ATTN_BLOCK
    return seg, nqb, tq, seg // tq


def _attn_fwd(qkv, bias, d, name):
    T = qkv.shape[0]
    seg, nqb, tq, ns = _attn_tiles(T, d, ATTN_QB_FWD)
    W = ATTN_OUT_WIDTH
    scale = HEAD_DIM ** -0.5

    def body(q_ref, kh_ref, kc_ref, vh_ref, vc_ref, b_ref, o_ref, l_ref, s_scr, p_scr):
        n = pl.program_id(1)
        qv = q_ref[...]
        kk = jnp.concatenate([kh_ref[...], kc_ref[...]], axis=0)
        vv = jnp.concatenate([vh_ref[...], vc_ref[...]], axis=0)
        col = lax.broadcasted_iota(jnp.int32, (ATTN_BLOCK, 2 * ATTN_BLOCK), 1)
        kill = jnp.logical_and(n == 0, col < ATTN_BLOCK)
        H = HEADS_PER_GROUP
        for j in range(nqb):
            rows = slice(j * ATTN_BLOCK, (j + 1) * ATTN_BLOCK)
            keys = slice(j * ATTN_BLOCK, (j + 2) * ATTN_BLOCK)
            for h in range(H):
                sl = slice(h * HEAD_DIM, (h + 1) * HEAD_DIM)
                s_scr[h] = lax.dot_general(qv[rows, sl], kk[keys, sl], NT, preferred_element_type=F32)
            s = s_scr[...] * scale + b_ref[...]
            if j == 0:
                s = jnp.where(kill[None], NEG, s)
            m = jnp.max(s, axis=-1, keepdims=True)
            p = jnp.exp(s - m)
            den = jnp.sum(p, axis=-1, keepdims=True)
            p_scr[...] = p.astype(BF16)
            inv = 1.0 / den
            lse = m + jnp.log(den)
            for h in range(H):
                sl = slice(h * HEAD_DIM, (h + 1) * HEAD_DIM)
                o_ref[rows, sl] = jnp.dot(p_scr[h], vv[keys, sl], preferred_element_type=F32) * inv[h]
                l_ref[rows, sl] = jnp.broadcast_to(lse[h], (ATTN_BLOCK, HEAD_DIM))

    def cur(c):
        return pl.BlockSpec((tq, W), lambda r, n: (r * ns + n, c))

    def halo(c):
        return pl.BlockSpec((ATTN_BLOCK, W), lambda r, n: (jnp.maximum((r * ns + n) * nqb - 1, 0), c))

    return pl.pallas_call(
        body, grid=(d, ns),
        in_specs=[cur(0), halo(1), cur(1), halo(2), cur(2), pl.BlockSpec(bias.shape, lambda r, n: (0, 0, 0))],
        out_specs=[cur(0), cur(0)], out_shape=[S((T, W), F32)] * 2,
        scratch_shapes=[pltpu.VMEM((HEADS_PER_GROUP, ATTN_BLOCK, 2 * ATTN_BLOCK), F32),
                        pltpu.VMEM((HEADS_PER_GROUP, ATTN_BLOCK, 2 * ATTN_BLOCK), BF16)],
        compiler_params=_cparams(("parallel", "arbitrary")), name=name)(qkv, qkv, qkv, qkv, qkv, bias)


def _attn_bwd(qkv, bias, lse, do, dd, d, name):
    T = qkv.shape[0]
    seg, nqb, tq, ns = _attn_tiles(T, d, ATTN_QB_BWD)
    W = ATTN_OUT_WIDTH
    B = ATTN_BLOCK
    scale = HEAD_DIM ** -0.5

    def body(q_ref, kh_ref, kc_ref, vh_ref, vc_ref, b_ref, l_ref, do_ref, dd_ref, dq_ref, dk_ref, dv_ref, db_ref, pk_ref, pv_ref,
             s_scr, dp_scr, p_scr, ds_scr):
        r, n = pl.program_id(0), pl.program_id(1)

        @pl.when(jnp.logical_and(r == 0, n == 0))
        def _():
            db_ref[...] = jnp.zeros_like(db_ref)

        @pl.when(n == 0)
        def _():
            pk_ref[...] = jnp.zeros_like(pk_ref)
            pv_ref[...] = jnp.zeros_like(pv_ref)

        @pl.when(n < ns)
        def _():
            qv = q_ref[...]
            kk = jnp.concatenate([kh_ref[...], kc_ref[...]], axis=0)
            vv = jnp.concatenate([vh_ref[...], vc_ref[...]], axis=0)
            lse_v, do_v, dd_v = l_ref[...], do_ref[...], dd_ref[...]
            col = lax.broadcasted_iota(jnp.int32, (B, 2 * B), 1)
            kill = jnp.logical_and(n == 0, col < B)
            dqs = [[None] * HEADS_PER_GROUP for _ in range(nqb)]
            dks = [[None] * HEADS_PER_GROUP for _ in range(nqb)]
            dvs = [[None] * HEADS_PER_GROUP for _ in range(nqb)]
            H = HEADS_PER_GROUP
            do_b = do_v.astype(BF16)
            for j in range(nqb):
                rows = slice(j * B, (j + 1) * B)
                keys = slice(j * B, (j + 2) * B)
                for h in range(H):
                    sl = slice(h * HEAD_DIM, (h + 1) * HEAD_DIM)
                    s_scr[h] = lax.dot_general(qv[rows, sl], kk[keys, sl], NT, preferred_element_type=F32)
                    dp_scr[h] = lax.dot_general(do_b[rows, sl], vv[keys, sl], NT, preferred_element_type=F32)
                lse_h = jnp.stack([lse_v[rows, h * HEAD_DIM:h * HEAD_DIM + 1] for h in range(H)], axis=0)
                dd_h = jnp.stack([dd_v[rows, h * HEAD_DIM:h * HEAD_DIM + 1] for h in range(H)], axis=0)
                s = s_scr[...] * scale + b_ref[...]
                if j == 0:
                    s = jnp.where(kill[None], NEG, s)
                p = jnp.exp(s - lse_h)
                ds = p * (dp_scr[...] - dd_h)
                db_ref[...] += ds
                p_scr[...] = p.astype(BF16)
                ds_scr[...] = (ds * scale).astype(BF16)
                for h in range(H):
                    sl = slice(h * HEAD_DIM, (h + 1) * HEAD_DIM)
                    dvs[j][h] = lax.dot_general(p_scr[h], do_b[rows, sl], TN, preferred_element_type=F32)
                    dqs[j][h] = jnp.dot(ds_scr[h], kk[keys, sl], preferred_element_type=F32)
                    dks[j][h] = lax.dot_general(ds_scr[h], qv[rows, sl], TN, preferred_element_type=F32)
            dq_ref[...] = jnp.concatenate([jnp.concatenate(dqs[j], axis=1) for j in range(nqb)], axis=0).astype(dq_ref.dtype)
            for parts, out_ref, pend in ((dks, dk_ref, pk_ref), (dvs, dv_ref, pv_ref)):
                full = [jnp.concatenate(parts[j], axis=1) for j in range(nqb)]
                if tq > B:
                    out_ref[:tq - B] = pend[:tq - B].astype(out_ref.dtype)
                out_ref[tq - B:] = (pend[tq - B:] + full[0][:B]).astype(out_ref.dtype)
                for j in range(nqb - 1):
                    pend[j * B:(j + 1) * B] = full[j][B:] + full[j + 1][:B]
                pend[tq - B:] = full[nqb - 1][B:]

        @pl.when(n == ns)
        def _():
            dk_ref[...] = pk_ref[...].astype(dk_ref.dtype)
            dv_ref[...] = pv_ref[...].astype(dv_ref.dtype)

    def cur(c):
        return pl.BlockSpec((tq, W), lambda r, n: (r * ns + jnp.minimum(n, ns - 1), c))

    def halo(c):
        return pl.BlockSpec((B, W), lambda r, n: (jnp.maximum((r * ns + jnp.minimum(n, ns - 1)) * nqb - 1, 0), c))

    late = pl.BlockSpec((tq, W), lambda r, n: (r * ns + jnp.clip(n - 1, 0, ns - 1), 0))
    bspec = pl.BlockSpec(bias.shape, lambda r, n: (0, 0, 0))
    return pl.pallas_call(
        body, grid=(d, ns + 1),
        in_specs=[cur(0), halo(1), cur(1), halo(2), cur(2), bspec, cur(0), cur(0), cur(0)],
        out_specs=[cur(0), late, late, bspec],
        out_shape=[S((T, W), BF16)] * 3 + [S(bias.shape, F32)],
        scratch_shapes=[pltpu.VMEM((tq, W), F32), pltpu.VMEM((tq, W), F32)]
                       + [pltpu.VMEM((HEADS_PER_GROUP, B, 2 * B), t) for t in (F32, F32, BF16, BF16)],
        compiler_params=_cparams(("arbitrary", "arbitrary")), name=name,
    )(qkv, qkv, qkv, qkv, qkv, bias, lse, do, dd)


CONV_TM, CONV_TC = 512, 1024


def _shift_down(x, halo8, s, row8):
    xr = pltpu.roll(x, s, 0)
    first = jnp.where(row8 < s, pltpu.roll(halo8, s, 0), xr[:8])
    return jnp.concatenate([first, xr[8:]], axis=0)


def _shift_up(x, halo8, s, row8):
    n = x.shape[0]
    xr = pltpu.roll(x, n - s, 0)
    last = jnp.where(row8 >= 8 - s, pltpu.roll(halo8, 8 - s, 0), xr[n - 8:])
    return jnp.concatenate([xr[:n - 8], last], axis=0)


def _conv_fwd(x, w, b, name):
    T, C = x.shape
    tm, tc = min(CONV_TM, T), CONV_TC

    def body(x_ref, p_ref, w_ref, b_ref, u_ref, a_ref):
        ti = pl.program_id(1)
        xv = x_ref[...]
        p8 = jnp.where(ti == 0, 0.0, p_ref[...])
        wv = w_ref[...]
        row8 = lax.broadcasted_iota(jnp.int32, (8, tc), 0)
        u = xv * wv[3:4] + b_ref[...]
        for s in (1, 2, 3):
            u = u + _shift_down(xv, p8, s, row8) * wv[3 - s:4 - s]
        u_ref[...] = u
        a_ref[...] = _silu(u)

    cur = pl.BlockSpec((tm, tc), lambda cj, ti: (ti, cj))
    halo = pl.BlockSpec((8, tc), lambda cj, ti: (jnp.maximum(ti * (tm // 8) - 1, 0), cj))
    return pl.pallas_call(
        body, grid=(C // tc, T // tm),
        in_specs=[cur, halo, pl.BlockSpec((CONV_WIDTH, tc), lambda cj, ti: (0, cj)), pl.BlockSpec((1, tc), lambda cj, ti: (0, cj))],
        out_specs=[cur, cur], out_shape=[S((T, C), F32)] * 2,
        compiler_params=_cparams(("parallel", "arbitrary")), name=name)(x, x, w, b)


def _conv_bwd(dact, u, x, w, name):
    T, C = x.shape
    tm, tc = min(CONV_TM, T), CONV_TC
    nt = T // tm

    def body(d_ref, dn_ref, u_ref, un_ref, x_ref, w_ref, dx_ref, dw_ref, db_ref):
        ti = pl.program_id(1)

        @pl.when(ti == 0)
        def _():
            dw_ref[...] = jnp.zeros_like(dw_ref)
            db_ref[...] = jnp.zeros_like(db_ref)

        du = d_ref[...] * _dsilu(u_ref[...])
        dun = jnp.where(ti == nt - 1, 0.0, dn_ref[...] * _dsilu(un_ref[...]))
        xv = x_ref[...]
        wv = w_ref[...]
        row8 = lax.broadcasted_iota(jnp.int32, (8, tc), 0)
        dx = du * wv[3:4]
        dws = [None] * CONV_WIDTH
        dws[3] = jnp.sum(du * xv, axis=0, keepdims=True)
        for s in (1, 2, 3):
            up = _shift_up(du, dun, s, row8)
            dx = dx + up * wv[3 - s:4 - s]
            dws[3 - s] = jnp.sum(up * xv, axis=0, keepdims=True)
        dx_ref[...] = dx.astype(dx_ref.dtype)
        dw_ref[...] += jnp.concatenate(dws, axis=0)
        db_ref[...] += jnp.sum(du, axis=0, keepdims=True)

    cur = pl.BlockSpec((tm, tc), lambda cj, ti: (ti, cj))
    nxt = pl.BlockSpec((8, tc), lambda cj, ti: (jnp.minimum((ti + 1) * (tm // 8), T // 8 - 1), cj))
    return pl.pallas_call(
        body, grid=(C // tc, nt),
        in_specs=[cur, nxt, cur, nxt, cur, pl.BlockSpec((CONV_WIDTH, tc), lambda cj, ti: (0, cj))],
        out_specs=[cur, pl.BlockSpec((CONV_WIDTH, tc), lambda cj, ti: (0, cj)), pl.BlockSpec((1, tc), lambda cj, ti: (0, cj))],
        out_shape=[S((T, C), BF16), S((CONV_WIDTH, C), F32), S((1, C), F32)],
        compiler_params=_cparams(("parallel", "arbitrary")), name=name)(dact, dact, u, u, x, w)


def _ssd_consts():
    i = np.arange(SSD_CHUNK)
    tril = (i[None, :] <= i[:, None]).astype(np.float32)
    trils = (i[None, :] < i[:, None]).astype(np.float32)
    head = np.repeat(np.arange(N_SSM_HEADS), D_INNER // N_SSM_HEADS)
    et = (head[None, :] == np.arange(N_SSM_HEADS)[:, None]).astype(np.float32)
    c = lambda a: jnp.asarray(a, BF16)
    return dict(tril=c(tril), triu=c(tril.T), trils=c(trils), trius=c(trils.T), et=c(et), e=c(et.T))


def _ssd_common(act_ref, dt_ref, dtT_ref, al_ref, alT_ref, tril_ref, triu_ref, et_ref):
    a_row = -jnp.exp(al_ref[...])
    a_col = -jnp.exp(alT_ref[...])
    dt, dtT = dt_ref[...], dtT_ref[...]
    la = _dotx_l(tril_ref[...], dt * a_row)
    laT = _dotx_r(dtT * a_col, triu_ref[...])
    et = et_ref[...]
    la_e = _dotx_r(la, et)
    dt_e = _dotx_r(dt, et, parts=2)
    x = act_ref[:, :D_INNER]
    xdt = x * dt_e
    la_q = la_e[SSD_CHUNK - 1:SSD_CHUNK, :]
    return a_row, a_col, dt, dtT, la, laT, la_e, dt_e, x, xdt, la_q


def _decay(la, laT, h, causal):
    seg = la[:, h:h + 1] - laT[h:h + 1, :]
    return jnp.exp(jnp.where(causal, seg, NEG))


def _ssd_fwd(act, dt, dtT, alog, dskip_e, cs, name):
    T = act.shape[0]
    nc = T // SSD_CHUNK
    Q, G, GW = SSD_CHUNK, N_SSM_GROUPS, D_INNER // N_SSM_GROUPS

    def body(act_ref, dt_ref, dtT_ref, al_ref, alT_ref, dsk_ref, tril_ref, triu_ref, et_ref, y_ref, st_ref, scr, m_scr):
        @pl.when(pl.program_id(0) == 0)
        def _():
            scr[...] = jnp.zeros_like(scr)
        st_ref[0] = scr[...]
        a_row, a_col, dtv, dtTv, la, laT, la_e, dt_e, x, xdt, la_q = _ssd_common(
            act_ref, dt_ref, dtT_ref, al_ref, alT_ref, tril_ref, triu_ref, et_ref)
        ela = jnp.exp(la_e)
        xdt_b = xdt.astype(BF16)
        xdte_b = (xdt * jnp.exp(la_q - la_e)).astype(BF16)
        ela_q = jnp.exp(la_q)
        causal = lax.broadcasted_iota(jnp.int32, (Q, Q), 0) >= lax.broadcasted_iota(jnp.int32, (Q, Q), 1)
        for g in range(G):
            gs = slice(g * GW, (g + 1) * GW)
            Bg = act_ref[:, D_INNER + g * D_STATE:D_INNER + (g + 1) * D_STATE].astype(BF16)
            Cg = act_ref[:, D_INNER + G * D_STATE + g * D_STATE:D_INNER + G * D_STATE + (g + 1) * D_STATE].astype(BF16)
            cb = lax.dot_general(Cg, Bg, NT, preferred_element_type=F32)
            st = scr[g]
            y_inter = jnp.dot(Cg, st.astype(BF16), preferred_element_type=F32) * ela[:, gs]
            for hh in range(HEADS_PER_GROUP):
                m_scr[hh] = (cb * _decay(la, laT, g * HEADS_PER_GROUP + hh, causal)).astype(BF16)
            ys = []
            for hh in range(HEADS_PER_GROUP):
                h = g * HEADS_PER_GROUP + hh
                ys.append(jnp.dot(m_scr[hh], xdt_b[:, h * HEAD_DIM:(h + 1) * HEAD_DIM], preferred_element_type=F32))
            y_ref[:, gs] = jnp.concatenate(ys, axis=1) + y_inter + x[:, gs] * dsk_ref[:, gs]
            scr[g] = st * ela_q[:, gs] + lax.dot_general(Bg, xdte_b[:, gs], TN, preferred_element_type=F32)

    full = lambda a: pl.BlockSpec(a.shape, lambda c: (0,) * a.ndim)
    al, alT = alog.reshape(1, -1), alog.reshape(-1, 1)
    return pl.pallas_call(
        body, grid=(nc,),
        in_specs=[pl.BlockSpec((Q, XBC_WIDTH), lambda c: (c, 0)), pl.BlockSpec((Q, N_SSM_HEADS), lambda c: (c, 0)),
                  pl.BlockSpec((N_SSM_HEADS, Q), lambda c: (0, c)), full(al), full(alT), full(dskip_e),
                  full(cs["tril"]), full(cs["triu"]), full(cs["et"])],
        out_specs=[pl.BlockSpec((Q, D_INNER), lambda c: (c, 0)), pl.BlockSpec((1, G, D_STATE, GW), lambda c: (c, 0, 0, 0))],
        out_shape=[S((T, D_INNER), F32), S((nc, G, D_STATE, GW), F32)],
        scratch_shapes=[pltpu.VMEM((G, D_STATE, GW), F32), pltpu.VMEM((HEADS_PER_GROUP, Q, Q), BF16)],
        compiler_params=_cparams(("arbitrary",)), name=name)(act, dt, dtT, al, alT, dskip_e, cs["tril"], cs["triu"], cs["et"])


def _ssd_bwd(dy, act, dt, dtT, alog, dskip_e, states, cs, name):
    T = act.shape[0]
    nc = T // SSD_CHUNK
    Q, G, GW, H = SSD_CHUNK, N_SSM_GROUPS, D_INNER // N_SSM_GROUPS, N_SSM_HEADS

    def body(dy_ref, act_ref, dt_ref, dtT_ref, al_ref, alT_ref, dsk_ref, stp_ref, tril_ref, triu_ref, trils_ref, trius_ref,
             et_ref, e_ref, dact_ref, ddt_ref, ddtT_ref, da_ref, daT_ref, dsk_out_ref, dst, wbuf, ubuf, vbuf, sbuf, dm_scr, m_scr):
        @pl.when(pl.program_id(0) == 0)
        def _():
            dst[...] = jnp.zeros_like(dst)
            da_ref[...] = jnp.zeros_like(da_ref)
            daT_ref[...] = jnp.zeros_like(daT_ref)
            dsk_out_ref[...] = jnp.zeros_like(dsk_out_ref)
        a_row, a_col, dtv, dtTv, la, laT, la_e, dt_e, x, xdt, la_q = _ssd_common(
            act_ref, dt_ref, dtT_ref, al_ref, alT_ref, tril_ref, triu_ref, et_ref)
        dyv = dy_ref[...]
        ela = jnp.exp(la_e)
        e_end = jnp.exp(la_q - la_e)
        ela_q = jnp.exp(la_q)
        dye_b = (dyv * ela).astype(BF16)
        dy_b = dyv.astype(BF16)
        xdt_b = xdt.astype(BF16)
        xdte_b = (xdt * e_end).astype(BF16)
        ri = lax.broadcasted_iota(jnp.int32, (Q, Q), 0)
        ci = lax.broadcasted_iota(jnp.int32, (Q, Q), 1)
        causal = ri >= ci
        rows = []
        for g in range(G):
            gs = slice(g * GW, (g + 1) * GW)
            Bg = act_ref[:, D_INNER + g * D_STATE:D_INNER + (g + 1) * D_STATE].astype(BF16)
            Cg = act_ref[:, D_INNER + G * D_STATE + g * D_STATE:D_INNER + G * D_STATE + (g + 1) * D_STATE].astype(BF16)
            cb = lax.dot_general(Cg, Bg, NT, preferred_element_type=F32)
            stp = stp_ref[0, g]
            stp_b = stp.astype(BF16)
            dstv = dst[g]
            dst_b = dstv.astype(BF16)
            y_inter = jnp.dot(Cg, stp_b, preferred_element_type=F32) * ela[:, gs]
            wbuf[:, gs] = dyv[:, gs] * y_inter
            dxdt_state = jnp.dot(Bg, dst_b, preferred_element_type=F32) * e_end[:, gs]
            ubuf[:, gs] = dxdt_state * xdt[:, gs]
            dC = lax.dot_general(dye_b[:, gs], stp_b, NT, preferred_element_type=F32)
            dB = lax.dot_general(xdte_b[:, gs], dst_b, NT, preferred_element_type=F32)
            sbuf[:, gs] = jnp.broadcast_to(jnp.sum(dstv * stp, axis=0, keepdims=True), (8, GW))
            dst[g] = dstv * ela_q[:, gs] + lax.dot_general(Cg, dye_b[:, gs], TN, preferred_element_type=F32)
            for hh in range(HEADS_PER_GROUP):
                hs = slice((g * HEADS_PER_GROUP + hh) * HEAD_DIM, (g * HEADS_PER_GROUP + hh + 1) * HEAD_DIM)
                dm_scr[hh] = lax.dot_general(dy_b[:, hs], xdt_b[:, hs], NT, preferred_element_type=F32)
            dG = jnp.zeros((Q, Q), F32)
            for hh in range(HEADS_PER_GROUP):
                L = _decay(la, laT, g * HEADS_PER_GROUP + hh, causal)
                M = cb * L
                dM = dm_scr[hh]
                dG = dG + dM * L
                W = dM * M
                rows.append(jnp.sum(W.T, axis=0, keepdims=True) - jnp.sum(W, axis=0, keepdims=True))
                m_scr[hh] = M.astype(BF16)
            dxs = []
            for hh in range(HEADS_PER_GROUP):
                hs = slice((g * HEADS_PER_GROUP + hh) * HEAD_DIM, (g * HEADS_PER_GROUP + hh + 1) * HEAD_DIM)
                dxs.append(lax.dot_general(m_scr[hh], dy_b[:, hs], TN, preferred_element_type=F32))
            dG_b = dG.astype(BF16)
            dC = dC + jnp.dot(dG_b, Bg, preferred_element_type=F32)
            dB = dB + lax.dot_general(dG_b, Cg, TN, preferred_element_type=F32)
            dxdt = jnp.concatenate(dxs, axis=1) + dxdt_state
            vbuf[:, gs] = dxdt * x[:, gs]
            dact_ref[:, gs] = dxdt * dt_e[:, gs] + dyv[:, gs] * dsk_ref[:, gs]
            dact_ref[:, D_INNER + g * D_STATE:D_INNER + (g + 1) * D_STATE] = dB
            dact_ref[:, D_INNER + G * D_STATE + g * D_STATE:D_INNER + G * D_STATE + (g + 1) * D_STATE] = dC
        e = e_ref[...]
        w = _dotx_r(wbuf[...], e, parts=2)
        u = _dotx_r(ubuf[...], e, parts=2)
        vx = _dotx_r(vbuf[...], e, parts=2)
        dsk = _dotx_r(jnp.broadcast_to(jnp.sum(dyv * x, axis=0, keepdims=True), (8, D_INNER)), e, parts=2)[0:1]
        s0 =_dotx_r(sbuf[...], e, parts=2)[0:1] * jnp.exp(la[Q - 1:Q, :])
        ddelta = _dotx_l(triu_ref[...], w) + _dotx_l(trils_ref[...], u) + s0
        ddt_ref[...] = ddelta * a_row + vx
        ddeltaT = _dotx_r(jnp.concatenate(rows, axis=0), tril_ref[...])
        ddtT_ref[...] = ddeltaT * a_col
        da_ref[...] += jnp.sum(ddelta * dtv, axis=0, keepdims=True)
        daT_ref[...] += jnp.sum(ddeltaT * dtTv, axis=1, keepdims=True)
        dsk_out_ref[...] += dsk

    rev = lambda c: nc - 1 - c
    full = lambda a: pl.BlockSpec(a.shape, lambda c: (0,) * a.ndim)
    al, alT = alog.reshape(1, -1), alog.reshape(-1, 1)
    consts = [cs[k] for k in ("tril", "triu", "trils", "trius", "et", "e")]
    return pl.pallas_call(
        body, grid=(nc,),
        in_specs=[pl.BlockSpec((Q, D_INNER), lambda c: (rev(c), 0)), pl.BlockSpec((Q, XBC_WIDTH), lambda c: (rev(c), 0)),
                  pl.BlockSpec((Q, H), lambda c: (rev(c), 0)), pl.BlockSpec((H, Q), lambda c: (0, rev(c))),
                  full(al), full(alT), full(dskip_e), pl.BlockSpec((1, G, D_STATE, GW), lambda c: (rev(c), 0, 0, 0))]
                 + [full(a) for a in consts],
        out_specs=[pl.BlockSpec((Q, XBC_WIDTH), lambda c: (rev(c), 0)), pl.BlockSpec((Q, H), lambda c: (rev(c), 0)),
                   pl.BlockSpec((H, Q), lambda c: (0, rev(c))), pl.BlockSpec((1, H), lambda c: (0, 0)),
                   pl.BlockSpec((H, 1), lambda c: (0, 0)), pl.BlockSpec((1, H), lambda c: (0, 0))],
        out_shape=[S((T, XBC_WIDTH), F32), S((T, H), F32), S((H, T), F32), S((1, H), F32), S((H, 1), F32), S((1, H), F32)],
        scratch_shapes=[pltpu.VMEM((G, D_STATE, GW), F32), pltpu.VMEM((Q, D_INNER), F32), pltpu.VMEM((Q, D_INNER), F32),
                        pltpu.VMEM((Q, D_INNER), F32), pltpu.VMEM((8, D_INNER), F32),
                        pltpu.VMEM((HEADS_PER_GROUP, Q, Q), F32), pltpu.VMEM((HEADS_PER_GROUP, Q, Q), BF16)],
        compiler_params=_cparams(("arbitrary",)), name=name)(dy, act, dt, dtT, al, alT, dskip_e, states, *consts)


def _a_log_grad(da, daT_row, alog, name):
    def body(a_ref, b_ref, al_ref, o_ref):
        o_ref[...] = (a_ref[...] + b_ref[...]) * (-jnp.exp(al_ref[...]))
    return pl.pallas_call(body, out_shape=S((1, N_SSM_HEADS), F32), name=name)(da, daT_row, alog.reshape(1, -1))


def _place():
    x, y, c = lax.axis_index("x"), lax.axis_index("y"), lax.axis_index("c")
    return x, y, c, [(1 - x, y), (x, 1 - y), (1 - x, 1 - y)]


def _all_gather(shards, name):
    npc = len(shards)

    def body(*refs):
        x_refs, o_refs = refs[:npc], refs[npc:2 * npc]
        send_sems, recv_sems, local_sems = refs[2 * npc:]
        x, y, c, chips = _place()
        me, sib = (x, y, c), (x, y, 1 - c)

        def rows(dev, i):
            return o_refs[i].at[4 * dev[0] + 2 * dev[1] + dev[2]]

        def copy(k, i, block, to, src=None):
            return pltpu.make_async_remote_copy(
                src_ref=rows(block, i) if src is None else src, dst_ref=rows(block, i),
                send_sem=send_sems.at[k, i], recv_sem=recv_sems.at[k, i], device_id=to, device_id_type=MESH)

        mine = [pltpu.make_async_copy(x_refs[i], rows(me, i), local_sems.at[i]) for i in range(npc)]
        for cp in mine:
            cp.start()
        first = []
        for i in range(npc):
            first.append(copy(0, i, me, sib, src=x_refs[i]))
            first += [copy(1 + j, i, me, (*chip, c), src=x_refs[i]) for j, chip in enumerate(chips)]
        for cp in first:
            cp.start()
        passed = []
        for i in range(npc):
            for j, chip in enumerate(chips):
                copy(1 + j, i, (*chip, c), me).wait_recv()
                cp = copy(4 + j, i, (*chip, c), sib)
                cp.start()
                passed.append(cp)
        for i in range(npc):
            copy(0, i, sib, me).wait_recv()
            for j, chip in enumerate(chips):
                copy(4 + j, i, (*chip, 1 - c), me).wait_recv()
        for cp in first + passed:
            cp.wait_send()
        for cp in mine:
            cp.wait()

    anys = pl.BlockSpec(memory_space=pl.ANY)
    return pl.pallas_call(
        body, in_specs=[anys] * npc, out_specs=[anys] * npc, out_shape=[S((N_DEV,) + s.shape, s.dtype) for s in shards],
        scratch_shapes=[pltpu.SemaphoreType.DMA((7, npc)), pltpu.SemaphoreType.DMA((7, npc)), pltpu.SemaphoreType.DMA((npc,))],
        name=name)(*shards)


def _to_sibling(to_sib, name):
    npc = len(to_sib)

    def body(*refs):
        s_refs, o_refs, send_sems, recv_sems = refs[:npc], refs[npc:2 * npc], refs[2 * npc], refs[2 * npc + 1]
        x, y, c, _ = _place()
        cps = [pltpu.make_async_remote_copy(
            src_ref=s_refs[i], dst_ref=o_refs[i], send_sem=send_sems.at[i], recv_sem=recv_sems.at[i],
            device_id=(x, y, 1 - c), device_id_type=MESH) for i in range(npc)]
        for cp in cps:
            cp.start()
        for cp in cps:
            cp.wait()

    anys = pl.BlockSpec(memory_space=pl.ANY)
    return pl.pallas_call(
        body, in_specs=[anys] * npc, out_specs=[anys] * npc, out_shape=[S(s.shape, s.dtype) for s in to_sib],
        scratch_shapes=[pltpu.SemaphoreType.DMA((npc,)), pltpu.SemaphoreType.DMA((npc,))],
        name=name)(*to_sib)


def _to_chips(parts, name):
    npc = len(parts)

    def body(*refs):
        p_refs, o_refs, send_sems, recv_sems = refs[:npc], refs[npc:2 * npc], refs[2 * npc], refs[2 * npc + 1]
        x, y, c, chips = _place()
        cps = [pltpu.make_async_remote_copy(
            src_ref=p_refs[i].at[2 * chip[0] + chip[1]], dst_ref=o_refs[i].at[j], send_sem=send_sems.at[j, i],
            recv_sem=recv_sems.at[j, i], device_id=(*chip, c), device_id_type=MESH)
            for i in range(npc) for j, chip in enumerate(chips)]
        for cp in cps:
            cp.start()
        for cp in cps:
            cp.wait()

    anys = pl.BlockSpec(memory_space=pl.ANY)
    return pl.pallas_call(
        body, in_specs=[anys] * npc, out_specs=[anys] * npc, out_shape=[S((3,) + p.shape[1:], p.dtype) for p in parts],
        scratch_shapes=[pltpu.SemaphoreType.DMA((3, npc)), pltpu.SemaphoreType.DMA((3, npc))],
        name=name)(*parts)


def _reduce_scatter(pieces, name):
    c = lax.axis_index("c")
    x, y = lax.axis_index("x"), lax.axis_index("y")
    by_core = [p.reshape(4, 2, p.shape[1], p.shape[2]) for p in pieces]
    to_sib = [lax.dynamic_index_in_dim(p, 1 - c, axis=1, keepdims=False).astype(BF16) for p in by_core]
    keep = [lax.dynamic_index_in_dim(p, c, axis=1, keepdims=False) for p in by_core]
    from_sib = _to_sibling(to_sib, name + "_d2d")

    def add1(a, b):
        s = a + b.astype(F32)
        return [s, s], []

    def add2(a, b, c_, d_):
        return [((a + b.astype(F32)) + c_.astype(F32)) + d_.astype(F32)], []

    parts, parts_b = [], []
    for i, (k, f) in enumerate(zip(keep, from_sib)):
        _, r, C = k.shape
        p, pb = _rowwise(add1, [k.reshape(4 * r, C), f.reshape(4 * r, C)], [], [(C, F32), (C, BF16)], tm=2048, name=f"{name}_add1_{i}")
        parts.append(p.reshape(4, r, C))
        parts_b.append(pb.reshape(4, r, C))
    got = _to_chips(parts_b, name + "_ici")
    outs = []
    for i, (p, g) in enumerate(zip(parts, got)):
        own = lax.dynamic_index_in_dim(p, 2 * x + y, axis=0, keepdims=False)
        outs.append(_rowwise(add2, [own, g[0], g[1], g[2]], [], [(p.shape[2], F32)], tm=2048, name=f"{name}_add2_{i}")[0])
    return outs


def _all_reduce_small(v, name):
    R, C = v.shape

    def body(x_ref, out_ref, buf, send_sems, recv_sems):
        x, y, c, chips = _place()
        me, sib = (x, y, c), (x, y, 1 - c)

        def rows(dev):
            return buf.at[4 * dev[0] + 2 * dev[1] + dev[2]]

        def copy(k, block, to, src=None):
            return pltpu.make_async_remote_copy(
                src_ref=rows(block) if src is None else src, dst_ref=rows(block),
                send_sem=send_sems.at[k], recv_sem=recv_sems.at[k], device_id=to, device_id_type=MESH)

        buf[4 * x + 2 * y + c] = x_ref[...]
        first = [copy(0, me, sib, src=x_ref)] + [copy(1 + j, me, (*chip, c), src=x_ref) for j, chip in enumerate(chips)]
        for cp in first:
            cp.start()
        passed = [copy(4 + j, (*chip, c), sib) for j, chip in enumerate(chips)]
        for j, chip in enumerate(chips):
            copy(1 + j, (*chip, c), me).wait_recv()
            passed[j].start()
        copy(0, sib, me).wait_recv()
        for j, chip in enumerate(chips):
            copy(4 + j, (*chip, 1 - c), me).wait_recv()
        for cp in first + passed:
            cp.wait_send()
        acc = buf[0]
        for j in range(1, N_DEV):
            acc = acc + buf[j]
        out_ref[...] = acc

    vm = pl.BlockSpec(memory_space=pltpu.VMEM)
    return pl.pallas_call(
        body, in_specs=[vm], out_specs=vm, out_shape=S((R, C), F32),
        scratch_shapes=[pltpu.VMEM((N_DEV, R, C), F32), pltpu.SemaphoreType.DMA((7,)), pltpu.SemaphoreType.DMA((7,))],
        compiler_params=pltpu.CompilerParams(vmem_limit_bytes=VMEM_LIMIT), name=name)(v)


SEG = (("q", 0, 1536), ("k", 1536, 1536), ("v", 3072, 1536), ("z", 4608, 2048), ("xbc", 6656, 3072), ("dt", 9728, 32), ("gl", 9760, 2048))


def _split_w_in(w_in_full):
    out = {}
    for nm, off, n in SEG:
        w = w_in_full[:, off:off + n]
        if nm == "dt":
            w = jnp.pad(w, ((0, 0), (0, LANES - n)))
        out[nm] = w
    W = ATTN_OUT_WIDTH
    out["qkv"] = [jnp.concatenate([out[s][:, g * W:(g + 1) * W] for s in ("q", "k", "v")], axis=1) for g in range(N_DIL)]
    out["qkv_t"] = [[out[s][:, g * W:(g + 1) * W] for s in ("q", "k", "v")] for g in range(N_DIL)]
    return out


def _layer_fwd(h, p, W, biases, cs, l):
    T = h.shape[0]
    nm = lambda s: f"{s}_l{l}"
    sv = {"h_in": h}
    xns = _rmsnorm_fwd(h, p["norm1_w"], nm("norm1"), dils=[d for _, d in DILATED_GROUPS[1:]])
    xn = xns[0]
    wi = W["w_in"]
    z = _mm(xn, wi["z"], out_dtype=BF16, name=nm("proj_z"))
    xbc = _mm(xn, wi["xbc"], name=nm("proj_xbc"))
    dt_raw = _mm(xn, wi["dt"], name=nm("proj_dt"))
    gl = _mm(xn, wi["gl"], out_dtype=BF16, name=nm("proj_gl"))
    os_, ls, qkvs = [], [], []
    for g, (window, dil) in enumerate(DILATED_GROUPS):
        qkv = _mm(xns[g], wi["qkv"][g], out_dtype=BF16, name=nm(f"proj_qkv_g{g}"))
        o, lse = _attn_fwd(qkv, biases[g], dil, nm(f"attn_fwd_g{g}"))
        os_.append(o)
        ls.append(lse)
        qkvs.append(qkv)
    attn_b, attn_f = _combine_fwd(os_, ls, nm("combine"))
    u_conv, act = _conv_fwd(xbc, p["conv_w"], p["conv_b"].reshape(1, -1), nm("conv"))
    dt = _dt_fwd(dt_raw, p["dt_bias"], nm("dt"))
    dtT = dt.T
    dskip_e = jnp.repeat(p["d_skip"], D_INNER // N_SSM_HEADS).reshape(1, -1)
    y, states = _ssd_fwd(act, dt, dtT, p["a_log"], dskip_e, cs, nm("ssd_fwd"))
    ssm = _ssm_norm_fwd(y, z, p["ssm_norm_w"], nm("ssm_norm"))
    a_br = _mm(attn_b, W["w_attn_branch"], out_dtype=BF16, name=nm("attn_branch"))
    s_br = _mm(ssm, W["w_ssm_branch"], out_dtype=BF16, name=nm("ssm_branch"))
    h_mid, merged = _gate_out_proj(a_br, s_br, gl, h, W["w_out"], nm("gate_out_proj"))
    xn2 = _rmsnorm_fwd(h_mid, p["norm2_w"], nm("norm2"))[0]
    u_ffn, ffn_act = _mm(xn2, W["w_ffn_in_p"], tm=512, tn=D_FF, epilogue=_swiglu_epilogue, outs=[(2 * D_FF, BF16), (D_FF, BF16)],
                         name=nm("ffn_in_swiglu"))
    h_out = _mm(ffn_act, W["w_ffn_out"], acc=h_mid, name=nm("ffn_out"))
    sv.update(xn=xn, xns=xns, qkvs=qkvs, z=z, xbc=xbc, dt_raw=dt_raw, gl=gl, ls=ls, attn_b=attn_b, attn_f=attn_f, u_conv=u_conv,
              act=act, dt=dt, dtT=dtT, dskip_e=dskip_e, y=y, states=states, ssm=ssm, a_br=a_br, s_br=s_br, merged=merged,
              h_mid=h_mid, xn2=xn2, u_ffn=u_ffn, ffn_act=ffn_act)
    return h_out, sv


def _layer_bwd(dh, sv, p, W, biases, cs, head_ones, l):
    T = dh.shape[0]
    nm = lambda s: f"{s}_l{l}"
    gr = {}
    du = _mm(dh, W["w_ffn_out"], tb=True, tm=512, tn=FFN_HALF, extras=[sv["u_ffn"]], epilogue=_dswiglu_epilogue, outs=[(2 * D_FF, BF16)],
             name=nm("d_ffn_act_swiglu"))
    gr["w_ffn_out"] = _mm(sv["ffn_act"], dh, ta=True, name=nm("g_ffn_out"))
    dxn2 = _mm(du, W["w_ffn_in_p"], tb=True, name=nm("d_xn2"))
    gr["w_ffn_in"] = _ffn_unperm(_mm(sv["xn2"], du, ta=True, name=nm("g_ffn_in")))
    dh_mid, gr["norm2_w"] = _rmsnorm_bwd(dxn2, sv["h_mid"], p["norm2_w"], dh, nm("d_norm2"))
    gr["w_out"] = _mm(sv["merged"], dh_mid, ta=True, name=nm("g_out"))
    d_a, d_s, dgl = _d_out_proj_gate(dh_mid, sv["a_br"], sv["s_br"], sv["gl"], W["w_out"], nm("d_out_proj_gate"))
    dattn = _mm(d_a, W["w_attn_branch"], tb=True, out_dtype=BF16, name=nm("d_attn"))
    gr["w_attn_branch"] = _mm(sv["attn_b"], d_a, ta=True, name=nm("g_attn_branch"))
    dssm = _mm(d_s, W["w_ssm_branch"], tb=True, out_dtype=BF16, name=nm("d_ssm"))
    gr["w_ssm_branch"] = _mm(sv["ssm"], d_s, ta=True, name=nm("g_ssm_branch"))
    dy, dz, gr["ssm_norm_w"] = _ssm_norm_bwd(dssm, sv["y"], sv["z"], p["ssm_norm_w"], nm("d_ssm_norm"))
    dact_c, ddt_a, ddt_bT, da, daT, dskip = _ssd_bwd(dy, sv["act"], sv["dt"], sv["dtT"], p["a_log"], sv["dskip_e"], sv["states"], cs, nm("ssd_bwd"))
    gr["a_log"] = _a_log_grad(da, daT.T, p["a_log"], nm("g_a_log")).reshape(-1)
    gr["d_skip"] = dskip.reshape(-1)
    ddt_raw, ddt_bias = _dt_bwd(ddt_a, ddt_bT.T, sv["dt_raw"], p["dt_bias"], nm("d_dt"))
    gr["dt_bias"] = ddt_bias.reshape(-1)
    dxbc, gr["conv_w"], dconv_b = _conv_bwd(dact_c, sv["u_conv"], sv["xbc"], p["conv_w"], nm("d_conv"))
    gr["conv_b"] = dconv_b.reshape(-1)
    outs = _combine_bwd(dattn, sv["attn_f"], sv["ls"], head_ones, nm("d_combine"))
    wi = W["w_in"]
    dbias, dxn = [], None
    gqkv = [[None] * N_DIL for _ in range(3)]
    for g, (window, dil) in enumerate(DILATED_GROUPS):
        dq, dk, dv, db = _attn_bwd(sv["qkvs"][g], biases[g], sv["ls"][g], outs[2 * g], outs[2 * g + 1], dil, nm(f"attn_bwd_g{g}"))
        dbias.append(db)
        dxn = _mm_dil([dq, dk, dv], wi["qkv_t"][g], dil, dxn, nm(f"d_xn_qkv_g{g}"))
        for i, dseg in enumerate((dq, dk, dv)):
            gqkv[i][g] = _mm(sv["xns"][g], dseg, ta=True, name=nm(f"g_in_{'qkv'[i]}_g{g}"))
    parts = (("z", dz), ("xbc", dxbc), ("dt", ddt_raw), ("gl", dgl))
    gws = gqkv[0] + gqkv[1] + gqkv[2]
    for sname, dseg in parts:
        dxn = _mm(dseg, wi[sname], tb=True, acc=dxn, name=nm("d_xn_" + sname))
        gw = _mm(sv["xn"], dseg, ta=True, name=nm("g_in_" + sname))
        gws.append(gw[:, :N_SSM_HEADS] if sname == "dt" else gw)
    gr["w_in"] = jnp.concatenate(gws, axis=1)
    dh_in, gr["norm1_w"] = _rmsnorm_bwd(dxn, sv["h_in"], p["norm1_w"], dh_mid, nm("d_norm1"))
    return dh_in, gr, dbias


def _step_local(x, tgt, small, Wfull, rel_bias, final_norm_w):
    cs = _ssd_consts()
    head = np.repeat(np.arange(HEADS_PER_GROUP), HEAD_DIM)
    head_ones = jnp.asarray(head[:, None] == head[None, :], BF16)
    biases, onehots = [], []
    for g, (window, dil) in enumerate(DILATED_GROUPS):
        onehot, valid = _bias_consts(dil, window // dil)
        rel_g_t = rel_bias[:, g * HEADS_PER_GROUP:(g + 1) * HEADS_PER_GROUP].T
        b = _bias_gather(rel_g_t, onehot, valid, f"bias_gather_g{g}")
        biases.append(b.reshape(HEADS_PER_GROUP, ATTN_BLOCK, 2 * ATTN_BLOCK))
        onehots.append(onehot)
    h, saved = x, []
    for l in range(DEPTH):
        W = dict(Wfull[l])
        W["w_in"] = _split_w_in(W["w_in"])
        W["w_ffn_in_p"] = _ffn_perm(W["w_ffn_in"])
        Wfull[l] = W
        h, sv = _layer_fwd(h, small[l], W, biases, cs, l)
        saved.append(sv)
    dh, g_final, loss = _loss_head(h, final_norm_w, tgt, "loss_head")
    grads = [None] * DEPTH
    dbias_tot = [None] * N_DIL
    for l in reversed(range(DEPTH)):
        dh, grads[l], dbias = _layer_bwd(dh, saved[l], small[l], Wfull[l], biases, cs, head_ones, l)
        for g in range(N_DIL):
            dbias_tot[g] = dbias[g] if dbias_tot[g] is None else dbias_tot[g] + dbias[g]
    d_rel = jnp.concatenate(
        [_bias_scatter(dbias_tot[g].reshape(HEADS_PER_GROUP, -1), onehots[g], f"bias_scatter_g{g}").T for g in range(N_DIL)], axis=1)
    return loss, dh, grads, d_rel, g_final


def _unshard(nm, g):
    _, rows, cols = g.shape
    if nm in COL_SHARDED:
        return g.transpose(1, 0, 2).reshape(rows, N_DEV * cols)
    return g.reshape(N_DEV * rows, cols)


def _shard(nm, w):
    rows, cols = w.shape
    if nm in COL_SHARDED:
        return w.reshape(rows, N_DEV, cols // N_DEV).transpose(1, 0, 2)
    return w.reshape(N_DEV, rows // N_DEV, cols)


SMALL_LAYER = (("norm1_w", 1024), ("conv_w", 12288), ("conv_b", 3072), ("dt_bias", 32), ("a_log", 32), ("d_skip", 32),
               ("ssm_norm_w", 2048), ("norm2_w", 1024))
SMALL_GLOBAL = (("rel_bias", 768), ("final_norm_w", 1024), ("loss", 1))


def _pad128(v):
    n = v.shape[0]
    return jnp.pad(v, (0, -n % LANES))


def _pack_small(per_layer, glob):
    parts = [_pad128(per_layer[l][nm].reshape(-1)) for l in range(DEPTH) for nm, _ in SMALL_LAYER]
    parts += [_pad128(glob[nm].reshape(-1)) for nm, _ in SMALL_GLOBAL]
    flat = jnp.concatenate(parts)
    flat = jnp.pad(flat, (0, -flat.shape[0] % (8 * LANES)))
    return flat.reshape(-1, LANES)


def _unpack_small(packed):
    flat = packed.reshape(-1)
    per_layer, glob, off = [dict() for _ in range(DEPTH)], {}, 0
    for l in range(DEPTH):
        for nm, n in SMALL_LAYER:
            per_layer[l][nm] = flat[off:off + n]
            off += n + (-n % LANES)
    for nm, n in SMALL_GLOBAL:
        glob[nm] = flat[off:off + n]
        off += n + (-n % LANES)
    return per_layer, glob


def kernel(x, norm1_w, w_in, conv_w, conv_b, dt_bias, a_log, d_skip, ssm_norm_w, w_attn_branch, w_ssm_branch, w_out, norm2_w, w_ffn_in, w_ffn_out, rel_bias, final_norm_w, loss_target, m_norm1_w, m_w_in, m_conv_w, m_conv_b, m_dt_bias, m_a_log, m_d_skip, m_ssm_norm_w, m_w_attn_branch, m_w_ssm_branch, m_w_out, m_norm2_w, m_w_ffn_in, m_w_ffn_out, m_rel_bias, m_final_norm_w, v_norm1_w, v_w_in, v_conv_w, v_conv_b, v_dt_bias, v_a_log, v_d_skip, v_ssm_norm_w, v_w_attn_branch, v_w_ssm_branch, v_w_out, v_norm2_w, v_w_ffn_in, v_w_ffn_out, v_rel_bias, v_final_norm_w):
    big = dict(w_in=w_in, w_attn_branch=w_attn_branch, w_ssm_branch=w_ssm_branch, w_out=w_out, w_ffn_in=w_ffn_in, w_ffn_out=w_ffn_out)
    big_m = dict(w_in=m_w_in, w_attn_branch=m_w_attn_branch, w_ssm_branch=m_w_ssm_branch, w_out=m_w_out, w_ffn_in=m_w_ffn_in, w_ffn_out=m_w_ffn_out)
    big_v = dict(w_in=v_w_in, w_attn_branch=v_w_attn_branch, w_ssm_branch=v_w_ssm_branch, w_out=v_w_out, w_ffn_in=v_w_ffn_in, w_ffn_out=v_w_ffn_out)
    sm = dict(norm1_w=norm1_w, conv_w=conv_w, conv_b=conv_b, dt_bias=dt_bias, a_log=a_log, d_skip=d_skip, ssm_norm_w=ssm_norm_w, norm2_w=norm2_w)
    sm_m = dict(norm1_w=m_norm1_w, conv_w=m_conv_w, conv_b=m_conv_b, dt_bias=m_dt_bias, a_log=m_a_log, d_skip=m_d_skip, ssm_norm_w=m_ssm_norm_w, norm2_w=m_norm2_w)
    sm_v = dict(norm1_w=v_norm1_w, conv_w=v_conv_w, conv_b=v_conv_b, dt_bias=v_dt_bias, a_log=v_a_log, d_skip=v_d_skip, ssm_norm_w=v_ssm_norm_w, norm2_w=v_norm2_w)
    me = 4 * lax.axis_index("x") + 2 * lax.axis_index("y") + lax.axis_index("c")

    gathered = _all_gather([big[nm][l].astype(BF16) for l in range(DEPTH) for nm in BIG], "all_gather_weights")
    Wfull = [{nm: _unshard(nm, gathered[l * len(BIG) + i]) for i, nm in enumerate(BIG)} for l in range(DEPTH)]

    conv_full = []
    for l in range(DEPTH):
        z = jnp.zeros((N_DEV, CONV_WIDTH, XBC_WIDTH // N_DEV), F32)
        conv_full.append(lax.dynamic_update_index_in_dim(z, conv_w[l], me, axis=0))
    cw = jnp.stack(conv_full).reshape(-1, LANES)
    cw = _all_reduce_small(cw, "gather_conv_w").reshape(DEPTH, N_DEV, CONV_WIDTH, XBC_WIDTH // N_DEV)
    cw = cw.transpose(0, 2, 1, 3).reshape(DEPTH, CONV_WIDTH, XBC_WIDTH)

    small = [{nm: (cw[l] if nm == "conv_w" else a[l]) for nm, a in sm.items()} for l in range(DEPTH)]
    loss, dx, grads, d_rel, g_final = _step_local(x[0], loss_target[0], small, Wfull, rel_bias, final_norm_w)

    mine = _reduce_scatter([_shard(nm, grads[l][nm]) for l in range(DEPTH) for nm in BIG], "reduce_scatter_grads")
    g_big = {nm: jnp.stack([mine[l * len(BIG) + i] for l in range(DEPTH)]) for i, nm in enumerate(BIG)}

    per_layer = [{nm: grads[l][nm] for nm, _ in SMALL_LAYER} for l in range(DEPTH)]
    packet = _pack_small(per_layer, dict(rel_bias=d_rel, final_norm_w=g_final, loss=loss[0, :1]))
    per_layer, glob = _unpack_small(_all_reduce_small(packet, "all_reduce_small"))
    g_small = {nm: jnp.stack([per_layer[l][nm] for l in range(DEPTH)]) for nm, _ in SMALL_LAYER}
    cwg = g_small["conv_w"].reshape(DEPTH, CONV_WIDTH, N_DEV, XBC_WIDTH // N_DEV)
    g_small["conv_w"] = lax.dynamic_index_in_dim(cwg, me, axis=2, keepdims=False)
    for nm in sm:
        g_small[nm] = g_small[nm].reshape(sm[nm].shape)
    g_rel = glob["rel_bias"].reshape(rel_bias.shape)
    g_fin = glob["final_norm_w"]
    loss_out = glob["loss"][0]

    def adam(w, g, m, v, name):
        shp = w.shape
        two = lambda a: a.reshape(-1, shp[-1]) if a.ndim > 1 else a.reshape(1, -1)
        d, nm_, nv = _adamw(two(w), two(g), two(m), two(v), name)
        return d.reshape(shp), nm_.reshape(shp), nv.reshape(shp)

    order = ["norm1_w", "w_in", "conv_w", "conv_b", "dt_bias", "a_log", "d_skip", "ssm_norm_w", "w_attn_branch", "w_ssm_branch",
             "w_out", "norm2_w", "w_ffn_in", "w_ffn_out", "rel_bias", "final_norm_w"]
    allw = {**big, **sm, "rel_bias": rel_bias, "final_norm_w": final_norm_w}
    allm = {**big_m, **sm_m, "rel_bias": m_rel_bias, "final_norm_w": m_final_norm_w}
    allv = {**big_v, **sm_v, "rel_bias": v_rel_bias, "final_norm_w": v_final_norm_w}
    allg = {**g_big, **g_small, "rel_bias": g_rel, "final_norm_w": g_fin}
    deltas, new_m, new_v = [], [], []
    for nm in order:
        d, a, b = adam(allw[nm], allg[nm], allm[nm], allv[nm], "adamw_" + nm)
        deltas.append(d)
        new_m.append(a)
        new_v.append(b)
    return (loss_out, dx[None], *[allg[nm] for nm in order], *deltas, *new_m, *new_v)
```

```python
import functools
import math

import numpy as np
import jax
import jax.numpy as jnp
from jax import lax
from jax.experimental import pallas as pl
from jax.experimental.pallas import tpu as pltpu

F32, BF16 = jnp.float32, jnp.bfloat16
S = jax.ShapeDtypeStruct
MESH = pl.DeviceIdType.MESH

D_MODEL = 1024
DEPTH = 2
HEAD_DIM = 64
DILATED_GROUPS = ((128, 1), (512, 4), (2048, 16))
N_DIL = 3
HEADS_PER_GROUP = 8
ATTN_WIDTH = 1536
ATTN_OUT_WIDTH = 512
ATTN_BLOCK = 128
N_REL_BUCKETS = 32
REL_MAX_DISTANCE = 2048
D_INNER = 2048
N_SSM_HEADS = 32
N_SSM_GROUPS = 4
D_STATE = 128
CONV_WIDTH = 4
SSD_CHUNK = 128
XBC_WIDTH = 3072
D_FF = 2816
EPS = 1e-6
ADAM_LR, ADAM_B1, ADAM_B2, ADAM_EPS, ADAM_WD, ADAM_STEP = 0.001, 0.9, 0.999, 1e-08, 0.01, 10

N_DEV = 8
LANES = 128
VMEM_LIMIT = 56 * 1024 * 1024
ROW_TILES_BYTES = 36 * 1024 * 1024
NEG = -1e30
BIG = ("w_in", "w_attn_branch", "w_ssm_branch", "w_out", "w_ffn_in", "w_ffn_out")
COL_SHARDED = ("w_in", "w_attn_branch", "w_ffn_in")

NT = (((1,), (1,)), ((), ()))
TN = (((0,), (0,)), ((), ()))


def _cparams(sem=None):
    return pltpu.CompilerParams(dimension_semantics=sem, vmem_limit_bytes=VMEM_LIMIT)


def _pick(n, target, mult=LANES):
    best = None
    for t in range(mult, min(n, target) + 1, mult):
        if n % t == 0:
            best = t
    return best or n


def _silu(x):
    return x * jax.nn.sigmoid(x)


def _dsilu(x):
    s = jax.nn.sigmoid(x)
    return s * (1.0 + x * (1.0 - s))


def _split2(x):
    hi = x.astype(BF16)
    lo = (x - hi.astype(F32)).astype(BF16)
    return hi, lo


def _split3(x):
    x1 = x.astype(BF16)
    r1 = x - x1.astype(F32)
    x2 = r1.astype(BF16)
    x3 = (r1 - x2.astype(F32)).astype(BF16)
    return x1, x2, x3


def _dotx_r(x, m, parts=3):
    xs = _split3(x) if parts == 3 else _split2(x)
    out = jnp.dot(xs[0], m, preferred_element_type=F32)
    for xi in xs[1:]:
        out = out + jnp.dot(xi, m, preferred_element_type=F32)
    return out


def _dotx_l(m, x, parts=3):
    xs = _split3(x) if parts == 3 else _split2(x)
    out = jnp.dot(m, xs[0], preferred_element_type=F32)
    for xi in xs[1:]:
        out = out + jnp.dot(m, xi, preferred_element_type=F32)
    return out


def _mm(a, b, *, ta=False, tb=False, out_dtype=F32, acc=None, name, tm=1536, tn=1536, tk=1536, extras=(), epilogue=None, outs=None):
    M, K = (a.shape[1], a.shape[0]) if ta else a.shape
    N = b.shape[0] if tb else b.shape[1]
    tm, tn, tk = _pick(M, tm), _pick(N, tn), _pick(K, tk)
    nk = K // tk
    dims = (((0 if ta else 1,), (1 if tb else 0,)), ((), ()))
    has_acc = acc is not None
    outs = [(N, out_dtype)] if outs is None else outs
    ne, no = len(extras), len(outs)

    def body(*refs):
        a_ref, b_ref = refs[:2]
        c_ref = refs[2] if has_acc else None
        e_refs = refs[2 + has_acc:2 + has_acc + ne]
        o_refs = refs[2 + has_acc + ne:2 + has_acc + ne + no]
        acc_ref = refs[-1]
        k = pl.program_id(2)
        part = lax.dot_general(a_ref[...].astype(BF16), b_ref[...].astype(BF16), dims, preferred_element_type=F32)

        @pl.when(k == 0)
        def _():
            acc_ref[...] = part + c_ref[...].astype(F32) if has_acc else part

        @pl.when(k > 0)
        def _():
            acc_ref[...] += part

        @pl.when(k == nk - 1)
        def _():
            res = acc_ref[...]
            tiles = [res] if epilogue is None else epilogue(res, *[e[...] for e in e_refs])
            for o_ref, t in zip(o_refs, tiles):
                o_ref[...] = t.astype(o_ref.dtype)

    def cspec(cols):
        return pl.BlockSpec((tm, cols * tn // N), lambda i, j, k: (i, j))

    a_spec = pl.BlockSpec((tk, tm), lambda i, j, k: (k, i)) if ta else pl.BlockSpec((tm, tk), lambda i, j, k: (i, k))
    b_spec = pl.BlockSpec((tn, tk), lambda i, j, k: (j, k)) if tb else pl.BlockSpec((tk, tn), lambda i, j, k: (k, j))
    in_specs, args = [a_spec, b_spec], [a, b]
    if has_acc:
        in_specs.append(cspec(N))
        args.append(acc)
    in_specs += [cspec(e.shape[1]) for e in extras]
    args += list(extras)
    res = pl.pallas_call(
        body, grid=(M // tm, N // tn, nk), in_specs=in_specs, out_specs=[cspec(c) for c, _ in outs],
        out_shape=[S((M, c), dt) for c, dt in outs], scratch_shapes=[pltpu.VMEM((tm, tn), F32)],
        compiler_params=_cparams(("parallel", "parallel", "arbitrary")), name=name)(*args)
    return res[0] if len(outs) == 1 else res


def _mm_dil(a_list, b_list, d, acc, name, tm=1024):
    T, K = a_list[0].shape
    N = b_list[0].shape[0]
    tm, tn = min(tm, T), _pick(N, 1024)
    na = len(a_list)
    has_acc = acc is not None

    def body(*refs):
        a_refs, b_refs, rest = refs[:na], refs[na:2 * na], refs[2 * na:]
        c_ref = rest[0] if has_acc else None
        o_ref, scr = rest[-2], rest[-1]
        out = c_ref[...] if has_acc else None
        for a_ref, b_ref in zip(a_refs, b_refs):
            a_tok = _dil_to_tok(scr, a_ref, d).astype(BF16) if d > 1 else a_ref[...]
            part = lax.dot_general(a_tok, b_ref[...], NT, preferred_element_type=F32)
            out = part if out is None else out + part
        o_ref[...] = out

    if d > 1:
        a_spec = pl.BlockSpec((d, tm // d, K), lambda i, j: (0, i, 0))
        a_args = [a.reshape(d, T // d, K) for a in a_list]
    else:
        a_spec = pl.BlockSpec((tm, K), lambda i, j: (i, 0))
        a_args = list(a_list)
    o_spec = pl.BlockSpec((tm, tn), lambda i, j: (i, j))
    in_specs = [a_spec] * na + [pl.BlockSpec((tn, K), lambda i, j: (j, 0))] * na + ([o_spec] if has_acc else [])
    return pl.pallas_call(
        body, grid=(T // tm, N // tn), in_specs=in_specs, out_specs=o_spec, out_shape=S((T, N), F32),
        scratch_shapes=[pltpu.VMEM((K // LANES, tm, LANES), F32)],
        compiler_params=_cparams(("parallel", "parallel")), name=name)(*a_args, *b_list, *([acc] if has_acc else []))


class _Comm:
    def __init__(self, ins, outs, sems, start, wait, alias=None):
        self.ins, self.outs, self.sems, self.start, self.wait, self.alias = list(ins), list(outs), list(sems), start, wait, alias or {}


def _carrier_call(body, n, in_specs, out_specs, out_shape, scratch_shapes, args, comm, name):
    ni, no, ns = len(in_specs), len(out_specs), len(scratch_shapes)
    if comm is None:
        res = pl.pallas_call(body, grid=(n,), in_specs=in_specs, out_specs=out_specs, out_shape=out_shape,
                             scratch_shapes=scratch_shapes, compiler_params=_cparams(("arbitrary",)), name=name)(*args)
        return list(res), []
    ci, co = len(comm.ins), len(comm.outs)

    def wrapped(*refs):
        ins, cins = refs[:ni], refs[ni:ni + ci]
        outs, couts = refs[ni + ci:ni + ci + no], refs[ni + ci + no:ni + ci + no + co]
        scr, csems = refs[ni + ci + no + co:ni + ci + no + co + ns], refs[ni + ci + no + co + ns:]

        @pl.when(pl.program_id(0) == 0)
        def _():
            comm.start(cins, couts, csems)

        body(*ins, *outs, *scr)

        @pl.when(pl.program_id(0) == n - 1)
        def _():
            comm.wait(cins, couts, csems)

    anys = pl.BlockSpec(memory_space=pl.ANY)
    res = pl.pallas_call(
        wrapped, grid=(n,), in_specs=list(in_specs) + [anys] * ci, out_specs=list(out_specs) + [anys] * co,
        out_shape=list(out_shape) + comm.outs, scratch_shapes=list(scratch_shapes) + comm.sems,
        input_output_aliases={ni + a: no + b for a, b in comm.alias.items()},
        compiler_params=_cparams(("arbitrary",)), name=name)(*args, *comm.ins)
    return list(res[:no]), list(res[no:])


def _dev_index(dev):
    return 4 * dev[0] + 2 * dev[1] + dev[2]


def _comm_gather_spread(shards):
    npc = len(shards)

    def copies(x_refs, o_refs, sems):
        x, y, c, chips = _place()
        me = (x, y, c)
        peers = [(x, y, 1 - c)] + [(*chip, c) for chip in chips]
        return [[pltpu.make_async_remote_copy(src_ref=x_refs[i], dst_ref=o_refs[i].at[_dev_index(me)], send_sem=sems[0].at[k, i],
                                              recv_sem=sems[1].at[k, i], device_id=peer, device_id_type=MESH)
                 for k, peer in enumerate(peers)] for i in range(npc)], peers, me

    def local(x_refs, o_refs, sems, me):
        return [pltpu.make_async_copy(x_refs[i], o_refs[i].at[_dev_index(me)], sems[2].at[i]) for i in range(npc)]

    def start(x_refs, o_refs, sems):
        cps, _, me = copies(x_refs, o_refs, sems)
        for cp in local(x_refs, o_refs, sems, me):
            cp.start()
        for row in cps:
            for cp in row:
                cp.start()

    def wait(x_refs, o_refs, sems):
        cps, peers, me = copies(x_refs, o_refs, sems)
        for i in range(npc):
            for k, peer in enumerate(peers):
                pltpu.make_async_remote_copy(src_ref=x_refs[i], dst_ref=o_refs[i].at[_dev_index(peer)], send_sem=sems[0].at[k, i],
                                             recv_sem=sems[1].at[k, i], device_id=peer, device_id_type=MESH).wait_recv()
        for row in cps:
            for cp in row:
                cp.wait_send()
        for cp in local(x_refs, o_refs, sems, me):
            cp.wait()

    return _Comm(shards, [S((N_DEV,) + s.shape, s.dtype) for s in shards],
                 [pltpu.SemaphoreType.DMA((4, npc)), pltpu.SemaphoreType.DMA((4, npc)), pltpu.SemaphoreType.DMA((npc,))], start, wait)


def _comm_gather_pass(gathered):
    npc = len(gathered)

    def copies(o_refs, sems, sent):
        x, y, c, chips = _place()
        return [pltpu.make_async_remote_copy(
            src_ref=o_refs[i].at[_dev_index((*chip, c))], dst_ref=o_refs[i].at[_dev_index((*chip, c if sent else 1 - c))],
            send_sem=sems[0].at[j, i], recv_sem=sems[1].at[j, i], device_id=(x, y, 1 - c), device_id_type=MESH)
            for i in range(npc) for j, chip in enumerate(chips)]

    def start(g_refs, o_refs, sems):
        for cp in copies(o_refs, sems, True):
            cp.start()

    def wait(g_refs, o_refs, sems):
        for cp in copies(o_refs, sems, False):
            cp.wait_recv()
        for cp in copies(o_refs, sems, True):
            cp.wait_send()

    return _Comm(gathered, [S(g.shape, g.dtype) for g in gathered],
                 [pltpu.SemaphoreType.DMA((3, npc)), pltpu.SemaphoreType.DMA((3, npc))], start, wait,
                 alias={i: i for i in range(npc)})


def _comm_to_chips(parts):
    npc = len(parts)

    def copies(p_refs, o_refs, sems):
        x, y, c, chips = _place()
        return [pltpu.make_async_remote_copy(
            src_ref=p_refs[i].at[2 * chip[0] + chip[1]], dst_ref=o_refs[i].at[j], send_sem=sems[0].at[j, i],
            recv_sem=sems[1].at[j, i], device_id=(*chip, c), device_id_type=MESH)
            for i in range(npc) for j, chip in enumerate(chips)]

    def start(p_refs, o_refs, sems):
        for cp in copies(p_refs, o_refs, sems):
            cp.start()

    def wait(p_refs, o_refs, sems):
        for cp in copies(p_refs, o_refs, sems):
            cp.wait()

    return _Comm(parts, [S((3,) + p.shape[1:], p.dtype) for p in parts],
                 [pltpu.SemaphoreType.DMA((3, npc)), pltpu.SemaphoreType.DMA((3, npc))], start, wait)


def _dil_to_tok(scr, ref, d):
    n, C = ref.shape[1], ref.shape[2]
    for r in range(d):
        v = ref[r].astype(F32)
        for cb in range(C // LANES):
            scr.at[cb][pl.ds(r, n, stride=d), :] = v[:, cb * LANES:(cb + 1) * LANES]
    return jnp.concatenate([scr[cb] for cb in range(C // LANES)], axis=1)


def _tok_to_dil(scr, val, ref, d):
    n, C = ref.shape[1], ref.shape[2]
    for cb in range(C // LANES):
        scr[cb] = val[:, cb * LANES:(cb + 1) * LANES].astype(F32)
    for r in range(d):
        ref[r] = jnp.concatenate([scr.at[cb][pl.ds(r, n, stride=d), :] for cb in range(C // LANES)], axis=1).astype(ref.dtype)


def _rowwise(fn, rows, fulls, outs, accs=(), *, tm, name, cap=True, comm=None):
    rows = [r if isinstance(r, tuple) else (r, r.shape[1], 0) for r in rows]
    first = rows[0]
    T = (first[1] if isinstance(first[0], str) else first[0]).shape[0]
    widest = max([r[1].shape[1] if isinstance(r[0], str) else r[1] for r in rows] + [o[0] for o in outs])
    if cap:
        tm = min(tm, max(8, ROW_TILES_BYTES // (2 * (len(rows) + len(outs))) // (4 * widest) // 8 * 8))
    tm = T if T <= tm else _pick(T, tm, 8)
    nr, nf, no, na = len(rows), len(fulls), len(outs), len(accs)
    dil_in = [i for i, r in enumerate(rows) if isinstance(r[0], str) and r[2] > 1]
    dil_out = [i for i, o in enumerate(outs) if len(o) == 3 and o[2] > 1]
    scr_cols = [rows[i][1].shape[1] for i in dil_in] + [outs[i][0] for i in dil_out]

    def body(*refs):
        r, f = refs[:nr], refs[nr:nr + nf]
        o, a = refs[nr + nf:nr + nf + no], refs[nr + nf + no:nr + nf + no + na]
        scr = refs[nr + nf + no + na:]
        tiles = []
        for i, x in enumerate(r):
            if i in dil_in:
                tiles.append(_dil_to_tok(scr[dil_in.index(i)], x, rows[i][2]))
            else:
                tiles.append(x[...].astype(F32))
        ro, ra = fn(*tiles, *[x[...] for x in f])
        for i, (ref, val) in enumerate(zip(o, ro)):
            if i in dil_out:
                _tok_to_dil(scr[len(dil_in) + dil_out.index(i)], val, ref, outs[i][2])
            else:
                ref[...] = val.astype(ref.dtype)
        if na:
            @pl.when(pl.program_id(0) == 0)
            def _():
                for ref in a:
                    ref[...] = jnp.zeros_like(ref)
            for ref, val in zip(a, ra):
                ref[...] += val

    in_specs, args = [], []
    for i, rr in enumerate(rows):
        if isinstance(rr[0], str):
            arr, d = rr[1], rr[2]
            if d > 1:
                in_specs.append(pl.BlockSpec((d, tm // d, arr.shape[1]), lambda i: (0, i, 0)))
                args.append(arr.reshape(d, T // d, arr.shape[1]))
            else:
                in_specs.append(pl.BlockSpec((tm, arr.shape[1]), lambda i: (i, 0)))
                args.append(arr)
        else:
            in_specs.append(pl.BlockSpec((tm, rr[1]), functools.partial(lambda i, cb: (i, cb), cb=rr[2])))
            args.append(rr[0])
    in_specs += [pl.BlockSpec(f.shape, lambda i: (0, 0)) for f in fulls]
    out_specs, out_shape = [], []
    for i, oo in enumerate(outs):
        if i in dil_out:
            d = oo[2]
            out_specs.append(pl.BlockSpec((d, tm // d, oo[0]), lambda i: (0, i, 0)))
            out_shape.append(S((d, T // d, oo[0]), oo[1]))
        else:
            out_specs.append(pl.BlockSpec((tm, oo[0]), lambda i: (i, 0)))
            out_shape.append(S((T, oo[0]), oo[1]))
    out_specs += [pl.BlockSpec(sh, lambda i: (0, 0)) for sh in accs]
    out_shape += [S(sh, F32) for sh in accs]
    res, comm_res = _carrier_call(
        body, T // tm, in_specs, out_specs, out_shape, [pltpu.VMEM((c // LANES, tm, LANES), F32) for c in scr_cols],
        list(args) + list(fulls), comm, name)
    res = [x.reshape(T, x.shape[2]) if i in dil_out else x for i, x in enumerate(res)]
    return res if comm is None else (res, comm_res)


def _rmsnorm_fwd(h, w, name, dils=()):
    D = h.shape[1]

    def fn(h, w):
        r = lax.rsqrt(jnp.mean(h * h, axis=-1, keepdims=True) + EPS)
        xn = h * r * w
        return [xn] * (1 + len(dils)), []
    return _rowwise(fn, [h], [w.reshape(1, -1)], [(D, BF16)] + [(D, BF16, d) for d in dils], tm=512, name=name)


def _rmsnorm_bwd(dxn, h, w, dres, name):
    def fn(dxn, h, dres, w):
        r = lax.rsqrt(jnp.mean(h * h, axis=-1, keepdims=True) + EPS)
        n = h * r
        dn = dxn * w
        dh = r * (dn - n * jnp.mean(dn * n, axis=-1, keepdims=True)) + dres
        return [dh], [jnp.sum(dxn * n, axis=0, keepdims=True)]
    D = h.shape[1]
    return _rowwise(fn, [dxn, h, dres], [w.reshape(1, -1)], [(D, F32)], [(1, D)], tm=256, name=name)


def _loss_head(h, w, tgt, name):
    D = h.shape[1]

    def fn(h, tgt, w):
        r = lax.rsqrt(jnp.mean(h * h, axis=-1, keepdims=True) + EPS)
        n = h * r
        e = n * w - tgt
        row_loss = 0.5 * jnp.mean(e * e, axis=-1, keepdims=True)
        dy = e * (1.0 / D)
        dn = dy * w
        dh = r * (dn - n * jnp.mean(dn * n, axis=-1, keepdims=True))
        return [dh], [jnp.sum(dy * n, axis=0, keepdims=True), jnp.broadcast_to(jnp.sum(row_loss, axis=0, keepdims=True), (1, LANES))]
    return _rowwise(fn, [h, tgt], [w.reshape(1, -1)], [(D, F32)], [(1, D), (1, LANES)], tm=256, name=name)


def _combine_fwd(os_, ls, name):
    def fn(o0, o1, o2, l0, l1, l2):
        m = jnp.maximum(jnp.maximum(l0, l1), l2)
        e0, e1, e2 = jnp.exp(l0 - m), jnp.exp(l1 - m), jnp.exp(l2 - m)
        attn = (e0 * o0 + e1 * o1 + e2 * o2) / (e0 + e1 + e2)
        return [attn, attn], []
    dil = [("dil", t, d) for t, (_, d) in zip(list(os_) + list(ls), DILATED_GROUPS * 2)]
    return _rowwise(fn, dil, [], [(ATTN_OUT_WIDTH, BF16), (ATTN_OUT_WIDTH, F32)], tm=512, name=name)


def _combine_bwd(dattn, attn, ls, head_ones, name):
    def fn(dattn, attn, l0, l1, l2, ones):
        m = jnp.maximum(jnp.maximum(l0, l1), l2)
        e0, e1, e2 = jnp.exp(l0 - m), jnp.exp(l1 - m), jnp.exp(l2 - m)
        inv = 1.0 / (e0 + e1 + e2)
        t = _dotx_r(dattn * attn, ones, parts=2)
        outs = []
        for e in (e0, e1, e2):
            al = e * inv
            outs += [al * dattn, al * t]
        return outs, []
    W = ATTN_OUT_WIDTH
    dil = [("dil", t, d) for t, (_, d) in zip(ls, DILATED_GROUPS)]
    outs = [(W, F32, d) for _, d in DILATED_GROUPS for _ in range(2)]
    return _rowwise(fn, [dattn, attn] + dil, [head_ones], outs, tm=512, name=name)


def _dt_fwd(dt_raw, dt_bias, name):
    def fn(raw, b):
        z = raw[:, :N_SSM_HEADS] + b
        return [jnp.maximum(z, 0.0) + jnp.log(1.0 + jnp.exp(-jnp.abs(z)))], []
    return _rowwise(fn, [dt_raw], [dt_bias.reshape(1, -1)], [(N_SSM_HEADS, F32)], tm=1024, name=name)[0]


def _dt_bwd(ddt_a, ddt_b, dt_raw, dt_bias, name):
    def fn(da, db, raw, b):
        g = (da + db) * jax.nn.sigmoid(raw[:, :N_SSM_HEADS] + b)
        pad = jnp.zeros((g.shape[0], LANES - N_SSM_HEADS), F32)
        return [jnp.concatenate([g, pad], axis=1)], [jnp.sum(g, axis=0, keepdims=True)]
    return _rowwise(fn, [ddt_a, ddt_b, dt_raw], [dt_bias.reshape(1, -1)], [(LANES, BF16)], [(1, N_SSM_HEADS)], tm=1024, name=name)


def _ssm_norm_fwd(y, z, w, name, comm=None):
    G = D_INNER // N_SSM_GROUPS

    def fn(y, z, w):
        yg = y * _silu(z)
        outs = []
        for g in range(N_SSM_GROUPS):
            t = yg[:, g * G:(g + 1) * G]
            outs.append(t * lax.rsqrt(jnp.mean(t * t, axis=-1, keepdims=True) + EPS))
        return [jnp.concatenate(outs, axis=1) * w], []
    res = _rowwise(fn, [y, z], [w.reshape(1, -1)], [(D_INNER, BF16)], tm=256, name=name, comm=comm)
    return res[0] if comm is None else (res[0][0], res[1])


def _ssm_norm_bwd(dssm, y, z, w, name):
    G = D_INNER // N_SSM_GROUPS

    def fn(dssm, y, z, w):
        sz = _silu(z)
        yg = y * sz
        dn = dssm * w
        ns, dygs = [], []
        for g in range(N_SSM_GROUPS):
            t = yg[:, g * G:(g + 1) * G]
            r = lax.rsqrt(jnp.mean(t * t, axis=-1, keepdims=True) + EPS)
            n = t * r
            d = dn[:, g * G:(g + 1) * G]
            dygs.append(r * (d - n * jnp.mean(d * n, axis=-1, keepdims=True)))
            ns.append(n)
        n, dyg = jnp.concatenate(ns, axis=1), jnp.concatenate(dygs, axis=1)
        return [dyg * sz, dyg * y * _dsilu(z)], [jnp.sum(dssm * n, axis=0, keepdims=True)]
    return _rowwise(fn, [dssm, y, z], [w.reshape(1, -1)], [(D_INNER, F32), (D_INNER, BF16)], [(1, D_INNER)], tm=256, name=name)


def _gate_out_proj(a, sb, gl, h, w_out, name):
    def fn(a, sb, gl, h, w):
        g = jax.nn.sigmoid(gl)
        merged = (g[:, :D_MODEL] * a + g[:, D_MODEL:] * sb).astype(BF16)
        return [h + jnp.dot(merged, w, preferred_element_type=F32), merged], []
    return _rowwise(fn, [a, sb, gl, h], [w_out], [(D_MODEL, F32), (D_MODEL, BF16)], tm=512, cap=False, name=name)


def _d_out_proj_gate(dh, a, sb, gl, w_out, name):
    def fn(dh, a, sb, gl, w):
        dm = lax.dot_general(dh.astype(BF16), w, NT, preferred_element_type=F32)
        g = jax.nn.sigmoid(gl)
        g0, g1 = g[:, :D_MODEL], g[:, D_MODEL:]
        dgl = jnp.concatenate([dm * a * g0 * (1.0 - g0), dm * sb * g1 * (1.0 - g1)], axis=1)
        return [g0 * dm, g1 * dm, dgl], []
    return _rowwise(fn, [dh, a, sb, gl], [w_out], [(D_MODEL, BF16), (D_MODEL, BF16), (2 * D_MODEL, BF16)], tm=512, cap=False, name=name)


FFN_HALF = D_FF // 2


def _ffn_perm(w):
    h = FFN_HALF
    return jnp.concatenate([w[:, 0:h], w[:, D_FF:D_FF + h], w[:, h:D_FF], w[:, D_FF + h:]], axis=1)


def _ffn_unperm(w):
    h = FFN_HALF
    return jnp.concatenate([w[:, 0:h], w[:, 2 * h:3 * h], w[:, h:2 * h], w[:, 3 * h:]], axis=1)


def _swiglu_epilogue(res):
    return [res, _silu(res[:, :FFN_HALF]) * res[:, FFN_HALF:]]


def _dswiglu_epilogue(dact, u):
    u = u.astype(F32)
    gate, up = u[:, :FFN_HALF], u[:, FFN_HALF:]
    return [jnp.concatenate([dact * up * _dsilu(gate), dact * _silu(gate)], axis=1)]


def _adamw(w, g, m, v, name):
    c1 = 1.0 - ADAM_B1 ** ADAM_STEP
    c2 = 1.0 - ADAM_B2 ** ADAM_STEP

    def fn(w, g, m, v):
        m = ADAM_B1 * m + (1.0 - ADAM_B1) * g
        v = ADAM_B2 * v + (1.0 - ADAM_B2) * (g * g)
        delta = -ADAM_LR * ((m / c1) / (jnp.sqrt(v / c2) + ADAM_EPS) + ADAM_WD * w)
        return [delta, m, v], []
    C = w.shape[1]
    return _rowwise(fn, [w, g, m, v], [], [(C, F32)] * 3, tm=256, name=name)


def _bias_consts(dilation, n_steps):
    qi = np.arange(ATTN_BLOCK)[:, None]
    kj = np.arange(2 * ATTN_BLOCK)[None, :]
    steps = qi + ATTN_BLOCK - kj
    valid = (steps >= 0) & (steps <= n_steps)
    dist = jnp.asarray(np.clip(steps, 0, n_steps) * dilation, jnp.int32)
    max_exact = N_REL_BUCKETS // 2
    d_f = jnp.maximum(dist, 1).astype(F32)
    large = max_exact + (jnp.log(d_f / max_exact) / math.log(REL_MAX_DISTANCE / max_exact)
                         * (N_REL_BUCKETS - max_exact)).astype(jnp.int32)
    large = jnp.minimum(large, N_REL_BUCKETS - 1)
    bucket = jnp.where(dist < max_exact, dist, large).reshape(-1)
    onehot = (bucket[None, :] == jnp.arange(N_REL_BUCKETS)[:, None]).astype(F32)
    return onehot, jnp.asarray(valid.reshape(1, -1), F32)


def _bias_gather(rel_g_t, onehot, valid, name):
    def body(r_ref, oh_ref, v_ref, o_ref):
        b = jnp.dot(r_ref[...], oh_ref[...], preferred_element_type=F32, precision=lax.Precision.HIGHEST)
        o_ref[...] = jnp.where(v_ref[...] > 0.5, b, NEG)
    return pl.pallas_call(body, out_shape=S((HEADS_PER_GROUP, onehot.shape[1]), F32), compiler_params=_cparams(), name=name)(rel_g_t, onehot, valid)


def _bias_scatter(dbias, onehot, name):
    def body(d_ref, oh_ref, o_ref):
        o_ref[...] = lax.dot_general(d_ref[...], oh_ref[...], NT, preferred_element_type=F32, precision=lax.Precision.HIGHEST)
    return pl.pallas_call(body, out_shape=S((HEADS_PER_GROUP, N_REL_BUCKETS), F32), compiler_params=_cparams(), name=name)(dbias, onehot)


ATTN_QB_FWD, ATTN_QB_BWD = 4, 4


def _attn_tiles(T, d, qb):
    seg = T // d
    nqb = min(qb, seg // ATTN_BLOCK)
    tq = nqb * ATTN_BLOCK
    return seg, nqb, tq, seg // tq


def _attn_fwd(qkv, bias, d, name):
    T = qkv.shape[0]
    seg, nqb, tq, ns = _attn_tiles(T, d, ATTN_QB_FWD)
    W = ATTN_OUT_WIDTH
    scale = HEAD_DIM ** -0.5

    def body(q_ref, kh_ref, kc_ref, vh_ref, vc_ref, b_ref, o_ref, l_ref, s_scr, p_scr):
        n = pl.program_id(1)
        qv = q_ref[...]
        kk = jnp.concatenate([kh_ref[...], kc_ref[...]], axis=0)
        vv = jnp.concatenate([vh_ref[...], vc_ref[...]], axis=0)
        col = lax.broadcasted_iota(jnp.int32, (ATTN_BLOCK, 2 * ATTN_BLOCK), 1)
        kill = jnp.logical_and(n == 0, col < ATTN_BLOCK)
        H = HEADS_PER_GROUP
        for j in range(nqb):
            rows = slice(j * ATTN_BLOCK, (j + 1) * ATTN_BLOCK)
            keys = slice(j * ATTN_BLOCK, (j + 2) * ATTN_BLOCK)
            for h in range(H):
                sl = slice(h * HEAD_DIM, (h + 1) * HEAD_DIM)
                s_scr[h] = lax.dot_general(qv[rows, sl], kk[keys, sl], NT, preferred_element_type=F32)
            s = s_scr[...] * scale + b_ref[...]
            if j == 0:
                s = jnp.where(kill[None], NEG, s)
            m = jnp.max(s, axis=-1, keepdims=True)
            p = jnp.exp(s - m)
            den = jnp.sum(p, axis=-1, keepdims=True)
            p_scr[...] = p.astype(BF16)
            inv = 1.0 / den
            lse = m + jnp.log(den)
            for h in range(H):
                sl = slice(h * HEAD_DIM, (h + 1) * HEAD_DIM)
                o_ref[rows, sl] = jnp.dot(p_scr[h], vv[keys, sl], preferred_element_type=F32) * inv[h]
                l_ref[rows, sl] = jnp.broadcast_to(lse[h], (ATTN_BLOCK, HEAD_DIM))

    def cur(c):
        return pl.BlockSpec((tq, W), lambda r, n: (r * ns + n, c))

    def halo(c):
        return pl.BlockSpec((ATTN_BLOCK, W), lambda r, n: (jnp.maximum((r * ns + n) * nqb - 1, 0), c))

    return pl.pallas_call(
        body, grid=(d, ns),
        in_specs=[cur(0), halo(1), cur(1), halo(2), cur(2), pl.BlockSpec(bias.shape, lambda r, n: (0, 0, 0))],
        out_specs=[cur(0), cur(0)], out_shape=[S((T, W), F32)] * 2,
        scratch_shapes=[pltpu.VMEM((HEADS_PER_GROUP, ATTN_BLOCK, 2 * ATTN_BLOCK), F32),
                        pltpu.VMEM((HEADS_PER_GROUP, ATTN_BLOCK, 2 * ATTN_BLOCK), BF16)],
        compiler_params=_cparams(("parallel", "arbitrary")), name=name)(qkv, qkv, qkv, qkv, qkv, bias)


def _attn_bwd(qkv, bias, lse, do, dd, d, name):
    T = qkv.shape[0]
    seg, nqb, tq, ns = _attn_tiles(T, d, ATTN_QB_BWD)
    W = ATTN_OUT_WIDTH
    B = ATTN_BLOCK
    scale = HEAD_DIM ** -0.5

    def body(q_ref, kh_ref, kc_ref, vh_ref, vc_ref, b_ref, l_ref, do_ref, dd_ref, dq_ref, dk_ref, dv_ref, db_ref, pk_ref, pv_ref,
             s_scr, dp_scr, p_scr, ds_scr):
        r, n = pl.program_id(0), pl.program_id(1)

        @pl.when(jnp.logical_and(r == 0, n == 0))
        def _():
            db_ref[...] = jnp.zeros_like(db_ref)

        @pl.when(n == 0)
        def _():
            pk_ref[...] = jnp.zeros_like(pk_ref)
            pv_ref[...] = jnp.zeros_like(pv_ref)

        @pl.when(n < ns)
        def _():
            qv = q_ref[...]
            kk = jnp.concatenate([kh_ref[...], kc_ref[...]], axis=0)
            vv = jnp.concatenate([vh_ref[...], vc_ref[...]], axis=0)
            lse_v, do_v, dd_v = l_ref[...], do_ref[...], dd_ref[...]
            col = lax.broadcasted_iota(jnp.int32, (B, 2 * B), 1)
            kill = jnp.logical_and(n == 0, col < B)
            dqs = [[None] * HEADS_PER_GROUP for _ in range(nqb)]
            dks = [[None] * HEADS_PER_GROUP for _ in range(nqb)]
            dvs = [[None] * HEADS_PER_GROUP for _ in range(nqb)]
            H = HEADS_PER_GROUP
            do_b = do_v.astype(BF16)
            for j in range(nqb):
                rows = slice(j * B, (j + 1) * B)
                keys = slice(j * B, (j + 2) * B)
                for h in range(H):
                    sl = slice(h * HEAD_DIM, (h + 1) * HEAD_DIM)
                    s_scr[h] = lax.dot_general(qv[rows, sl], kk[keys, sl], NT, preferred_element_type=F32)
                    dp_scr[h] = lax.dot_general(do_b[rows, sl], vv[keys, sl], NT, preferred_element_type=F32)
                lse_h = jnp.stack([lse_v[rows, h * HEAD_DIM:h * HEAD_DIM + 1] for h in range(H)], axis=0)
                dd_h = jnp.stack([dd_v[rows, h * HEAD_DIM:h * HEAD_DIM + 1] for h in range(H)], axis=0)
                s = s_scr[...] * scale + b_ref[...]
                if j == 0:
                    s = jnp.where(kill[None], NEG, s)
                p = jnp.exp(s - lse_h)
                ds = p * (dp_scr[...] - dd_h)
                db_ref[...] += ds
                p_scr[...] = p.astype(BF16)
                ds_scr[...] = (ds * scale).astype(BF16)
                for h in range(H):
                    sl = slice(h * HEAD_DIM, (h + 1) * HEAD_DIM)
                    dvs[j][h] = lax.dot_general(p_scr[h], do_b[rows, sl], TN, preferred_element_type=F32)
                    dqs[j][h] = jnp.dot(ds_scr[h], kk[keys, sl], preferred_element_type=F32)
                    dks[j][h] = lax.dot_general(ds_scr[h], qv[rows, sl], TN, preferred_element_type=F32)
            dq_ref[...] = jnp.concatenate([jnp.concatenate(dqs[j], axis=1) for j in range(nqb)], axis=0).astype(dq_ref.dtype)
            for parts, out_ref, pend in ((dks, dk_ref, pk_ref), (dvs, dv_ref, pv_ref)):
                full = [jnp.concatenate(parts[j], axis=1) for j in range(nqb)]
                if tq > B:
                    out_ref[:tq - B] = pend[:tq - B].astype(out_ref.dtype)
                out_ref[tq - B:] = (pend[tq - B:] + full[0][:B]).astype(out_ref.dtype)
                for j in range(nqb - 1):
                    pend[j * B:(j + 1) * B] = full[j][B:] + full[j + 1][:B]
                pend[tq - B:] = full[nqb - 1][B:]

        @pl.when(n == ns)
        def _():
            dk_ref[...] = pk_ref[...].astype(dk_ref.dtype)
            dv_ref[...] = pv_ref[...].astype(dv_ref.dtype)

    def cur(c):
        return pl.BlockSpec((tq, W), lambda r, n: (r * ns + jnp.minimum(n, ns - 1), c))

    def halo(c):
        return pl.BlockSpec((B, W), lambda r, n: (jnp.maximum((r * ns + jnp.minimum(n, ns - 1)) * nqb - 1, 0), c))

    late = pl.BlockSpec((tq, W), lambda r, n: (r * ns + jnp.clip(n - 1, 0, ns - 1), 0))
    bspec = pl.BlockSpec(bias.shape, lambda r, n: (0, 0, 0))
    return pl.pallas_call(
        body, grid=(d, ns + 1),
        in_specs=[cur(0), halo(1), cur(1), halo(2), cur(2), bspec, cur(0), cur(0), cur(0)],
        out_specs=[cur(0), late, late, bspec],
        out_shape=[S((T, W), BF16)] * 3 + [S(bias.shape, F32)],
        scratch_shapes=[pltpu.VMEM((tq, W), F32), pltpu.VMEM((tq, W), F32)]
                       + [pltpu.VMEM((HEADS_PER_GROUP, B, 2 * B), t) for t in (F32, F32, BF16, BF16)],
        compiler_params=_cparams(("arbitrary", "arbitrary")), name=name,
    )(qkv, qkv, qkv, qkv, qkv, bias, lse, do, dd)


CONV_TM, CONV_TC = 512, 1024


def _shift_down(x, halo8, s, row8):
    xr = pltpu.roll(x, s, 0)
    first = jnp.where(row8 < s, pltpu.roll(halo8, s, 0), xr[:8])
    return jnp.concatenate([first, xr[8:]], axis=0)


def _shift_up(x, halo8, s, row8):
    n = x.shape[0]
    xr = pltpu.roll(x, n - s, 0)
    last = jnp.where(row8 >= 8 - s, pltpu.roll(halo8, 8 - s, 0), xr[n - 8:])
    return jnp.concatenate([xr[:n - 8], last], axis=0)


def _conv_fwd(x, w, b, name):
    T, C = x.shape
    tm, tc = min(CONV_TM, T), CONV_TC

    def body(x_ref, p_ref, w_ref, b_ref, u_ref, a_ref):
        ti = pl.program_id(1)
        xv = x_ref[...]
        p8 = jnp.where(ti == 0, 0.0, p_ref[...])
        wv = w_ref[...]
        row8 = lax.broadcasted_iota(jnp.int32, (8, tc), 0)
        u = xv * wv[3:4] + b_ref[...]
        for s in (1, 2, 3):
            u = u + _shift_down(xv, p8, s, row8) * wv[3 - s:4 - s]
        u_ref[...] = u
        a_ref[...] = _silu(u)

    cur = pl.BlockSpec((tm, tc), lambda cj, ti: (ti, cj))
    halo = pl.BlockSpec((8, tc), lambda cj, ti: (jnp.maximum(ti * (tm // 8) - 1, 0), cj))
    return pl.pallas_call(
        body, grid=(C // tc, T // tm),
        in_specs=[cur, halo, pl.BlockSpec((CONV_WIDTH, tc), lambda cj, ti: (0, cj)), pl.BlockSpec((1, tc), lambda cj, ti: (0, cj))],
        out_specs=[cur, cur], out_shape=[S((T, C), F32)] * 2,
        compiler_params=_cparams(("parallel", "arbitrary")), name=name)(x, x, w, b)


def _conv_bwd(dact, u, x, w, name):
    T, C = x.shape
    tm, tc = min(CONV_TM, T), CONV_TC
    nt = T // tm

    def body(d_ref, dn_ref, u_ref, un_ref, x_ref, w_ref, dx_ref, dw_ref, db_ref):
        ti = pl.program_id(1)

        @pl.when(ti == 0)
        def _():
            dw_ref[...] = jnp.zeros_like(dw_ref)
            db_ref[...] = jnp.zeros_like(db_ref)

        du = d_ref[...] * _dsilu(u_ref[...])
        dun = jnp.where(ti == nt - 1, 0.0, dn_ref[...] * _dsilu(un_ref[...]))
        xv = x_ref[...]
        wv = w_ref[...]
        row8 = lax.broadcasted_iota(jnp.int32, (8, tc), 0)
        dx = du * wv[3:4]
        dws = [None] * CONV_WIDTH
        dws[3] = jnp.sum(du * xv, axis=0, keepdims=True)
        for s in (1, 2, 3):
            up = _shift_up(du, dun, s, row8)
            dx = dx + up * wv[3 - s:4 - s]
            dws[3 - s] = jnp.sum(up * xv, axis=0, keepdims=True)
        dx_ref[...] = dx.astype(dx_ref.dtype)
        dw_ref[...] += jnp.concatenate(dws, axis=0)
        db_ref[...] += jnp.sum(du, axis=0, keepdims=True)

    cur = pl.BlockSpec((tm, tc), lambda cj, ti: (ti, cj))
    nxt = pl.BlockSpec((8, tc), lambda cj, ti: (jnp.minimum((ti + 1) * (tm // 8), T // 8 - 1), cj))
    return pl.pallas_call(
        body, grid=(C // tc, nt),
        in_specs=[cur, nxt, cur, nxt, cur, pl.BlockSpec((CONV_WIDTH, tc), lambda cj, ti: (0, cj))],
        out_specs=[cur, pl.BlockSpec((CONV_WIDTH, tc), lambda cj, ti: (0, cj)), pl.BlockSpec((1, tc), lambda cj, ti: (0, cj))],
        out_shape=[S((T, C), BF16), S((CONV_WIDTH, C), F32), S((1, C), F32)],
        compiler_params=_cparams(("parallel", "arbitrary")), name=name)(dact, dact, u, u, x, w)


def _ssd_consts():
    i = np.arange(SSD_CHUNK)
    tril = (i[None, :] <= i[:, None]).astype(np.float32)
    trils = (i[None, :] < i[:, None]).astype(np.float32)
    head = np.repeat(np.arange(N_SSM_HEADS), D_INNER // N_SSM_HEADS)
    et = (head[None, :] == np.arange(N_SSM_HEADS)[:, None]).astype(np.float32)
    c = lambda a: jnp.asarray(a, BF16)
    return dict(tril=c(tril), triu=c(tril.T), trils=c(trils), trius=c(trils.T), et=c(et), e=c(et.T))


def _ssd_common(act_ref, dt_ref, dtT_ref, al_ref, alT_ref, tril_ref, triu_ref, et_ref):
    a_row = -jnp.exp(al_ref[...])
    a_col = -jnp.exp(alT_ref[...])
    dt, dtT = dt_ref[...], dtT_ref[...]
    la = _dotx_l(tril_ref[...], dt * a_row)
    laT = _dotx_r(dtT * a_col, triu_ref[...])
    et = et_ref[...]
    la_e = _dotx_r(la, et)
    dt_e = _dotx_r(dt, et, parts=2)
    x = act_ref[:, :D_INNER]
    xdt = x * dt_e
    la_q = la_e[SSD_CHUNK - 1:SSD_CHUNK, :]
    return a_row, a_col, dt, dtT, la, laT, la_e, dt_e, x, xdt, la_q


def _decay(la, laT, h, causal):
    seg = la[:, h:h + 1] - laT[h:h + 1, :]
    return jnp.exp(jnp.where(causal, seg, NEG))


def _ssd_fwd(act, dt, dtT, alog, dskip_e, cs, name, comm=None):
    T = act.shape[0]
    nc = T // SSD_CHUNK
    Q, G, GW = SSD_CHUNK, N_SSM_GROUPS, D_INNER // N_SSM_GROUPS

    def body(act_ref, dt_ref, dtT_ref, al_ref, alT_ref, dsk_ref, tril_ref, triu_ref, et_ref, y_ref, st_ref, scr, m_scr):
        @pl.when(pl.program_id(0) == 0)
        def _():
            scr[...] = jnp.zeros_like(scr)
        st_ref[0] = scr[...]
        a_row, a_col, dtv, dtTv, la, laT, la_e, dt_e, x, xdt, la_q = _ssd_common(
            act_ref, dt_ref, dtT_ref, al_ref, alT_ref, tril_ref, triu_ref, et_ref)
        ela = jnp.exp(la_e)
        xdt_b = xdt.astype(BF16)
        xdte_b = (xdt * jnp.exp(la_q - la_e)).astype(BF16)
        ela_q = jnp.exp(la_q)
        causal = lax.broadcasted_iota(jnp.int32, (Q, Q), 0) >= lax.broadcasted_iota(jnp.int32, (Q, Q), 1)
        for g in range(G):
            gs = slice(g * GW, (g + 1) * GW)
            Bg = act_ref[:, D_INNER + g * D_STATE:D_INNER + (g + 1) * D_STATE].astype(BF16)
            Cg = act_ref[:, D_INNER + G * D_STATE + g * D_STATE:D_INNER + G * D_STATE + (g + 1) * D_STATE].astype(BF16)
            cb = lax.dot_general(Cg, Bg, NT, preferred_element_type=F32)
            st = scr[g]
            y_inter = jnp.dot(Cg, st.astype(BF16), preferred_element_type=F32) * ela[:, gs]
            for hh in range(HEADS_PER_GROUP):
                m_scr[hh] = (cb * _decay(la, laT, g * HEADS_PER_GROUP + hh, causal)).astype(BF16)
            ys = []
            for hh in range(HEADS_PER_GROUP):
                h = g * HEADS_PER_GROUP + hh
                ys.append(jnp.dot(m_scr[hh], xdt_b[:, h * HEAD_DIM:(h + 1) * HEAD_DIM], preferred_element_type=F32))
            y_ref[:, gs] = jnp.concatenate(ys, axis=1) + y_inter + x[:, gs] * dsk_ref[:, gs]
            scr[g] = st * ela_q[:, gs] + lax.dot_general(Bg, xdte_b[:, gs], TN, preferred_element_type=F32)

    full = lambda a: pl.BlockSpec(a.shape, lambda c: (0,) * a.ndim)
    al, alT = alog.reshape(1, -1), alog.reshape(-1, 1)
    res, comm_res = _carrier_call(
        body, nc,
        [pl.BlockSpec((Q, XBC_WIDTH), lambda c: (c, 0)), pl.BlockSpec((Q, N_SSM_HEADS), lambda c: (c, 0)),
         pl.BlockSpec((N_SSM_HEADS, Q), lambda c: (0, c)), full(al), full(alT), full(dskip_e),
         full(cs["tril"]), full(cs["triu"]), full(cs["et"])],
        [pl.BlockSpec((Q, D_INNER), lambda c: (c, 0)), pl.BlockSpec((1, G, D_STATE, GW), lambda c: (c, 0, 0, 0))],
        [S((T, D_INNER), F32), S((nc, G, D_STATE, GW), F32)],
        [pltpu.VMEM((G, D_STATE, GW), F32), pltpu.VMEM((HEADS_PER_GROUP, Q, Q), BF16)],
        [act, dt, dtT, al, alT, dskip_e, cs["tril"], cs["triu"], cs["et"]], comm, name)
    return res if comm is None else (res, comm_res)


def _ssd_bwd(dy, act, dt, dtT, alog, dskip_e, states, cs, name, comm=None):
    T = act.shape[0]
    nc = T // SSD_CHUNK
    Q, G, GW, H = SSD_CHUNK, N_SSM_GROUPS, D_INNER // N_SSM_GROUPS, N_SSM_HEADS

    def body(dy_ref, act_ref, dt_ref, dtT_ref, al_ref, alT_ref, dsk_ref, stp_ref, tril_ref, triu_ref, trils_ref, trius_ref,
             et_ref, e_ref, dact_ref, ddt_ref, ddtT_ref, da_ref, daT_ref, dsk_out_ref, dst, wbuf, ubuf, vbuf, sbuf, dm_scr, m_scr):
        @pl.when(pl.program_id(0) == 0)
        def _():
            dst[...] = jnp.zeros_like(dst)
            da_ref[...] = jnp.zeros_like(da_ref)
            daT_ref[...] = jnp.zeros_like(daT_ref)
            dsk_out_ref[...] = jnp.zeros_like(dsk_out_ref)
        a_row, a_col, dtv, dtTv, la, laT, la_e, dt_e, x, xdt, la_q = _ssd_common(
            act_ref, dt_ref, dtT_ref, al_ref, alT_ref, tril_ref, triu_ref, et_ref)
        dyv = dy_ref[...]
        ela = jnp.exp(la_e)
        e_end = jnp.exp(la_q - la_e)
        ela_q = jnp.exp(la_q)
        dye_b = (dyv * ela).astype(BF16)
        dy_b = dyv.astype(BF16)
        xdt_b = xdt.astype(BF16)
        xdte_b = (xdt * e_end).astype(BF16)
        ri = lax.broadcasted_iota(jnp.int32, (Q, Q), 0)
        ci = lax.broadcasted_iota(jnp.int32, (Q, Q), 1)
        causal = ri >= ci
        rows = []
        for g in range(G):
            gs = slice(g * GW, (g + 1) * GW)
            Bg = act_ref[:, D_INNER + g * D_STATE:D_INNER + (g + 1) * D_STATE].astype(BF16)
            Cg = act_ref[:, D_INNER + G * D_STATE + g * D_STATE:D_INNER + G * D_STATE + (g + 1) * D_STATE].astype(BF16)
            cb = lax.dot_general(Cg, Bg, NT, preferred_element_type=F32)
            stp = stp_ref[0, g]
            stp_b = stp.astype(BF16)
            dstv = dst[g]
            dst_b = dstv.astype(BF16)
            y_inter = jnp.dot(Cg, stp_b, preferred_element_type=F32) * ela[:, gs]
            wbuf[:, gs] = dyv[:, gs] * y_inter
            dxdt_state = jnp.dot(Bg, dst_b, preferred_element_type=F32) * e_end[:, gs]
            ubuf[:, gs] = dxdt_state * xdt[:, gs]
            dC = lax.dot_general(dye_b[:, gs], stp_b, NT, preferred_element_type=F32)
            dB = lax.dot_general(xdte_b[:, gs], dst_b, NT, preferred_element_type=F32)
            sbuf[:, gs] = jnp.broadcast_to(jnp.sum(dstv * stp, axis=0, keepdims=True), (8, GW))
            dst[g] = dstv * ela_q[:, gs] + lax.dot_general(Cg, dye_b[:, gs], TN, preferred_element_type=F32)
            for hh in range(HEADS_PER_GROUP):
                hs = slice((g * HEADS_PER_GROUP + hh) * HEAD_DIM, (g * HEADS_PER_GROUP + hh + 1) * HEAD_DIM)
                dm_scr[hh] = lax.dot_general(dy_b[:, hs], xdt_b[:, hs], NT, preferred_element_type=F32)
            dG = jnp.zeros((Q, Q), F32)
            for hh in range(HEADS_PER_GROUP):
                L = _decay(la, laT, g * HEADS_PER_GROUP + hh, causal)
                M = cb * L
                dM = dm_scr[hh]
                dG = dG + dM * L
                W = dM * M
                rows.append(jnp.sum(W.T, axis=0, keepdims=True) - jnp.sum(W, axis=0, keepdims=True))
                m_scr[hh] = M.astype(BF16)
            dxs = []
            for hh in range(HEADS_PER_GROUP):
                hs = slice((g * HEADS_PER_GROUP + hh) * HEAD_DIM, (g * HEADS_PER_GROUP + hh + 1) * HEAD_DIM)
                dxs.append(lax.dot_general(m_scr[hh], dy_b[:, hs], TN, preferred_element_type=F32))
            dG_b = dG.astype(BF16)
            dC = dC + jnp.dot(dG_b, Bg, preferred_element_type=F32)
            dB = dB + lax.dot_general(dG_b, Cg, TN, preferred_element_type=F32)
            dxdt = jnp.concatenate(dxs, axis=1) + dxdt_state
            vbuf[:, gs] = dxdt * x[:, gs]
            dact_ref[:, gs] = dxdt * dt_e[:, gs] + dyv[:, gs] * dsk_ref[:, gs]
            dact_ref[:, D_INNER + g * D_STATE:D_INNER + (g + 1) * D_STATE] = dB
            dact_ref[:, D_INNER + G * D_STATE + g * D_STATE:D_INNER + G * D_STATE + (g + 1) * D_STATE] = dC
        e = e_ref[...]
        w = _dotx_r(wbuf[...], e, parts=2)
        u = _dotx_r(ubuf[...], e, parts=2)
        vx = _dotx_r(vbuf[...], e, parts=2)
        dsk = _dotx_r(jnp.broadcast_to(jnp.sum(dyv * x, axis=0, keepdims=True), (8, D_INNER)), e, parts=2)[0:1]
        s0 =_dotx_r(sbuf[...], e, parts=2)[0:1] * jnp.exp(la[Q - 1:Q, :])
        ddelta = _dotx_l(triu_ref[...], w) + _dotx_l(trils_ref[...], u) + s0
        ddt_ref[...] = ddelta * a_row + vx
        ddeltaT = _dotx_r(jnp.concatenate(rows, axis=0), tril_ref[...])
        ddtT_ref[...] = ddeltaT * a_col
        da_ref[...] += jnp.sum(ddelta * dtv, axis=0, keepdims=True)
        daT_ref[...] += jnp.sum(ddeltaT * dtTv, axis=1, keepdims=True)
        dsk_out_ref[...] += dsk

    rev = lambda c: nc - 1 - c
    full = lambda a: pl.BlockSpec(a.shape, lambda c: (0,) * a.ndim)
    al, alT = alog.reshape(1, -1), alog.reshape(-1, 1)
    consts = [cs[k] for k in ("tril", "triu", "trils", "trius", "et", "e")]
    res, comm_res = _carrier_call(
        body, nc,
        [pl.BlockSpec((Q, D_INNER), lambda c: (rev(c), 0)), pl.BlockSpec((Q, XBC_WIDTH), lambda c: (rev(c), 0)),
         pl.BlockSpec((Q, H), lambda c: (rev(c), 0)), pl.BlockSpec((H, Q), lambda c: (0, rev(c))),
         full(al), full(alT), full(dskip_e), pl.BlockSpec((1, G, D_STATE, GW), lambda c: (rev(c), 0, 0, 0))]
        + [full(a) for a in consts],
        [pl.BlockSpec((Q, XBC_WIDTH), lambda c: (rev(c), 0)), pl.BlockSpec((Q, H), lambda c: (rev(c), 0)),
         pl.BlockSpec((H, Q), lambda c: (0, rev(c))), pl.BlockSpec((1, H), lambda c: (0, 0)),
         pl.BlockSpec((H, 1), lambda c: (0, 0)), pl.BlockSpec((1, H), lambda c: (0, 0))],
        [S((T, XBC_WIDTH), F32), S((T, H), F32), S((H, T), F32), S((1, H), F32), S((H, 1), F32), S((1, H), F32)],
        [pltpu.VMEM((G, D_STATE, GW), F32), pltpu.VMEM((Q, D_INNER), F32), pltpu.VMEM((Q, D_INNER), F32),
         pltpu.VMEM((Q, D_INNER), F32), pltpu.VMEM((8, D_INNER), F32),
         pltpu.VMEM((HEADS_PER_GROUP, Q, Q), F32), pltpu.VMEM((HEADS_PER_GROUP, Q, Q), BF16)],
        [dy, act, dt, dtT, al, alT, dskip_e, states] + consts, comm, name)
    return res if comm is None else (res, comm_res)


def _a_log_grad(da, daT_row, alog, name):
    def body(a_ref, b_ref, al_ref, o_ref):
        o_ref[...] = (a_ref[...] + b_ref[...]) * (-jnp.exp(al_ref[...]))
    return pl.pallas_call(body, out_shape=S((1, N_SSM_HEADS), F32), name=name)(da, daT_row, alog.reshape(1, -1))


def _place():
    x, y, c = lax.axis_index("x"), lax.axis_index("y"), lax.axis_index("c")
    return x, y, c, [(1 - x, y), (x, 1 - y), (1 - x, 1 - y)]


def _all_gather(shards, name):
    npc = len(shards)

    def body(*refs):
        x_refs, o_refs = refs[:npc], refs[npc:2 * npc]
        send_sems, recv_sems, local_sems = refs[2 * npc:]
        x, y, c, chips = _place()
        me, sib = (x, y, c), (x, y, 1 - c)

        def rows(dev, i):
            return o_refs[i].at[4 * dev[0] + 2 * dev[1] + dev[2]]

        def copy(k, i, block, to, src=None):
            return pltpu.make_async_remote_copy(
                src_ref=rows(block, i) if src is None else src, dst_ref=rows(block, i),
                send_sem=send_sems.at[k, i], recv_sem=recv_sems.at[k, i], device_id=to, device_id_type=MESH)

        mine = [pltpu.make_async_copy(x_refs[i], rows(me, i), local_sems.at[i]) for i in range(npc)]
        for cp in mine:
            cp.start()
        first = []
        for i in range(npc):
            first.append(copy(0, i, me, sib, src=x_refs[i]))
            first += [copy(1 + j, i, me, (*chip, c), src=x_refs[i]) for j, chip in enumerate(chips)]
        for cp in first:
            cp.start()
        passed = []
        for i in range(npc):
            for j, chip in enumerate(chips):
                copy(1 + j, i, (*chip, c), me).wait_recv()
                cp = copy(4 + j, i, (*chip, c), sib)
                cp.start()
                passed.append(cp)
        for i in range(npc):
            copy(0, i, sib, me).wait_recv()
            for j, chip in enumerate(chips):
                copy(4 + j, i, (*chip, 1 - c), me).wait_recv()
        for cp in first + passed:
            cp.wait_send()
        for cp in mine:
            cp.wait()

    anys = pl.BlockSpec(memory_space=pl.ANY)
    return pl.pallas_call(
        body, in_specs=[anys] * npc, out_specs=[anys] * npc, out_shape=[S((N_DEV,) + s.shape, s.dtype) for s in shards],
        scratch_shapes=[pltpu.SemaphoreType.DMA((7, npc)), pltpu.SemaphoreType.DMA((7, npc)), pltpu.SemaphoreType.DMA((npc,))],
        name=name)(*shards)


def _to_sibling(to_sib, name):
    npc = len(to_sib)

    def body(*refs):
        s_refs, o_refs, send_sems, recv_sems = refs[:npc], refs[npc:2 * npc], refs[2 * npc], refs[2 * npc + 1]
        x, y, c, _ = _place()
        cps = [pltpu.make_async_remote_copy(
            src_ref=s_refs[i], dst_ref=o_refs[i], send_sem=send_sems.at[i], recv_sem=recv_sems.at[i],
            device_id=(x, y, 1 - c), device_id_type=MESH) for i in range(npc)]
        for cp in cps:
            cp.start()
        for cp in cps:
            cp.wait()

    anys = pl.BlockSpec(memory_space=pl.ANY)
    return pl.pallas_call(
        body, in_specs=[anys] * npc, out_specs=[anys] * npc, out_shape=[S(s.shape, s.dtype) for s in to_sib],
        scratch_shapes=[pltpu.SemaphoreType.DMA((npc,)), pltpu.SemaphoreType.DMA((npc,))],
        name=name)(*to_sib)


def _to_chips(parts, name):
    npc = len(parts)

    def body(*refs):
        p_refs, o_refs, send_sems, recv_sems = refs[:npc], refs[npc:2 * npc], refs[2 * npc], refs[2 * npc + 1]
        x, y, c, chips = _place()
        cps = [pltpu.make_async_remote_copy(
            src_ref=p_refs[i].at[2 * chip[0] + chip[1]], dst_ref=o_refs[i].at[j], send_sem=send_sems.at[j, i],
            recv_sem=recv_sems.at[j, i], device_id=(*chip, c), device_id_type=MESH)
            for i in range(npc) for j, chip in enumerate(chips)]
        for cp in cps:
            cp.start()
        for cp in cps:
            cp.wait()

    anys = pl.BlockSpec(memory_space=pl.ANY)
    return pl.pallas_call(
        body, in_specs=[anys] * npc, out_specs=[anys] * npc, out_shape=[S((3,) + p.shape[1:], p.dtype) for p in parts],
        scratch_shapes=[pltpu.SemaphoreType.DMA((3, npc)), pltpu.SemaphoreType.DMA((3, npc))],
        name=name)(*parts)


def _rs_begin(pieces, name):
    c = lax.axis_index("c")
    by_core = [p.reshape(4, 2, p.shape[1], p.shape[2]) for p in pieces]
    to_sib = [lax.dynamic_index_in_dim(p, 1 - c, axis=1, keepdims=False).astype(BF16) for p in by_core]
    keep = [lax.dynamic_index_in_dim(p, c, axis=1, keepdims=False) for p in by_core]
    from_sib = _to_sibling(to_sib, name + "_d2d")

    def add1(a, b):
        s = a + b
        return [s, s], []

    parts, parts_b = [], []
    for i, (k, f) in enumerate(zip(keep, from_sib)):
        _, r, C = k.shape
        p, pb = _rowwise(add1, [k.reshape(4 * r, C), f.reshape(4 * r, C)], [], [(C, F32), (C, BF16)], tm=2048, name=f"{name}_add1_{i}")
        parts.append(p.reshape(4, r, C))
        parts_b.append(pb.reshape(4, r, C))
    return parts, parts_b


def _rs_finish(parts, got, name):
    x, y = lax.axis_index("x"), lax.axis_index("y")

    def add2(a, b, c_, d_):
        return [((a + b) + c_) + d_], []

    outs = []
    for i, (p, g) in enumerate(zip(parts, got)):
        own = lax.dynamic_index_in_dim(p, 2 * x + y, axis=0, keepdims=False)
        outs.append(_rowwise(add2, [own, g[0], g[1], g[2]], [], [(p.shape[2], F32)], tm=2048, name=f"{name}_add2_{i}")[0])
    return outs


def _reduce_scatter(pieces, name):
    parts, parts_b = _rs_begin(pieces, name)
    return _rs_finish(parts, _to_chips(parts_b, name + "_ici"), name)


class _GradReduce:
    EARLY = ("w_ffn_in", "w_ffn_out", "w_out", "w_attn_branch", "w_ssm_branch")

    def __init__(self):
        self.out, self.keys, self.parts, self.parts_b = {}, [], [], []

    def _begin(self, l, names, gr, name):
        parts, parts_b = _rs_begin([_shard(nm, gr[nm]) for nm in names], name)
        self.keys += [(l, nm) for nm in names]
        self.parts += parts
        self.parts_b += parts_b

    def carry_fn(self, l):
        if l != 0:
            return None

        def fn(gr):
            self._begin(0, self.EARLY, gr, "rs_early_l0")
            return _comm_to_chips(self.parts_b)
        return fn

    def done(self, l, gr, carried):
        if l == DEPTH - 1:
            self._begin(l, BIG, gr, f"rs_l{l}")
            return
        for key, o in zip(self.keys, _rs_finish(self.parts, carried, "rs_carried")):
            self.out[key] = o
        rest = [nm for nm in BIG if nm not in self.EARLY]
        for nm, o in zip(rest, _reduce_scatter([_shard(nm, gr[nm]) for nm in rest], "rs_rest_l0")):
            self.out[(0, nm)] = o


def _all_reduce_small(v, name):
    R, C = v.shape

    def body(x_ref, out_ref, buf, send_sems, recv_sems):
        x, y, c, chips = _place()
        me, sib = (x, y, c), (x, y, 1 - c)

        def rows(dev):
            return buf.at[4 * dev[0] + 2 * dev[1] + dev[2]]

        def copy(k, block, to, src=None):
            return pltpu.make_async_remote_copy(
                src_ref=rows(block) if src is None else src, dst_ref=rows(block),
                send_sem=send_sems.at[k], recv_sem=recv_sems.at[k], device_id=to, device_id_type=MESH)

        buf[4 * x + 2 * y + c] = x_ref[...]
        first = [copy(0, me, sib, src=x_ref)] + [copy(1 + j, me, (*chip, c), src=x_ref) for j, chip in enumerate(chips)]
        for cp in first:
            cp.start()
        passed = [copy(4 + j, (*chip, c), sib) for j, chip in enumerate(chips)]
        for j, chip in enumerate(chips):
            copy(1 + j, (*chip, c), me).wait_recv()
            passed[j].start()
        copy(0, sib, me).wait_recv()
        for j, chip in enumerate(chips):
            copy(4 + j, (*chip, 1 - c), me).wait_recv()
        for cp in first + passed:
            cp.wait_send()
        acc = buf[0]
        for j in range(1, N_DEV):
            acc = acc + buf[j]
        out_ref[...] = acc

    vm = pl.BlockSpec(memory_space=pltpu.VMEM)
    return pl.pallas_call(
        body, in_specs=[vm], out_specs=vm, out_shape=S((R, C), F32),
        scratch_shapes=[pltpu.VMEM((N_DEV, R, C), F32), pltpu.SemaphoreType.DMA((7,)), pltpu.SemaphoreType.DMA((7,))],
        compiler_params=pltpu.CompilerParams(vmem_limit_bytes=VMEM_LIMIT), name=name)(v)


SEG = (("q", 0, 1536), ("k", 1536, 1536), ("v", 3072, 1536), ("z", 4608, 2048), ("xbc", 6656, 3072), ("dt", 9728, 32), ("gl", 9760, 2048))


def _split_w_in(w_in_full):
    out = {}
    for nm, off, n in SEG:
        w = w_in_full[:, off:off + n]
        if nm == "dt":
            w = jnp.pad(w, ((0, 0), (0, LANES - n)))
        out[nm] = w
    W = ATTN_OUT_WIDTH
    out["qkv"] = [jnp.concatenate([out[s][:, g * W:(g + 1) * W] for s in ("q", "k", "v")], axis=1) for g in range(N_DIL)]
    out["qkv_t"] = [[out[s][:, g * W:(g + 1) * W] for s in ("q", "k", "v")] for g in range(N_DIL)]
    return out


def _layer_fwd(h, p, W, biases, cs, l, carry=None):
    T = h.shape[0]
    nm = lambda s: f"{s}_l{l}"
    sv = {"h_in": h}
    xns = _rmsnorm_fwd(h, p["norm1_w"], nm("norm1"), dils=[d for _, d in DILATED_GROUPS[1:]])
    xn = xns[0]
    wi = W["w_in"]
    z = _mm(xn, wi["z"], out_dtype=BF16, name=nm("proj_z"))
    xbc = _mm(xn, wi["xbc"], name=nm("proj_xbc"))
    dt_raw = _mm(xn, wi["dt"], name=nm("proj_dt"))
    gl = _mm(xn, wi["gl"], out_dtype=BF16, name=nm("proj_gl"))
    os_, ls, qkvs = [], [], []
    for g, (window, dil) in enumerate(DILATED_GROUPS):
        qkv = _mm(xns[g], wi["qkv"][g], out_dtype=BF16, name=nm(f"proj_qkv_g{g}"))
        o, lse = _attn_fwd(qkv, biases[g], dil, nm(f"attn_fwd_g{g}"))
        os_.append(o)
        ls.append(lse)
        qkvs.append(qkv)
    attn_b, attn_f = _combine_fwd(os_, ls, nm("combine"))
    u_conv, act = _conv_fwd(xbc, p["conv_w"], p["conv_b"].reshape(1, -1), nm("conv"))
    dt = _dt_fwd(dt_raw, p["dt_bias"], nm("dt"))
    dtT = dt.T
    dskip_e = jnp.repeat(p["d_skip"], D_INNER // N_SSM_HEADS).reshape(1, -1)
    carried = None
    if carry is None:
        y, states = _ssd_fwd(act, dt, dtT, p["a_log"], dskip_e, cs, nm("ssd_fwd"))
        ssm = _ssm_norm_fwd(y, z, p["ssm_norm_w"], nm("ssm_norm"))
    else:
        (y, states), spread = _ssd_fwd(act, dt, dtT, p["a_log"], dskip_e, cs, nm("ssd_fwd"), comm=carry)
        ssm, carried = _ssm_norm_fwd(y, z, p["ssm_norm_w"], nm("ssm_norm"), comm=_comm_gather_pass(spread))
    a_br = _mm(attn_b, W["w_attn_branch"], out_dtype=BF16, name=nm("attn_branch"))
    s_br = _mm(ssm, W["w_ssm_branch"], out_dtype=BF16, name=nm("ssm_branch"))
    h_mid, merged = _gate_out_proj(a_br, s_br, gl, h, W["w_out"], nm("gate_out_proj"))
    xn2 = _rmsnorm_fwd(h_mid, p["norm2_w"], nm("norm2"))[0]
    u_ffn, ffn_act = _mm(xn2, W["w_ffn_in_p"], tm=512, tn=D_FF, epilogue=_swiglu_epilogue, outs=[(2 * D_FF, BF16), (D_FF, BF16)],
                         name=nm("ffn_in_swiglu"))
    h_out = _mm(ffn_act, W["w_ffn_out"], acc=h_mid, name=nm("ffn_out"))
    sv.update(xn=xn, xns=xns, qkvs=qkvs, z=z, xbc=xbc, dt_raw=dt_raw, gl=gl, ls=ls, attn_b=attn_b, attn_f=attn_f, u_conv=u_conv,
              act=act, dt=dt, dtT=dtT, dskip_e=dskip_e, y=y, states=states, ssm=ssm, a_br=a_br, s_br=s_br, merged=merged,
              h_mid=h_mid, xn2=xn2, u_ffn=u_ffn, ffn_act=ffn_act)
    return h_out, sv, carried


def _layer_bwd(dh, sv, p, W, biases, cs, head_ones, l, carry_fn=None):
    T = dh.shape[0]
    nm = lambda s: f"{s}_l{l}"
    gr = {}
    du = _mm(dh, W["w_ffn_out"], tb=True, tm=512, tn=FFN_HALF, extras=[sv["u_ffn"]], epilogue=_dswiglu_epilogue, outs=[(2 * D_FF, BF16)],
             name=nm("d_ffn_act_swiglu"))
    gr["w_ffn_out"] = _mm(sv["ffn_act"], dh, ta=True, name=nm("g_ffn_out"))
    dxn2 = _mm(du, W["w_ffn_in_p"], tb=True, name=nm("d_xn2"))
    gr["w_ffn_in"] = _ffn_unperm(_mm(sv["xn2"], du, ta=True, name=nm("g_ffn_in")))
    dh_mid, gr["norm2_w"] = _rmsnorm_bwd(dxn2, sv["h_mid"], p["norm2_w"], dh, nm("d_norm2"))
    gr["w_out"] = _mm(sv["merged"], dh_mid, ta=True, name=nm("g_out"))
    d_a, d_s, dgl = _d_out_proj_gate(dh_mid, sv["a_br"], sv["s_br"], sv["gl"], W["w_out"], nm("d_out_proj_gate"))
    dattn = _mm(d_a, W["w_attn_branch"], tb=True, out_dtype=BF16, name=nm("d_attn"))
    gr["w_attn_branch"] = _mm(sv["attn_b"], d_a, ta=True, name=nm("g_attn_branch"))
    dssm = _mm(d_s, W["w_ssm_branch"], tb=True, out_dtype=BF16, name=nm("d_ssm"))
    gr["w_ssm_branch"] = _mm(sv["ssm"], d_s, ta=True, name=nm("g_ssm_branch"))
    dy, dz, gr["ssm_norm_w"] = _ssm_norm_bwd(dssm, sv["y"], sv["z"], p["ssm_norm_w"], nm("d_ssm_norm"))
    ssd_args = (dy, sv["act"], sv["dt"], sv["dtT"], p["a_log"], sv["dskip_e"], sv["states"], cs, nm("ssd_bwd"))
    carried = None
    if carry_fn is None:
        dact_c, ddt_a, ddt_bT, da, daT, dskip = _ssd_bwd(*ssd_args)
    else:
        (dact_c, ddt_a, ddt_bT, da, daT, dskip), carried = _ssd_bwd(*ssd_args, comm=carry_fn(gr))
    gr["a_log"] = _a_log_grad(da, daT.T, p["a_log"], nm("g_a_log")).reshape(-1)
    gr["d_skip"] = dskip.reshape(-1)
    ddt_raw, ddt_bias = _dt_bwd(ddt_a, ddt_bT.T, sv["dt_raw"], p["dt_bias"], nm("d_dt"))
    gr["dt_bias"] = ddt_bias.reshape(-1)
    dxbc, gr["conv_w"], dconv_b = _conv_bwd(dact_c, sv["u_conv"], sv["xbc"], p["conv_w"], nm("d_conv"))
    gr["conv_b"] = dconv_b.reshape(-1)
    outs = _combine_bwd(dattn, sv["attn_f"], sv["ls"], head_ones, nm("d_combine"))
    wi = W["w_in"]
    dbias, dxn = [], None
    gqkv = [[None] * N_DIL for _ in range(3)]
    for g, (window, dil) in enumerate(DILATED_GROUPS):
        dq, dk, dv, db = _attn_bwd(sv["qkvs"][g], biases[g], sv["ls"][g], outs[2 * g], outs[2 * g + 1], dil, nm(f"attn_bwd_g{g}"))
        dbias.append(db)
        dxn = _mm_dil([dq, dk, dv], wi["qkv_t"][g], dil, dxn, nm(f"d_xn_qkv_g{g}"))
        for i, dseg in enumerate((dq, dk, dv)):
            gqkv[i][g] = _mm(sv["xns"][g], dseg, ta=True, name=nm(f"g_in_{'qkv'[i]}_g{g}"))
    parts = (("z", dz), ("xbc", dxbc), ("dt", ddt_raw), ("gl", dgl))
    gws = gqkv[0] + gqkv[1] + gqkv[2]
    for sname, dseg in parts:
        dxn = _mm(dseg, wi[sname], tb=True, acc=dxn, name=nm("d_xn_" + sname))
        gw = _mm(sv["xn"], dseg, ta=True, name=nm("g_in_" + sname))
        gws.append(gw[:, :N_SSM_HEADS] if sname == "dt" else gw)
    gr["w_in"] = jnp.concatenate(gws, axis=1)
    dh_in, gr["norm1_w"] = _rmsnorm_bwd(dxn, sv["h_in"], p["norm1_w"], dh_mid, nm("d_norm1"))
    return dh_in, gr, dbias, carried


def _step_local(x, tgt, small, Wfull, rel_bias, final_norm_w, prefetch=None, grad_reduce=None):
    cs = _ssd_consts()
    head = np.repeat(np.arange(HEADS_PER_GROUP), HEAD_DIM)
    head_ones = jnp.asarray(head[:, None] == head[None, :], BF16)
    biases, onehots = [], []
    for g, (window, dil) in enumerate(DILATED_GROUPS):
        onehot, valid = _bias_consts(dil, window // dil)
        rel_g_t = rel_bias[:, g * HEADS_PER_GROUP:(g + 1) * HEADS_PER_GROUP].T
        b = _bias_gather(rel_g_t, onehot, valid, f"bias_gather_g{g}")
        biases.append(b.reshape(HEADS_PER_GROUP, ATTN_BLOCK, 2 * ATTN_BLOCK))
        onehots.append(onehot)
    h, saved, carried = x, [], None
    Wfull = list(Wfull)
    for l in range(DEPTH):
        W = dict(prefetch[1](carried) if Wfull[l] is None else Wfull[l])
        W["w_in"] = _split_w_in(W["w_in"])
        W["w_ffn_in_p"] = _ffn_perm(W["w_ffn_in"])
        Wfull[l] = W
        h, sv, carried = _layer_fwd(h, small[l], W, biases, cs, l, prefetch[0] if prefetch is not None and l == 0 else None)
        saved.append(sv)
    dh, g_final, loss = _loss_head(h, final_norm_w, tgt, "loss_head")
    grads = [None] * DEPTH
    dbias_tot = [None] * N_DIL
    for l in reversed(range(DEPTH)):
        carry_fn = grad_reduce.carry_fn(l) if grad_reduce is not None else None
        dh, grads[l], dbias, carried = _layer_bwd(dh, saved[l], small[l], Wfull[l], biases, cs, head_ones, l, carry_fn)
        if grad_reduce is not None:
            grad_reduce.done(l, grads[l], carried)
        for g in range(N_DIL):
            dbias_tot[g] = dbias[g] if dbias_tot[g] is None else dbias_tot[g] + dbias[g]
    d_rel = jnp.concatenate(
        [_bias_scatter(dbias_tot[g].reshape(HEADS_PER_GROUP, -1), onehots[g], f"bias_scatter_g{g}").T for g in range(N_DIL)], axis=1)
    return loss, dh, grads, d_rel, g_final


def _unshard(nm, g):
    _, rows, cols = g.shape
    if nm in COL_SHARDED:
        return g.transpose(1, 0, 2).reshape(rows, N_DEV * cols)
    return g.reshape(N_DEV * rows, cols)


def _shard(nm, w):
    rows, cols = w.shape
    if nm in COL_SHARDED:
        return w.reshape(rows, N_DEV, cols // N_DEV).transpose(1, 0, 2)
    return w.reshape(N_DEV, rows // N_DEV, cols)


SMALL_LAYER = (("norm1_w", 1024), ("conv_w", 12288), ("conv_b", 3072), ("dt_bias", 32), ("a_log", 32), ("d_skip", 32),
               ("ssm_norm_w", 2048), ("norm2_w", 1024))
SMALL_GLOBAL = (("rel_bias", 768), ("final_norm_w", 1024), ("loss", 1))


def _pad128(v):
    n = v.shape[0]
    return jnp.pad(v, (0, -n % LANES))


def _pack_small(per_layer, glob):
    parts = [_pad128(per_layer[l][nm].reshape(-1)) for l in range(DEPTH) for nm, _ in SMALL_LAYER]
    parts += [_pad128(glob[nm].reshape(-1)) for nm, _ in SMALL_GLOBAL]
    flat = jnp.concatenate(parts)
    flat = jnp.pad(flat, (0, -flat.shape[0] % (8 * LANES)))
    return flat.reshape(-1, LANES)


def _unpack_small(packed):
    flat = packed.reshape(-1)
    per_layer, glob, off = [dict() for _ in range(DEPTH)], {}, 0
    for l in range(DEPTH):
        for nm, n in SMALL_LAYER:
            per_layer[l][nm] = flat[off:off + n]
            off += n + (-n % LANES)
    for nm, n in SMALL_GLOBAL:
        glob[nm] = flat[off:off + n]
        off += n + (-n % LANES)
    return per_layer, glob


def kernel(x, norm1_w, w_in, conv_w, conv_b, dt_bias, a_log, d_skip, ssm_norm_w, w_attn_branch, w_ssm_branch, w_out, norm2_w, w_ffn_in, w_ffn_out, rel_bias, final_norm_w, loss_target, m_norm1_w, m_w_in, m_conv_w, m_conv_b, m_dt_bias, m_a_log, m_d_skip, m_ssm_norm_w, m_w_attn_branch, m_w_ssm_branch, m_w_out, m_norm2_w, m_w_ffn_in, m_w_ffn_out, m_rel_bias, m_final_norm_w, v_norm1_w, v_w_in, v_conv_w, v_conv_b, v_dt_bias, v_a_log, v_d_skip, v_ssm_norm_w, v_w_attn_branch, v_w_ssm_branch, v_w_out, v_norm2_w, v_w_ffn_in, v_w_ffn_out, v_rel_bias, v_final_norm_w):
    big = dict(w_in=w_in, w_attn_branch=w_attn_branch, w_ssm_branch=w_ssm_branch, w_out=w_out, w_ffn_in=w_ffn_in, w_ffn_out=w_ffn_out)
    big_m = dict(w_in=m_w_in, w_attn_branch=m_w_attn_branch, w_ssm_branch=m_w_ssm_branch, w_out=m_w_out, w_ffn_in=m_w_ffn_in, w_ffn_out=m_w_ffn_out)
    big_v = dict(w_in=v_w_in, w_attn_branch=v_w_attn_branch, w_ssm_branch=v_w_ssm_branch, w_out=v_w_out, w_ffn_in=v_w_ffn_in, w_ffn_out=v_w_ffn_out)
    sm = dict(norm1_w=norm1_w, conv_w=conv_w, conv_b=conv_b, dt_bias=dt_bias, a_log=a_log, d_skip=d_skip, ssm_norm_w=ssm_norm_w, norm2_w=norm2_w)
    sm_m = dict(norm1_w=m_norm1_w, conv_w=m_conv_w, conv_b=m_conv_b, dt_bias=m_dt_bias, a_log=m_a_log, d_skip=m_d_skip, ssm_norm_w=m_ssm_norm_w, norm2_w=m_norm2_w)
    sm_v = dict(norm1_w=v_norm1_w, conv_w=v_conv_w, conv_b=v_conv_b, dt_bias=v_dt_bias, a_log=v_a_log, d_skip=v_d_skip, ssm_norm_w=v_ssm_norm_w, norm2_w=v_norm2_w)
    me = 4 * lax.axis_index("x") + 2 * lax.axis_index("y") + lax.axis_index("c")

    def full_weights(gathered):
        return {nm: _unshard(nm, g) for nm, g in zip(BIG, gathered)}

    Wfull = [full_weights(_all_gather([big[nm][0].astype(BF16) for nm in BIG], "all_gather_l0")), None]
    prefetch = (_comm_gather_spread([big[nm][DEPTH - 1].astype(BF16) for nm in BIG]), full_weights)

    conv_full = []
    for l in range(DEPTH):
        z = jnp.zeros((N_DEV, CONV_WIDTH, XBC_WIDTH // N_DEV), F32)
        conv_full.append(lax.dynamic_update_index_in_dim(z, conv_w[l], me, axis=0))
    cw = jnp.stack(conv_full).reshape(-1, LANES)
    cw = _all_reduce_small(cw, "gather_conv_w").reshape(DEPTH, N_DEV, CONV_WIDTH, XBC_WIDTH // N_DEV)
    cw = cw.transpose(0, 2, 1, 3).reshape(DEPTH, CONV_WIDTH, XBC_WIDTH)

    small = [{nm: (cw[l] if nm == "conv_w" else a[l]) for nm, a in sm.items()} for l in range(DEPTH)]
    grad_reduce = _GradReduce()
    loss, dx, grads, d_rel, g_final = _step_local(x[0], loss_target[0], small, Wfull, rel_bias, final_norm_w, prefetch, grad_reduce)
    g_big = {nm: jnp.stack([grad_reduce.out[(l, nm)] for l in range(DEPTH)]) for nm in BIG}

    per_layer = [{nm: grads[l][nm] for nm, _ in SMALL_LAYER} for l in range(DEPTH)]
    packet = _pack_small(per_layer, dict(rel_bias=d_rel, final_norm_w=g_final, loss=loss[0, :1]))
    per_layer, glob = _unpack_small(_all_reduce_small(packet, "all_reduce_small"))
    g_small = {nm: jnp.stack([per_layer[l][nm] for l in range(DEPTH)]) for nm, _ in SMALL_LAYER}
    cwg = g_small["conv_w"].reshape(DEPTH, CONV_WIDTH, N_DEV, XBC_WIDTH // N_DEV)
    g_small["conv_w"] = lax.dynamic_index_in_dim(cwg, me, axis=2, keepdims=False)
    for nm in sm:
        g_small[nm] = g_small[nm].reshape(sm[nm].shape)
    g_rel = glob["rel_bias"].reshape(rel_bias.shape)
    g_fin = glob["final_norm_w"]
    loss_out = glob["loss"][0]

    def adam(w, g, m, v, name):
        shp = w.shape
        two = lambda a: a.reshape(-1, shp[-1]) if a.ndim > 1 else a.reshape(1, -1)
        d, nm_, nv = _adamw(two(w), two(g), two(m), two(v), name)
        return d.reshape(shp), nm_.reshape(shp), nv.reshape(shp)

    order = ["norm1_w", "w_in", "conv_w", "conv_b", "dt_bias", "a_log", "d_skip", "ssm_norm_w", "w_attn_branch", "w_ssm_branch",
             "w_out", "norm2_w", "w_ffn_in", "w_ffn_out", "rel_bias", "final_norm_w"]
    allw = {**big, **sm, "rel_bias": rel_bias, "final_norm_w": final_norm_w}
    allm = {**big_m, **sm_m, "rel_bias": m_rel_bias, "final_norm_w": m_final_norm_w}
    allv = {**big_v, **sm_v, "rel_bias": v_rel_bias, "final_norm_w": v_final_norm_w}
    allg = {**g_big, **g_small, "rel_bias": g_rel, "final_norm_w": g_fin}
    deltas, new_m, new_v = [], [], []
    for nm in order:
        d, a, b = adam(allw[nm], allg[nm], allm[nm], allv[nm], "adamw_" + nm)
        deltas.append(d)
        new_m.append(a)
        new_v.append(b)
    return (loss_out, dx[None], *[allg[nm] for nm in order], *deltas, *new_m, *new_v)
```

```python
import functools
import math

import numpy as np
import jax
import jax.numpy as jnp
from jax import lax
from jax.experimental import pallas as pl
from jax.experimental.pallas import tpu as pltpu

F32, BF16 = jnp.float32, jnp.bfloat16
S = jax.ShapeDtypeStruct
MESH = pl.DeviceIdType.MESH

D_MODEL = 1024
DEPTH = 2
HEAD_DIM = 64
DILATED_GROUPS = ((128, 1), (512, 4), (2048, 16))
N_DIL = 3
HEADS_PER_GROUP = 8
ATTN_WIDTH = 1536
ATTN_OUT_WIDTH = 512
ATTN_BLOCK = 128
N_REL_BUCKETS = 32
REL_MAX_DISTANCE = 2048
D_INNER = 2048
N_SSM_HEADS = 32
N_SSM_GROUPS = 4
D_STATE = 128
CONV_WIDTH = 4
SSD_CHUNK = 128
XBC_WIDTH = 3072
D_FF = 2816
EPS = 1e-6
ADAM_LR, ADAM_B1, ADAM_B2, ADAM_EPS, ADAM_WD, ADAM_STEP = 0.001, 0.9, 0.999, 1e-08, 0.01, 10

N_DEV = 8
LANES = 128
VMEM_LIMIT = 56 * 1024 * 1024
ROW_TILES_BYTES = 36 * 1024 * 1024
NEG = -1e30
BIG = ("w_in", "w_attn_branch", "w_ssm_branch", "w_out", "w_ffn_in", "w_ffn_out")
COL_SHARDED = ("w_in", "w_attn_branch", "w_ffn_in")

NT = (((1,), (1,)), ((), ()))
TN = (((0,), (0,)), ((), ()))


def _cparams(sem=None):
    return pltpu.CompilerParams(dimension_semantics=sem, vmem_limit_bytes=VMEM_LIMIT)


def _pick(n, target, mult=LANES):
    best = None
    for t in range(mult, min(n, target) + 1, mult):
        if n % t == 0:
            best = t
    return best or n


def _silu(x):
    return x * jax.nn.sigmoid(x)


def _dsilu(x):
    s = jax.nn.sigmoid(x)
    return s * (1.0 + x * (1.0 - s))


def _split2(x):
    hi = x.astype(BF16)
    lo = (x - hi.astype(F32)).astype(BF16)
    return hi, lo


def _split3(x):
    x1 = x.astype(BF16)
    r1 = x - x1.astype(F32)
    x2 = r1.astype(BF16)
    x3 = (r1 - x2.astype(F32)).astype(BF16)
    return x1, x2, x3


def _dotx_r(x, m, parts=3):
    xs = _split3(x) if parts == 3 else _split2(x)
    out = jnp.dot(xs[0], m, preferred_element_type=F32)
    for xi in xs[1:]:
        out = out + jnp.dot(xi, m, preferred_element_type=F32)
    return out


def _dotx_l(m, x, parts=3):
    xs = _split3(x) if parts == 3 else _split2(x)
    out = jnp.dot(m, xs[0], preferred_element_type=F32)
    for xi in xs[1:]:
        out = out + jnp.dot(m, xi, preferred_element_type=F32)
    return out


def _mm(a, b, *, ta=False, tb=False, out_dtype=F32, acc=None, name, tm=1536, tn=1536, tk=1536, extras=(), epilogue=None, outs=None):
    M, K = (a.shape[1], a.shape[0]) if ta else a.shape
    N = b.shape[0] if tb else b.shape[1]
    tm, tn, tk = _pick(M, tm), _pick(N, tn), _pick(K, tk)
    nk = K // tk
    dims = (((0 if ta else 1,), (1 if tb else 0,)), ((), ()))
    has_acc = acc is not None
    outs = [(N, out_dtype)] if outs is None else outs
    ne, no = len(extras), len(outs)

    def body(*refs):
        a_ref, b_ref = refs[:2]
        c_ref = refs[2] if has_acc else None
        e_refs = refs[2 + has_acc:2 + has_acc + ne]
        o_refs = refs[2 + has_acc + ne:2 + has_acc + ne + no]
        acc_ref = refs[-1]
        k = pl.program_id(2)
        part = lax.dot_general(a_ref[...].astype(BF16), b_ref[...].astype(BF16), dims, preferred_element_type=F32)

        def finish(res):
            tiles = [res] if epilogue is None else epilogue(res, *[e[...] for e in e_refs])
            for o_ref, t in zip(o_refs, tiles):
                o_ref[...] = t.astype(o_ref.dtype)

        if nk == 1:
            finish(part + c_ref[...].astype(F32) if has_acc else part)
        else:
            @pl.when(k == 0)
            def _():
                acc_ref[...] = part + c_ref[...].astype(F32) if has_acc else part

            @pl.when(jnp.logical_and(k > 0, k < nk - 1))
            def _():
                acc_ref[...] += part

            @pl.when(k == nk - 1)
            def _():
                finish(acc_ref[...] + part)

    def cspec(cols):
        return pl.BlockSpec((tm, cols * tn // N), lambda i, j, k: (i, j))

    a_spec = pl.BlockSpec((tk, tm), lambda i, j, k: (k, i)) if ta else pl.BlockSpec((tm, tk), lambda i, j, k: (i, k))
    b_spec = pl.BlockSpec((tn, tk), lambda i, j, k: (j, k)) if tb else pl.BlockSpec((tk, tn), lambda i, j, k: (k, j))
    in_specs, args = [a_spec, b_spec], [a, b]
    if has_acc:
        in_specs.append(cspec(N))
        args.append(acc)
    in_specs += [cspec(e.shape[1]) for e in extras]
    args += list(extras)
    res = pl.pallas_call(
        body, grid=(M // tm, N // tn, nk), in_specs=in_specs, out_specs=[cspec(c) for c, _ in outs],
        out_shape=[S((M, c), dt) for c, dt in outs], scratch_shapes=[pltpu.VMEM((tm, tn), F32)] if nk > 1 else [],
        compiler_params=_cparams(("parallel", "parallel", "arbitrary")), name=name)(*args)
    return res[0] if len(outs) == 1 else res


def _mm_dil(a_list, b_list, d, acc, name, tm=1024):
    T, K = a_list[0].shape
    N = b_list[0].shape[0]
    tm, tn = min(tm, T), _pick(N, 1024)
    na = len(a_list)
    has_acc = acc is not None

    def body(*refs):
        a_refs, b_refs, rest = refs[:na], refs[na:2 * na], refs[2 * na:]
        c_ref = rest[0] if has_acc else None
        o_ref, scr = rest[-2], rest[-1]
        out = c_ref[...] if has_acc else None
        for a_ref, b_ref in zip(a_refs, b_refs):
            a_tok = _dil_to_tok(scr, a_ref, d).astype(BF16) if d > 1 else a_ref[...]
            part = lax.dot_general(a_tok, b_ref[...], NT, preferred_element_type=F32)
            out = part if out is None else out + part
        o_ref[...] = out

    if d > 1:
        a_spec = pl.BlockSpec((d, tm // d, K), lambda i, j: (0, i, 0))
        a_args = [a.reshape(d, T // d, K) for a in a_list]
    else:
        a_spec = pl.BlockSpec((tm, K), lambda i, j: (i, 0))
        a_args = list(a_list)
    o_spec = pl.BlockSpec((tm, tn), lambda i, j: (i, j))
    in_specs = [a_spec] * na + [pl.BlockSpec((tn, K), lambda i, j: (j, 0))] * na + ([o_spec] if has_acc else [])
    return pl.pallas_call(
        body, grid=(T // tm, N // tn), in_specs=in_specs, out_specs=o_spec, out_shape=S((T, N), F32),
        scratch_shapes=[pltpu.VMEM((K // LANES, tm, LANES), F32)],
        compiler_params=_cparams(("parallel", "parallel")), name=name)(*a_args, *b_list, *([acc] if has_acc else []))


class _Comm:
    def __init__(self, ins, outs, sems, start, wait, alias=None):
        self.ins, self.outs, self.sems, self.start, self.wait, self.alias = list(ins), list(outs), list(sems), start, wait, alias or {}


def _carrier_call(body, n, in_specs, out_specs, out_shape, scratch_shapes, args, comm, name):
    ni, no, ns = len(in_specs), len(out_specs), len(scratch_shapes)
    if comm is None:
        res = pl.pallas_call(body, grid=(n,), in_specs=in_specs, out_specs=out_specs, out_shape=out_shape,
                             scratch_shapes=scratch_shapes, compiler_params=_cparams(("arbitrary",)), name=name)(*args)
        return list(res), []
    ci, co = len(comm.ins), len(comm.outs)

    def wrapped(*refs):
        ins, cins = refs[:ni], refs[ni:ni + ci]
        outs, couts = refs[ni + ci:ni + ci + no], refs[ni + ci + no:ni + ci + no + co]
        scr, csems = refs[ni + ci + no + co:ni + ci + no + co + ns], refs[ni + ci + no + co + ns:]

        @pl.when(pl.program_id(0) == 0)
        def _():
            comm.start(cins, couts, csems)

        body(*ins, *outs, *scr)

        @pl.when(pl.program_id(0) == n - 1)
        def _():
            comm.wait(cins, couts, csems)

    anys = pl.BlockSpec(memory_space=pl.ANY)
    res = pl.pallas_call(
        wrapped, grid=(n,), in_specs=list(in_specs) + [anys] * ci, out_specs=list(out_specs) + [anys] * co,
        out_shape=list(out_shape) + comm.outs, scratch_shapes=list(scratch_shapes) + comm.sems,
        input_output_aliases={ni + a: no + b for a, b in comm.alias.items()},
        compiler_params=_cparams(("arbitrary",)), name=name)(*args, *comm.ins)
    return list(res[:no]), list(res[no:])


def _dev_index(dev):
    return 4 * dev[0] + 2 * dev[1] + dev[2]


def _comm_gather_spread(shards):
    npc = len(shards)

    def copies(x_refs, o_refs, sems):
        x, y, c, chips = _place()
        me = (x, y, c)
        peers = [(x, y, 1 - c)] + [(*chip, c) for chip in chips]
        return [[pltpu.make_async_remote_copy(src_ref=x_refs[i], dst_ref=o_refs[i].at[_dev_index(me)], send_sem=sems[0].at[k, i],
                                              recv_sem=sems[1].at[k, i], device_id=peer, device_id_type=MESH)
                 for k, peer in enumerate(peers)] for i in range(npc)], peers, me

    def local(x_refs, o_refs, sems, me):
        return [pltpu.make_async_copy(x_refs[i], o_refs[i].at[_dev_index(me)], sems[2].at[i]) for i in range(npc)]

    def start(x_refs, o_refs, sems):
        cps, _, me = copies(x_refs, o_refs, sems)
        for cp in local(x_refs, o_refs, sems, me):
            cp.start()
        for row in cps:
            for cp in row:
                cp.start()

    def wait(x_refs, o_refs, sems):
        cps, peers, me = copies(x_refs, o_refs, sems)
        for i in range(npc):
            for k, peer in enumerate(peers):
                pltpu.make_async_remote_copy(src_ref=x_refs[i], dst_ref=o_refs[i].at[_dev_index(peer)], send_sem=sems[0].at[k, i],
                                             recv_sem=sems[1].at[k, i], device_id=peer, device_id_type=MESH).wait_recv()
        for row in cps:
            for cp in row:
                cp.wait_send()
        for cp in local(x_refs, o_refs, sems, me):
            cp.wait()

    return _Comm(shards, [S((N_DEV,) + s.shape, s.dtype) for s in shards],
                 [pltpu.SemaphoreType.DMA((4, npc)), pltpu.SemaphoreType.DMA((4, npc)), pltpu.SemaphoreType.DMA((npc,))], start, wait)


def _comm_gather_pass(gathered):
    npc = len(gathered)

    def copies(o_refs, sems, sent):
        x, y, c, chips = _place()
        return [pltpu.make_async_remote_copy(
            src_ref=o_refs[i].at[_dev_index((*chip, c))], dst_ref=o_refs[i].at[_dev_index((*chip, c if sent else 1 - c))],
            send_sem=sems[0].at[j, i], recv_sem=sems[1].at[j, i], device_id=(x, y, 1 - c), device_id_type=MESH)
            for i in range(npc) for j, chip in enumerate(chips)]

    def start(g_refs, o_refs, sems):
        for cp in copies(o_refs, sems, True):
            cp.start()

    def wait(g_refs, o_refs, sems):
        for cp in copies(o_refs, sems, False):
            cp.wait_recv()
        for cp in copies(o_refs, sems, True):
            cp.wait_send()

    return _Comm(gathered, [S(g.shape, g.dtype) for g in gathered],
                 [pltpu.SemaphoreType.DMA((3, npc)), pltpu.SemaphoreType.DMA((3, npc))], start, wait,
                 alias={i: i for i in range(npc)})


def _comm_to_chips(parts):
    npc = len(parts)

    def copies(p_refs, o_refs, sems):
        x, y, c, chips = _place()
        return [pltpu.make_async_remote_copy(
            src_ref=p_refs[i].at[2 * chip[0] + chip[1]], dst_ref=o_refs[i].at[j], send_sem=sems[0].at[j, i],
            recv_sem=sems[1].at[j, i], device_id=(*chip, c), device_id_type=MESH)
            for i in range(npc) for j, chip in enumerate(chips)]

    def start(p_refs, o_refs, sems):
        for cp in copies(p_refs, o_refs, sems):
            cp.start()

    def wait(p_refs, o_refs, sems):
        for cp in copies(p_refs, o_refs, sems):
            cp.wait()

    return _Comm(parts, [S((3,) + p.shape[1:], p.dtype) for p in parts],
                 [pltpu.SemaphoreType.DMA((3, npc)), pltpu.SemaphoreType.DMA((3, npc))], start, wait)


def _dil_to_tok(scr, ref, d):
    n, C = ref.shape[1], ref.shape[2]
    for r in range(d):
        v = ref[r].astype(F32)
        for cb in range(C // LANES):
            scr.at[cb][pl.ds(r, n, stride=d), :] = v[:, cb * LANES:(cb + 1) * LANES]
    return jnp.concatenate([scr[cb] for cb in range(C // LANES)], axis=1)


def _tok_to_dil(scr, val, ref, d):
    n, C = ref.shape[1], ref.shape[2]
    for cb in range(C // LANES):
        scr[cb] = val[:, cb * LANES:(cb + 1) * LANES].astype(F32)
    for r in range(d):
        ref[r] = jnp.concatenate([scr.at[cb][pl.ds(r, n, stride=d), :] for cb in range(C // LANES)], axis=1).astype(ref.dtype)


def _rowwise(fn, rows, fulls, outs, accs=(), *, tm, name, cap=True, comm=None):
    rows = [r if isinstance(r, tuple) else (r, r.shape[1], 0) for r in rows]
    first = rows[0]
    T = (first[1] if isinstance(first[0], str) else first[0]).shape[0]
    widest = max([r[1].shape[1] if isinstance(r[0], str) else r[1] for r in rows] + [o[0] for o in outs])
    if cap:
        tm = min(tm, max(8, ROW_TILES_BYTES // (2 * (len(rows) + len(outs))) // (4 * widest) // 8 * 8))
    tm = T if T <= tm else _pick(T, tm, 8)
    nr, nf, no, na = len(rows), len(fulls), len(outs), len(accs)
    dil_in = [i for i, r in enumerate(rows) if isinstance(r[0], str) and r[2] > 1]
    dil_out = [i for i, o in enumerate(outs) if len(o) == 3 and o[2] > 1]
    scr_cols = [rows[i][1].shape[1] for i in dil_in] + [outs[i][0] for i in dil_out]

    def body(*refs):
        r, f = refs[:nr], refs[nr:nr + nf]
        o, a = refs[nr + nf:nr + nf + no], refs[nr + nf + no:nr + nf + no + na]
        scr = refs[nr + nf + no + na:]
        tiles = []
        for i, x in enumerate(r):
            if i in dil_in:
                tiles.append(_dil_to_tok(scr[dil_in.index(i)], x, rows[i][2]))
            else:
                tiles.append(x[...].astype(F32))
        ro, ra = fn(*tiles, *[x[...] for x in f])
        for i, (ref, val) in enumerate(zip(o, ro)):
            if i in dil_out:
                _tok_to_dil(scr[len(dil_in) + dil_out.index(i)], val, ref, outs[i][2])
            else:
                ref[...] = val.astype(ref.dtype)
        if na:
            @pl.when(pl.program_id(0) == 0)
            def _():
                for ref in a:
                    ref[...] = jnp.zeros_like(ref)
            for ref, val in zip(a, ra):
                ref[...] += val

    in_specs, args = [], []
    for i, rr in enumerate(rows):
        if isinstance(rr[0], str):
            arr, d = rr[1], rr[2]
            if d > 1:
                in_specs.append(pl.BlockSpec((d, tm // d, arr.shape[1]), lambda i: (0, i, 0)))
                args.append(arr.reshape(d, T // d, arr.shape[1]))
            else:
                in_specs.append(pl.BlockSpec((tm, arr.shape[1]), lambda i: (i, 0)))
                args.append(arr)
        else:
            in_specs.append(pl.BlockSpec((tm, rr[1]), functools.partial(lambda i, cb: (i, cb), cb=rr[2])))
            args.append(rr[0])
    in_specs += [pl.BlockSpec(f.shape, lambda i: (0, 0)) for f in fulls]
    out_specs, out_shape = [], []
    for i, oo in enumerate(outs):
        if i in dil_out:
            d = oo[2]
            out_specs.append(pl.BlockSpec((d, tm // d, oo[0]), lambda i: (0, i, 0)))
            out_shape.append(S((d, T // d, oo[0]), oo[1]))
        else:
            out_specs.append(pl.BlockSpec((tm, oo[0]), lambda i: (i, 0)))
            out_shape.append(S((T, oo[0]), oo[1]))
    out_specs += [pl.BlockSpec(sh, lambda i: (0, 0)) for sh in accs]
    out_shape += [S(sh, F32) for sh in accs]
    res, comm_res = _carrier_call(
        body, T // tm, in_specs, out_specs, out_shape, [pltpu.VMEM((c // LANES, tm, LANES), F32) for c in scr_cols],
        list(args) + list(fulls), comm, name)
    res = [x.reshape(T, x.shape[2]) if i in dil_out else x for i, x in enumerate(res)]
    return res if comm is None else (res, comm_res)


def _rmsnorm_fwd(h, w, name, dils=()):
    D = h.shape[1]

    def fn(h, w):
        r = lax.rsqrt(jnp.mean(h * h, axis=-1, keepdims=True) + EPS)
        xn = h * r * w
        return [xn] * (1 + len(dils)), []
    return _rowwise(fn, [h], [w.reshape(1, -1)], [(D, BF16)] + [(D, BF16, d) for d in dils], tm=512, name=name)


def _rmsnorm_bwd(dxn, h, w, dres, name):
    def fn(dxn, h, dres, w):
        r = lax.rsqrt(jnp.mean(h * h, axis=-1, keepdims=True) + EPS)
        n = h * r
        dn = dxn * w
        dh = r * (dn - n * jnp.mean(dn * n, axis=-1, keepdims=True)) + dres
        return [dh], [jnp.sum(dxn * n, axis=0, keepdims=True)]
    D = h.shape[1]
    return _rowwise(fn, [dxn, h, dres], [w.reshape(1, -1)], [(D, F32)], [(1, D)], tm=256, name=name)


def _loss_head(h, w, tgt, name):
    D = h.shape[1]

    def fn(h, tgt, w):
        r = lax.rsqrt(jnp.mean(h * h, axis=-1, keepdims=True) + EPS)
        n = h * r
        e = n * w - tgt
        row_loss = 0.5 * jnp.mean(e * e, axis=-1, keepdims=True)
        dy = e * (1.0 / D)
        dn = dy * w
        dh = r * (dn - n * jnp.mean(dn * n, axis=-1, keepdims=True))
        return [dh], [jnp.sum(dy * n, axis=0, keepdims=True), jnp.broadcast_to(jnp.sum(row_loss, axis=0, keepdims=True), (1, LANES))]
    return _rowwise(fn, [h, tgt], [w.reshape(1, -1)], [(D, F32)], [(1, D), (1, LANES)], tm=256, name=name)


def _combine_fwd(os_, ls, name):
    def fn(o0, o1, o2, l0, l1, l2):
        m = jnp.maximum(jnp.maximum(l0, l1), l2)
        e0, e1, e2 = jnp.exp(l0 - m), jnp.exp(l1 - m), jnp.exp(l2 - m)
        attn = (e0 * o0 + e1 * o1 + e2 * o2) / (e0 + e1 + e2)
        return [attn, attn], []
    dil = [("dil", t, d) for t, (_, d) in zip(list(os_) + list(ls), DILATED_GROUPS * 2)]
    return _rowwise(fn, dil, [], [(ATTN_OUT_WIDTH, BF16), (ATTN_OUT_WIDTH, F32)], tm=512, name=name)


def _combine_bwd(dattn, attn, ls, head_ones, name):
    def fn(dattn, attn, l0, l1, l2, ones):
        m = jnp.maximum(jnp.maximum(l0, l1), l2)
        e0, e1, e2 = jnp.exp(l0 - m), jnp.exp(l1 - m), jnp.exp(l2 - m)
        inv = 1.0 / (e0 + e1 + e2)
        t = _dotx_r(dattn * attn, ones, parts=2)
        outs = []
        for e in (e0, e1, e2):
            al = e * inv
            outs += [al * dattn, al * t]
        return outs, []
    W = ATTN_OUT_WIDTH
    dil = [("dil", t, d) for t, (_, d) in zip(ls, DILATED_GROUPS)]
    outs = [(W, F32, d) for _, d in DILATED_GROUPS for _ in range(2)]
    return _rowwise(fn, [dattn, attn] + dil, [head_ones], outs, tm=512, name=name)


def _dt_fwd(dt_raw, dt_bias, name):
    def fn(raw, b):
        z = raw[:, :N_SSM_HEADS] + b
        return [jnp.maximum(z, 0.0) + jnp.log(1.0 + jnp.exp(-jnp.abs(z)))], []
    return _rowwise(fn, [dt_raw], [dt_bias.reshape(1, -1)], [(N_SSM_HEADS, F32)], tm=1024, name=name)[0]


def _dt_bwd(ddt_a, ddt_b, dt_raw, dt_bias, name):
    def fn(da, db, raw, b):
        g = (da + db) * jax.nn.sigmoid(raw[:, :N_SSM_HEADS] + b)
        pad = jnp.zeros((g.shape[0], LANES - N_SSM_HEADS), F32)
        return [jnp.concatenate([g, pad], axis=1)], [jnp.sum(g, axis=0, keepdims=True)]
    return _rowwise(fn, [ddt_a, ddt_b, dt_raw], [dt_bias.reshape(1, -1)], [(LANES, BF16)], [(1, N_SSM_HEADS)], tm=1024, name=name)


def _ssm_norm_fwd(y, z, w, name, comm=None):
    G = D_INNER // N_SSM_GROUPS

    def fn(y, z, w):
        yg = y * _silu(z)
        outs = []
        for g in range(N_SSM_GROUPS):
            t = yg[:, g * G:(g + 1) * G]
            outs.append(t * lax.rsqrt(jnp.mean(t * t, axis=-1, keepdims=True) + EPS))
        return [jnp.concatenate(outs, axis=1) * w], []
    res = _rowwise(fn, [y, z], [w.reshape(1, -1)], [(D_INNER, BF16)], tm=256, name=name, comm=comm)
    return res[0] if comm is None else (res[0][0], res[1])


def _ssm_norm_bwd(dssm, y, z, w, name):
    G = D_INNER // N_SSM_GROUPS

    def fn(dssm, y, z, w):
        sz = _silu(z)
        yg = y * sz
        dn = dssm * w
        ns, dygs = [], []
        for g in range(N_SSM_GROUPS):
            t = yg[:, g * G:(g + 1) * G]
            r = lax.rsqrt(jnp.mean(t * t, axis=-1, keepdims=True) + EPS)
            n = t * r
            d = dn[:, g * G:(g + 1) * G]
            dygs.append(r * (d - n * jnp.mean(d * n, axis=-1, keepdims=True)))
            ns.append(n)
        n, dyg = jnp.concatenate(ns, axis=1), jnp.concatenate(dygs, axis=1)
        return [dyg * sz, dyg * y * _dsilu(z)], [jnp.sum(dssm * n, axis=0, keepdims=True)]
    return _rowwise(fn, [dssm, y, z], [w.reshape(1, -1)], [(D_INNER, F32), (D_INNER, BF16)], [(1, D_INNER)], tm=256, name=name)


def _gate_out_proj(a, sb, gl, h, w_out, name):
    def fn(a, sb, gl, h, w):
        g = jax.nn.sigmoid(gl)
        merged = (g[:, :D_MODEL] * a + g[:, D_MODEL:] * sb).astype(BF16)
        return [h + jnp.dot(merged, w, preferred_element_type=F32), merged], []
    return _rowwise(fn, [a, sb, gl, h], [w_out], [(D_MODEL, F32), (D_MODEL, BF16)], tm=512, cap=False, name=name)


def _d_out_proj_gate(dh, a, sb, gl, w_out, name):
    def fn(dh, a, sb, gl, w):
        dm = lax.dot_general(dh.astype(BF16), w, NT, preferred_element_type=F32)
        g = jax.nn.sigmoid(gl)
        g0, g1 = g[:, :D_MODEL], g[:, D_MODEL:]
        dgl = jnp.concatenate([dm * a * g0 * (1.0 - g0), dm * sb * g1 * (1.0 - g1)], axis=1)
        return [g0 * dm, g1 * dm, dgl], []
    return _rowwise(fn, [dh, a, sb, gl], [w_out], [(D_MODEL, BF16), (D_MODEL, BF16), (2 * D_MODEL, BF16)], tm=512, cap=False, name=name)


FFN_HALF = D_FF // 2


def _ffn_perm(w):
    h = FFN_HALF
    return jnp.concatenate([w[:, 0:h], w[:, D_FF:D_FF + h], w[:, h:D_FF], w[:, D_FF + h:]], axis=1)


def _ffn_unperm(w):
    h = FFN_HALF
    return jnp.concatenate([w[:, 0:h], w[:, 2 * h:3 * h], w[:, h:2 * h], w[:, 3 * h:]], axis=1)


def _swiglu_epilogue(res):
    return [res, _silu(res[:, :FFN_HALF]) * res[:, FFN_HALF:]]


def _dswiglu_epilogue(dact, u):
    u = u.astype(F32)
    gate, up = u[:, :FFN_HALF], u[:, FFN_HALF:]
    return [jnp.concatenate([dact * up * _dsilu(gate), dact * _silu(gate)], axis=1)]


def _adamw(w, g, m, v, name):
    c1 = 1.0 - ADAM_B1 ** ADAM_STEP
    c2 = 1.0 - ADAM_B2 ** ADAM_STEP

    def fn(w, g, m, v):
        m = ADAM_B1 * m + (1.0 - ADAM_B1) * g
        v = ADAM_B2 * v + (1.0 - ADAM_B2) * (g * g)
        delta = -ADAM_LR * ((m / c1) / (jnp.sqrt(v / c2) + ADAM_EPS) + ADAM_WD * w)
        return [delta, m, v], []
    C = w.shape[1]
    return _rowwise(fn, [w, g, m, v], [], [(C, F32)] * 3, tm=256, name=name)


def _bias_consts(dilation, n_steps):
    qi = np.arange(ATTN_BLOCK)[:, None]
    kj = np.arange(2 * ATTN_BLOCK)[None, :]
    steps = qi + ATTN_BLOCK - kj
    valid = (steps >= 0) & (steps <= n_steps)
    dist = jnp.asarray(np.clip(steps, 0, n_steps) * dilation, jnp.int32)
    max_exact = N_REL_BUCKETS // 2
    d_f = jnp.maximum(dist, 1).astype(F32)
    large = max_exact + (jnp.log(d_f / max_exact) / math.log(REL_MAX_DISTANCE / max_exact)
                         * (N_REL_BUCKETS - max_exact)).astype(jnp.int32)
    large = jnp.minimum(large, N_REL_BUCKETS - 1)
    bucket = jnp.where(dist < max_exact, dist, large).reshape(-1)
    onehot = (bucket[None, :] == jnp.arange(N_REL_BUCKETS)[:, None]).astype(F32)
    return onehot, jnp.asarray(valid.reshape(1, -1), F32)


def _bias_gather(rel_g_t, onehot, valid, name):
    def body(r_ref, oh_ref, v_ref, o_ref):
        b = jnp.dot(r_ref[...], oh_ref[...], preferred_element_type=F32, precision=lax.Precision.HIGHEST)
        o_ref[...] = jnp.where(v_ref[...] > 0.5, b, NEG)
    return pl.pallas_call(body, out_shape=S((HEADS_PER_GROUP, onehot.shape[1]), F32), compiler_params=_cparams(), name=name)(rel_g_t, onehot, valid)


def _bias_scatter(dbias, onehot, name):
    def body(d_ref, oh_ref, o_ref):
        o_ref[...] = lax.dot_general(d_ref[...], oh_ref[...], NT, preferred_element_type=F32, precision=lax.Precision.HIGHEST)
    return pl.pallas_call(body, out_shape=S((HEADS_PER_GROUP, N_REL_BUCKETS), F32), compiler_params=_cparams(), name=name)(dbias, onehot)


ATTN_QB_FWD, ATTN_QB_BWD = 4, 4


def _attn_tiles(T, d, qb):
    seg = T // d
    nqb = min(qb, seg // ATTN_BLOCK)
    tq = nqb * ATTN_BLOCK
    return seg, nqb, tq, seg // tq


def _attn_fwd(qkv, bias, d, name):
    T = qkv.shape[0]
    seg, nqb, tq, ns = _attn_tiles(T, d, ATTN_QB_FWD)
    W = ATTN_OUT_WIDTH
    scale = HEAD_DIM ** -0.5

    def body(q_ref, kh_ref, kc_ref, vh_ref, vc_ref, b_ref, o_ref, l_ref, s_scr, p_scr):
        n = pl.program_id(1)
        qv = q_ref[...]
        kk = jnp.concatenate([kh_ref[...], kc_ref[...]], axis=0)
        vv = jnp.concatenate([vh_ref[...], vc_ref[...]], axis=0)
        col = lax.broadcasted_iota(jnp.int32, (ATTN_BLOCK, 2 * ATTN_BLOCK), 1)
        kill = jnp.logical_and(n == 0, col < ATTN_BLOCK)
        lo = lax.broadcasted_iota(jnp.int32, (1, LANES), 1) < HEAD_DIM
        zero = jnp.zeros((), BF16)
        for j in range(nqb):
            rows = slice(j * ATTN_BLOCK, (j + 1) * ATTN_BLOCK)
            keys = slice(j * ATTN_BLOCK, (j + 2) * ATTN_BLOCK)
            for hp in range(HEADS_PER_GROUP // 2):
                ps = slice(hp * LANES, (hp + 1) * LANES)
                q2 = (qv[rows, ps].astype(F32) * scale).astype(BF16)
                k2 = kk[keys, ps]
                s_scr[2 * hp] = lax.dot_general(q2, jnp.where(lo, k2, zero), NT, preferred_element_type=F32)
                s_scr[2 * hp + 1] = lax.dot_general(q2, jnp.where(lo, zero, k2), NT, preferred_element_type=F32)
            s = s_scr[...] + b_ref[...]
            if j == 0:
                s = jnp.where(kill[None], NEG, s)
            m = jnp.max(s, axis=-1, keepdims=True)
            p = jnp.exp(s - m)
            den = jnp.sum(p, axis=-1, keepdims=True)
            p_scr[...] = p.astype(BF16)
            inv = 1.0 / den
            lse = m + jnp.log(den)
            for hp in range(HEADS_PER_GROUP // 2):
                ps = slice(hp * LANES, (hp + 1) * LANES)
                v2 = vv[keys, ps]
                o2 = (jnp.dot(p_scr[2 * hp], jnp.where(lo, v2, zero), preferred_element_type=F32)
                      + jnp.dot(p_scr[2 * hp + 1], jnp.where(lo, zero, v2), preferred_element_type=F32))
                o_ref[rows, ps] = o2 * jnp.where(lo, inv[2 * hp], inv[2 * hp + 1])
                l_ref[rows, ps] = jnp.where(lo, lse[2 * hp], lse[2 * hp + 1])

    def cur(c):
        return pl.BlockSpec((tq, W), lambda r, n: (r * ns + n, c))

    def halo(c):
        return pl.BlockSpec((ATTN_BLOCK, W), lambda r, n: (jnp.maximum((r * ns + n) * nqb - 1, 0), c))

    return pl.pallas_call(
        body, grid=(d, ns),
        in_specs=[cur(0), halo(1), cur(1), halo(2), cur(2), pl.BlockSpec(bias.shape, lambda r, n: (0, 0, 0))],
        out_specs=[cur(0), cur(0)], out_shape=[S((T, W), F32)] * 2,
        scratch_shapes=[pltpu.VMEM((HEADS_PER_GROUP, ATTN_BLOCK, 2 * ATTN_BLOCK), F32),
                        pltpu.VMEM((HEADS_PER_GROUP, ATTN_BLOCK, 2 * ATTN_BLOCK), BF16)],
        compiler_params=_cparams(("parallel", "arbitrary")), name=name)(qkv, qkv, qkv, qkv, qkv, bias)


def _attn_bwd(qkv, bias, lse, do, dd, d, name):
    T = qkv.shape[0]
    seg, nqb, tq, ns = _attn_tiles(T, d, ATTN_QB_BWD)
    W = ATTN_OUT_WIDTH
    B = ATTN_BLOCK
    scale = HEAD_DIM ** -0.5

    def body(q_ref, kh_ref, kc_ref, vh_ref, vc_ref, b_ref, l_ref, do_ref, dd_ref, dq_ref, dk_ref, dv_ref, db_ref, pk_ref, pv_ref,
             s_scr, dp_scr, p_scr, ds_scr):
        r, n = pl.program_id(0), pl.program_id(1)

        @pl.when(jnp.logical_and(r == 0, n == 0))
        def _():
            db_ref[...] = jnp.zeros_like(db_ref)

        @pl.when(n == 0)
        def _():
            pk_ref[...] = jnp.zeros_like(pk_ref)
            pv_ref[...] = jnp.zeros_like(pv_ref)

        @pl.when(n < ns)
        def _():
            qv = q_ref[...]
            kk = jnp.concatenate([kh_ref[...], kc_ref[...]], axis=0)
            vv = jnp.concatenate([vh_ref[...], vc_ref[...]], axis=0)
            lse_v, do_v, dd_v = l_ref[...], do_ref[...], dd_ref[...]
            col = lax.broadcasted_iota(jnp.int32, (B, 2 * B), 1)
            kill = jnp.logical_and(n == 0, col < B)
            dqs = [[None] * (HEADS_PER_GROUP // 2) for _ in range(nqb)]
            dks = [[None] * (HEADS_PER_GROUP // 2) for _ in range(nqb)]
            dvs = [[None] * (HEADS_PER_GROUP // 2) for _ in range(nqb)]
            H, HP = HEADS_PER_GROUP, HEADS_PER_GROUP // 2
            do_b = do_v.astype(BF16)
            lo = lax.broadcasted_iota(jnp.int32, (1, LANES), 1) < HEAD_DIM
            zero = jnp.zeros((), BF16)
            first = lambda t: jnp.where(lo, t, zero)
            second = lambda t: jnp.where(lo, zero, t)
            for j in range(nqb):
                rows = slice(j * B, (j + 1) * B)
                keys = slice(j * B, (j + 2) * B)
                for hp in range(HP):
                    ps = slice(hp * LANES, (hp + 1) * LANES)
                    q2 = (qv[rows, ps].astype(F32) * scale).astype(BF16)
                    k2, v2, do2 = kk[keys, ps], vv[keys, ps], do_b[rows, ps]
                    s_scr[2 * hp] = lax.dot_general(q2, first(k2), NT, preferred_element_type=F32)
                    s_scr[2 * hp + 1] = lax.dot_general(q2, second(k2), NT, preferred_element_type=F32)
                    dp_scr[2 * hp] = lax.dot_general(do2, first(v2), NT, preferred_element_type=F32)
                    dp_scr[2 * hp + 1] = lax.dot_general(do2, second(v2), NT, preferred_element_type=F32)
                lse_h = jnp.stack([lse_v[rows, h * HEAD_DIM:h * HEAD_DIM + 1] for h in range(H)], axis=0)
                dd_h = jnp.stack([dd_v[rows, h * HEAD_DIM:h * HEAD_DIM + 1] for h in range(H)], axis=0)
                s = s_scr[...] + b_ref[...]
                if j == 0:
                    s = jnp.where(kill[None], NEG, s)
                p = jnp.exp(s - lse_h)
                ds = p * (dp_scr[...] - dd_h)
                db_ref[...] += ds
                p_scr[...] = p.astype(BF16)
                ds_scr[...] = ds.astype(BF16)
                for hp in range(HP):
                    ps = slice(hp * LANES, (hp + 1) * LANES)
                    q2 = (qv[rows, ps].astype(F32) * scale).astype(BF16)
                    k2, do2 = kk[keys, ps], do_b[rows, ps]
                    pa, pb, da, db_ = p_scr[2 * hp], p_scr[2 * hp + 1], ds_scr[2 * hp], ds_scr[2 * hp + 1]
                    dvs[j][hp] = (lax.dot_general(pa, first(do2), TN, preferred_element_type=F32)
                                  + lax.dot_general(pb, second(do2), TN, preferred_element_type=F32))
                    dqs[j][hp] = (jnp.dot(da, first(k2), preferred_element_type=F32)
                                  + jnp.dot(db_, second(k2), preferred_element_type=F32)) * scale
                    dks[j][hp] = (lax.dot_general(da, first(q2), TN, preferred_element_type=F32)
                                  + lax.dot_general(db_, second(q2), TN, preferred_element_type=F32))
            dq_ref[...] = jnp.concatenate([jnp.concatenate(dqs[j], axis=1) for j in range(nqb)], axis=0).astype(dq_ref.dtype)
            for parts, out_ref, pend in ((dks, dk_ref, pk_ref), (dvs, dv_ref, pv_ref)):
                full = [jnp.concatenate(parts[j], axis=1) for j in range(nqb)]
                if tq > B:
                    out_ref[:tq - B] = pend[:tq - B].astype(out_ref.dtype)
                out_ref[tq - B:] = (pend[tq - B:] + full[0][:B]).astype(out_ref.dtype)
                for j in range(nqb - 1):
                    pend[j * B:(j + 1) * B] = full[j][B:] + full[j + 1][:B]
                pend[tq - B:] = full[nqb - 1][B:]

        @pl.when(n == ns)
        def _():
            dk_ref[...] = pk_ref[...].astype(dk_ref.dtype)
            dv_ref[...] = pv_ref[...].astype(dv_ref.dtype)

    def cur(c):
        return pl.BlockSpec((tq, W), lambda r, n: (r * ns + jnp.minimum(n, ns - 1), c))

    def halo(c):
        return pl.BlockSpec((B, W), lambda r, n: (jnp.maximum((r * ns + jnp.minimum(n, ns - 1)) * nqb - 1, 0), c))

    late = pl.BlockSpec((tq, W), lambda r, n: (r * ns + jnp.clip(n - 1, 0, ns - 1), 0))
    bspec = pl.BlockSpec(bias.shape, lambda r, n: (0, 0, 0))
    return pl.pallas_call(
        body, grid=(d, ns + 1),
        in_specs=[cur(0), halo(1), cur(1), halo(2), cur(2), bspec, cur(0), cur(0), cur(0)],
        out_specs=[cur(0), late, late, bspec],
        out_shape=[S((T, W), BF16)] * 3 + [S(bias.shape, F32)],
        scratch_shapes=[pltpu.VMEM((tq, W), F32), pltpu.VMEM((tq, W), F32)]
                       + [pltpu.VMEM((HEADS_PER_GROUP, B, 2 * B), t) for t in (F32, F32, BF16, BF16)],
        compiler_params=_cparams(("arbitrary", "arbitrary")), name=name,
    )(qkv, qkv, qkv, qkv, qkv, bias, lse, do, dd)


CONV_TM, CONV_TC = 512, 1024


def _shift_down(x, halo8, s, row8):
    xr = pltpu.roll(x, s, 0)
    first = jnp.where(row8 < s, pltpu.roll(halo8, s, 0), xr[:8])
    return jnp.concatenate([first, xr[8:]], axis=0)


def _shift_up(x, halo8, s, row8):
    n = x.shape[0]
    xr = pltpu.roll(x, n - s, 0)
    last = jnp.where(row8 >= 8 - s, pltpu.roll(halo8, 8 - s, 0), xr[n - 8:])
    return jnp.concatenate([xr[:n - 8], last], axis=0)


def _conv_fwd(x, w, b, name):
    T, C = x.shape
    tm, tc = min(CONV_TM, T), CONV_TC

    def body(x_ref, p_ref, w_ref, b_ref, u_ref, a_ref):
        ti = pl.program_id(1)
        xv = x_ref[...]
        p8 = jnp.where(ti == 0, 0.0, p_ref[...])
        wv = w_ref[...]
        row8 = lax.broadcasted_iota(jnp.int32, (8, tc), 0)
        u = xv * wv[3:4] + b_ref[...]
        for s in (1, 2, 3):
            u = u + _shift_down(xv, p8, s, row8) * wv[3 - s:4 - s]
        u_ref[...] = u
        a_ref[...] = _silu(u)

    cur = pl.BlockSpec((tm, tc), lambda cj, ti: (ti, cj))
    halo = pl.BlockSpec((8, tc), lambda cj, ti: (jnp.maximum(ti * (tm // 8) - 1, 0), cj))
    return pl.pallas_call(
        body, grid=(C // tc, T // tm),
        in_specs=[cur, halo, pl.BlockSpec((CONV_WIDTH, tc), lambda cj, ti: (0, cj)), pl.BlockSpec((1, tc), lambda cj, ti: (0, cj))],
        out_specs=[cur, cur], out_shape=[S((T, C), F32)] * 2,
        compiler_params=_cparams(("parallel", "arbitrary")), name=name)(x, x, w, b)


def _conv_bwd(dact, u, x, w, name):
    T, C = x.shape
    tm, tc = min(CONV_TM, T), CONV_TC
    nt = T // tm

    def body(d_ref, dn_ref, u_ref, un_ref, x_ref, w_ref, dx_ref, dw_ref, db_ref):
        ti = pl.program_id(1)

        @pl.when(ti == 0)
        def _():
            dw_ref[...] = jnp.zeros_like(dw_ref)
            db_ref[...] = jnp.zeros_like(db_ref)

        du = d_ref[...] * _dsilu(u_ref[...])
        dun = jnp.where(ti == nt - 1, 0.0, dn_ref[...] * _dsilu(un_ref[...]))
        xv = x_ref[...]
        wv = w_ref[...]
        row8 = lax.broadcasted_iota(jnp.int32, (8, tc), 0)
        dx = du * wv[3:4]
        dws = [None] * CONV_WIDTH
        dws[3] = jnp.sum(du * xv, axis=0, keepdims=True)
        for s in (1, 2, 3):
            up = _shift_up(du, dun, s, row8)
            dx = dx + up * wv[3 - s:4 - s]
            dws[3 - s] = jnp.sum(up * xv, axis=0, keepdims=True)
        dx_ref[...] = dx.astype(dx_ref.dtype)
        dw_ref[...] += jnp.concatenate(dws, axis=0)
        db_ref[...] += jnp.sum(du, axis=0, keepdims=True)

    cur = pl.BlockSpec((tm, tc), lambda cj, ti: (ti, cj))
    nxt = pl.BlockSpec((8, tc), lambda cj, ti: (jnp.minimum((ti + 1) * (tm // 8), T // 8 - 1), cj))
    return pl.pallas_call(
        body, grid=(C // tc, nt),
        in_specs=[cur, nxt, cur, nxt, cur, pl.BlockSpec((CONV_WIDTH, tc), lambda cj, ti: (0, cj))],
        out_specs=[cur, pl.BlockSpec((CONV_WIDTH, tc), lambda cj, ti: (0, cj)), pl.BlockSpec((1, tc), lambda cj, ti: (0, cj))],
        out_shape=[S((T, C), BF16), S((CONV_WIDTH, C), F32), S((1, C), F32)],
        compiler_params=_cparams(("parallel", "arbitrary")), name=name)(dact, dact, u, u, x, w)


def _ssd_consts():
    i = np.arange(SSD_CHUNK)
    tril = (i[None, :] <= i[:, None]).astype(np.float32)
    trils = (i[None, :] < i[:, None]).astype(np.float32)
    head = np.repeat(np.arange(N_SSM_HEADS), D_INNER // N_SSM_HEADS)
    et = (head[None, :] == np.arange(N_SSM_HEADS)[:, None]).astype(np.float32)
    c = lambda a: jnp.asarray(a, BF16)
    return dict(tril=c(tril), triu=c(tril.T), trils=c(trils), trius=c(trils.T), et=c(et), e=c(et.T))


def _ssd_common(act_ref, dt_ref, dtT_ref, al_ref, alT_ref, tril_ref, triu_ref, et_ref):
    a_row = -jnp.exp(al_ref[...])
    a_col = -jnp.exp(alT_ref[...])
    dt, dtT = dt_ref[...], dtT_ref[...]
    la = _dotx_l(tril_ref[...], dt * a_row)
    laT = _dotx_r(dtT * a_col, triu_ref[...])
    et = et_ref[...]
    la_e = _dotx_r(la, et)
    dt_e = _dotx_r(dt, et, parts=2)
    x = act_ref[:, :D_INNER]
    xdt = x * dt_e
    la_q = la_e[SSD_CHUNK - 1:SSD_CHUNK, :]
    return a_row, a_col, dt, dtT, la, laT, la_e, dt_e, x, xdt, la_q


def _decay(la, laT, h, causal):
    seg = la[:, h:h + 1] - laT[h:h + 1, :]
    return jnp.exp(jnp.where(causal, seg, NEG))


def _ssd_fwd(act, dt, dtT, alog, dskip_e, cs, name, comm=None):
    T = act.shape[0]
    nc = T // SSD_CHUNK
    Q, G, GW = SSD_CHUNK, N_SSM_GROUPS, D_INNER // N_SSM_GROUPS

    def body(act_ref, dt_ref, dtT_ref, al_ref, alT_ref, dsk_ref, tril_ref, triu_ref, et_ref, y_ref, st_ref, scr):
        @pl.when(pl.program_id(0) == 0)
        def _():
            scr[...] = jnp.zeros_like(scr)
        st_ref[0] = scr[...]
        a_row, a_col, dtv, dtTv, la, laT, la_e, dt_e, x, xdt, la_q = _ssd_common(
            act_ref, dt_ref, dtT_ref, al_ref, alT_ref, tril_ref, triu_ref, et_ref)
        ela = jnp.exp(la_e)
        xdt_b = xdt.astype(BF16)
        xdte_b = (xdt * jnp.exp(la_q - la_e)).astype(BF16)
        ela_q = jnp.exp(la_q)
        causal = lax.broadcasted_iota(jnp.int32, (Q, Q), 0) >= lax.broadcasted_iota(jnp.int32, (Q, Q), 1)
        for g in range(G):
            gs = slice(g * GW, (g + 1) * GW)
            Bg = act_ref[:, D_INNER + g * D_STATE:D_INNER + (g + 1) * D_STATE].astype(BF16)
            Cg = act_ref[:, D_INNER + G * D_STATE + g * D_STATE:D_INNER + G * D_STATE + (g + 1) * D_STATE].astype(BF16)
            cb = lax.dot_general(Cg, Bg, NT, preferred_element_type=F32)
            st = scr[g]
            y_inter = jnp.dot(Cg, st.astype(BF16), preferred_element_type=F32) * ela[:, gs]
            ys = []
            for hh in range(HEADS_PER_GROUP):
                h = g * HEADS_PER_GROUP + hh
                m = (cb * _decay(la, laT, h, causal)).astype(BF16)
                ys.append(jnp.dot(m, xdt_b[:, h * HEAD_DIM:(h + 1) * HEAD_DIM], preferred_element_type=F32))
            y_ref[:, gs] = jnp.concatenate(ys, axis=1) + y_inter + x[:, gs] * dsk_ref[:, gs]
            scr[g] = st * ela_q[:, gs] + lax.dot_general(Bg, xdte_b[:, gs], TN, preferred_element_type=F32)

    full = lambda a: pl.BlockSpec(a.shape, lambda c: (0,) * a.ndim)
    al, alT = alog.reshape(1, -1), alog.reshape(-1, 1)
    res, comm_res = _carrier_call(
        body, nc,
        [pl.BlockSpec((Q, XBC_WIDTH), lambda c: (c, 0)), pl.BlockSpec((Q, N_SSM_HEADS), lambda c: (c, 0)),
         pl.BlockSpec((N_SSM_HEADS, Q), lambda c: (0, c)), full(al), full(alT), full(dskip_e),
         full(cs["tril"]), full(cs["triu"]), full(cs["et"])],
        [pl.BlockSpec((Q, D_INNER), lambda c: (c, 0)), pl.BlockSpec((1, G, D_STATE, GW), lambda c: (c, 0, 0, 0))],
        [S((T, D_INNER), F32), S((nc, G, D_STATE, GW), F32)],
        [pltpu.VMEM((G, D_STATE, GW), F32)],
        [act, dt, dtT, al, alT, dskip_e, cs["tril"], cs["triu"], cs["et"]], comm, name)
    return res if comm is None else (res, comm_res)


def _ssd_bwd(dy, act, dt, dtT, alog, dskip_e, states, cs, name, comm=None):
    T = act.shape[0]
    nc = T // SSD_CHUNK
    Q, G, GW, H = SSD_CHUNK, N_SSM_GROUPS, D_INNER // N_SSM_GROUPS, N_SSM_HEADS

    def body(dy_ref, act_ref, dt_ref, dtT_ref, al_ref, alT_ref, dsk_ref, stp_ref, tril_ref, triu_ref, trils_ref, trius_ref,
             et_ref, e_ref, dact_ref, ddt_ref, ddtT_ref, da_ref, daT_ref, dsk_out_ref, dst, wbuf, ubuf, vbuf, sbuf, dm_scr, m_scr):
        @pl.when(pl.program_id(0) == 0)
        def _():
            dst[...] = jnp.zeros_like(dst)
            da_ref[...] = jnp.zeros_like(da_ref)
            daT_ref[...] = jnp.zeros_like(daT_ref)
            dsk_out_ref[...] = jnp.zeros_like(dsk_out_ref)
        a_row, a_col, dtv, dtTv, la, laT, la_e, dt_e, x, xdt, la_q = _ssd_common(
            act_ref, dt_ref, dtT_ref, al_ref, alT_ref, tril_ref, triu_ref, et_ref)
        dyv = dy_ref[...]
        ela = jnp.exp(la_e)
        e_end = jnp.exp(la_q - la_e)
        ela_q = jnp.exp(la_q)
        dye_b = (dyv * ela).astype(BF16)
        dy_b = dyv.astype(BF16)
        xdt_b = xdt.astype(BF16)
        xdte_b = (xdt * e_end).astype(BF16)
        ri = lax.broadcasted_iota(jnp.int32, (Q, Q), 0)
        ci = lax.broadcasted_iota(jnp.int32, (Q, Q), 1)
        causal = ri >= ci
        rows = []
        for g in range(G):
            gs = slice(g * GW, (g + 1) * GW)
            Bg = act_ref[:, D_INNER + g * D_STATE:D_INNER + (g + 1) * D_STATE].astype(BF16)
            Cg = act_ref[:, D_INNER + G * D_STATE + g * D_STATE:D_INNER + G * D_STATE + (g + 1) * D_STATE].astype(BF16)
            cb = lax.dot_general(Cg, Bg, NT, preferred_element_type=F32)
            stp = stp_ref[0, g]
            stp_b = stp.astype(BF16)
            dstv = dst[g]
            dst_b = dstv.astype(BF16)
            y_inter = jnp.dot(Cg, stp_b, preferred_element_type=F32) * ela[:, gs]
            wbuf[:, gs] = dyv[:, gs] * y_inter
            dxdt_state = jnp.dot(Bg, dst_b, preferred_element_type=F32) * e_end[:, gs]
            ubuf[:, gs] = dxdt_state * xdt[:, gs]
            dC = lax.dot_general(dye_b[:, gs], stp_b, NT, preferred_element_type=F32)
            dB = lax.dot_general(xdte_b[:, gs], dst_b, NT, preferred_element_type=F32)
            sbuf[:, gs] = jnp.broadcast_to(jnp.sum(dstv * stp, axis=0, keepdims=True), (8, GW))
            dst[g] = dstv * ela_q[:, gs] + lax.dot_general(Cg, dye_b[:, gs], TN, preferred_element_type=F32)
            for hh in range(HEADS_PER_GROUP):
                hs = slice((g * HEADS_PER_GROUP + hh) * HEAD_DIM, (g * HEADS_PER_GROUP + hh + 1) * HEAD_DIM)
                dm_scr[hh] = lax.dot_general(dy_b[:, hs], xdt_b[:, hs], NT, preferred_element_type=F32)
            dG = jnp.zeros((Q, Q), F32)
            for hh in range(HEADS_PER_GROUP):
                L = _decay(la, laT, g * HEADS_PER_GROUP + hh, causal)
                M = cb * L
                dM = dm_scr[hh]
                dG = dG + dM * L
                W = dM * M
                rows.append(jnp.sum(W.T, axis=0, keepdims=True) - jnp.sum(W, axis=0, keepdims=True))
                m_scr[hh] = M.astype(BF16)
            dxs = []
            for hh in range(HEADS_PER_GROUP):
                hs = slice((g * HEADS_PER_GROUP + hh) * HEAD_DIM, (g * HEADS_PER_GROUP + hh + 1) * HEAD_DIM)
                dxs.append(lax.dot_general(m_scr[hh], dy_b[:, hs], TN, preferred_element_type=F32))
            dG_b = dG.astype(BF16)
            dC = dC + jnp.dot(dG_b, Bg, preferred_element_type=F32)
            dB = dB + lax.dot_general(dG_b, Cg, TN, preferred_element_type=F32)
            dxdt = jnp.concatenate(dxs, axis=1) + dxdt_state
            vbuf[:, gs] = dxdt * x[:, gs]
            dact_ref[:, gs] = dxdt * dt_e[:, gs] + dyv[:, gs] * dsk_ref[:, gs]
            dact_ref[:, D_INNER + g * D_STATE:D_INNER + (g + 1) * D_STATE] = dB
            dact_ref[:, D_INNER + G * D_STATE + g * D_STATE:D_INNER + G * D_STATE + (g + 1) * D_STATE] = dC
        e = e_ref[...]
        w = _dotx_r(wbuf[...], e, parts=2)
        u = _dotx_r(ubuf[...], e, parts=2)
        vx = _dotx_r(vbuf[...], e, parts=2)
        dsk = _dotx_r(jnp.broadcast_to(jnp.sum(dyv * x, axis=0, keepdims=True), (8, D_INNER)), e, parts=2)[0:1]
        s0 =_dotx_r(sbuf[...], e, parts=2)[0:1] * jnp.exp(la[Q - 1:Q, :])
        ddelta = _dotx_l(triu_ref[...], w) + _dotx_l(trils_ref[...], u) + s0
        ddt_ref[...] = ddelta * a_row + vx
        ddeltaT = _dotx_r(jnp.concatenate(rows, axis=0), tril_ref[...])
        ddtT_ref[...] = ddeltaT * a_col
        da_ref[...] += jnp.sum(ddelta * dtv, axis=0, keepdims=True)
        daT_ref[...] += jnp.sum(ddeltaT * dtTv, axis=1, keepdims=True)
        dsk_out_ref[...] += dsk

    rev = lambda c: nc - 1 - c
    full = lambda a: pl.BlockSpec(a.shape, lambda c: (0,) * a.ndim)
    al, alT = alog.reshape(1, -1), alog.reshape(-1, 1)
    consts = [cs[k] for k in ("tril", "triu", "trils", "trius", "et", "e")]
    res, comm_res = _carrier_call(
        body, nc,
        [pl.BlockSpec((Q, D_INNER), lambda c: (rev(c), 0)), pl.BlockSpec((Q, XBC_WIDTH), lambda c: (rev(c), 0)),
         pl.BlockSpec((Q, H), lambda c: (rev(c), 0)), pl.BlockSpec((H, Q), lambda c: (0, rev(c))),
         full(al), full(alT), full(dskip_e), pl.BlockSpec((1, G, D_STATE, GW), lambda c: (rev(c), 0, 0, 0))]
        + [full(a) for a in consts],
        [pl.BlockSpec((Q, XBC_WIDTH), lambda c: (rev(c), 0)), pl.BlockSpec((Q, H), lambda c: (rev(c), 0)),
         pl.BlockSpec((H, Q), lambda c: (0, rev(c))), pl.BlockSpec((1, H), lambda c: (0, 0)),
         pl.BlockSpec((H, 1), lambda c: (0, 0)), pl.BlockSpec((1, H), lambda c: (0, 0))],
        [S((T, XBC_WIDTH), F32), S((T, H), F32), S((H, T), F32), S((1, H), F32), S((H, 1), F32), S((1, H), F32)],
        [pltpu.VMEM((G, D_STATE, GW), F32), pltpu.VMEM((Q, D_INNER), F32), pltpu.VMEM((Q, D_INNER), F32),
         pltpu.VMEM((Q, D_INNER), F32), pltpu.VMEM((8, D_INNER), F32),
         pltpu.VMEM((HEADS_PER_GROUP, Q, Q), F32), pltpu.VMEM((HEADS_PER_GROUP, Q, Q), BF16)],
        [dy, act, dt, dtT, al, alT, dskip_e, states] + consts, comm, name)
    return res if comm is None else (res, comm_res)


def _a_log_grad(da, daT_row, alog, name):
    def body(a_ref, b_ref, al_ref, o_ref):
        o_ref[...] = (a_ref[...] + b_ref[...]) * (-jnp.exp(al_ref[...]))
    return pl.pallas_call(body, out_shape=S((1, N_SSM_HEADS), F32), name=name)(da, daT_row, alog.reshape(1, -1))


def _place():
    x, y, c = lax.axis_index("x"), lax.axis_index("y"), lax.axis_index("c")
    return x, y, c, [(1 - x, y), (x, 1 - y), (1 - x, 1 - y)]


def _all_gather(shards, name):
    npc = len(shards)

    def body(*refs):
        x_refs, o_refs = refs[:npc], refs[npc:2 * npc]
        send_sems, recv_sems, local_sems = refs[2 * npc:]
        x, y, c, chips = _place()
        me, sib = (x, y, c), (x, y, 1 - c)

        def rows(dev, i):
            return o_refs[i].at[4 * dev[0] + 2 * dev[1] + dev[2]]

        def copy(k, i, block, to, src=None):
            return pltpu.make_async_remote_copy(
                src_ref=rows(block, i) if src is None else src, dst_ref=rows(block, i),
                send_sem=send_sems.at[k, i], recv_sem=recv_sems.at[k, i], device_id=to, device_id_type=MESH)

        mine = [pltpu.make_async_copy(x_refs[i], rows(me, i), local_sems.at[i]) for i in range(npc)]
        for cp in mine:
            cp.start()
        first = []
        for i in range(npc):
            first.append(copy(0, i, me, sib, src=x_refs[i]))
            first += [copy(1 + j, i, me, (*chip, c), src=x_refs[i]) for j, chip in enumerate(chips)]
        for cp in first:
            cp.start()
        passed = []
        for i in range(npc):
            for j, chip in enumerate(chips):
                copy(1 + j, i, (*chip, c), me).wait_recv()
                cp = copy(4 + j, i, (*chip, c), sib)
                cp.start()
                passed.append(cp)
        for i in range(npc):
            copy(0, i, sib, me).wait_recv()
            for j, chip in enumerate(chips):
                copy(4 + j, i, (*chip, 1 - c), me).wait_recv()
        for cp in first + passed:
            cp.wait_send()
        for cp in mine:
            cp.wait()

    anys = pl.BlockSpec(memory_space=pl.ANY)
    return pl.pallas_call(
        body, in_specs=[anys] * npc, out_specs=[anys] * npc, out_shape=[S((N_DEV,) + s.shape, s.dtype) for s in shards],
        scratch_shapes=[pltpu.SemaphoreType.DMA((7, npc)), pltpu.SemaphoreType.DMA((7, npc)), pltpu.SemaphoreType.DMA((npc,))],
        name=name)(*shards)


def _to_sibling(to_sib, name):
    npc = len(to_sib)

    def body(*refs):
        s_refs, o_refs, send_sems, recv_sems = refs[:npc], refs[npc:2 * npc], refs[2 * npc], refs[2 * npc + 1]
        x, y, c, _ = _place()
        cps = [pltpu.make_async_remote_copy(
            src_ref=s_refs[i], dst_ref=o_refs[i], send_sem=send_sems.at[i], recv_sem=recv_sems.at[i],
            device_id=(x, y, 1 - c), device_id_type=MESH) for i in range(npc)]
        for cp in cps:
            cp.start()
        for cp in cps:
            cp.wait()

    anys = pl.BlockSpec(memory_space=pl.ANY)
    return pl.pallas_call(
        body, in_specs=[anys] * npc, out_specs=[anys] * npc, out_shape=[S(s.shape, s.dtype) for s in to_sib],
        scratch_shapes=[pltpu.SemaphoreType.DMA((npc,)), pltpu.SemaphoreType.DMA((npc,))],
        name=name)(*to_sib)


def _to_chips(parts, name):
    npc = len(parts)

    def body(*refs):
        p_refs, o_refs, send_sems, recv_sems = refs[:npc], refs[npc:2 * npc], refs[2 * npc], refs[2 * npc + 1]
        x, y, c, chips = _place()
        cps = [pltpu.make_async_remote_copy(
            src_ref=p_refs[i].at[2 * chip[0] + chip[1]], dst_ref=o_refs[i].at[j], send_sem=send_sems.at[j, i],
            recv_sem=recv_sems.at[j, i], device_id=(*chip, c), device_id_type=MESH)
            for i in range(npc) for j, chip in enumerate(chips)]
        for cp in cps:
            cp.start()
        for cp in cps:
            cp.wait()

    anys = pl.BlockSpec(memory_space=pl.ANY)
    return pl.pallas_call(
        body, in_specs=[anys] * npc, out_specs=[anys] * npc, out_shape=[S((3,) + p.shape[1:], p.dtype) for p in parts],
        scratch_shapes=[pltpu.SemaphoreType.DMA((3, npc)), pltpu.SemaphoreType.DMA((3, npc))],
        name=name)(*parts)


def _rs_begin(pieces, name):
    c = lax.axis_index("c")
    by_core = [p.reshape(4, 2, p.shape[1], p.shape[2]) for p in pieces]
    to_sib = [lax.dynamic_index_in_dim(p, 1 - c, axis=1, keepdims=False).astype(BF16) for p in by_core]
    keep = [lax.dynamic_index_in_dim(p, c, axis=1, keepdims=False) for p in by_core]
    from_sib = _to_sibling(to_sib, name + "_d2d")

    def add1(a, b):
        s = a + b
        return [s, s], []

    parts, parts_b = [], []
    for i, (k, f) in enumerate(zip(keep, from_sib)):
        _, r, C = k.shape
        p, pb = _rowwise(add1, [k.reshape(4 * r, C), f.reshape(4 * r, C)], [], [(C, F32), (C, BF16)], tm=2048, name=f"{name}_add1_{i}")
        parts.append(p.reshape(4, r, C))
        parts_b.append(pb.reshape(4, r, C))
    return parts, parts_b


def _rs_finish(parts, got, name):
    x, y = lax.axis_index("x"), lax.axis_index("y")

    def add2(a, b, c_, d_):
        return [((a + b) + c_) + d_], []

    outs = []
    for i, (p, g) in enumerate(zip(parts, got)):
        own = lax.dynamic_index_in_dim(p, 2 * x + y, axis=0, keepdims=False)
        outs.append(_rowwise(add2, [own, g[0], g[1], g[2]], [], [(p.shape[2], F32)], tm=2048, name=f"{name}_add2_{i}")[0])
    return outs


def _reduce_scatter(pieces, name):
    parts, parts_b = _rs_begin(pieces, name)
    return _rs_finish(parts, _to_chips(parts_b, name + "_ici"), name)


class _GradReduce:
    EARLY = ("w_ffn_in", "w_ffn_out", "w_out", "w_attn_branch", "w_ssm_branch")

    def __init__(self):
        self.out, self.keys, self.parts, self.parts_b = {}, [], [], []

    def _begin(self, l, names, gr, name):
        parts, parts_b = _rs_begin([_shard(nm, gr[nm]) for nm in names], name)
        self.keys += [(l, nm) for nm in names]
        self.parts += parts
        self.parts_b += parts_b

    def carry_fn(self, l):
        if l != 0:
            return None

        def fn(gr):
            self._begin(0, self.EARLY, gr, "rs_early_l0")
            return _comm_to_chips(self.parts_b)
        return fn

    def done(self, l, gr, carried):
        if l == DEPTH - 1:
            self._begin(l, BIG, gr, f"rs_l{l}")
            return
        for key, o in zip(self.keys, _rs_finish(self.parts, carried, "rs_carried")):
            self.out[key] = o
        rest = [nm for nm in BIG if nm not in self.EARLY]
        for nm, o in zip(rest, _reduce_scatter([_shard(nm, gr[nm]) for nm in rest], "rs_rest_l0")):
            self.out[(0, nm)] = o


def _all_reduce_small(v, name):
    R, C = v.shape

    def body(x_ref, out_ref, buf, send_sems, recv_sems):
        x, y, c, chips = _place()
        me, sib = (x, y, c), (x, y, 1 - c)

        def rows(dev):
            return buf.at[4 * dev[0] + 2 * dev[1] + dev[2]]

        def copy(k, block, to, src=None):
            return pltpu.make_async_remote_copy(
                src_ref=rows(block) if src is None else src, dst_ref=rows(block),
                send_sem=send_sems.at[k], recv_sem=recv_sems.at[k], device_id=to, device_id_type=MESH)

        buf[4 * x + 2 * y + c] = x_ref[...]
        first = [copy(0, me, sib, src=x_ref)] + [copy(1 + j, me, (*chip, c), src=x_ref) for j, chip in enumerate(chips)]
        for cp in first:
            cp.start()
        passed = [copy(4 + j, (*chip, c), sib) for j, chip in enumerate(chips)]
        for j, chip in enumerate(chips):
            copy(1 + j, (*chip, c), me).wait_recv()
            passed[j].start()
        copy(0, sib, me).wait_recv()
        for j, chip in enumerate(chips):
            copy(4 + j, (*chip, 1 - c), me).wait_recv()
        for cp in first + passed:
            cp.wait_send()
        acc = buf[0]
        for j in range(1, N_DEV):
            acc = acc + buf[j]
        out_ref[...] = acc

    vm = pl.BlockSpec(memory_space=pltpu.VMEM)
    return pl.pallas_call(
        body, in_specs=[vm], out_specs=vm, out_shape=S((R, C), F32),
        scratch_shapes=[pltpu.VMEM((N_DEV, R, C), F32), pltpu.SemaphoreType.DMA((7,)), pltpu.SemaphoreType.DMA((7,))],
        compiler_params=pltpu.CompilerParams(vmem_limit_bytes=VMEM_LIMIT), name=name)(v)


SEG = (("q", 0, 1536), ("k", 1536, 1536), ("v", 3072, 1536), ("z", 4608, 2048), ("xbc", 6656, 3072), ("dt", 9728, 32), ("gl", 9760, 2048))


def _split_w_in(w_in_full):
    out = {}
    for nm, off, n in SEG:
        w = w_in_full[:, off:off + n]
        if nm == "dt":
            w = jnp.pad(w, ((0, 0), (0, LANES - n)))
        out[nm] = w
    W = ATTN_OUT_WIDTH
    out["qkv"] = [jnp.concatenate([out[s][:, g * W:(g + 1) * W] for s in ("q", "k", "v")], axis=1) for g in range(N_DIL)]
    out["qkv_t"] = [[out[s][:, g * W:(g + 1) * W] for s in ("q", "k", "v")] for g in range(N_DIL)]
    return out


def _layer_fwd(h, p, W, biases, cs, l, carry=None):
    T = h.shape[0]
    nm = lambda s: f"{s}_l{l}"
    sv = {"h_in": h}
    xns = _rmsnorm_fwd(h, p["norm1_w"], nm("norm1"), dils=[d for _, d in DILATED_GROUPS[1:]])
    xn = xns[0]
    wi = W["w_in"]
    z = _mm(xn, wi["z"], out_dtype=BF16, name=nm("proj_z"))
    xbc = _mm(xn, wi["xbc"], name=nm("proj_xbc"))
    dt_raw = _mm(xn, wi["dt"], name=nm("proj_dt"))
    gl = _mm(xn, wi["gl"], out_dtype=BF16, name=nm("proj_gl"))
    os_, ls, qkvs = [], [], []
    for g, (window, dil) in enumerate(DILATED_GROUPS):
        qkv = _mm(xns[g], wi["qkv"][g], out_dtype=BF16, name=nm(f"proj_qkv_g{g}"))
        o, lse = _attn_fwd(qkv, biases[g], dil, nm(f"attn_fwd_g{g}"))
        os_.append(o)
        ls.append(lse)
        qkvs.append(qkv)
    attn_b, attn_f = _combine_fwd(os_, ls, nm("combine"))
    u_conv, act = _conv_fwd(xbc, p["conv_w"], p["conv_b"].reshape(1, -1), nm("conv"))
    dt = _dt_fwd(dt_raw, p["dt_bias"], nm("dt"))
    dtT = dt.T
    dskip_e = jnp.repeat(p["d_skip"], D_INNER // N_SSM_HEADS).reshape(1, -1)
    carried = None
    if carry is None:
        y, states = _ssd_fwd(act, dt, dtT, p["a_log"], dskip_e, cs, nm("ssd_fwd"))
        ssm = _ssm_norm_fwd(y, z, p["ssm_norm_w"], nm("ssm_norm"))
    else:
        (y, states), spread = _ssd_fwd(act, dt, dtT, p["a_log"], dskip_e, cs, nm("ssd_fwd"), comm=carry)
        ssm, carried = _ssm_norm_fwd(y, z, p["ssm_norm_w"], nm("ssm_norm"), comm=_comm_gather_pass(spread))
    a_br = _mm(attn_b, W["w_attn_branch"], out_dtype=BF16, name=nm("attn_branch"))
    s_br = _mm(ssm, W["w_ssm_branch"], out_dtype=BF16, name=nm("ssm_branch"))
    h_mid, merged = _gate_out_proj(a_br, s_br, gl, h, W["w_out"], nm("gate_out_proj"))
    xn2 = _rmsnorm_fwd(h_mid, p["norm2_w"], nm("norm2"))[0]
    u_ffn, ffn_act = _mm(xn2, W["w_ffn_in_p"], tm=512, tn=D_FF, epilogue=_swiglu_epilogue, outs=[(2 * D_FF, BF16), (D_FF, BF16)],
                         name=nm("ffn_in_swiglu"))
    h_out = _mm(ffn_act, W["w_ffn_out"], acc=h_mid, name=nm("ffn_out"))
    sv.update(xn=xn, xns=xns, qkvs=qkvs, z=z, xbc=xbc, dt_raw=dt_raw, gl=gl, ls=ls, attn_b=attn_b, attn_f=attn_f, u_conv=u_conv,
              act=act, dt=dt, dtT=dtT, dskip_e=dskip_e, y=y, states=states, ssm=ssm, a_br=a_br, s_br=s_br, merged=merged,
              h_mid=h_mid, xn2=xn2, u_ffn=u_ffn, ffn_act=ffn_act)
    return h_out, sv, carried


def _layer_bwd(dh, sv, p, W, biases, cs, head_ones, l, carry_fn=None):
    T = dh.shape[0]
    nm = lambda s: f"{s}_l{l}"
    gr = {}
    du = _mm(dh, W["w_ffn_out"], tb=True, tm=512, tn=FFN_HALF, extras=[sv["u_ffn"]], epilogue=_dswiglu_epilogue, outs=[(2 * D_FF, BF16)],
             name=nm("d_ffn_act_swiglu"))
    gr["w_ffn_out"] = _mm(sv["ffn_act"], dh, ta=True, name=nm("g_ffn_out"))
    dxn2 = _mm(du, W["w_ffn_in_p"], tb=True, name=nm("d_xn2"))
    gr["w_ffn_in"] = _ffn_unperm(_mm(sv["xn2"], du, ta=True, name=nm("g_ffn_in")))
    dh_mid, gr["norm2_w"] = _rmsnorm_bwd(dxn2, sv["h_mid"], p["norm2_w"], dh, nm("d_norm2"))
    gr["w_out"] = _mm(sv["merged"], dh_mid, ta=True, name=nm("g_out"))
    d_a, d_s, dgl = _d_out_proj_gate(dh_mid, sv["a_br"], sv["s_br"], sv["gl"], W["w_out"], nm("d_out_proj_gate"))
    dattn = _mm(d_a, W["w_attn_branch"], tb=True, out_dtype=BF16, name=nm("d_attn"))
    gr["w_attn_branch"] = _mm(sv["attn_b"], d_a, ta=True, name=nm("g_attn_branch"))
    dssm = _mm(d_s, W["w_ssm_branch"], tb=True, out_dtype=BF16, name=nm("d_ssm"))
    gr["w_ssm_branch"] = _mm(sv["ssm"], d_s, ta=True, name=nm("g_ssm_branch"))
    dy, dz, gr["ssm_norm_w"] = _ssm_norm_bwd(dssm, sv["y"], sv["z"], p["ssm_norm_w"], nm("d_ssm_norm"))
    ssd_args = (dy, sv["act"], sv["dt"], sv["dtT"], p["a_log"], sv["dskip_e"], sv["states"], cs, nm("ssd_bwd"))
    carried = None
    if carry_fn is None:
        dact_c, ddt_a, ddt_bT, da, daT, dskip = _ssd_bwd(*ssd_args)
    else:
        (dact_c, ddt_a, ddt_bT, da, daT, dskip), carried = _ssd_bwd(*ssd_args, comm=carry_fn(gr))
    gr["a_log"] = _a_log_grad(da, daT.T, p["a_log"], nm("g_a_log")).reshape(-1)
    gr["d_skip"] = dskip.reshape(-1)
    ddt_raw, ddt_bias = _dt_bwd(ddt_a, ddt_bT.T, sv["dt_raw"], p["dt_bias"], nm("d_dt"))
    gr["dt_bias"] = ddt_bias.reshape(-1)
    dxbc, gr["conv_w"], dconv_b = _conv_bwd(dact_c, sv["u_conv"], sv["xbc"], p["conv_w"], nm("d_conv"))
    gr["conv_b"] = dconv_b.reshape(-1)
    outs = _combine_bwd(dattn, sv["attn_f"], sv["ls"], head_ones, nm("d_combine"))
    wi = W["w_in"]
    dbias, dxn = [], None
    gqkv = [[None] * N_DIL for _ in range(3)]
    for g, (window, dil) in enumerate(DILATED_GROUPS):
        dq, dk, dv, db = _attn_bwd(sv["qkvs"][g], biases[g], sv["ls"][g], outs[2 * g], outs[2 * g + 1], dil, nm(f"attn_bwd_g{g}"))
        dbias.append(db)
        dxn = _mm_dil([dq, dk, dv], wi["qkv_t"][g], dil, dxn, nm(f"d_xn_qkv_g{g}"))
        for i, dseg in enumerate((dq, dk, dv)):
            gqkv[i][g] = _mm(sv["xns"][g], dseg, ta=True, name=nm(f"g_in_{'qkv'[i]}_g{g}"))
    parts = (("z", dz), ("xbc", dxbc), ("dt", ddt_raw), ("gl", dgl))
    gws = gqkv[0] + gqkv[1] + gqkv[2]
    for sname, dseg in parts:
        dxn = _mm(dseg, wi[sname], tb=True, acc=dxn, name=nm("d_xn_" + sname))
        gw = _mm(sv["xn"], dseg, ta=True, name=nm("g_in_" + sname))
        gws.append(gw[:, :N_SSM_HEADS] if sname == "dt" else gw)
    gr["w_in"] = jnp.concatenate(gws, axis=1)
    dh_in, gr["norm1_w"] = _rmsnorm_bwd(dxn, sv["h_in"], p["norm1_w"], dh_mid, nm("d_norm1"))
    return dh_in, gr, dbias, carried


def _step_local(x, tgt, small, Wfull, rel_bias, final_norm_w, prefetch=None, grad_reduce=None):
    cs = _ssd_consts()
    head = np.repeat(np.arange(HEADS_PER_GROUP), HEAD_DIM)
    head_ones = jnp.asarray(head[:, None] == head[None, :], BF16)
    biases, onehots = [], []
    for g, (window, dil) in enumerate(DILATED_GROUPS):
        onehot, valid = _bias_consts(dil, window // dil)
        rel_g_t = rel_bias[:, g * HEADS_PER_GROUP:(g + 1) * HEADS_PER_GROUP].T
        b = _bias_gather(rel_g_t, onehot, valid, f"bias_gather_g{g}")
        biases.append(b.reshape(HEADS_PER_GROUP, ATTN_BLOCK, 2 * ATTN_BLOCK))
        onehots.append(onehot)
    h, saved, carried = x, [], None
    Wfull = list(Wfull)
    for l in range(DEPTH):
        W = dict(prefetch[1](carried) if Wfull[l] is None else Wfull[l])
        W["w_in"] = _split_w_in(W["w_in"])
        W["w_ffn_in_p"] = _ffn_perm(W["w_ffn_in"])
        Wfull[l] = W
        h, sv, carried = _layer_fwd(h, small[l], W, biases, cs, l, prefetch[0] if prefetch is not None and l == 0 else None)
        saved.append(sv)
    dh, g_final, loss = _loss_head(h, final_norm_w, tgt, "loss_head")
    grads = [None] * DEPTH
    dbias_tot = [None] * N_DIL
    for l in reversed(range(DEPTH)):
        carry_fn = grad_reduce.carry_fn(l) if grad_reduce is not None else None
        dh, grads[l], dbias, carried = _layer_bwd(dh, saved[l], small[l], Wfull[l], biases, cs, head_ones, l, carry_fn)
        if grad_reduce is not None:
            grad_reduce.done(l, grads[l], carried)
        for g in range(N_DIL):
            dbias_tot[g] = dbias[g] if dbias_tot[g] is None else dbias_tot[g] + dbias[g]
    d_rel = jnp.concatenate(
        [_bias_scatter(dbias_tot[g].reshape(HEADS_PER_GROUP, -1), onehots[g], f"bias_scatter_g{g}").T for g in range(N_DIL)], axis=1)
    return loss, dh, grads, d_rel, g_final


def _unshard(nm, g):
    _, rows, cols = g.shape
    if nm in COL_SHARDED:
        return g.transpose(1, 0, 2).reshape(rows, N_DEV * cols)
    return g.reshape(N_DEV * rows, cols)


def _shard(nm, w):
    rows, cols = w.shape
    if nm in COL_SHARDED:
        return w.reshape(rows, N_DEV, cols // N_DEV).transpose(1, 0, 2)
    return w.reshape(N_DEV, rows // N_DEV, cols)


SMALL_LAYER = (("norm1_w", 1024), ("conv_w", 12288), ("conv_b", 3072), ("dt_bias", 32), ("a_log", 32), ("d_skip", 32),
               ("ssm_norm_w", 2048), ("norm2_w", 1024))
SMALL_GLOBAL = (("rel_bias", 768), ("final_norm_w", 1024), ("loss", 1))


def _pad128(v):
    n = v.shape[0]
    return jnp.pad(v, (0, -n % LANES))


def _pack_small(per_layer, glob):
    parts = [_pad128(per_layer[l][nm].reshape(-1)) for l in range(DEPTH) for nm, _ in SMALL_LAYER]
    parts += [_pad128(glob[nm].reshape(-1)) for nm, _ in SMALL_GLOBAL]
    flat = jnp.concatenate(parts)
    flat = jnp.pad(flat, (0, -flat.shape[0] % (8 * LANES)))
    return flat.reshape(-1, LANES)


def _unpack_small(packed):
    flat = packed.reshape(-1)
    per_layer, glob, off = [dict() for _ in range(DEPTH)], {}, 0
    for l in range(DEPTH):
        for nm, n in SMALL_LAYER:
            per_layer[l][nm] = flat[off:off + n]
            off += n + (-n % LANES)
    for nm, n in SMALL_GLOBAL:
        glob[nm] = flat[off:off + n]
        off += n + (-n % LANES)
    return per_layer, glob


def kernel(x, norm1_w, w_in, conv_w, conv_b, dt_bias, a_log, d_skip, ssm_norm_w, w_attn_branch, w_ssm_branch, w_out, norm2_w, w_ffn_in, w_ffn_out, rel_bias, final_norm_w, loss_target, m_norm1_w, m_w_in, m_conv_w, m_conv_b, m_dt_bias, m_a_log, m_d_skip, m_ssm_norm_w, m_w_attn_branch, m_w_ssm_branch, m_w_out, m_norm2_w, m_w_ffn_in, m_w_ffn_out, m_rel_bias, m_final_norm_w, v_norm1_w, v_w_in, v_conv_w, v_conv_b, v_dt_bias, v_a_log, v_d_skip, v_ssm_norm_w, v_w_attn_branch, v_w_ssm_branch, v_w_out, v_norm2_w, v_w_ffn_in, v_w_ffn_out, v_rel_bias, v_final_norm_w):
    big = dict(w_in=w_in, w_attn_branch=w_attn_branch, w_ssm_branch=w_ssm_branch, w_out=w_out, w_ffn_in=w_ffn_in, w_ffn_out=w_ffn_out)
    big_m = dict(w_in=m_w_in, w_attn_branch=m_w_attn_branch, w_ssm_branch=m_w_ssm_branch, w_out=m_w_out, w_ffn_in=m_w_ffn_in, w_ffn_out=m_w_ffn_out)
    big_v = dict(w_in=v_w_in, w_attn_branch=v_w_attn_branch, w_ssm_branch=v_w_ssm_branch, w_out=v_w_out, w_ffn_in=v_w_ffn_in, w_ffn_out=v_w_ffn_out)
    sm = dict(norm1_w=norm1_w, conv_w=conv_w, conv_b=conv_b, dt_bias=dt_bias, a_log=a_log, d_skip=d_skip, ssm_norm_w=ssm_norm_w, norm2_w=norm2_w)
    sm_m = dict(norm1_w=m_norm1_w, conv_w=m_conv_w, conv_b=m_conv_b, dt_bias=m_dt_bias, a_log=m_a_log, d_skip=m_d_skip, ssm_norm_w=m_ssm_norm_w, norm2_w=m_norm2_w)
    sm_v = dict(norm1_w=v_norm1_w, conv_w=v_conv_w, conv_b=v_conv_b, dt_bias=v_dt_bias, a_log=v_a_log, d_skip=v_d_skip, ssm_norm_w=v_ssm_norm_w, norm2_w=v_norm2_w)
    me = 4 * lax.axis_index("x") + 2 * lax.axis_index("y") + lax.axis_index("c")

    def full_weights(gathered):
        return {nm: _unshard(nm, g) for nm, g in zip(BIG, gathered)}

    Wfull = [full_weights(_all_gather([big[nm][0].astype(BF16) for nm in BIG], "all_gather_l0")), None]
    prefetch = (_comm_gather_spread([big[nm][DEPTH - 1].astype(BF16) for nm in BIG]), full_weights)

    conv_full = []
    for l in range(DEPTH):
        z = jnp.zeros((N_DEV, CONV_WIDTH, XBC_WIDTH // N_DEV), F32)
        conv_full.append(lax.dynamic_update_index_in_dim(z, conv_w[l], me, axis=0))
    cw = jnp.stack(conv_full).reshape(-1, LANES)
    cw = _all_reduce_small(cw, "gather_conv_w").reshape(DEPTH, N_DEV, CONV_WIDTH, XBC_WIDTH // N_DEV)
    cw = cw.transpose(0, 2, 1, 3).reshape(DEPTH, CONV_WIDTH, XBC_WIDTH)

    small = [{nm: (cw[l] if nm == "conv_w" else a[l]) for nm, a in sm.items()} for l in range(DEPTH)]
    grad_reduce = _GradReduce()
    loss, dx, grads, d_rel, g_final = _step_local(x[0], loss_target[0], small, Wfull, rel_bias, final_norm_w, prefetch, grad_reduce)
    g_big = {nm: jnp.stack([grad_reduce.out[(l, nm)] for l in range(DEPTH)]) for nm in BIG}

    per_layer = [{nm: grads[l][nm] for nm, _ in SMALL_LAYER} for l in range(DEPTH)]
    packet = _pack_small(per_layer, dict(rel_bias=d_rel, final_norm_w=g_final, loss=loss[0, :1]))
    per_layer, glob = _unpack_small(_all_reduce_small(packet, "all_reduce_small"))
    g_small = {nm: jnp.stack([per_layer[l][nm] for l in range(DEPTH)]) for nm, _ in SMALL_LAYER}
    cwg = g_small["conv_w"].reshape(DEPTH, CONV_WIDTH, N_DEV, XBC_WIDTH // N_DEV)
    g_small["conv_w"] = lax.dynamic_index_in_dim(cwg, me, axis=2, keepdims=False)
    for nm in sm:
        g_small[nm] = g_small[nm].reshape(sm[nm].shape)
    g_rel = glob["rel_bias"].reshape(rel_bias.shape)
    g_fin = glob["final_norm_w"]
    loss_out = glob["loss"][0]

    def adam(w, g, m, v, name):
        shp = w.shape
        two = lambda a: a.reshape(-1, shp[-1]) if a.ndim > 1 else a.reshape(1, -1)
        d, nm_, nv = _adamw(two(w), two(g), two(m), two(v), name)
        return d.reshape(shp), nm_.reshape(shp), nv.reshape(shp)

    order = ["norm1_w", "w_in", "conv_w", "conv_b", "dt_bias", "a_log", "d_skip", "ssm_norm_w", "w_attn_branch", "w_ssm_branch",
             "w_out", "norm2_w", "w_ffn_in", "w_ffn_out", "rel_bias", "final_norm_w"]
    allw = {**big, **sm, "rel_bias": rel_bias, "final_norm_w": final_norm_w}
    allm = {**big_m, **sm_m, "rel_bias": m_rel_bias, "final_norm_w": m_final_norm_w}
    allv = {**big_v, **sm_v, "rel_bias": v_rel_bias, "final_norm_w": v_final_norm_w}
    allg = {**g_big, **g_small, "rel_bias": g_rel, "final_norm_w": g_fin}
    deltas, new_m, new_v = [], [], []
    for nm in order:
        d, a, b = adam(allw[nm], allg[nm], allm[nm], allv[nm], "adamw_" + nm)
        deltas.append(d)
        new_m.append(a)
        new_v.append(b)
    return (loss_out, dx[None], *[allg[nm] for nm in order], *deltas, *new_m, *new_v)
```

```python
import functools
import math

import numpy as np
import jax
import jax.numpy as jnp
from jax import lax
from jax.experimental import pallas as pl
from jax.experimental.pallas import tpu as pltpu

F32, BF16 = jnp.float32, jnp.bfloat16
S = jax.ShapeDtypeStruct
MESH = pl.DeviceIdType.MESH

D_MODEL = 1024
DEPTH = 2
HEAD_DIM = 64
DILATED_GROUPS = ((128, 1), (512, 4), (2048, 16))
N_DIL = 3
HEADS_PER_GROUP = 8
ATTN_WIDTH = 1536
ATTN_OUT_WIDTH = 512
ATTN_BLOCK = 128
N_REL_BUCKETS = 32
REL_MAX_DISTANCE = 2048
D_INNER = 2048
N_SSM_HEADS = 32
N_SSM_GROUPS = 4
D_STATE = 128
CONV_WIDTH = 4
SSD_CHUNK = 128
XBC_WIDTH = 3072
D_FF = 2816
EPS = 1e-6
ADAM_LR, ADAM_B1, ADAM_B2, ADAM_EPS, ADAM_WD, ADAM_STEP = 0.001, 0.9, 0.999, 1e-08, 0.01, 10

N_DEV = 8
LANES = 128
VMEM_LIMIT = 56 * 1024 * 1024
ROW_TILES_BYTES = 36 * 1024 * 1024
NEG = -1e30
BIG = ("w_in", "w_attn_branch", "w_ssm_branch", "w_out", "w_ffn_in", "w_ffn_out")
COL_SHARDED = ("w_in", "w_attn_branch", "w_ffn_in")

NT = (((1,), (1,)), ((), ()))
TN = (((0,), (0,)), ((), ()))


def _cparams(sem=None):
    return pltpu.CompilerParams(dimension_semantics=sem, vmem_limit_bytes=VMEM_LIMIT)


def _pick(n, target, mult=LANES):
    best = None
    for t in range(mult, min(n, target) + 1, mult):
        if n % t == 0:
            best = t
    return best or n


def _silu(x):
    return x * jax.nn.sigmoid(x)


def _dsilu(x):
    s = jax.nn.sigmoid(x)
    return s * (1.0 + x * (1.0 - s))


def _split2(x):
    hi = x.astype(BF16)
    lo = (x - hi.astype(F32)).astype(BF16)
    return hi, lo


def _split3(x):
    x1 = x.astype(BF16)
    r1 = x - x1.astype(F32)
    x2 = r1.astype(BF16)
    x3 = (r1 - x2.astype(F32)).astype(BF16)
    return x1, x2, x3


def _dotx_r(x, m, parts=3):
    xs = _split3(x) if parts == 3 else _split2(x)
    out = jnp.dot(xs[0], m, preferred_element_type=F32)
    for xi in xs[1:]:
        out = out + jnp.dot(xi, m, preferred_element_type=F32)
    return out


def _dotx_l(m, x, parts=3):
    xs = _split3(x) if parts == 3 else _split2(x)
    out = jnp.dot(m, xs[0], preferred_element_type=F32)
    for xi in xs[1:]:
        out = out + jnp.dot(m, xi, preferred_element_type=F32)
    return out


def _mm(a, b, *, ta=False, tb=False, out_dtype=F32, acc=None, name, tm=1536, tn=1536, tk=1536, extras=(), epilogue=None, outs=None, comm=None):
    M, K = (a.shape[1], a.shape[0]) if ta else a.shape
    N = b.shape[0] if tb else b.shape[1]
    tm, tn, tk = _pick(M, tm), _pick(N, tn), _pick(K, tk)
    nk = K // tk
    dims = (((0 if ta else 1,), (1 if tb else 0,)), ((), ()))
    has_acc = acc is not None
    outs = [(N, out_dtype)] if outs is None else outs
    ne, no = len(extras), len(outs)

    def body(*refs):
        a_ref, b_ref = refs[:2]
        c_ref = refs[2] if has_acc else None
        e_refs = refs[2 + has_acc:2 + has_acc + ne]
        o_refs = refs[2 + has_acc + ne:2 + has_acc + ne + no]
        acc_ref = refs[-1]
        k = pl.program_id(2)
        part = lax.dot_general(a_ref[...].astype(BF16), b_ref[...].astype(BF16), dims, preferred_element_type=F32)

        def finish(res):
            tiles = [res] if epilogue is None else epilogue(res, *[e[...] for e in e_refs])
            for o_ref, t in zip(o_refs, tiles):
                o_ref[...] = t.astype(o_ref.dtype)

        if nk == 1:
            finish(part + c_ref[...].astype(F32) if has_acc else part)
        else:
            @pl.when(k == 0)
            def _():
                acc_ref[...] = part + c_ref[...].astype(F32) if has_acc else part

            @pl.when(jnp.logical_and(k > 0, k < nk - 1))
            def _():
                acc_ref[...] += part

            @pl.when(k == nk - 1)
            def _():
                finish(acc_ref[...] + part)

    def cspec(cols):
        return pl.BlockSpec((tm, cols * tn // N), lambda i, j, k: (i, j))

    a_spec = pl.BlockSpec((tk, tm), lambda i, j, k: (k, i)) if ta else pl.BlockSpec((tm, tk), lambda i, j, k: (i, k))
    b_spec = pl.BlockSpec((tn, tk), lambda i, j, k: (j, k)) if tb else pl.BlockSpec((tk, tn), lambda i, j, k: (k, j))
    in_specs, args = [a_spec, b_spec], [a, b]
    if has_acc:
        in_specs.append(cspec(N))
        args.append(acc)
    in_specs += [cspec(e.shape[1]) for e in extras]
    args += list(extras)
    res, comm_res = _carrier_call(
        body, (M // tm, N // tn, nk), in_specs, [cspec(c) for c, _ in outs], [S((M, c), dt) for c, dt in outs],
        [pltpu.VMEM((tm, tn), F32)] if nk > 1 else [], args, comm, name)
    res = res[0] if len(outs) == 1 else res
    return res if comm is None else (res, comm_res)


def _mm_dil(a_list, b_list, d, acc, name, tm=1024):
    T, K = a_list[0].shape
    N = b_list[0].shape[0]
    tm, tn = min(tm, T), _pick(N, 1024)
    na = len(a_list)
    has_acc = acc is not None

    def body(*refs):
        a_refs, b_refs, rest = refs[:na], refs[na:2 * na], refs[2 * na:]
        c_ref = rest[0] if has_acc else None
        o_ref, scr = rest[-2], rest[-1]
        out = c_ref[...] if has_acc else None
        for a_ref, b_ref in zip(a_refs, b_refs):
            a_tok = _dil_to_tok(scr, a_ref, d).astype(BF16) if d > 1 else a_ref[...]
            part = lax.dot_general(a_tok, b_ref[...], NT, preferred_element_type=F32)
            out = part if out is None else out + part
        o_ref[...] = out

    if d > 1:
        a_spec = pl.BlockSpec((d, tm // d, K), lambda i, j: (0, i, 0))
        a_args = [a.reshape(d, T // d, K) for a in a_list]
    else:
        a_spec = pl.BlockSpec((tm, K), lambda i, j: (i, 0))
        a_args = list(a_list)
    o_spec = pl.BlockSpec((tm, tn), lambda i, j: (i, j))
    in_specs = [a_spec] * na + [pl.BlockSpec((tn, K), lambda i, j: (j, 0))] * na + ([o_spec] if has_acc else [])
    return pl.pallas_call(
        body, grid=(T // tm, N // tn), in_specs=in_specs, out_specs=o_spec, out_shape=S((T, N), F32),
        scratch_shapes=[pltpu.VMEM((K // LANES, tm, LANES), F32)],
        compiler_params=_cparams(("parallel", "parallel")), name=name)(*a_args, *b_list, *([acc] if has_acc else []))


class _Comm:
    def __init__(self, ins, outs, sems, start, wait, alias=None):
        self.ins, self.outs, self.sems, self.start, self.wait, self.alias = list(ins), list(outs), list(sems), start, wait, alias or {}


def _carrier_call(body, grid, in_specs, out_specs, out_shape, scratch_shapes, args, comm, name):
    grid = (grid,) if isinstance(grid, int) else tuple(grid)
    seq = ("arbitrary",) * len(grid)
    ni, no, ns = len(in_specs), len(out_specs), len(scratch_shapes)
    if comm is None:
        res = pl.pallas_call(body, grid=grid, in_specs=in_specs, out_specs=out_specs, out_shape=out_shape,
                             scratch_shapes=scratch_shapes, compiler_params=_cparams(seq), name=name)(*args)
        return list(res), []
    ci, co = len(comm.ins), len(comm.outs)

    def wrapped(*refs):
        ins, cins = refs[:ni], refs[ni:ni + ci]
        outs, couts = refs[ni + ci:ni + ci + no], refs[ni + ci + no:ni + ci + no + co]
        scr, csems = refs[ni + ci + no + co:ni + ci + no + co + ns], refs[ni + ci + no + co + ns:]
        ids = [pl.program_id(i) for i in range(len(grid))]
        first = functools.reduce(jnp.logical_and, [i == 0 for i in ids])
        last = functools.reduce(jnp.logical_and, [i == g - 1 for i, g in zip(ids, grid)])

        @pl.when(first)
        def _():
            comm.start(cins, couts, csems)

        body(*ins, *outs, *scr)

        @pl.when(last)
        def _():
            comm.wait(cins, couts, csems)

    anys = pl.BlockSpec(memory_space=pl.ANY)
    res = pl.pallas_call(
        wrapped, grid=grid, in_specs=list(in_specs) + [anys] * ci, out_specs=list(out_specs) + [anys] * co,
        out_shape=list(out_shape) + comm.outs, scratch_shapes=list(scratch_shapes) + comm.sems,
        input_output_aliases={ni + a: no + b for a, b in comm.alias.items()},
        compiler_params=_cparams(seq), name=name)(*args, *comm.ins)
    return list(res[:no]), list(res[no:])


def _dev_index(dev):
    return 4 * dev[0] + 2 * dev[1] + dev[2]


def _comm_gather_spread(shards):
    npc = len(shards)

    def copies(x_refs, o_refs, sems):
        x, y, c, chips = _place()
        me = (x, y, c)
        peers = [(x, y, 1 - c)] + [(*chip, c) for chip in chips]
        return [[pltpu.make_async_remote_copy(src_ref=x_refs[i], dst_ref=o_refs[i].at[_dev_index(me)], send_sem=sems[0].at[k, i],
                                              recv_sem=sems[1].at[k, i], device_id=peer, device_id_type=MESH)
                 for k, peer in enumerate(peers)] for i in range(npc)], peers, me

    def local(x_refs, o_refs, sems, me):
        return [pltpu.make_async_copy(x_refs[i], o_refs[i].at[_dev_index(me)], sems[2].at[i]) for i in range(npc)]

    def start(x_refs, o_refs, sems):
        cps, _, me = copies(x_refs, o_refs, sems)
        for cp in local(x_refs, o_refs, sems, me):
            cp.start()
        for row in cps:
            for cp in row:
                cp.start()

    def wait(x_refs, o_refs, sems):
        cps, peers, me = copies(x_refs, o_refs, sems)
        for i in range(npc):
            for k, peer in enumerate(peers):
                pltpu.make_async_remote_copy(src_ref=x_refs[i], dst_ref=o_refs[i].at[_dev_index(peer)], send_sem=sems[0].at[k, i],
                                             recv_sem=sems[1].at[k, i], device_id=peer, device_id_type=MESH).wait_recv()
        for row in cps:
            for cp in row:
                cp.wait_send()
        for cp in local(x_refs, o_refs, sems, me):
            cp.wait()

    return _Comm(shards, [S((N_DEV,) + s.shape, s.dtype) for s in shards],
                 [pltpu.SemaphoreType.DMA((4, npc)), pltpu.SemaphoreType.DMA((4, npc)), pltpu.SemaphoreType.DMA((npc,))], start, wait)


def _comm_gather_pass(gathered):
    npc = len(gathered)

    def copies(o_refs, sems, sent):
        x, y, c, chips = _place()
        return [pltpu.make_async_remote_copy(
            src_ref=o_refs[i].at[_dev_index((*chip, c))], dst_ref=o_refs[i].at[_dev_index((*chip, c if sent else 1 - c))],
            send_sem=sems[0].at[j, i], recv_sem=sems[1].at[j, i], device_id=(x, y, 1 - c), device_id_type=MESH)
            for i in range(npc) for j, chip in enumerate(chips)]

    def start(g_refs, o_refs, sems):
        for cp in copies(o_refs, sems, True):
            cp.start()

    def wait(g_refs, o_refs, sems):
        for cp in copies(o_refs, sems, False):
            cp.wait_recv()
        for cp in copies(o_refs, sems, True):
            cp.wait_send()

    return _Comm(gathered, [S(g.shape, g.dtype) for g in gathered],
                 [pltpu.SemaphoreType.DMA((3, npc)), pltpu.SemaphoreType.DMA((3, npc))], start, wait,
                 alias={i: i for i in range(npc)})


def _comm_to_chips(parts):
    npc = len(parts)

    def copies(p_refs, o_refs, sems):
        x, y, c, chips = _place()
        return [pltpu.make_async_remote_copy(
            src_ref=p_refs[i].at[2 * chip[0] + chip[1]], dst_ref=o_refs[i].at[j], send_sem=sems[0].at[j, i],
            recv_sem=sems[1].at[j, i], device_id=(*chip, c), device_id_type=MESH)
            for i in range(npc) for j, chip in enumerate(chips)]

    def start(p_refs, o_refs, sems):
        for cp in copies(p_refs, o_refs, sems):
            cp.start()

    def wait(p_refs, o_refs, sems):
        for cp in copies(p_refs, o_refs, sems):
            cp.wait()

    return _Comm(parts, [S((3,) + p.shape[1:], p.dtype) for p in parts],
                 [pltpu.SemaphoreType.DMA((3, npc)), pltpu.SemaphoreType.DMA((3, npc))], start, wait)


def _dil_to_tok(scr, ref, d):
    n, C = ref.shape[1], ref.shape[2]
    for r in range(d):
        v = ref[r].astype(F32)
        for cb in range(C // LANES):
            scr.at[cb][pl.ds(r, n, stride=d), :] = v[:, cb * LANES:(cb + 1) * LANES]
    return jnp.concatenate([scr[cb] for cb in range(C // LANES)], axis=1)


def _tok_to_dil(scr, val, ref, d):
    n, C = ref.shape[1], ref.shape[2]
    for cb in range(C // LANES):
        scr[cb] = val[:, cb * LANES:(cb + 1) * LANES].astype(F32)
    for r in range(d):
        ref[r] = jnp.concatenate([scr.at[cb][pl.ds(r, n, stride=d), :] for cb in range(C // LANES)], axis=1).astype(ref.dtype)


def _rowwise(fn, rows, fulls, outs, accs=(), *, tm, name, cap=True, comm=None):
    rows = [r if isinstance(r, tuple) else (r, r.shape[1], 0) for r in rows]
    first = rows[0]
    T = (first[1] if isinstance(first[0], str) else first[0]).shape[0]
    widest = max([r[1].shape[1] if isinstance(r[0], str) else r[1] for r in rows] + [o[0] for o in outs])
    if cap:
        tm = min(tm, max(8, ROW_TILES_BYTES // (2 * (len(rows) + len(outs))) // (4 * widest) // 8 * 8))
    tm = T if T <= tm else _pick(T, tm, 8)
    nr, nf, no, na = len(rows), len(fulls), len(outs), len(accs)
    dil_in = [i for i, r in enumerate(rows) if isinstance(r[0], str) and r[2] > 1]
    dil_out = [i for i, o in enumerate(outs) if len(o) == 3 and o[2] > 1]
    scr_cols = [rows[i][1].shape[1] for i in dil_in] + [outs[i][0] for i in dil_out]

    def body(*refs):
        r, f = refs[:nr], refs[nr:nr + nf]
        o, a = refs[nr + nf:nr + nf + no], refs[nr + nf + no:nr + nf + no + na]
        scr = refs[nr + nf + no + na:]
        tiles = []
        for i, x in enumerate(r):
            if i in dil_in:
                tiles.append(_dil_to_tok(scr[dil_in.index(i)], x, rows[i][2]))
            else:
                tiles.append(x[...].astype(F32))
        ro, ra = fn(*tiles, *[x[...] for x in f])
        for i, (ref, val) in enumerate(zip(o, ro)):
            if i in dil_out:
                _tok_to_dil(scr[len(dil_in) + dil_out.index(i)], val, ref, outs[i][2])
            else:
                ref[...] = val.astype(ref.dtype)
        if na:
            @pl.when(pl.program_id(0) == 0)
            def _():
                for ref in a:
                    ref[...] = jnp.zeros_like(ref)
            for ref, val in zip(a, ra):
                ref[...] += val

    in_specs, args = [], []
    for i, rr in enumerate(rows):
        if isinstance(rr[0], str):
            arr, d = rr[1], rr[2]
            if d > 1:
                in_specs.append(pl.BlockSpec((d, tm // d, arr.shape[1]), lambda i: (0, i, 0)))
                args.append(arr.reshape(d, T // d, arr.shape[1]))
            else:
                in_specs.append(pl.BlockSpec((tm, arr.shape[1]), lambda i: (i, 0)))
                args.append(arr)
        else:
            in_specs.append(pl.BlockSpec((tm, rr[1]), functools.partial(lambda i, cb: (i, cb), cb=rr[2])))
            args.append(rr[0])
    in_specs += [pl.BlockSpec(f.shape, lambda i: (0, 0)) for f in fulls]
    out_specs, out_shape = [], []
    for i, oo in enumerate(outs):
        if i in dil_out:
            d = oo[2]
            out_specs.append(pl.BlockSpec((d, tm // d, oo[0]), lambda i: (0, i, 0)))
            out_shape.append(S((d, T // d, oo[0]), oo[1]))
        else:
            out_specs.append(pl.BlockSpec((tm, oo[0]), lambda i: (i, 0)))
            out_shape.append(S((T, oo[0]), oo[1]))
    out_specs += [pl.BlockSpec(sh, lambda i: (0, 0)) for sh in accs]
    out_shape += [S(sh, F32) for sh in accs]
    res, comm_res = _carrier_call(
        body, T // tm, in_specs, out_specs, out_shape, [pltpu.VMEM((c // LANES, tm, LANES), F32) for c in scr_cols],
        list(args) + list(fulls), comm, name)
    res = [x.reshape(T, x.shape[2]) if i in dil_out else x for i, x in enumerate(res)]
    return res if comm is None else (res, comm_res)


def _rmsnorm_fwd(h, w, name, dils=()):
    D = h.shape[1]

    def fn(h, w):
        r = lax.rsqrt(jnp.mean(h * h, axis=-1, keepdims=True) + EPS)
        xn = h * r * w
        return [xn] * (1 + len(dils)), []
    return _rowwise(fn, [h], [w.reshape(1, -1)], [(D, BF16)] + [(D, BF16, d) for d in dils], tm=512, name=name)


def _rmsnorm_bwd(dxn, h, w, dres, name):
    def fn(dxn, h, dres, w):
        r = lax.rsqrt(jnp.mean(h * h, axis=-1, keepdims=True) + EPS)
        n = h * r
        dn = dxn * w
        dh = r * (dn - n * jnp.mean(dn * n, axis=-1, keepdims=True)) + dres
        return [dh], [jnp.sum(dxn * n, axis=0, keepdims=True)]
    D = h.shape[1]
    return _rowwise(fn, [dxn, h, dres], [w.reshape(1, -1)], [(D, F32)], [(1, D)], tm=256, name=name)


def _loss_head(h, w, tgt, name):
    D = h.shape[1]

    def fn(h, tgt, w):
        r = lax.rsqrt(jnp.mean(h * h, axis=-1, keepdims=True) + EPS)
        n = h * r
        e = n * w - tgt
        row_loss = 0.5 * jnp.mean(e * e, axis=-1, keepdims=True)
        dy = e * (1.0 / D)
        dn = dy * w
        dh = r * (dn - n * jnp.mean(dn * n, axis=-1, keepdims=True))
        return [dh], [jnp.sum(dy * n, axis=0, keepdims=True), jnp.broadcast_to(jnp.sum(row_loss, axis=0, keepdims=True), (1, LANES))]
    return _rowwise(fn, [h, tgt], [w.reshape(1, -1)], [(D, F32)], [(1, D), (1, LANES)], tm=256, name=name)


def _combine_fwd(os_, ls, name):
    def fn(o0, o1, o2, l0, l1, l2):
        m = jnp.maximum(jnp.maximum(l0, l1), l2)
        e0, e1, e2 = jnp.exp(l0 - m), jnp.exp(l1 - m), jnp.exp(l2 - m)
        attn = (e0 * o0 + e1 * o1 + e2 * o2) / (e0 + e1 + e2)
        return [attn, attn], []
    dil = [("dil", t, d) for t, (_, d) in zip(list(os_) + list(ls), DILATED_GROUPS * 2)]
    return _rowwise(fn, dil, [], [(ATTN_OUT_WIDTH, BF16), (ATTN_OUT_WIDTH, F32)], tm=512, name=name)


def _combine_bwd(dattn, attn, ls, head_ones, name):
    def fn(dattn, attn, l0, l1, l2, ones):
        m = jnp.maximum(jnp.maximum(l0, l1), l2)
        e0, e1, e2 = jnp.exp(l0 - m), jnp.exp(l1 - m), jnp.exp(l2 - m)
        inv = 1.0 / (e0 + e1 + e2)
        t = _dotx_r(dattn * attn, ones, parts=2)
        outs = []
        for e in (e0, e1, e2):
            al = e * inv
            outs += [al * dattn, al * t]
        return outs, []
    W = ATTN_OUT_WIDTH
    dil = [("dil", t, d) for t, (_, d) in zip(ls, DILATED_GROUPS)]
    outs = [(W, dt, d) for _, d in DILATED_GROUPS for dt in (BF16, F32)]
    return _rowwise(fn, [dattn, attn] + dil, [head_ones], outs, tm=512, name=name)


def _dt_fwd(dt_raw, dt_bias, name):
    def fn(raw, b):
        z = raw[:, :N_SSM_HEADS] + b
        return [jnp.maximum(z, 0.0) + jnp.log(1.0 + jnp.exp(-jnp.abs(z)))], []
    return _rowwise(fn, [dt_raw], [dt_bias.reshape(1, -1)], [(N_SSM_HEADS, F32)], tm=1024, name=name)[0]


def _dt_bwd(ddt_a, ddt_b, dt_raw, dt_bias, name):
    def fn(da, db, raw, b):
        g = (da + db) * jax.nn.sigmoid(raw[:, :N_SSM_HEADS] + b)
        pad = jnp.zeros((g.shape[0], LANES - N_SSM_HEADS), F32)
        return [jnp.concatenate([g, pad], axis=1)], [jnp.sum(g, axis=0, keepdims=True)]
    return _rowwise(fn, [ddt_a, ddt_b, dt_raw], [dt_bias.reshape(1, -1)], [(LANES, BF16)], [(1, N_SSM_HEADS)], tm=1024, name=name)


def _ssm_norm_fwd(y, z, w, name, comm=None):
    G = D_INNER // N_SSM_GROUPS

    def fn(y, z, w):
        yg = y * _silu(z)
        outs = []
        for g in range(N_SSM_GROUPS):
            t = yg[:, g * G:(g + 1) * G]
            outs.append(t * lax.rsqrt(jnp.mean(t * t, axis=-1, keepdims=True) + EPS))
        return [jnp.concatenate(outs, axis=1) * w], []
    res = _rowwise(fn, [y, z], [w.reshape(1, -1)], [(D_INNER, BF16)], tm=256, name=name, comm=comm)
    return res[0] if comm is None else (res[0][0], res[1])


def _ssm_norm_bwd(dssm, y, z, w, name):
    G = D_INNER // N_SSM_GROUPS

    def fn(dssm, y, z, w):
        sz = _silu(z)
        yg = y * sz
        dn = dssm * w
        ns, dygs = [], []
        for g in range(N_SSM_GROUPS):
            t = yg[:, g * G:(g + 1) * G]
            r = lax.rsqrt(jnp.mean(t * t, axis=-1, keepdims=True) + EPS)
            n = t * r
            d = dn[:, g * G:(g + 1) * G]
            dygs.append(r * (d - n * jnp.mean(d * n, axis=-1, keepdims=True)))
            ns.append(n)
        n, dyg = jnp.concatenate(ns, axis=1), jnp.concatenate(dygs, axis=1)
        return [dyg * sz, dyg * y * _dsilu(z)], [jnp.sum(dssm * n, axis=0, keepdims=True)]
    return _rowwise(fn, [dssm, y, z], [w.reshape(1, -1)], [(D_INNER, F32), (D_INNER, BF16)], [(1, D_INNER)], tm=256, name=name)


def _gate_out_proj(a, sb, gl, h, w_out, name):
    def fn(a, sb, gl, h, w):
        g = jax.nn.sigmoid(gl)
        merged = (g[:, :D_MODEL] * a + g[:, D_MODEL:] * sb).astype(BF16)
        return [h + jnp.dot(merged, w, preferred_element_type=F32), merged], []
    return _rowwise(fn, [a, sb, gl, h], [w_out], [(D_MODEL, F32), (D_MODEL, BF16)], tm=512, cap=False, name=name)


def _d_out_proj_gate(dh, a, sb, gl, w_out, name):
    def fn(dh, a, sb, gl, w):
        dm = lax.dot_general(dh.astype(BF16), w, NT, preferred_element_type=F32)
        g = jax.nn.sigmoid(gl)
        g0, g1 = g[:, :D_MODEL], g[:, D_MODEL:]
        dgl = jnp.concatenate([dm * a * g0 * (1.0 - g0), dm * sb * g1 * (1.0 - g1)], axis=1)
        return [g0 * dm, g1 * dm, dgl], []
    return _rowwise(fn, [dh, a, sb, gl], [w_out], [(D_MODEL, BF16), (D_MODEL, BF16), (2 * D_MODEL, BF16)], tm=512, cap=False, name=name)


FFN_HALF = D_FF // 2


def _ffn_perm(w):
    h = FFN_HALF
    return jnp.concatenate([w[:, 0:h], w[:, D_FF:D_FF + h], w[:, h:D_FF], w[:, D_FF + h:]], axis=1)


def _ffn_unperm(w):
    h = FFN_HALF
    return jnp.concatenate([w[:, 0:h], w[:, 2 * h:3 * h], w[:, h:2 * h], w[:, 3 * h:]], axis=1)


def _swiglu_epilogue(res):
    return [res, _silu(res[:, :FFN_HALF]) * res[:, FFN_HALF:]]


def _dswiglu_epilogue(dact, u):
    u = u.astype(F32)
    gate, up = u[:, :FFN_HALF], u[:, FFN_HALF:]
    return [jnp.concatenate([dact * up * _dsilu(gate), dact * _silu(gate)], axis=1)]


def _adamw(w, g, m, v, name):
    c1 = 1.0 - ADAM_B1 ** ADAM_STEP
    c2 = 1.0 - ADAM_B2 ** ADAM_STEP

    def fn(w, g, m, v):
        m = ADAM_B1 * m + (1.0 - ADAM_B1) * g
        v = ADAM_B2 * v + (1.0 - ADAM_B2) * (g * g)
        delta = -ADAM_LR * ((m / c1) / (jnp.sqrt(v / c2) + ADAM_EPS) + ADAM_WD * w)
        return [delta, m, v], []
    C = w.shape[1]
    return _rowwise(fn, [w, g, m, v], [], [(C, F32)] * 3, tm=256, name=name)


def _bias_consts(dilation, n_steps):
    qi = np.arange(ATTN_BLOCK)[:, None]
    kj = np.arange(2 * ATTN_BLOCK)[None, :]
    steps = qi + ATTN_BLOCK - kj
    valid = (steps >= 0) & (steps <= n_steps)
    dist = jnp.asarray(np.clip(steps, 0, n_steps) * dilation, jnp.int32)
    max_exact = N_REL_BUCKETS // 2
    d_f = jnp.maximum(dist, 1).astype(F32)
    large = max_exact + (jnp.log(d_f / max_exact) / math.log(REL_MAX_DISTANCE / max_exact)
                         * (N_REL_BUCKETS - max_exact)).astype(jnp.int32)
    large = jnp.minimum(large, N_REL_BUCKETS - 1)
    bucket = jnp.where(dist < max_exact, dist, large).reshape(-1)
    onehot = (bucket[None, :] == jnp.arange(N_REL_BUCKETS)[:, None]).astype(F32)
    return onehot, jnp.asarray(valid.reshape(1, -1), F32)


def _bias_gather(rel_g_t, onehot, valid, name):
    def body(r_ref, oh_ref, v_ref, o_ref):
        b = jnp.dot(r_ref[...], oh_ref[...], preferred_element_type=F32, precision=lax.Precision.HIGHEST)
        o_ref[...] = jnp.where(v_ref[...] > 0.5, b, NEG)
    return pl.pallas_call(body, out_shape=S((HEADS_PER_GROUP, onehot.shape[1]), F32), compiler_params=_cparams(), name=name)(rel_g_t, onehot, valid)


def _bias_scatter(dbias, onehot, name):
    def body(d_ref, oh_ref, o_ref):
        o_ref[...] = lax.dot_general(d_ref[...], oh_ref[...], NT, preferred_element_type=F32, precision=lax.Precision.HIGHEST)
    return pl.pallas_call(body, out_shape=S((HEADS_PER_GROUP, N_REL_BUCKETS), F32), compiler_params=_cparams(), name=name)(dbias, onehot)


ATTN_QB_FWD, ATTN_QB_BWD = 4, 4


def _attn_tiles(T, d, qb):
    seg = T // d
    nqb = min(qb, seg // ATTN_BLOCK)
    tq = nqb * ATTN_BLOCK
    return seg, nqb, tq, seg // tq


def _attn_fwd(qkv, bias, d, name, comm=None):
    T = qkv.shape[0]
    seg, nqb, tq, ns = _attn_tiles(T, d, ATTN_QB_FWD)
    W = ATTN_OUT_WIDTH
    scale = HEAD_DIM ** -0.5

    def body(q_ref, kh_ref, kc_ref, vh_ref, vc_ref, b_ref, o_ref, l_ref, s_scr, p_scr):
        n = pl.program_id(1)
        qv = q_ref[...]
        kk = jnp.concatenate([kh_ref[...], kc_ref[...]], axis=0)
        vv = jnp.concatenate([vh_ref[...], vc_ref[...]], axis=0)
        col = lax.broadcasted_iota(jnp.int32, (ATTN_BLOCK, 2 * ATTN_BLOCK), 1)
        kill = jnp.logical_and(n == 0, col < ATTN_BLOCK)
        lo = lax.broadcasted_iota(jnp.int32, (1, LANES), 1) < HEAD_DIM
        zero = jnp.zeros((), BF16)
        for j in range(nqb):
            rows = slice(j * ATTN_BLOCK, (j + 1) * ATTN_BLOCK)
            keys = slice(j * ATTN_BLOCK, (j + 2) * ATTN_BLOCK)
            for hp in range(HEADS_PER_GROUP // 2):
                ps = slice(hp * LANES, (hp + 1) * LANES)
                q2 = (qv[rows, ps].astype(F32) * scale).astype(BF16)
                k2 = kk[keys, ps]
                s_scr[2 * hp] = lax.dot_general(q2, jnp.where(lo, k2, zero), NT, preferred_element_type=F32)
                s_scr[2 * hp + 1] = lax.dot_general(q2, jnp.where(lo, zero, k2), NT, preferred_element_type=F32)
            s = s_scr[...] + b_ref[...]
            if j == 0:
                s = jnp.where(kill[None], NEG, s)
            m = jnp.max(s, axis=-1, keepdims=True)
            p = jnp.exp(s - m)
            den = jnp.sum(p, axis=-1, keepdims=True)
            p_scr[...] = p.astype(BF16)
            inv = 1.0 / den
            lse = m + jnp.log(den)
            for hp in range(HEADS_PER_GROUP // 2):
                ps = slice(hp * LANES, (hp + 1) * LANES)
                v2 = vv[keys, ps]
                o2 = (jnp.dot(p_scr[2 * hp], jnp.where(lo, v2, zero), preferred_element_type=F32)
                      + jnp.dot(p_scr[2 * hp + 1], jnp.where(lo, zero, v2), preferred_element_type=F32))
                o_ref[rows, ps] = (o2 * jnp.where(lo, inv[2 * hp], inv[2 * hp + 1])).astype(o_ref.dtype)
                l_ref[rows, ps] = jnp.where(lo, lse[2 * hp], lse[2 * hp + 1])

    def cur(c):
        return pl.BlockSpec((tq, W), lambda r, n: (r * ns + n, c))

    def halo(c):
        return pl.BlockSpec((ATTN_BLOCK, W), lambda r, n: (jnp.maximum((r * ns + n) * nqb - 1, 0), c))

    res, comm_res = _carrier_call(
        body, (d, ns),
        [cur(0), halo(1), cur(1), halo(2), cur(2), pl.BlockSpec(bias.shape, lambda r, n: (0, 0, 0))],
        [cur(0), cur(0)], [S((T, W), BF16), S((T, W), F32)],
        [pltpu.VMEM((HEADS_PER_GROUP, ATTN_BLOCK, 2 * ATTN_BLOCK), F32), pltpu.VMEM((HEADS_PER_GROUP, ATTN_BLOCK, 2 * ATTN_BLOCK), BF16)],
        [qkv, qkv, qkv, qkv, qkv, bias], comm, name)
    return res if comm is None else (res, comm_res)


def _attn_bwd(qkv, bias, lse, do, dd, d, name):
    T = qkv.shape[0]
    seg, nqb, tq, ns = _attn_tiles(T, d, ATTN_QB_BWD)
    W = ATTN_OUT_WIDTH
    B = ATTN_BLOCK
    scale = HEAD_DIM ** -0.5

    def body(q_ref, kh_ref, kc_ref, vh_ref, vc_ref, b_ref, l_ref, do_ref, dd_ref, dq_ref, dk_ref, dv_ref, db_ref, pk_ref, pv_ref,
             s_scr, dp_scr, p_scr, ds_scr):
        r, n = pl.program_id(0), pl.program_id(1)

        @pl.when(jnp.logical_and(r == 0, n == 0))
        def _():
            db_ref[...] = jnp.zeros_like(db_ref)

        @pl.when(n == 0)
        def _():
            pk_ref[...] = jnp.zeros_like(pk_ref)
            pv_ref[...] = jnp.zeros_like(pv_ref)

        @pl.when(n < ns)
        def _():
            qv = q_ref[...]
            kk = jnp.concatenate([kh_ref[...], kc_ref[...]], axis=0)
            vv = jnp.concatenate([vh_ref[...], vc_ref[...]], axis=0)
            lse_v, do_v, dd_v = l_ref[...], do_ref[...], dd_ref[...]
            col = lax.broadcasted_iota(jnp.int32, (B, 2 * B), 1)
            kill = jnp.logical_and(n == 0, col < B)
            dqs = [[None] * (HEADS_PER_GROUP // 2) for _ in range(nqb)]
            dks = [[None] * (HEADS_PER_GROUP // 2) for _ in range(nqb)]
            dvs = [[None] * (HEADS_PER_GROUP // 2) for _ in range(nqb)]
            H, HP = HEADS_PER_GROUP, HEADS_PER_GROUP // 2
            do_b = do_v.astype(BF16)
            lo = lax.broadcasted_iota(jnp.int32, (1, LANES), 1) < HEAD_DIM
            zero = jnp.zeros((), BF16)
            first = lambda t: jnp.where(lo, t, zero)
            second = lambda t: jnp.where(lo, zero, t)
            for j in range(nqb):
                rows = slice(j * B, (j + 1) * B)
                keys = slice(j * B, (j + 2) * B)
                for hp in range(HP):
                    ps = slice(hp * LANES, (hp + 1) * LANES)
                    q2 = (qv[rows, ps].astype(F32) * scale).astype(BF16)
                    k2, v2, do2 = kk[keys, ps], vv[keys, ps], do_b[rows, ps]
                    s_scr[2 * hp] = lax.dot_general(q2, first(k2), NT, preferred_element_type=F32)
                    s_scr[2 * hp + 1] = lax.dot_general(q2, second(k2), NT, preferred_element_type=F32)
                    dp_scr[2 * hp] = lax.dot_general(do2, first(v2), NT, preferred_element_type=F32)
                    dp_scr[2 * hp + 1] = lax.dot_general(do2, second(v2), NT, preferred_element_type=F32)
                lse_h = jnp.stack([lse_v[rows, h * HEAD_DIM:h * HEAD_DIM + 1] for h in range(H)], axis=0)
                dd_h = jnp.stack([dd_v[rows, h * HEAD_DIM:h * HEAD_DIM + 1] for h in range(H)], axis=0)
                s = s_scr[...] + b_ref[...]
                if j == 0:
                    s = jnp.where(kill[None], NEG, s)
                p = jnp.exp(s - lse_h)
                ds = p * (dp_scr[...] - dd_h)
                db_ref[...] += ds
                p_scr[...] = p.astype(BF16)
                ds_scr[...] = ds.astype(BF16)
                for hp in range(HP):
                    ps = slice(hp * LANES, (hp + 1) * LANES)
                    q2 = (qv[rows, ps].astype(F32) * scale).astype(BF16)
                    k2, do2 = kk[keys, ps], do_b[rows, ps]
                    pa, pb, da, db_ = p_scr[2 * hp], p_scr[2 * hp + 1], ds_scr[2 * hp], ds_scr[2 * hp + 1]
                    dvs[j][hp] = (lax.dot_general(pa, first(do2), TN, preferred_element_type=F32)
                                  + lax.dot_general(pb, second(do2), TN, preferred_element_type=F32))
                    dqs[j][hp] = (jnp.dot(da, first(k2), preferred_element_type=F32)
                                  + jnp.dot(db_, second(k2), preferred_element_type=F32)) * scale
                    dks[j][hp] = (lax.dot_general(da, first(q2), TN, preferred_element_type=F32)
                                  + lax.dot_general(db_, second(q2), TN, preferred_element_type=F32))
            dq_ref[...] = jnp.concatenate([jnp.concatenate(dqs[j], axis=1) for j in range(nqb)], axis=0).astype(dq_ref.dtype)
            for parts, out_ref, pend in ((dks, dk_ref, pk_ref), (dvs, dv_ref, pv_ref)):
                full = [jnp.concatenate(parts[j], axis=1) for j in range(nqb)]
                if tq > B:
                    out_ref[:tq - B] = pend[:tq - B].astype(out_ref.dtype)
                out_ref[tq - B:] = (pend[tq - B:] + full[0][:B]).astype(out_ref.dtype)
                for j in range(nqb - 1):
                    pend[j * B:(j + 1) * B] = full[j][B:] + full[j + 1][:B]
                pend[tq - B:] = full[nqb - 1][B:]

        @pl.when(n == ns)
        def _():
            dk_ref[...] = pk_ref[...].astype(dk_ref.dtype)
            dv_ref[...] = pv_ref[...].astype(dv_ref.dtype)

    def cur(c):
        return pl.BlockSpec((tq, W), lambda r, n: (r * ns + jnp.minimum(n, ns - 1), c))

    def halo(c):
        return pl.BlockSpec((B, W), lambda r, n: (jnp.maximum((r * ns + jnp.minimum(n, ns - 1)) * nqb - 1, 0), c))

    late = pl.BlockSpec((tq, W), lambda r, n: (r * ns + jnp.clip(n - 1, 0, ns - 1), 0))
    bspec = pl.BlockSpec(bias.shape, lambda r, n: (0, 0, 0))
    return pl.pallas_call(
        body, grid=(d, ns + 1),
        in_specs=[cur(0), halo(1), cur(1), halo(2), cur(2), bspec, cur(0), cur(0), cur(0)],
        out_specs=[cur(0), late, late, bspec],
        out_shape=[S((T, W), BF16)] * 3 + [S(bias.shape, F32)],
        scratch_shapes=[pltpu.VMEM((tq, W), F32), pltpu.VMEM((tq, W), F32)]
                       + [pltpu.VMEM((HEADS_PER_GROUP, B, 2 * B), t) for t in (F32, F32, BF16, BF16)],
        compiler_params=_cparams(("arbitrary", "arbitrary")), name=name,
    )(qkv, qkv, qkv, qkv, qkv, bias, lse, do, dd)


CONV_TM, CONV_TC = 512, 1024


def _shift_down(x, halo8, s, row8):
    xr = pltpu.roll(x, s, 0)
    first = jnp.where(row8 < s, pltpu.roll(halo8, s, 0), xr[:8])
    return jnp.concatenate([first, xr[8:]], axis=0)


def _shift_up(x, halo8, s, row8):
    n = x.shape[0]
    xr = pltpu.roll(x, n - s, 0)
    last = jnp.where(row8 >= 8 - s, pltpu.roll(halo8, 8 - s, 0), xr[n - 8:])
    return jnp.concatenate([xr[:n - 8], last], axis=0)


def _conv_fwd(x, w, b, name, comm=None):
    T, C = x.shape
    tm, tc = min(CONV_TM, T), CONV_TC

    def body(x_ref, p_ref, w_ref, b_ref, u_ref, a_ref):
        ti = pl.program_id(1)
        xv = x_ref[...]
        p8 = jnp.where(ti == 0, 0.0, p_ref[...])
        wv = w_ref[...]
        row8 = lax.broadcasted_iota(jnp.int32, (8, tc), 0)
        u = xv * wv[3:4] + b_ref[...]
        for s in (1, 2, 3):
            u = u + _shift_down(xv, p8, s, row8) * wv[3 - s:4 - s]
        u_ref[...] = u
        a_ref[...] = _silu(u)

    cur = pl.BlockSpec((tm, tc), lambda cj, ti: (ti, cj))
    halo = pl.BlockSpec((8, tc), lambda cj, ti: (jnp.maximum(ti * (tm // 8) - 1, 0), cj))
    res, comm_res = _carrier_call(
        body, (C // tc, T // tm),
        [cur, halo, pl.BlockSpec((CONV_WIDTH, tc), lambda cj, ti: (0, cj)), pl.BlockSpec((1, tc), lambda cj, ti: (0, cj))],
        [cur, cur], [S((T, C), F32)] * 2, [], [x, x, w, b], comm, name)
    return res if comm is None else (res, comm_res)


def _conv_bwd(dact, u, x, w, name):
    T, C = x.shape
    tm, tc = min(CONV_TM, T), CONV_TC
    nt = T // tm

    def body(d_ref, dn_ref, u_ref, un_ref, x_ref, w_ref, dx_ref, dw_ref, db_ref):
        ti = pl.program_id(1)

        @pl.when(ti == 0)
        def _():
            dw_ref[...] = jnp.zeros_like(dw_ref)
            db_ref[...] = jnp.zeros_like(db_ref)

        du = d_ref[...] * _dsilu(u_ref[...])
        dun = jnp.where(ti == nt - 1, 0.0, dn_ref[...] * _dsilu(un_ref[...]))
        xv = x_ref[...]
        wv = w_ref[...]
        row8 = lax.broadcasted_iota(jnp.int32, (8, tc), 0)
        dx = du * wv[3:4]
        dws = [None] * CONV_WIDTH
        dws[3] = jnp.sum(du * xv, axis=0, keepdims=True)
        for s in (1, 2, 3):
            up = _shift_up(du, dun, s, row8)
            dx = dx + up * wv[3 - s:4 - s]
            dws[3 - s] = jnp.sum(up * xv, axis=0, keepdims=True)
        dx_ref[...] = dx.astype(dx_ref.dtype)
        dw_ref[...] += jnp.concatenate(dws, axis=0)
        db_ref[...] += jnp.sum(du, axis=0, keepdims=True)

    cur = pl.BlockSpec((tm, tc), lambda cj, ti: (ti, cj))
    nxt = pl.BlockSpec((8, tc), lambda cj, ti: (jnp.minimum((ti + 1) * (tm // 8), T // 8 - 1), cj))
    return pl.pallas_call(
        body, grid=(C // tc, nt),
        in_specs=[cur, nxt, cur, nxt, cur, pl.BlockSpec((CONV_WIDTH, tc), lambda cj, ti: (0, cj))],
        out_specs=[cur, pl.BlockSpec((CONV_WIDTH, tc), lambda cj, ti: (0, cj)), pl.BlockSpec((1, tc), lambda cj, ti: (0, cj))],
        out_shape=[S((T, C), BF16), S((CONV_WIDTH, C), F32), S((1, C), F32)],
        compiler_params=_cparams(("parallel", "arbitrary")), name=name)(dact, dact, u, u, x, w)


def _ssd_consts():
    i = np.arange(SSD_CHUNK)
    tril = (i[None, :] <= i[:, None]).astype(np.float32)
    trils = (i[None, :] < i[:, None]).astype(np.float32)
    head = np.repeat(np.arange(N_SSM_HEADS), D_INNER // N_SSM_HEADS)
    et = (head[None, :] == np.arange(N_SSM_HEADS)[:, None]).astype(np.float32)
    c = lambda a: jnp.asarray(a, BF16)
    return dict(tril=c(tril), triu=c(tril.T), trils=c(trils), trius=c(trils.T), et=c(et), e=c(et.T))


def _ssd_common(act_ref, dt_ref, dtT_ref, al_ref, alT_ref, tril_ref, triu_ref, et_ref):
    a_row = -jnp.exp(al_ref[...])
    a_col = -jnp.exp(alT_ref[...])
    dt, dtT = dt_ref[...], dtT_ref[...]
    la = _dotx_l(tril_ref[...], dt * a_row)
    laT = _dotx_r(dtT * a_col, triu_ref[...])
    et = et_ref[...]
    la_e = _dotx_r(la, et)
    dt_e = _dotx_r(dt, et, parts=2)
    x = act_ref[:, :D_INNER]
    xdt = x * dt_e
    la_q = la_e[SSD_CHUNK - 1:SSD_CHUNK, :]
    return a_row, a_col, dt, dtT, la, laT, la_e, dt_e, x, xdt, la_q


def _decay(la, laT, h, causal):
    seg = la[:, h:h + 1] - laT[h:h + 1, :]
    return jnp.exp(jnp.where(causal, seg, NEG))


def _ssd_fwd(act, dt, dtT, alog, dskip_e, cs, name, comm=None):
    T = act.shape[0]
    nc = T // SSD_CHUNK
    Q, G, GW = SSD_CHUNK, N_SSM_GROUPS, D_INNER // N_SSM_GROUPS

    def body(act_ref, dt_ref, dtT_ref, al_ref, alT_ref, dsk_ref, tril_ref, triu_ref, et_ref, y_ref, st_ref, scr):
        @pl.when(pl.program_id(0) == 0)
        def _():
            scr[...] = jnp.zeros_like(scr)
        st_ref[0] = scr[...]
        a_row, a_col, dtv, dtTv, la, laT, la_e, dt_e, x, xdt, la_q = _ssd_common(
            act_ref, dt_ref, dtT_ref, al_ref, alT_ref, tril_ref, triu_ref, et_ref)
        ela = jnp.exp(la_e)
        xdt_b = xdt.astype(BF16)
        xdte_b = (xdt * jnp.exp(la_q - la_e)).astype(BF16)
        ela_q = jnp.exp(la_q)
        causal = lax.broadcasted_iota(jnp.int32, (Q, Q), 0) >= lax.broadcasted_iota(jnp.int32, (Q, Q), 1)
        for g in range(G):
            gs = slice(g * GW, (g + 1) * GW)
            Bg = act_ref[:, D_INNER + g * D_STATE:D_INNER + (g + 1) * D_STATE].astype(BF16)
            Cg = act_ref[:, D_INNER + G * D_STATE + g * D_STATE:D_INNER + G * D_STATE + (g + 1) * D_STATE].astype(BF16)
            cb = lax.dot_general(Cg, Bg, NT, preferred_element_type=F32)
            st = scr[g]
            y_inter = jnp.dot(Cg, st.astype(BF16), preferred_element_type=F32) * ela[:, gs]
            ys = []
            for hh in range(HEADS_PER_GROUP):
                h = g * HEADS_PER_GROUP + hh
                m = (cb * _decay(la, laT, h, causal)).astype(BF16)
                ys.append(jnp.dot(m, xdt_b[:, h * HEAD_DIM:(h + 1) * HEAD_DIM], preferred_element_type=F32))
            y_ref[:, gs] =jnp.concatenate(ys, axis=1) + y_inter + x[:, gs] * dsk_ref[:, gs]
            scr[g] = st * ela_q[:, gs] + lax.dot_general(Bg, xdte_b[:, gs], TN, preferred_element_type=F32)

    full = lambda a: pl.BlockSpec(a.shape, lambda c: (0,) * a.ndim)
    al, alT = alog.reshape(1, -1), alog.reshape(-1, 1)
    res, comm_res = _carrier_call(
        body, nc,
        [pl.BlockSpec((Q, XBC_WIDTH), lambda c: (c, 0)), pl.BlockSpec((Q, N_SSM_HEADS), lambda c: (c, 0)),
         pl.BlockSpec((N_SSM_HEADS, Q), lambda c: (0, c)), full(al), full(alT), full(dskip_e),
         full(cs["tril"]), full(cs["triu"]), full(cs["et"])],
        [pl.BlockSpec((Q, D_INNER), lambda c: (c, 0)), pl.BlockSpec((1, G, D_STATE, GW), lambda c: (c, 0, 0, 0))],
        [S((T, D_INNER), F32), S((nc, G, D_STATE, GW), F32)],
        [pltpu.VMEM((G, D_STATE, GW), F32)],
        [act, dt, dtT, al, alT, dskip_e, cs["tril"], cs["triu"], cs["et"]], comm, name)
    return res if comm is None else (res, comm_res)


def _ssd_bwd(dy, act, dt, dtT, alog, dskip_e, states, cs, name, comm=None):
    T = act.shape[0]
    nc = T // SSD_CHUNK
    Q, G, GW, H = SSD_CHUNK, N_SSM_GROUPS, D_INNER // N_SSM_GROUPS, N_SSM_HEADS

    def body(dy_ref, act_ref, dt_ref, dtT_ref, al_ref, alT_ref, dsk_ref, stp_ref, tril_ref, triu_ref, trils_ref, trius_ref,
             et_ref, e_ref, dact_ref, ddt_ref, ddtT_ref, da_ref, daT_ref, dsk_out_ref, dst, wbuf, ubuf, vbuf, sbuf, dm_scr, m_scr):
        @pl.when(pl.program_id(0) == 0)
        def _():
            dst[...] = jnp.zeros_like(dst)
            da_ref[...] = jnp.zeros_like(da_ref)
            daT_ref[...] = jnp.zeros_like(daT_ref)
            dsk_out_ref[...] = jnp.zeros_like(dsk_out_ref)
        a_row, a_col, dtv, dtTv, la, laT, la_e, dt_e, x, xdt, la_q = _ssd_common(
            act_ref, dt_ref, dtT_ref, al_ref, alT_ref, tril_ref, triu_ref, et_ref)
        dyv = dy_ref[...]
        ela = jnp.exp(la_e)
        e_end = jnp.exp(la_q - la_e)
        ela_q = jnp.exp(la_q)
        dye_b = (dyv * ela).astype(BF16)
        dy_b = dyv.astype(BF16)
        xdt_b = xdt.astype(BF16)
        xdte_b = (xdt * e_end).astype(BF16)
        ri = lax.broadcasted_iota(jnp.int32, (Q, Q), 0)
        ci = lax.broadcasted_iota(jnp.int32, (Q, Q), 1)
        causal = ri >= ci
        rows = []
        for g in range(G):
            gs = slice(g * GW, (g + 1) * GW)
            Bg = act_ref[:, D_INNER + g * D_STATE:D_INNER + (g + 1) * D_STATE].astype(BF16)
            Cg = act_ref[:, D_INNER + G * D_STATE + g * D_STATE:D_INNER + G * D_STATE + (g + 1) * D_STATE].astype(BF16)
            cb = lax.dot_general(Cg, Bg, NT, preferred_element_type=F32)
            stp = stp_ref[0, g]
            stp_b = stp.astype(BF16)
            dstv = dst[g]
            dst_b = dstv.astype(BF16)
            y_inter = jnp.dot(Cg, stp_b, preferred_element_type=F32) * ela[:, gs]
            wbuf[:, gs] = dyv[:, gs] * y_inter
            dxdt_state = jnp.dot(Bg, dst_b, preferred_element_type=F32) * e_end[:, gs]
            ubuf[:, gs] = dxdt_state * xdt[:, gs]
            dC = lax.dot_general(dye_b[:, gs], stp_b, NT, preferred_element_type=F32)
            dB = lax.dot_general(xdte_b[:, gs], dst_b, NT, preferred_element_type=F32)
            sbuf[:, gs] = jnp.broadcast_to(jnp.sum(dstv * stp, axis=0, keepdims=True), (8, GW))
            dst[g] = dstv * ela_q[:, gs] + lax.dot_general(Cg, dye_b[:, gs], TN, preferred_element_type=F32)
            for hh in range(HEADS_PER_GROUP):
                hs = slice((g * HEADS_PER_GROUP + hh) * HEAD_DIM, (g * HEADS_PER_GROUP + hh + 1) * HEAD_DIM)
                dm_scr[hh] = lax.dot_general(dy_b[:, hs], xdt_b[:, hs], NT, preferred_element_type=F32)
            dG = jnp.zeros((Q, Q), F32)
            for hh in range(HEADS_PER_GROUP):
                L = _decay(la, laT, g * HEADS_PER_GROUP + hh, causal)
                M = cb * L
                dM = dm_scr[hh]
                dG = dG + dM * L
                W = dM * M
                rows.append(jnp.sum(W.T, axis=0, keepdims=True) - jnp.sum(W, axis=0, keepdims=True))
                m_scr[hh] = M.astype(BF16)
            dxs = []
            for hh in range(HEADS_PER_GROUP):
                hs = slice((g * HEADS_PER_GROUP + hh) * HEAD_DIM, (g * HEADS_PER_GROUP + hh + 1) * HEAD_DIM)
                dxs.append(lax.dot_general(m_scr[hh], dy_b[:, hs], TN, preferred_element_type=F32))
            dG_b = dG.astype(BF16)
            dC = dC + jnp.dot(dG_b, Bg, preferred_element_type=F32)
            dB = dB + lax.dot_general(dG_b, Cg, TN, preferred_element_type=F32)
            dxdt = jnp.concatenate(dxs, axis=1) + dxdt_state
            vbuf[:, gs] = dxdt * x[:, gs]
            dact_ref[:, gs] = dxdt * dt_e[:, gs] + dyv[:, gs] * dsk_ref[:, gs]
            dact_ref[:, D_INNER + g * D_STATE:D_INNER + (g + 1) * D_STATE] = dB
            dact_ref[:, D_INNER + G * D_STATE + g * D_STATE:D_INNER + G * D_STATE + (g + 1) * D_STATE] = dC
        e = e_ref[...]
        w = _dotx_r(wbuf[...], e, parts=2)
        u = _dotx_r(ubuf[...], e, parts=2)
        vx = _dotx_r(vbuf[...], e, parts=2)
        dsk = _dotx_r(jnp.broadcast_to(jnp.sum(dyv * x, axis=0, keepdims=True), (8, D_INNER)), e, parts=2)[0:1]
        s0 =_dotx_r(sbuf[...], e, parts=2)[0:1] * jnp.exp(la[Q - 1:Q, :])
        ddelta = _dotx_l(triu_ref[...], w) + _dotx_l(trils_ref[...], u) + s0
        ddt_ref[...] = ddelta * a_row + vx
        ddeltaT = _dotx_r(jnp.concatenate(rows, axis=0), tril_ref[...])
        ddtT_ref[...] = ddeltaT * a_col
        da_ref[...] += jnp.sum(ddelta * dtv, axis=0, keepdims=True)
        daT_ref[...] += jnp.sum(ddeltaT * dtTv, axis=1, keepdims=True)
        dsk_out_ref[...] += dsk

    rev = lambda c: nc - 1 - c
    full = lambda a: pl.BlockSpec(a.shape, lambda c: (0,) * a.ndim)
    al, alT = alog.reshape(1, -1), alog.reshape(-1, 1)
    consts = [cs[k] for k in ("tril", "triu", "trils", "trius", "et", "e")]
    res, comm_res = _carrier_call(
        body, nc,
        [pl.BlockSpec((Q, D_INNER), lambda c: (rev(c), 0)), pl.BlockSpec((Q, XBC_WIDTH), lambda c: (rev(c), 0)),
         pl.BlockSpec((Q, H), lambda c: (rev(c), 0)), pl.BlockSpec((H, Q), lambda c: (0, rev(c))),
         full(al), full(alT), full(dskip_e), pl.BlockSpec((1, G, D_STATE, GW), lambda c: (rev(c), 0, 0, 0))]
        + [full(a) for a in consts],
        [pl.BlockSpec((Q, XBC_WIDTH), lambda c: (rev(c), 0)), pl.BlockSpec((Q, H), lambda c: (rev(c), 0)),
         pl.BlockSpec((H, Q), lambda c: (0, rev(c))), pl.BlockSpec((1, H), lambda c: (0, 0)),
         pl.BlockSpec((H, 1), lambda c: (0, 0)), pl.BlockSpec((1, H), lambda c: (0, 0))],
        [S((T, XBC_WIDTH), F32), S((T, H), F32), S((H, T), F32), S((1, H), F32), S((H, 1), F32), S((1, H), F32)],
        [pltpu.VMEM((G, D_STATE, GW), F32), pltpu.VMEM((Q, D_INNER), F32), pltpu.VMEM((Q, D_INNER), F32),
         pltpu.VMEM((Q, D_INNER), F32), pltpu.VMEM((8, D_INNER), F32),
         pltpu.VMEM((HEADS_PER_GROUP, Q, Q), F32), pltpu.VMEM((HEADS_PER_GROUP, Q, Q), BF16)],
        [dy, act, dt, dtT, al, alT, dskip_e, states] + consts, comm, name)
    return res if comm is None else (res, comm_res)


def _a_log_grad(da, daT_row, alog, name):
    def body(a_ref, b_ref, al_ref, o_ref):
        o_ref[...] = (a_ref[...] + b_ref[...]) * (-jnp.exp(al_ref[...]))
    return pl.pallas_call(body, out_shape=S((1, N_SSM_HEADS), F32), name=name)(da, daT_row, alog.reshape(1, -1))


def _place():
    x, y, c = lax.axis_index("x"), lax.axis_index("y"), lax.axis_index("c")
    return x, y, c, [(1 - x, y), (x, 1 - y), (1 - x, 1 - y)]


def _all_gather(shards, name):
    npc = len(shards)

    def body(*refs):
        x_refs, o_refs = refs[:npc], refs[npc:2 * npc]
        send_sems, recv_sems, local_sems = refs[2 * npc:]
        x, y, c, chips = _place()
        me, sib = (x, y, c), (x, y, 1 - c)

        def rows(dev, i):
            return o_refs[i].at[4 * dev[0] + 2 * dev[1] + dev[2]]

        def copy(k, i, block, to, src=None):
            return pltpu.make_async_remote_copy(
                src_ref=rows(block, i) if src is None else src, dst_ref=rows(block, i),
                send_sem=send_sems.at[k, i], recv_sem=recv_sems.at[k, i], device_id=to, device_id_type=MESH)

        mine = [pltpu.make_async_copy(x_refs[i], rows(me, i), local_sems.at[i]) for i in range(npc)]
        for cp in mine:
            cp.start()
        first = []
        for i in range(npc):
            first.append(copy(0, i, me, sib, src=x_refs[i]))
            first += [copy(1 + j, i, me, (*chip, c), src=x_refs[i]) for j, chip in enumerate(chips)]
        for cp in first:
            cp.start()
        passed = []
        for i in range(npc):
            for j, chip in enumerate(chips):
                copy(1 + j, i, (*chip, c), me).wait_recv()
                cp = copy(4 + j, i, (*chip, c), sib)
                cp.start()
                passed.append(cp)
        for i in range(npc):
            copy(0, i, sib, me).wait_recv()
            for j, chip in enumerate(chips):
                copy(4 + j, i, (*chip, 1 - c), me).wait_recv()
        for cp in first + passed:
            cp.wait_send()
        for cp in mine:
            cp.wait()

    anys = pl.BlockSpec(memory_space=pl.ANY)
    return pl.pallas_call(
        body, in_specs=[anys] * npc, out_specs=[anys] * npc, out_shape=[S((N_DEV,) + s.shape, s.dtype) for s in shards],
        scratch_shapes=[pltpu.SemaphoreType.DMA((7, npc)), pltpu.SemaphoreType.DMA((7, npc)), pltpu.SemaphoreType.DMA((npc,))],
        name=name)(*shards)


def _to_sibling(to_sib, name):
    npc = len(to_sib)

    def body(*refs):
        s_refs, o_refs, send_sems, recv_sems = refs[:npc], refs[npc:2 * npc], refs[2 * npc], refs[2 * npc + 1]
        x, y, c, _ = _place()
        cps = [pltpu.make_async_remote_copy(
            src_ref=s_refs[i], dst_ref=o_refs[i], send_sem=send_sems.at[i], recv_sem=recv_sems.at[i],
            device_id=(x, y, 1 - c), device_id_type=MESH) for i in range(npc)]
        for cp in cps:
            cp.start()
        for cp in cps:
            cp.wait()

    anys = pl.BlockSpec(memory_space=pl.ANY)
    return pl.pallas_call(
        body, in_specs=[anys] * npc, out_specs=[anys] * npc, out_shape=[S(s.shape, s.dtype) for s in to_sib],
        scratch_shapes=[pltpu.SemaphoreType.DMA((npc,)), pltpu.SemaphoreType.DMA((npc,))],
        name=name)(*to_sib)


def _to_chips(parts, name):
    npc = len(parts)

    def body(*refs):
        p_refs, o_refs, send_sems, recv_sems = refs[:npc], refs[npc:2 * npc], refs[2 * npc], refs[2 * npc + 1]
        x, y, c, chips = _place()
        cps = [pltpu.make_async_remote_copy(
            src_ref=p_refs[i].at[2 * chip[0] + chip[1]], dst_ref=o_refs[i].at[j], send_sem=send_sems.at[j, i],
            recv_sem=recv_sems.at[j, i], device_id=(*chip, c), device_id_type=MESH)
            for i in range(npc) for j, chip in enumerate(chips)]
        for cp in cps:
            cp.start()
        for cp in cps:
            cp.wait()

    anys = pl.BlockSpec(memory_space=pl.ANY)
    return pl.pallas_call(
        body, in_specs=[anys] * npc, out_specs=[anys] * npc, out_shape=[S((3,) + p.shape[1:], p.dtype) for p in parts],
        scratch_shapes=[pltpu.SemaphoreType.DMA((3, npc)), pltpu.SemaphoreType.DMA((3, npc))],
        name=name)(*parts)


def _rs_begin(pieces, name):
    c = lax.axis_index("c")
    by_core = [p.reshape(4, 2, p.shape[1], p.shape[2]) for p in pieces]
    to_sib = [lax.dynamic_index_in_dim(p, 1 - c, axis=1, keepdims=False).astype(BF16) for p in by_core]
    keep = [lax.dynamic_index_in_dim(p, c, axis=1, keepdims=False) for p in by_core]
    from_sib = _to_sibling(to_sib, name + "_d2d")

    def add1(a, b):
        s = a + b
        return [s, s], []

    parts, parts_b = [], []
    for i, (k, f) in enumerate(zip(keep, from_sib)):
        _, r, C = k.shape
        p, pb = _rowwise(add1, [k.reshape(4 * r, C), f.reshape(4 * r, C)], [], [(C, F32), (C, BF16)], tm=2048, name=f"{name}_add1_{i}")
        parts.append(p.reshape(4, r, C))
        parts_b.append(pb.reshape(4, r, C))
    return parts, parts_b


def _rs_finish(parts, got, name):
    x, y = lax.axis_index("x"), lax.axis_index("y")

    def add2(a, b, c_, d_):
        return [((a + b) + c_) + d_], []

    outs = []
    for i, (p, g) in enumerate(zip(parts, got)):
        own = lax.dynamic_index_in_dim(p, 2 * x + y, axis=0, keepdims=False)
        outs.append(_rowwise(add2, [own, g[0], g[1], g[2]], [], [(p.shape[2], F32)], tm=2048, name=f"{name}_add2_{i}")[0])
    return outs


def _reduce_scatter(pieces, name):
    parts, parts_b = _rs_begin(pieces, name)
    return _rs_finish(parts, _to_chips(parts_b, name + "_ici"), name)


class _GradReduce:
    EARLY = ("w_ffn_in", "w_ffn_out", "w_out", "w_attn_branch", "w_ssm_branch")

    def __init__(self):
        self.out, self.keys, self.parts, self.parts_b = {}, [], [], []

    def _begin(self, l, names, gr, name):
        parts, parts_b = _rs_begin([_shard(nm, gr[nm]) for nm in names], name)
        self.keys += [(l, nm) for nm in names]
        self.parts += parts
        self.parts_b += parts_b

    def carry_fn(self, l):
        if l != 0:
            return None

        def fn(gr):
            self._begin(0, self.EARLY, gr, "rs_early_l0")
            return _comm_to_chips(self.parts_b)
        return fn

    def carry2_fn(self, l):
        if l != 0:
            return None

        def fn(gr):
            self.parts2, parts2_b = _rs_begin([_shard("w_in", gr["w_in"])], "rs_w_in_l0")
            return _comm_to_chips(parts2_b)
        return fn

    def done(self, l, gr, carried):
        if l == DEPTH - 1:
            self._begin(l, BIG, gr, f"rs_l{l}")
            return
        for key, o in zip(self.keys, _rs_finish(self.parts, carried[0], "rs_carried")):
            self.out[key] = o
        self.out[(0, "w_in")] = _rs_finish(self.parts2, carried[1], "rs_w_in_l0")[0]


def _all_reduce_small(v, name):
    R, C = v.shape

    def body(x_ref, out_ref, buf, send_sems, recv_sems):
        x, y, c, chips = _place()
        me, sib = (x, y, c), (x, y, 1 - c)

        def rows(dev):
            return buf.at[4 * dev[0] + 2 * dev[1] + dev[2]]

        def copy(k, block, to, src=None):
            return pltpu.make_async_remote_copy(
                src_ref=rows(block) if src is None else src, dst_ref=rows(block),
                send_sem=send_sems.at[k], recv_sem=recv_sems.at[k], device_id=to, device_id_type=MESH)

        buf[4 * x + 2 * y + c] = x_ref[...]
        first = [copy(0, me, sib, src=x_ref)] + [copy(1 + j, me, (*chip, c), src=x_ref) for j, chip in enumerate(chips)]
        for cp in first:
            cp.start()
        passed = [copy(4 + j, (*chip, c), sib) for j, chip in enumerate(chips)]
        for j, chip in enumerate(chips):
            copy(1 + j, (*chip, c), me).wait_recv()
            passed[j].start()
        copy(0, sib, me).wait_recv()
        for j, chip in enumerate(chips):
            copy(4 + j, (*chip, 1 - c), me).wait_recv()
        for cp in first + passed:
            cp.wait_send()
        acc = buf[0]
        for j in range(1, N_DEV):
            acc = acc + buf[j]
        out_ref[...] = acc

    vm = pl.BlockSpec(memory_space=pltpu.VMEM)
    return pl.pallas_call(
        body, in_specs=[vm], out_specs=vm, out_shape=S((R, C), F32),
        scratch_shapes=[pltpu.VMEM((N_DEV, R, C), F32), pltpu.SemaphoreType.DMA((7,)), pltpu.SemaphoreType.DMA((7,))],
        compiler_params=pltpu.CompilerParams(vmem_limit_bytes=VMEM_LIMIT), name=name)(v)


SEG = (("q", 0, 1536), ("k", 1536, 1536), ("v", 3072, 1536), ("z", 4608, 2048), ("xbc", 6656, 3072), ("dt", 9728, 32), ("gl", 9760, 2048))


def _split_w_in(w_in_full):
    out = {}
    for nm, off, n in SEG:
        w = w_in_full[:, off:off + n]
        if nm == "dt":
            w = jnp.pad(w, ((0, 0), (0, LANES - n)))
        out[nm] = w
    W = ATTN_OUT_WIDTH
    out["qkv"] = [jnp.concatenate([out[s][:, g * W:(g + 1) * W] for s in ("q", "k", "v")], axis=1) for g in range(N_DIL)]
    out["qkv_t"] = [[out[s][:, g * W:(g + 1) * W] for s in ("q", "k", "v")] for g in range(N_DIL)]
    return out


def _layer_fwd(h, p, W, biases, cs, l, carry=None, late=None):
    T = h.shape[0]
    nm = lambda s: f"{s}_l{l}"
    sv = {"h_in": h}
    xns = _rmsnorm_fwd(h, p["norm1_w"], nm("norm1"), dils=[d for _, d in DILATED_GROUPS[1:]])
    xn = xns[0]
    wi = W["w_in"]
    z = _mm(xn, wi["z"], out_dtype=BF16, name=nm("proj_z"))
    xbc = _mm(xn, wi["xbc"], name=nm("proj_xbc"))
    dt_raw = _mm(xn, wi["dt"], name=nm("proj_dt"))
    gl = _mm(xn, wi["gl"], out_dtype=BF16, name=nm("proj_gl"))
    conv_args = (xbc, p["conv_w"], p["conv_b"].reshape(1, -1), nm("conv"))
    if late is None:
        u_conv, act = _conv_fwd(*conv_args)
    else:
        (u_conv, act), spread = _conv_fwd(*conv_args, comm=late[0])
    os_, ls, qkvs = [], [], []
    for g, (window, dil) in enumerate(DILATED_GROUPS):
        qkv = _mm(xns[g], wi["qkv"][g], out_dtype=BF16, name=nm(f"proj_qkv_g{g}"))
        if late is not None and g == 0:
            (o, lse), arrived = _attn_fwd(qkv, biases[g], dil, nm(f"attn_fwd_g{g}"), comm=_comm_gather_pass(spread))
            W.update(late[1](arrived))
        else:
            o, lse = _attn_fwd(qkv, biases[g], dil, nm(f"attn_fwd_g{g}"))
        os_.append(o)
        ls.append(lse)
        qkvs.append(qkv)
    attn_b, attn_f = _combine_fwd(os_, ls, nm("combine"))
    dt = _dt_fwd(dt_raw, p["dt_bias"], nm("dt"))
    dtT = dt.T
    dskip_e = jnp.repeat(p["d_skip"], D_INNER // N_SSM_HEADS).reshape(1, -1)
    carried = None
    if carry is None:
        y, states = _ssd_fwd(act, dt, dtT, p["a_log"], dskip_e, cs, nm("ssd_fwd"))
        ssm = _ssm_norm_fwd(y, z, p["ssm_norm_w"], nm("ssm_norm"))
    else:
        (y, states), spread = _ssd_fwd(act, dt, dtT, p["a_log"], dskip_e, cs, nm("ssd_fwd"), comm=carry)
        ssm, carried = _ssm_norm_fwd(y, z, p["ssm_norm_w"], nm("ssm_norm"), comm=_comm_gather_pass(spread))
    a_br = _mm(attn_b, W["w_attn_branch"], out_dtype=BF16, name=nm("attn_branch"))
    s_br = _mm(ssm, W["w_ssm_branch"], out_dtype=BF16, name=nm("ssm_branch"))
    h_mid, merged = _gate_out_proj(a_br, s_br, gl, h, W["w_out"], nm("gate_out_proj"))
    xn2 = _rmsnorm_fwd(h_mid, p["norm2_w"], nm("norm2"))[0]
    W["w_ffn_in_p"] = _ffn_perm(W["w_ffn_in"])
    u_ffn, ffn_act = _mm(xn2, W["w_ffn_in_p"], tm=512, tn=D_FF, epilogue=_swiglu_epilogue, outs=[(2 * D_FF, BF16), (D_FF, BF16)],
                         name=nm("ffn_in_swiglu"))
    h_out = _mm(ffn_act, W["w_ffn_out"], acc=h_mid, name=nm("ffn_out"))
    sv.update(xn=xn, xns=xns, qkvs=qkvs, z=z, xbc=xbc, dt_raw=dt_raw, gl=gl, ls=ls, attn_b=attn_b, attn_f=attn_f, u_conv=u_conv,
              act=act, dt=dt, dtT=dtT, dskip_e=dskip_e, y=y, states=states, ssm=ssm, a_br=a_br, s_br=s_br, merged=merged,
              h_mid=h_mid, xn2=xn2, u_ffn=u_ffn, ffn_act=ffn_act)
    return h_out, sv, carried


def _layer_bwd(dh, sv, p, W, biases, cs, head_ones, l, carry_fn=None, carry2_fn=None):
    T = dh.shape[0]
    nm = lambda s: f"{s}_l{l}"
    gr = {}
    du = _mm(dh, W["w_ffn_out"], tb=True, tm=512, tn=FFN_HALF, extras=[sv["u_ffn"]], epilogue=_dswiglu_epilogue, outs=[(2 * D_FF, BF16)],
             name=nm("d_ffn_act_swiglu"))
    gr["w_ffn_out"] = _mm(sv["ffn_act"], dh, ta=True, name=nm("g_ffn_out"))
    dxn2 = _mm(du, W["w_ffn_in_p"], tb=True, name=nm("d_xn2"))
    gr["w_ffn_in"] = _ffn_unperm(_mm(sv["xn2"], du, ta=True, name=nm("g_ffn_in")))
    dh_mid, gr["norm2_w"] = _rmsnorm_bwd(dxn2, sv["h_mid"], p["norm2_w"], dh, nm("d_norm2"))
    gr["w_out"] = _mm(sv["merged"], dh_mid, ta=True, name=nm("g_out"))
    d_a, d_s, dgl = _d_out_proj_gate(dh_mid, sv["a_br"], sv["s_br"], sv["gl"], W["w_out"], nm("d_out_proj_gate"))
    dattn = _mm(d_a, W["w_attn_branch"], tb=True, out_dtype=BF16, name=nm("d_attn"))
    gr["w_attn_branch"] = _mm(sv["attn_b"], d_a, ta=True, name=nm("g_attn_branch"))
    dssm = _mm(d_s, W["w_ssm_branch"], tb=True, out_dtype=BF16, name=nm("d_ssm"))
    gr["w_ssm_branch"] = _mm(sv["ssm"], d_s, ta=True, name=nm("g_ssm_branch"))
    dy, dz, gr["ssm_norm_w"] = _ssm_norm_bwd(dssm, sv["y"], sv["z"], p["ssm_norm_w"], nm("d_ssm_norm"))
    ssd_args = (dy, sv["act"], sv["dt"], sv["dtT"], p["a_log"], sv["dskip_e"], sv["states"], cs, nm("ssd_bwd"))
    carried = None
    if carry_fn is None:
        dact_c, ddt_a, ddt_bT, da, daT, dskip = _ssd_bwd(*ssd_args)
    else:
        (dact_c, ddt_a, ddt_bT, da, daT, dskip), carried = _ssd_bwd(*ssd_args, comm=carry_fn(gr))
    gr["a_log"] = _a_log_grad(da, daT.T, p["a_log"], nm("g_a_log")).reshape(-1)
    gr["d_skip"] = dskip.reshape(-1)
    ddt_raw, ddt_bias = _dt_bwd(ddt_a, ddt_bT.T, sv["dt_raw"], p["dt_bias"], nm("d_dt"))
    gr["dt_bias"] = ddt_bias.reshape(-1)
    dxbc, gr["conv_w"], dconv_b = _conv_bwd(dact_c, sv["u_conv"], sv["xbc"], p["conv_w"], nm("d_conv"))
    gr["conv_b"] = dconv_b.reshape(-1)
    outs = _combine_bwd(dattn, sv["attn_f"], sv["ls"], head_ones, nm("d_combine"))
    wi = W["w_in"]
    dbias, dxn = [], None
    gqkv = [[None] * N_DIL for _ in range(3)]
    for g, (window, dil) in enumerate(DILATED_GROUPS):
        dq, dk, dv, db = _attn_bwd(sv["qkvs"][g], biases[g], sv["ls"][g], outs[2 * g], outs[2 * g + 1], dil, nm(f"attn_bwd_g{g}"))
        dbias.append(db)
        dxn = _mm_dil([dq, dk, dv], wi["qkv_t"][g], dil, dxn, nm(f"d_xn_qkv_g{g}"))
        for i, dseg in enumerate((dq, dk, dv)):
            gqkv[i][g] = _mm(sv["xns"][g], dseg, ta=True, name=nm(f"g_in_{'qkv'[i]}_g{g}"))
    parts = (("z", dz), ("xbc", dxbc), ("dt", ddt_raw), ("gl", dgl))
    gws = gqkv[0] + gqkv[1] + gqkv[2]
    for sname, dseg in parts:
        gw = _mm(sv["xn"], dseg, ta=True, name=nm("g_in_" + sname))
        gws.append(gw[:, :N_SSM_HEADS] if sname == "dt" else gw)
    gr["w_in"] = jnp.concatenate(gws, axis=1)
    carried2 = None
    for sname, dseg in parts:
        if carry2_fn is not None and sname == "xbc":
            dxn, carried2 = _mm(dseg, wi[sname], tb=True, acc=dxn, name=nm("d_xn_" + sname), comm=carry2_fn(gr))
        else:
            dxn = _mm(dseg, wi[sname], tb=True, acc=dxn, name=nm("d_xn_" + sname))
    dh_in, gr["norm1_w"] = _rmsnorm_bwd(dxn, sv["h_in"], p["norm1_w"], dh_mid, nm("d_norm1"))
    return dh_in, gr, dbias, (carried, carried2)


def _step_local(x, tgt, small, Wfull, rel_bias, final_norm_w, prefetch=None, grad_reduce=None, late0=None):
    cs = _ssd_consts()
    head = np.repeat(np.arange(HEADS_PER_GROUP), HEAD_DIM)
    head_ones = jnp.asarray(head[:, None] == head[None, :], BF16)
    biases, onehots = [], []
    for g, (window, dil) in enumerate(DILATED_GROUPS):
        onehot, valid = _bias_consts(dil, window // dil)
        rel_g_t = rel_bias[:, g * HEADS_PER_GROUP:(g + 1) * HEADS_PER_GROUP].T
        b = _bias_gather(rel_g_t, onehot, valid, f"bias_gather_g{g}")
        biases.append(b.reshape(HEADS_PER_GROUP, ATTN_BLOCK, 2 * ATTN_BLOCK))
        onehots.append(onehot)
    h, saved, carried = x, [], None
    Wfull = list(Wfull)
    for l in range(DEPTH):
        W = dict(prefetch[1](carried) if Wfull[l] is None else Wfull[l])
        W["w_in"] = _split_w_in(W["w_in"])
        Wfull[l] = W
        first = prefetch is not None and l == 0
        h, sv, carried = _layer_fwd(h, small[l], W, biases, cs, l, prefetch[0] if first else None, late0 if l == 0 else None)
        saved.append(sv)
    dh, g_final, loss = _loss_head(h, final_norm_w, tgt, "loss_head")
    grads = [None] * DEPTH
    dbias_tot = [None] * N_DIL
    for l in reversed(range(DEPTH)):
        carry_fn = grad_reduce.carry_fn(l) if grad_reduce is not None else None
        carry2_fn = grad_reduce.carry2_fn(l) if grad_reduce is not None else None
        dh, grads[l], dbias, carried = _layer_bwd(dh, saved[l], small[l], Wfull[l], biases, cs, head_ones, l, carry_fn, carry2_fn)
        if grad_reduce is not None:
            grad_reduce.done(l, grads[l], carried)
        for g in range(N_DIL):
            dbias_tot[g] = dbias[g] if dbias_tot[g] is None else dbias_tot[g] + dbias[g]
    d_rel = jnp.concatenate(
        [_bias_scatter(dbias_tot[g].reshape(HEADS_PER_GROUP, -1), onehots[g], f"bias_scatter_g{g}").T for g in range(N_DIL)], axis=1)
    return loss, dh, grads, d_rel, g_final


def _unshard(nm, g):
    _, rows, cols = g.shape
    if nm in COL_SHARDED:
        return g.transpose(1, 0, 2).reshape(rows, N_DEV * cols)
    return g.reshape(N_DEV * rows, cols)


def _shard(nm, w):
    rows, cols = w.shape
    if nm in COL_SHARDED:
        return w.reshape(rows, N_DEV, cols // N_DEV).transpose(1, 0, 2)
    return w.reshape(N_DEV, rows // N_DEV, cols)


SMALL_LAYER = (("norm1_w", 1024), ("conv_w", 12288), ("conv_b", 3072), ("dt_bias", 32), ("a_log", 32), ("d_skip", 32),
               ("ssm_norm_w", 2048), ("norm2_w", 1024))
SMALL_GLOBAL = (("rel_bias", 768), ("final_norm_w", 1024), ("loss", 1))


def _pad128(v):
    n = v.shape[0]
    return jnp.pad(v, (0, -n % LANES))


def _pack_small(per_layer, glob):
    parts = [_pad128(per_layer[l][nm].reshape(-1)) for l in range(DEPTH) for nm, _ in SMALL_LAYER]
    parts += [_pad128(glob[nm].reshape(-1)) for nm, _ in SMALL_GLOBAL]
    flat = jnp.concatenate(parts)
    flat = jnp.pad(flat, (0, -flat.shape[0] % (8 * LANES)))
    return flat.reshape(-1, LANES)


def _unpack_small(packed):
    flat = packed.reshape(-1)
    per_layer, glob, off = [dict() for _ in range(DEPTH)], {}, 0
    for l in range(DEPTH):
        for nm, n in SMALL_LAYER:
            per_layer[l][nm] = flat[off:off + n]
            off += n + (-n % LANES)
    for nm, n in SMALL_GLOBAL:
        glob[nm] = flat[off:off + n]
        off += n + (-n % LANES)
    return per_layer, glob


def kernel(x, norm1_w, w_in, conv_w, conv_b, dt_bias, a_log, d_skip, ssm_norm_w, w_attn_branch, w_ssm_branch, w_out, norm2_w, w_ffn_in, w_ffn_out, rel_bias, final_norm_w, loss_target, m_norm1_w, m_w_in, m_conv_w, m_conv_b, m_dt_bias, m_a_log, m_d_skip, m_ssm_norm_w, m_w_attn_branch, m_w_ssm_branch, m_w_out, m_norm2_w, m_w_ffn_in, m_w_ffn_out, m_rel_bias, m_final_norm_w, v_norm1_w, v_w_in, v_conv_w, v_conv_b, v_dt_bias, v_a_log, v_d_skip, v_ssm_norm_w, v_w_attn_branch, v_w_ssm_branch, v_w_out, v_norm2_w, v_w_ffn_in, v_w_ffn_out, v_rel_bias, v_final_norm_w):
    big = dict(w_in=w_in, w_attn_branch=w_attn_branch, w_ssm_branch=w_ssm_branch, w_out=w_out, w_ffn_in=w_ffn_in, w_ffn_out=w_ffn_out)
    big_m = dict(w_in=m_w_in, w_attn_branch=m_w_attn_branch, w_ssm_branch=m_w_ssm_branch, w_out=m_w_out, w_ffn_in=m_w_ffn_in, w_ffn_out=m_w_ffn_out)
    big_v = dict(w_in=v_w_in, w_attn_branch=v_w_attn_branch, w_ssm_branch=v_w_ssm_branch, w_out=v_w_out, w_ffn_in=v_w_ffn_in, w_ffn_out=v_w_ffn_out)
    sm = dict(norm1_w=norm1_w, conv_w=conv_w, conv_b=conv_b, dt_bias=dt_bias, a_log=a_log, d_skip=d_skip, ssm_norm_w=ssm_norm_w, norm2_w=norm2_w)
    sm_m = dict(norm1_w=m_norm1_w, conv_w=m_conv_w, conv_b=m_conv_b, dt_bias=m_dt_bias, a_log=m_a_log, d_skip=m_d_skip, ssm_norm_w=m_ssm_norm_w, norm2_w=m_norm2_w)
    sm_v = dict(norm1_w=v_norm1_w, conv_w=v_conv_w, conv_b=v_conv_b, dt_bias=v_dt_bias, a_log=v_a_log, d_skip=v_d_skip, ssm_norm_w=v_ssm_norm_w, norm2_w=v_norm2_w)
    me = 4 * lax.axis_index("x") + 2 * lax.axis_index("y") + lax.axis_index("c")

    def full_weights(gathered, names=BIG):
        return {nm: _unshard(nm, g) for nm, g in zip(names, gathered)}

    later = [nm for nm in BIG if nm != "w_in"]
    Wfull = [full_weights(_all_gather([big["w_in"][0].astype(BF16)], "all_gather_w_in_l0"), ["w_in"]), None]
    late0 = (_comm_gather_spread([big[nm][0].astype(BF16) for nm in later]), functools.partial(full_weights, names=later))
    prefetch = (_comm_gather_spread([big[nm][DEPTH - 1].astype(BF16) for nm in BIG]), full_weights)

    conv_full = []
    for l in range(DEPTH):
        z = jnp.zeros((N_DEV, CONV_WIDTH, XBC_WIDTH // N_DEV), F32)
        conv_full.append(lax.dynamic_update_index_in_dim(z, conv_w[l], me, axis=0))
    cw = jnp.stack(conv_full).reshape(-1, LANES)
    cw = _all_reduce_small(cw, "gather_conv_w").reshape(DEPTH, N_DEV, CONV_WIDTH, XBC_WIDTH // N_DEV)
    cw = cw.transpose(0, 2, 1, 3).reshape(DEPTH, CONV_WIDTH, XBC_WIDTH)

    small = [{nm: (cw[l] if nm == "conv_w" else a[l]) for nm, a in sm.items()} for l in range(DEPTH)]
    grad_reduce = _GradReduce()
    loss, dx, grads, d_rel, g_final = _step_local(x[0], loss_target[0], small, Wfull, rel_bias, final_norm_w, prefetch, grad_reduce, late0)
    g_big = {nm: jnp.stack([grad_reduce.out[(l, nm)] for l in range(DEPTH)]) for nm in BIG}

    per_layer = [{nm: grads[l][nm] for nm, _ in SMALL_LAYER} for l in range(DEPTH)]
    packet = _pack_small(per_layer, dict(rel_bias=d_rel, final_norm_w=g_final, loss=loss[0, :1]))
    per_layer, glob = _unpack_small(_all_reduce_small(packet, "all_reduce_small"))
    g_small = {nm: jnp.stack([per_layer[l][nm] for l in range(DEPTH)]) for nm, _ in SMALL_LAYER}
    cwg = g_small["conv_w"].reshape(DEPTH, CONV_WIDTH, N_DEV, XBC_WIDTH // N_DEV)
    g_small["conv_w"] = lax.dynamic_index_in_dim(cwg, me, axis=2, keepdims=False)
    for nm in sm:
        g_small[nm] = g_small[nm].reshape(sm[nm].shape)
    g_rel = glob["rel_bias"].reshape(rel_bias.shape)
    g_fin = glob["final_norm_w"]
    loss_out = glob["loss"][0]

    def adam(w, g, m, v, name):
        shp = w.shape
        two = lambda a: a.reshape(-1, shp[-1]) if a.ndim > 1 else a.reshape(1, -1)
        d, nm_, nv = _adamw(two(w), two(g), two(m), two(v), name)
        return d.reshape(shp), nm_.reshape(shp), nv.reshape(shp)

    order = ["norm1_w", "w_in", "conv_w", "conv_b", "dt_bias", "a_log", "d_skip", "ssm_norm_w", "w_attn_branch", "w_ssm_branch",
             "w_out", "norm2_w", "w_ffn_in", "w_ffn_out", "rel_bias", "final_norm_w"]
    allw = {**big, **sm, "rel_bias": rel_bias, "final_norm_w": final_norm_w}
    allm = {**big_m, **sm_m, "rel_bias": m_rel_bias, "final_norm_w": m_final_norm_w}
    allv = {**big_v, **sm_v, "rel_bias": v_rel_bias, "final_norm_w": v_final_norm_w}
    allg = {**g_big, **g_small, "rel_bias": g_rel, "final_norm_w": g_fin}
    deltas, new_m, new_v = [], [], []
    for nm in order:
        d, a, b = adam(allw[nm], allg[nm], allm[nm], allv[nm], "adamw_" + nm)
        deltas.append(d)
        new_m.append(a)
        new_v.append(b)
    return (loss_out, dx[None], *[allg[nm] for nm in order], *deltas, *new_m, *new_v)
```

```python
import functools
import math

import numpy as np
import jax
import jax.numpy as jnp
from jax import lax
from jax.experimental import pallas as pl
from jax.experimental.pallas import tpu as pltpu

F32, BF16 = jnp.float32, jnp.bfloat16
S = jax.ShapeDtypeStruct
MESH = pl.DeviceIdType.MESH

D_MODEL = 1024
DEPTH = 2
HEAD_DIM = 64
DILATED_GROUPS = ((128, 1), (512, 4), (2048, 16))
N_DIL = 3
HEADS_PER_GROUP = 8
ATTN_WIDTH = 1536
ATTN_OUT_WIDTH = 512
ATTN_BLOCK = 128
N_REL_BUCKETS = 32
REL_MAX_DISTANCE = 2048
D_INNER = 2048
N_SSM_HEADS = 32
N_SSM_GROUPS = 4
D_STATE = 128
CONV_WIDTH = 4
SSD_CHUNK = 128
XBC_WIDTH = 3072
D_FF = 2816
EPS = 1e-6
ADAM_LR, ADAM_B1, ADAM_B2, ADAM_EPS, ADAM_WD, ADAM_STEP = 0.001, 0.9, 0.999, 1e-08, 0.01, 10

N_DEV = 8
LANES = 128
VMEM_LIMIT = 56 * 1024 * 1024
MM_VMEM_BYTES = 40 * 1024 * 1024
ROW_TILES_BYTES = 36 * 1024 * 1024
NEG = -1e30
BIG = ("w_in", "w_attn_branch", "w_ssm_branch", "w_out", "w_ffn_in", "w_ffn_out")
COL_SHARDED = ("w_in", "w_attn_branch", "w_ffn_in")

NT = (((1,), (1,)), ((), ()))
TN = (((0,), (0,)), ((), ()))


def _cparams(sem=None):
    return pltpu.CompilerParams(dimension_semantics=sem, vmem_limit_bytes=VMEM_LIMIT)


def _pick(n, target, mult=LANES):
    best = None
    for t in range(mult, min(n, target) + 1, mult):
        if n % t == 0:
            best = t
    return best or n


def _silu(x):
    return x * jax.nn.sigmoid(x)


def _dsilu(x):
    s = jax.nn.sigmoid(x)
    return s * (1.0 + x * (1.0 - s))


def _split2(x):
    hi = x.astype(BF16)
    lo = (x - hi.astype(F32)).astype(BF16)
    return hi, lo


def _split3(x):
    x1 = x.astype(BF16)
    r1 = x - x1.astype(F32)
    x2 = r1.astype(BF16)
    x3 = (r1 - x2.astype(F32)).astype(BF16)
    return x1, x2, x3


def _dotx_r(x, m, parts=3):
    xs = _split3(x) if parts == 3 else _split2(x)
    out = jnp.dot(xs[0], m, preferred_element_type=F32)
    for xi in xs[1:]:
        out = out + jnp.dot(xi, m, preferred_element_type=F32)
    return out


def _dotx_l(m, x, parts=3):
    xs = _split3(x) if parts == 3 else _split2(x)
    out = jnp.dot(m, xs[0], preferred_element_type=F32)
    for xi in xs[1:]:
        out = out + jnp.dot(m, xi, preferred_element_type=F32)
    return out


def _mm(a, b, *, ta=False, tb=False, out_dtype=F32, acc=None, name, tm=None, tn=1536, tk=1536, extras=(), epilogue=None, outs=None, comm=None):
    M, K = (a.shape[1], a.shape[0]) if ta else a.shape
    N = b.shape[0] if tb else b.shape[1]
    tn, tk = _pick(N, tn), _pick(K, tk)
    nk = K // tk
    if tm is None:
        def vmem(t):
            out_b = jnp.dtype(out_dtype).itemsize
            return (2 * t * tk * a.dtype.itemsize + 2 * tk * tn * b.dtype.itemsize + 2 * t * tn * out_b
                    + (t * tn * 4 if nk > 1 else 0) + (2 * t * tn * acc.dtype.itemsize if acc is not None else 0))
        tm = _pick(M, 1536)
        while M % (2 * tm) == 0 and vmem(2 * tm) <= MM_VMEM_BYTES:
            tm *= 2
    else:
        tm = _pick(M, tm)
    dims = (((0 if ta else 1,), (1 if tb else 0,)), ((), ()))
    has_acc = acc is not None
    outs = [(N, out_dtype)] if outs is None else outs
    ne, no = len(extras), len(outs)

    def body(*refs):
        a_ref, b_ref = refs[:2]
        c_ref = refs[2] if has_acc else None
        e_refs = refs[2 + has_acc:2 + has_acc + ne]
        o_refs = refs[2 + has_acc + ne:2 + has_acc + ne + no]
        acc_ref = refs[-1]
        k = pl.program_id(2)
        part = lax.dot_general(a_ref[...].astype(BF16), b_ref[...].astype(BF16), dims, preferred_element_type=F32)

        def finish(res):
            tiles = [res] if epilogue is None else epilogue(res, *[e[...] for e in e_refs])
            for o_ref, t in zip(o_refs, tiles):
                o_ref[...] = t.astype(o_ref.dtype)

        if nk == 1:
            finish(part + c_ref[...].astype(F32) if has_acc else part)
        else:
            @pl.when(k == 0)
            def _():
                acc_ref[...] = part + c_ref[...].astype(F32) if has_acc else part

            @pl.when(jnp.logical_and(k > 0, k < nk - 1))
            def _():
                acc_ref[...] += part

            @pl.when(k == nk - 1)
            def _():
                finish(acc_ref[...] + part)

    def cspec(cols):
        return pl.BlockSpec((tm, cols * tn // N), lambda i, j, k: (i, j))

    a_spec = pl.BlockSpec((tk, tm), lambda i, j, k: (k, i)) if ta else pl.BlockSpec((tm, tk), lambda i, j, k: (i, k))
    b_spec = pl.BlockSpec((tn, tk), lambda i, j, k: (j, k)) if tb else pl.BlockSpec((tk, tn), lambda i, j, k: (k, j))
    in_specs, args = [a_spec, b_spec], [a, b]
    if has_acc:
        in_specs.append(cspec(N))
        args.append(acc)
    in_specs += [cspec(e.shape[1]) for e in extras]
    args += list(extras)
    res, comm_res = _carrier_call(
        body, (M // tm, N // tn, nk), in_specs, [cspec(c) for c, _ in outs], [S((M, c), dt) for c, dt in outs],
        [pltpu.VMEM((tm, tn), F32)] if nk > 1 else [], args, comm, name)
    res = res[0] if len(outs) == 1 else res
    return res if comm is None else (res, comm_res)


def _mm_dil(a_list, b_list, d, acc, name, tm=1024):
    T, K = a_list[0].shape
    N = b_list[0].shape[0]
    tm, tn = min(tm, T), _pick(N, 1024)
    na = len(a_list)
    has_acc = acc is not None

    def body(*refs):
        a_refs, b_refs, rest = refs[:na], refs[na:2 * na], refs[2 * na:]
        c_ref = rest[0] if has_acc else None
        o_ref, scr = rest[-2], rest[-1]
        out = c_ref[...] if has_acc else None
        for a_ref, b_ref in zip(a_refs, b_refs):
            a_tok = _dil_to_tok(scr, a_ref, d).astype(BF16) if d > 1 else a_ref[...]
            part = lax.dot_general(a_tok, b_ref[...], NT, preferred_element_type=F32)
            out = part if out is None else out + part
        o_ref[...] = out

    if d > 1:
        a_spec = pl.BlockSpec((d, tm // d, K), lambda i, j: (0, i, 0))
        a_args = [a.reshape(d, T // d, K) for a in a_list]
    else:
        a_spec = pl.BlockSpec((tm, K), lambda i, j: (i, 0))
        a_args = list(a_list)
    o_spec = pl.BlockSpec((tm, tn), lambda i, j: (i, j))
    in_specs = [a_spec] * na + [pl.BlockSpec((tn, K), lambda i, j: (j, 0))] * na + ([o_spec] if has_acc else [])
    return pl.pallas_call(
        body, grid=(T // tm, N // tn), in_specs=in_specs, out_specs=o_spec, out_shape=S((T, N), F32),
        scratch_shapes=[pltpu.VMEM((K // LANES, tm, LANES), F32)],
        compiler_params=_cparams(("parallel", "parallel")), name=name)(*a_args, *b_list, *([acc] if has_acc else []))


class _Comm:
    def __init__(self, ins, outs, sems, start, wait, alias=None):
        self.ins, self.outs, self.sems, self.start, self.wait, self.alias = list(ins), list(outs), list(sems), start, wait, alias or {}


def _carrier_call(body, grid, in_specs, out_specs, out_shape, scratch_shapes, args, comm, name):
    grid = (grid,) if isinstance(grid, int) else tuple(grid)
    seq = ("arbitrary",) * len(grid)
    ni, no, ns = len(in_specs), len(out_specs), len(scratch_shapes)
    if comm is None:
        res = pl.pallas_call(body, grid=grid, in_specs=in_specs, out_specs=out_specs, out_shape=out_shape,
                             scratch_shapes=scratch_shapes, compiler_params=_cparams(seq), name=name)(*args)
        return list(res), []
    ci, co = len(comm.ins), len(comm.outs)

    def wrapped(*refs):
        ins, cins = refs[:ni], refs[ni:ni + ci]
        outs, couts = refs[ni + ci:ni + ci + no], refs[ni + ci + no:ni + ci + no + co]
        scr, csems = refs[ni + ci + no + co:ni + ci + no + co + ns], refs[ni + ci + no + co + ns:]
        ids = [pl.program_id(i) for i in range(len(grid))]
        first = functools.reduce(jnp.logical_and, [i == 0 for i in ids])
        last = functools.reduce(jnp.logical_and, [i == g - 1 for i, g in zip(ids, grid)])

        @pl.when(first)
        def _():
            comm.start(cins, couts, csems)

        body(*ins, *outs, *scr)

        @pl.when(last)
        def _():
            comm.wait(cins, couts, csems)

    anys = pl.BlockSpec(memory_space=pl.ANY)
    res = pl.pallas_call(
        wrapped, grid=grid, in_specs=list(in_specs) + [anys] * ci, out_specs=list(out_specs) + [anys] * co,
        out_shape=list(out_shape) + comm.outs, scratch_shapes=list(scratch_shapes) + comm.sems,
        input_output_aliases={ni + a: no + b for a, b in comm.alias.items()},
        compiler_params=_cparams(seq), name=name)(*args, *comm.ins)
    return list(res[:no]), list(res[no:])


def _dev_index(dev):
    return 4 * dev[0] + 2 * dev[1] + dev[2]


def _comm_gather_spread(shards):
    npc = len(shards)

    def copies(x_refs, o_refs, sems):
        x, y, c, chips = _place()
        me = (x, y, c)
        peers = [(x, y, 1 - c)] + [(*chip, c) for chip in chips]
        return [[pltpu.make_async_remote_copy(src_ref=x_refs[i], dst_ref=o_refs[i].at[_dev_index(me)], send_sem=sems[0].at[k, i],
                                              recv_sem=sems[1].at[k, i], device_id=peer, device_id_type=MESH)
                 for k, peer in enumerate(peers)] for i in range(npc)], peers, me

    def local(x_refs, o_refs, sems, me):
        return [pltpu.make_async_copy(x_refs[i], o_refs[i].at[_dev_index(me)], sems[2].at[i]) for i in range(npc)]

    def start(x_refs, o_refs, sems):
        cps, _, me = copies(x_refs, o_refs, sems)
        for cp in local(x_refs, o_refs, sems, me):
            cp.start()
        for row in cps:
            for cp in row:
                cp.start()

    def wait(x_refs, o_refs, sems):
        cps, peers, me = copies(x_refs, o_refs, sems)
        for i in range(npc):
            for k, peer in enumerate(peers):
                pltpu.make_async_remote_copy(src_ref=x_refs[i], dst_ref=o_refs[i].at[_dev_index(peer)], send_sem=sems[0].at[k, i],
                                             recv_sem=sems[1].at[k, i], device_id=peer, device_id_type=MESH).wait_recv()
        for row in cps:
            for cp in row:
                cp.wait_send()
        for cp in local(x_refs, o_refs, sems, me):
            cp.wait()

    return _Comm(shards, [S((N_DEV,) + s.shape, s.dtype) for s in shards],
                 [pltpu.SemaphoreType.DMA((4, npc)), pltpu.SemaphoreType.DMA((4, npc)), pltpu.SemaphoreType.DMA((npc,))], start, wait)


def _comm_gather_pass(gathered):
    npc = len(gathered)

    def copies(o_refs, sems, sent):
        x, y, c, chips = _place()
        return [pltpu.make_async_remote_copy(
            src_ref=o_refs[i].at[_dev_index((*chip, c))], dst_ref=o_refs[i].at[_dev_index((*chip, c if sent else 1 - c))],
            send_sem=sems[0].at[j, i], recv_sem=sems[1].at[j, i], device_id=(x, y, 1 - c), device_id_type=MESH)
            for i in range(npc) for j, chip in enumerate(chips)]

    def start(g_refs, o_refs, sems):
        for cp in copies(o_refs, sems, True):
            cp.start()

    def wait(g_refs, o_refs, sems):
        for cp in copies(o_refs, sems, False):
            cp.wait_recv()
        for cp in copies(o_refs, sems, True):
            cp.wait_send()

    return _Comm(gathered, [S(g.shape, g.dtype) for g in gathered],
                 [pltpu.SemaphoreType.DMA((3, npc)), pltpu.SemaphoreType.DMA((3, npc))], start, wait,
                 alias={i: i for i in range(npc)})


def _comm_to_chips(parts):
    npc = len(parts)

    def copies(p_refs, o_refs, sems):
        x, y, c, chips = _place()
        return [pltpu.make_async_remote_copy(
            src_ref=p_refs[i].at[2 * chip[0] + chip[1]], dst_ref=o_refs[i].at[j], send_sem=sems[0].at[j, i],
            recv_sem=sems[1].at[j, i], device_id=(*chip, c), device_id_type=MESH)
            for i in range(npc) for j, chip in enumerate(chips)]

    def start(p_refs, o_refs, sems):
        for cp in copies(p_refs, o_refs, sems):
            cp.start()

    def wait(p_refs, o_refs, sems):
        for cp in copies(p_refs, o_refs, sems):
            cp.wait()

    return _Comm(parts, [S((3,) + p.shape[1:], p.dtype) for p in parts],
                 [pltpu.SemaphoreType.DMA((3, npc)), pltpu.SemaphoreType.DMA((3, npc))], start, wait)


def _dil_to_tok(scr, ref, d):
    n, C = ref.shape[1], ref.shape[2]
    for r in range(d):
        v = ref[r].astype(F32)
        for cb in range(C // LANES):
            scr.at[cb][pl.ds(r, n, stride=d), :] = v[:, cb * LANES:(cb + 1) * LANES]
    return jnp.concatenate([scr[cb] for cb in range(C // LANES)], axis=1)


def _tok_to_dil(scr, val, ref, d):
    n, C = ref.shape[1], ref.shape[2]
    for cb in range(C // LANES):
        scr[cb] = val[:, cb * LANES:(cb + 1) * LANES].astype(F32)
    for r in range(d):
        ref[r] = jnp.concatenate([scr.at[cb][pl.ds(r, n, stride=d), :] for cb in range(C // LANES)], axis=1).astype(ref.dtype)


def _rowwise(fn, rows, fulls, outs, accs=(), *, tm, name, cap=True, comm=None):
    rows = [r if isinstance(r, tuple) else (r, r.shape[1], 0) for r in rows]
    first = rows[0]
    T = (first[1] if isinstance(first[0], str) else first[0]).shape[0]
    widest = max([r[1].shape[1] if isinstance(r[0], str) else r[1] for r in rows] + [o[0] for o in outs])
    if cap:
        tm = min(tm, max(8, ROW_TILES_BYTES // (2 * (len(rows) + len(outs))) // (4 * widest) // 8 * 8))
    tm = T if T <= tm else _pick(T, tm, 8)
    nr, nf, no, na = len(rows), len(fulls), len(outs), len(accs)
    dil_in = [i for i, r in enumerate(rows) if isinstance(r[0], str) and r[2] > 1]
    dil_out = [i for i, o in enumerate(outs) if len(o) == 3 and o[2] > 1]
    scr_cols = [rows[i][1].shape[1] for i in dil_in] + [outs[i][0] for i in dil_out]

    def body(*refs):
        r, f = refs[:nr], refs[nr:nr + nf]
        o, a = refs[nr + nf:nr + nf + no], refs[nr + nf + no:nr + nf + no + na]
        scr = refs[nr + nf + no + na:]
        tiles = []
        for i, x in enumerate(r):
            if i in dil_in:
                tiles.append(_dil_to_tok(scr[dil_in.index(i)], x, rows[i][2]))
            else:
                tiles.append(x[...].astype(F32))
        ro, ra = fn(*tiles, *[x[...] for x in f])
        for i, (ref, val) in enumerate(zip(o, ro)):
            if i in dil_out:
                _tok_to_dil(scr[len(dil_in) + dil_out.index(i)], val, ref, outs[i][2])
            else:
                ref[...] = val.astype(ref.dtype)
        if na:
            @pl.when(pl.program_id(0) == 0)
            def _():
                for ref in a:
                    ref[...] = jnp.zeros_like(ref)
            for ref, val in zip(a, ra):
                ref[...] += val

    in_specs, args = [], []
    for i, rr in enumerate(rows):
        if isinstance(rr[0], str):
            arr, d = rr[1], rr[2]
            if d > 1:
                in_specs.append(pl.BlockSpec((d, tm // d, arr.shape[1]), lambda i: (0, i, 0)))
                args.append(arr.reshape(d, T // d, arr.shape[1]))
            else:
                in_specs.append(pl.BlockSpec((tm, arr.shape[1]), lambda i: (i, 0)))
                args.append(arr)
        else:
            in_specs.append(pl.BlockSpec((tm, rr[1]), functools.partial(lambda i, cb: (i, cb), cb=rr[2])))
            args.append(rr[0])
    in_specs += [pl.BlockSpec(f.shape, lambda i: (0, 0)) for f in fulls]
    out_specs, out_shape = [], []
    for i, oo in enumerate(outs):
        if i in dil_out:
            d = oo[2]
            out_specs.append(pl.BlockSpec((d, tm // d, oo[0]), lambda i: (0, i, 0)))
            out_shape.append(S((d, T // d, oo[0]), oo[1]))
        else:
            out_specs.append(pl.BlockSpec((tm, oo[0]), lambda i: (i, 0)))
            out_shape.append(S((T, oo[0]), oo[1]))
    out_specs += [pl.BlockSpec(sh, lambda i: (0, 0)) for sh in accs]
    out_shape += [S(sh, F32) for sh in accs]
    res, comm_res = _carrier_call(
        body, T // tm, in_specs, out_specs, out_shape, [pltpu.VMEM((c // LANES, tm, LANES), F32) for c in scr_cols],
        list(args) + list(fulls), comm, name)
    res = [x.reshape(T, x.shape[2]) if i in dil_out else x for i, x in enumerate(res)]
    return res if comm is None else (res, comm_res)


def _rmsnorm_fwd(h, w, name, dils=()):
    D = h.shape[1]

    def fn(h, w):
        r = lax.rsqrt(jnp.mean(h * h, axis=-1, keepdims=True) + EPS)
        xn = h * r * w
        return [xn] * (1 + len(dils)), []
    return _rowwise(fn, [h], [w.reshape(1, -1)], [(D, BF16)] + [(D, BF16, d) for d in dils], tm=512, name=name)


def _rmsnorm_bwd(dxn, h, w, dres, name):
    def fn(dxn, h, dres, w):
        r = lax.rsqrt(jnp.mean(h * h, axis=-1, keepdims=True) + EPS)
        n = h * r
        dn = dxn * w
        dh = r * (dn - n * jnp.mean(dn * n, axis=-1, keepdims=True)) + dres
        return [dh], [jnp.sum(dxn * n, axis=0, keepdims=True)]
    D = h.shape[1]
    return _rowwise(fn, [dxn, h, dres], [w.reshape(1, -1)], [(D, F32)], [(1, D)], tm=256, name=name)


def _loss_head(h, w, tgt, name):
    D = h.shape[1]

    def fn(h, tgt, w):
        r = lax.rsqrt(jnp.mean(h * h, axis=-1, keepdims=True) + EPS)
        n = h * r
        e = n * w - tgt
        row_loss = 0.5 * jnp.mean(e * e, axis=-1, keepdims=True)
        dy = e * (1.0 / D)
        dn = dy * w
        dh = r * (dn - n * jnp.mean(dn * n, axis=-1, keepdims=True))
        return [dh], [jnp.sum(dy * n, axis=0, keepdims=True), jnp.broadcast_to(jnp.sum(row_loss, axis=0, keepdims=True), (1, LANES))]
    return _rowwise(fn, [h, tgt], [w.reshape(1, -1)], [(D, F32)], [(1, D), (1, LANES)], tm=256, name=name)


def _combine_fwd(os_, ls, name):
    def fn(o0, o1, o2, l0, l1, l2):
        m = jnp.maximum(jnp.maximum(l0, l1), l2)
        e0, e1, e2 = jnp.exp(l0 - m), jnp.exp(l1 - m), jnp.exp(l2 - m)
        attn = (e0 * o0 + e1 * o1 + e2 * o2) / (e0 + e1 + e2)
        return [attn, attn], []
    dil = [("dil", t, d) for t, (_, d) in zip(list(os_) + list(ls), DILATED_GROUPS * 2)]
    return _rowwise(fn, dil, [], [(ATTN_OUT_WIDTH, BF16), (ATTN_OUT_WIDTH, F32)], tm=512, name=name)


def _combine_bwd(dattn, attn, ls, head_ones, name):
    def fn(dattn, attn, l0, l1, l2, ones):
        m = jnp.maximum(jnp.maximum(l0, l1), l2)
        e0, e1, e2 = jnp.exp(l0 - m), jnp.exp(l1 - m), jnp.exp(l2 - m)
        inv = 1.0 / (e0 + e1 + e2)
        t = _dotx_r(dattn * attn, ones, parts=2)
        outs = []
        for e in (e0, e1, e2):
            al = e * inv
            outs += [al * dattn, al * t]
        return outs, []
    W = ATTN_OUT_WIDTH
    dil = [("dil", t, d) for t, (_, d) in zip(ls, DILATED_GROUPS)]
    outs = [(W, dt, d) for _, d in DILATED_GROUPS for dt in (BF16, F32)]
    return _rowwise(fn, [dattn, attn] + dil, [head_ones], outs, tm=512, name=name)


def _dt_fwd(dt_raw, dt_bias, name):
    def fn(raw, b):
        z = raw[:, :N_SSM_HEADS] + b
        return [jnp.maximum(z, 0.0) + jnp.log(1.0 + jnp.exp(-jnp.abs(z)))], []
    return _rowwise(fn, [dt_raw], [dt_bias.reshape(1, -1)], [(N_SSM_HEADS, F32)], tm=1024, name=name)[0]


def _dt_bwd(ddt_a, ddt_b, dt_raw, dt_bias, name):
    def fn(da, db, raw, b):
        g = (da + db) * jax.nn.sigmoid(raw[:, :N_SSM_HEADS] + b)
        pad = jnp.zeros((g.shape[0], LANES - N_SSM_HEADS), F32)
        return [jnp.concatenate([g, pad], axis=1)], [jnp.sum(g, axis=0, keepdims=True)]
    return _rowwise(fn, [ddt_a, ddt_b, dt_raw], [dt_bias.reshape(1, -1)], [(LANES, BF16)], [(1, N_SSM_HEADS)], tm=1024, name=name)


def _ssm_norm_fwd(y, z, w, name, comm=None):
    G = D_INNER // N_SSM_GROUPS

    def fn(y, z, w):
        yg = y * _silu(z)
        outs = []
        for g in range(N_SSM_GROUPS):
            t = yg[:, g * G:(g + 1) * G]
            outs.append(t * lax.rsqrt(jnp.mean(t * t, axis=-1, keepdims=True) + EPS))
        return [jnp.concatenate(outs, axis=1) * w], []
    res = _rowwise(fn, [y, z], [w.reshape(1, -1)], [(D_INNER, BF16)], tm=256, name=name, comm=comm)
    return res[0] if comm is None else (res[0][0], res[1])


def _ssm_norm_bwd(dssm, y, z, w, name):
    G = D_INNER // N_SSM_GROUPS

    def fn(dssm, y, z, w):
        sz = _silu(z)
        yg = y * sz
        dn = dssm * w
        ns, dygs = [], []
        for g in range(N_SSM_GROUPS):
            t = yg[:, g * G:(g + 1) * G]
            r = lax.rsqrt(jnp.mean(t * t, axis=-1, keepdims=True) + EPS)
            n = t * r
            d = dn[:, g * G:(g + 1) * G]
            dygs.append(r * (d - n * jnp.mean(d * n, axis=-1, keepdims=True)))
            ns.append(n)
        n, dyg = jnp.concatenate(ns, axis=1), jnp.concatenate(dygs, axis=1)
        return [dyg * sz, dyg * y * _dsilu(z)], [jnp.sum(dssm * n, axis=0, keepdims=True)]
    return _rowwise(fn, [dssm, y, z], [w.reshape(1, -1)], [(D_INNER, F32), (D_INNER, BF16)], [(1, D_INNER)], tm=256, name=name)


def _gate_out_proj(a, sb, gl, h, w_out, name):
    def fn(a, sb, gl, h, w):
        g = jax.nn.sigmoid(gl)
        merged = (g[:, :D_MODEL] * a + g[:, D_MODEL:] * sb).astype(BF16)
        return [h + jnp.dot(merged, w, preferred_element_type=F32), merged], []
    return _rowwise(fn, [a, sb, gl, h], [w_out], [(D_MODEL, F32), (D_MODEL, BF16)], tm=512, cap=False, name=name)


def _d_out_proj_gate(dh, a, sb, gl, w_out, name):
    def fn(dh, a, sb, gl, w):
        dm = lax.dot_general(dh.astype(BF16), w, NT, preferred_element_type=F32)
        g = jax.nn.sigmoid(gl)
        g0, g1 = g[:, :D_MODEL], g[:, D_MODEL:]
        dgl = jnp.concatenate([dm * a * g0 * (1.0 - g0), dm * sb * g1 * (1.0 - g1)], axis=1)
        return [g0 * dm, g1 * dm, dgl], []
    return _rowwise(fn, [dh, a, sb, gl], [w_out], [(D_MODEL, BF16), (D_MODEL, BF16), (2 * D_MODEL, BF16)], tm=512, cap=False, name=name)


FFN_HALF = D_FF // 2


def _ffn_perm(w):
    h = FFN_HALF
    return jnp.concatenate([w[:, 0:h], w[:, D_FF:D_FF + h], w[:, h:D_FF], w[:, D_FF + h:]], axis=1)


def _ffn_unperm(w):
    h = FFN_HALF
    return jnp.concatenate([w[:, 0:h], w[:, 2 * h:3 * h], w[:, h:2 * h], w[:, 3 * h:]], axis=1)


def _swiglu_epilogue(res):
    return [res, _silu(res[:, :FFN_HALF]) * res[:, FFN_HALF:]]


def _dswiglu_epilogue(dact, u):
    u = u.astype(F32)
    gate, up = u[:, :FFN_HALF], u[:, FFN_HALF:]
    return [jnp.concatenate([dact * up * _dsilu(gate), dact * _silu(gate)], axis=1)]


def _adamw(w, g, m, v, name):
    c1 = 1.0 - ADAM_B1 ** ADAM_STEP
    c2 = 1.0 - ADAM_B2 ** ADAM_STEP

    def fn(w, g, m, v):
        m = ADAM_B1 * m + (1.0 - ADAM_B1) * g
        v = ADAM_B2 * v + (1.0 - ADAM_B2) * (g * g)
        delta = -ADAM_LR * ((m / c1) / (jnp.sqrt(v / c2) + ADAM_EPS) + ADAM_WD * w)
        return [delta, m, v], []
    C = w.shape[1]
    return _rowwise(fn, [w, g, m, v], [], [(C, F32)] * 3, tm=256, name=name)


def _bias_consts(dilation, n_steps):
    qi = np.arange(ATTN_BLOCK)[:, None]
    kj = np.arange(2 * ATTN_BLOCK)[None, :]
    steps = qi + ATTN_BLOCK - kj
    valid = (steps >= 0) & (steps <= n_steps)
    dist = jnp.asarray(np.clip(steps, 0, n_steps) * dilation, jnp.int32)
    max_exact = N_REL_BUCKETS // 2
    d_f = jnp.maximum(dist, 1).astype(F32)
    large = max_exact + (jnp.log(d_f / max_exact) / math.log(REL_MAX_DISTANCE / max_exact)
                         * (N_REL_BUCKETS - max_exact)).astype(jnp.int32)
    large = jnp.minimum(large, N_REL_BUCKETS - 1)
    bucket = jnp.where(dist < max_exact, dist, large).reshape(-1)
    onehot = (bucket[None, :] == jnp.arange(N_REL_BUCKETS)[:, None]).astype(F32)
    return onehot, jnp.asarray(valid.reshape(1, -1), F32)


def _bias_gather(rel_g_t, onehot, valid, name):
    def body(r_ref, oh_ref, v_ref, o_ref):
        b = jnp.dot(r_ref[...], oh_ref[...], preferred_element_type=F32, precision=lax.Precision.HIGHEST)
        o_ref[...] = jnp.where(v_ref[...] > 0.5, b, NEG)
    return pl.pallas_call(body, out_shape=S((HEADS_PER_GROUP, onehot.shape[1]), F32), compiler_params=_cparams(), name=name)(rel_g_t, onehot, valid)


def _bias_scatter(dbias, onehot, name):
    def body(d_ref, oh_ref, o_ref):
        o_ref[...] = lax.dot_general(d_ref[...], oh_ref[...], NT, preferred_element_type=F32, precision=lax.Precision.HIGHEST)
    return pl.pallas_call(body, out_shape=S((HEADS_PER_GROUP, N_REL_BUCKETS), F32), compiler_params=_cparams(), name=name)(dbias, onehot)


ATTN_QB_FWD, ATTN_QB_BWD = 4, 4


def _attn_tiles(T, d, qb):
    seg = T // d
    nqb = min(qb, seg // ATTN_BLOCK)
    tq = nqb * ATTN_BLOCK
    return seg, nqb, tq, seg // tq


def _attn_fwd(qkv, bias, d, name, comm=None):
    T = qkv.shape[0]
    seg, nqb, tq, ns = _attn_tiles(T, d, ATTN_QB_FWD)
    W = ATTN_OUT_WIDTH
    scale = HEAD_DIM ** -0.5

    def body(q_ref, kh_ref, kc_ref, vh_ref, vc_ref, b_ref, o_ref, l_ref, s_scr, p_scr):
        n = pl.program_id(1)
        qv = q_ref[...]
        kk = jnp.concatenate([kh_ref[...], kc_ref[...]], axis=0)
        vv = jnp.concatenate([vh_ref[...], vc_ref[...]], axis=0)
        col = lax.broadcasted_iota(jnp.int32, (ATTN_BLOCK, 2 * ATTN_BLOCK), 1)
        kill = jnp.logical_and(n == 0, col < ATTN_BLOCK)
        lo = lax.broadcasted_iota(jnp.int32, (1, LANES), 1) < HEAD_DIM
        zero = jnp.zeros((), BF16)
        for j in range(nqb):
            rows = slice(j * ATTN_BLOCK, (j + 1) * ATTN_BLOCK)
            keys = slice(j * ATTN_BLOCK, (j + 2) * ATTN_BLOCK)
            for hp in range(HEADS_PER_GROUP // 2):
                ps = slice(hp * LANES, (hp + 1) * LANES)
                q2 = (qv[rows, ps].astype(F32) * scale).astype(BF16)
                k2 = kk[keys, ps]
                s_scr[2 * hp] = lax.dot_general(q2, jnp.where(lo, k2, zero), NT, preferred_element_type=F32)
                s_scr[2 * hp + 1] = lax.dot_general(q2, jnp.where(lo, zero, k2), NT, preferred_element_type=F32)
            s = s_scr[...] + b_ref[...]
            if j == 0:
                s = jnp.where(kill[None], NEG, s)
            m = jnp.max(s, axis=-1, keepdims=True)
            p = jnp.exp(s - m)
            den = jnp.sum(p, axis=-1, keepdims=True)
            p_scr[...] = p.astype(BF16)
            inv = 1.0 / den
            lse = m + jnp.log(den)
            for hp in range(HEADS_PER_GROUP // 2):
                ps = slice(hp * LANES, (hp + 1) * LANES)
                v2 = vv[keys, ps]
                o2 = (jnp.dot(p_scr[2 * hp], jnp.where(lo, v2, zero), preferred_element_type=F32)
                      + jnp.dot(p_scr[2 * hp + 1], jnp.where(lo, zero, v2), preferred_element_type=F32))
                o_ref[rows, ps] = (o2 * jnp.where(lo, inv[2 * hp], inv[2 * hp + 1])).astype(o_ref.dtype)
                l_ref[rows, ps] = jnp.where(lo, lse[2 * hp], lse[2 * hp + 1])

    def cur(c):
        return pl.BlockSpec((tq, W), lambda r, n: (r * ns + n, c))

    def halo(c):
        return pl.BlockSpec((ATTN_BLOCK, W), lambda r, n: (jnp.maximum((r * ns + n) * nqb - 1, 0), c))

    res, comm_res = _carrier_call(
        body, (d, ns),
        [cur(0), halo(1), cur(1), halo(2), cur(2), pl.BlockSpec(bias.shape, lambda r, n: (0, 0, 0))],
        [cur(0), cur(0)], [S((T, W), BF16), S((T, W), F32)],
        [pltpu.VMEM((HEADS_PER_GROUP, ATTN_BLOCK, 2 * ATTN_BLOCK), F32), pltpu.VMEM((HEADS_PER_GROUP, ATTN_BLOCK, 2 * ATTN_BLOCK), BF16)],
        [qkv, qkv, qkv, qkv, qkv, bias], comm, name)
    return res if comm is None else (res, comm_res)


def _attn_bwd(qkv, bias, lse, do, dd, d, name):
    T = qkv.shape[0]
    seg, nqb, tq, ns = _attn_tiles(T, d, ATTN_QB_BWD)
    W = ATTN_OUT_WIDTH
    B = ATTN_BLOCK
    scale = HEAD_DIM ** -0.5

    def body(q_ref, kh_ref, kc_ref, vh_ref, vc_ref, b_ref, l_ref, do_ref, dd_ref, dq_ref, dk_ref, dv_ref, db_ref, pk_ref, pv_ref,
             s_scr, dp_scr, p_scr, ds_scr):
        r, n = pl.program_id(0), pl.program_id(1)

        @pl.when(jnp.logical_and(r == 0, n == 0))
        def _():
            db_ref[...] = jnp.zeros_like(db_ref)

        @pl.when(n == 0)
        def _():
            pk_ref[...] = jnp.zeros_like(pk_ref)
            pv_ref[...] = jnp.zeros_like(pv_ref)

        @pl.when(n < ns)
        def _():
            qv = q_ref[...]
            kk = jnp.concatenate([kh_ref[...], kc_ref[...]], axis=0)
            vv = jnp.concatenate([vh_ref[...], vc_ref[...]], axis=0)
            lse_v, do_v, dd_v = l_ref[...], do_ref[...], dd_ref[...]
            col = lax.broadcasted_iota(jnp.int32, (B, 2 * B), 1)
            kill = jnp.logical_and(n == 0, col < B)
            dqs = [[None] * (HEADS_PER_GROUP // 2) for _ in range(nqb)]
            dks = [[None] * (HEADS_PER_GROUP // 2) for _ in range(nqb)]
            dvs = [[None] * (HEADS_PER_GROUP // 2) for _ in range(nqb)]
            H, HP = HEADS_PER_GROUP, HEADS_PER_GROUP // 2
            do_b = do_v.astype(BF16)
            lo = lax.broadcasted_iota(jnp.int32, (1, LANES), 1) < HEAD_DIM
            zero = jnp.zeros((), BF16)
            first = lambda t: jnp.where(lo, t, zero)
            second = lambda t: jnp.where(lo, zero, t)
            for j in range(nqb):
                rows = slice(j * B, (j + 1) * B)
                keys = slice(j * B, (j + 2) * B)
                for hp in range(HP):
                    ps = slice(hp * LANES, (hp + 1) * LANES)
                    q2 = (qv[rows, ps].astype(F32) * scale).astype(BF16)
                    k2, v2, do2 = kk[keys, ps], vv[keys, ps], do_b[rows, ps]
                    s_scr[2 * hp] = lax.dot_general(q2, first(k2), NT, preferred_element_type=F32)
                    s_scr[2 * hp + 1] = lax.dot_general(q2, second(k2), NT, preferred_element_type=F32)
                    dp_scr[2 * hp] = lax.dot_general(do2, first(v2), NT, preferred_element_type=F32)
                    dp_scr[2 * hp + 1] = lax.dot_general(do2, second(v2), NT, preferred_element_type=F32)
                lse_h = jnp.stack([lse_v[rows, h * HEAD_DIM:h * HEAD_DIM + 1] for h in range(H)], axis=0)
                dd_h = jnp.stack([dd_v[rows, h * HEAD_DIM:h * HEAD_DIM + 1] for h in range(H)], axis=0)
                s = s_scr[...] + b_ref[...]
                if j == 0:
                    s = jnp.where(kill[None], NEG, s)
                p = jnp.exp(s - lse_h)
                ds = p * (dp_scr[...] - dd_h)
                db_ref[...] += ds
                p_scr[...] = p.astype(BF16)
                ds_scr[...] = ds.astype(BF16)
                for hp in range(HP):
                    ps = slice(hp * LANES, (hp + 1) * LANES)
                    q2 = (qv[rows, ps].astype(F32) * scale).astype(BF16)
                    k2, do2 = kk[keys, ps], do_b[rows, ps]
                    pa, pb, da, db_ = p_scr[2 * hp], p_scr[2 * hp + 1], ds_scr[2 * hp], ds_scr[2 * hp + 1]
                    dvs[j][hp] = (lax.dot_general(pa, first(do2), TN, preferred_element_type=F32)
                                  + lax.dot_general(pb, second(do2), TN, preferred_element_type=F32))
                    dqs[j][hp] = (jnp.dot(da, first(k2), preferred_element_type=F32)
                                  + jnp.dot(db_, second(k2), preferred_element_type=F32)) * scale
                    dks[j][hp] = (lax.dot_general(da, first(q2), TN, preferred_element_type=F32)
                                  + lax.dot_general(db_, second(q2), TN, preferred_element_type=F32))
            dq_ref[...] = jnp.concatenate([jnp.concatenate(dqs[j], axis=1) for j in range(nqb)], axis=0).astype(dq_ref.dtype)
            for parts, out_ref, pend in ((dks, dk_ref, pk_ref), (dvs, dv_ref, pv_ref)):
                full = [jnp.concatenate(parts[j], axis=1) for j in range(nqb)]
                if tq > B:
                    out_ref[:tq - B] = pend[:tq - B].astype(out_ref.dtype)
                out_ref[tq - B:] = (pend[tq - B:] + full[0][:B]).astype(out_ref.dtype)
                for j in range(nqb - 1):
                    pend[j * B:(j + 1) * B] = full[j][B:] + full[j + 1][:B]
                pend[tq - B:] = full[nqb - 1][B:]

        @pl.when(n == ns)
        def _():
            dk_ref[...] = pk_ref[...].astype(dk_ref.dtype)
            dv_ref[...] = pv_ref[...].astype(dv_ref.dtype)

    def cur(c):
        return pl.BlockSpec((tq, W), lambda r, n: (r * ns + jnp.minimum(n, ns - 1), c))

    def halo(c):
        return pl.BlockSpec((B, W), lambda r, n: (jnp.maximum((r * ns + jnp.minimum(n, ns - 1)) * nqb - 1, 0), c))

    late = pl.BlockSpec((tq, W), lambda r, n: (r * ns + jnp.clip(n - 1, 0, ns - 1), 0))
    bspec = pl.BlockSpec(bias.shape, lambda r, n: (0, 0, 0))
    return pl.pallas_call(
        body, grid=(d, ns + 1),
        in_specs=[cur(0), halo(1), cur(1), halo(2), cur(2), bspec, cur(0), cur(0), cur(0)],
        out_specs=[cur(0), late, late, bspec],
        out_shape=[S((T, W), BF16)] * 3 + [S(bias.shape, F32)],
        scratch_shapes=[pltpu.VMEM((tq, W), F32), pltpu.VMEM((tq, W), F32)]
                       + [pltpu.VMEM((HEADS_PER_GROUP, B, 2 * B), t) for t in (F32, F32, BF16, BF16)],
        compiler_params=_cparams(("arbitrary", "arbitrary")), name=name,
    )(qkv, qkv, qkv, qkv, qkv, bias, lse, do, dd)


CONV_TM, CONV_TC = 512, 1024


def _shift_down(x, halo8, s, row8):
    xr = pltpu.roll(x, s, 0)
    first = jnp.where(row8 < s, pltpu.roll(halo8, s, 0), xr[:8])
    return jnp.concatenate([first, xr[8:]], axis=0)


def _shift_up(x, halo8, s, row8):
    n = x.shape[0]
    xr = pltpu.roll(x, n - s, 0)
    last = jnp.where(row8 >= 8 - s, pltpu.roll(halo8, 8 - s, 0), xr[n - 8:])
    return jnp.concatenate([xr[:n - 8], last], axis=0)


def _conv_fwd(x, w, b, name, comm=None):
    T, C = x.shape
    tm, tc = min(CONV_TM, T), CONV_TC

    def body(x_ref, p_ref, w_ref, b_ref, u_ref, a_ref):
        ti = pl.program_id(1)
        xv = x_ref[...]
        p8 = jnp.where(ti == 0, 0.0, p_ref[...])
        wv = w_ref[...]
        row8 = lax.broadcasted_iota(jnp.int32, (8, tc), 0)
        u = xv * wv[3:4] + b_ref[...]
        for s in (1, 2, 3):
            u = u + _shift_down(xv, p8, s, row8) * wv[3 - s:4 - s]
        u_ref[...] = u
        a_ref[...] = _silu(u)

    cur = pl.BlockSpec((tm, tc), lambda cj, ti: (ti, cj))
    halo = pl.BlockSpec((8, tc), lambda cj, ti: (jnp.maximum(ti * (tm // 8) - 1, 0), cj))
    res, comm_res = _carrier_call(
        body, (C // tc, T // tm),
        [cur, halo, pl.BlockSpec((CONV_WIDTH, tc), lambda cj, ti: (0, cj)), pl.BlockSpec((1, tc), lambda cj, ti: (0, cj))],
        [cur, cur], [S((T, C), F32)] * 2, [], [x, x, w, b], comm, name)
    return res if comm is None else (res, comm_res)


def _conv_bwd(dact, u, x, w, name):
    T, C = x.shape
    tm, tc = min(CONV_TM, T), CONV_TC
    nt = T // tm

    def body(d_ref, dn_ref, u_ref, un_ref, x_ref, w_ref, dx_ref, dw_ref, db_ref):
        ti = pl.program_id(1)

        @pl.when(ti == 0)
        def _():
            dw_ref[...] = jnp.zeros_like(dw_ref)
            db_ref[...] = jnp.zeros_like(db_ref)

        du = d_ref[...] * _dsilu(u_ref[...])
        dun = jnp.where(ti == nt - 1, 0.0, dn_ref[...] * _dsilu(un_ref[...]))
        xv = x_ref[...]
        wv = w_ref[...]
        row8 = lax.broadcasted_iota(jnp.int32, (8, tc), 0)
        dx = du * wv[3:4]
        dws = [None] * CONV_WIDTH
        dws[3] = jnp.sum(du * xv, axis=0, keepdims=True)
        for s in (1, 2, 3):
            up = _shift_up(du, dun, s, row8)
            dx = dx + up * wv[3 - s:4 - s]
            dws[3 - s] = jnp.sum(up * xv, axis=0, keepdims=True)
        dx_ref[...] = dx.astype(dx_ref.dtype)
        dw_ref[...] += jnp.concatenate(dws, axis=0)
        db_ref[...] += jnp.sum(du, axis=0, keepdims=True)

    cur = pl.BlockSpec((tm, tc), lambda cj, ti: (ti, cj))
    nxt = pl.BlockSpec((8, tc), lambda cj, ti: (jnp.minimum((ti + 1) * (tm // 8), T // 8 - 1), cj))
    return pl.pallas_call(
        body, grid=(C // tc, nt),
        in_specs=[cur, nxt, cur, nxt, cur, pl.BlockSpec((CONV_WIDTH, tc), lambda cj, ti: (0, cj))],
        out_specs=[cur, pl.BlockSpec((CONV_WIDTH, tc), lambda cj, ti: (0, cj)), pl.BlockSpec((1, tc), lambda cj, ti: (0, cj))],
        out_shape=[S((T, C), BF16), S((CONV_WIDTH, C), F32), S((1, C), F32)],
        compiler_params=_cparams(("parallel", "arbitrary")), name=name)(dact, dact, u, u, x, w)


def _ssd_consts():
    i = np.arange(SSD_CHUNK)
    tril = (i[None, :] <= i[:, None]).astype(np.float32)
    trils = (i[None, :] < i[:, None]).astype(np.float32)
    head = np.repeat(np.arange(N_SSM_HEADS), D_INNER // N_SSM_HEADS)
    et = (head[None, :] == np.arange(N_SSM_HEADS)[:, None]).astype(np.float32)
    c = lambda a: jnp.asarray(a, BF16)
    return dict(tril=c(tril), triu=c(tril.T), trils=c(trils), et=c(et), e=c(et.T))


def _ssd_common(act_ref, dt_ref, dtT_ref, al_ref, alT_ref, tril_ref, triu_ref, et_ref):
    a_row = -jnp.exp(al_ref[...])
    a_col = -jnp.exp(alT_ref[...])
    dt, dtT = dt_ref[...], dtT_ref[...]
    la = _dotx_l(tril_ref[...], dt * a_row)
    laT = _dotx_r(dtT * a_col, triu_ref[...])
    et = et_ref[...]
    la_e = _dotx_r(la, et)
    dt_e = _dotx_r(dt, et, parts=2)
    x = act_ref[:, :D_INNER]
    xdt = x * dt_e
    la_q = la_e[SSD_CHUNK - 1:SSD_CHUNK, :]
    return a_row, a_col, dt, dtT, la, laT, la_e, dt_e, x, xdt, la_q


def _decay(la, laT, h, causal):
    seg = la[:, h:h + 1] - laT[h:h + 1, :]
    return jnp.exp(jnp.where(causal, seg, NEG))


def _ssd_fwd(act, dt, dtT, alog, dskip_e, cs, name, comm=None):
    T = act.shape[0]
    nc = T // SSD_CHUNK
    Q, G, GW = SSD_CHUNK, N_SSM_GROUPS, D_INNER // N_SSM_GROUPS

    def body(act_ref, dt_ref, dtT_ref, al_ref, alT_ref, dsk_ref, tril_ref, triu_ref, et_ref, y_ref, st_ref, scr):
        @pl.when(pl.program_id(0) == 0)
        def _():
            scr[...] = jnp.zeros_like(scr)
        st_ref[0] = scr[...]
        a_row, a_col, dtv, dtTv, la, laT, la_e, dt_e, x, xdt, la_q = _ssd_common(
            act_ref, dt_ref, dtT_ref, al_ref, alT_ref, tril_ref, triu_ref, et_ref)
        ela = jnp.exp(la_e)
        xdt_b = xdt.astype(BF16)
        xdte_b = (xdt * jnp.exp(la_q - la_e)).astype(BF16)
        ela_q = jnp.exp(la_q)
        causal = lax.broadcasted_iota(jnp.int32, (Q, Q), 0) >= lax.broadcasted_iota(jnp.int32, (Q, Q), 1)
        for g in range(G):
            gs = slice(g * GW, (g + 1) * GW)
            Bg = act_ref[:, D_INNER + g * D_STATE:D_INNER + (g + 1) * D_STATE].astype(BF16)
            Cg = act_ref[:, D_INNER + G * D_STATE + g * D_STATE:D_INNER + G * D_STATE + (g + 1) * D_STATE].astype(BF16)
            cb = lax.dot_general(Cg, Bg, NT, preferred_element_type=F32)
            st = scr[g]
            y_inter = jnp.dot(Cg, st.astype(BF16), preferred_element_type=F32) * ela[:, gs]
            ys = []
            for hh in range(HEADS_PER_GROUP):
                h = g * HEADS_PER_GROUP + hh
                m = (cb * _decay(la, laT, h, causal)).astype(BF16)
                ys.append(jnp.dot(m, xdt_b[:, h * HEAD_DIM:(h + 1) * HEAD_DIM], preferred_element_type=F32))
            y_ref[:, gs] =jnp.concatenate(ys, axis=1) + y_inter + x[:, gs] * dsk_ref[:, gs]
            scr[g] = st * ela_q[:, gs] + lax.dot_general(Bg, xdte_b[:, gs], TN, preferred_element_type=F32)

    full = lambda a: pl.BlockSpec(a.shape, lambda c: (0,) * a.ndim)
    al, alT = alog.reshape(1, -1), alog.reshape(-1, 1)
    res, comm_res = _carrier_call(
        body, nc,
        [pl.BlockSpec((Q, XBC_WIDTH), lambda c: (c, 0)), pl.BlockSpec((Q, N_SSM_HEADS), lambda c: (c, 0)),
         pl.BlockSpec((N_SSM_HEADS, Q), lambda c: (0, c)), full(al), full(alT), full(dskip_e),
         full(cs["tril"]), full(cs["triu"]), full(cs["et"])],
        [pl.BlockSpec((Q, D_INNER), lambda c: (c, 0)), pl.BlockSpec((1, G, D_STATE, GW), lambda c: (c, 0, 0, 0))],
        [S((T, D_INNER), F32), S((nc, G, D_STATE, GW), F32)],
        [pltpu.VMEM((G, D_STATE, GW), F32)],
        [act, dt, dtT, al, alT, dskip_e, cs["tril"], cs["triu"], cs["et"]], comm, name)
    return res if comm is None else (res, comm_res)


def _ssd_bwd(dy, act, dt, dtT, alog, dskip_e, states, cs, name, comm=None):
    T = act.shape[0]
    nc = T // SSD_CHUNK
    Q, G, GW, H = SSD_CHUNK, N_SSM_GROUPS, D_INNER // N_SSM_GROUPS, N_SSM_HEADS

    def body(dy_ref, act_ref, dt_ref, dtT_ref, al_ref, alT_ref, dsk_ref, stp_ref, tril_ref, triu_ref, trils_ref,
             et_ref, e_ref, dact_ref, ddt_ref, ddtT_ref, da_ref, daT_ref, dsk_out_ref, dst, wbuf, ubuf, vbuf, sbuf, dm_scr, m_scr):
        @pl.when(pl.program_id(0) == 0)
        def _():
            dst[...] = jnp.zeros_like(dst)
            da_ref[...] = jnp.zeros_like(da_ref)
            daT_ref[...] = jnp.zeros_like(daT_ref)
            dsk_out_ref[...] = jnp.zeros_like(dsk_out_ref)
        a_row, a_col, dtv, dtTv, la, laT, la_e, dt_e, x, xdt, la_q = _ssd_common(
            act_ref, dt_ref, dtT_ref, al_ref, alT_ref, tril_ref, triu_ref, et_ref)
        dyv = dy_ref[...]
        ela = jnp.exp(la_e)
        e_end = jnp.exp(la_q - la_e)
        ela_q = jnp.exp(la_q)
        dye_b = (dyv * ela).astype(BF16)
        dy_b = dyv.astype(BF16)
        xdt_b = xdt.astype(BF16)
        xdte_b = (xdt * e_end).astype(BF16)
        ri = lax.broadcasted_iota(jnp.int32, (Q, Q), 0)
        ci = lax.broadcasted_iota(jnp.int32, (Q, Q), 1)
        causal = ri >= ci
        rows = []
        for g in range(G):
            gs = slice(g * GW, (g + 1) * GW)
            Bg = act_ref[:, D_INNER + g * D_STATE:D_INNER + (g + 1) * D_STATE].astype(BF16)
            Cg = act_ref[:, D_INNER + G * D_STATE + g * D_STATE:D_INNER + G * D_STATE + (g + 1) * D_STATE].astype(BF16)
            cb = lax.dot_general(Cg, Bg, NT, preferred_element_type=F32)
            stp = stp_ref[0, g]
            stp_b = stp.astype(BF16)
            dstv = dst[g]
            dst_b = dstv.astype(BF16)
            y_inter = jnp.dot(Cg, stp_b, preferred_element_type=F32) * ela[:, gs]
            wbuf[:, gs] = dyv[:, gs] * y_inter
            dxdt_state = jnp.dot(Bg, dst_b, preferred_element_type=F32) * e_end[:, gs]
            ubuf[:, gs] = dxdt_state * xdt[:, gs]
            dC = lax.dot_general(dye_b[:, gs], stp_b, NT, preferred_element_type=F32)
            dB = lax.dot_general(xdte_b[:, gs], dst_b, NT, preferred_element_type=F32)
            sbuf[:, gs] = jnp.broadcast_to(jnp.sum(dstv * stp, axis=0, keepdims=True), (8, GW))
            dst[g] = dstv * ela_q[:, gs] + lax.dot_general(Cg, dye_b[:, gs], TN, preferred_element_type=F32)
            for hh in range(HEADS_PER_GROUP):
                hs = slice((g * HEADS_PER_GROUP + hh) * HEAD_DIM, (g * HEADS_PER_GROUP + hh + 1) * HEAD_DIM)
                dm_scr[hh] = lax.dot_general(dy_b[:, hs], xdt_b[:, hs], NT, preferred_element_type=F32)
            dG = jnp.zeros((Q, Q), F32)
            for hh in range(HEADS_PER_GROUP):
                L = _decay(la, laT, g * HEADS_PER_GROUP + hh, causal)
                M = cb * L
                dM = dm_scr[hh]
                dG = dG + dM * L
                W = dM * M
                rows.append(jnp.sum(W.T, axis=0, keepdims=True) - jnp.sum(W, axis=0, keepdims=True))
                m_scr[hh] = M.astype(BF16)
            dxs = []
            for hh in range(HEADS_PER_GROUP):
                hs = slice((g * HEADS_PER_GROUP + hh) * HEAD_DIM, (g * HEADS_PER_GROUP + hh + 1) * HEAD_DIM)
                dxs.append(lax.dot_general(m_scr[hh], dy_b[:, hs], TN, preferred_element_type=F32))
            dG_b = dG.astype(BF16)
            dC = dC + jnp.dot(dG_b, Bg, preferred_element_type=F32)
            dB = dB + lax.dot_general(dG_b, Cg, TN, preferred_element_type=F32)
            dxdt = jnp.concatenate(dxs, axis=1) + dxdt_state
            vbuf[:, gs] = dxdt * x[:, gs]
            dact_ref[:, gs] = dxdt * dt_e[:, gs] + dyv[:, gs] * dsk_ref[:, gs]
            dact_ref[:, D_INNER + g * D_STATE:D_INNER + (g + 1) * D_STATE] = dB
            dact_ref[:, D_INNER + G * D_STATE + g * D_STATE:D_INNER + G * D_STATE + (g + 1) * D_STATE] = dC
        e = e_ref[...]
        w = _dotx_r(wbuf[...], e, parts=2)
        u = _dotx_r(ubuf[...], e, parts=2)
        vx = _dotx_r(vbuf[...], e, parts=2)
        dsk = _dotx_r(jnp.broadcast_to(jnp.sum(dyv * x, axis=0, keepdims=True), (8, D_INNER)), e, parts=2)[0:1]
        s0 =_dotx_r(sbuf[...], e, parts=2)[0:1] * jnp.exp(la[Q - 1:Q, :])
        ddelta = _dotx_l(triu_ref[...], w) + _dotx_l(trils_ref[...], u) + s0
        ddt_ref[...] = ddelta * a_row + vx
        ddeltaT = _dotx_r(jnp.concatenate(rows, axis=0), tril_ref[...])
        ddtT_ref[...] = ddeltaT * a_col
        da_ref[...] += jnp.sum(ddelta * dtv, axis=0, keepdims=True)
        daT_ref[...] += jnp.sum(ddeltaT * dtTv, axis=1, keepdims=True)
        dsk_out_ref[...] += dsk

    rev = lambda c: nc - 1 - c
    full = lambda a: pl.BlockSpec(a.shape, lambda c: (0,) * a.ndim)
    al, alT = alog.reshape(1, -1), alog.reshape(-1, 1)
    consts = [cs[k] for k in ("tril", "triu", "trils", "et", "e")]
    res, comm_res = _carrier_call(
        body, nc,
        [pl.BlockSpec((Q, D_INNER), lambda c: (rev(c), 0)), pl.BlockSpec((Q, XBC_WIDTH), lambda c: (rev(c), 0)),
         pl.BlockSpec((Q, H), lambda c: (rev(c), 0)), pl.BlockSpec((H, Q), lambda c: (0, rev(c))),
         full(al), full(alT), full(dskip_e), pl.BlockSpec((1, G, D_STATE, GW), lambda c: (rev(c), 0, 0, 0))]
        + [full(a) for a in consts],
        [pl.BlockSpec((Q, XBC_WIDTH), lambda c: (rev(c), 0)), pl.BlockSpec((Q, H), lambda c: (rev(c), 0)),
         pl.BlockSpec((H, Q), lambda c: (0, rev(c))), pl.BlockSpec((1, H), lambda c: (0, 0)),
         pl.BlockSpec((H, 1), lambda c: (0, 0)), pl.BlockSpec((1, H), lambda c: (0, 0))],
        [S((T, XBC_WIDTH), F32), S((T, H), F32), S((H, T), F32), S((1, H), F32), S((H, 1), F32), S((1, H), F32)],
        [pltpu.VMEM((G, D_STATE, GW), F32), pltpu.VMEM((Q, D_INNER), F32), pltpu.VMEM((Q, D_INNER), F32),
         pltpu.VMEM((Q, D_INNER), F32), pltpu.VMEM((8, D_INNER), F32),
         pltpu.VMEM((HEADS_PER_GROUP, Q, Q), F32), pltpu.VMEM((HEADS_PER_GROUP, Q, Q), BF16)],
        [dy, act, dt, dtT, al, alT, dskip_e, states] + consts, comm, name)
    return res if comm is None else (res, comm_res)


def _a_log_grad(da, daT_row, alog, name):
    def body(a_ref, b_ref, al_ref, o_ref):
        o_ref[...] = (a_ref[...] + b_ref[...]) * (-jnp.exp(al_ref[...]))
    return pl.pallas_call(body, out_shape=S((1, N_SSM_HEADS), F32), name=name)(da, daT_row, alog.reshape(1, -1))


def _place():
    x, y, c = lax.axis_index("x"), lax.axis_index("y"), lax.axis_index("c")
    return x, y, c, [(1 - x, y), (x, 1 - y), (1 - x, 1 - y)]


def _all_gather(shards, name):
    npc = len(shards)

    def body(*refs):
        x_refs, o_refs = refs[:npc], refs[npc:2 * npc]
        send_sems, recv_sems, local_sems = refs[2 * npc:]
        x, y, c, chips = _place()
        me, sib = (x, y, c), (x, y, 1 - c)

        def rows(dev, i):
            return o_refs[i].at[4 * dev[0] + 2 * dev[1] + dev[2]]

        def copy(k, i, block, to, src=None):
            return pltpu.make_async_remote_copy(
                src_ref=rows(block, i) if src is None else src, dst_ref=rows(block, i),
                send_sem=send_sems.at[k, i], recv_sem=recv_sems.at[k, i], device_id=to, device_id_type=MESH)

        mine = [pltpu.make_async_copy(x_refs[i], rows(me, i), local_sems.at[i]) for i in range(npc)]
        for cp in mine:
            cp.start()
        first = []
        for i in range(npc):
            first.append(copy(0, i, me, sib, src=x_refs[i]))
            first += [copy(1 + j, i, me, (*chip, c), src=x_refs[i]) for j, chip in enumerate(chips)]
        for cp in first:
            cp.start()
        passed = []
        for i in range(npc):
            for j, chip in enumerate(chips):
                copy(1 + j, i, (*chip, c), me).wait_recv()
                cp = copy(4 + j, i, (*chip, c), sib)
                cp.start()
                passed.append(cp)
        for i in range(npc):
            copy(0, i, sib, me).wait_recv()
            for j, chip in enumerate(chips):
                copy(4 + j, i, (*chip, 1 - c), me).wait_recv()
        for cp in first + passed:
            cp.wait_send()
        for cp in mine:
            cp.wait()

    anys = pl.BlockSpec(memory_space=pl.ANY)
    return pl.pallas_call(
        body, in_specs=[anys] * npc, out_specs=[anys] * npc, out_shape=[S((N_DEV,) + s.shape, s.dtype) for s in shards],
        scratch_shapes=[pltpu.SemaphoreType.DMA((7, npc)), pltpu.SemaphoreType.DMA((7, npc)), pltpu.SemaphoreType.DMA((npc,))],
        name=name)(*shards)


def _to_sibling(to_sib, name):
    npc = len(to_sib)

    def body(*refs):
        s_refs, o_refs, send_sems, recv_sems = refs[:npc], refs[npc:2 * npc], refs[2 * npc], refs[2 * npc + 1]
        x, y, c, _ = _place()
        cps = [pltpu.make_async_remote_copy(
            src_ref=s_refs[i], dst_ref=o_refs[i], send_sem=send_sems.at[i], recv_sem=recv_sems.at[i],
            device_id=(x, y, 1 - c), device_id_type=MESH) for i in range(npc)]
        for cp in cps:
            cp.start()
        for cp in cps:
            cp.wait()

    anys = pl.BlockSpec(memory_space=pl.ANY)
    return pl.pallas_call(
        body, in_specs=[anys] * npc, out_specs=[anys] * npc, out_shape=[S(s.shape, s.dtype) for s in to_sib],
        scratch_shapes=[pltpu.SemaphoreType.DMA((npc,)), pltpu.SemaphoreType.DMA((npc,))],
        name=name)(*to_sib)


def _rs_begin(pieces, name):
    c = lax.axis_index("c")
    by_core = [p.reshape(4, 2, p.shape[1], p.shape[2]) for p in pieces]
    to_sib = [lax.dynamic_index_in_dim(p, 1 - c, axis=1, keepdims=False).astype(BF16) for p in by_core]
    keep = [lax.dynamic_index_in_dim(p, c, axis=1, keepdims=False) for p in by_core]
    from_sib = _to_sibling(to_sib, name + "_d2d")

    def add1(a, b):
        s = a + b
        return [s, s], []

    parts, parts_b = [], []
    for i, (k, f) in enumerate(zip(keep, from_sib)):
        _, r, C = k.shape
        p, pb = _rowwise(add1, [k.reshape(4 * r, C), f.reshape(4 * r, C)], [], [(C, F32), (C, BF16)], tm=2048, name=f"{name}_add1_{i}")
        parts.append(p.reshape(4, r, C))
        parts_b.append(pb.reshape(4, r, C))
    return parts, parts_b


def _rs_finish(parts, got, name):
    x, y = lax.axis_index("x"), lax.axis_index("y")

    def add2(a, b, c_, d_):
        return [((a + b) + c_) + d_], []

    outs = []
    for i, (p, g) in enumerate(zip(parts, got)):
        own = lax.dynamic_index_in_dim(p, 2 * x + y, axis=0, keepdims=False)
        outs.append(_rowwise(add2, [own, g[0], g[1], g[2]], [], [(p.shape[2], F32)], tm=2048, name=f"{name}_add2_{i}")[0])
    return outs


class _GradReduce:
    EARLY = ("w_ffn_in", "w_ffn_out", "w_out", "w_attn_branch", "w_ssm_branch")

    def __init__(self):
        self.out, self.keys, self.parts, self.parts_b = {}, [], [], []

    def _begin(self, l, names, gr, name):
        parts, parts_b = _rs_begin([_shard(nm, gr[nm]) for nm in names], name)
        self.keys += [(l, nm) for nm in names]
        self.parts += parts
        self.parts_b += parts_b

    def carry_fn(self, l):
        if l != 0:
            return None

        def fn(gr):
            self._begin(0, self.EARLY, gr, "rs_early_l0")
            return _comm_to_chips(self.parts_b)
        return fn

    def carry2_fn(self, l):
        if l != 0:
            return None

        def fn(gr):
            self.parts2, parts2_b = _rs_begin([_shard("w_in", gr["w_in"])], "rs_w_in_l0")
            return _comm_to_chips(parts2_b)
        return fn

    def done(self, l, gr, carried):
        if l == DEPTH - 1:
            self._begin(l, BIG, gr, f"rs_l{l}")
            return
        for key, o in zip(self.keys, _rs_finish(self.parts, carried[0], "rs_carried")):
            self.out[key] = o
        self.out[(0, "w_in")] = _rs_finish(self.parts2, carried[1], "rs_w_in_l0")[0]


def _all_reduce_small(v, name):
    R, C = v.shape

    def body(x_ref, out_ref, buf, send_sems, recv_sems):
        x, y, c, chips = _place()
        me, sib = (x, y, c), (x, y, 1 - c)

        def rows(dev):
            return buf.at[4 * dev[0] + 2 * dev[1] + dev[2]]

        def copy(k, block, to, src=None):
            return pltpu.make_async_remote_copy(
                src_ref=rows(block) if src is None else src, dst_ref=rows(block),
                send_sem=send_sems.at[k], recv_sem=recv_sems.at[k], device_id=to, device_id_type=MESH)

        buf[4 * x + 2 * y + c] = x_ref[...]
        first = [copy(0, me, sib, src=x_ref)] + [copy(1 + j, me, (*chip, c), src=x_ref) for j, chip in enumerate(chips)]
        for cp in first:
            cp.start()
        passed = [copy(4 + j, (*chip, c), sib) for j, chip in enumerate(chips)]
        for j, chip in enumerate(chips):
            copy(1 + j, (*chip, c), me).wait_recv()
            passed[j].start()
        copy(0, sib, me).wait_recv()
        for j, chip in enumerate(chips):
            copy(4 + j, (*chip, 1 - c), me).wait_recv()
        for cp in first + passed:
            cp.wait_send()
        acc = buf[0]
        for j in range(1, N_DEV):
            acc = acc + buf[j]
        out_ref[...] = acc

    vm = pl.BlockSpec(memory_space=pltpu.VMEM)
    return pl.pallas_call(
        body, in_specs=[vm], out_specs=vm, out_shape=S((R, C), F32),
        scratch_shapes=[pltpu.VMEM((N_DEV, R, C), F32), pltpu.SemaphoreType.DMA((7,)), pltpu.SemaphoreType.DMA((7,))],
        compiler_params=pltpu.CompilerParams(vmem_limit_bytes=VMEM_LIMIT), name=name)(v)


SEG = (("q", 0, 1536), ("k", 1536, 1536), ("v", 3072, 1536), ("z", 4608, 2048), ("xbc", 6656, 3072), ("dt", 9728, 32), ("gl", 9760, 2048))


def _split_w_in(w_in_full):
    out = {}
    for nm, off, n in SEG:
        w = w_in_full[:, off:off + n]
        if nm == "dt":
            w = jnp.pad(w, ((0, 0), (0, LANES - n)))
        out[nm] = w
    W = ATTN_OUT_WIDTH
    out["qkv"] = [jnp.concatenate([out[s][:, g * W:(g + 1) * W] for s in ("q", "k", "v")], axis=1) for g in range(N_DIL)]
    out["qkv_t"] = [[out[s][:, g * W:(g + 1) * W] for s in ("q", "k", "v")] for g in range(N_DIL)]
    return out


def _layer_fwd(h, p, W, biases, cs, l, carry=None, late=None):
    T = h.shape[0]
    nm = lambda s: f"{s}_l{l}"
    sv = {"h_in": h}
    xns = _rmsnorm_fwd(h, p["norm1_w"], nm("norm1"), dils=[d for _, d in DILATED_GROUPS[1:]])
    xn = xns[0]
    wi = W["w_in"]
    z = _mm(xn, wi["z"], out_dtype=BF16, name=nm("proj_z"))
    xbc = _mm(xn, wi["xbc"], name=nm("proj_xbc"))
    dt_raw = _mm(xn, wi["dt"], name=nm("proj_dt"))
    gl = _mm(xn, wi["gl"], out_dtype=BF16, name=nm("proj_gl"))
    conv_args = (xbc, p["conv_w"], p["conv_b"].reshape(1, -1), nm("conv"))
    if late is None:
        u_conv, act = _conv_fwd(*conv_args)
    else:
        (u_conv, act), spread = _conv_fwd(*conv_args, comm=late[0])
    os_, ls, qkvs = [], [], []
    for g, (window, dil) in enumerate(DILATED_GROUPS):
        qkv = _mm(xns[g], wi["qkv"][g], out_dtype=BF16, name=nm(f"proj_qkv_g{g}"))
        if late is not None and g == 0:
            (o, lse), arrived = _attn_fwd(qkv, biases[g], dil, nm(f"attn_fwd_g{g}"), comm=_comm_gather_pass(spread))
            W.update(late[1](arrived))
        else:
            o, lse = _attn_fwd(qkv, biases[g], dil, nm(f"attn_fwd_g{g}"))
        os_.append(o)
        ls.append(lse)
        qkvs.append(qkv)
    attn_b, attn_f = _combine_fwd(os_, ls, nm("combine"))
    dt = _dt_fwd(dt_raw, p["dt_bias"], nm("dt"))
    dtT = dt.T
    dskip_e = jnp.repeat(p["d_skip"], D_INNER // N_SSM_HEADS).reshape(1, -1)
    carried = None
    if carry is None:
        y, states = _ssd_fwd(act, dt, dtT, p["a_log"], dskip_e, cs, nm("ssd_fwd"))
        ssm = _ssm_norm_fwd(y, z, p["ssm_norm_w"], nm("ssm_norm"))
    else:
        (y, states), spread = _ssd_fwd(act, dt, dtT, p["a_log"], dskip_e, cs, nm("ssd_fwd"), comm=carry)
        ssm, carried = _ssm_norm_fwd(y, z, p["ssm_norm_w"], nm("ssm_norm"), comm=_comm_gather_pass(spread))
    a_br = _mm(attn_b, W["w_attn_branch"], out_dtype=BF16, name=nm("attn_branch"))
    s_br = _mm(ssm, W["w_ssm_branch"], out_dtype=BF16, name=nm("ssm_branch"))
    h_mid, merged = _gate_out_proj(a_br, s_br, gl, h, W["w_out"], nm("gate_out_proj"))
    xn2 = _rmsnorm_fwd(h_mid, p["norm2_w"], nm("norm2"))[0]
    W["w_ffn_in_p"] = _ffn_perm(W["w_ffn_in"])
    u_ffn, ffn_act = _mm(xn2, W["w_ffn_in_p"], tm=512, tn=D_FF, epilogue=_swiglu_epilogue, outs=[(2 * D_FF, BF16), (D_FF, BF16)],
                         name=nm("ffn_in_swiglu"))
    h_out = _mm(ffn_act, W["w_ffn_out"], acc=h_mid, name=nm("ffn_out"))
    sv.update(xn=xn, xns=xns, qkvs=qkvs, z=z, xbc=xbc, dt_raw=dt_raw, gl=gl, ls=ls, attn_b=attn_b, attn_f=attn_f, u_conv=u_conv,
              act=act, dt=dt, dtT=dtT, dskip_e=dskip_e, y=y, states=states, ssm=ssm, a_br=a_br, s_br=s_br, merged=merged,
              h_mid=h_mid, xn2=xn2, u_ffn=u_ffn, ffn_act=ffn_act)
    return h_out, sv, carried


def _layer_bwd(dh, sv, p, W, biases, cs, head_ones, l, carry_fn=None, carry2_fn=None):
    T = dh.shape[0]
    nm = lambda s: f"{s}_l{l}"
    gr = {}
    du = _mm(dh, W["w_ffn_out"], tb=True, tm=512, tn=FFN_HALF, extras=[sv["u_ffn"]], epilogue=_dswiglu_epilogue, outs=[(2 * D_FF, BF16)],
             name=nm("d_ffn_act_swiglu"))
    gr["w_ffn_out"] = _mm(sv["ffn_act"], dh, ta=True, name=nm("g_ffn_out"))
    dxn2 = _mm(du, W["w_ffn_in_p"], tb=True, name=nm("d_xn2"))
    gr["w_ffn_in"] = _ffn_unperm(_mm(sv["xn2"], du, ta=True, name=nm("g_ffn_in")))
    dh_mid, gr["norm2_w"] = _rmsnorm_bwd(dxn2, sv["h_mid"], p["norm2_w"], dh, nm("d_norm2"))
    gr["w_out"] = _mm(sv["merged"], dh_mid, ta=True, name=nm("g_out"))
    d_a, d_s, dgl = _d_out_proj_gate(dh_mid, sv["a_br"], sv["s_br"], sv["gl"], W["w_out"], nm("d_out_proj_gate"))
    dattn = _mm(d_a, W["w_attn_branch"], tb=True, out_dtype=BF16, name=nm("d_attn"))
    gr["w_attn_branch"] = _mm(sv["attn_b"], d_a, ta=True, name=nm("g_attn_branch"))
    dssm = _mm(d_s, W["w_ssm_branch"], tb=True, out_dtype=BF16, name=nm("d_ssm"))
    gr["w_ssm_branch"] = _mm(sv["ssm"], d_s, ta=True, name=nm("g_ssm_branch"))
    dy, dz, gr["ssm_norm_w"] = _ssm_norm_bwd(dssm, sv["y"], sv["z"], p["ssm_norm_w"], nm("d_ssm_norm"))
    ssd_args = (dy, sv["act"], sv["dt"], sv["dtT"], p["a_log"], sv["dskip_e"], sv["states"], cs, nm("ssd_bwd"))
    carried = None
    if carry_fn is None:
        dact_c, ddt_a, ddt_bT, da, daT, dskip = _ssd_bwd(*ssd_args)
    else:
        (dact_c, ddt_a, ddt_bT, da, daT, dskip), carried = _ssd_bwd(*ssd_args, comm=carry_fn(gr))
    gr["a_log"] = _a_log_grad(da, daT.T, p["a_log"], nm("g_a_log")).reshape(-1)
    gr["d_skip"] = dskip.reshape(-1)
    ddt_raw, ddt_bias = _dt_bwd(ddt_a, ddt_bT.T, sv["dt_raw"], p["dt_bias"], nm("d_dt"))
    gr["dt_bias"] = ddt_bias.reshape(-1)
    dxbc, gr["conv_w"], dconv_b = _conv_bwd(dact_c, sv["u_conv"], sv["xbc"], p["conv_w"], nm("d_conv"))
    gr["conv_b"] = dconv_b.reshape(-1)
    outs = _combine_bwd(dattn, sv["attn_f"], sv["ls"], head_ones, nm("d_combine"))
    wi = W["w_in"]
    dbias, dxn = [], None
    gqkv = [[None] * N_DIL for _ in range(3)]
    for g, (window, dil) in enumerate(DILATED_GROUPS):
        dq, dk, dv, db = _attn_bwd(sv["qkvs"][g], biases[g], sv["ls"][g], outs[2 * g], outs[2 * g + 1], dil, nm(f"attn_bwd_g{g}"))
        dbias.append(db)
        dxn = _mm_dil([dq, dk, dv], wi["qkv_t"][g], dil, dxn, nm(f"d_xn_qkv_g{g}"))
        for i, dseg in enumerate((dq, dk, dv)):
            gqkv[i][g] = _mm(sv["xns"][g], dseg, ta=True, name=nm(f"g_in_{'qkv'[i]}_g{g}"))
    parts = (("z", dz), ("xbc", dxbc), ("dt", ddt_raw), ("gl", dgl))
    gws = gqkv[0] + gqkv[1] + gqkv[2]
    for sname, dseg in parts:
        gw = _mm(sv["xn"], dseg, ta=True, name=nm("g_in_" + sname))
        gws.append(gw[:, :N_SSM_HEADS] if sname == "dt" else gw)
    gr["w_in"] = jnp.concatenate(gws, axis=1)
    carried2 = None
    for sname, dseg in parts:
        if carry2_fn is not None and sname == "xbc":
            dxn, carried2 = _mm(dseg, wi[sname], tb=True, acc=dxn, name=nm("d_xn_" + sname), comm=carry2_fn(gr))
        else:
            dxn = _mm(dseg, wi[sname], tb=True, acc=dxn, name=nm("d_xn_" + sname))
    dh_in, gr["norm1_w"] = _rmsnorm_bwd(dxn, sv["h_in"], p["norm1_w"], dh_mid, nm("d_norm1"))
    return dh_in, gr, dbias, (carried, carried2)


def _step_local(x, tgt, small, Wfull, rel_bias, final_norm_w, prefetch=None, grad_reduce=None, late0=None):
    cs = _ssd_consts()
    head = np.repeat(np.arange(HEADS_PER_GROUP), HEAD_DIM)
    head_ones = jnp.asarray(head[:, None] == head[None, :], BF16)
    biases, onehots = [], []
    for g, (window, dil) in enumerate(DILATED_GROUPS):
        onehot, valid = _bias_consts(dil, window // dil)
        rel_g_t = rel_bias[:, g * HEADS_PER_GROUP:(g + 1) * HEADS_PER_GROUP].T
        b = _bias_gather(rel_g_t, onehot, valid, f"bias_gather_g{g}")
        biases.append(b.reshape(HEADS_PER_GROUP, ATTN_BLOCK, 2 * ATTN_BLOCK))
        onehots.append(onehot)
    h, saved, carried = x, [], None
    Wfull = list(Wfull)
    for l in range(DEPTH):
        W = dict(prefetch[1](carried) if Wfull[l] is None else Wfull[l])
        W["w_in"] = _split_w_in(W["w_in"])
        Wfull[l] = W
        first = prefetch is not None and l == 0
        h, sv, carried = _layer_fwd(h, small[l], W, biases, cs, l, prefetch[0] if first else None, late0 if l == 0 else None)
        saved.append(sv)
    dh, g_final, loss = _loss_head(h, final_norm_w, tgt, "loss_head")
    grads = [None] * DEPTH
    dbias_tot = [None] * N_DIL
    for l in reversed(range(DEPTH)):
        carry_fn = grad_reduce.carry_fn(l) if grad_reduce is not None else None
        carry2_fn = grad_reduce.carry2_fn(l) if grad_reduce is not None else None
        dh, grads[l], dbias, carried = _layer_bwd(dh, saved[l], small[l], Wfull[l], biases, cs, head_ones, l, carry_fn, carry2_fn)
        if grad_reduce is not None:
            grad_reduce.done(l, grads[l], carried)
        for g in range(N_DIL):
            dbias_tot[g] = dbias[g] if dbias_tot[g] is None else dbias_tot[g] + dbias[g]
    d_rel = jnp.concatenate(
        [_bias_scatter(dbias_tot[g].reshape(HEADS_PER_GROUP, -1), onehots[g], f"bias_scatter_g{g}").T for g in range(N_DIL)], axis=1)
    return loss, dh, grads, d_rel, g_final


def _unshard(nm, g):
    _, rows, cols = g.shape
    if nm in COL_SHARDED:
        return g.transpose(1, 0, 2).reshape(rows, N_DEV * cols)
    return g.reshape(N_DEV * rows, cols)


def _shard(nm, w):
    rows, cols = w.shape
    if nm in COL_SHARDED:
        return w.reshape(rows, N_DEV, cols // N_DEV).transpose(1, 0, 2)
    return w.reshape(N_DEV, rows // N_DEV, cols)


SMALL_LAYER = (("norm1_w", 1024), ("conv_w", 12288), ("conv_b", 3072), ("dt_bias", 32), ("a_log", 32), ("d_skip", 32),
               ("ssm_norm_w", 2048), ("norm2_w", 1024))
SMALL_GLOBAL = (("rel_bias", 768), ("final_norm_w", 1024), ("loss", 1))


def _pad128(v):
    n = v.shape[0]
    return jnp.pad(v, (0, -n % LANES))


def _pack_small(per_layer, glob):
    parts = [_pad128(per_layer[l][nm].reshape(-1)) for l in range(DEPTH) for nm, _ in SMALL_LAYER]
    parts += [_pad128(glob[nm].reshape(-1)) for nm, _ in SMALL_GLOBAL]
    flat = jnp.concatenate(parts)
    flat = jnp.pad(flat, (0, -flat.shape[0] % (8 * LANES)))
    return flat.reshape(-1, LANES)


def _unpack_small(packed):
    flat = packed.reshape(-1)
    per_layer, glob, off = [dict() for _ in range(DEPTH)], {}, 0
    for l in range(DEPTH):
        for nm, n in SMALL_LAYER:
            per_layer[l][nm] = flat[off:off + n]
            off += n + (-n % LANES)
    for nm, n in SMALL_GLOBAL:
        glob[nm] = flat[off:off + n]
        off += n + (-n % LANES)
    return per_layer, glob


def kernel(x, norm1_w, w_in, conv_w, conv_b, dt_bias, a_log, d_skip, ssm_norm_w, w_attn_branch, w_ssm_branch, w_out, norm2_w, w_ffn_in, w_ffn_out, rel_bias, final_norm_w, loss_target, m_norm1_w, m_w_in, m_conv_w, m_conv_b, m_dt_bias, m_a_log, m_d_skip, m_ssm_norm_w, m_w_attn_branch, m_w_ssm_branch, m_w_out, m_norm2_w, m_w_ffn_in, m_w_ffn_out, m_rel_bias, m_final_norm_w, v_norm1_w, v_w_in, v_conv_w, v_conv_b, v_dt_bias, v_a_log, v_d_skip, v_ssm_norm_w, v_w_attn_branch, v_w_ssm_branch, v_w_out, v_norm2_w, v_w_ffn_in, v_w_ffn_out, v_rel_bias, v_final_norm_w):
    big = dict(w_in=w_in, w_attn_branch=w_attn_branch, w_ssm_branch=w_ssm_branch, w_out=w_out, w_ffn_in=w_ffn_in, w_ffn_out=w_ffn_out)
    big_m = dict(w_in=m_w_in, w_attn_branch=m_w_attn_branch, w_ssm_branch=m_w_ssm_branch, w_out=m_w_out, w_ffn_in=m_w_ffn_in, w_ffn_out=m_w_ffn_out)
    big_v = dict(w_in=v_w_in, w_attn_branch=v_w_attn_branch, w_ssm_branch=v_w_ssm_branch, w_out=v_w_out, w_ffn_in=v_w_ffn_in, w_ffn_out=v_w_ffn_out)
    sm = dict(norm1_w=norm1_w, conv_w=conv_w, conv_b=conv_b, dt_bias=dt_bias, a_log=a_log, d_skip=d_skip, ssm_norm_w=ssm_norm_w, norm2_w=norm2_w)
    sm_m = dict(norm1_w=m_norm1_w, conv_w=m_conv_w, conv_b=m_conv_b, dt_bias=m_dt_bias, a_log=m_a_log, d_skip=m_d_skip, ssm_norm_w=m_ssm_norm_w, norm2_w=m_norm2_w)
    sm_v = dict(norm1_w=v_norm1_w, conv_w=v_conv_w, conv_b=v_conv_b, dt_bias=v_dt_bias, a_log=v_a_log, d_skip=v_d_skip, ssm_norm_w=v_ssm_norm_w, norm2_w=v_norm2_w)
    me = 4 * lax.axis_index("x") + 2 * lax.axis_index("y") + lax.axis_index("c")

    def full_weights(gathered, names=BIG):
        return {nm: _unshard(nm, g) for nm, g in zip(names, gathered)}

    later = [nm for nm in BIG if nm != "w_in"]
    Wfull = [full_weights(_all_gather([big["w_in"][0].astype(BF16)], "all_gather_w_in_l0"), ["w_in"]), None]
    late0 = (_comm_gather_spread([big[nm][0].astype(BF16) for nm in later]), functools.partial(full_weights, names=later))
    prefetch = (_comm_gather_spread([big[nm][DEPTH - 1].astype(BF16) for nm in BIG]), full_weights)

    conv_full = []
    for l in range(DEPTH):
        z = jnp.zeros((N_DEV, CONV_WIDTH, XBC_WIDTH // N_DEV), F32)
        conv_full.append(lax.dynamic_update_index_in_dim(z, conv_w[l], me, axis=0))
    cw = jnp.stack(conv_full).reshape(-1, LANES)
    cw = _all_reduce_small(cw, "gather_conv_w").reshape(DEPTH, N_DEV, CONV_WIDTH, XBC_WIDTH // N_DEV)
    cw = cw.transpose(0, 2, 1, 3).reshape(DEPTH, CONV_WIDTH, XBC_WIDTH)

    small = [{nm: (cw[l] if nm == "conv_w" else a[l]) for nm, a in sm.items()} for l in range(DEPTH)]
    grad_reduce = _GradReduce()
    loss, dx, grads, d_rel, g_final = _step_local(x[0], loss_target[0], small, Wfull, rel_bias, final_norm_w, prefetch, grad_reduce, late0)
    g_big = {nm: jnp.stack([grad_reduce.out[(l, nm)] for l in range(DEPTH)]) for nm in BIG}

    per_layer = [{nm: grads[l][nm] for nm, _ in SMALL_LAYER} for l in range(DEPTH)]
    packet = _pack_small(per_layer, dict(rel_bias=d_rel, final_norm_w=g_final, loss=loss[0, :1]))
    per_layer, glob = _unpack_small(_all_reduce_small(packet, "all_reduce_small"))
    g_small = {nm: jnp.stack([per_layer[l][nm] for l in range(DEPTH)]) for nm, _ in SMALL_LAYER}
    cwg = g_small["conv_w"].reshape(DEPTH, CONV_WIDTH, N_DEV, XBC_WIDTH // N_DEV)
    g_small["conv_w"] = lax.dynamic_index_in_dim(cwg, me, axis=2, keepdims=False)
    for nm in sm:
        g_small[nm] = g_small[nm].reshape(sm[nm].shape)
    g_rel = glob["rel_bias"].reshape(rel_bias.shape)
    g_fin = glob["final_norm_w"]
    loss_out = glob["loss"][0]

    def adam(w, g, m, v, name):
        shp = w.shape
        two = lambda a: a.reshape(-1, shp[-1]) if a.ndim > 1 else a.reshape(1, -1)
        d, nm_, nv = _adamw(two(w), two(g), two(m), two(v), name)
        return d.reshape(shp), nm_.reshape(shp), nv.reshape(shp)

    order = ["norm1_w", "w_in", "conv_w", "conv_b", "dt_bias", "a_log", "d_skip", "ssm_norm_w", "w_attn_branch", "w_ssm_branch",
             "w_out", "norm2_w", "w_ffn_in", "w_ffn_out", "rel_bias", "final_norm_w"]
    allw = {**big, **sm, "rel_bias": rel_bias, "final_norm_w": final_norm_w}
    allm = {**big_m, **sm_m, "rel_bias": m_rel_bias, "final_norm_w": m_final_norm_w}
    allv = {**big_v, **sm_v, "rel_bias": v_rel_bias, "final_norm_w": v_final_norm_w}
    allg = {**g_big, **g_small, "rel_bias": g_rel, "final_norm_w": g_fin}
    deltas, new_m, new_v = [], [], []
    for nm in order:
        d, a, b = adam(allw[nm], allg[nm], allm[nm], allv[nm], "adamw_" + nm)
        deltas.append(d)
        new_m.append(a)
        new_v.append(b)
    return (loss_out, dx[None], *[allg[nm] for nm in order], *deltas, *new_m, *new_v)
```

```python
import functools
import math

import numpy as np
import jax
import jax.numpy as jnp
from jax import lax
from jax.experimental import pallas as pl
from jax.experimental.pallas import tpu as pltpu

F32, BF16 = jnp.float32, jnp.bfloat16
S = jax.ShapeDtypeStruct
MESH = pl.DeviceIdType.MESH

D_MODEL = 1024
DEPTH = 2
HEAD_DIM = 64
DILATED_GROUPS = ((128, 1), (512, 4), (2048, 16))
N_DIL = 3
HEADS_PER_GROUP = 8
ATTN_WIDTH = 1536
ATTN_OUT_WIDTH = 512
ATTN_BLOCK = 128
N_REL_BUCKETS = 32
REL_MAX_DISTANCE = 2048
D_INNER = 2048
N_SSM_HEADS = 32
N_SSM_GROUPS = 4
D_STATE = 128
CONV_WIDTH = 4
SSD_CHUNK = 128
XBC_WIDTH = 3072
D_FF = 2816
EPS = 1e-6
ADAM_LR, ADAM_B1, ADAM_B2, ADAM_EPS, ADAM_WD, ADAM_STEP = 0.001, 0.9, 0.999, 1e-08, 0.01, 10

N_DEV = 8
LANES = 128
VMEM_LIMIT = 56 * 1024 * 1024
MM_VMEM_BYTES = 40 * 1024 * 1024
ROW_TILES_BYTES = 36 * 1024 * 1024
NEG = -1e30
BIG = ("w_in", "w_attn_branch", "w_ssm_branch", "w_out", "w_ffn_in", "w_ffn_out")
COL_SHARDED = ("w_in", "w_attn_branch", "w_ffn_in")

NT = (((1,), (1,)), ((), ()))
TN = (((0,), (0,)), ((), ()))


def _cparams(sem=None):
    return pltpu.CompilerParams(dimension_semantics=sem, vmem_limit_bytes=VMEM_LIMIT)


def _pick(n, target, mult=LANES):
    best = None
    for t in range(mult, min(n, target) + 1, mult):
        if n % t == 0:
            best = t
    return best or n


def _silu(x):
    return x * jax.nn.sigmoid(x)


def _dsilu(x):
    s = jax.nn.sigmoid(x)
    return s * (1.0 + x * (1.0 - s))


def _split2(x):
    hi = x.astype(BF16)
    lo = (x - hi.astype(F32)).astype(BF16)
    return hi, lo


def _split3(x):
    x1 = x.astype(BF16)
    r1 = x - x1.astype(F32)
    x2 = r1.astype(BF16)
    x3 = (r1 - x2.astype(F32)).astype(BF16)
    return x1, x2, x3


def _dotx_r(x, m, parts=3):
    xs = _split3(x) if parts == 3 else _split2(x)
    out = jnp.dot(xs[0], m, preferred_element_type=F32)
    for xi in xs[1:]:
        out = out + jnp.dot(xi, m, preferred_element_type=F32)
    return out


def _dotx_l(m, x, parts=3):
    xs = _split3(x) if parts == 3 else _split2(x)
    out = jnp.dot(m, xs[0], preferred_element_type=F32)
    for xi in xs[1:]:
        out = out + jnp.dot(m, xi, preferred_element_type=F32)
    return out


def _mm(a, b, *, ta=False, tb=False, out_dtype=F32, acc=None, name, tm=None, tn=1536, tk=1536, extras=(), epilogue=None, outs=None, comm=None):
    M, K = (a.shape[1], a.shape[0]) if ta else a.shape
    N = b.shape[0] if tb else b.shape[1]
    tn, tk = _pick(N, tn), _pick(K, tk)
    if tm is None:
        def vmem(t, k):
            out_b = jnp.dtype(out_dtype).itemsize
            return (2 * t * k * a.dtype.itemsize + 2 * k * tn * b.dtype.itemsize + 2 * t * tn * out_b
                    + (t * tn * 4 if K > k else 0) + (2 * t * tn * acc.dtype.itemsize if acc is not None else 0))
        tm = _pick(M, 1536)
        while M % (2 * tm) == 0 and vmem(2 * tm, tk) <= MM_VMEM_BYTES:
            tm *= 2
        while K % (2 * tk) == 0 and vmem(tm, 2 * tk) <= MM_VMEM_BYTES:
            tk *= 2
    else:
        tm = _pick(M, tm)
    nk = K // tk
    dims = (((0 if ta else 1,), (1 if tb else 0,)), ((), ()))
    has_acc = acc is not None
    outs = [(N, out_dtype)] if outs is None else outs
    ne, no = len(extras), len(outs)

    def body(*refs):
        a_ref, b_ref = refs[:2]
        c_ref = refs[2] if has_acc else None
        e_refs = refs[2 + has_acc:2 + has_acc + ne]
        o_refs = refs[2 + has_acc + ne:2 + has_acc + ne + no]
        acc_ref = refs[-1]
        k = pl.program_id(2)
        part = lax.dot_general(a_ref[...].astype(BF16), b_ref[...].astype(BF16), dims, preferred_element_type=F32)

        def finish(res):
            tiles = [res] if epilogue is None else epilogue(res, *[e[...] for e in e_refs])
            for o_ref, t in zip(o_refs, tiles):
                o_ref[...] = t.astype(o_ref.dtype)

        if nk == 1:
            finish(part + c_ref[...].astype(F32) if has_acc else part)
        else:
            @pl.when(k == 0)
            def _():
                acc_ref[...] = part + c_ref[...].astype(F32) if has_acc else part

            @pl.when(jnp.logical_and(k > 0, k < nk - 1))
            def _():
                acc_ref[...] += part

            @pl.when(k == nk - 1)
            def _():
                finish(acc_ref[...] + part)

    def cspec(cols):
        return pl.BlockSpec((tm, cols * tn // N), lambda i, j, k: (i, j))

    a_spec = pl.BlockSpec((tk, tm), lambda i, j, k: (k, i)) if ta else pl.BlockSpec((tm, tk), lambda i, j, k: (i, k))
    b_spec = pl.BlockSpec((tn, tk), lambda i, j, k: (j, k)) if tb else pl.BlockSpec((tk, tn), lambda i, j, k: (k, j))
    in_specs, args = [a_spec, b_spec], [a, b]
    if has_acc:
        in_specs.append(cspec(N))
        args.append(acc)
    in_specs += [cspec(e.shape[1]) for e in extras]
    args += list(extras)
    res, comm_res = _carrier_call(
        body, (M // tm, N // tn, nk), in_specs, [cspec(c) for c, _ in outs], [S((M, c), dt) for c, dt in outs],
        [pltpu.VMEM((tm, tn), F32)] if nk > 1 else [], args, comm, name)
    res = res[0] if len(outs) == 1 else res
    return res if comm is None else (res, comm_res)


def _mm_dil(a_list, b_list, d, acc, name, tm=1024):
    T, K = a_list[0].shape
    N = b_list[0].shape[0]
    tm, tn = min(tm, T), _pick(N, 1024)
    na = len(a_list)
    has_acc = acc is not None

    def body(*refs):
        a_refs, b_refs, rest = refs[:na], refs[na:2 * na], refs[2 * na:]
        c_ref = rest[0] if has_acc else None
        o_ref, scr = rest[-2], rest[-1]
        out = c_ref[...] if has_acc else None
        for a_ref, b_ref in zip(a_refs, b_refs):
            a_tok = _dil_to_tok(scr, a_ref, d).astype(BF16) if d > 1 else a_ref[...]
            part = lax.dot_general(a_tok, b_ref[...], NT, preferred_element_type=F32)
            out = part if out is None else out + part
        o_ref[...] = out

    if d > 1:
        a_spec = pl.BlockSpec((d, tm // d, K), lambda i, j: (0, i, 0))
        a_args = [a.reshape(d, T // d, K) for a in a_list]
    else:
        a_spec = pl.BlockSpec((tm, K), lambda i, j: (i, 0))
        a_args = list(a_list)
    o_spec = pl.BlockSpec((tm, tn), lambda i, j: (i, j))
    in_specs = [a_spec] * na + [pl.BlockSpec((tn, K), lambda i, j: (j, 0))] * na + ([o_spec] if has_acc else [])
    return pl.pallas_call(
        body, grid=(T // tm, N // tn), in_specs=in_specs, out_specs=o_spec, out_shape=S((T, N), F32),
        scratch_shapes=[pltpu.VMEM((K // LANES, tm, LANES), F32)],
        compiler_params=_cparams(("parallel", "parallel")), name=name)(*a_args, *b_list, *([acc] if has_acc else []))


class _Comm:
    def __init__(self, ins, outs, sems, start, wait, alias=None):
        self.ins, self.outs, self.sems, self.start, self.wait, self.alias = list(ins), list(outs), list(sems), start, wait, alias or {}


def _carrier_call(body, grid, in_specs, out_specs, out_shape, scratch_shapes, args, comm, name):
    grid = (grid,) if isinstance(grid, int) else tuple(grid)
    seq = ("arbitrary",) * len(grid)
    ni, no, ns = len(in_specs), len(out_specs), len(scratch_shapes)
    if comm is None:
        res = pl.pallas_call(body, grid=grid, in_specs=in_specs, out_specs=out_specs, out_shape=out_shape,
                             scratch_shapes=scratch_shapes, compiler_params=_cparams(seq), name=name)(*args)
        return list(res), []
    ci, co = len(comm.ins), len(comm.outs)

    def wrapped(*refs):
        ins, cins = refs[:ni], refs[ni:ni + ci]
        outs, couts = refs[ni + ci:ni + ci + no], refs[ni + ci + no:ni + ci + no + co]
        scr, csems = refs[ni + ci + no + co:ni + ci + no + co + ns], refs[ni + ci + no + co + ns:]
        ids = [pl.program_id(i) for i in range(len(grid))]
        first = functools.reduce(jnp.logical_and, [i == 0 for i in ids])
        last = functools.reduce(jnp.logical_and, [i == g - 1 for i, g in zip(ids, grid)])

        @pl.when(first)
        def _():
            comm.start(cins, couts, csems)

        body(*ins, *outs, *scr)

        @pl.when(last)
        def _():
            comm.wait(cins, couts, csems)

    anys = pl.BlockSpec(memory_space=pl.ANY)
    res = pl.pallas_call(
        wrapped, grid=grid, in_specs=list(in_specs) + [anys] * ci, out_specs=list(out_specs) + [anys] * co,
        out_shape=list(out_shape) + comm.outs, scratch_shapes=list(scratch_shapes) + comm.sems,
        input_output_aliases={ni + a: no + b for a, b in comm.alias.items()},
        compiler_params=_cparams(seq), name=name)(*args, *comm.ins)
    return list(res[:no]), list(res[no:])


def _dev_index(dev):
    return 4 * dev[0] + 2 * dev[1] + dev[2]


def _comm_gather_spread(shards):
    npc = len(shards)

    def copies(x_refs, o_refs, sems):
        x, y, c, chips = _place()
        me = (x, y, c)
        peers = [(x, y, 1 - c)] + [(*chip, c) for chip in chips]
        return [[pltpu.make_async_remote_copy(src_ref=x_refs[i], dst_ref=o_refs[i].at[_dev_index(me)], send_sem=sems[0].at[k, i],
                                              recv_sem=sems[1].at[k, i], device_id=peer, device_id_type=MESH)
                 for k, peer in enumerate(peers)] for i in range(npc)], peers, me

    def local(x_refs, o_refs, sems, me):
        return [pltpu.make_async_copy(x_refs[i], o_refs[i].at[_dev_index(me)], sems[2].at[i]) for i in range(npc)]

    def start(x_refs, o_refs, sems):
        cps, _, me = copies(x_refs, o_refs, sems)
        for cp in local(x_refs, o_refs, sems, me):
            cp.start()
        for row in cps:
            for cp in row:
                cp.start()

    def wait(x_refs, o_refs, sems):
        cps, peers, me = copies(x_refs, o_refs, sems)
        for i in range(npc):
            for k, peer in enumerate(peers):
                pltpu.make_async_remote_copy(src_ref=x_refs[i], dst_ref=o_refs[i].at[_dev_index(peer)], send_sem=sems[0].at[k, i],
                                             recv_sem=sems[1].at[k, i], device_id=peer, device_id_type=MESH).wait_recv()
        for row in cps:
            for cp in row:
                cp.wait_send()
        for cp in local(x_refs, o_refs, sems, me):
            cp.wait()

    return _Comm(shards, [S((N_DEV,) + s.shape, s.dtype) for s in shards],
                 [pltpu.SemaphoreType.DMA((4, npc)), pltpu.SemaphoreType.DMA((4, npc)), pltpu.SemaphoreType.DMA((npc,))], start, wait)


def _comm_gather_pass(gathered):
    npc = len(gathered)

    def copies(o_refs, sems, sent):
        x, y, c, chips = _place()
        return [pltpu.make_async_remote_copy(
            src_ref=o_refs[i].at[_dev_index((*chip, c))], dst_ref=o_refs[i].at[_dev_index((*chip, c if sent else 1 - c))],
            send_sem=sems[0].at[j, i], recv_sem=sems[1].at[j, i], device_id=(x, y, 1 - c), device_id_type=MESH)
            for i in range(npc) for j, chip in enumerate(chips)]

    def start(g_refs, o_refs, sems):
        for cp in copies(o_refs, sems, True):
            cp.start()

    def wait(g_refs, o_refs, sems):
        for cp in copies(o_refs, sems, False):
            cp.wait_recv()
        for cp in copies(o_refs, sems, True):
            cp.wait_send()

    return _Comm(gathered, [S(g.shape, g.dtype) for g in gathered],
                 [pltpu.SemaphoreType.DMA((3, npc)), pltpu.SemaphoreType.DMA((3, npc))], start, wait,
                 alias={i: i for i in range(npc)})


def _comm_to_chips(parts):
    npc = len(parts)

    def copies(p_refs, o_refs, sems):
        x, y, c, chips = _place()
        return [pltpu.make_async_remote_copy(
            src_ref=p_refs[i].at[2 * chip[0] + chip[1]], dst_ref=o_refs[i].at[j], send_sem=sems[0].at[j, i],
            recv_sem=sems[1].at[j, i], device_id=(*chip, c), device_id_type=MESH)
            for i in range(npc) for j, chip in enumerate(chips)]

    def start(p_refs, o_refs, sems):
        for cp in copies(p_refs, o_refs, sems):
            cp.start()

    def wait(p_refs, o_refs, sems):
        for cp in copies(p_refs, o_refs, sems):
            cp.wait()

    return _Comm(parts, [S((3,) + p.shape[1:], p.dtype) for p in parts],
                 [pltpu.SemaphoreType.DMA((3, npc)), pltpu.SemaphoreType.DMA((3, npc))], start, wait)


def _dil_to_tok(scr, ref, d):
    n, C = ref.shape[1], ref.shape[2]
    for r in range(d):
        v = ref[r].astype(F32)
        for cb in range(C // LANES):
            scr.at[cb][pl.ds(r, n, stride=d), :] = v[:, cb * LANES:(cb + 1) * LANES]
    return jnp.concatenate([scr[cb] for cb in range(C // LANES)], axis=1)


def _tok_to_dil(scr, val, ref, d):
    n, C = ref.shape[1], ref.shape[2]
    for cb in range(C // LANES):
        scr[cb] = val[:, cb * LANES:(cb + 1) * LANES].astype(F32)
    for r in range(d):
        ref[r] = jnp.concatenate([scr.at[cb][pl.ds(r, n, stride=d), :] for cb in range(C // LANES)], axis=1).astype(ref.dtype)


def _rowwise(fn, rows, fulls, outs, accs=(), *, tm, name, cap=True, comm=None):
    rows = [r if isinstance(r, tuple) else (r, r.shape[1], 0) for r in rows]
    first = rows[0]
    T = (first[1] if isinstance(first[0], str) else first[0]).shape[0]
    widest = max([r[1].shape[1] if isinstance(r[0], str) else r[1] for r in rows] + [o[0] for o in outs])
    if cap:
        tm = min(tm, max(8, ROW_TILES_BYTES // (2 * (len(rows) + len(outs))) // (4 * widest) // 8 * 8))
    tm = T if T <= tm else _pick(T, tm, 8)
    nr, nf, no, na = len(rows), len(fulls), len(outs), len(accs)
    dil_in = [i for i, r in enumerate(rows) if isinstance(r[0], str) and r[2] > 1]
    dil_out = [i for i, o in enumerate(outs) if len(o) == 3 and o[2] > 1]
    scr_cols = [rows[i][1].shape[1] for i in dil_in] + [outs[i][0] for i in dil_out]

    def body(*refs):
        r, f = refs[:nr], refs[nr:nr + nf]
        o, a = refs[nr + nf:nr + nf + no], refs[nr + nf + no:nr + nf + no + na]
        scr = refs[nr + nf + no + na:]
        tiles = []
        for i, x in enumerate(r):
            if i in dil_in:
                tiles.append(_dil_to_tok(scr[dil_in.index(i)], x, rows[i][2]))
            else:
                tiles.append(x[...].astype(F32))
        ro, ra = fn(*tiles, *[x[...] for x in f])
        for i, (ref, val) in enumerate(zip(o, ro)):
            if i in dil_out:
                _tok_to_dil(scr[len(dil_in) + dil_out.index(i)], val, ref, outs[i][2])
            else:
                ref[...] = val.astype(ref.dtype)
        if na:
            @pl.when(pl.program_id(0) == 0)
            def _():
                for ref in a:
                    ref[...] = jnp.zeros_like(ref)
            for ref, val in zip(a, ra):
                ref[...] += val

    in_specs, args = [], []
    for i, rr in enumerate(rows):
        if isinstance(rr[0], str):
            arr, d = rr[1], rr[2]
            if d > 1:
                in_specs.append(pl.BlockSpec((d, tm // d, arr.shape[1]), lambda i: (0, i, 0)))
                args.append(arr.reshape(d, T // d, arr.shape[1]))
            else:
                in_specs.append(pl.BlockSpec((tm, arr.shape[1]), lambda i: (i, 0)))
                args.append(arr)
        else:
            in_specs.append(pl.BlockSpec((tm, rr[1]), functools.partial(lambda i, cb: (i, cb), cb=rr[2])))
            args.append(rr[0])
    in_specs += [pl.BlockSpec(f.shape, lambda i: (0, 0)) for f in fulls]
    out_specs, out_shape = [], []
    for i, oo in enumerate(outs):
        if i in dil_out:
            d = oo[2]
            out_specs.append(pl.BlockSpec((d, tm // d, oo[0]), lambda i: (0, i, 0)))
            out_shape.append(S((d, T // d, oo[0]), oo[1]))
        else:
            out_specs.append(pl.BlockSpec((tm, oo[0]), lambda i: (i, 0)))
            out_shape.append(S((T, oo[0]), oo[1]))
    out_specs += [pl.BlockSpec(sh, lambda i: (0, 0)) for sh in accs]
    out_shape += [S(sh, F32) for sh in accs]
    res, comm_res = _carrier_call(
        body, T // tm, in_specs, out_specs, out_shape, [pltpu.VMEM((c // LANES, tm, LANES), F32) for c in scr_cols],
        list(args) + list(fulls), comm, name)
    res = [x.reshape(T, x.shape[2]) if i in dil_out else x for i, x in enumerate(res)]
    return res if comm is None else (res, comm_res)


def _rmsnorm_fwd(h, w, name, dils=()):
    D = h.shape[1]

    def fn(h, w):
        r = lax.rsqrt(jnp.mean(h * h, axis=-1, keepdims=True) + EPS)
        xn = h * r * w
        return [xn] * (1 + len(dils)), []
    return _rowwise(fn, [h], [w.reshape(1, -1)], [(D, BF16)] + [(D, BF16, d) for d in dils], tm=512, name=name)


def _rmsnorm_bwd(dxn, h, w, dres, name):
    def fn(dxn, h, dres, w):
        r = lax.rsqrt(jnp.mean(h * h, axis=-1, keepdims=True) + EPS)
        n = h * r
        dn = dxn * w
        dh = r * (dn - n * jnp.mean(dn * n, axis=-1, keepdims=True)) + dres
        return [dh], [jnp.sum(dxn * n, axis=0, keepdims=True)]
    D = h.shape[1]
    return _rowwise(fn, [dxn, h, dres], [w.reshape(1, -1)], [(D, F32)], [(1, D)], tm=512, name=name)


def _loss_head(h, w, tgt, name):
    D = h.shape[1]

    def fn(h, tgt, w):
        r = lax.rsqrt(jnp.mean(h * h, axis=-1, keepdims=True) + EPS)
        n = h * r
        e = n * w - tgt
        row_loss = 0.5 * jnp.mean(e * e, axis=-1, keepdims=True)
        dy = e * (1.0 / D)
        dn = dy * w
        dh = r * (dn - n * jnp.mean(dn * n, axis=-1, keepdims=True))
        return [dh], [jnp.sum(dy * n, axis=0, keepdims=True), jnp.broadcast_to(jnp.sum(row_loss, axis=0, keepdims=True), (1, LANES))]
    return _rowwise(fn, [h, tgt], [w.reshape(1, -1)], [(D, F32)], [(1, D), (1, LANES)], tm=256, name=name)


def _combine_fwd(os_, ls, name):
    def fn(o0, o1, o2, l0, l1, l2):
        m = jnp.maximum(jnp.maximum(l0, l1), l2)
        e0, e1, e2 = jnp.exp(l0 - m), jnp.exp(l1 - m), jnp.exp(l2 - m)
        attn = (e0 * o0 + e1 * o1 + e2 * o2) / (e0 + e1 + e2)
        return [attn, attn], []
    dil = [("dil", t, d) for t, (_, d) in zip(list(os_) + list(ls), DILATED_GROUPS * 2)]
    return _rowwise(fn, dil, [], [(ATTN_OUT_WIDTH, BF16), (ATTN_OUT_WIDTH, F32)], tm=512, name=name)


def _combine_bwd(dattn, attn, ls, head_ones, name):
    def fn(dattn, attn, l0, l1, l2, ones):
        m = jnp.maximum(jnp.maximum(l0, l1), l2)
        e0, e1, e2 = jnp.exp(l0 - m), jnp.exp(l1 - m), jnp.exp(l2 - m)
        inv = 1.0 / (e0 + e1 + e2)
        t = _dotx_r(dattn * attn, ones, parts=2)
        outs = []
        for e in (e0, e1, e2):
            al = e * inv
            outs += [al * dattn, al * t]
        return outs, []
    W = ATTN_OUT_WIDTH
    dil = [("dil", t, d) for t, (_, d) in zip(ls, DILATED_GROUPS)]
    outs = [(W, dt, d) for _, d in DILATED_GROUPS for dt in (BF16, F32)]
    return _rowwise(fn, [dattn, attn] + dil, [head_ones], outs, tm=512, name=name)


def _dt_fwd(dt_raw, dt_bias, name):
    def fn(raw, b):
        z = raw[:, :N_SSM_HEADS] + b
        return [jnp.maximum(z, 0.0) + jnp.log(1.0 + jnp.exp(-jnp.abs(z)))], []
    return _rowwise(fn, [dt_raw], [dt_bias.reshape(1, -1)], [(N_SSM_HEADS, F32)], tm=1024, name=name)[0]


def _dt_bwd(ddt_a, ddt_b, dt_raw, dt_bias, name):
    def fn(da, db, raw, b):
        g = (da + db) * jax.nn.sigmoid(raw[:, :N_SSM_HEADS] + b)
        pad = jnp.zeros((g.shape[0], LANES - N_SSM_HEADS), F32)
        return [jnp.concatenate([g, pad], axis=1)], [jnp.sum(g, axis=0, keepdims=True)]
    return _rowwise(fn, [ddt_a, ddt_b, dt_raw], [dt_bias.reshape(1, -1)], [(LANES, BF16)], [(1, N_SSM_HEADS)], tm=1024, name=name)


def _ssm_norm_fwd(y, z, w, name, comm=None):
    G = D_INNER // N_SSM_GROUPS

    def fn(y, z, w):
        yg = y * _silu(z)
        outs = []
        for g in range(N_SSM_GROUPS):
            t = yg[:, g * G:(g + 1) * G]
            outs.append(t * lax.rsqrt(jnp.mean(t * t, axis=-1, keepdims=True) + EPS))
        return [jnp.concatenate(outs, axis=1) * w], []
    res = _rowwise(fn, [y, z], [w.reshape(1, -1)], [(D_INNER, BF16)], tm=256, name=name, comm=comm)
    return res[0] if comm is None else (res[0][0], res[1])


def _ssm_norm_bwd(dssm, y, z, w, name):
    G = D_INNER // N_SSM_GROUPS

    def fn(dssm, y, z, w):
        sz = _silu(z)
        yg = y * sz
        dn = dssm * w
        ns, dygs = [], []
        for g in range(N_SSM_GROUPS):
            t = yg[:, g * G:(g + 1) * G]
            r = lax.rsqrt(jnp.mean(t * t, axis=-1, keepdims=True) + EPS)
            n = t * r
            d = dn[:, g * G:(g + 1) * G]
            dygs.append(r * (d - n * jnp.mean(d * n, axis=-1, keepdims=True)))
            ns.append(n)
        n, dyg = jnp.concatenate(ns, axis=1), jnp.concatenate(dygs, axis=1)
        return [dyg * sz, dyg * y * _dsilu(z)], [jnp.sum(dssm * n, axis=0, keepdims=True)]
    return _rowwise(fn, [dssm, y, z], [w.reshape(1, -1)], [(D_INNER, BF16), (D_INNER, BF16)], [(1, D_INNER)], tm=256, name=name)


def _gate_out_proj(a, sb, gl, h, w_out, name):
    def fn(a, sb, gl, h, w):
        g = jax.nn.sigmoid(gl)
        merged = (g[:, :D_MODEL] * a + g[:, D_MODEL:] * sb).astype(BF16)
        return [h + jnp.dot(merged, w, preferred_element_type=F32), merged], []
    return _rowwise(fn, [a, sb, gl, h], [w_out], [(D_MODEL, F32), (D_MODEL, BF16)], tm=512, cap=False, name=name)


def _d_out_proj_gate(dh, a, sb, gl, w_out, name):
    def fn(dh, a, sb, gl, w):
        dm = lax.dot_general(dh.astype(BF16), w, NT, preferred_element_type=F32)
        g = jax.nn.sigmoid(gl)
        g0, g1 = g[:, :D_MODEL], g[:, D_MODEL:]
        dgl = jnp.concatenate([dm * a * g0 * (1.0 - g0), dm * sb * g1 * (1.0 - g1)], axis=1)
        return [g0 * dm, g1 * dm, dgl], []
    return _rowwise(fn, [dh, a, sb, gl], [w_out], [(D_MODEL, BF16), (D_MODEL, BF16), (2 * D_MODEL, BF16)], tm=512, cap=False, name=name)


FFN_HALF = D_FF // 2


def _ffn_perm(w):
    h = FFN_HALF
    return jnp.concatenate([w[:, 0:h], w[:, D_FF:D_FF + h], w[:, h:D_FF], w[:, D_FF + h:]], axis=1)


def _ffn_unperm(w):
    h = FFN_HALF
    return jnp.concatenate([w[:, 0:h], w[:, 2 * h:3 * h], w[:, h:2 * h], w[:, 3 * h:]], axis=1)


def _swiglu_epilogue(res):
    return [res, _silu(res[:, :FFN_HALF]) * res[:, FFN_HALF:]]


def _dswiglu_epilogue(dact, u):
    u = u.astype(F32)
    gate, up = u[:, :FFN_HALF], u[:, FFN_HALF:]
    return [jnp.concatenate([dact * up * _dsilu(gate), dact * _silu(gate)], axis=1)]


def _adamw(w, g, m, v, name):
    c1 = 1.0 - ADAM_B1 ** ADAM_STEP
    c2 = 1.0 - ADAM_B2 ** ADAM_STEP

    def fn(w, g, m, v):
        m = ADAM_B1 * m + (1.0 - ADAM_B1) * g
        v = ADAM_B2 * v + (1.0 - ADAM_B2) * (g * g)
        delta = -ADAM_LR * ((m / c1) / (jnp.sqrt(v / c2) + ADAM_EPS) + ADAM_WD * w)
        return [delta, m, v], []
    C = w.shape[1]
    return _rowwise(fn, [w, g, m, v], [], [(C, F32)] * 3, tm=256, name=name)


def _bias_consts(dilation, n_steps):
    qi = np.arange(ATTN_BLOCK)[:, None]
    kj = np.arange(2 * ATTN_BLOCK)[None, :]
    steps = qi + ATTN_BLOCK - kj
    valid = (steps >= 0) & (steps <= n_steps)
    dist = jnp.asarray(np.clip(steps, 0, n_steps) * dilation, jnp.int32)
    max_exact = N_REL_BUCKETS // 2
    d_f = jnp.maximum(dist, 1).astype(F32)
    large = max_exact + (jnp.log(d_f / max_exact) / math.log(REL_MAX_DISTANCE / max_exact)
                         * (N_REL_BUCKETS - max_exact)).astype(jnp.int32)
    large = jnp.minimum(large, N_REL_BUCKETS - 1)
    bucket = jnp.where(dist < max_exact, dist, large).reshape(-1)
    onehot = (bucket[None, :] == jnp.arange(N_REL_BUCKETS)[:, None]).astype(F32)
    return onehot, jnp.asarray(valid.reshape(1, -1), F32)


def _bias_gather(rel_g_t, onehot, valid, name):
    def body(r_ref, oh_ref, v_ref, o_ref):
        b = jnp.dot(r_ref[...], oh_ref[...], preferred_element_type=F32, precision=lax.Precision.HIGHEST)
        o_ref[...] = jnp.where(v_ref[...] > 0.5, b, NEG)
    return pl.pallas_call(body, out_shape=S((HEADS_PER_GROUP, onehot.shape[1]), F32), compiler_params=_cparams(), name=name)(rel_g_t, onehot, valid)


def _bias_scatter(dbias, onehot, name):
    def body(d_ref, oh_ref, o_ref):
        o_ref[...] = lax.dot_general(d_ref[...], oh_ref[...], NT, preferred_element_type=F32, precision=lax.Precision.HIGHEST)
    return pl.pallas_call(body, out_shape=S((HEADS_PER_GROUP, N_REL_BUCKETS), F32), compiler_params=_cparams(), name=name)(dbias, onehot)


ATTN_QB_FWD, ATTN_QB_BWD = 4, 4


def _attn_tiles(T, d, qb):
    seg = T // d
    nqb = min(qb, seg // ATTN_BLOCK)
    tq = nqb * ATTN_BLOCK
    return seg, nqb, tq, seg // tq


def _attn_fwd(qkv, bias, d, name, comm=None):
    T = qkv.shape[0]
    seg, nqb, tq, ns = _attn_tiles(T, d, ATTN_QB_FWD)
    W = ATTN_OUT_WIDTH
    scale = HEAD_DIM ** -0.5

    def body(q_ref, kh_ref, kc_ref, vh_ref, vc_ref, b_ref, o_ref, l_ref, s_scr, p_scr):
        n = pl.program_id(1)
        qv = q_ref[...]
        kk = jnp.concatenate([kh_ref[...], kc_ref[...]], axis=0)
        vv = jnp.concatenate([vh_ref[...], vc_ref[...]], axis=0)
        col = lax.broadcasted_iota(jnp.int32, (ATTN_BLOCK, 2 * ATTN_BLOCK), 1)
        kill = jnp.logical_and(n == 0, col < ATTN_BLOCK)
        lo = lax.broadcasted_iota(jnp.int32, (1, LANES), 1) < HEAD_DIM
        zero = jnp.zeros((), BF16)
        for j in range(nqb):
            rows = slice(j * ATTN_BLOCK, (j + 1) * ATTN_BLOCK)
            keys = slice(j * ATTN_BLOCK, (j + 2) * ATTN_BLOCK)
            for hp in range(HEADS_PER_GROUP // 2):
                ps = slice(hp * LANES, (hp + 1) * LANES)
                q2 = (qv[rows, ps].astype(F32) * scale).astype(BF16)
                k2 = kk[keys, ps]
                s_scr[2 * hp] = lax.dot_general(q2, jnp.where(lo, k2, zero), NT, preferred_element_type=F32)
                s_scr[2 * hp + 1] = lax.dot_general(q2, jnp.where(lo, zero, k2), NT, preferred_element_type=F32)
            s = s_scr[...] + b_ref[...]
            if j == 0:
                s = jnp.where(kill[None], NEG, s)
            m = jnp.max(s, axis=-1, keepdims=True)
            p = jnp.exp(s - m)
            den = jnp.sum(p, axis=-1, keepdims=True)
            p_scr[...] = p.astype(BF16)
            inv = 1.0 / den
            lse = m + jnp.log(den)
            for hp in range(HEADS_PER_GROUP // 2):
                ps = slice(hp * LANES, (hp + 1) * LANES)
                v2 = vv[keys, ps]
                o2 = (jnp.dot(p_scr[2 * hp], jnp.where(lo, v2, zero), preferred_element_type=F32)
                      + jnp.dot(p_scr[2 * hp + 1], jnp.where(lo, zero, v2), preferred_element_type=F32))
                o_ref[rows, ps] = (o2 * jnp.where(lo, inv[2 * hp], inv[2 * hp + 1])).astype(o_ref.dtype)
                l_ref[rows, ps] = jnp.where(lo, lse[2 * hp], lse[2 * hp + 1])

    def cur(c):
        return pl.BlockSpec((tq, W), lambda r, n: (r * ns + n, c))

    def halo(c):
        return pl.BlockSpec((ATTN_BLOCK, W), lambda r, n: (jnp.maximum((r * ns + n) * nqb - 1, 0), c))

    res, comm_res = _carrier_call(
        body, (d, ns),
        [cur(0), halo(1), cur(1), halo(2), cur(2), pl.BlockSpec(bias.shape, lambda r, n: (0, 0, 0))],
        [cur(0), cur(0)], [S((T, W), BF16), S((T, W), F32)],
        [pltpu.VMEM((HEADS_PER_GROUP, ATTN_BLOCK, 2 * ATTN_BLOCK), F32), pltpu.VMEM((HEADS_PER_GROUP, ATTN_BLOCK, 2 * ATTN_BLOCK), BF16)],
        [qkv, qkv, qkv, qkv, qkv, bias], comm, name)
    return res if comm is None else (res, comm_res)


def _attn_bwd(qkv, bias, lse, do, dd, d, name):
    T = qkv.shape[0]
    seg, nqb, tq, ns = _attn_tiles(T, d, ATTN_QB_BWD)
    W = ATTN_OUT_WIDTH
    B = ATTN_BLOCK
    scale = HEAD_DIM ** -0.5

    def body(q_ref, kh_ref, kc_ref, vh_ref, vc_ref, b_ref, l_ref, do_ref, dd_ref, dq_ref, dk_ref, dv_ref, db_ref, pk_ref, pv_ref,
             s_scr, dp_scr, p_scr, ds_scr):
        r, n = pl.program_id(0), pl.program_id(1)

        @pl.when(jnp.logical_and(r == 0, n == 0))
        def _():
            db_ref[...] = jnp.zeros_like(db_ref)

        @pl.when(n == 0)
        def _():
            pk_ref[...] = jnp.zeros_like(pk_ref)
            pv_ref[...] = jnp.zeros_like(pv_ref)

        @pl.when(n < ns)
        def _():
            qv = q_ref[...]
            kk = jnp.concatenate([kh_ref[...], kc_ref[...]], axis=0)
            vv = jnp.concatenate([vh_ref[...], vc_ref[...]], axis=0)
            lse_v, do_v, dd_v = l_ref[...], do_ref[...], dd_ref[...]
            col = lax.broadcasted_iota(jnp.int32, (B, 2 * B), 1)
            kill = jnp.logical_and(n == 0, col < B)
            dqs = [[None] * (HEADS_PER_GROUP // 2) for _ in range(nqb)]
            dks = [[None] * (HEADS_PER_GROUP // 2) for _ in range(nqb)]
            dvs = [[None] * (HEADS_PER_GROUP // 2) for _ in range(nqb)]
            H, HP = HEADS_PER_GROUP, HEADS_PER_GROUP // 2
            do_b = do_v.astype(BF16)
            lo = lax.broadcasted_iota(jnp.int32, (1, LANES), 1) < HEAD_DIM
            zero = jnp.zeros((), BF16)
            first = lambda t: jnp.where(lo, t, zero)
            second = lambda t: jnp.where(lo, zero, t)
            for j in range(nqb):
                rows = slice(j * B, (j + 1) * B)
                keys = slice(j * B, (j + 2) * B)
                for hp in range(HP):
                    ps = slice(hp * LANES, (hp + 1) * LANES)
                    q2 = (qv[rows, ps].astype(F32) * scale).astype(BF16)
                    k2, v2, do2 = kk[keys, ps], vv[keys, ps], do_b[rows, ps]
                    s_scr[2 * hp] = lax.dot_general(q2, first(k2), NT, preferred_element_type=F32)
                    s_scr[2 * hp + 1] = lax.dot_general(q2, second(k2), NT, preferred_element_type=F32)
                    dp_scr[2 * hp] = lax.dot_general(do2, first(v2), NT, preferred_element_type=F32)
                    dp_scr[2 * hp + 1] = lax.dot_general(do2, second(v2), NT, preferred_element_type=F32)
                lse_h = jnp.stack([lse_v[rows, h * HEAD_DIM:h * HEAD_DIM + 1] for h in range(H)], axis=0)
                dd_h = jnp.stack([dd_v[rows, h * HEAD_DIM:h * HEAD_DIM + 1] for h in range(H)], axis=0)
                s = s_scr[...] + b_ref[...]
                if j == 0:
                    s = jnp.where(kill[None], NEG, s)
                p = jnp.exp(s - lse_h)
                ds = p * (dp_scr[...] - dd_h)
                db_ref[...] += ds
                p_scr[...] = p.astype(BF16)
                ds_scr[...] = ds.astype(BF16)
                for hp in range(HP):
                    ps = slice(hp * LANES, (hp + 1) * LANES)
                    q2 = (qv[rows, ps].astype(F32) * scale).astype(BF16)
                    k2, do2 = kk[keys, ps], do_b[rows, ps]
                    pa, pb, da, db_ = p_scr[2 * hp], p_scr[2 * hp + 1], ds_scr[2 * hp], ds_scr[2 * hp + 1]
                    dvs[j][hp] = (lax.dot_general(pa, first(do2), TN, preferred_element_type=F32)
                                  + lax.dot_general(pb, second(do2), TN, preferred_element_type=F32))
                    dqs[j][hp] = (jnp.dot(da, first(k2), preferred_element_type=F32)
                                  + jnp.dot(db_, second(k2), preferred_element_type=F32)) * scale
                    dks[j][hp] = (lax.dot_general(da, first(q2), TN, preferred_element_type=F32)
                                  + lax.dot_general(db_, second(q2), TN, preferred_element_type=F32))
            dq_ref[...] = jnp.concatenate([jnp.concatenate(dqs[j], axis=1) for j in range(nqb)], axis=0).astype(dq_ref.dtype)
            for parts, out_ref, pend in ((dks, dk_ref, pk_ref), (dvs, dv_ref, pv_ref)):
                full = [jnp.concatenate(parts[j], axis=1) for j in range(nqb)]
                if tq > B:
                    out_ref[:tq - B] = pend[:tq - B].astype(out_ref.dtype)
                out_ref[tq - B:] = (pend[tq - B:] + full[0][:B]).astype(out_ref.dtype)
                for j in range(nqb - 1):
                    pend[j * B:(j + 1) * B] = full[j][B:] + full[j + 1][:B]
                pend[tq - B:] = full[nqb - 1][B:]

        @pl.when(n == ns)
        def _():
            dk_ref[...] = pk_ref[...].astype(dk_ref.dtype)
            dv_ref[...] = pv_ref[...].astype(dv_ref.dtype)

    def cur(c):
        return pl.BlockSpec((tq, W), lambda r, n: (r * ns + jnp.minimum(n, ns - 1), c))

    def halo(c):
        return pl.BlockSpec((B, W), lambda r, n: (jnp.maximum((r * ns + jnp.minimum(n, ns - 1)) * nqb - 1, 0), c))

    late = pl.BlockSpec((tq, W), lambda r, n: (r * ns + jnp.clip(n - 1, 0, ns - 1), 0))
    bspec = pl.BlockSpec(bias.shape, lambda r, n: (0, 0, 0))
    return pl.pallas_call(
        body, grid=(d, ns + 1),
        in_specs=[cur(0), halo(1), cur(1), halo(2), cur(2), bspec, cur(0), cur(0), cur(0)],
        out_specs=[cur(0), late, late, bspec],
        out_shape=[S((T, W), BF16)] * 3 + [S(bias.shape, F32)],
        scratch_shapes=[pltpu.VMEM((tq, W), F32), pltpu.VMEM((tq, W), F32)]
                       + [pltpu.VMEM((HEADS_PER_GROUP, B, 2 * B), t) for t in (F32, F32, BF16, BF16)],
        compiler_params=_cparams(("arbitrary", "arbitrary")), name=name,
    )(qkv, qkv, qkv, qkv, qkv, bias, lse, do, dd)


CONV_TM, CONV_TC = 512, 1024


def _shift_down(x, halo8, s, row8):
    xr = pltpu.roll(x, s, 0)
    first = jnp.where(row8 < s, pltpu.roll(halo8, s, 0), xr[:8])
    return jnp.concatenate([first, xr[8:]], axis=0)


def _shift_up(x, halo8, s, row8):
    n = x.shape[0]
    xr = pltpu.roll(x, n - s, 0)
    last = jnp.where(row8 >= 8 - s, pltpu.roll(halo8, 8 - s, 0), xr[n - 8:])
    return jnp.concatenate([xr[:n - 8], last], axis=0)


def _conv_fwd(x, w, b, name, comm=None):
    T, C = x.shape
    tm, tc = min(CONV_TM, T), CONV_TC

    def body(x_ref, p_ref, w_ref, b_ref, u_ref, a_ref):
        ti = pl.program_id(1)
        xv = x_ref[...]
        p8 = jnp.where(ti == 0, 0.0, p_ref[...])
        wv = w_ref[...]
        row8 = lax.broadcasted_iota(jnp.int32, (8, tc), 0)
        u = xv * wv[3:4] + b_ref[...]
        for s in (1, 2, 3):
            u = u + _shift_down(xv, p8, s, row8) * wv[3 - s:4 - s]
        u_ref[...] = u
        a_ref[...] = _silu(u)

    cur = pl.BlockSpec((tm, tc), lambda cj, ti: (ti, cj))
    halo = pl.BlockSpec((8, tc), lambda cj, ti: (jnp.maximum(ti * (tm // 8) - 1, 0), cj))
    res, comm_res = _carrier_call(
        body, (C // tc, T // tm),
        [cur, halo, pl.BlockSpec((CONV_WIDTH, tc), lambda cj, ti: (0, cj)), pl.BlockSpec((1, tc), lambda cj, ti: (0, cj))],
        [cur, cur], [S((T, C), F32)] * 2, [], [x, x, w, b], comm, name)
    return res if comm is None else (res, comm_res)


def _conv_bwd(dact, u, x, w, name):
    T, C = x.shape
    tm, tc = min(CONV_TM, T), CONV_TC
    nt = T // tm

    def body(d_ref, dn_ref, u_ref, un_ref, x_ref, w_ref, dx_ref, dw_ref, db_ref):
        ti = pl.program_id(1)

        @pl.when(ti == 0)
        def _():
            dw_ref[...] = jnp.zeros_like(dw_ref)
            db_ref[...] = jnp.zeros_like(db_ref)

        du = d_ref[...] * _dsilu(u_ref[...])
        dun = jnp.where(ti == nt - 1, 0.0, dn_ref[...] * _dsilu(un_ref[...]))
        xv = x_ref[...]
        wv = w_ref[...]
        row8 = lax.broadcasted_iota(jnp.int32, (8, tc), 0)
        dx = du * wv[3:4]
        dws = [None] * CONV_WIDTH
        dws[3] = jnp.sum(du * xv, axis=0, keepdims=True)
        for s in (1, 2, 3):
            up = _shift_up(du, dun, s, row8)
            dx = dx + up * wv[3 - s:4 - s]
            dws[3 - s] = jnp.sum(up * xv, axis=0, keepdims=True)
        dx_ref[...] = dx.astype(dx_ref.dtype)
        dw_ref[...] += jnp.concatenate(dws, axis=0)
        db_ref[...] += jnp.sum(du, axis=0, keepdims=True)

    cur = pl.BlockSpec((tm, tc), lambda cj, ti: (ti, cj))
    nxt = pl.BlockSpec((8, tc), lambda cj, ti: (jnp.minimum((ti + 1) * (tm // 8), T // 8 - 1), cj))
    return pl.pallas_call(
        body, grid=(C // tc, nt),
        in_specs=[cur, nxt, cur, nxt, cur, pl.BlockSpec((CONV_WIDTH, tc), lambda cj, ti: (0, cj))],
        out_specs=[cur, pl.BlockSpec((CONV_WIDTH, tc), lambda cj, ti: (0, cj)), pl.BlockSpec((1, tc), lambda cj, ti: (0, cj))],
        out_shape=[S((T, C), BF16), S((CONV_WIDTH, C), F32), S((1, C), F32)],
        compiler_params=_cparams(("parallel", "arbitrary")), name=name)(dact, dact, u, u, x, w)


def _ssd_consts():
    i = np.arange(SSD_CHUNK)
    tril = (i[None, :] <= i[:, None]).astype(np.float32)
    trils = (i[None, :] < i[:, None]).astype(np.float32)
    head = np.repeat(np.arange(N_SSM_HEADS), D_INNER // N_SSM_HEADS)
    et = (head[None, :] == np.arange(N_SSM_HEADS)[:, None]).astype(np.float32)
    c = lambda a: jnp.asarray(a, BF16)
    return dict(tril=c(tril), triu=c(tril.T), trils=c(trils), et=c(et), e=c(et.T))


def _ssd_common(act_ref, dt_ref, dtT_ref, al_ref, alT_ref, tril_ref, triu_ref, et_ref):
    a_row = -jnp.exp(al_ref[...])
    a_col = -jnp.exp(alT_ref[...])
    dt, dtT = dt_ref[...], dtT_ref[...]
    la = _dotx_l(tril_ref[...], dt * a_row)
    laT = _dotx_r(dtT * a_col, triu_ref[...])
    et = et_ref[...]
    la_e = _dotx_r(la, et)
    dt_e = _dotx_r(dt, et, parts=2)
    x = act_ref[:, :D_INNER]
    xdt = x * dt_e
    la_q = la_e[SSD_CHUNK - 1:SSD_CHUNK, :]
    return a_row, a_col, dt, dtT, la, laT, la_e, dt_e, x, xdt, la_q


def _decay(la, laT, h, causal):
    seg = la[:, h:h + 1] - laT[h:h + 1, :]
    return jnp.exp(jnp.where(causal, seg, NEG))


def _ssd_fwd(act, dt, dtT, alog, dskip_e, cs, name, comm=None):
    T = act.shape[0]
    nc = T // SSD_CHUNK
    Q, G, GW = SSD_CHUNK, N_SSM_GROUPS, D_INNER // N_SSM_GROUPS

    def body(act_ref, dt_ref, dtT_ref, al_ref, alT_ref, dsk_ref, tril_ref, triu_ref, et_ref, y_ref, st_ref, scr):
        @pl.when(pl.program_id(0) == 0)
        def _():
            scr[...] = jnp.zeros_like(scr)
        st_ref[0] = scr[...]
        a_row, a_col, dtv, dtTv, la, laT, la_e, dt_e, x, xdt, la_q = _ssd_common(
            act_ref, dt_ref, dtT_ref, al_ref, alT_ref, tril_ref, triu_ref, et_ref)
        ela = jnp.exp(la_e)
        xdt_b = xdt.astype(BF16)
        xdte_b = (xdt * jnp.exp(la_q - la_e)).astype(BF16)
        ela_q = jnp.exp(la_q)
        causal = lax.broadcasted_iota(jnp.int32, (Q, Q), 0) >= lax.broadcasted_iota(jnp.int32, (Q, Q), 1)
        for g in range(G):
            gs = slice(g * GW, (g + 1) * GW)
            Bg = act_ref[:, D_INNER + g * D_STATE:D_INNER + (g + 1) * D_STATE].astype(BF16)
            Cg = act_ref[:, D_INNER + G * D_STATE + g * D_STATE:D_INNER + G * D_STATE + (g + 1) * D_STATE].astype(BF16)
            cb = lax.dot_general(Cg, Bg, NT, preferred_element_type=F32)
            st = scr[g]
            y_inter = jnp.dot(Cg, st.astype(BF16), preferred_element_type=F32) * ela[:, gs]
            ys = []
            for hh in range(HEADS_PER_GROUP):
                h = g * HEADS_PER_GROUP + hh
                m = (cb * _decay(la, laT, h, causal)).astype(BF16)
                ys.append(jnp.dot(m, xdt_b[:, h * HEAD_DIM:(h + 1) * HEAD_DIM], preferred_element_type=F32))
            y_ref[:, gs] = (jnp.concatenate(ys, axis=1) + y_inter + x[:, gs] * dsk_ref[:, gs]).astype(y_ref.dtype)
            scr[g] = st * ela_q[:, gs] + lax.dot_general(Bg, xdte_b[:, gs], TN, preferred_element_type=F32)

    full = lambda a: pl.BlockSpec(a.shape, lambda c: (0,) * a.ndim)
    al, alT = alog.reshape(1, -1), alog.reshape(-1, 1)
    res, comm_res = _carrier_call(
        body, nc,
        [pl.BlockSpec((Q, XBC_WIDTH), lambda c: (c, 0)), pl.BlockSpec((Q, N_SSM_HEADS), lambda c: (c, 0)),
         pl.BlockSpec((N_SSM_HEADS, Q), lambda c: (0, c)), full(al), full(alT), full(dskip_e),
         full(cs["tril"]), full(cs["triu"]), full(cs["et"])],
        [pl.BlockSpec((Q, D_INNER), lambda c: (c, 0)), pl.BlockSpec((1, G, D_STATE, GW), lambda c: (c, 0, 0, 0))],
        [S((T, D_INNER), BF16), S((nc, G, D_STATE, GW), F32)],
        [pltpu.VMEM((G, D_STATE, GW), F32)],
        [act, dt, dtT, al, alT, dskip_e, cs["tril"], cs["triu"], cs["et"]], comm, name)
    return res if comm is None else (res, comm_res)


def _ssd_bwd(dy, act, dt, dtT, alog, dskip_e, states, cs, name, comm=None):
    T = act.shape[0]
    nc = T // SSD_CHUNK
    Q, G, GW, H = SSD_CHUNK, N_SSM_GROUPS, D_INNER // N_SSM_GROUPS, N_SSM_HEADS

    def body(dy_ref, act_ref, dt_ref, dtT_ref, al_ref, alT_ref, dsk_ref, stp_ref, tril_ref, triu_ref, trils_ref,
             et_ref, e_ref, dact_ref, ddt_ref, ddtT_ref, da_ref, daT_ref, dsk_out_ref, dst, wbuf, ubuf, vbuf, sbuf, dm_scr, m_scr):
        @pl.when(pl.program_id(0) == 0)
        def _():
            dst[...] = jnp.zeros_like(dst)
            da_ref[...] = jnp.zeros_like(da_ref)
            daT_ref[...] = jnp.zeros_like(daT_ref)
            dsk_out_ref[...] = jnp.zeros_like(dsk_out_ref)
        a_row, a_col, dtv, dtTv, la, laT, la_e, dt_e, x, xdt, la_q = _ssd_common(
            act_ref, dt_ref, dtT_ref, al_ref, alT_ref, tril_ref, triu_ref, et_ref)
        dyv = dy_ref[...].astype(F32)
        ela = jnp.exp(la_e)
        e_end = jnp.exp(la_q - la_e)
        ela_q = jnp.exp(la_q)
        dye_b = (dyv * ela).astype(BF16)
        dy_b = dyv.astype(BF16)
        xdt_b = xdt.astype(BF16)
        xdte_b = (xdt * e_end).astype(BF16)
        ri = lax.broadcasted_iota(jnp.int32, (Q, Q), 0)
        ci = lax.broadcasted_iota(jnp.int32, (Q, Q), 1)
        causal = ri >= ci
        rows = []
        for g in range(G):
            gs = slice(g * GW, (g + 1) * GW)
            Bg = act_ref[:, D_INNER + g * D_STATE:D_INNER + (g + 1) * D_STATE].astype(BF16)
            Cg = act_ref[:, D_INNER + G * D_STATE + g * D_STATE:D_INNER + G * D_STATE + (g + 1) * D_STATE].astype(BF16)
            cb = lax.dot_general(Cg, Bg, NT, preferred_element_type=F32)
            stp = stp_ref[0, g]
            stp_b = stp.astype(BF16)
            dstv = dst[g]
            dst_b = dstv.astype(BF16)
            y_inter = jnp.dot(Cg, stp_b, preferred_element_type=F32) * ela[:, gs]
            wbuf[:, gs] = dyv[:, gs] * y_inter
            dxdt_state = jnp.dot(Bg, dst_b, preferred_element_type=F32) * e_end[:, gs]
            ubuf[:, gs] = dxdt_state * xdt[:, gs]
            dC = lax.dot_general(dye_b[:, gs], stp_b, NT, preferred_element_type=F32)
            dB = lax.dot_general(xdte_b[:, gs], dst_b, NT, preferred_element_type=F32)
            sbuf[:, gs] = jnp.broadcast_to(jnp.sum(dstv * stp, axis=0, keepdims=True), (8, GW))
            dst[g] = dstv * ela_q[:, gs] + lax.dot_general(Cg, dye_b[:, gs], TN, preferred_element_type=F32)
            for hh in range(HEADS_PER_GROUP):
                hs = slice((g * HEADS_PER_GROUP + hh) * HEAD_DIM, (g * HEADS_PER_GROUP + hh + 1) * HEAD_DIM)
                dm_scr[hh] = lax.dot_general(dy_b[:, hs], xdt_b[:, hs], NT, preferred_element_type=F32)
            dG = jnp.zeros((Q, Q), F32)
            for hh in range(HEADS_PER_GROUP):
                L = _decay(la, laT, g * HEADS_PER_GROUP + hh, causal)
                M = cb * L
                dM = dm_scr[hh]
                dG = dG + dM * L
                W = dM * M
                rows.append(jnp.sum(W.T, axis=0, keepdims=True) - jnp.sum(W, axis=0, keepdims=True))
                m_scr[hh] = M.astype(BF16)
            dxs = []
            for hh in range(HEADS_PER_GROUP):
                hs = slice((g * HEADS_PER_GROUP + hh) * HEAD_DIM, (g * HEADS_PER_GROUP + hh + 1) * HEAD_DIM)
                dxs.append(lax.dot_general(m_scr[hh], dy_b[:, hs], TN, preferred_element_type=F32))
            dG_b = dG.astype(BF16)
            dC = dC + jnp.dot(dG_b, Bg, preferred_element_type=F32)
            dB = dB + lax.dot_general(dG_b, Cg, TN, preferred_element_type=F32)
            dxdt = jnp.concatenate(dxs, axis=1) + dxdt_state
            vbuf[:, gs] = dxdt * x[:, gs]
            dact_ref[:, gs] = dxdt * dt_e[:, gs] + dyv[:, gs] * dsk_ref[:, gs]
            dact_ref[:, D_INNER + g * D_STATE:D_INNER + (g + 1) * D_STATE] = dB
            dact_ref[:, D_INNER + G * D_STATE + g * D_STATE:D_INNER + G * D_STATE + (g + 1) * D_STATE] = dC
        e = e_ref[...]
        w = _dotx_r(wbuf[...], e, parts=2)
        u = _dotx_r(ubuf[...], e, parts=2)
        vx = _dotx_r(vbuf[...], e, parts=2)
        dsk = _dotx_r(jnp.broadcast_to(jnp.sum(dyv * x, axis=0, keepdims=True), (8, D_INNER)), e, parts=2)[0:1]
        s0 =_dotx_r(sbuf[...], e, parts=2)[0:1] * jnp.exp(la[Q - 1:Q, :])
        ddelta = _dotx_l(triu_ref[...], w) + _dotx_l(trils_ref[...], u) + s0
        ddt_ref[...] = ddelta * a_row + vx
        ddeltaT = _dotx_r(jnp.concatenate(rows, axis=0), tril_ref[...])
        ddtT_ref[...] = ddeltaT * a_col
        da_ref[...] += jnp.sum(ddelta * dtv, axis=0, keepdims=True)
        daT_ref[...] += jnp.sum(ddeltaT * dtTv, axis=1, keepdims=True)
        dsk_out_ref[...] += dsk

    rev = lambda c: nc - 1 - c
    full = lambda a: pl.BlockSpec(a.shape, lambda c: (0,) * a.ndim)
    al, alT = alog.reshape(1, -1), alog.reshape(-1, 1)
    consts = [cs[k] for k in ("tril", "triu", "trils", "et", "e")]
    res, comm_res = _carrier_call(
        body, nc,
        [pl.BlockSpec((Q, D_INNER), lambda c: (rev(c), 0)), pl.BlockSpec((Q, XBC_WIDTH), lambda c: (rev(c), 0)),
         pl.BlockSpec((Q, H), lambda c: (rev(c), 0)), pl.BlockSpec((H, Q), lambda c: (0, rev(c))),
         full(al), full(alT), full(dskip_e), pl.BlockSpec((1, G, D_STATE, GW), lambda c: (rev(c), 0, 0, 0))]
        + [full(a) for a in consts],
        [pl.BlockSpec((Q, XBC_WIDTH), lambda c: (rev(c), 0)), pl.BlockSpec((Q, H), lambda c: (rev(c), 0)),
         pl.BlockSpec((H, Q), lambda c: (0, rev(c))), pl.BlockSpec((1, H), lambda c: (0, 0)),
         pl.BlockSpec((H, 1), lambda c: (0, 0)), pl.BlockSpec((1, H), lambda c: (0, 0))],
        [S((T, XBC_WIDTH), F32), S((T, H), F32), S((H, T), F32), S((1, H), F32), S((H, 1), F32), S((1, H), F32)],
        [pltpu.VMEM((G, D_STATE, GW), F32), pltpu.VMEM((Q, D_INNER), F32), pltpu.VMEM((Q, D_INNER), F32),
         pltpu.VMEM((Q, D_INNER), F32), pltpu.VMEM((8, D_INNER), F32),
         pltpu.VMEM((HEADS_PER_GROUP, Q, Q), F32), pltpu.VMEM((HEADS_PER_GROUP, Q, Q), BF16)],
        [dy, act, dt, dtT, al, alT, dskip_e, states] + consts, comm, name)
    return res if comm is None else (res, comm_res)


def _a_log_grad(da, daT_row, alog, name):
    def body(a_ref, b_ref, al_ref, o_ref):
        o_ref[...] = (a_ref[...] + b_ref[...]) * (-jnp.exp(al_ref[...]))
    return pl.pallas_call(body, out_shape=S((1, N_SSM_HEADS), F32), name=name)(da, daT_row, alog.reshape(1, -1))


def _place():
    x, y, c = lax.axis_index("x"), lax.axis_index("y"), lax.axis_index("c")
    return x, y, c, [(1 - x, y), (x, 1 - y), (1 - x, 1 - y)]


def _all_gather(shards, name):
    npc = len(shards)

    def body(*refs):
        x_refs, o_refs = refs[:npc], refs[npc:2 * npc]
        send_sems, recv_sems, local_sems = refs[2 * npc:]
        x, y, c, chips = _place()
        me, sib = (x, y, c), (x, y, 1 - c)

        def rows(dev, i):
            return o_refs[i].at[4 * dev[0] + 2 * dev[1] + dev[2]]

        def copy(k, i, block, to, src=None):
            return pltpu.make_async_remote_copy(
                src_ref=rows(block, i) if src is None else src, dst_ref=rows(block, i),
                send_sem=send_sems.at[k, i], recv_sem=recv_sems.at[k, i], device_id=to, device_id_type=MESH)

        mine = [pltpu.make_async_copy(x_refs[i], rows(me, i), local_sems.at[i]) for i in range(npc)]
        for cp in mine:
            cp.start()
        first = []
        for i in range(npc):
            first.append(copy(0, i, me, sib, src=x_refs[i]))
            first += [copy(1 + j, i, me, (*chip, c), src=x_refs[i]) for j, chip in enumerate(chips)]
        for cp in first:
            cp.start()
        passed = []
        for i in range(npc):
            for j, chip in enumerate(chips):
                copy(1 + j, i, (*chip, c), me).wait_recv()
                cp = copy(4 + j, i, (*chip, c), sib)
                cp.start()
                passed.append(cp)
        for i in range(npc):
            copy(0, i, sib, me).wait_recv()
            for j, chip in enumerate(chips):
                copy(4 + j, i, (*chip, 1 - c), me).wait_recv()
        for cp in first + passed:
            cp.wait_send()
        for cp in mine:
            cp.wait()

    anys = pl.BlockSpec(memory_space=pl.ANY)
    return pl.pallas_call(
        body, in_specs=[anys] * npc, out_specs=[anys] * npc, out_shape=[S((N_DEV,) + s.shape, s.dtype) for s in shards],
        scratch_shapes=[pltpu.SemaphoreType.DMA((7, npc)), pltpu.SemaphoreType.DMA((7, npc)), pltpu.SemaphoreType.DMA((npc,))],
        name=name)(*shards)


def _to_sibling(to_sib, name):
    npc = len(to_sib)

    def body(*refs):
        s_refs, o_refs, send_sems, recv_sems = refs[:npc], refs[npc:2 * npc], refs[2 * npc], refs[2 * npc + 1]
        x, y, c, _ = _place()
        cps = [pltpu.make_async_remote_copy(
            src_ref=s_refs[i], dst_ref=o_refs[i], send_sem=send_sems.at[i], recv_sem=recv_sems.at[i],
            device_id=(x, y, 1 - c), device_id_type=MESH) for i in range(npc)]
        for cp in cps:
            cp.start()
        for cp in cps:
            cp.wait()

    anys = pl.BlockSpec(memory_space=pl.ANY)
    return pl.pallas_call(
        body, in_specs=[anys] * npc, out_specs=[anys] * npc, out_shape=[S(s.shape, s.dtype) for s in to_sib],
        scratch_shapes=[pltpu.SemaphoreType.DMA((npc,)), pltpu.SemaphoreType.DMA((npc,))],
        name=name)(*to_sib)


def _rs_begin(pieces, name):
    c = lax.axis_index("c")
    by_core = [p.reshape(4, 2, p.shape[1], p.shape[2]) for p in pieces]
    to_sib = [lax.dynamic_index_in_dim(p, 1 - c, axis=1, keepdims=False).astype(BF16) for p in by_core]
    keep = [lax.dynamic_index_in_dim(p, c, axis=1, keepdims=False) for p in by_core]
    from_sib = _to_sibling(to_sib, name + "_d2d")

    def add1(a, b):
        s = a + b
        return [s, s], []

    parts, parts_b = [], []
    for i, (k, f) in enumerate(zip(keep, from_sib)):
        _, r, C = k.shape
        p, pb = _rowwise(add1, [k.reshape(4 * r, C), f.reshape(4 * r, C)], [], [(C, F32), (C, BF16)], tm=2048, name=f"{name}_add1_{i}")
        parts.append(p.reshape(4, r, C))
        parts_b.append(pb.reshape(4, r, C))
    return parts, parts_b


def _rs_finish(parts, got, name):
    x, y = lax.axis_index("x"), lax.axis_index("y")

    def add2(a, b, c_, d_):
        return [((a + b) + c_) + d_], []

    outs = []
    for i, (p, g) in enumerate(zip(parts, got)):
        own = lax.dynamic_index_in_dim(p, 2 * x + y, axis=0, keepdims=False)
        outs.append(_rowwise(add2, [own, g[0], g[1], g[2]], [], [(p.shape[2], F32)], tm=2048, name=f"{name}_add2_{i}")[0])
    return outs


class _GradReduce:
    EARLY = ("w_ffn_in", "w_ffn_out", "w_out", "w_attn_branch", "w_ssm_branch")

    def __init__(self):
        self.out, self.keys, self.parts, self.parts_b = {}, [], [], []

    def _begin(self, l, names, gr, name):
        parts, parts_b = _rs_begin([_shard(nm, gr[nm]) for nm in names], name)
        self.keys += [(l, nm) for nm in names]
        self.parts += parts
        self.parts_b += parts_b

    def carry_fn(self, l):
        if l != 0:
            return None

        def fn(gr):
            self._begin(0, self.EARLY, gr, "rs_early_l0")
            return _comm_to_chips(self.parts_b)
        return fn

    def carry2_fn(self, l):
        if l != 0:
            return None

        def fn(gr):
            self.parts2, parts2_b = _rs_begin([_shard("w_in", gr["w_in"])], "rs_w_in_l0")
            return _comm_to_chips(parts2_b)
        return fn

    def done(self, l, gr, carried):
        if l == DEPTH - 1:
            self._begin(l, BIG, gr, f"rs_l{l}")
            return
        for key, o in zip(self.keys, _rs_finish(self.parts, carried[0], "rs_carried")):
            self.out[key] = o
        self.out[(0, "w_in")] = _rs_finish(self.parts2, carried[1], "rs_w_in_l0")[0]


def _all_reduce_small(v, name):
    R, C = v.shape

    def body(x_ref, out_ref, buf, send_sems, recv_sems):
        x, y, c, chips = _place()
        me, sib = (x, y, c), (x, y, 1 - c)

        def rows(dev):
            return buf.at[4 * dev[0] + 2 * dev[1] + dev[2]]

        def copy(k, block, to, src=None):
            return pltpu.make_async_remote_copy(
                src_ref=rows(block) if src is None else src, dst_ref=rows(block),
                send_sem=send_sems.at[k], recv_sem=recv_sems.at[k], device_id=to, device_id_type=MESH)

        buf[4 * x + 2 * y + c] = x_ref[...]
        first = [copy(0, me, sib, src=x_ref)] + [copy(1 + j, me, (*chip, c), src=x_ref) for j, chip in enumerate(chips)]
        for cp in first:
            cp.start()
        passed = [copy(4 + j, (*chip, c), sib) for j, chip in enumerate(chips)]
        for j, chip in enumerate(chips):
            copy(1 + j, (*chip, c), me).wait_recv()
            passed[j].start()
        copy(0, sib, me).wait_recv()
        for j, chip in enumerate(chips):
            copy(4 + j, (*chip, 1 - c), me).wait_recv()
        for cp in first + passed:
            cp.wait_send()
        acc = buf[0]
        for j in range(1, N_DEV):
            acc = acc + buf[j]
        out_ref[...] = acc

    vm = pl.BlockSpec(memory_space=pltpu.VMEM)
    return pl.pallas_call(
        body, in_specs=[vm], out_specs=vm, out_shape=S((R, C), F32),
        scratch_shapes=[pltpu.VMEM((N_DEV, R, C), F32), pltpu.SemaphoreType.DMA((7,)), pltpu.SemaphoreType.DMA((7,))],
        compiler_params=pltpu.CompilerParams(vmem_limit_bytes=VMEM_LIMIT), name=name)(v)


SEG = (("q", 0, 1536), ("k", 1536, 1536), ("v", 3072, 1536), ("z", 4608, 2048), ("xbc", 6656, 3072), ("dt", 9728, 32), ("gl", 9760, 2048))


def _split_w_in(w_in_full):
    out = {}
    for nm, off, n in SEG:
        w = w_in_full[:, off:off + n]
        if nm == "dt":
            w = jnp.pad(w, ((0, 0), (0, LANES - n)))
        out[nm] = w
    W = ATTN_OUT_WIDTH
    out["qkv"] = [jnp.concatenate([out[s][:, g * W:(g + 1) * W] for s in ("q", "k", "v")], axis=1) for g in range(N_DIL)]
    out["qkv_t"] = [[out[s][:, g * W:(g + 1) * W] for s in ("q", "k", "v")] for g in range(N_DIL)]
    return out


def _layer_fwd(h, p, W, biases, cs, l, carry=None, late=None):
    T = h.shape[0]
    nm = lambda s: f"{s}_l{l}"
    sv = {"h_in": h}
    xns = _rmsnorm_fwd(h, p["norm1_w"], nm("norm1"), dils=[d for _, d in DILATED_GROUPS[1:]])
    xn = xns[0]
    wi = W["w_in"]
    z = _mm(xn, wi["z"], out_dtype=BF16, name=nm("proj_z"))
    xbc = _mm(xn, wi["xbc"], name=nm("proj_xbc"))
    dt_raw = _mm(xn, wi["dt"], name=nm("proj_dt"))
    gl = _mm(xn, wi["gl"], out_dtype=BF16, name=nm("proj_gl"))
    conv_args = (xbc, p["conv_w"], p["conv_b"].reshape(1, -1), nm("conv"))
    if late is None:
        u_conv, act = _conv_fwd(*conv_args)
    else:
        (u_conv, act), spread = _conv_fwd(*conv_args, comm=late[0])
    os_, ls, qkvs = [], [], []
    for g, (window, dil) in enumerate(DILATED_GROUPS):
        qkv = _mm(xns[g], wi["qkv"][g], out_dtype=BF16, name=nm(f"proj_qkv_g{g}"))
        if late is not None and g == 0:
            (o, lse), arrived = _attn_fwd(qkv, biases[g], dil, nm(f"attn_fwd_g{g}"), comm=_comm_gather_pass(spread))
            W.update(late[1](arrived))
        else:
            o, lse = _attn_fwd(qkv, biases[g], dil, nm(f"attn_fwd_g{g}"))
        os_.append(o)
        ls.append(lse)
        qkvs.append(qkv)
    attn_b, attn_f = _combine_fwd(os_, ls, nm("combine"))
    dt = _dt_fwd(dt_raw, p["dt_bias"], nm("dt"))
    dtT = dt.T
    dskip_e = jnp.repeat(p["d_skip"], D_INNER // N_SSM_HEADS).reshape(1, -1)
    carried = None
    if carry is None:
        y, states = _ssd_fwd(act, dt, dtT, p["a_log"], dskip_e, cs, nm("ssd_fwd"))
        ssm = _ssm_norm_fwd(y, z, p["ssm_norm_w"], nm("ssm_norm"))
    else:
        (y, states), spread = _ssd_fwd(act, dt, dtT, p["a_log"], dskip_e, cs, nm("ssd_fwd"), comm=carry)
        ssm, carried = _ssm_norm_fwd(y, z, p["ssm_norm_w"], nm("ssm_norm"), comm=_comm_gather_pass(spread))
    a_br = _mm(attn_b, W["w_attn_branch"], out_dtype=BF16, name=nm("attn_branch"))
    s_br = _mm(ssm, W["w_ssm_branch"], out_dtype=BF16, name=nm("ssm_branch"))
    h_mid, merged = _gate_out_proj(a_br, s_br, gl, h, W["w_out"], nm("gate_out_proj"))
    xn2 = _rmsnorm_fwd(h_mid, p["norm2_w"], nm("norm2"))[0]
    W["w_ffn_in_p"] = _ffn_perm(W["w_ffn_in"])
    u_ffn, ffn_act = _mm(xn2, W["w_ffn_in_p"], tm=512, tn=D_FF, epilogue=_swiglu_epilogue, outs=[(2 * D_FF, BF16), (D_FF, BF16)],
                         name=nm("ffn_in_swiglu"))
    h_out = _mm(ffn_act, W["w_ffn_out"], acc=h_mid, name=nm("ffn_out"))
    sv.update(xn=xn, xns=xns, qkvs=qkvs, z=z, xbc=xbc, dt_raw=dt_raw, gl=gl, ls=ls, attn_b=attn_b, attn_f=attn_f, u_conv=u_conv,
              act=act, dt=dt, dtT=dtT, dskip_e=dskip_e, y=y, states=states, ssm=ssm, a_br=a_br, s_br=s_br, merged=merged,
              h_mid=h_mid, xn2=xn2, u_ffn=u_ffn, ffn_act=ffn_act)
    return h_out, sv, carried


def _layer_bwd(dh, sv, p, W, biases, cs, head_ones, l, carry_fn=None, carry2_fn=None):
    T = dh.shape[0]
    nm = lambda s: f"{s}_l{l}"
    gr = {}
    du = _mm(dh, W["w_ffn_out"], tb=True, tm=512, tn=FFN_HALF, extras=[sv["u_ffn"]], epilogue=_dswiglu_epilogue, outs=[(2 * D_FF, BF16)],
             name=nm("d_ffn_act_swiglu"))
    gr["w_ffn_out"] = _mm(sv["ffn_act"], dh, ta=True, name=nm("g_ffn_out"))
    dxn2 = _mm(du, W["w_ffn_in_p"], tb=True, name=nm("d_xn2"))
    gr["w_ffn_in"] = _ffn_unperm(_mm(sv["xn2"], du, ta=True, name=nm("g_ffn_in")))
    dh_mid, gr["norm2_w"] = _rmsnorm_bwd(dxn2, sv["h_mid"], p["norm2_w"], dh, nm("d_norm2"))
    gr["w_out"] = _mm(sv["merged"], dh_mid, ta=True, name=nm("g_out"))
    d_a, d_s, dgl = _d_out_proj_gate(dh_mid, sv["a_br"], sv["s_br"], sv["gl"], W["w_out"], nm("d_out_proj_gate"))
    dattn = _mm(d_a, W["w_attn_branch"], tb=True, out_dtype=BF16, name=nm("d_attn"))
    gr["w_attn_branch"] = _mm(sv["attn_b"], d_a, ta=True, name=nm("g_attn_branch"))
    dssm = _mm(d_s, W["w_ssm_branch"], tb=True, out_dtype=BF16, name=nm("d_ssm"))
    gr["w_ssm_branch"] = _mm(sv["ssm"], d_s, ta=True, name=nm("g_ssm_branch"))
    dy, dz, gr["ssm_norm_w"] = _ssm_norm_bwd(dssm, sv["y"], sv["z"], p["ssm_norm_w"], nm("d_ssm_norm"))
    ssd_args = (dy, sv["act"], sv["dt"], sv["dtT"], p["a_log"], sv["dskip_e"], sv["states"], cs, nm("ssd_bwd"))
    carried = None
    if carry_fn is None:
        dact_c, ddt_a, ddt_bT, da, daT, dskip = _ssd_bwd(*ssd_args)
    else:
        (dact_c, ddt_a, ddt_bT, da, daT, dskip), carried = _ssd_bwd(*ssd_args, comm=carry_fn(gr))
    gr["a_log"] = _a_log_grad(da, daT.T, p["a_log"], nm("g_a_log")).reshape(-1)
    gr["d_skip"] = dskip.reshape(-1)
    ddt_raw, ddt_bias = _dt_bwd(ddt_a, ddt_bT.T, sv["dt_raw"], p["dt_bias"], nm("d_dt"))
    gr["dt_bias"] = ddt_bias.reshape(-1)
    dxbc, gr["conv_w"], dconv_b = _conv_bwd(dact_c, sv["u_conv"], sv["xbc"], p["conv_w"], nm("d_conv"))
    gr["conv_b"] = dconv_b.reshape(-1)
    outs = _combine_bwd(dattn, sv["attn_f"], sv["ls"], head_ones, nm("d_combine"))
    wi = W["w_in"]
    dbias, dxn = [], None
    gqkv = [[None] * N_DIL for _ in range(3)]
    for g, (window, dil) in enumerate(DILATED_GROUPS):
        dq, dk, dv, db = _attn_bwd(sv["qkvs"][g], biases[g], sv["ls"][g], outs[2 * g], outs[2 * g + 1], dil, nm(f"attn_bwd_g{g}"))
        dbias.append(db)
        dxn = _mm_dil([dq, dk, dv], wi["qkv_t"][g], dil, dxn, nm(f"d_xn_qkv_g{g}"))
        for i, dseg in enumerate((dq, dk, dv)):
            gqkv[i][g] = _mm(sv["xns"][g], dseg, ta=True, name=nm(f"g_in_{'qkv'[i]}_g{g}"))
    parts = (("z", dz), ("xbc", dxbc), ("dt", ddt_raw), ("gl", dgl))
    gws = gqkv[0] + gqkv[1] + gqkv[2]
    for sname, dseg in parts:
        gw = _mm(sv["xn"], dseg, ta=True, name=nm("g_in_" + sname))
        gws.append(gw[:, :N_SSM_HEADS] if sname == "dt" else gw)
    gr["w_in"] = jnp.concatenate(gws, axis=1)
    carried2 = None
    for sname, dseg in parts:
        if carry2_fn is not None and sname == "xbc":
            dxn, carried2 = _mm(dseg, wi[sname], tb=True, acc=dxn, name=nm("d_xn_" + sname), comm=carry2_fn(gr))
        else:
            dxn = _mm(dseg, wi[sname], tb=True, acc=dxn, name=nm("d_xn_" + sname))
    dh_in, gr["norm1_w"] = _rmsnorm_bwd(dxn, sv["h_in"], p["norm1_w"], dh_mid, nm("d_norm1"))
    return dh_in, gr, dbias, (carried, carried2)


def _step_local(x, tgt, small, Wfull, rel_bias, final_norm_w, prefetch=None, grad_reduce=None, late0=None):
    cs = _ssd_consts()
    head = np.repeat(np.arange(HEADS_PER_GROUP), HEAD_DIM)
    head_ones = jnp.asarray(head[:, None] == head[None, :], BF16)
    biases, onehots = [], []
    for g, (window, dil) in enumerate(DILATED_GROUPS):
        onehot, valid = _bias_consts(dil, window // dil)
        rel_g_t = rel_bias[:, g * HEADS_PER_GROUP:(g + 1) * HEADS_PER_GROUP].T
        b = _bias_gather(rel_g_t, onehot, valid, f"bias_gather_g{g}")
        biases.append(b.reshape(HEADS_PER_GROUP, ATTN_BLOCK, 2 * ATTN_BLOCK))
        onehots.append(onehot)
    h, saved, carried = x, [], None
    Wfull = list(Wfull)
    for l in range(DEPTH):
        W = dict(prefetch[1](carried) if Wfull[l] is None else Wfull[l])
        W["w_in"] = _split_w_in(W["w_in"])
        Wfull[l] = W
        first = prefetch is not None and l == 0
        h, sv, carried = _layer_fwd(h, small[l], W, biases, cs, l, prefetch[0] if first else None, late0 if l == 0 else None)
        saved.append(sv)
    dh, g_final, loss = _loss_head(h, final_norm_w, tgt, "loss_head")
    grads = [None] * DEPTH
    dbias_tot = [None] * N_DIL
    for l in reversed(range(DEPTH)):
        carry_fn = grad_reduce.carry_fn(l) if grad_reduce is not None else None
        carry2_fn = grad_reduce.carry2_fn(l) if grad_reduce is not None else None
        dh, grads[l], dbias, carried = _layer_bwd(dh, saved[l], small[l], Wfull[l], biases, cs, head_ones, l, carry_fn, carry2_fn)
        if grad_reduce is not None:
            grad_reduce.done(l, grads[l], carried)
        for g in range(N_DIL):
            dbias_tot[g] = dbias[g] if dbias_tot[g] is None else dbias_tot[g] + dbias[g]
    d_rel = jnp.concatenate(
        [_bias_scatter(dbias_tot[g].reshape(HEADS_PER_GROUP, -1), onehots[g], f"bias_scatter_g{g}").T for g in range(N_DIL)], axis=1)
    return loss, dh, grads, d_rel, g_final


def _unshard(nm, g):
    _, rows, cols = g.shape
    if nm in COL_SHARDED:
        return g.transpose(1, 0, 2).reshape(rows, N_DEV * cols)
    return g.reshape(N_DEV * rows, cols)


def _shard(nm, w):
    rows, cols = w.shape
    if nm in COL_SHARDED:
        return w.reshape(rows, N_DEV, cols // N_DEV).transpose(1, 0, 2)
    return w.reshape(N_DEV, rows // N_DEV, cols)


SMALL_LAYER = (("norm1_w", 1024), ("conv_w", 12288), ("conv_b", 3072), ("dt_bias", 32), ("a_log", 32), ("d_skip", 32),
               ("ssm_norm_w", 2048), ("norm2_w", 1024))
SMALL_GLOBAL = (("rel_bias", 768), ("final_norm_w", 1024), ("loss", 1))


def _pad128(v):
    n = v.shape[0]
    return jnp.pad(v, (0, -n % LANES))


def _pack_small(per_layer, glob):
    parts = [_pad128(per_layer[l][nm].reshape(-1)) for l in range(DEPTH) for nm, _ in SMALL_LAYER]
    parts += [_pad128(glob[nm].reshape(-1)) for nm, _ in SMALL_GLOBAL]
    flat = jnp.concatenate(parts)
    flat = jnp.pad(flat, (0, -flat.shape[0] % (8 * LANES)))
    return flat.reshape(-1, LANES)


def _unpack_small(packed):
    flat = packed.reshape(-1)
    per_layer, glob, off = [dict() for _ in range(DEPTH)], {}, 0
    for l in range(DEPTH):
        for nm, n in SMALL_LAYER:
            per_layer[l][nm] = flat[off:off + n]
            off += n + (-n % LANES)
    for nm, n in SMALL_GLOBAL:
        glob[nm] = flat[off:off + n]
        off += n + (-n % LANES)
    return per_layer, glob


def kernel(x, norm1_w, w_in, conv_w, conv_b, dt_bias, a_log, d_skip, ssm_norm_w, w_attn_branch, w_ssm_branch, w_out, norm2_w, w_ffn_in, w_ffn_out, rel_bias, final_norm_w, loss_target, m_norm1_w, m_w_in, m_conv_w, m_conv_b, m_dt_bias, m_a_log, m_d_skip, m_ssm_norm_w, m_w_attn_branch, m_w_ssm_branch, m_w_out, m_norm2_w, m_w_ffn_in, m_w_ffn_out, m_rel_bias, m_final_norm_w, v_norm1_w, v_w_in, v_conv_w, v_conv_b, v_dt_bias, v_a_log, v_d_skip, v_ssm_norm_w, v_w_attn_branch, v_w_ssm_branch, v_w_out, v_norm2_w, v_w_ffn_in, v_w_ffn_out, v_rel_bias, v_final_norm_w):
    big = dict(w_in=w_in, w_attn_branch=w_attn_branch, w_ssm_branch=w_ssm_branch, w_out=w_out, w_ffn_in=w_ffn_in, w_ffn_out=w_ffn_out)
    big_m = dict(w_in=m_w_in, w_attn_branch=m_w_attn_branch, w_ssm_branch=m_w_ssm_branch, w_out=m_w_out, w_ffn_in=m_w_ffn_in, w_ffn_out=m_w_ffn_out)
    big_v = dict(w_in=v_w_in, w_attn_branch=v_w_attn_branch, w_ssm_branch=v_w_ssm_branch, w_out=v_w_out, w_ffn_in=v_w_ffn_in, w_ffn_out=v_w_ffn_out)
    sm = dict(norm1_w=norm1_w, conv_w=conv_w, conv_b=conv_b, dt_bias=dt_bias, a_log=a_log, d_skip=d_skip, ssm_norm_w=ssm_norm_w, norm2_w=norm2_w)
    sm_m = dict(norm1_w=m_norm1_w, conv_w=m_conv_w, conv_b=m_conv_b, dt_bias=m_dt_bias, a_log=m_a_log, d_skip=m_d_skip, ssm_norm_w=m_ssm_norm_w, norm2_w=m_norm2_w)
    sm_v = dict(norm1_w=v_norm1_w, conv_w=v_conv_w, conv_b=v_conv_b, dt_bias=v_dt_bias, a_log=v_a_log, d_skip=v_d_skip, ssm_norm_w=v_ssm_norm_w, norm2_w=v_norm2_w)
    me = 4 * lax.axis_index("x") + 2 * lax.axis_index("y") + lax.axis_index("c")

    def full_weights(gathered, names=BIG):
        return {nm: _unshard(nm, g) for nm, g in zip(names, gathered)}

    later = [nm for nm in BIG if nm != "w_in"]
    Wfull = [full_weights(_all_gather([big["w_in"][0].astype(BF16)], "all_gather_w_in_l0"), ["w_in"]), None]
    late0 = (_comm_gather_spread([big[nm][0].astype(BF16) for nm in later]), functools.partial(full_weights, names=later))
    prefetch = (_comm_gather_spread([big[nm][DEPTH - 1].astype(BF16) for nm in BIG]), full_weights)

    conv_full = []
    for l in range(DEPTH):
        z = jnp.zeros((N_DEV, CONV_WIDTH, XBC_WIDTH // N_DEV), F32)
        conv_full.append(lax.dynamic_update_index_in_dim(z, conv_w[l], me, axis=0))
    cw = jnp.stack(conv_full).reshape(-1, LANES)
    cw = _all_reduce_small(cw, "gather_conv_w").reshape(DEPTH, N_DEV, CONV_WIDTH, XBC_WIDTH // N_DEV)
    cw = cw.transpose(0, 2, 1, 3).reshape(DEPTH, CONV_WIDTH, XBC_WIDTH)

    small = [{nm: (cw[l] if nm == "conv_w" else a[l]) for nm, a in sm.items()} for l in range(DEPTH)]
    grad_reduce = _GradReduce()
    loss, dx, grads, d_rel, g_final = _step_local(x[0], loss_target[0], small, Wfull, rel_bias, final_norm_w, prefetch, grad_reduce, late0)
    g_big = {nm: jnp.stack([grad_reduce.out[(l, nm)] for l in range(DEPTH)]) for nm in BIG}

    per_layer = [{nm: grads[l][nm] for nm, _ in SMALL_LAYER} for l in range(DEPTH)]
    packet = _pack_small(per_layer, dict(rel_bias=d_rel, final_norm_w=g_final, loss=loss[0, :1]))
    per_layer, glob = _unpack_small(_all_reduce_small(packet, "all_reduce_small"))
    g_small = {nm: jnp.stack([per_layer[l][nm] for l in range(DEPTH)]) for nm, _ in SMALL_LAYER}
    cwg = g_small["conv_w"].reshape(DEPTH, CONV_WIDTH, N_DEV, XBC_WIDTH // N_DEV)
    g_small["conv_w"] = lax.dynamic_index_in_dim(cwg, me, axis=2, keepdims=False)
    for nm in sm:
        g_small[nm] = g_small[nm].reshape(sm[nm].shape)
    g_rel = glob["rel_bias"].reshape(rel_bias.shape)
    g_fin = glob["final_norm_w"]
    loss_out = glob["loss"][0]

    def adam(w, g, m, v, name):
        shp = w.shape
        two = lambda a: a.reshape(-1, shp[-1]) if a.ndim > 1 else a.reshape(1, -1)
        d, nm_, nv = _adamw(two(w), two(g), two(m), two(v), name)
        return d.reshape(shp), nm_.reshape(shp), nv.reshape(shp)

    order = ["norm1_w", "w_in", "conv_w", "conv_b", "dt_bias", "a_log", "d_skip", "ssm_norm_w", "w_attn_branch", "w_ssm_branch",
             "w_out", "norm2_w", "w_ffn_in", "w_ffn_out", "rel_bias", "final_norm_w"]
    allw = {**big, **sm, "rel_bias": rel_bias, "final_norm_w": final_norm_w}
    allm = {**big_m, **sm_m, "rel_bias": m_rel_bias, "final_norm_w": m_final_norm_w}
    allv = {**big_v, **sm_v, "rel_bias": v_rel_bias, "final_norm_w": v_final_norm_w}
    allg = {**g_big, **g_small, "rel_bias": g_rel, "final_norm_w": g_fin}
    deltas, new_m, new_v = [], [], []
    for nm in order:
        d, a, b = adam(allw[nm], allg[nm], allm[nm], allv[nm], "adamw_" + nm)
        deltas.append(d)
        new_m.append(a)
        new_v.append(b)
    return (loss_out, dx[None], *[allg[nm] for nm in order], *deltas, *new_m, *new_v)
```

```python
import functools
import math

import numpy as np
import jax
import jax.numpy as jnp
from jax import lax
from jax.experimental import pallas as pl
from jax.experimental.pallas import tpu as pltpu

F32, BF16 = jnp.float32, jnp.bfloat16
S = jax.ShapeDtypeStruct
MESH = pl.DeviceIdType.MESH

D_MODEL = 1024
DEPTH = 2
HEAD_DIM = 64
DILATED_GROUPS = ((128, 1), (512, 4), (2048, 16))
N_DIL = 3
HEADS_PER_GROUP = 8
ATTN_WIDTH = 1536
ATTN_OUT_WIDTH = 512
ATTN_BLOCK = 128
N_REL_BUCKETS = 32
REL_MAX_DISTANCE = 2048
D_INNER = 2048
N_SSM_HEADS = 32
N_SSM_GROUPS = 4
D_STATE = 128
CONV_WIDTH = 4
SSD_CHUNK = 128
XBC_WIDTH = 3072
D_FF = 2816
EPS = 1e-6
ADAM_LR, ADAM_B1, ADAM_B2, ADAM_EPS, ADAM_WD, ADAM_STEP = 0.001, 0.9, 0.999, 1e-08, 0.01, 10

N_DEV = 8
LANES = 128
VMEM_LIMIT = 56 * 1024 * 1024
MM_VMEM_BYTES = 40 * 1024 * 1024
ROW_TILES_BYTES = 36 * 1024 * 1024
NEG = -1e30
BIG = ("w_in", "w_attn_branch", "w_ssm_branch", "w_out", "w_ffn_in", "w_ffn_out")
COL_SHARDED = ("w_in", "w_attn_branch", "w_ffn_in")

NT = (((1,), (1,)), ((), ()))
TN = (((0,), (0,)), ((), ()))


def _cparams(sem=None):
    return pltpu.CompilerParams(dimension_semantics=sem, vmem_limit_bytes=VMEM_LIMIT)


def _pick(n, target, mult=LANES):
    best = None
    for t in range(mult, min(n, target) + 1, mult):
        if n % t == 0:
            best = t
    return best or n


def _silu(x):
    return x * jax.nn.sigmoid(x)


def _dsilu(x):
    s = jax.nn.sigmoid(x)
    return s * (1.0 + x * (1.0 - s))


def _split2(x):
    hi = x.astype(BF16)
    lo = (x - hi.astype(F32)).astype(BF16)
    return hi, lo


def _split3(x):
    x1 = x.astype(BF16)
    r1 = x - x1.astype(F32)
    x2 = r1.astype(BF16)
    x3 = (r1 - x2.astype(F32)).astype(BF16)
    return x1, x2, x3


def _dotx_r(x, m, parts=3):
    xs = _split3(x) if parts == 3 else _split2(x)
    out = jnp.dot(xs[0], m, preferred_element_type=F32)
    for xi in xs[1:]:
        out = out + jnp.dot(xi, m, preferred_element_type=F32)
    return out


def _dotx_l(m, x, parts=3):
    xs = _split3(x) if parts == 3 else _split2(x)
    out = jnp.dot(m, xs[0], preferred_element_type=F32)
    for xi in xs[1:]:
        out = out + jnp.dot(m, xi, preferred_element_type=F32)
    return out


def _mm(a, b, *, ta=False, tb=False, out_dtype=F32, acc=None, name, tm=None, tn=1536, tk=1536, extras=(), epilogue=None, outs=None, comm=None):
    M, K = (a.shape[1], a.shape[0]) if ta else a.shape
    N = b.shape[0] if tb else b.shape[1]
    tn, tk = _pick(N, tn), _pick(K, tk)
    if tm is None:
        def vmem(t, k):
            out_b = jnp.dtype(out_dtype).itemsize
            return (2 * t * k * a.dtype.itemsize + 2 * k * tn * b.dtype.itemsize + 2 * t * tn * out_b
                    + (t * tn * 4 if K > k else 0) + (2 * t * tn * acc.dtype.itemsize if acc is not None else 0))
        tm = _pick(M, 1536)
        while M % (2 * tm) == 0 and vmem(2 * tm, tk) <= MM_VMEM_BYTES:
            tm *= 2
        while K % (2 * tk) == 0 and vmem(tm, 2 * tk) <= MM_VMEM_BYTES:
            tk *= 2
    else:
        tm = _pick(M, tm)
    nk = K // tk
    dims = (((0 if ta else 1,), (1 if tb else 0,)), ((), ()))
    has_acc = acc is not None
    outs = [(N, out_dtype)] if outs is None else outs
    ne, no = len(extras), len(outs)

    def body(*refs):
        a_ref, b_ref = refs[:2]
        c_ref = refs[2] if has_acc else None
        e_refs = refs[2 + has_acc:2 + has_acc + ne]
        o_refs = refs[2 + has_acc + ne:2 + has_acc + ne + no]
        acc_ref = refs[-1]
        k = pl.program_id(2)
        part = lax.dot_general(a_ref[...].astype(BF16), b_ref[...].astype(BF16), dims, preferred_element_type=F32)

        def finish(res):
            tiles = [res] if epilogue is None else epilogue(res, *[e[...] for e in e_refs])
            for o_ref, t in zip(o_refs, tiles):
                o_ref[...] = t.astype(o_ref.dtype)

        if nk == 1:
            finish(part + c_ref[...].astype(F32) if has_acc else part)
        else:
            @pl.when(k == 0)
            def _():
                acc_ref[...] = part + c_ref[...].astype(F32) if has_acc else part

            @pl.when(jnp.logical_and(k > 0, k < nk - 1))
            def _():
                acc_ref[...] += part

            @pl.when(k == nk - 1)
            def _():
                finish(acc_ref[...] + part)

    def cspec(cols):
        return pl.BlockSpec((tm, cols * tn // N), lambda i, j, k: (i, j))

    a_spec = pl.BlockSpec((tk, tm), lambda i, j, k: (k, i)) if ta else pl.BlockSpec((tm, tk), lambda i, j, k: (i, k))
    b_spec = pl.BlockSpec((tn, tk), lambda i, j, k: (j, k)) if tb else pl.BlockSpec((tk, tn), lambda i, j, k: (k, j))
    in_specs, args = [a_spec, b_spec], [a, b]
    if has_acc:
        in_specs.append(cspec(N))
        args.append(acc)
    in_specs += [cspec(e.shape[1]) for e in extras]
    args += list(extras)
    res, comm_res = _carrier_call(
        body, (M // tm, N // tn, nk), in_specs, [cspec(c) for c, _ in outs], [S((M, c), dt) for c, dt in outs],
        [pltpu.VMEM((tm, tn), F32)] if nk > 1 else [], args, comm, name)
    res = res[0] if len(outs) == 1 else res
    return res if comm is None else (res, comm_res)


def _mm_dil(a_list, b_list, d, acc, name, tm=1024):
    T, K = a_list[0].shape
    N = b_list[0].shape[0]
    tm, tn = min(tm, T), _pick(N, 1024)
    na = len(a_list)
    has_acc = acc is not None

    def body(*refs):
        a_refs, b_refs, rest = refs[:na], refs[na:2 * na], refs[2 * na:]
        c_ref = rest[0] if has_acc else None
        o_ref, scr = rest[-2], rest[-1]
        out = c_ref[...] if has_acc else None
        for a_ref, b_ref in zip(a_refs, b_refs):
            a_tok = _dil_to_tok(scr, a_ref, d).astype(BF16) if d > 1 else a_ref[...]
            part = lax.dot_general(a_tok, b_ref[...], NT, preferred_element_type=F32)
            out = part if out is None else out + part
        o_ref[...] = out

    if d > 1:
        a_spec = pl.BlockSpec((d, tm // d, K), lambda i, j: (0, i, 0))
        a_args = [a.reshape(d, T // d, K) for a in a_list]
    else:
        a_spec = pl.BlockSpec((tm, K), lambda i, j: (i, 0))
        a_args = list(a_list)
    o_spec = pl.BlockSpec((tm, tn), lambda i, j: (i, j))
    in_specs = [a_spec] * na + [pl.BlockSpec((tn, K), lambda i, j: (j, 0))] * na + ([o_spec] if has_acc else [])
    return pl.pallas_call(
        body, grid=(T // tm, N // tn), in_specs=in_specs, out_specs=o_spec, out_shape=S((T, N), F32),
        scratch_shapes=[pltpu.VMEM((K // LANES, tm, LANES), F32)],
        compiler_params=_cparams(("parallel", "parallel")), name=name)(*a_args, *b_list, *([acc] if has_acc else []))


class _Comm:
    def __init__(self, ins, outs, sems, start, wait, alias=None):
        self.ins, self.outs, self.sems, self.start, self.wait, self.alias = list(ins), list(outs), list(sems), start, wait, alias or {}


def _carrier_call(body, grid, in_specs, out_specs, out_shape, scratch_shapes, args, comm, name):
    grid = (grid,) if isinstance(grid, int) else tuple(grid)
    seq = ("arbitrary",) * len(grid)
    ni, no, ns = len(in_specs), len(out_specs), len(scratch_shapes)
    if comm is None:
        res = pl.pallas_call(body, grid=grid, in_specs=in_specs, out_specs=out_specs, out_shape=out_shape,
                             scratch_shapes=scratch_shapes, compiler_params=_cparams(seq), name=name)(*args)
        return list(res), []
    ci, co = len(comm.ins), len(comm.outs)

    def wrapped(*refs):
        ins, cins = refs[:ni], refs[ni:ni + ci]
        outs, couts = refs[ni + ci:ni + ci + no], refs[ni + ci + no:ni + ci + no + co]
        scr, csems = refs[ni + ci + no + co:ni + ci + no + co + ns], refs[ni + ci + no + co + ns:]
        ids = [pl.program_id(i) for i in range(len(grid))]
        first = functools.reduce(jnp.logical_and, [i == 0 for i in ids])
        last = functools.reduce(jnp.logical_and, [i == g - 1 for i, g in zip(ids, grid)])

        @pl.when(first)
        def _():
            comm.start(cins, couts, csems)

        body(*ins, *outs, *scr)

        @pl.when(last)
        def _():
            comm.wait(cins, couts, csems)

    anys = pl.BlockSpec(memory_space=pl.ANY)
    res = pl.pallas_call(
        wrapped, grid=grid, in_specs=list(in_specs) + [anys] * ci, out_specs=list(out_specs) + [anys] * co,
        out_shape=list(out_shape) + comm.outs, scratch_shapes=list(scratch_shapes) + comm.sems,
        input_output_aliases={ni + a: no + b for a, b in comm.alias.items()},
        compiler_params=_cparams(seq), name=name)(*args, *comm.ins)
    return list(res[:no]), list(res[no:])


def _run_comm(comm, name):
    ci, co = len(comm.ins), len(comm.outs)

    def body(*refs):
        cins, couts, csems = refs[:ci], refs[ci:ci + co], refs[ci + co:]
        comm.start(cins, couts, csems)
        comm.wait(cins, couts, csems)

    anys = pl.BlockSpec(memory_space=pl.ANY)
    return pl.pallas_call(body, in_specs=[anys] * ci, out_specs=[anys] * co, out_shape=comm.outs, scratch_shapes=comm.sems,
                          input_output_aliases=dict(comm.alias), name=name)(*comm.ins)


def _dev_index(dev):
    return 4 * dev[0] + 2 * dev[1] + dev[2]


def _comm_gather_spread(shards):
    npc = len(shards)

    def copies(x_refs, o_refs, sems):
        x, y, c, chips = _place()
        me = (x, y, c)
        peers = [(x, y, 1 - c)] + [(*chip, c) for chip in chips]
        return [[pltpu.make_async_remote_copy(src_ref=x_refs[i], dst_ref=o_refs[i].at[_dev_index(me)], send_sem=sems[0].at[k, i],
                                              recv_sem=sems[1].at[k, i], device_id=peer, device_id_type=MESH)
                 for k, peer in enumerate(peers)] for i in range(npc)], peers, me

    def local(x_refs, o_refs, sems, me):
        return [pltpu.make_async_copy(x_refs[i], o_refs[i].at[_dev_index(me)], sems[2].at[i]) for i in range(npc)]

    def start(x_refs, o_refs, sems):
        cps, _, me = copies(x_refs, o_refs, sems)
        for cp in local(x_refs, o_refs, sems, me):
            cp.start()
        for row in cps:
            for cp in row:
                cp.start()

    def wait(x_refs, o_refs, sems):
        cps, peers, me = copies(x_refs, o_refs, sems)
        for i in range(npc):
            for k, peer in enumerate(peers):
                pltpu.make_async_remote_copy(src_ref=x_refs[i], dst_ref=o_refs[i].at[_dev_index(peer)], send_sem=sems[0].at[k, i],
                                             recv_sem=sems[1].at[k, i], device_id=peer, device_id_type=MESH).wait_recv()
        for row in cps:
            for cp in row:
                cp.wait_send()
        for cp in local(x_refs, o_refs, sems, me):
            cp.wait()

    return _Comm(shards, [S((N_DEV,) + s.shape, s.dtype) for s in shards],
                 [pltpu.SemaphoreType.DMA((4, npc)), pltpu.SemaphoreType.DMA((4, npc)), pltpu.SemaphoreType.DMA((npc,))], start, wait)


def _comm_gather_pass(gathered):
    npc = len(gathered)

    def copies(o_refs, sems, sent):
        x, y, c, chips = _place()
        return [pltpu.make_async_remote_copy(
            src_ref=o_refs[i].at[_dev_index((*chip, c))], dst_ref=o_refs[i].at[_dev_index((*chip, c if sent else 1 - c))],
            send_sem=sems[0].at[j, i], recv_sem=sems[1].at[j, i], device_id=(x, y, 1 - c), device_id_type=MESH)
            for i in range(npc) for j, chip in enumerate(chips)]

    def start(g_refs, o_refs, sems):
        for cp in copies(o_refs, sems, True):
            cp.start()

    def wait(g_refs, o_refs, sems):
        for cp in copies(o_refs, sems, False):
            cp.wait_recv()
        for cp in copies(o_refs, sems, True):
            cp.wait_send()

    return _Comm(gathered, [S(g.shape, g.dtype) for g in gathered],
                 [pltpu.SemaphoreType.DMA((3, npc)), pltpu.SemaphoreType.DMA((3, npc))], start, wait,
                 alias={i: i for i in range(npc)})


def _comm_to_chips(parts):
    npc = len(parts)

    def copies(p_refs, o_refs, sems):
        x, y, c, chips = _place()
        return [pltpu.make_async_remote_copy(
            src_ref=p_refs[i].at[2 * chip[0] + chip[1]], dst_ref=o_refs[i].at[j], send_sem=sems[0].at[j, i],
            recv_sem=sems[1].at[j, i], device_id=(*chip, c), device_id_type=MESH)
            for i in range(npc) for j, chip in enumerate(chips)]

    def start(p_refs, o_refs, sems):
        for cp in copies(p_refs, o_refs, sems):
            cp.start()

    def wait(p_refs, o_refs, sems):
        for cp in copies(p_refs, o_refs, sems):
            cp.wait()

    return _Comm(parts, [S((3,) + p.shape[1:], p.dtype) for p in parts],
                 [pltpu.SemaphoreType.DMA((3, npc)), pltpu.SemaphoreType.DMA((3, npc))], start, wait)


def _dil_to_tok(scr, ref, d):
    n, C = ref.shape[1], ref.shape[2]
    for r in range(d):
        v = ref[r].astype(F32)
        for cb in range(C // LANES):
            scr.at[cb][pl.ds(r, n, stride=d), :] = v[:, cb * LANES:(cb + 1) * LANES]
    return jnp.concatenate([scr[cb] for cb in range(C // LANES)], axis=1)


def _tok_to_dil(scr, val, ref, d):
    n, C = ref.shape[1], ref.shape[2]
    for cb in range(C // LANES):
        scr[cb] = val[:, cb * LANES:(cb + 1) * LANES].astype(F32)
    for r in range(d):
        ref[r] = jnp.concatenate([scr.at[cb][pl.ds(r, n, stride=d), :] for cb in range(C // LANES)], axis=1).astype(ref.dtype)


def _rowwise(fn, rows, fulls, outs, accs=(), *, tm, name, cap=True, comm=None):
    rows = [r if isinstance(r, tuple) else (r, r.shape[1], 0) for r in rows]
    first = rows[0]
    T = (first[1] if isinstance(first[0], str) else first[0]).shape[0]
    widest = max([r[1].shape[1] if isinstance(r[0], str) else r[1] for r in rows] + [o[0] for o in outs])
    if cap:
        tm = min(tm, max(8, ROW_TILES_BYTES // (2 * (len(rows) + len(outs))) // (4 * widest) // 8 * 8))
    tm = T if T <= tm else _pick(T, tm, 8)
    nr, nf, no, na = len(rows), len(fulls), len(outs), len(accs)
    dil_in = [i for i, r in enumerate(rows) if isinstance(r[0], str) and r[2] > 1]
    dil_out = [i for i, o in enumerate(outs) if len(o) == 3 and o[2] > 1]
    scr_cols = [rows[i][1].shape[1] for i in dil_in] + [outs[i][0] for i in dil_out]

    def body(*refs):
        r, f = refs[:nr], refs[nr:nr + nf]
        o, a = refs[nr + nf:nr + nf + no], refs[nr + nf + no:nr + nf + no + na]
        scr = refs[nr + nf + no + na:]
        tiles = []
        for i, x in enumerate(r):
            if i in dil_in:
                tiles.append(_dil_to_tok(scr[dil_in.index(i)], x, rows[i][2]))
            else:
                tiles.append(x[...].astype(F32))
        ro, ra = fn(*tiles, *[x[...] for x in f])
        for i, (ref, val) in enumerate(zip(o, ro)):
            if i in dil_out:
                _tok_to_dil(scr[len(dil_in) + dil_out.index(i)], val, ref, outs[i][2])
            else:
                ref[...] = val.astype(ref.dtype)
        if na:
            @pl.when(pl.program_id(0) == 0)
            def _():
                for ref in a:
                    ref[...] = jnp.zeros_like(ref)
            for ref, val in zip(a, ra):
                ref[...] += val

    in_specs, args = [], []
    for i, rr in enumerate(rows):
        if isinstance(rr[0], str):
            arr, d = rr[1], rr[2]
            if d > 1:
                in_specs.append(pl.BlockSpec((d, tm // d, arr.shape[1]), lambda i: (0, i, 0)))
                args.append(arr.reshape(d, T // d, arr.shape[1]))
            else:
                in_specs.append(pl.BlockSpec((tm, arr.shape[1]), lambda i: (i, 0)))
                args.append(arr)
        else:
            in_specs.append(pl.BlockSpec((tm, rr[1]), functools.partial(lambda i, cb: (i, cb), cb=rr[2])))
            args.append(rr[0])
    in_specs += [pl.BlockSpec(f.shape, lambda i: (0, 0)) for f in fulls]
    out_specs, out_shape = [], []
    for i, oo in enumerate(outs):
        if i in dil_out:
            d = oo[2]
            out_specs.append(pl.BlockSpec((d, tm // d, oo[0]), lambda i: (0, i, 0)))
            out_shape.append(S((d, T // d, oo[0]), oo[1]))
        else:
            out_specs.append(pl.BlockSpec((tm, oo[0]), lambda i: (i, 0)))
            out_shape.append(S((T, oo[0]), oo[1]))
    out_specs += [pl.BlockSpec(sh, lambda i: (0, 0)) for sh in accs]
    out_shape += [S(sh, F32) for sh in accs]
    res, comm_res = _carrier_call(
        body, T // tm, in_specs, out_specs, out_shape, [pltpu.VMEM((c // LANES, tm, LANES), F32) for c in scr_cols],
        list(args) + list(fulls), comm, name)
    res = [x.reshape(T, x.shape[2]) if i in dil_out else x for i, x in enumerate(res)]
    return res if comm is None else (res, comm_res)


def _rmsnorm_fwd(h, w, name, dils=(), comm=None):
    D = h.shape[1]

    def fn(h, w):
        r = lax.rsqrt(jnp.mean(h * h, axis=-1, keepdims=True) + EPS)
        xn = h * r * w
        return [xn] * (1 + len(dils)), []
    return _rowwise(fn, [h], [w.reshape(1, -1)], [(D, BF16)] + [(D, BF16, d) for d in dils], tm=512, name=name, comm=comm)


def _rmsnorm_bwd(dxn, h, w, dres, name):
    def fn(dxn, h, dres, w):
        r = lax.rsqrt(jnp.mean(h * h, axis=-1, keepdims=True) + EPS)
        n = h * r
        dn = dxn * w
        dh = r * (dn - n * jnp.mean(dn * n, axis=-1, keepdims=True)) + dres
        return [dh], [jnp.sum(dxn * n, axis=0, keepdims=True)]
    D = h.shape[1]
    return _rowwise(fn, [dxn, h, dres], [w.reshape(1, -1)], [(D, F32)], [(1, D)], tm=512, name=name)


def _loss_head(h, w, tgt, name):
    D = h.shape[1]

    def fn(h, tgt, w):
        r = lax.rsqrt(jnp.mean(h * h, axis=-1, keepdims=True) + EPS)
        n = h * r
        e = n * w - tgt
        row_loss = 0.5 * jnp.mean(e * e, axis=-1, keepdims=True)
        dy = e * (1.0 / D)
        dn = dy * w
        dh = r * (dn - n * jnp.mean(dn * n, axis=-1, keepdims=True))
        return [dh], [jnp.sum(dy * n, axis=0, keepdims=True), jnp.broadcast_to(jnp.sum(row_loss, axis=0, keepdims=True), (1, LANES))]
    return _rowwise(fn, [h, tgt], [w.reshape(1, -1)], [(D, F32)], [(1, D), (1, LANES)], tm=256, name=name)


def _combine_fwd(os_, ls, name):
    def fn(o0, o1, o2, l0, l1, l2):
        m = jnp.maximum(jnp.maximum(l0, l1), l2)
        e0, e1, e2 = jnp.exp(l0 - m), jnp.exp(l1 - m), jnp.exp(l2 - m)
        attn = (e0 * o0 + e1 * o1 + e2 * o2) / (e0 + e1 + e2)
        return [attn, attn], []
    dil = [("dil", t, d) for t, (_, d) in zip(list(os_) + list(ls), DILATED_GROUPS * 2)]
    return _rowwise(fn, dil, [], [(ATTN_OUT_WIDTH, BF16), (ATTN_OUT_WIDTH, F32)], tm=512, name=name)


def _combine_bwd(dattn, attn, ls, head_ones, name):
    def fn(dattn, attn, l0, l1, l2, ones):
        m = jnp.maximum(jnp.maximum(l0, l1), l2)
        e0, e1, e2 = jnp.exp(l0 - m), jnp.exp(l1 - m), jnp.exp(l2 - m)
        inv = 1.0 / (e0 + e1 + e2)
        t = _dotx_r(dattn * attn, ones, parts=2)
        outs = []
        for e in (e0, e1, e2):
            al = e * inv
            outs += [al * dattn, al * t]
        return outs, []
    W = ATTN_OUT_WIDTH
    dil = [("dil", t, d) for t, (_, d) in zip(ls, DILATED_GROUPS)]
    outs = [(W, dt, d) for _, d in DILATED_GROUPS for dt in (BF16, F32)]
    return _rowwise(fn, [dattn, attn] + dil, [head_ones], outs, tm=512, name=name)


def _dt_fwd(dt_raw, dt_bias, name):
    def fn(raw, b):
        z = raw[:, :N_SSM_HEADS] + b
        return [jnp.maximum(z, 0.0) + jnp.log(1.0 + jnp.exp(-jnp.abs(z)))], []
    return _rowwise(fn, [dt_raw], [dt_bias.reshape(1, -1)], [(N_SSM_HEADS, F32)], tm=1024, name=name)[0]


def _dt_bwd(ddt_a, ddt_b, dt_raw, dt_bias, name):
    def fn(da, db, raw, b):
        g = (da + db) * jax.nn.sigmoid(raw[:, :N_SSM_HEADS] + b)
        pad = jnp.zeros((g.shape[0], LANES - N_SSM_HEADS), F32)
        return [jnp.concatenate([g, pad], axis=1)], [jnp.sum(g, axis=0, keepdims=True)]
    return _rowwise(fn, [ddt_a, ddt_b, dt_raw], [dt_bias.reshape(1, -1)], [(LANES, BF16)], [(1, N_SSM_HEADS)], tm=1024, name=name)


def _ssm_norm_fwd(y, z, w, name, comm=None):
    G = D_INNER // N_SSM_GROUPS

    def fn(y, z, w):
        yg = y * _silu(z)
        outs = []
        for g in range(N_SSM_GROUPS):
            t = yg[:, g * G:(g + 1) * G]
            outs.append(t * lax.rsqrt(jnp.mean(t * t, axis=-1, keepdims=True) + EPS))
        return [jnp.concatenate(outs, axis=1) * w], []
    res = _rowwise(fn, [y, z], [w.reshape(1, -1)], [(D_INNER, BF16)], tm=256, name=name, comm=comm)
    return res[0] if comm is None else (res[0][0], res[1])


def _ssm_norm_bwd(dssm, y, z, w, name):
    G = D_INNER // N_SSM_GROUPS

    def fn(dssm, y, z, w):
        sz = _silu(z)
        yg = y * sz
        dn = dssm * w
        ns, dygs = [], []
        for g in range(N_SSM_GROUPS):
            t = yg[:, g * G:(g + 1) * G]
            r = lax.rsqrt(jnp.mean(t * t, axis=-1, keepdims=True) + EPS)
            n = t * r
            d = dn[:, g * G:(g + 1) * G]
            dygs.append(r * (d - n * jnp.mean(d * n, axis=-1, keepdims=True)))
            ns.append(n)
        n, dyg = jnp.concatenate(ns, axis=1), jnp.concatenate(dygs, axis=1)
        return [dyg * sz, dyg * y * _dsilu(z)], [jnp.sum(dssm * n, axis=0, keepdims=True)]
    return _rowwise(fn, [dssm, y, z], [w.reshape(1, -1)], [(D_INNER, BF16), (D_INNER, BF16)], [(1, D_INNER)], tm=256, name=name)


def _gate_out_proj(a, sb, gl, h, w_out, name):
    def fn(a, sb, gl, h, w):
        g = jax.nn.sigmoid(gl)
        merged = (g[:, :D_MODEL] * a + g[:, D_MODEL:] * sb).astype(BF16)
        return [h + jnp.dot(merged, w, preferred_element_type=F32), merged], []
    return _rowwise(fn, [a, sb, gl, h], [w_out], [(D_MODEL, F32), (D_MODEL, BF16)], tm=512, cap=False, name=name)


def _d_out_proj_gate(dh, a, sb, gl, w_out, name):
    def fn(dh, a, sb, gl, w):
        dm = lax.dot_general(dh.astype(BF16), w, NT, preferred_element_type=F32)
        g = jax.nn.sigmoid(gl)
        g0, g1 = g[:, :D_MODEL], g[:, D_MODEL:]
        dgl = jnp.concatenate([dm * a * g0 * (1.0 - g0), dm * sb * g1 * (1.0 - g1)], axis=1)
        return [g0 * dm, g1 * dm, dgl], []
    return _rowwise(fn, [dh, a, sb, gl], [w_out], [(D_MODEL, BF16), (D_MODEL, BF16), (2 * D_MODEL, BF16)], tm=512, cap=False, name=name)


FFN_HALF = D_FF // 2


def _ffn_perm(w):
    h = FFN_HALF
    return jnp.concatenate([w[:, 0:h], w[:, D_FF:D_FF + h], w[:, h:D_FF], w[:, D_FF + h:]], axis=1)


def _ffn_unperm(w):
    h = FFN_HALF
    return jnp.concatenate([w[:, 0:h], w[:, 2 * h:3 * h], w[:, h:2 * h], w[:, 3 * h:]], axis=1)


def _swiglu_epilogue(res):
    return [res, _silu(res[:, :FFN_HALF]) * res[:, FFN_HALF:]]


def _dswiglu_epilogue(dact, u):
    u = u.astype(F32)
    gate, up = u[:, :FFN_HALF], u[:, FFN_HALF:]
    return [jnp.concatenate([dact * up * _dsilu(gate), dact * _silu(gate)], axis=1)]


def _adamw(w, g, m, v, name):
    c1 = 1.0 - ADAM_B1 ** ADAM_STEP
    c2 = 1.0 - ADAM_B2 ** ADAM_STEP

    def fn(w, g, m, v):
        m = ADAM_B1 * m + (1.0 - ADAM_B1) * g
        v = ADAM_B2 * v + (1.0 - ADAM_B2) * (g * g)
        delta = -ADAM_LR * ((m / c1) / (jnp.sqrt(v / c2) + ADAM_EPS) + ADAM_WD * w)
        return [delta, m, v], []
    C = w.shape[1]
    return _rowwise(fn, [w, g, m, v], [], [(C, F32)] * 3, tm=256, name=name)


def _bias_consts(dilation, n_steps):
    qi = np.arange(ATTN_BLOCK)[:, None]
    kj = np.arange(2 * ATTN_BLOCK)[None, :]
    steps = qi + ATTN_BLOCK - kj
    valid = (steps >= 0) & (steps <= n_steps)
    dist = jnp.asarray(np.clip(steps, 0, n_steps) * dilation, jnp.int32)
    max_exact = N_REL_BUCKETS // 2
    d_f = jnp.maximum(dist, 1).astype(F32)
    large = max_exact + (jnp.log(d_f / max_exact) / math.log(REL_MAX_DISTANCE / max_exact)
                         * (N_REL_BUCKETS - max_exact)).astype(jnp.int32)
    large = jnp.minimum(large, N_REL_BUCKETS - 1)
    bucket = jnp.where(dist < max_exact, dist, large).reshape(-1)
    onehot = (bucket[None, :] == jnp.arange(N_REL_BUCKETS)[:, None]).astype(F32)
    return onehot, jnp.asarray(valid.reshape(1, -1), F32)


def _bias_gather(rel_g_t, onehot, valid, name):
    def body(r_ref, oh_ref, v_ref, o_ref):
        b = jnp.dot(r_ref[...], oh_ref[...], preferred_element_type=F32, precision=lax.Precision.HIGHEST)
        o_ref[...] = jnp.where(v_ref[...] > 0.5, b, NEG)
    return pl.pallas_call(body, out_shape=S((HEADS_PER_GROUP, onehot.shape[1]), F32), compiler_params=_cparams(), name=name)(rel_g_t, onehot, valid)


def _bias_scatter(dbias, onehot, name):
    def body(d_ref, oh_ref, o_ref):
        o_ref[...] = lax.dot_general(d_ref[...], oh_ref[...], NT, preferred_element_type=F32, precision=lax.Precision.HIGHEST)
    return pl.pallas_call(body, out_shape=S((HEADS_PER_GROUP, N_REL_BUCKETS), F32), compiler_params=_cparams(), name=name)(dbias, onehot)


ATTN_QB_FWD, ATTN_QB_BWD = 4, 4


def _attn_tiles(T, d, qb):
    seg = T // d
    nqb = min(qb, seg // ATTN_BLOCK)
    tq = nqb * ATTN_BLOCK
    return seg, nqb, tq, seg // tq


def _attn_fwd(qkv, bias, d, name, comm=None):
    T = qkv.shape[0]
    seg, nqb, tq, ns = _attn_tiles(T, d, ATTN_QB_FWD)
    W = ATTN_OUT_WIDTH
    scale = HEAD_DIM ** -0.5

    def body(q_ref, kh_ref, kc_ref, vh_ref, vc_ref, b_ref, o_ref, l_ref, s_scr, p_scr):
        n = pl.program_id(1)
        qv = q_ref[...]
        kk = jnp.concatenate([kh_ref[...], kc_ref[...]], axis=0)
        vv = jnp.concatenate([vh_ref[...], vc_ref[...]], axis=0)
        col = lax.broadcasted_iota(jnp.int32, (ATTN_BLOCK, 2 * ATTN_BLOCK), 1)
        kill = jnp.logical_and(n == 0, col < ATTN_BLOCK)
        lo = lax.broadcasted_iota(jnp.int32, (1, LANES), 1) < HEAD_DIM
        zero = jnp.zeros((), BF16)
        for j in range(nqb):
            rows = slice(j * ATTN_BLOCK, (j + 1) * ATTN_BLOCK)
            keys = slice(j * ATTN_BLOCK, (j + 2) * ATTN_BLOCK)
            for hp in range(HEADS_PER_GROUP // 2):
                ps = slice(hp * LANES, (hp + 1) * LANES)
                q2 = (qv[rows, ps].astype(F32) * scale).astype(BF16)
                k2 = kk[keys, ps]
                s_scr[2 * hp] = lax.dot_general(q2, jnp.where(lo, k2, zero), NT, preferred_element_type=F32)
                s_scr[2 * hp + 1] = lax.dot_general(q2, jnp.where(lo, zero, k2), NT, preferred_element_type=F32)
            s = s_scr[...] + b_ref[...]
            if j == 0:
                s = jnp.where(kill[None], NEG, s)
            m = jnp.max(s, axis=-1, keepdims=True)
            p = jnp.exp(s - m)
            den = jnp.sum(p, axis=-1, keepdims=True)
            p_scr[...] = p.astype(BF16)
            inv = 1.0 / den
            lse = m + jnp.log(den)
            for hp in range(HEADS_PER_GROUP // 2):
                ps = slice(hp * LANES, (hp + 1) * LANES)
                v2 = vv[keys, ps]
                o2 = (jnp.dot(p_scr[2 * hp], jnp.where(lo, v2, zero), preferred_element_type=F32)
                      + jnp.dot(p_scr[2 * hp + 1], jnp.where(lo, zero, v2), preferred_element_type=F32))
                o_ref[rows, ps] = (o2 * jnp.where(lo, inv[2 * hp], inv[2 * hp + 1])).astype(o_ref.dtype)
                l_ref[rows, ps] = jnp.where(lo, lse[2 * hp], lse[2 * hp + 1])

    def cur(c):
        return pl.BlockSpec((tq, W), lambda r, n: (r * ns + n, c))

    def halo(c):
        return pl.BlockSpec((ATTN_BLOCK, W), lambda r, n: (jnp.maximum((r * ns + n) * nqb - 1, 0), c))

    res, comm_res = _carrier_call(
        body, (d, ns),
        [cur(0), halo(1), cur(1), halo(2), cur(2), pl.BlockSpec(bias.shape, lambda r, n: (0, 0, 0))],
        [cur(0), cur(0)], [S((T, W), BF16), S((T, W), F32)],
        [pltpu.VMEM((HEADS_PER_GROUP, ATTN_BLOCK, 2 * ATTN_BLOCK), F32), pltpu.VMEM((HEADS_PER_GROUP, ATTN_BLOCK, 2 * ATTN_BLOCK), BF16)],
        [qkv, qkv, qkv, qkv, qkv, bias], comm, name)
    return res if comm is None else (res, comm_res)


def _attn_bwd(qkv, bias, lse, do, dd, d, name):
    T = qkv.shape[0]
    seg, nqb, tq, ns = _attn_tiles(T, d, ATTN_QB_BWD)
    W = ATTN_OUT_WIDTH
    B = ATTN_BLOCK
    scale = HEAD_DIM ** -0.5

    def body(q_ref, kh_ref, kc_ref, vh_ref, vc_ref, b_ref, l_ref, do_ref, dd_ref, dq_ref, dk_ref, dv_ref, db_ref, pk_ref, pv_ref,
             s_scr, dp_scr, p_scr, ds_scr):
        r, n = pl.program_id(0), pl.program_id(1)

        @pl.when(jnp.logical_and(r == 0, n == 0))
        def _():
            db_ref[...] = jnp.zeros_like(db_ref)

        @pl.when(n == 0)
        def _():
            pk_ref[...] = jnp.zeros_like(pk_ref)
            pv_ref[...] = jnp.zeros_like(pv_ref)

        @pl.when(n < ns)
        def _():
            qv = q_ref[...]
            kk = jnp.concatenate([kh_ref[...], kc_ref[...]], axis=0)
            vv = jnp.concatenate([vh_ref[...], vc_ref[...]], axis=0)
            lse_v, do_v, dd_v = l_ref[...], do_ref[...], dd_ref[...]
            col = lax.broadcasted_iota(jnp.int32, (B, 2 * B), 1)
            kill = jnp.logical_and(n == 0, col < B)
            dqs = [[None] * (HEADS_PER_GROUP // 2) for _ in range(nqb)]
            dks = [[None] * (HEADS_PER_GROUP // 2) for _ in range(nqb)]
            dvs = [[None] * (HEADS_PER_GROUP // 2) for _ in range(nqb)]
            H, HP = HEADS_PER_GROUP, HEADS_PER_GROUP // 2
            do_b = do_v.astype(BF16)
            lo = lax.broadcasted_iota(jnp.int32, (1, LANES), 1) < HEAD_DIM
            zero = jnp.zeros((), BF16)
            first = lambda t: jnp.where(lo, t, zero)
            second = lambda t: jnp.where(lo, zero, t)
            for j in range(nqb):
                rows = slice(j * B, (j + 1) * B)
                keys = slice(j * B, (j + 2) * B)
                for hp in range(HP):
                    ps = slice(hp * LANES, (hp + 1) * LANES)
                    q2 = (qv[rows, ps].astype(F32) * scale).astype(BF16)
                    k2, v2, do2 = kk[keys, ps], vv[keys, ps], do_b[rows, ps]
                    s_scr[2 * hp] = lax.dot_general(q2, first(k2), NT, preferred_element_type=F32)
                    s_scr[2 * hp + 1] = lax.dot_general(q2, second(k2), NT, preferred_element_type=F32)
                    dp_scr[2 * hp] = lax.dot_general(do2, first(v2), NT, preferred_element_type=F32)
                    dp_scr[2 * hp + 1] = lax.dot_general(do2, second(v2), NT, preferred_element_type=F32)
                lse_h = jnp.stack([lse_v[rows, h * HEAD_DIM:h * HEAD_DIM + 1] for h in range(H)], axis=0)
                dd_h = jnp.stack([dd_v[rows, h * HEAD_DIM:h * HEAD_DIM + 1] for h in range(H)], axis=0)
                s = s_scr[...] + b_ref[...]
                if j == 0:
                    s = jnp.where(kill[None], NEG, s)
                p = jnp.exp(s - lse_h)
                ds = p * (dp_scr[...] - dd_h)
                db_ref[...] += ds
                p_scr[...] = p.astype(BF16)
                ds_scr[...] = ds.astype(BF16)
                for hp in range(HP):
                    ps = slice(hp * LANES, (hp + 1) * LANES)
                    q2 = (qv[rows, ps].astype(F32) * scale).astype(BF16)
                    k2, do2 = kk[keys, ps], do_b[rows, ps]
                    pa, pb, da, db_ = p_scr[2 * hp], p_scr[2 * hp + 1], ds_scr[2 * hp], ds_scr[2 * hp + 1]
                    dvs[j][hp] = (lax.dot_general(pa, first(do2), TN, preferred_element_type=F32)
                                  + lax.dot_general(pb, second(do2), TN, preferred_element_type=F32))
                    dqs[j][hp] = (jnp.dot(da, first(k2), preferred_element_type=F32)
                                  + jnp.dot(db_, second(k2), preferred_element_type=F32)) * scale
                    dks[j][hp] = (lax.dot_general(da, first(q2), TN, preferred_element_type=F32)
                                  + lax.dot_general(db_, second(q2), TN, preferred_element_type=F32))
            dq_ref[...] = jnp.concatenate([jnp.concatenate(dqs[j], axis=1) for j in range(nqb)], axis=0).astype(dq_ref.dtype)
            for parts, out_ref, pend in ((dks, dk_ref, pk_ref), (dvs, dv_ref, pv_ref)):
                full = [jnp.concatenate(parts[j], axis=1) for j in range(nqb)]
                if tq > B:
                    out_ref[:tq - B] = pend[:tq - B].astype(out_ref.dtype)
                out_ref[tq - B:] = (pend[tq - B:] + full[0][:B]).astype(out_ref.dtype)
                for j in range(nqb - 1):
                    pend[j * B:(j + 1) * B] = full[j][B:] + full[j + 1][:B]
                pend[tq - B:] = full[nqb - 1][B:]

        @pl.when(n == ns)
        def _():
            dk_ref[...] = pk_ref[...].astype(dk_ref.dtype)
            dv_ref[...] = pv_ref[...].astype(dv_ref.dtype)

    def cur(c):
        return pl.BlockSpec((tq, W), lambda r, n: (r * ns + jnp.minimum(n, ns - 1), c))

    def halo(c):
        return pl.BlockSpec((B, W), lambda r, n: (jnp.maximum((r * ns + jnp.minimum(n, ns - 1)) * nqb - 1, 0), c))

    late = pl.BlockSpec((tq, W), lambda r, n: (r * ns + jnp.clip(n - 1, 0, ns - 1), 0))
    bspec = pl.BlockSpec(bias.shape, lambda r, n: (0, 0, 0))
    return pl.pallas_call(
        body, grid=(d, ns + 1),
        in_specs=[cur(0), halo(1), cur(1), halo(2), cur(2), bspec, cur(0), cur(0), cur(0)],
        out_specs=[cur(0), late, late, bspec],
        out_shape=[S((T, W), BF16)] * 3 + [S(bias.shape, F32)],
        scratch_shapes=[pltpu.VMEM((tq, W), F32), pltpu.VMEM((tq, W), F32)]
                       + [pltpu.VMEM((HEADS_PER_GROUP, B, 2 * B), t) for t in (F32, F32, BF16, BF16)],
        compiler_params=_cparams(("arbitrary", "arbitrary")), name=name,
    )(qkv, qkv, qkv, qkv, qkv, bias, lse, do, dd)


CONV_TM, CONV_TC = 512, 1024


def _shift_down(x, halo8, s, row8):
    xr = pltpu.roll(x, s, 0)
    first = jnp.where(row8 < s, pltpu.roll(halo8, s, 0), xr[:8])
    return jnp.concatenate([first, xr[8:]], axis=0)


def _shift_up(x, halo8, s, row8):
    n = x.shape[0]
    xr = pltpu.roll(x, n - s, 0)
    last = jnp.where(row8 >= 8 - s, pltpu.roll(halo8, 8 - s, 0), xr[n - 8:])
    return jnp.concatenate([xr[:n - 8], last], axis=0)


def _conv_fwd(x, w, b, name, comm=None):
    T, C = x.shape
    tm, tc = min(CONV_TM, T), CONV_TC

    def body(x_ref, p_ref, w_ref, b_ref, u_ref, a_ref):
        ti = pl.program_id(1)
        xv = x_ref[...]
        p8 = jnp.where(ti == 0, 0.0, p_ref[...])
        wv = w_ref[...]
        row8 = lax.broadcasted_iota(jnp.int32, (8, tc), 0)
        u = xv * wv[3:4] + b_ref[...]
        for s in (1, 2, 3):
            u = u + _shift_down(xv, p8, s, row8) * wv[3 - s:4 - s]
        u_ref[...] = u
        a_ref[...] = _silu(u)

    cur = pl.BlockSpec((tm, tc), lambda cj, ti: (ti, cj))
    halo = pl.BlockSpec((8, tc), lambda cj, ti: (jnp.maximum(ti * (tm // 8) - 1, 0), cj))
    res, comm_res = _carrier_call(
        body, (C // tc, T // tm),
        [cur, halo, pl.BlockSpec((CONV_WIDTH, tc), lambda cj, ti: (0, cj)), pl.BlockSpec((1, tc), lambda cj, ti: (0, cj))],
        [cur, cur], [S((T, C), F32)] * 2, [], [x, x, w, b], comm, name)
    return res if comm is None else (res, comm_res)


def _conv_bwd(dact, u, x, w, name):
    T, C = x.shape
    tm, tc = min(CONV_TM, T), CONV_TC
    nt = T // tm

    def body(d_ref, dn_ref, u_ref, un_ref, x_ref, w_ref, dx_ref, dw_ref, db_ref):
        ti = pl.program_id(1)

        @pl.when(ti == 0)
        def _():
            dw_ref[...] = jnp.zeros_like(dw_ref)
            db_ref[...] = jnp.zeros_like(db_ref)

        du = d_ref[...] * _dsilu(u_ref[...])
        dun = jnp.where(ti == nt - 1, 0.0, dn_ref[...] * _dsilu(un_ref[...]))
        xv = x_ref[...]
        wv = w_ref[...]
        row8 = lax.broadcasted_iota(jnp.int32, (8, tc), 0)
        dx = du * wv[3:4]
        dws = [None] * CONV_WIDTH
        dws[3] = jnp.sum(du * xv, axis=0, keepdims=True)
        for s in (1, 2, 3):
            up = _shift_up(du, dun, s, row8)
            dx = dx + up * wv[3 - s:4 - s]
            dws[3 - s] = jnp.sum(up * xv, axis=0, keepdims=True)
        dx_ref[...] = dx.astype(dx_ref.dtype)
        dw_ref[...] += jnp.concatenate(dws, axis=0)
        db_ref[...] += jnp.sum(du, axis=0, keepdims=True)

    cur = pl.BlockSpec((tm, tc), lambda cj, ti: (ti, cj))
    nxt = pl.BlockSpec((8, tc), lambda cj, ti: (jnp.minimum((ti + 1) * (tm // 8), T // 8 - 1), cj))
    return pl.pallas_call(
        body, grid=(C // tc, nt),
        in_specs=[cur, nxt, cur, nxt, cur, pl.BlockSpec((CONV_WIDTH, tc), lambda cj, ti: (0, cj))],
        out_specs=[cur, pl.BlockSpec((CONV_WIDTH, tc), lambda cj, ti: (0, cj)), pl.BlockSpec((1, tc), lambda cj, ti: (0, cj))],
        out_shape=[S((T, C), BF16), S((CONV_WIDTH, C), F32), S((1, C), F32)],
        compiler_params=_cparams(("parallel", "arbitrary")), name=name)(dact, dact, u, u, x, w)


def _ssd_consts():
    i = np.arange(SSD_CHUNK)
    tril = (i[None, :] <= i[:, None]).astype(np.float32)
    trils = (i[None, :] < i[:, None]).astype(np.float32)
    head = np.repeat(np.arange(N_SSM_HEADS), D_INNER // N_SSM_HEADS)
    et = (head[None, :] == np.arange(N_SSM_HEADS)[:, None]).astype(np.float32)
    c = lambda a: jnp.asarray(a, BF16)
    return dict(tril=c(tril), triu=c(tril.T), trils=c(trils), et=c(et), e=c(et.T))


def _ssd_common(act_ref, dt_ref, dtT_ref, al_ref, alT_ref, tril_ref, triu_ref, et_ref):
    a_row = -jnp.exp(al_ref[...])
    a_col = -jnp.exp(alT_ref[...])
    dt, dtT = dt_ref[...], dtT_ref[...]
    la = _dotx_l(tril_ref[...], dt * a_row)
    laT = _dotx_r(dtT * a_col, triu_ref[...])
    et = et_ref[...]
    la_e = _dotx_r(la, et)
    dt_e = _dotx_r(dt, et, parts=2)
    x = act_ref[:, :D_INNER]
    xdt = x * dt_e
    la_q = la_e[SSD_CHUNK - 1:SSD_CHUNK, :]
    return a_row, a_col, dt, dtT, la, laT, la_e, dt_e, x, xdt, la_q


def _decay(la, laT, h, causal):
    seg = la[:, h:h + 1] - laT[h:h + 1, :]
    return jnp.exp(jnp.where(causal, seg, NEG))


def _ssd_fwd(act, dt, dtT, alog, dskip_e, cs, name, comm=None):
    T = act.shape[0]
    nc = T // SSD_CHUNK
    Q, G, GW = SSD_CHUNK, N_SSM_GROUPS, D_INNER // N_SSM_GROUPS

    def body(act_ref, dt_ref, dtT_ref, al_ref, alT_ref, dsk_ref, tril_ref, triu_ref, et_ref, y_ref, st_ref, scr):
        @pl.when(pl.program_id(0) == 0)
        def _():
            scr[...] = jnp.zeros_like(scr)
        st_ref[0] = scr[...]
        a_row, a_col, dtv, dtTv, la, laT, la_e, dt_e, x, xdt, la_q = _ssd_common(
            act_ref, dt_ref, dtT_ref, al_ref, alT_ref, tril_ref, triu_ref, et_ref)
        ela = jnp.exp(la_e)
        xdt_b = xdt.astype(BF16)
        xdte_b = (xdt * jnp.exp(la_q - la_e)).astype(BF16)
        ela_q = jnp.exp(la_q)
        causal = lax.broadcasted_iota(jnp.int32, (Q, Q), 0) >= lax.broadcasted_iota(jnp.int32, (Q, Q), 1)
        for g in range(G):
            gs = slice(g * GW, (g + 1) * GW)
            Bg = act_ref[:, D_INNER + g * D_STATE:D_INNER + (g + 1) * D_STATE].astype(BF16)
            Cg = act_ref[:, D_INNER + G * D_STATE + g * D_STATE:D_INNER + G * D_STATE + (g + 1) * D_STATE].astype(BF16)
            cb = lax.dot_general(Cg, Bg, NT, preferred_element_type=F32)
            st = scr[g]
            y_inter = jnp.dot(Cg, st.astype(BF16), preferred_element_type=F32) * ela[:, gs]
            ys = []
            for hh in range(HEADS_PER_GROUP):
                h = g * HEADS_PER_GROUP + hh
                m = (cb * _decay(la, laT, h, causal)).astype(BF16)
                ys.append(jnp.dot(m, xdt_b[:, h * HEAD_DIM:(h + 1) * HEAD_DIM], preferred_element_type=F32))
            y_ref[:, gs] = (jnp.concatenate(ys, axis=1) + y_inter + x[:, gs] * dsk_ref[:, gs]).astype(y_ref.dtype)
            scr[g] = st * ela_q[:, gs] + lax.dot_general(Bg, xdte_b[:, gs], TN, preferred_element_type=F32)

    full = lambda a: pl.BlockSpec(a.shape, lambda c: (0,) * a.ndim)
    al, alT = alog.reshape(1, -1), alog.reshape(-1, 1)
    res, comm_res = _carrier_call(
        body, nc,
        [pl.BlockSpec((Q, XBC_WIDTH), lambda c: (c, 0)), pl.BlockSpec((Q, N_SSM_HEADS), lambda c: (c, 0)),
         pl.BlockSpec((N_SSM_HEADS, Q), lambda c: (0, c)), full(al), full(alT), full(dskip_e),
         full(cs["tril"]), full(cs["triu"]), full(cs["et"])],
        [pl.BlockSpec((Q, D_INNER), lambda c: (c, 0)), pl.BlockSpec((1, G, D_STATE, GW), lambda c: (c, 0, 0, 0))],
        [S((T, D_INNER), BF16), S((nc, G, D_STATE, GW), F32)],
        [pltpu.VMEM((G, D_STATE, GW), F32)],
        [act, dt, dtT, al, alT, dskip_e, cs["tril"], cs["triu"], cs["et"]], comm, name)
    return res if comm is None else (res, comm_res)


def _ssd_bwd(dy, act, dt, dtT, alog, dskip_e, states, cs, name, comm=None):
    T = act.shape[0]
    nc = T // SSD_CHUNK
    Q, G, GW, H = SSD_CHUNK, N_SSM_GROUPS, D_INNER // N_SSM_GROUPS, N_SSM_HEADS

    def body(dy_ref, act_ref, dt_ref, dtT_ref, al_ref, alT_ref, dsk_ref, stp_ref, tril_ref, triu_ref, trils_ref,
             et_ref, e_ref, dact_ref, ddt_ref, ddtT_ref, da_ref, daT_ref, dsk_out_ref, dst, wbuf, ubuf, vbuf, sbuf, dm_scr, m_scr):
        @pl.when(pl.program_id(0) == 0)
        def _():
            dst[...] = jnp.zeros_like(dst)
            da_ref[...] = jnp.zeros_like(da_ref)
            daT_ref[...] = jnp.zeros_like(daT_ref)
            dsk_out_ref[...] = jnp.zeros_like(dsk_out_ref)
        a_row, a_col, dtv, dtTv, la, laT, la_e, dt_e, x, xdt, la_q = _ssd_common(
            act_ref, dt_ref, dtT_ref, al_ref, alT_ref, tril_ref, triu_ref, et_ref)
        dyv = dy_ref[...].astype(F32)
        ela = jnp.exp(la_e)
        e_end = jnp.exp(la_q - la_e)
        ela_q = jnp.exp(la_q)
        dye_b = (dyv * ela).astype(BF16)
        dy_b = dyv.astype(BF16)
        xdt_b = xdt.astype(BF16)
        xdte_b = (xdt * e_end).astype(BF16)
        ri = lax.broadcasted_iota(jnp.int32, (Q, Q), 0)
        ci = lax.broadcasted_iota(jnp.int32, (Q, Q), 1)
        causal = ri >= ci
        rows = []
        for g in range(G):
            gs = slice(g * GW, (g + 1) * GW)
            Bg = act_ref[:, D_INNER + g * D_STATE:D_INNER + (g + 1) * D_STATE].astype(BF16)
            Cg = act_ref[:, D_INNER + G * D_STATE + g * D_STATE:D_INNER + G * D_STATE + (g + 1) * D_STATE].astype(BF16)
            cb = lax.dot_general(Cg, Bg, NT, preferred_element_type=F32)
            stp = stp_ref[0, g]
            stp_b = stp.astype(BF16)
            dstv = dst[g]
            dst_b = dstv.astype(BF16)
            y_inter = jnp.dot(Cg, stp_b, preferred_element_type=F32) * ela[:, gs]
            wbuf[:, gs] = dyv[:, gs] * y_inter
            dxdt_state = jnp.dot(Bg, dst_b, preferred_element_type=F32) * e_end[:, gs]
            ubuf[:, gs] = dxdt_state * xdt[:, gs]
            dC = lax.dot_general(dye_b[:, gs], stp_b, NT, preferred_element_type=F32)
            dB = lax.dot_general(xdte_b[:, gs], dst_b, NT, preferred_element_type=F32)
            sbuf[:, gs] = jnp.broadcast_to(jnp.sum(dstv * stp, axis=0, keepdims=True), (8, GW))
            dst[g] = dstv * ela_q[:, gs] + lax.dot_general(Cg, dye_b[:, gs], TN, preferred_element_type=F32)
            for hh in range(HEADS_PER_GROUP):
                hs = slice((g * HEADS_PER_GROUP + hh) * HEAD_DIM, (g * HEADS_PER_GROUP + hh + 1) * HEAD_DIM)
                dm_scr[hh] = lax.dot_general(dy_b[:, hs], xdt_b[:, hs], NT, preferred_element_type=F32)
            dG = jnp.zeros((Q, Q), F32)
            for hh in range(HEADS_PER_GROUP):
                L = _decay(la, laT, g * HEADS_PER_GROUP + hh, causal)
                M = cb * L
                dM = dm_scr[hh]
                dG = dG + dM * L
                W = dM * M
                rows.append(jnp.sum(W.T, axis=0, keepdims=True) - jnp.sum(W, axis=0, keepdims=True))
                m_scr[hh] = M.astype(BF16)
            dxs = []
            for hh in range(HEADS_PER_GROUP):
                hs = slice((g * HEADS_PER_GROUP + hh) * HEAD_DIM, (g * HEADS_PER_GROUP + hh + 1) * HEAD_DIM)
                dxs.append(lax.dot_general(m_scr[hh], dy_b[:, hs], TN, preferred_element_type=F32))
            dG_b = dG.astype(BF16)
            dC = dC + jnp.dot(dG_b, Bg, preferred_element_type=F32)
            dB = dB + lax.dot_general(dG_b, Cg, TN, preferred_element_type=F32)
            dxdt = jnp.concatenate(dxs, axis=1) + dxdt_state
            vbuf[:, gs] = dxdt * x[:, gs]
            dact_ref[:, gs] = dxdt * dt_e[:, gs] + dyv[:, gs] * dsk_ref[:, gs]
            dact_ref[:, D_INNER + g * D_STATE:D_INNER + (g + 1) * D_STATE] = dB
            dact_ref[:, D_INNER + G * D_STATE + g * D_STATE:D_INNER + G * D_STATE + (g + 1) * D_STATE] = dC
        e = e_ref[...]
        w = _dotx_r(wbuf[...], e, parts=2)
        u = _dotx_r(ubuf[...], e, parts=2)
        vx = _dotx_r(vbuf[...], e, parts=2)
        dsk = _dotx_r(jnp.broadcast_to(jnp.sum(dyv * x, axis=0, keepdims=True), (8, D_INNER)), e, parts=2)[0:1]
        s0 =_dotx_r(sbuf[...], e, parts=2)[0:1] * jnp.exp(la[Q - 1:Q, :])
        ddelta = _dotx_l(triu_ref[...], w) + _dotx_l(trils_ref[...], u) + s0
        ddt_ref[...] = ddelta * a_row + vx
        ddeltaT = _dotx_r(jnp.concatenate(rows, axis=0), tril_ref[...])
        ddtT_ref[...] = ddeltaT * a_col
        da_ref[...] += jnp.sum(ddelta * dtv, axis=0, keepdims=True)
        daT_ref[...] += jnp.sum(ddeltaT * dtTv, axis=1, keepdims=True)
        dsk_out_ref[...] += dsk

    rev = lambda c: nc - 1 - c
    full = lambda a: pl.BlockSpec(a.shape, lambda c: (0,) * a.ndim)
    al, alT = alog.reshape(1, -1), alog.reshape(-1, 1)
    consts = [cs[k] for k in ("tril", "triu", "trils", "et", "e")]
    res, comm_res = _carrier_call(
        body, nc,
        [pl.BlockSpec((Q, D_INNER), lambda c: (rev(c), 0)), pl.BlockSpec((Q, XBC_WIDTH), lambda c: (rev(c), 0)),
         pl.BlockSpec((Q, H), lambda c: (rev(c), 0)), pl.BlockSpec((H, Q), lambda c: (0, rev(c))),
         full(al), full(alT), full(dskip_e), pl.BlockSpec((1, G, D_STATE, GW), lambda c: (rev(c), 0, 0, 0))]
        + [full(a) for a in consts],
        [pl.BlockSpec((Q, XBC_WIDTH), lambda c: (rev(c), 0)), pl.BlockSpec((Q, H), lambda c: (rev(c), 0)),
         pl.BlockSpec((H, Q), lambda c: (0, rev(c))), pl.BlockSpec((1, H), lambda c: (0, 0)),
         pl.BlockSpec((H, 1), lambda c: (0, 0)), pl.BlockSpec((1, H), lambda c: (0, 0))],
        [S((T, XBC_WIDTH), F32), S((T, H), F32), S((H, T), F32), S((1, H), F32), S((H, 1), F32), S((1, H), F32)],
        [pltpu.VMEM((G, D_STATE, GW), F32), pltpu.VMEM((Q, D_INNER), F32), pltpu.VMEM((Q, D_INNER), F32),
         pltpu.VMEM((Q, D_INNER), F32), pltpu.VMEM((8, D_INNER), F32),
         pltpu.VMEM((HEADS_PER_GROUP, Q, Q), F32), pltpu.VMEM((HEADS_PER_GROUP, Q, Q), BF16)],
        [dy, act, dt, dtT, al, alT, dskip_e, states] + consts, comm, name)
    return res if comm is None else (res, comm_res)


def _a_log_grad(da, daT_row, alog, name):
    def body(a_ref, b_ref, al_ref, o_ref):
        o_ref[...] = (a_ref[...] + b_ref[...]) * (-jnp.exp(al_ref[...]))
    return pl.pallas_call(body, out_shape=S((1, N_SSM_HEADS), F32), name=name)(da, daT_row, alog.reshape(1, -1))


def _place():
    x, y, c = lax.axis_index("x"), lax.axis_index("y"), lax.axis_index("c")
    return x, y, c, [(1 - x, y), (x, 1 - y), (1 - x, 1 - y)]


def _to_sibling(to_sib, name):
    npc = len(to_sib)

    def body(*refs):
        s_refs, o_refs, send_sems, recv_sems = refs[:npc], refs[npc:2 * npc], refs[2 * npc], refs[2 * npc + 1]
        x, y, c, _ = _place()
        cps = [pltpu.make_async_remote_copy(
            src_ref=s_refs[i], dst_ref=o_refs[i], send_sem=send_sems.at[i], recv_sem=recv_sems.at[i],
            device_id=(x, y, 1 - c), device_id_type=MESH) for i in range(npc)]
        for cp in cps:
            cp.start()
        for cp in cps:
            cp.wait()

    anys = pl.BlockSpec(memory_space=pl.ANY)
    return pl.pallas_call(
        body, in_specs=[anys] * npc, out_specs=[anys] * npc, out_shape=[S(s.shape, s.dtype) for s in to_sib],
        scratch_shapes=[pltpu.SemaphoreType.DMA((npc,)), pltpu.SemaphoreType.DMA((npc,))],
        name=name)(*to_sib)


def _rs_begin(pieces, name):
    c = lax.axis_index("c")
    by_core = [p.reshape(4, 2, p.shape[1], p.shape[2]) for p in pieces]
    to_sib = [lax.dynamic_index_in_dim(p, 1 - c, axis=1, keepdims=False).astype(BF16) for p in by_core]
    keep = [lax.dynamic_index_in_dim(p, c, axis=1, keepdims=False) for p in by_core]
    from_sib = _to_sibling(to_sib, name + "_d2d")

    def add1(a, b):
        s = a + b
        return [s, s], []

    parts, parts_b = [], []
    for i, (k, f) in enumerate(zip(keep, from_sib)):
        _, r, C = k.shape
        p, pb = _rowwise(add1, [k.reshape(4 * r, C), f.reshape(4 * r, C)], [], [(C, F32), (C, BF16)], tm=2048, name=f"{name}_add1_{i}")
        parts.append(p.reshape(4, r, C))
        parts_b.append(pb.reshape(4, r, C))
    return parts, parts_b


def _rs_finish(parts, got, name):
    x, y = lax.axis_index("x"), lax.axis_index("y")

    def add2(a, b, c_, d_):
        return [((a + b) + c_) + d_], []

    outs = []
    for i, (p, g) in enumerate(zip(parts, got)):
        own = lax.dynamic_index_in_dim(p, 2 * x + y, axis=0, keepdims=False)
        outs.append(_rowwise(add2, [own, g[0], g[1], g[2]], [], [(p.shape[2], F32)], tm=2048, name=f"{name}_add2_{i}")[0])
    return outs


class _GradReduce:
    EARLY = ("w_ffn_in", "w_ffn_out", "w_out", "w_attn_branch", "w_ssm_branch")

    def __init__(self):
        self.out, self.keys, self.parts, self.parts_b = {}, [], [], []

    def _begin(self, l, names, gr, name):
        parts, parts_b = _rs_begin([_shard(nm, gr[nm]) for nm in names], name)
        self.keys += [(l, nm) for nm in names]
        self.parts += parts
        self.parts_b += parts_b

    def carry_fn(self, l):
        if l != 0:
            return None

        def fn(gr):
            self._begin(0, self.EARLY, gr, "rs_early_l0")
            return _comm_to_chips(self.parts_b)
        return fn

    def carry2_fn(self, l):
        if l != 0:
            return None

        def fn(gr):
            self.parts2, parts2_b = _rs_begin([_shard("w_in", gr["w_in"])], "rs_w_in_l0")
            return _comm_to_chips(parts2_b)
        return fn

    def done(self, l, gr, carried):
        if l == DEPTH - 1:
            self._begin(l, BIG, gr, f"rs_l{l}")
            return
        for key, o in zip(self.keys, _rs_finish(self.parts, carried[0], "rs_carried")):
            self.out[key] = o
        self.out[(0, "w_in")] = _rs_finish(self.parts2, carried[1], "rs_w_in_l0")[0]


def _all_reduce_small(v, name):
    R, C = v.shape

    def body(x_ref, out_ref, buf, send_sems, recv_sems):
        x, y, c, chips = _place()
        me, sib = (x, y, c), (x, y, 1 - c)

        def rows(dev):
            return buf.at[4 * dev[0] + 2 * dev[1] + dev[2]]

        def copy(k, block, to, src=None):
            return pltpu.make_async_remote_copy(
                src_ref=rows(block) if src is None else src, dst_ref=rows(block),
                send_sem=send_sems.at[k], recv_sem=recv_sems.at[k], device_id=to, device_id_type=MESH)

        buf[4 * x + 2 * y + c] = x_ref[...]
        first = [copy(0, me, sib, src=x_ref)] + [copy(1 + j, me, (*chip, c), src=x_ref) for j, chip in enumerate(chips)]
        for cp in first:
            cp.start()
        passed = [copy(4 + j, (*chip, c), sib) for j, chip in enumerate(chips)]
        for j, chip in enumerate(chips):
            copy(1 + j, (*chip, c), me).wait_recv()
            passed[j].start()
        copy(0, sib, me).wait_recv()
        for j, chip in enumerate(chips):
            copy(4 + j, (*chip, 1 - c), me).wait_recv()
        for cp in first + passed:
            cp.wait_send()
        acc = buf[0]
        for j in range(1, N_DEV):
            acc = acc + buf[j]
        out_ref[...] = acc

    vm = pl.BlockSpec(memory_space=pltpu.VMEM)
    return pl.pallas_call(
        body, in_specs=[vm], out_specs=vm, out_shape=S((R, C), F32),
        scratch_shapes=[pltpu.VMEM((N_DEV, R, C), F32), pltpu.SemaphoreType.DMA((7,)), pltpu.SemaphoreType.DMA((7,))],
        compiler_params=pltpu.CompilerParams(vmem_limit_bytes=VMEM_LIMIT), name=name)(v)


SEG = (("q", 0, 1536), ("k", 1536, 1536), ("v", 3072, 1536), ("z", 4608, 2048), ("xbc", 6656, 3072), ("dt", 9728, 32), ("gl", 9760, 2048))


def _split_w_in(w_in_full):
    out = {}
    for nm, off, n in SEG:
        w = w_in_full[:, off:off + n]
        if nm == "dt":
            w = jnp.pad(w, ((0, 0), (0, LANES - n)))
        out[nm] = w
    W = ATTN_OUT_WIDTH
    out["qkv"] = [jnp.concatenate([out[s][:, g * W:(g + 1) * W] for s in ("q", "k", "v")], axis=1) for g in range(N_DIL)]
    out["qkv_t"] = [[out[s][:, g * W:(g + 1) * W] for s in ("q", "k", "v")] for g in range(N_DIL)]
    return out


def _layer_fwd(h, p, W, biases, cs, l, carry=None, late=None, early=None):
    T = h.shape[0]
    nm = lambda s: f"{s}_l{l}"
    sv = {"h_in": h}
    dils = [d for _, d in DILATED_GROUPS[1:]]
    if early is None:
        xns = _rmsnorm_fwd(h, p["norm1_w"], nm("norm1"), dils=dils)
    else:
        xns, spread = _rmsnorm_fwd(h, p["norm1_w"], nm("norm1"), dils=dils, comm=early[0])
        W["w_in"] = _split_w_in(early[1](_run_comm(_comm_gather_pass(spread), nm("pass_w_in")))["w_in"])
    xn = xns[0]
    wi = W["w_in"]
    z = _mm(xn, wi["z"], out_dtype=BF16, name=nm("proj_z"))
    xbc = _mm(xn, wi["xbc"], name=nm("proj_xbc"))
    dt_raw = _mm(xn, wi["dt"], name=nm("proj_dt"))
    gl = _mm(xn, wi["gl"], out_dtype=BF16, name=nm("proj_gl"))
    conv_args = (xbc, p["conv_w"], p["conv_b"].reshape(1, -1), nm("conv"))
    if late is None:
        u_conv, act = _conv_fwd(*conv_args)
    else:
        (u_conv, act), spread = _conv_fwd(*conv_args, comm=late[0])
    os_, ls, qkvs = [], [], []
    for g, (window, dil) in enumerate(DILATED_GROUPS):
        qkv = _mm(xns[g], wi["qkv"][g], out_dtype=BF16, name=nm(f"proj_qkv_g{g}"))
        if late is not None and g == 0:
            (o, lse), arrived = _attn_fwd(qkv, biases[g], dil, nm(f"attn_fwd_g{g}"), comm=_comm_gather_pass(spread))
            W.update(late[1](arrived))
        else:
            o, lse = _attn_fwd(qkv, biases[g], dil, nm(f"attn_fwd_g{g}"))
        os_.append(o)
        ls.append(lse)
        qkvs.append(qkv)
    attn_b, attn_f = _combine_fwd(os_, ls, nm("combine"))
    dt = _dt_fwd(dt_raw, p["dt_bias"], nm("dt"))
    dtT = dt.T
    dskip_e = jnp.repeat(p["d_skip"], D_INNER // N_SSM_HEADS).reshape(1, -1)
    carried = None
    if carry is None:
        y, states = _ssd_fwd(act, dt, dtT, p["a_log"], dskip_e, cs, nm("ssd_fwd"))
        ssm = _ssm_norm_fwd(y, z, p["ssm_norm_w"], nm("ssm_norm"))
    else:
        (y, states), spread = _ssd_fwd(act, dt, dtT, p["a_log"], dskip_e, cs, nm("ssd_fwd"), comm=carry)
        ssm, carried = _ssm_norm_fwd(y, z, p["ssm_norm_w"], nm("ssm_norm"), comm=_comm_gather_pass(spread))
    a_br = _mm(attn_b, W["w_attn_branch"], out_dtype=BF16, name=nm("attn_branch"))
    s_br = _mm(ssm, W["w_ssm_branch"], out_dtype=BF16, name=nm("ssm_branch"))
    h_mid, merged = _gate_out_proj(a_br, s_br, gl, h, W["w_out"], nm("gate_out_proj"))
    xn2 = _rmsnorm_fwd(h_mid, p["norm2_w"], nm("norm2"))[0]
    W["w_ffn_in_p"] = _ffn_perm(W["w_ffn_in"])
    u_ffn, ffn_act = _mm(xn2, W["w_ffn_in_p"], tm=512, tn=D_FF, epilogue=_swiglu_epilogue, outs=[(2 * D_FF, BF16), (D_FF, BF16)],
                         name=nm("ffn_in_swiglu"))
    h_out = _mm(ffn_act, W["w_ffn_out"], acc=h_mid, name=nm("ffn_out"))
    sv.update(xn=xn, xns=xns, qkvs=qkvs, z=z, xbc=xbc, dt_raw=dt_raw, gl=gl, ls=ls, attn_b=attn_b, attn_f=attn_f, u_conv=u_conv,
              act=act, dt=dt, dtT=dtT, dskip_e=dskip_e, y=y, states=states, ssm=ssm, a_br=a_br, s_br=s_br, merged=merged,
              h_mid=h_mid, xn2=xn2, u_ffn=u_ffn, ffn_act=ffn_act)
    return h_out, sv, carried


def _layer_bwd(dh, sv, p, W, biases, cs, head_ones, l, carry_fn=None, carry2_fn=None):
    T = dh.shape[0]
    nm = lambda s: f"{s}_l{l}"
    gr = {}
    du = _mm(dh, W["w_ffn_out"], tb=True, tm=512, tn=FFN_HALF, extras=[sv["u_ffn"]], epilogue=_dswiglu_epilogue, outs=[(2 * D_FF, BF16)],
             name=nm("d_ffn_act_swiglu"))
    gr["w_ffn_out"] = _mm(sv["ffn_act"], dh, ta=True, name=nm("g_ffn_out"))
    dxn2 = _mm(du, W["w_ffn_in_p"], tb=True, out_dtype=BF16, name=nm("d_xn2"))
    gr["w_ffn_in"] = _ffn_unperm(_mm(sv["xn2"], du, ta=True, name=nm("g_ffn_in")))
    dh_mid, gr["norm2_w"] = _rmsnorm_bwd(dxn2, sv["h_mid"], p["norm2_w"], dh, nm("d_norm2"))
    gr["w_out"] = _mm(sv["merged"], dh_mid, ta=True, name=nm("g_out"))
    d_a, d_s, dgl = _d_out_proj_gate(dh_mid, sv["a_br"], sv["s_br"], sv["gl"], W["w_out"], nm("d_out_proj_gate"))
    dattn = _mm(d_a, W["w_attn_branch"], tb=True, out_dtype=BF16, name=nm("d_attn"))
    gr["w_attn_branch"] = _mm(sv["attn_b"], d_a, ta=True, name=nm("g_attn_branch"))
    dssm = _mm(d_s, W["w_ssm_branch"], tb=True, out_dtype=BF16, name=nm("d_ssm"))
    gr["w_ssm_branch"] = _mm(sv["ssm"], d_s, ta=True, name=nm("g_ssm_branch"))
    dy, dz, gr["ssm_norm_w"] = _ssm_norm_bwd(dssm, sv["y"], sv["z"], p["ssm_norm_w"], nm("d_ssm_norm"))
    ssd_args = (dy, sv["act"], sv["dt"], sv["dtT"], p["a_log"], sv["dskip_e"], sv["states"], cs, nm("ssd_bwd"))
    carried = None
    if carry_fn is None:
        dact_c, ddt_a, ddt_bT, da, daT, dskip = _ssd_bwd(*ssd_args)
    else:
        (dact_c, ddt_a, ddt_bT, da, daT, dskip), carried = _ssd_bwd(*ssd_args, comm=carry_fn(gr))
    gr["a_log"] = _a_log_grad(da, daT.T, p["a_log"], nm("g_a_log")).reshape(-1)
    gr["d_skip"] = dskip.reshape(-1)
    ddt_raw, ddt_bias = _dt_bwd(ddt_a, ddt_bT.T, sv["dt_raw"], p["dt_bias"], nm("d_dt"))
    gr["dt_bias"] = ddt_bias.reshape(-1)
    dxbc, gr["conv_w"], dconv_b = _conv_bwd(dact_c, sv["u_conv"], sv["xbc"], p["conv_w"], nm("d_conv"))
    gr["conv_b"] = dconv_b.reshape(-1)
    outs = _combine_bwd(dattn, sv["attn_f"], sv["ls"], head_ones, nm("d_combine"))
    wi = W["w_in"]
    dbias, dxn = [], None
    gqkv = [[None] * N_DIL for _ in range(3)]
    for g, (window, dil) in enumerate(DILATED_GROUPS):
        dq, dk, dv, db = _attn_bwd(sv["qkvs"][g], biases[g], sv["ls"][g], outs[2 * g], outs[2 * g + 1], dil, nm(f"attn_bwd_g{g}"))
        dbias.append(db)
        dxn = _mm_dil([dq, dk, dv], wi["qkv_t"][g], dil, dxn, nm(f"d_xn_qkv_g{g}"))
        for i, dseg in enumerate((dq, dk, dv)):
            gqkv[i][g] = _mm(sv["xns"][g], dseg, ta=True, name=nm(f"g_in_{'qkv'[i]}_g{g}"))
    parts = (("z", dz), ("xbc", dxbc), ("dt", ddt_raw), ("gl", dgl))
    gws = gqkv[0] + gqkv[1] + gqkv[2]
    for sname, dseg in parts:
        gw = _mm(sv["xn"], dseg, ta=True, name=nm("g_in_" + sname))
        gws.append(gw[:, :N_SSM_HEADS] if sname == "dt" else gw)
    gr["w_in"] = jnp.concatenate(gws, axis=1)
    carried2 = None
    for sname, dseg in parts:
        if carry2_fn is not None and sname == "xbc":
            dxn, carried2 = _mm(dseg, wi[sname], tb=True, acc=dxn, name=nm("d_xn_" + sname), comm=carry2_fn(gr))
        else:
            dxn = _mm(dseg, wi[sname], tb=True, acc=dxn, name=nm("d_xn_" + sname))
    dh_in, gr["norm1_w"] = _rmsnorm_bwd(dxn, sv["h_in"], p["norm1_w"], dh_mid, nm("d_norm1"))
    return dh_in, gr, dbias, (carried, carried2)


def _step_local(x, tgt, small, Wfull, rel_bias, final_norm_w, prefetch=None, grad_reduce=None, late0=None, early0=None):
    cs = _ssd_consts()
    head = np.repeat(np.arange(HEADS_PER_GROUP), HEAD_DIM)
    head_ones = jnp.asarray(head[:, None] == head[None, :], BF16)
    biases, onehots = [], []
    for g, (window, dil) in enumerate(DILATED_GROUPS):
        onehot, valid = _bias_consts(dil, window // dil)
        rel_g_t = rel_bias[:, g * HEADS_PER_GROUP:(g + 1) * HEADS_PER_GROUP].T
        b = _bias_gather(rel_g_t, onehot, valid, f"bias_gather_g{g}")
        biases.append(b.reshape(HEADS_PER_GROUP, ATTN_BLOCK, 2 * ATTN_BLOCK))
        onehots.append(onehot)
    h, saved, carried = x, [], None
    Wfull = list(Wfull)
    for l in range(DEPTH):
        W = dict(prefetch[1](carried) if Wfull[l] is None else Wfull[l])
        if "w_in" in W:
            W["w_in"] = _split_w_in(W["w_in"])
        Wfull[l] = W
        first = prefetch is not None and l == 0
        h, sv, carried = _layer_fwd(h, small[l], W, biases, cs, l, prefetch[0] if first else None,
                                    late0 if l == 0 else None, early0 if l == 0 else None)
        saved.append(sv)
    dh, g_final, loss = _loss_head(h, final_norm_w, tgt, "loss_head")
    grads = [None] * DEPTH
    dbias_tot = [None] * N_DIL
    for l in reversed(range(DEPTH)):
        carry_fn = grad_reduce.carry_fn(l) if grad_reduce is not None else None
        carry2_fn = grad_reduce.carry2_fn(l) if grad_reduce is not None else None
        dh, grads[l], dbias, carried = _layer_bwd(dh, saved[l], small[l], Wfull[l], biases, cs, head_ones, l, carry_fn, carry2_fn)
        if grad_reduce is not None:
            grad_reduce.done(l, grads[l], carried)
        for g in range(N_DIL):
            dbias_tot[g] = dbias[g] if dbias_tot[g] is None else dbias_tot[g] + dbias[g]
    d_rel = jnp.concatenate(
        [_bias_scatter(dbias_tot[g].reshape(HEADS_PER_GROUP, -1), onehots[g], f"bias_scatter_g{g}").T for g in range(N_DIL)], axis=1)
    return loss, dh, grads, d_rel, g_final


def _unshard(nm, g):
    _, rows, cols = g.shape
    if nm in COL_SHARDED:
        return g.transpose(1, 0, 2).reshape(rows, N_DEV * cols)
    return g.reshape(N_DEV * rows, cols)


def _shard(nm, w):
    rows, cols = w.shape
    if nm in COL_SHARDED:
        return w.reshape(rows, N_DEV, cols // N_DEV).transpose(1, 0, 2)
    return w.reshape(N_DEV, rows // N_DEV, cols)


SMALL_LAYER = (("norm1_w", 1024), ("conv_w", 12288), ("conv_b", 3072), ("dt_bias", 32), ("a_log", 32), ("d_skip", 32),
               ("ssm_norm_w", 2048), ("norm2_w", 1024))
SMALL_GLOBAL = (("rel_bias", 768), ("final_norm_w", 1024), ("loss", 1))


def _pad128(v):
    n = v.shape[0]
    return jnp.pad(v, (0, -n % LANES))


def _pack_small(per_layer, glob):
    parts = [_pad128(per_layer[l][nm].reshape(-1)) for l in range(DEPTH) for nm, _ in SMALL_LAYER]
    parts += [_pad128(glob[nm].reshape(-1)) for nm, _ in SMALL_GLOBAL]
    flat = jnp.concatenate(parts)
    flat = jnp.pad(flat, (0, -flat.shape[0] % (8 * LANES)))
    return flat.reshape(-1, LANES)


def _unpack_small(packed):
    flat = packed.reshape(-1)
    per_layer, glob, off = [dict() for _ in range(DEPTH)], {}, 0
    for l in range(DEPTH):
        for nm, n in SMALL_LAYER:
            per_layer[l][nm] = flat[off:off + n]
            off += n + (-n % LANES)
    for nm, n in SMALL_GLOBAL:
        glob[nm] = flat[off:off + n]
        off += n + (-n % LANES)
    return per_layer, glob


def kernel(x, norm1_w, w_in, conv_w, conv_b, dt_bias, a_log, d_skip, ssm_norm_w, w_attn_branch, w_ssm_branch, w_out, norm2_w, w_ffn_in, w_ffn_out, rel_bias, final_norm_w, loss_target, m_norm1_w, m_w_in, m_conv_w, m_conv_b, m_dt_bias, m_a_log, m_d_skip, m_ssm_norm_w, m_w_attn_branch, m_w_ssm_branch, m_w_out, m_norm2_w, m_w_ffn_in, m_w_ffn_out, m_rel_bias, m_final_norm_w, v_norm1_w, v_w_in, v_conv_w, v_conv_b, v_dt_bias, v_a_log, v_d_skip, v_ssm_norm_w, v_w_attn_branch, v_w_ssm_branch, v_w_out, v_norm2_w, v_w_ffn_in, v_w_ffn_out, v_rel_bias, v_final_norm_w):
    big = dict(w_in=w_in, w_attn_branch=w_attn_branch, w_ssm_branch=w_ssm_branch, w_out=w_out, w_ffn_in=w_ffn_in, w_ffn_out=w_ffn_out)
    big_m = dict(w_in=m_w_in, w_attn_branch=m_w_attn_branch, w_ssm_branch=m_w_ssm_branch, w_out=m_w_out, w_ffn_in=m_w_ffn_in, w_ffn_out=m_w_ffn_out)
    big_v = dict(w_in=v_w_in, w_attn_branch=v_w_attn_branch, w_ssm_branch=v_w_ssm_branch, w_out=v_w_out, w_ffn_in=v_w_ffn_in, w_ffn_out=v_w_ffn_out)
    sm = dict(norm1_w=norm1_w, conv_w=conv_w, conv_b=conv_b, dt_bias=dt_bias, a_log=a_log, d_skip=d_skip, ssm_norm_w=ssm_norm_w, norm2_w=norm2_w)
    sm_m = dict(norm1_w=m_norm1_w, conv_w=m_conv_w, conv_b=m_conv_b, dt_bias=m_dt_bias, a_log=m_a_log, d_skip=m_d_skip, ssm_norm_w=m_ssm_norm_w, norm2_w=m_norm2_w)
    sm_v = dict(norm1_w=v_norm1_w, conv_w=v_conv_w, conv_b=v_conv_b, dt_bias=v_dt_bias, a_log=v_a_log, d_skip=v_d_skip, ssm_norm_w=v_ssm_norm_w, norm2_w=v_norm2_w)
    me = 4 * lax.axis_index("x") + 2 * lax.axis_index("y") + lax.axis_index("c")

    def full_weights(gathered, names=BIG):
        return {nm: _unshard(nm, g) for nm, g in zip(names, gathered)}

    later = [nm for nm in BIG if nm != "w_in"]
    Wfull = [{}, None]
    early0 = (_comm_gather_spread([big["w_in"][0].astype(BF16)]), functools.partial(full_weights, names=["w_in"]))
    late0 = (_comm_gather_spread([big[nm][0].astype(BF16) for nm in later]), functools.partial(full_weights, names=later))
    prefetch = (_comm_gather_spread([big[nm][DEPTH - 1].astype(BF16) for nm in BIG]), full_weights)

    conv_full = []
    for l in range(DEPTH):
        z = jnp.zeros((N_DEV, CONV_WIDTH, XBC_WIDTH // N_DEV), F32)
        conv_full.append(lax.dynamic_update_index_in_dim(z, conv_w[l], me, axis=0))
    cw = jnp.stack(conv_full).reshape(-1, LANES)
    cw = _all_reduce_small(cw, "gather_conv_w").reshape(DEPTH, N_DEV, CONV_WIDTH, XBC_WIDTH // N_DEV)
    cw = cw.transpose(0, 2, 1, 3).reshape(DEPTH, CONV_WIDTH, XBC_WIDTH)

    small = [{nm: (cw[l] if nm == "conv_w" else a[l]) for nm, a in sm.items()} for l in range(DEPTH)]
    grad_reduce = _GradReduce()
    loss, dx, grads, d_rel, g_final = _step_local(x[0], loss_target[0], small, Wfull, rel_bias, final_norm_w, prefetch, grad_reduce, late0, early0)
    g_big = {nm: jnp.stack([grad_reduce.out[(l, nm)] for l in range(DEPTH)]) for nm in BIG}

    per_layer = [{nm: grads[l][nm] for nm, _ in SMALL_LAYER} for l in range(DEPTH)]
    packet = _pack_small(per_layer, dict(rel_bias=d_rel, final_norm_w=g_final, loss=loss[0, :1]))
    per_layer, glob = _unpack_small(_all_reduce_small(packet, "all_reduce_small"))
    g_small = {nm: jnp.stack([per_layer[l][nm] for l in range(DEPTH)]) for nm, _ in SMALL_LAYER}
    cwg = g_small["conv_w"].reshape(DEPTH, CONV_WIDTH, N_DEV, XBC_WIDTH // N_DEV)
    g_small["conv_w"] = lax.dynamic_index_in_dim(cwg, me, axis=2, keepdims=False)
    for nm in sm:
        g_small[nm] = g_small[nm].reshape(sm[nm].shape)
    g_rel = glob["rel_bias"].reshape(rel_bias.shape)
    g_fin = glob["final_norm_w"]
    loss_out = glob["loss"][0]

    def adam(w, g, m, v, name):
        shp = w.shape
        two = lambda a: a.reshape(-1, shp[-1]) if a.ndim > 1 else a.reshape(1, -1)
        d, nm_, nv = _adamw(two(w), two(g), two(m), two(v), name)
        return d.reshape(shp), nm_.reshape(shp), nv.reshape(shp)

    order = ["norm1_w", "w_in", "conv_w", "conv_b", "dt_bias", "a_log", "d_skip", "ssm_norm_w", "w_attn_branch", "w_ssm_branch",
             "w_out", "norm2_w", "w_ffn_in", "w_ffn_out", "rel_bias", "final_norm_w"]
    allw = {**big, **sm, "rel_bias": rel_bias, "final_norm_w": final_norm_w}
    allm = {**big_m, **sm_m, "rel_bias": m_rel_bias, "final_norm_w": m_final_norm_w}
    allv = {**big_v, **sm_v, "rel_bias": v_rel_bias, "final_norm_w": v_final_norm_w}
    allg = {**g_big, **g_small, "rel_bias": g_rel, "final_norm_w": g_fin}
    deltas, new_m, new_v = [], [], []
    for nm in order:
        d, a, b = adam(allw[nm], allg[nm], allm[nm], allv[nm], "adamw_" + nm)
        deltas.append(d)
        new_m.append(a)
        new_v.append(b)
    return (loss_out, dx[None], *[allg[nm] for nm in order], *deltas, *new_m, *new_v)
```

```python
import functools
import math

import numpy as np
import jax
import jax.numpy as jnp
from jax import lax
from jax.experimental import pallas as pl
from jax.experimental.pallas import tpu as pltpu

F32, BF16 = jnp.float32, jnp.bfloat16
S = jax.ShapeDtypeStruct
MESH = pl.DeviceIdType.MESH

D_MODEL = 1024
DEPTH = 2
HEAD_DIM = 64
DILATED_GROUPS = ((128, 1), (512, 4), (2048, 16))
N_DIL = 3
HEADS_PER_GROUP = 8
ATTN_WIDTH = 1536
ATTN_OUT_WIDTH = 512
ATTN_BLOCK = 128
N_REL_BUCKETS = 32
REL_MAX_DISTANCE = 2048
D_INNER = 2048
N_SSM_HEADS = 32
N_SSM_GROUPS = 4
D_STATE = 128
CONV_WIDTH = 4
SSD_CHUNK = 128
XBC_WIDTH = 3072
D_FF = 2816
EPS = 1e-6
ADAM_LR, ADAM_B1, ADAM_B2, ADAM_EPS, ADAM_WD, ADAM_STEP = 0.001, 0.9, 0.999, 1e-08, 0.01, 10

N_DEV = 8
LANES = 128
VMEM_LIMIT = 56 * 1024 * 1024
MM_VMEM_BYTES = 40 * 1024 * 1024
ROW_TILES_BYTES = 36 * 1024 * 1024
NEG = -1e30
BIG = ("w_in", "w_attn_branch", "w_ssm_branch", "w_out", "w_ffn_in", "w_ffn_out")
COL_SHARDED = ("w_in", "w_attn_branch", "w_ffn_in")

NT = (((1,), (1,)), ((), ()))
TN = (((0,), (0,)), ((), ()))


def _cparams(sem=None):
    return pltpu.CompilerParams(dimension_semantics=sem, vmem_limit_bytes=VMEM_LIMIT)


def _pick(n, target, mult=LANES):
    best = None
    for t in range(mult, min(n, target) + 1, mult):
        if n % t == 0:
            best = t
    return best or n


def _silu(x):
    return x * jax.nn.sigmoid(x)


def _dsilu(x):
    s = jax.nn.sigmoid(x)
    return s * (1.0 + x * (1.0 - s))


def _split2(x):
    hi = x.astype(BF16)
    lo = (x - hi.astype(F32)).astype(BF16)
    return hi, lo


def _split3(x):
    x1 = x.astype(BF16)
    r1 = x - x1.astype(F32)
    x2 = r1.astype(BF16)
    x3 = (r1 - x2.astype(F32)).astype(BF16)
    return x1, x2, x3


def _dotx_r(x, m, parts=3):
    xs = _split3(x) if parts == 3 else _split2(x)
    out = jnp.dot(xs[0], m, preferred_element_type=F32)
    for xi in xs[1:]:
        out = out + jnp.dot(xi, m, preferred_element_type=F32)
    return out


def _dotx_l(m, x, parts=3):
    xs = _split3(x) if parts == 3 else _split2(x)
    out = jnp.dot(m, xs[0], preferred_element_type=F32)
    for xi in xs[1:]:
        out = out + jnp.dot(m, xi, preferred_element_type=F32)
    return out


def _mm(a, b, *, ta=False, tb=False, out_dtype=F32, acc=None, name, tm=None, tn=1536, tk=1536, extras=(), epilogue=None, outs=None, comm=None):
    M, K = (a.shape[1], a.shape[0]) if ta else a.shape
    N = b.shape[0] if tb else b.shape[1]
    tn, tk = _pick(N, tn), _pick(K, tk)
    if tm is None:
        def vmem(t, k):
            out_b = jnp.dtype(out_dtype).itemsize
            return (2 * t * k * a.dtype.itemsize + 2 * k * tn * b.dtype.itemsize + 2 * t * tn * out_b
                    + (t * tn * 4 if K > k else 0) + (2 * t * tn * acc.dtype.itemsize if acc is not None else 0))
        tm = _pick(M, 1536)
        while K % (2 * tk) == 0 and vmem(tm, 2 * tk) <= MM_VMEM_BYTES:
            tk *= 2
        while M % (2 * tm) == 0 and vmem(2 * tm, tk) <= MM_VMEM_BYTES:
            tm *= 2
    else:
        tm = _pick(M, tm)
    nk = K // tk
    dims = (((0 if ta else 1,), (1 if tb else 0,)), ((), ()))
    has_acc = acc is not None
    outs = [(N, out_dtype)] if outs is None else outs
    ne, no = len(extras), len(outs)

    def body(*refs):
        a_ref, b_ref = refs[:2]
        c_ref = refs[2] if has_acc else None
        e_refs = refs[2 + has_acc:2 + has_acc + ne]
        o_refs = refs[2 + has_acc + ne:2 + has_acc + ne + no]
        acc_ref = refs[-1]
        k = pl.program_id(2)
        part = lax.dot_general(a_ref[...].astype(BF16), b_ref[...].astype(BF16), dims, preferred_element_type=F32)

        def finish(res):
            tiles = [res] if epilogue is None else epilogue(res, *[e[...] for e in e_refs])
            for o_ref, t in zip(o_refs, tiles):
                o_ref[...] = t.astype(o_ref.dtype)

        if nk == 1:
            finish(part + c_ref[...].astype(F32) if has_acc else part)
        else:
            @pl.when(k == 0)
            def _():
                acc_ref[...] = part + c_ref[...].astype(F32) if has_acc else part

            @pl.when(jnp.logical_and(k > 0, k < nk - 1))
            def _():
                acc_ref[...] += part

            @pl.when(k == nk - 1)
            def _():
                finish(acc_ref[...] + part)

    def cspec(cols):
        return pl.BlockSpec((tm, cols * tn // N), lambda i, j, k: (i, j))

    a_spec = pl.BlockSpec((tk, tm), lambda i, j, k: (k, i)) if ta else pl.BlockSpec((tm, tk), lambda i, j, k: (i, k))
    b_spec = pl.BlockSpec((tn, tk), lambda i, j, k: (j, k)) if tb else pl.BlockSpec((tk, tn), lambda i, j, k: (k, j))
    in_specs, args = [a_spec, b_spec], [a, b]
    if has_acc:
        in_specs.append(cspec(N))
        args.append(acc)
    in_specs += [cspec(e.shape[1]) for e in extras]
    args += list(extras)
    res, comm_res = _carrier_call(
        body, (M // tm, N // tn, nk), in_specs, [cspec(c) for c, _ in outs], [S((M, c), dt) for c, dt in outs],
        [pltpu.VMEM((tm, tn), F32)] if nk > 1 else [], args, comm, name)
    res = res[0] if len(outs) == 1 else res
    return res if comm is None else (res, comm_res)


def _mm_dil(a_list, b_list, d, acc, name, tm=1024):
    T, K = a_list[0].shape
    N = b_list[0].shape[0]
    tm, tn = min(tm, T), _pick(N, 1024)
    na = len(a_list)
    has_acc = acc is not None

    def body(*refs):
        a_refs, b_refs, rest = refs[:na], refs[na:2 * na], refs[2 * na:]
        c_ref = rest[0] if has_acc else None
        o_ref, scr = rest[-2], rest[-1]
        out = c_ref[...] if has_acc else None
        for a_ref, b_ref in zip(a_refs, b_refs):
            a_tok = _dil_to_tok(scr, a_ref, d).astype(BF16) if d > 1 else a_ref[...]
            part = lax.dot_general(a_tok, b_ref[...], NT, preferred_element_type=F32)
            out = part if out is None else out + part
        o_ref[...] = out

    if d > 1:
        a_spec = pl.BlockSpec((d, tm // d, K), lambda i, j: (0, i, 0))
        a_args = [a.reshape(d, T // d, K) for a in a_list]
    else:
        a_spec = pl.BlockSpec((tm, K), lambda i, j: (i, 0))
        a_args = list(a_list)
    o_spec = pl.BlockSpec((tm, tn), lambda i, j: (i, j))
    in_specs = [a_spec] * na + [pl.BlockSpec((tn, K), lambda i, j: (j, 0))] * na + ([o_spec] if has_acc else [])
    return pl.pallas_call(
        body, grid=(T // tm, N // tn), in_specs=in_specs, out_specs=o_spec, out_shape=S((T, N), F32),
        scratch_shapes=[pltpu.VMEM((K // LANES, tm, LANES), F32)],
        compiler_params=_cparams(("parallel", "parallel")), name=name)(*a_args, *b_list, *([acc] if has_acc else []))


class _Comm:
    def __init__(self, ins, outs, sems, start, wait, alias=None):
        self.ins, self.outs, self.sems, self.start, self.wait, self.alias = list(ins), list(outs), list(sems), start, wait, alias or {}


def _carrier_call(body, grid, in_specs, out_specs, out_shape, scratch_shapes, args, comm, name):
    grid = (grid,) if isinstance(grid, int) else tuple(grid)
    seq = ("arbitrary",) * len(grid)
    ni, no, ns = len(in_specs), len(out_specs), len(scratch_shapes)
    if comm is None:
        res = pl.pallas_call(body, grid=grid, in_specs=in_specs, out_specs=out_specs, out_shape=out_shape,
                             scratch_shapes=scratch_shapes, compiler_params=_cparams(seq), name=name)(*args)
        return list(res), []
    ci, co = len(comm.ins), len(comm.outs)

    def wrapped(*refs):
        ins, cins = refs[:ni], refs[ni:ni + ci]
        outs, couts = refs[ni + ci:ni + ci + no], refs[ni + ci + no:ni + ci + no + co]
        scr, csems = refs[ni + ci + no + co:ni + ci + no + co + ns], refs[ni + ci + no + co + ns:]
        ids = [pl.program_id(i) for i in range(len(grid))]
        first = functools.reduce(jnp.logical_and, [i == 0 for i in ids])
        last = functools.reduce(jnp.logical_and, [i == g - 1 for i, g in zip(ids, grid)])

        @pl.when(first)
        def _():
            comm.start(cins, couts, csems)

        body(*ins, *outs, *scr)

        @pl.when(last)
        def _():
            comm.wait(cins, couts, csems)

    anys = pl.BlockSpec(memory_space=pl.ANY)
    res = pl.pallas_call(
        wrapped, grid=grid, in_specs=list(in_specs) + [anys] * ci, out_specs=list(out_specs) + [anys] * co,
        out_shape=list(out_shape) + comm.outs, scratch_shapes=list(scratch_shapes) + comm.sems,
        input_output_aliases={ni + a: no + b for a, b in comm.alias.items()},
        compiler_params=_cparams(seq), name=name)(*args, *comm.ins)
    return list(res[:no]), list(res[no:])


def _run_comm(comm, name):
    ci, co = len(comm.ins), len(comm.outs)

    def body(*refs):
        cins, couts, csems = refs[:ci], refs[ci:ci + co], refs[ci + co:]
        comm.start(cins, couts, csems)
        comm.wait(cins, couts, csems)

    anys = pl.BlockSpec(memory_space=pl.ANY)
    return pl.pallas_call(body, in_specs=[anys] * ci, out_specs=[anys] * co, out_shape=comm.outs, scratch_shapes=comm.sems,
                          input_output_aliases=dict(comm.alias), name=name)(*comm.ins)


def _dev_index(dev):
    return 4 * dev[0] + 2 * dev[1] + dev[2]


def _comm_gather_spread(shards):
    npc = len(shards)

    def copies(x_refs, o_refs, sems):
        x, y, c, chips = _place()
        me = (x, y, c)
        peers = [(x, y, 1 - c)] + [(*chip, c) for chip in chips]
        return [[pltpu.make_async_remote_copy(src_ref=x_refs[i], dst_ref=o_refs[i].at[_dev_index(me)], send_sem=sems[0].at[k, i],
                                              recv_sem=sems[1].at[k, i], device_id=peer, device_id_type=MESH)
                 for k, peer in enumerate(peers)] for i in range(npc)], peers, me

    def local(x_refs, o_refs, sems, me):
        return [pltpu.make_async_copy(x_refs[i], o_refs[i].at[_dev_index(me)], sems[2].at[i]) for i in range(npc)]

    def start(x_refs, o_refs, sems):
        cps, _, me = copies(x_refs, o_refs, sems)
        for cp in local(x_refs, o_refs, sems, me):
            cp.start()
        for row in cps:
            for cp in row:
                cp.start()

    def wait(x_refs, o_refs, sems):
        cps, peers, me = copies(x_refs, o_refs, sems)
        for i in range(npc):
            for k, peer in enumerate(peers):
                pltpu.make_async_remote_copy(src_ref=x_refs[i], dst_ref=o_refs[i].at[_dev_index(peer)], send_sem=sems[0].at[k, i],
                                             recv_sem=sems[1].at[k, i], device_id=peer, device_id_type=MESH).wait_recv()
        for row in cps:
            for cp in row:
                cp.wait_send()
        for cp in local(x_refs, o_refs, sems, me):
            cp.wait()

    return _Comm(shards, [S((N_DEV,) + s.shape, s.dtype) for s in shards],
                 [pltpu.SemaphoreType.DMA((4, npc)), pltpu.SemaphoreType.DMA((4, npc)), pltpu.SemaphoreType.DMA((npc,))], start, wait)


def _comm_gather_pass(gathered):
    npc = len(gathered)

    def copies(o_refs, sems, sent):
        x, y, c, chips = _place()
        return [pltpu.make_async_remote_copy(
            src_ref=o_refs[i].at[_dev_index((*chip, c))], dst_ref=o_refs[i].at[_dev_index((*chip, c if sent else 1 - c))],
            send_sem=sems[0].at[j, i], recv_sem=sems[1].at[j, i], device_id=(x, y, 1 - c), device_id_type=MESH)
            for i in range(npc) for j, chip in enumerate(chips)]

    def start(g_refs, o_refs, sems):
        for cp in copies(o_refs, sems, True):
            cp.start()

    def wait(g_refs, o_refs, sems):
        for cp in copies(o_refs, sems, False):
            cp.wait_recv()
        for cp in copies(o_refs, sems, True):
            cp.wait_send()

    return _Comm(gathered, [S(g.shape, g.dtype) for g in gathered],
                 [pltpu.SemaphoreType.DMA((3, npc)), pltpu.SemaphoreType.DMA((3, npc))], start, wait,
                 alias={i: i for i in range(npc)})


def _comm_to_chips(parts):
    npc = len(parts)

    def copies(p_refs, o_refs, sems):
        x, y, c, chips = _place()
        return [pltpu.make_async_remote_copy(
            src_ref=p_refs[i].at[2 * chip[0] + chip[1]], dst_ref=o_refs[i].at[j], send_sem=sems[0].at[j, i],
            recv_sem=sems[1].at[j, i], device_id=(*chip, c), device_id_type=MESH)
            for i in range(npc) for j, chip in enumerate(chips)]

    def start(p_refs, o_refs, sems):
        for cp in copies(p_refs, o_refs, sems):
            cp.start()

    def wait(p_refs, o_refs, sems):
        for cp in copies(p_refs, o_refs, sems):
            cp.wait()

    return _Comm(parts, [S((3,) + p.shape[1:], p.dtype) for p in parts],
                 [pltpu.SemaphoreType.DMA((3, npc)), pltpu.SemaphoreType.DMA((3, npc))], start, wait)


def _dil_to_tok(scr, ref, d):
    n, C = ref.shape[1], ref.shape[2]
    for r in range(d):
        v = ref[r].astype(F32)
        for cb in range(C // LANES):
            scr.at[cb][pl.ds(r, n, stride=d), :] = v[:, cb * LANES:(cb + 1) * LANES]
    return jnp.concatenate([scr[cb] for cb in range(C // LANES)], axis=1)


def _tok_to_dil(scr, val, ref, d):
    n, C = ref.shape[1], ref.shape[2]
    for cb in range(C // LANES):
        scr[cb] = val[:, cb * LANES:(cb + 1) * LANES].astype(F32)
    for r in range(d):
        ref[r] = jnp.concatenate([scr.at[cb][pl.ds(r, n, stride=d), :] for cb in range(C // LANES)], axis=1).astype(ref.dtype)


def _rowwise(fn, rows, fulls, outs, accs=(), *, tm, name, cap=True, comm=None):
    rows = [r if isinstance(r, tuple) else (r, r.shape[1], 0) for r in rows]
    first = rows[0]
    T = (first[1] if isinstance(first[0], str) else first[0]).shape[0]
    widest = max([r[1].shape[1] if isinstance(r[0], str) else r[1] for r in rows] + [o[0] for o in outs])
    if cap:
        tm = min(tm, max(8, ROW_TILES_BYTES // (2 * (len(rows) + len(outs))) // (4 * widest) // 8 * 8))
    tm = T if T <= tm else _pick(T, tm, 8)
    nr, nf, no, na = len(rows), len(fulls), len(outs), len(accs)
    dil_in = [i for i, r in enumerate(rows) if isinstance(r[0], str) and r[2] > 1]
    dil_out = [i for i, o in enumerate(outs) if len(o) == 3 and o[2] > 1]
    scr_cols = [rows[i][1].shape[1] for i in dil_in] + [outs[i][0] for i in dil_out]

    def body(*refs):
        r, f = refs[:nr], refs[nr:nr + nf]
        o, a = refs[nr + nf:nr + nf + no], refs[nr + nf + no:nr + nf + no + na]
        scr = refs[nr + nf + no + na:]
        tiles = []
        for i, x in enumerate(r):
            if i in dil_in:
                tiles.append(_dil_to_tok(scr[dil_in.index(i)], x, rows[i][2]))
            else:
                tiles.append(x[...].astype(F32))
        ro, ra = fn(*tiles, *[x[...] for x in f])
        for i, (ref, val) in enumerate(zip(o, ro)):
            if i in dil_out:
                _tok_to_dil(scr[len(dil_in) + dil_out.index(i)], val, ref, outs[i][2])
            else:
                ref[...] = val.astype(ref.dtype)
        if na:
            @pl.when(pl.program_id(0) == 0)
            def _():
                for ref in a:
                    ref[...] = jnp.zeros_like(ref)
            for ref, val in zip(a, ra):
                ref[...] += val

    in_specs, args = [], []
    for i, rr in enumerate(rows):
        if isinstance(rr[0], str):
            arr, d = rr[1], rr[2]
            if d > 1:
                in_specs.append(pl.BlockSpec((d, tm // d, arr.shape[1]), lambda i: (0, i, 0)))
                args.append(arr.reshape(d, T // d, arr.shape[1]))
            else:
                in_specs.append(pl.BlockSpec((tm, arr.shape[1]), lambda i: (i, 0)))
                args.append(arr)
        else:
            in_specs.append(pl.BlockSpec((tm, rr[1]), functools.partial(lambda i, cb: (i, cb), cb=rr[2])))
            args.append(rr[0])
    in_specs += [pl.BlockSpec(f.shape, lambda i: (0, 0)) for f in fulls]
    out_specs, out_shape = [], []
    for i, oo in enumerate(outs):
        if i in dil_out:
            d = oo[2]
            out_specs.append(pl.BlockSpec((d, tm // d, oo[0]), lambda i: (0, i, 0)))
            out_shape.append(S((d, T // d, oo[0]), oo[1]))
        else:
            out_specs.append(pl.BlockSpec((tm, oo[0]), lambda i: (i, 0)))
            out_shape.append(S((T, oo[0]), oo[1]))
    out_specs += [pl.BlockSpec(sh, lambda i: (0, 0)) for sh in accs]
    out_shape += [S(sh, F32) for sh in accs]
    res, comm_res = _carrier_call(
        body, T // tm, in_specs, out_specs, out_shape, [pltpu.VMEM((c // LANES, tm, LANES), F32) for c in scr_cols],
        list(args) + list(fulls), comm, name)
    res = [x.reshape(T, x.shape[2]) if i in dil_out else x for i, x in enumerate(res)]
    return res if comm is None else (res, comm_res)


def _rmsnorm_fwd(h, w, name, dils=(), comm=None):
    D = h.shape[1]

    def fn(h, w):
        r = lax.rsqrt(jnp.mean(h * h, axis=-1, keepdims=True) + EPS)
        xn = h * r * w
        return [xn] * (1 + len(dils)), []
    return _rowwise(fn, [h], [w.reshape(1, -1)], [(D, BF16)] + [(D, BF16, d) for d in dils], tm=512, name=name, comm=comm)


def _rmsnorm_bwd(dxn, h, w, dres, name):
    def fn(dxn, h, dres, w):
        r = lax.rsqrt(jnp.mean(h * h, axis=-1, keepdims=True) + EPS)
        n = h * r
        dn = dxn * w
        dh = r * (dn - n * jnp.mean(dn * n, axis=-1, keepdims=True)) + dres
        return [dh], [jnp.sum(dxn * n, axis=0, keepdims=True)]
    D = h.shape[1]
    return _rowwise(fn, [dxn, h, dres], [w.reshape(1, -1)], [(D, F32)], [(1, D)], tm=512, name=name)


def _loss_head(h, w, tgt, name):
    D = h.shape[1]

    def fn(h, tgt, w):
        r = lax.rsqrt(jnp.mean(h * h, axis=-1, keepdims=True) + EPS)
        n = h * r
        e = n * w - tgt
        row_loss = 0.5 * jnp.mean(e * e, axis=-1, keepdims=True)
        dy = e * (1.0 / D)
        dn = dy * w
        dh = r * (dn - n * jnp.mean(dn * n, axis=-1, keepdims=True))
        return [dh], [jnp.sum(dy * n, axis=0, keepdims=True), jnp.broadcast_to(jnp.sum(row_loss, axis=0, keepdims=True), (1, LANES))]
    return _rowwise(fn, [h, tgt], [w.reshape(1, -1)], [(D, F32)], [(1, D), (1, LANES)], tm=256, name=name)


def _combine_fwd(os_, ls, name):
    def fn(o0, o1, o2, l0, l1, l2):
        m = jnp.maximum(jnp.maximum(l0, l1), l2)
        e0, e1, e2 = jnp.exp(l0 - m), jnp.exp(l1 - m), jnp.exp(l2 - m)
        attn = (e0 * o0 + e1 * o1 + e2 * o2) / (e0 + e1 + e2)
        return [attn, attn], []
    dil = [("dil", t, d) for t, (_, d) in zip(list(os_) + list(ls), DILATED_GROUPS * 2)]
    return _rowwise(fn, dil, [], [(ATTN_OUT_WIDTH, BF16), (ATTN_OUT_WIDTH, F32)], tm=512, name=name)


def _combine_bwd(dattn, attn, ls, head_ones, name):
    def fn(dattn, attn, l0, l1, l2, ones):
        m = jnp.maximum(jnp.maximum(l0, l1), l2)
        e0, e1, e2 = jnp.exp(l0 - m), jnp.exp(l1 - m), jnp.exp(l2 - m)
        inv = 1.0 / (e0 + e1 + e2)
        t = _dotx_r(dattn * attn, ones, parts=2)
        outs = []
        for e in (e0, e1, e2):
            al = e * inv
            outs += [al * dattn, al * t]
        return outs, []
    W = ATTN_OUT_WIDTH
    dil = [("dil", t, d) for t, (_, d) in zip(ls, DILATED_GROUPS)]
    outs = [(W, dt, d) for _, d in DILATED_GROUPS for dt in (BF16, F32)]
    return _rowwise(fn, [dattn, attn] + dil, [head_ones], outs, tm=512, name=name)


def _dt_fwd(dt_raw, dt_bias, name):
    def fn(raw, b):
        z = raw[:, :N_SSM_HEADS] + b
        return [jnp.maximum(z, 0.0) + jnp.log(1.0 + jnp.exp(-jnp.abs(z)))], []
    return _rowwise(fn, [dt_raw], [dt_bias.reshape(1, -1)], [(N_SSM_HEADS, F32)], tm=1024, name=name)[0]


def _dt_bwd(ddt_a, ddt_b, dt_raw, dt_bias, name):
    def fn(da, db, raw, b):
        g = (da + db) * jax.nn.sigmoid(raw[:, :N_SSM_HEADS] + b)
        pad = jnp.zeros((g.shape[0], LANES - N_SSM_HEADS), F32)
        return [jnp.concatenate([g, pad], axis=1)], [jnp.sum(g, axis=0, keepdims=True)]
    return _rowwise(fn, [ddt_a, ddt_b, dt_raw], [dt_bias.reshape(1, -1)], [(LANES, BF16)], [(1, N_SSM_HEADS)], tm=1024, name=name)


def _ssm_norm_fwd(y, z, w, name, comm=None):
    G = D_INNER // N_SSM_GROUPS

    def fn(y, z, w):
        yg = y * _silu(z)
        outs = []
        for g in range(N_SSM_GROUPS):
            t = yg[:, g * G:(g + 1) * G]
            outs.append(t * lax.rsqrt(jnp.mean(t * t, axis=-1, keepdims=True) + EPS))
        return [jnp.concatenate(outs, axis=1) * w], []
    res = _rowwise(fn, [y, z], [w.reshape(1, -1)], [(D_INNER, BF16)], tm=256, name=name, comm=comm)
    return res[0] if comm is None else (res[0][0], res[1])


def _ssm_norm_bwd(dssm, y, z, w, name):
    G = D_INNER // N_SSM_GROUPS

    def fn(dssm, y, z, w):
        sz = _silu(z)
        yg = y * sz
        dn = dssm * w
        ns, dygs = [], []
        for g in range(N_SSM_GROUPS):
            t = yg[:, g * G:(g + 1) * G]
            r = lax.rsqrt(jnp.mean(t * t, axis=-1, keepdims=True) + EPS)
            n = t * r
            d = dn[:, g * G:(g + 1) * G]
            dygs.append(r * (d - n * jnp.mean(d * n, axis=-1, keepdims=True)))
            ns.append(n)
        n, dyg = jnp.concatenate(ns, axis=1), jnp.concatenate(dygs, axis=1)
        return [dyg * sz, dyg * y * _dsilu(z)], [jnp.sum(dssm * n, axis=0, keepdims=True)]
    return _rowwise(fn, [dssm, y, z], [w.reshape(1, -1)], [(D_INNER, BF16), (D_INNER, BF16)], [(1, D_INNER)], tm=256, name=name)


def _gate_out_proj(a, sb, gl, h, w_out, name):
    def fn(a, sb, gl, h, w):
        g = jax.nn.sigmoid(gl)
        merged = (g[:, :D_MODEL] * a + g[:, D_MODEL:] * sb).astype(BF16)
        return [h + jnp.dot(merged, w, preferred_element_type=F32), merged], []
    return _rowwise(fn, [a, sb, gl, h], [w_out], [(D_MODEL, F32), (D_MODEL, BF16)], tm=512, cap=False, name=name)


def _d_out_proj_gate(dh, a, sb, gl, w_out, name):
    def fn(dh, a, sb, gl, w):
        dm = lax.dot_general(dh.astype(BF16), w, NT, preferred_element_type=F32)
        g = jax.nn.sigmoid(gl)
        g0, g1 = g[:, :D_MODEL], g[:, D_MODEL:]
        dgl = jnp.concatenate([dm * a * g0 * (1.0 - g0), dm * sb * g1 * (1.0 - g1)], axis=1)
        return [g0 * dm, g1 * dm, dgl], []
    return _rowwise(fn, [dh, a, sb, gl], [w_out], [(D_MODEL, BF16), (D_MODEL, BF16), (2 * D_MODEL, BF16)], tm=512, cap=False, name=name)


FFN_HALF = D_FF // 2


def _ffn_perm(w):
    h = FFN_HALF
    return jnp.concatenate([w[:, 0:h], w[:, D_FF:D_FF + h], w[:, h:D_FF], w[:, D_FF + h:]], axis=1)


def _ffn_unperm(w):
    h = FFN_HALF
    return jnp.concatenate([w[:, 0:h], w[:, 2 * h:3 * h], w[:, h:2 * h], w[:, 3 * h:]], axis=1)


def _swiglu_epilogue(res):
    return [res, _silu(res[:, :FFN_HALF]) * res[:, FFN_HALF:]]


def _dswiglu_epilogue(dact, u):
    u = u.astype(F32)
    gate, up = u[:, :FFN_HALF], u[:, FFN_HALF:]
    return [jnp.concatenate([dact * up * _dsilu(gate), dact * _silu(gate)], axis=1)]


def _adamw(w, g, m, v, name):
    c1 = 1.0 - ADAM_B1 ** ADAM_STEP
    c2 = 1.0 - ADAM_B2 ** ADAM_STEP

    def fn(w, g, m, v):
        m = ADAM_B1 * m + (1.0 - ADAM_B1) * g
        v = ADAM_B2 * v + (1.0 - ADAM_B2) * (g * g)
        delta = -ADAM_LR * ((m / c1) / (jnp.sqrt(v / c2) + ADAM_EPS) + ADAM_WD * w)
        return [delta, m, v], []
    C = w.shape[1]
    return _rowwise(fn, [w, g, m, v], [], [(C, F32)] * 3, tm=256, name=name)


def _bias_consts(dilation, n_steps):
    qi = np.arange(ATTN_BLOCK)[:, None]
    kj = np.arange(2 * ATTN_BLOCK)[None, :]
    steps = qi + ATTN_BLOCK - kj
    valid = (steps >= 0) & (steps <= n_steps)
    dist = jnp.asarray(np.clip(steps, 0, n_steps) * dilation, jnp.int32)
    max_exact = N_REL_BUCKETS // 2
    d_f = jnp.maximum(dist, 1).astype(F32)
    large = max_exact + (jnp.log(d_f / max_exact) / math.log(REL_MAX_DISTANCE / max_exact)
                         * (N_REL_BUCKETS - max_exact)).astype(jnp.int32)
    large = jnp.minimum(large, N_REL_BUCKETS - 1)
    bucket = jnp.where(dist < max_exact, dist, large).reshape(-1)
    onehot = (bucket[None, :] == jnp.arange(N_REL_BUCKETS)[:, None]).astype(F32)
    return onehot, jnp.asarray(valid.reshape(1, -1), F32)


def _bias_gather(rel_g_t, onehot, valid, name):
    def body(r_ref, oh_ref, v_ref, o_ref):
        b = jnp.dot(r_ref[...], oh_ref[...], preferred_element_type=F32, precision=lax.Precision.HIGHEST)
        o_ref[...] = jnp.where(v_ref[...] > 0.5, b, NEG)
    return pl.pallas_call(body, out_shape=S((HEADS_PER_GROUP, onehot.shape[1]), F32), compiler_params=_cparams(), name=name)(rel_g_t, onehot, valid)


def _bias_scatter(dbias, onehot, name):
    def body(d_ref, oh_ref, o_ref):
        o_ref[...] = lax.dot_general(d_ref[...], oh_ref[...], NT, preferred_element_type=F32, precision=lax.Precision.HIGHEST)
    return pl.pallas_call(body, out_shape=S((HEADS_PER_GROUP, N_REL_BUCKETS), F32), compiler_params=_cparams(), name=name)(dbias, onehot)


ATTN_QB_FWD, ATTN_QB_BWD = 4, 4


def _attn_tiles(T, d, qb):
    seg = T // d
    nqb = min(qb, seg // ATTN_BLOCK)
    tq = nqb * ATTN_BLOCK
    return seg, nqb, tq, seg // tq


def _attn_fwd(qkv, bias, d, name, comm=None):
    T = qkv.shape[0]
    seg, nqb, tq, ns = _attn_tiles(T, d, ATTN_QB_FWD)
    W = ATTN_OUT_WIDTH
    scale = HEAD_DIM ** -0.5

    def body(q_ref, kh_ref, kc_ref, vh_ref, vc_ref, b_ref, o_ref, l_ref, s_scr, p_scr):
        n = pl.program_id(1)
        qv = q_ref[...]
        kk = jnp.concatenate([kh_ref[...], kc_ref[...]], axis=0)
        vv = jnp.concatenate([vh_ref[...], vc_ref[...]], axis=0)
        col = lax.broadcasted_iota(jnp.int32, (ATTN_BLOCK, 2 * ATTN_BLOCK), 1)
        kill = jnp.logical_and(n == 0, col < ATTN_BLOCK)
        lo = lax.broadcasted_iota(jnp.int32, (1, LANES), 1) < HEAD_DIM
        zero = jnp.zeros((), BF16)
        for j in range(nqb):
            rows = slice(j * ATTN_BLOCK, (j + 1) * ATTN_BLOCK)
            keys = slice(j * ATTN_BLOCK, (j + 2) * ATTN_BLOCK)
            for hp in range(HEADS_PER_GROUP // 2):
                ps = slice(hp * LANES, (hp + 1) * LANES)
                q2 = (qv[rows, ps].astype(F32) * scale).astype(BF16)
                k2 = kk[keys, ps]
                s_scr[2 * hp] = lax.dot_general(q2, jnp.where(lo, k2, zero), NT, preferred_element_type=F32)
                s_scr[2 * hp + 1] = lax.dot_general(q2, jnp.where(lo, zero, k2), NT, preferred_element_type=F32)
            s = s_scr[...] + b_ref[...]
            if j == 0:
                s = jnp.where(kill[None], NEG, s)
            m = jnp.max(s, axis=-1, keepdims=True)
            p = jnp.exp(s - m)
            den = jnp.sum(p, axis=-1, keepdims=True)
            p_scr[...] = p.astype(BF16)
            inv = 1.0 / den
            lse = m + jnp.log(den)
            for hp in range(HEADS_PER_GROUP // 2):
                ps = slice(hp * LANES, (hp + 1) * LANES)
                v2 = vv[keys, ps]
                o2 = (jnp.dot(p_scr[2 * hp], jnp.where(lo, v2, zero), preferred_element_type=F32)
                      + jnp.dot(p_scr[2 * hp + 1], jnp.where(lo, zero, v2), preferred_element_type=F32))
                o_ref[rows, ps] = (o2 * jnp.where(lo, inv[2 * hp], inv[2 * hp + 1])).astype(o_ref.dtype)
                l_ref[rows, ps] = jnp.where(lo, lse[2 * hp], lse[2 * hp + 1])

    def cur(c):
        return pl.BlockSpec((tq, W), lambda r, n: (r * ns + n, c))

    def halo(c):
        return pl.BlockSpec((ATTN_BLOCK, W), lambda r, n: (jnp.maximum((r * ns + n) * nqb - 1, 0), c))

    res, comm_res = _carrier_call(
        body, (d, ns),
        [cur(0), halo(1), cur(1), halo(2), cur(2), pl.BlockSpec(bias.shape, lambda r, n: (0, 0, 0))],
        [cur(0), cur(0)], [S((T, W), BF16), S((T, W), F32)],
        [pltpu.VMEM((HEADS_PER_GROUP, ATTN_BLOCK, 2 * ATTN_BLOCK), F32), pltpu.VMEM((HEADS_PER_GROUP, ATTN_BLOCK, 2 * ATTN_BLOCK), BF16)],
        [qkv, qkv, qkv, qkv, qkv, bias], comm, name)
    return res if comm is None else (res, comm_res)


def _attn_bwd(qkv, bias, lse, do, dd, d, name):
    T = qkv.shape[0]
    seg, nqb, tq, ns = _attn_tiles(T, d, ATTN_QB_BWD)
    W = ATTN_OUT_WIDTH
    B = ATTN_BLOCK
    scale = HEAD_DIM ** -0.5

    def body(q_ref, kh_ref, kc_ref, vh_ref, vc_ref, b_ref, l_ref, do_ref, dd_ref, dq_ref, dk_ref, dv_ref, db_ref, pk_ref, pv_ref,
             s_scr, dp_scr, p_scr, ds_scr):
        r, n = pl.program_id(0), pl.program_id(1)

        @pl.when(jnp.logical_and(r == 0, n == 0))
        def _():
            db_ref[...] = jnp.zeros_like(db_ref)

        @pl.when(n == 0)
        def _():
            pk_ref[...] = jnp.zeros_like(pk_ref)
            pv_ref[...] = jnp.zeros_like(pv_ref)

        @pl.when(n < ns)
        def _():
            qv = q_ref[...]
            kk = jnp.concatenate([kh_ref[...], kc_ref[...]], axis=0)
            vv = jnp.concatenate([vh_ref[...], vc_ref[...]], axis=0)
            lse_v, do_v, dd_v = l_ref[...], do_ref[...], dd_ref[...]
            col = lax.broadcasted_iota(jnp.int32, (B, 2 * B), 1)
            kill = jnp.logical_and(n == 0, col < B)
            dqs = [[None] * (HEADS_PER_GROUP // 2) for _ in range(nqb)]
            dks = [[None] * (HEADS_PER_GROUP // 2) for _ in range(nqb)]
            dvs = [[None] * (HEADS_PER_GROUP // 2) for _ in range(nqb)]
            H, HP = HEADS_PER_GROUP, HEADS_PER_GROUP // 2
            do_b = do_v.astype(BF16)
            lo = lax.broadcasted_iota(jnp.int32, (1, LANES), 1) < HEAD_DIM
            zero = jnp.zeros((), BF16)
            first = lambda t: jnp.where(lo, t, zero)
            second = lambda t: jnp.where(lo, zero, t)
            for j in range(nqb):
                rows = slice(j * B, (j + 1) * B)
                keys = slice(j * B, (j + 2) * B)
                for hp in range(HP):
                    ps = slice(hp * LANES, (hp + 1) * LANES)
                    q2 = (qv[rows, ps].astype(F32) * scale).astype(BF16)
                    k2, v2, do2 = kk[keys, ps], vv[keys, ps], do_b[rows, ps]
                    s_scr[2 * hp] = lax.dot_general(q2, first(k2), NT, preferred_element_type=F32)
                    s_scr[2 * hp + 1] = lax.dot_general(q2, second(k2), NT, preferred_element_type=F32)
                    dp_scr[2 * hp] = lax.dot_general(do2, first(v2), NT, preferred_element_type=F32)
                    dp_scr[2 * hp + 1] = lax.dot_general(do2, second(v2), NT, preferred_element_type=F32)
                lse_h = jnp.stack([lse_v[rows, h * HEAD_DIM:h * HEAD_DIM + 1] for h in range(H)], axis=0)
                dd_h = jnp.stack([dd_v[rows, h * HEAD_DIM:h * HEAD_DIM + 1] for h in range(H)], axis=0)
                s = s_scr[...] + b_ref[...]
                if j == 0:
                    s = jnp.where(kill[None], NEG, s)
                p = jnp.exp(s - lse_h)
                ds = p * (dp_scr[...] - dd_h)
                db_ref[...] += ds
                p_scr[...] = p.astype(BF16)
                ds_scr[...] = ds.astype(BF16)
                for hp in range(HP):
                    ps = slice(hp * LANES, (hp + 1) * LANES)
                    q2 = (qv[rows, ps].astype(F32) * scale).astype(BF16)
                    k2, do2 = kk[keys, ps], do_b[rows, ps]
                    pa, pb, da, db_ = p_scr[2 * hp], p_scr[2 * hp + 1], ds_scr[2 * hp], ds_scr[2 * hp + 1]
                    dvs[j][hp] = (lax.dot_general(pa, first(do2), TN, preferred_element_type=F32)
                                  + lax.dot_general(pb, second(do2), TN, preferred_element_type=F32))
                    dqs[j][hp] = (jnp.dot(da, first(k2), preferred_element_type=F32)
                                  + jnp.dot(db_, second(k2), preferred_element_type=F32)) * scale
                    dks[j][hp] = (lax.dot_general(da, first(q2), TN, preferred_element_type=F32)
                                  + lax.dot_general(db_, second(q2), TN, preferred_element_type=F32))
            dq_ref[...] = jnp.concatenate([jnp.concatenate(dqs[j], axis=1) for j in range(nqb)], axis=0).astype(dq_ref.dtype)
            for parts, out_ref, pend in ((dks, dk_ref, pk_ref), (dvs, dv_ref, pv_ref)):
                full = [jnp.concatenate(parts[j], axis=1) for j in range(nqb)]
                if tq > B:
                    out_ref[:tq - B] = pend[:tq - B].astype(out_ref.dtype)
                out_ref[tq - B:] = (pend[tq - B:] + full[0][:B]).astype(out_ref.dtype)
                for j in range(nqb - 1):
                    pend[j * B:(j + 1) * B] = full[j][B:] + full[j + 1][:B]
                pend[tq - B:] = full[nqb - 1][B:]

        @pl.when(n == ns)
        def _():
            dk_ref[...] = pk_ref[...].astype(dk_ref.dtype)
            dv_ref[...] = pv_ref[...].astype(dv_ref.dtype)

    def cur(c):
        return pl.BlockSpec((tq, W), lambda r, n: (r * ns + jnp.minimum(n, ns - 1), c))

    def halo(c):
        return pl.BlockSpec((B, W), lambda r, n: (jnp.maximum((r * ns + jnp.minimum(n, ns - 1)) * nqb - 1, 0), c))

    late = pl.BlockSpec((tq, W), lambda r, n: (r * ns + jnp.clip(n - 1, 0, ns - 1), 0))
    bspec = pl.BlockSpec(bias.shape, lambda r, n: (0, 0, 0))
    return pl.pallas_call(
        body, grid=(d, ns + 1),
        in_specs=[cur(0), halo(1), cur(1), halo(2), cur(2), bspec, cur(0), cur(0), cur(0)],
        out_specs=[cur(0), late, late, bspec],
        out_shape=[S((T, W), BF16)] * 3 + [S(bias.shape, F32)],
        scratch_shapes=[pltpu.VMEM((tq, W), F32), pltpu.VMEM((tq, W), F32)]
                       + [pltpu.VMEM((HEADS_PER_GROUP, B, 2 * B), t) for t in (F32, F32, BF16, BF16)],
        compiler_params=_cparams(("arbitrary", "arbitrary")), name=name,
    )(qkv, qkv, qkv, qkv, qkv, bias, lse, do, dd)


CONV_TM, CONV_TC = 512, 1024


def _shift_down(x, halo8, s, row8):
    xr = pltpu.roll(x, s, 0)
    first = jnp.where(row8 < s, pltpu.roll(halo8, s, 0), xr[:8])
    return jnp.concatenate([first, xr[8:]], axis=0)


def _shift_up(x, halo8, s, row8):
    n = x.shape[0]
    xr = pltpu.roll(x, n - s, 0)
    last = jnp.where(row8 >= 8 - s, pltpu.roll(halo8, 8 - s, 0), xr[n - 8:])
    return jnp.concatenate([xr[:n - 8], last], axis=0)


def _conv_fwd(x, w, b, name, comm=None):
    T, C = x.shape
    tm, tc = min(CONV_TM, T), CONV_TC

    def body(x_ref, p_ref, w_ref, b_ref, u_ref, a_ref):
        ti = pl.program_id(1)
        xv = x_ref[...]
        p8 = jnp.where(ti == 0, 0.0, p_ref[...])
        wv = w_ref[...]
        row8 = lax.broadcasted_iota(jnp.int32, (8, tc), 0)
        u = xv * wv[3:4] + b_ref[...]
        for s in (1, 2, 3):
            u = u + _shift_down(xv, p8, s, row8) * wv[3 - s:4 - s]
        u_ref[...] = u
        a_ref[...] = _silu(u)

    cur = pl.BlockSpec((tm, tc), lambda cj, ti: (ti, cj))
    halo = pl.BlockSpec((8, tc), lambda cj, ti: (jnp.maximum(ti * (tm // 8) - 1, 0), cj))
    res, comm_res = _carrier_call(
        body, (C // tc, T // tm),
        [cur, halo, pl.BlockSpec((CONV_WIDTH, tc), lambda cj, ti: (0, cj)), pl.BlockSpec((1, tc), lambda cj, ti: (0, cj))],
        [cur, cur], [S((T, C), F32)] * 2, [], [x, x, w, b], comm, name)
    return res if comm is None else (res, comm_res)


def _conv_bwd(dact, u, x, w, name):
    T, C = x.shape
    tm, tc = min(CONV_TM, T), CONV_TC
    nt = T // tm

    def body(d_ref, dn_ref, u_ref, un_ref, x_ref, w_ref, dx_ref, dw_ref, db_ref):
        ti = pl.program_id(1)

        @pl.when(ti == 0)
        def _():
            dw_ref[...] = jnp.zeros_like(dw_ref)
            db_ref[...] = jnp.zeros_like(db_ref)

        du = d_ref[...] * _dsilu(u_ref[...])
        dun = jnp.where(ti == nt - 1, 0.0, dn_ref[...] * _dsilu(un_ref[...]))
        xv = x_ref[...]
        wv = w_ref[...]
        row8 = lax.broadcasted_iota(jnp.int32, (8, tc), 0)
        dx = du * wv[3:4]
        dws = [None] * CONV_WIDTH
        dws[3] = jnp.sum(du * xv, axis=0, keepdims=True)
        for s in (1, 2, 3):
            up = _shift_up(du, dun, s, row8)
            dx = dx + up * wv[3 - s:4 - s]
            dws[3 - s] = jnp.sum(up * xv, axis=0, keepdims=True)
        dx_ref[...] = dx.astype(dx_ref.dtype)
        dw_ref[...] += jnp.concatenate(dws, axis=0)
        db_ref[...] += jnp.sum(du, axis=0, keepdims=True)

    cur = pl.BlockSpec((tm, tc), lambda cj, ti: (ti, cj))
    nxt = pl.BlockSpec((8, tc), lambda cj, ti: (jnp.minimum((ti + 1) * (tm // 8), T // 8 - 1), cj))
    return pl.pallas_call(
        body, grid=(C // tc, nt),
        in_specs=[cur, nxt, cur, nxt, cur, pl.BlockSpec((CONV_WIDTH, tc), lambda cj, ti: (0, cj))],
        out_specs=[cur, pl.BlockSpec((CONV_WIDTH, tc), lambda cj, ti: (0, cj)), pl.BlockSpec((1, tc), lambda cj, ti: (0, cj))],
        out_shape=[S((T, C), BF16), S((CONV_WIDTH, C), F32), S((1, C), F32)],
        compiler_params=_cparams(("parallel", "arbitrary")), name=name)(dact, dact, u, u, x, w)


def _ssd_consts():
    i = np.arange(SSD_CHUNK)
    tril = (i[None, :] <= i[:, None]).astype(np.float32)
    trils = (i[None, :] < i[:, None]).astype(np.float32)
    head = np.repeat(np.arange(N_SSM_HEADS), D_INNER // N_SSM_HEADS)
    et = (head[None, :] == np.arange(N_SSM_HEADS)[:, None]).astype(np.float32)
    c = lambda a: jnp.asarray(a, BF16)
    return dict(tril=c(tril), triu=c(tril.T), trils=c(trils), et=c(et), e=c(et.T))


def _ssd_common(act_ref, dt_ref, dtT_ref, al_ref, alT_ref, tril_ref, triu_ref, et_ref):
    a_row = -jnp.exp(al_ref[...])
    a_col = -jnp.exp(alT_ref[...])
    dt, dtT = dt_ref[...], dtT_ref[...]
    la = _dotx_l(tril_ref[...], dt * a_row)
    laT = _dotx_r(dtT * a_col, triu_ref[...])
    et = et_ref[...]
    la_e = _dotx_r(la, et)
    dt_e = _dotx_r(dt, et, parts=2)
    x = act_ref[:, :D_INNER]
    xdt = x * dt_e
    la_q = la_e[SSD_CHUNK - 1:SSD_CHUNK, :]
    return a_row, a_col, dt, dtT, la, laT, la_e, dt_e, x, xdt, la_q


def _decay(la, laT, h, causal):
    seg = la[:, h:h + 1] - laT[h:h + 1, :]
    return jnp.exp(jnp.where(causal, seg, NEG))


def _ssd_fwd(act, dt, dtT, alog, dskip_e, cs, name, comm=None):
    T = act.shape[0]
    nc = T // SSD_CHUNK
    Q, G, GW = SSD_CHUNK, N_SSM_GROUPS, D_INNER // N_SSM_GROUPS

    def body(act_ref, dt_ref, dtT_ref, al_ref, alT_ref, dsk_ref, tril_ref, triu_ref, et_ref, y_ref, st_ref, scr):
        @pl.when(pl.program_id(0) == 0)
        def _():
            scr[...] = jnp.zeros_like(scr)
        st_ref[0] = scr[...]
        a_row, a_col, dtv, dtTv, la, laT, la_e, dt_e, x, xdt, la_q = _ssd_common(
            act_ref, dt_ref, dtT_ref, al_ref, alT_ref, tril_ref, triu_ref, et_ref)
        ela = jnp.exp(la_e)
        xdt_b = xdt.astype(BF16)
        xdte_b = (xdt * jnp.exp(la_q - la_e)).astype(BF16)
        ela_q = jnp.exp(la_q)
        causal = lax.broadcasted_iota(jnp.int32, (Q, Q), 0) >= lax.broadcasted_iota(jnp.int32, (Q, Q), 1)
        for g in range(G):
            gs = slice(g * GW, (g + 1) * GW)
            Bg = act_ref[:, D_INNER + g * D_STATE:D_INNER + (g + 1) * D_STATE].astype(BF16)
            Cg = act_ref[:, D_INNER + G * D_STATE + g * D_STATE:D_INNER + G * D_STATE + (g + 1) * D_STATE].astype(BF16)
            cb = lax.dot_general(Cg, Bg, NT, preferred_element_type=F32)
            st = scr[g]
            y_inter = jnp.dot(Cg, st.astype(BF16), preferred_element_type=F32) * ela[:, gs]
            ys = []
            for hh in range(HEADS_PER_GROUP):
                h = g * HEADS_PER_GROUP + hh
                m = (cb * _decay(la, laT, h, causal)).astype(BF16)
                ys.append(jnp.dot(m, xdt_b[:, h * HEAD_DIM:(h + 1) * HEAD_DIM], preferred_element_type=F32))
            y_ref[:, gs] = (jnp.concatenate(ys, axis=1) + y_inter + x[:, gs] * dsk_ref[:, gs]).astype(y_ref.dtype)
            scr[g] = st * ela_q[:, gs] + lax.dot_general(Bg, xdte_b[:, gs], TN, preferred_element_type=F32)

    full = lambda a: pl.BlockSpec(a.shape, lambda c: (0,) * a.ndim)
    al, alT = alog.reshape(1, -1), alog.reshape(-1, 1)
    res, comm_res = _carrier_call(
        body, nc,
        [pl.BlockSpec((Q, XBC_WIDTH), lambda c: (c, 0)), pl.BlockSpec((Q, N_SSM_HEADS), lambda c: (c, 0)),
         pl.BlockSpec((N_SSM_HEADS, Q), lambda c: (0, c)), full(al), full(alT), full(dskip_e),
         full(cs["tril"]), full(cs["triu"]), full(cs["et"])],
        [pl.BlockSpec((Q, D_INNER), lambda c: (c, 0)), pl.BlockSpec((1, G, D_STATE, GW), lambda c: (c, 0, 0, 0))],
        [S((T, D_INNER), BF16), S((nc, G, D_STATE, GW), F32)],
        [pltpu.VMEM((G, D_STATE, GW), F32)],
        [act, dt, dtT, al, alT, dskip_e, cs["tril"], cs["triu"], cs["et"]], comm, name)
    return res if comm is None else (res, comm_res)


def _ssd_bwd(dy, act, dt, dtT, alog, dskip_e, states, cs, name, comm=None):
    T = act.shape[0]
    nc = T // SSD_CHUNK
    Q, G, GW, H = SSD_CHUNK, N_SSM_GROUPS, D_INNER // N_SSM_GROUPS, N_SSM_HEADS

    def body(dy_ref, act_ref, dt_ref, dtT_ref, al_ref, alT_ref, dsk_ref, stp_ref, tril_ref, triu_ref, trils_ref,
             et_ref, e_ref, dact_ref, ddt_ref, ddtT_ref, da_ref, daT_ref, dsk_out_ref, dst, wbuf, ubuf, vbuf, sbuf, dm_scr, m_scr):
        @pl.when(pl.program_id(0) == 0)
        def _():
            dst[...] = jnp.zeros_like(dst)
            da_ref[...] = jnp.zeros_like(da_ref)
            daT_ref[...] = jnp.zeros_like(daT_ref)
            dsk_out_ref[...] = jnp.zeros_like(dsk_out_ref)
        a_row, a_col, dtv, dtTv, la, laT, la_e, dt_e, x, xdt, la_q = _ssd_common(
            act_ref, dt_ref, dtT_ref, al_ref, alT_ref, tril_ref, triu_ref, et_ref)
        dyv = dy_ref[...].astype(F32)
        ela = jnp.exp(la_e)
        e_end = jnp.exp(la_q - la_e)
        ela_q = jnp.exp(la_q)
        dye_b = (dyv * ela).astype(BF16)
        dy_b = dyv.astype(BF16)
        xdt_b = xdt.astype(BF16)
        xdte_b = (xdt * e_end).astype(BF16)
        ri = lax.broadcasted_iota(jnp.int32, (Q, Q), 0)
        ci = lax.broadcasted_iota(jnp.int32, (Q, Q), 1)
        causal = ri >= ci
        rows = []
        for g in range(G):
            gs = slice(g * GW, (g + 1) * GW)
            Bg = act_ref[:, D_INNER + g * D_STATE:D_INNER + (g + 1) * D_STATE].astype(BF16)
            Cg = act_ref[:, D_INNER + G * D_STATE + g * D_STATE:D_INNER + G * D_STATE + (g + 1) * D_STATE].astype(BF16)
            cb = lax.dot_general(Cg, Bg, NT, preferred_element_type=F32)
            stp = stp_ref[0, g]
            stp_b = stp.astype(BF16)
            dstv = dst[g]
            dst_b = dstv.astype(BF16)
            y_inter = jnp.dot(Cg, stp_b, preferred_element_type=F32) * ela[:, gs]
            wbuf[:, gs] = dyv[:, gs] * y_inter
            dxdt_state = jnp.dot(Bg, dst_b, preferred_element_type=F32) * e_end[:, gs]
            ubuf[:, gs] = dxdt_state * xdt[:, gs]
            dC = lax.dot_general(dye_b[:, gs], stp_b, NT, preferred_element_type=F32)
            dB = lax.dot_general(xdte_b[:, gs], dst_b, NT, preferred_element_type=F32)
            sbuf[:, gs] = jnp.broadcast_to(jnp.sum(dstv * stp, axis=0, keepdims=True), (8, GW))
            dst[g] = dstv * ela_q[:, gs] + lax.dot_general(Cg, dye_b[:, gs], TN, preferred_element_type=F32)
            for hh in range(HEADS_PER_GROUP):
                hs = slice((g * HEADS_PER_GROUP + hh) * HEAD_DIM, (g * HEADS_PER_GROUP + hh + 1) * HEAD_DIM)
                dm_scr[hh] = lax.dot_general(dy_b[:, hs], xdt_b[:, hs], NT, preferred_element_type=F32)
            dG = jnp.zeros((Q, Q), F32)
            for hh in range(HEADS_PER_GROUP):
                L = _decay(la, laT, g * HEADS_PER_GROUP + hh, causal)
                M = cb * L
                dM = dm_scr[hh]
                dG = dG + dM * L
                W = dM * M
                rows.append(jnp.sum(W.T, axis=0, keepdims=True) - jnp.sum(W, axis=0, keepdims=True))
                m_scr[hh] = M.astype(BF16)
            dxs = []
            for hh in range(HEADS_PER_GROUP):
                hs = slice((g * HEADS_PER_GROUP + hh) * HEAD_DIM, (g * HEADS_PER_GROUP + hh + 1) * HEAD_DIM)
                dxs.append(lax.dot_general(m_scr[hh], dy_b[:, hs], TN, preferred_element_type=F32))
            dG_b = dG.astype(BF16)
            dC = dC + jnp.dot(dG_b, Bg, preferred_element_type=F32)
            dB = dB + lax.dot_general(dG_b, Cg, TN, preferred_element_type=F32)
            dxdt = jnp.concatenate(dxs, axis=1) + dxdt_state
            vbuf[:, gs] = dxdt * x[:, gs]
            dact_ref[:, gs] = dxdt * dt_e[:, gs] + dyv[:, gs] * dsk_ref[:, gs]
            dact_ref[:, D_INNER + g * D_STATE:D_INNER + (g + 1) * D_STATE] = dB
            dact_ref[:, D_INNER + G * D_STATE + g * D_STATE:D_INNER + G * D_STATE + (g + 1) * D_STATE] = dC
        e = e_ref[...]
        w = _dotx_r(wbuf[...], e, parts=2)
        u = _dotx_r(ubuf[...], e, parts=2)
        vx = _dotx_r(vbuf[...], e, parts=2)
        dsk = _dotx_r(jnp.broadcast_to(jnp.sum(dyv * x, axis=0, keepdims=True), (8, D_INNER)), e, parts=2)[0:1]
        s0 =_dotx_r(sbuf[...], e, parts=2)[0:1] * jnp.exp(la[Q - 1:Q, :])
        ddelta = _dotx_l(triu_ref[...], w) + _dotx_l(trils_ref[...], u) + s0
        ddt_ref[...] = ddelta * a_row + vx
        ddeltaT = _dotx_r(jnp.concatenate(rows, axis=0), tril_ref[...])
        ddtT_ref[...] = ddeltaT * a_col
        da_ref[...] += jnp.sum(ddelta * dtv, axis=0, keepdims=True)
        daT_ref[...] += jnp.sum(ddeltaT * dtTv, axis=1, keepdims=True)
        dsk_out_ref[...] += dsk

    rev = lambda c: nc - 1 - c
    full = lambda a: pl.BlockSpec(a.shape, lambda c: (0,) * a.ndim)
    al, alT = alog.reshape(1, -1), alog.reshape(-1, 1)
    consts = [cs[k] for k in ("tril", "triu", "trils", "et", "e")]
    res, comm_res = _carrier_call(
        body, nc,
        [pl.BlockSpec((Q, D_INNER), lambda c: (rev(c), 0)), pl.BlockSpec((Q, XBC_WIDTH), lambda c: (rev(c), 0)),
         pl.BlockSpec((Q, H), lambda c: (rev(c), 0)), pl.BlockSpec((H, Q), lambda c: (0, rev(c))),
         full(al), full(alT), full(dskip_e), pl.BlockSpec((1, G, D_STATE, GW), lambda c: (rev(c), 0, 0, 0))]
        + [full(a) for a in consts],
        [pl.BlockSpec((Q, XBC_WIDTH), lambda c: (rev(c), 0)), pl.BlockSpec((Q, H), lambda c: (rev(c), 0)),
         pl.BlockSpec((H, Q), lambda c: (0, rev(c))), pl.BlockSpec((1, H), lambda c: (0, 0)),
         pl.BlockSpec((H, 1), lambda c: (0, 0)), pl.BlockSpec((1, H), lambda c: (0, 0))],
        [S((T, XBC_WIDTH), F32), S((T, H), F32), S((H, T), F32), S((1, H), F32), S((H, 1), F32), S((1, H), F32)],
        [pltpu.VMEM((G, D_STATE, GW), F32), pltpu.VMEM((Q, D_INNER), F32), pltpu.VMEM((Q, D_INNER), F32),
         pltpu.VMEM((Q, D_INNER), F32), pltpu.VMEM((8, D_INNER), F32),
         pltpu.VMEM((HEADS_PER_GROUP, Q, Q), F32), pltpu.VMEM((HEADS_PER_GROUP, Q, Q), BF16)],
        [dy, act, dt, dtT, al, alT, dskip_e, states] + consts, comm, name)
    return res if comm is None else (res, comm_res)


def _a_log_grad(da, daT_row, alog, name):
    def body(a_ref, b_ref, al_ref, o_ref):
        o_ref[...] = (a_ref[...] + b_ref[...]) * (-jnp.exp(al_ref[...]))
    return pl.pallas_call(body, out_shape=S((1, N_SSM_HEADS), F32), name=name)(da, daT_row, alog.reshape(1, -1))


def _place():
    x, y, c = lax.axis_index("x"), lax.axis_index("y"), lax.axis_index("c")
    return x, y, c, [(1 - x, y), (x, 1 - y), (1 - x, 1 - y)]


def _to_sibling(to_sib, name):
    npc = len(to_sib)

    def body(*refs):
        s_refs, o_refs, send_sems, recv_sems = refs[:npc], refs[npc:2 * npc], refs[2 * npc], refs[2 * npc + 1]
        x, y, c, _ = _place()
        cps = [pltpu.make_async_remote_copy(
            src_ref=s_refs[i], dst_ref=o_refs[i], send_sem=send_sems.at[i], recv_sem=recv_sems.at[i],
            device_id=(x, y, 1 - c), device_id_type=MESH) for i in range(npc)]
        for cp in cps:
            cp.start()
        for cp in cps:
            cp.wait()

    anys = pl.BlockSpec(memory_space=pl.ANY)
    return pl.pallas_call(
        body, in_specs=[anys] * npc, out_specs=[anys] * npc, out_shape=[S(s.shape, s.dtype) for s in to_sib],
        scratch_shapes=[pltpu.SemaphoreType.DMA((npc,)), pltpu.SemaphoreType.DMA((npc,))],
        name=name)(*to_sib)


def _rs_begin(pieces, name):
    c = lax.axis_index("c")
    by_core = [p.reshape(4, 2, p.shape[1], p.shape[2]) for p in pieces]
    to_sib = [lax.dynamic_index_in_dim(p, 1 - c, axis=1, keepdims=False).astype(BF16) for p in by_core]
    keep = [lax.dynamic_index_in_dim(p, c, axis=1, keepdims=False) for p in by_core]
    from_sib = _to_sibling(to_sib, name + "_d2d")

    def add1(a, b):
        s = a + b
        return [s, s], []

    parts, parts_b = [], []
    for i, (k, f) in enumerate(zip(keep, from_sib)):
        _, r, C = k.shape
        p, pb = _rowwise(add1, [k.reshape(4 * r, C), f.reshape(4 * r, C)], [], [(C, F32), (C, BF16)], tm=2048, name=f"{name}_add1_{i}")
        parts.append(p.reshape(4, r, C))
        parts_b.append(pb.reshape(4, r, C))
    return parts, parts_b


def _rs_finish(parts, got, name):
    x, y = lax.axis_index("x"), lax.axis_index("y")

    def add2(a, b, c_, d_):
        return [((a + b) + c_) + d_], []

    outs = []
    for i, (p, g) in enumerate(zip(parts, got)):
        own = lax.dynamic_index_in_dim(p, 2 * x + y, axis=0, keepdims=False)
        outs.append(_rowwise(add2, [own, g[0], g[1], g[2]], [], [(p.shape[2], F32)], tm=2048, name=f"{name}_add2_{i}")[0])
    return outs


class _GradReduce:
    EARLY = ("w_ffn_in", "w_ffn_out", "w_out", "w_attn_branch", "w_ssm_branch")

    def __init__(self):
        self.out, self.keys, self.parts, self.parts_b = {}, [], [], []

    def _begin(self, l, names, gr, name):
        parts, parts_b = _rs_begin([_shard(nm, gr[nm]) for nm in names], name)
        self.keys += [(l, nm) for nm in names]
        self.parts += parts
        self.parts_b += parts_b

    def carry_fn(self, l):
        if l != 0:
            return None

        def fn(gr):
            self._begin(0, self.EARLY, gr, "rs_early_l0")
            return _comm_to_chips(self.parts_b)
        return fn

    def carry2_fn(self, l):
        if l != 0:
            return None

        def fn(gr):
            self.parts2, parts2_b = _rs_begin([_shard("w_in", gr["w_in"])], "rs_w_in_l0")
            return _comm_to_chips(parts2_b)
        return fn

    def done(self, l, gr, carried):
        if l == DEPTH - 1:
            self._begin(l, BIG, gr, f"rs_l{l}")
            return
        for key, o in zip(self.keys, _rs_finish(self.parts, carried[0], "rs_carried")):
            self.out[key] = o
        self.out[(0, "w_in")] = _rs_finish(self.parts2, carried[1], "rs_w_in_l0")[0]


def _all_reduce_small(v, name):
    R, C = v.shape

    def body(x_ref, out_ref, buf, send_sems, recv_sems):
        x, y, c, chips = _place()
        me, sib = (x, y, c), (x, y, 1 - c)

        def rows(dev):
            return buf.at[4 * dev[0] + 2 * dev[1] + dev[2]]

        def copy(k, block, to, src=None):
            return pltpu.make_async_remote_copy(
                src_ref=rows(block) if src is None else src, dst_ref=rows(block),
                send_sem=send_sems.at[k], recv_sem=recv_sems.at[k], device_id=to, device_id_type=MESH)

        buf[4 * x + 2 * y + c] = x_ref[...]
        first = [copy(0, me, sib, src=x_ref)] + [copy(1 + j, me, (*chip, c), src=x_ref) for j, chip in enumerate(chips)]
        for cp in first:
            cp.start()
        passed = [copy(4 + j, (*chip, c), sib) for j, chip in enumerate(chips)]
        for j, chip in enumerate(chips):
            copy(1 + j, (*chip, c), me).wait_recv()
            passed[j].start()
        copy(0, sib, me).wait_recv()
        for j, chip in enumerate(chips):
            copy(4 + j, (*chip, 1 - c), me).wait_recv()
        for cp in first + passed:
            cp.wait_send()
        acc = buf[0]
        for j in range(1, N_DEV):
            acc = acc + buf[j]
        out_ref[...] = acc

    vm = pl.BlockSpec(memory_space=pltpu.VMEM)
    return pl.pallas_call(
        body, in_specs=[vm], out_specs=vm, out_shape=S((R, C), F32),
        scratch_shapes=[pltpu.VMEM((N_DEV, R, C), F32), pltpu.SemaphoreType.DMA((7,)), pltpu.SemaphoreType.DMA((7,))],
        compiler_params=pltpu.CompilerParams(vmem_limit_bytes=VMEM_LIMIT), name=name)(v)


SEG = (("q", 0, 1536), ("k", 1536, 1536), ("v", 3072, 1536), ("z", 4608, 2048), ("xbc", 6656, 3072), ("dt", 9728, 32), ("gl", 9760, 2048))


def _split_w_in(w_in_full):
    out = {}
    for nm, off, n in SEG:
        w = w_in_full[:, off:off + n]
        if nm == "dt":
            w = jnp.pad(w, ((0, 0), (0, LANES - n)))
        out[nm] = w
    W = ATTN_OUT_WIDTH
    out["qkv"] = [jnp.concatenate([out[s][:, g * W:(g + 1) * W] for s in ("q", "k", "v")], axis=1) for g in range(N_DIL)]
    out["qkv_t"] = [[out[s][:, g * W:(g + 1) * W] for s in ("q", "k", "v")] for g in range(N_DIL)]
    return out


def _layer_fwd(h, p, W, biases, cs, l, carry=None, late=None, early=None):
    T = h.shape[0]
    nm = lambda s: f"{s}_l{l}"
    sv = {"h_in": h}
    dils = [d for _, d in DILATED_GROUPS[1:]]
    if early is None:
        xns = _rmsnorm_fwd(h, p["norm1_w"], nm("norm1"), dils=dils)
    else:
        xns, spread = _rmsnorm_fwd(h, p["norm1_w"], nm("norm1"), dils=dils, comm=early[0])
        W["w_in"] = _split_w_in(early[1](_run_comm(_comm_gather_pass(spread), nm("pass_w_in")))["w_in"])
    xn = xns[0]
    wi = W["w_in"]
    z = _mm(xn, wi["z"], out_dtype=BF16, name=nm("proj_z"))
    xbc = _mm(xn, wi["xbc"], name=nm("proj_xbc"))
    dt_raw = _mm(xn, wi["dt"], name=nm("proj_dt"))
    gl = _mm(xn, wi["gl"], out_dtype=BF16, name=nm("proj_gl"))
    conv_args = (xbc, p["conv_w"], p["conv_b"].reshape(1, -1), nm("conv"))
    if late is None:
        u_conv, act = _conv_fwd(*conv_args)
    else:
        (u_conv, act), spread = _conv_fwd(*conv_args, comm=late[0])
    os_, ls, qkvs = [], [], []
    for g, (window, dil) in enumerate(DILATED_GROUPS):
        qkv = _mm(xns[g], wi["qkv"][g], out_dtype=BF16, name=nm(f"proj_qkv_g{g}"))
        if late is not None and g == 0:
            (o, lse), arrived = _attn_fwd(qkv, biases[g], dil, nm(f"attn_fwd_g{g}"), comm=_comm_gather_pass(spread))
            W.update(late[1](arrived))
        else:
            o, lse = _attn_fwd(qkv, biases[g], dil, nm(f"attn_fwd_g{g}"))
        os_.append(o)
        ls.append(lse)
        qkvs.append(qkv)
    attn_b, attn_f = _combine_fwd(os_, ls, nm("combine"))
    dt = _dt_fwd(dt_raw, p["dt_bias"], nm("dt"))
    dtT = dt.T
    dskip_e = jnp.repeat(p["d_skip"], D_INNER // N_SSM_HEADS).reshape(1, -1)
    carried = None
    if carry is None:
        y, states = _ssd_fwd(act, dt, dtT, p["a_log"], dskip_e, cs, nm("ssd_fwd"))
        ssm = _ssm_norm_fwd(y, z, p["ssm_norm_w"], nm("ssm_norm"))
    else:
        (y, states), spread = _ssd_fwd(act, dt, dtT, p["a_log"], dskip_e, cs, nm("ssd_fwd"), comm=carry)
        ssm, carried = _ssm_norm_fwd(y, z, p["ssm_norm_w"], nm("ssm_norm"), comm=_comm_gather_pass(spread))
    a_br = _mm(attn_b, W["w_attn_branch"], out_dtype=BF16, name=nm("attn_branch"))
    s_br = _mm(ssm, W["w_ssm_branch"], out_dtype=BF16, name=nm("ssm_branch"))
    h_mid, merged = _gate_out_proj(a_br, s_br, gl, h, W["w_out"], nm("gate_out_proj"))
    xn2 = _rmsnorm_fwd(h_mid, p["norm2_w"], nm("norm2"))[0]
    W["w_ffn_in_p"] = _ffn_perm(W["w_ffn_in"])
    u_ffn, ffn_act = _mm(xn2, W["w_ffn_in_p"], tm=512, tn=D_FF, epilogue=_swiglu_epilogue, outs=[(2 * D_FF, BF16), (D_FF, BF16)],
                         name=nm("ffn_in_swiglu"))
    h_out = _mm(ffn_act, W["w_ffn_out"], acc=h_mid, name=nm("ffn_out"))
    sv.update(xn=xn, xns=xns, qkvs=qkvs, z=z, xbc=xbc, dt_raw=dt_raw, gl=gl, ls=ls, attn_b=attn_b, attn_f=attn_f, u_conv=u_conv,
              act=act, dt=dt, dtT=dtT, dskip_e=dskip_e, y=y, states=states, ssm=ssm, a_br=a_br, s_br=s_br, merged=merged,
              h_mid=h_mid, xn2=xn2, u_ffn=u_ffn, ffn_act=ffn_act)
    return h_out, sv, carried


def _layer_bwd(dh, sv, p, W, biases, cs, head_ones, l, carry_fn=None, carry2_fn=None):
    T = dh.shape[0]
    nm = lambda s: f"{s}_l{l}"
    gr = {}
    du = _mm(dh, W["w_ffn_out"], tb=True, tm=512, tn=FFN_HALF, extras=[sv["u_ffn"]], epilogue=_dswiglu_epilogue, outs=[(2 * D_FF, BF16)],
             name=nm("d_ffn_act_swiglu"))
    gr["w_ffn_out"] = _mm(sv["ffn_act"], dh, ta=True, name=nm("g_ffn_out"))
    dxn2 = _mm(du, W["w_ffn_in_p"], tb=True, out_dtype=BF16, name=nm("d_xn2"))
    gr["w_ffn_in"] = _ffn_unperm(_mm(sv["xn2"], du, ta=True, name=nm("g_ffn_in")))
    dh_mid, gr["norm2_w"] = _rmsnorm_bwd(dxn2, sv["h_mid"], p["norm2_w"], dh, nm("d_norm2"))
    gr["w_out"] = _mm(sv["merged"], dh_mid, ta=True, name=nm("g_out"))
    d_a, d_s, dgl = _d_out_proj_gate(dh_mid, sv["a_br"], sv["s_br"], sv["gl"], W["w_out"], nm("d_out_proj_gate"))
    dattn = _mm(d_a, W["w_attn_branch"], tb=True, out_dtype=BF16, name=nm("d_attn"))
    gr["w_attn_branch"] = _mm(sv["attn_b"], d_a, ta=True, name=nm("g_attn_branch"))
    dssm = _mm(d_s, W["w_ssm_branch"], tb=True, out_dtype=BF16, name=nm("d_ssm"))
    gr["w_ssm_branch"] = _mm(sv["ssm"], d_s, ta=True, name=nm("g_ssm_branch"))
    dy, dz, gr["ssm_norm_w"] = _ssm_norm_bwd(dssm, sv["y"], sv["z"], p["ssm_norm_w"], nm("d_ssm_norm"))
    ssd_args = (dy, sv["act"], sv["dt"], sv["dtT"], p["a_log"], sv["dskip_e"], sv["states"], cs, nm("ssd_bwd"))
    carried = None
    if carry_fn is None:
        dact_c, ddt_a, ddt_bT, da, daT, dskip = _ssd_bwd(*ssd_args)
    else:
        (dact_c, ddt_a, ddt_bT, da, daT, dskip), carried = _ssd_bwd(*ssd_args, comm=carry_fn(gr))
    gr["a_log"] = _a_log_grad(da, daT.T, p["a_log"], nm("g_a_log")).reshape(-1)
    gr["d_skip"] = dskip.reshape(-1)
    ddt_raw, ddt_bias = _dt_bwd(ddt_a, ddt_bT.T, sv["dt_raw"], p["dt_bias"], nm("d_dt"))
    gr["dt_bias"] = ddt_bias.reshape(-1)
    dxbc, gr["conv_w"], dconv_b = _conv_bwd(dact_c, sv["u_conv"], sv["xbc"], p["conv_w"], nm("d_conv"))
    gr["conv_b"] = dconv_b.reshape(-1)
    outs = _combine_bwd(dattn, sv["attn_f"], sv["ls"], head_ones, nm("d_combine"))
    wi = W["w_in"]
    dbias, dxn = [], None
    gqkv = [[None] * N_DIL for _ in range(3)]
    for g, (window, dil) in enumerate(DILATED_GROUPS):
        dq, dk, dv, db = _attn_bwd(sv["qkvs"][g], biases[g], sv["ls"][g], outs[2 * g], outs[2 * g + 1], dil, nm(f"attn_bwd_g{g}"))
        dbias.append(db)
        dxn = _mm_dil([dq, dk, dv], wi["qkv_t"][g], dil, dxn, nm(f"d_xn_qkv_g{g}"))
        for i, dseg in enumerate((dq, dk, dv)):
            gqkv[i][g] = _mm(sv["xns"][g], dseg, ta=True, name=nm(f"g_in_{'qkv'[i]}_g{g}"))
    parts = (("z", dz), ("xbc", dxbc), ("dt", ddt_raw), ("gl", dgl))
    gws = gqkv[0] + gqkv[1] + gqkv[2]
    for sname, dseg in parts:
        gw = _mm(sv["xn"], dseg, ta=True, name=nm("g_in_" + sname))
        gws.append(gw[:, :N_SSM_HEADS] if sname == "dt" else gw)
    gr["w_in"] = jnp.concatenate(gws, axis=1)
    carried2 = None
    for sname, dseg in parts:
        if carry2_fn is not None and sname == "xbc":
            dxn, carried2 = _mm(dseg, wi[sname], tb=True, acc=dxn, name=nm("d_xn_" + sname), comm=carry2_fn(gr))
        else:
            dxn = _mm(dseg, wi[sname], tb=True, acc=dxn, name=nm("d_xn_" + sname))
    dh_in, gr["norm1_w"] = _rmsnorm_bwd(dxn, sv["h_in"], p["norm1_w"], dh_mid, nm("d_norm1"))
    return dh_in, gr, dbias, (carried, carried2)


def _step_local(x, tgt, small, Wfull, rel_bias, final_norm_w, prefetch=None, grad_reduce=None, late0=None, early0=None):
    cs = _ssd_consts()
    head = np.repeat(np.arange(HEADS_PER_GROUP), HEAD_DIM)
    head_ones = jnp.asarray(head[:, None] == head[None, :], BF16)
    biases, onehots = [], []
    for g, (window, dil) in enumerate(DILATED_GROUPS):
        onehot, valid = _bias_consts(dil, window // dil)
        rel_g_t = rel_bias[:, g * HEADS_PER_GROUP:(g + 1) * HEADS_PER_GROUP].T
        b = _bias_gather(rel_g_t, onehot, valid, f"bias_gather_g{g}")
        biases.append(b.reshape(HEADS_PER_GROUP, ATTN_BLOCK, 2 * ATTN_BLOCK))
        onehots.append(onehot)
    h, saved, carried = x, [], None
    Wfull = list(Wfull)
    for l in range(DEPTH):
        W = dict(prefetch[1](carried) if Wfull[l] is None else Wfull[l])
        if "w_in" in W:
            W["w_in"] = _split_w_in(W["w_in"])
        Wfull[l] = W
        first = prefetch is not None and l == 0
        h, sv, carried = _layer_fwd(h, small[l], W, biases, cs, l, prefetch[0] if first else None,
                                    late0 if l == 0 else None, early0 if l == 0 else None)
        saved.append(sv)
    dh, g_final, loss = _loss_head(h, final_norm_w, tgt, "loss_head")
    grads = [None] * DEPTH
    dbias_tot = [None] * N_DIL
    for l in reversed(range(DEPTH)):
        carry_fn = grad_reduce.carry_fn(l) if grad_reduce is not None else None
        carry2_fn = grad_reduce.carry2_fn(l) if grad_reduce is not None else None
        dh, grads[l], dbias, carried = _layer_bwd(dh, saved[l], small[l], Wfull[l], biases, cs, head_ones, l, carry_fn, carry2_fn)
        if grad_reduce is not None:
            grad_reduce.done(l, grads[l], carried)
        for g in range(N_DIL):
            dbias_tot[g] = dbias[g] if dbias_tot[g] is None else dbias_tot[g] + dbias[g]
    d_rel = jnp.concatenate(
        [_bias_scatter(dbias_tot[g].reshape(HEADS_PER_GROUP, -1), onehots[g], f"bias_scatter_g{g}").T for g in range(N_DIL)], axis=1)
    return loss, dh, grads, d_rel, g_final


def _unshard(nm, g):
    _, rows, cols = g.shape
    if nm in COL_SHARDED:
        return g.transpose(1, 0, 2).reshape(rows, N_DEV * cols)
    return g.reshape(N_DEV * rows, cols)


def _shard(nm, w):
    rows, cols = w.shape
    if nm in COL_SHARDED:
        return w.reshape(rows, N_DEV, cols // N_DEV).transpose(1, 0, 2)
    return w.reshape(N_DEV, rows // N_DEV, cols)


SMALL_LAYER = (("norm1_w", 1024), ("conv_w", 12288), ("conv_b", 3072), ("dt_bias", 32), ("a_log", 32), ("d_skip", 32),
               ("ssm_norm_w", 2048), ("norm2_w", 1024))
SMALL_GLOBAL = (("rel_bias", 768), ("final_norm_w", 1024), ("loss", 1))


def _pad128(v):
    n = v.shape[0]
    return jnp.pad(v, (0, -n % LANES))


def _pack_small(per_layer, glob):
    parts = [_pad128(per_layer[l][nm].reshape(-1)) for l in range(DEPTH) for nm, _ in SMALL_LAYER]
    parts += [_pad128(glob[nm].reshape(-1)) for nm, _ in SMALL_GLOBAL]
    flat = jnp.concatenate(parts)
    flat = jnp.pad(flat, (0, -flat.shape[0] % (8 * LANES)))
    return flat.reshape(-1, LANES)


def _unpack_small(packed):
    flat = packed.reshape(-1)
    per_layer, glob, off = [dict() for _ in range(DEPTH)], {}, 0
    for l in range(DEPTH):
        for nm, n in SMALL_LAYER:
            per_layer[l][nm] = flat[off:off + n]
            off += n + (-n % LANES)
    for nm, n in SMALL_GLOBAL:
        glob[nm] = flat[off:off + n]
        off += n + (-n % LANES)
    return per_layer, glob


def kernel(x, norm1_w, w_in, conv_w, conv_b, dt_bias, a_log, d_skip, ssm_norm_w, w_attn_branch, w_ssm_branch, w_out, norm2_w, w_ffn_in, w_ffn_out, rel_bias, final_norm_w, loss_target, m_norm1_w, m_w_in, m_conv_w, m_conv_b, m_dt_bias, m_a_log, m_d_skip, m_ssm_norm_w, m_w_attn_branch, m_w_ssm_branch, m_w_out, m_norm2_w, m_w_ffn_in, m_w_ffn_out, m_rel_bias, m_final_norm_w, v_norm1_w, v_w_in, v_conv_w, v_conv_b, v_dt_bias, v_a_log, v_d_skip, v_ssm_norm_w, v_w_attn_branch, v_w_ssm_branch, v_w_out, v_norm2_w, v_w_ffn_in, v_w_ffn_out, v_rel_bias, v_final_norm_w):
    big = dict(w_in=w_in, w_attn_branch=w_attn_branch, w_ssm_branch=w_ssm_branch, w_out=w_out, w_ffn_in=w_ffn_in, w_ffn_out=w_ffn_out)
    big_m = dict(w_in=m_w_in, w_attn_branch=m_w_attn_branch, w_ssm_branch=m_w_ssm_branch, w_out=m_w_out, w_ffn_in=m_w_ffn_in, w_ffn_out=m_w_ffn_out)
    big_v = dict(w_in=v_w_in, w_attn_branch=v_w_attn_branch, w_ssm_branch=v_w_ssm_branch, w_out=v_w_out, w_ffn_in=v_w_ffn_in, w_ffn_out=v_w_ffn_out)
    sm = dict(norm1_w=norm1_w, conv_w=conv_w, conv_b=conv_b, dt_bias=dt_bias, a_log=a_log, d_skip=d_skip, ssm_norm_w=ssm_norm_w, norm2_w=norm2_w)
    sm_m = dict(norm1_w=m_norm1_w, conv_w=m_conv_w, conv_b=m_conv_b, dt_bias=m_dt_bias, a_log=m_a_log, d_skip=m_d_skip, ssm_norm_w=m_ssm_norm_w, norm2_w=m_norm2_w)
    sm_v = dict(norm1_w=v_norm1_w, conv_w=v_conv_w, conv_b=v_conv_b, dt_bias=v_dt_bias, a_log=v_a_log, d_skip=v_d_skip, ssm_norm_w=v_ssm_norm_w, norm2_w=v_norm2_w)
    me = 4 * lax.axis_index("x") + 2 * lax.axis_index("y") + lax.axis_index("c")

    def full_weights(gathered, names=BIG):
        return {nm: _unshard(nm, g) for nm, g in zip(names, gathered)}

    later = [nm for nm in BIG if nm != "w_in"]
    Wfull = [{}, None]
    early0 = (_comm_gather_spread([big["w_in"][0].astype(BF16)]), functools.partial(full_weights, names=["w_in"]))
    late0 = (_comm_gather_spread([big[nm][0].astype(BF16) for nm in later]), functools.partial(full_weights, names=later))
    prefetch = (_comm_gather_spread([big[nm][DEPTH - 1].astype(BF16) for nm in BIG]), full_weights)

    conv_full = []
    for l in range(DEPTH):
        z = jnp.zeros((N_DEV, CONV_WIDTH, XBC_WIDTH // N_DEV), F32)
        conv_full.append(lax.dynamic_update_index_in_dim(z, conv_w[l], me, axis=0))
    cw = jnp.stack(conv_full).reshape(-1, LANES)
    cw = _all_reduce_small(cw, "gather_conv_w").reshape(DEPTH, N_DEV, CONV_WIDTH, XBC_WIDTH // N_DEV)
    cw = cw.transpose(0, 2, 1, 3).reshape(DEPTH, CONV_WIDTH, XBC_WIDTH)

    small = [{nm: (cw[l] if nm == "conv_w" else a[l]) for nm, a in sm.items()} for l in range(DEPTH)]
    grad_reduce = _GradReduce()
    loss, dx, grads, d_rel, g_final = _step_local(x[0], loss_target[0], small, Wfull, rel_bias, final_norm_w, prefetch, grad_reduce, late0, early0)
    g_big = {nm: jnp.stack([grad_reduce.out[(l, nm)] for l in range(DEPTH)]) for nm in BIG}

    per_layer = [{nm: grads[l][nm] for nm, _ in SMALL_LAYER} for l in range(DEPTH)]
    packet = _pack_small(per_layer, dict(rel_bias=d_rel, final_norm_w=g_final, loss=loss[0, :1]))
    per_layer, glob = _unpack_small(_all_reduce_small(packet, "all_reduce_small"))
    g_small = {nm: jnp.stack([per_layer[l][nm] for l in range(DEPTH)]) for nm, _ in SMALL_LAYER}
    cwg = g_small["conv_w"].reshape(DEPTH, CONV_WIDTH, N_DEV, XBC_WIDTH // N_DEV)
    g_small["conv_w"] = lax.dynamic_index_in_dim(cwg, me, axis=2, keepdims=False)
    for nm in sm:
        g_small[nm] = g_small[nm].reshape(sm[nm].shape)
    g_rel = glob["rel_bias"].reshape(rel_bias.shape)
    g_fin = glob["final_norm_w"]
    loss_out = glob["loss"][0]

    def adam(w, g, m, v, name):
        shp = w.shape
        two = lambda a: a.reshape(-1, shp[-1]) if a.ndim > 1 else a.reshape(1, -1)
        d, nm_, nv = _adamw(two(w), two(g), two(m), two(v), name)
        return d.reshape(shp), nm_.reshape(shp), nv.reshape(shp)

    order = ["norm1_w", "w_in", "conv_w", "conv_b", "dt_bias", "a_log", "d_skip", "ssm_norm_w", "w_attn_branch", "w_ssm_branch",
             "w_out", "norm2_w", "w_ffn_in", "w_ffn_out", "rel_bias", "final_norm_w"]
    allw = {**big, **sm, "rel_bias": rel_bias, "final_norm_w": final_norm_w}
    allm = {**big_m, **sm_m, "rel_bias": m_rel_bias, "final_norm_w": m_final_norm_w}
    allv = {**big_v, **sm_v, "rel_bias": v_rel_bias, "final_norm_w": v_final_norm_w}
    allg = {**g_big, **g_small, "rel_bias": g_rel, "final_norm_w": g_fin}
    deltas, new_m, new_v = [], [], []
    for nm in order:
        d, a, b = adam(allw[nm], allg[nm], allm[nm], allv[nm], "adamw_" + nm)
        deltas.append(d)
        new_m.append(a)
        new_v.append(b)
    return (loss_out, dx[None], *[allg[nm] for nm in order], *deltas, *new_m, *new_v)
```

```python
import functools
import math

import numpy as np
import jax
import jax.numpy as jnp
from jax import lax
from jax.experimental import pallas as pl
from jax.experimental.pallas import tpu as pltpu

F32, BF16 = jnp.float32, jnp.bfloat16
S = jax.ShapeDtypeStruct
MESH = pl.DeviceIdType.MESH

D_MODEL = 1024
DEPTH = 2
HEAD_DIM = 64
DILATED_GROUPS = ((128, 1), (512, 4), (2048, 16))
N_DIL = 3
HEADS_PER_GROUP = 8
ATTN_WIDTH = 1536
ATTN_OUT_WIDTH = 512
ATTN_BLOCK = 128
N_REL_BUCKETS = 32
REL_MAX_DISTANCE = 2048
D_INNER = 2048
N_SSM_HEADS = 32
N_SSM_GROUPS = 4
D_STATE = 128
CONV_WIDTH = 4
SSD_CHUNK = 128
XBC_WIDTH = 3072
D_FF = 2816
EPS = 1e-6
ADAM_LR, ADAM_B1, ADAM_B2, ADAM_EPS, ADAM_WD, ADAM_STEP = 0.001, 0.9, 0.999, 1e-08, 0.01, 10

N_DEV = 8
LANES = 128
VMEM_LIMIT = 56 * 1024 * 1024
MM_VMEM_BYTES = 40 * 1024 * 1024
ROW_TILES_BYTES = 36 * 1024 * 1024
NEG = -1e30
BIG = ("w_in", "w_attn_branch", "w_ssm_branch", "w_out", "w_ffn_in", "w_ffn_out")
COL_SHARDED = ("w_in", "w_attn_branch", "w_ffn_in")

NT = (((1,), (1,)), ((), ()))
TN = (((0,), (0,)), ((), ()))


def _cparams(sem=None):
    return pltpu.CompilerParams(dimension_semantics=sem, vmem_limit_bytes=VMEM_LIMIT)


def _pick(n, target, mult=LANES):
    best = None
    for t in range(mult, min(n, target) + 1, mult):
        if n % t == 0:
            best = t
    return best or n


def _silu(x):
    return x * jax.nn.sigmoid(x)


def _dsilu(x):
    s = jax.nn.sigmoid(x)
    return s * (1.0 + x * (1.0 - s))


def _split2(x):
    hi = x.astype(BF16)
    lo = (x - hi.astype(F32)).astype(BF16)
    return hi, lo


def _split3(x):
    x1 = x.astype(BF16)
    r1 = x - x1.astype(F32)
    x2 = r1.astype(BF16)
    x3 = (r1 - x2.astype(F32)).astype(BF16)
    return x1, x2, x3


def _dotx_r(x, m, parts=3):
    xs = _split3(x) if parts == 3 else _split2(x)
    out = jnp.dot(xs[0], m, preferred_element_type=F32)
    for xi in xs[1:]:
        out = out + jnp.dot(xi, m, preferred_element_type=F32)
    return out


def _dotx_l(m, x, parts=3):
    xs = _split3(x) if parts == 3 else _split2(x)
    out = jnp.dot(m, xs[0], preferred_element_type=F32)
    for xi in xs[1:]:
        out = out + jnp.dot(m, xi, preferred_element_type=F32)
    return out


def _mm(a, b, *, ta=False, tb=False, out_dtype=F32, acc=None, name, tm=None, tn=1536, tk=1536, extras=(), epilogue=None, outs=None, comm=None):
    M, K = (a.shape[1], a.shape[0]) if ta else a.shape
    N = b.shape[0] if tb else b.shape[1]
    tn, tk = _pick(N, tn), _pick(K, tk)
    if tm is None:
        def vmem(t, k):
            out_b = jnp.dtype(out_dtype).itemsize
            return (2 * t * k * a.dtype.itemsize + 2 * k * tn * b.dtype.itemsize + 2 * t * tn * out_b
                    + (t * tn * 4 if K > k else 0) + (2 * t * tn * acc.dtype.itemsize if acc is not None else 0))
        tm = _pick(M, 1536)
        while K % (2 * tk) == 0 and vmem(tm, 2 * tk) <= MM_VMEM_BYTES:
            tk *= 2
        while M % (2 * tm) == 0 and vmem(2 * tm, tk) <= MM_VMEM_BYTES:
            tm *= 2
    else:
        tm = _pick(M, tm)
    nk = K // tk
    dims = (((0 if ta else 1,), (1 if tb else 0,)), ((), ()))
    has_acc = acc is not None
    outs = [(N, out_dtype)] if outs is None else outs
    ne, no = len(extras), len(outs)

    def body(*refs):
        a_ref, b_ref = refs[:2]
        c_ref = refs[2] if has_acc else None
        e_refs = refs[2 + has_acc:2 + has_acc + ne]
        o_refs = refs[2 + has_acc + ne:2 + has_acc + ne + no]
        acc_ref = refs[-1]
        k = pl.program_id(2)
        part = lax.dot_general(a_ref[...].astype(BF16), b_ref[...].astype(BF16), dims, preferred_element_type=F32)

        def finish(res):
            tiles = [res] if epilogue is None else epilogue(res, *[e[...] for e in e_refs])
            for o_ref, t in zip(o_refs, tiles):
                o_ref[...] = t.astype(o_ref.dtype)

        if nk == 1:
            finish(part + c_ref[...].astype(F32) if has_acc else part)
        else:
            @pl.when(k == 0)
            def _():
                acc_ref[...] = part + c_ref[...].astype(F32) if has_acc else part

            @pl.when(jnp.logical_and(k > 0, k < nk - 1))
            def _():
                acc_ref[...] += part

            @pl.when(k == nk - 1)
            def _():
                finish(acc_ref[...] + part)

    def cspec(cols):
        return pl.BlockSpec((tm, cols * tn // N), lambda i, j, k: (i, j))

    a_spec = pl.BlockSpec((tk, tm), lambda i, j, k: (k, i)) if ta else pl.BlockSpec((tm, tk), lambda i, j, k: (i, k))
    b_spec = pl.BlockSpec((tn, tk), lambda i, j, k: (j, k)) if tb else pl.BlockSpec((tk, tn), lambda i, j, k: (k, j))
    in_specs, args = [a_spec, b_spec], [a, b]
    if has_acc:
        in_specs.append(cspec(N))
        args.append(acc)
    in_specs += [cspec(e.shape[1]) for e in extras]
    args += list(extras)
    res, comm_res = _carrier_call(
        body, (M // tm, N // tn, nk), in_specs, [cspec(c) for c, _ in outs], [S((M, c), dt) for c, dt in outs],
        [pltpu.VMEM((tm, tn), F32)] if nk > 1 else [], args, comm, name)
    res = res[0] if len(outs) == 1 else res
    return res if comm is None else (res, comm_res)


def _mm_dil(a_list, b_list, d, acc, name, tm=1024):
    T, K = a_list[0].shape
    N = b_list[0].shape[0]
    tm, tn = min(tm, T), _pick(N, 1024)
    na = len(a_list)
    has_acc = acc is not None

    def body(*refs):
        a_refs, b_refs, rest = refs[:na], refs[na:2 * na], refs[2 * na:]
        c_ref = rest[0] if has_acc else None
        o_ref, scr = rest[-2], rest[-1]
        out = c_ref[...] if has_acc else None
        for a_ref, b_ref in zip(a_refs, b_refs):
            a_tok = _dil_to_tok(scr, a_ref, d).astype(BF16) if d > 1 else a_ref[...]
            part = lax.dot_general(a_tok, b_ref[...], NT, preferred_element_type=F32)
            out = part if out is None else out + part
        o_ref[...] = out

    if d > 1:
        a_spec = pl.BlockSpec((d, tm // d, K), lambda i, j: (0, i, 0))
        a_args = [a.reshape(d, T // d, K) for a in a_list]
    else:
        a_spec = pl.BlockSpec((tm, K), lambda i, j: (i, 0))
        a_args = list(a_list)
    o_spec = pl.BlockSpec((tm, tn), lambda i, j: (i, j))
    in_specs = [a_spec] * na + [pl.BlockSpec((tn, K), lambda i, j: (j, 0))] * na + ([o_spec] if has_acc else [])
    return pl.pallas_call(
        body, grid=(T // tm, N // tn), in_specs=in_specs, out_specs=o_spec, out_shape=S((T, N), F32),
        scratch_shapes=[pltpu.VMEM((K // LANES, tm, LANES), F32)],
        compiler_params=_cparams(("parallel", "parallel")), name=name)(*a_args, *b_list, *([acc] if has_acc else []))


class _Comm:
    def __init__(self, ins, outs, sems, start, wait, alias=None):
        self.ins, self.outs, self.sems, self.start, self.wait, self.alias = list(ins), list(outs), list(sems), start, wait, alias or {}


def _carrier_call(body, grid, in_specs, out_specs, out_shape, scratch_shapes, args, comm, name):
    grid = (grid,) if isinstance(grid, int) else tuple(grid)
    seq = ("arbitrary",) * len(grid)
    ni, no, ns = len(in_specs), len(out_specs), len(scratch_shapes)
    if comm is None:
        res = pl.pallas_call(body, grid=grid, in_specs=in_specs, out_specs=out_specs, out_shape=out_shape,
                             scratch_shapes=scratch_shapes, compiler_params=_cparams(seq), name=name)(*args)
        return list(res), []
    ci, co = len(comm.ins), len(comm.outs)

    def wrapped(*refs):
        ins, cins = refs[:ni], refs[ni:ni + ci]
        outs, couts = refs[ni + ci:ni + ci + no], refs[ni + ci + no:ni + ci + no + co]
        scr, csems = refs[ni + ci + no + co:ni + ci + no + co + ns], refs[ni + ci + no + co + ns:]
        ids = [pl.program_id(i) for i in range(len(grid))]
        first = functools.reduce(jnp.logical_and, [i == 0 for i in ids])
        last = functools.reduce(jnp.logical_and, [i == g - 1 for i, g in zip(ids, grid)])

        @pl.when(first)
        def _():
            comm.start(cins, couts, csems)

        body(*ins, *outs, *scr)

        @pl.when(last)
        def _():
            comm.wait(cins, couts, csems)

    anys = pl.BlockSpec(memory_space=pl.ANY)
    res = pl.pallas_call(
        wrapped, grid=grid, in_specs=list(in_specs) + [anys] * ci, out_specs=list(out_specs) + [anys] * co,
        out_shape=list(out_shape) + comm.outs, scratch_shapes=list(scratch_shapes) + comm.sems,
        input_output_aliases={ni + a: no + b for a, b in comm.alias.items()},
        compiler_params=_cparams(seq), name=name)(*args, *comm.ins)
    return list(res[:no]), list(res[no:])


def _run_comm(comm, name):
    ci, co = len(comm.ins), len(comm.outs)

    def body(*refs):
        cins, couts, csems = refs[:ci], refs[ci:ci + co], refs[ci + co:]
        comm.start(cins, couts, csems)
        comm.wait(cins, couts, csems)

    anys = pl.BlockSpec(memory_space=pl.ANY)
    return pl.pallas_call(body, in_specs=[anys] * ci, out_specs=[anys] * co, out_shape=comm.outs, scratch_shapes=comm.sems,
                          input_output_aliases=dict(comm.alias), name=name)(*comm.ins)


def _dev_index(dev):
    return 4 * dev[0] + 2 * dev[1] + dev[2]


def _comm_gather_spread(shards):
    npc = len(shards)

    def copies(x_refs, o_refs, sems):
        x, y, c, chips = _place()
        me = (x, y, c)
        peers = [(x, y, 1 - c)] + [(*chip, c) for chip in chips]
        return [[pltpu.make_async_remote_copy(src_ref=x_refs[i], dst_ref=o_refs[i].at[_dev_index(me)], send_sem=sems[0].at[k, i],
                                              recv_sem=sems[1].at[k, i], device_id=peer, device_id_type=MESH)
                 for k, peer in enumerate(peers)] for i in range(npc)], peers, me

    def local(x_refs, o_refs, sems, me):
        return [pltpu.make_async_copy(x_refs[i], o_refs[i].at[_dev_index(me)], sems[2].at[i]) for i in range(npc)]

    def start(x_refs, o_refs, sems):
        cps, _, me = copies(x_refs, o_refs, sems)
        for cp in local(x_refs, o_refs, sems, me):
            cp.start()
        for row in cps:
            for cp in row:
                cp.start()

    def wait(x_refs, o_refs, sems):
        cps, peers, me = copies(x_refs, o_refs, sems)
        for i in range(npc):
            for k, peer in enumerate(peers):
                pltpu.make_async_remote_copy(src_ref=x_refs[i], dst_ref=o_refs[i].at[_dev_index(peer)], send_sem=sems[0].at[k, i],
                                             recv_sem=sems[1].at[k, i], device_id=peer, device_id_type=MESH).wait_recv()
        for row in cps:
            for cp in row:
                cp.wait_send()
        for cp in local(x_refs, o_refs, sems, me):
            cp.wait()

    return _Comm(shards, [S((N_DEV,) + s.shape, s.dtype) for s in shards],
                 [pltpu.SemaphoreType.DMA((4, npc)), pltpu.SemaphoreType.DMA((4, npc)), pltpu.SemaphoreType.DMA((npc,))], start, wait)


def _comm_gather_pass(gathered):
    npc = len(gathered)

    def copies(o_refs, sems, sent):
        x, y, c, chips = _place()
        return [pltpu.make_async_remote_copy(
            src_ref=o_refs[i].at[_dev_index((*chip, c))], dst_ref=o_refs[i].at[_dev_index((*chip, c if sent else 1 - c))],
            send_sem=sems[0].at[j, i], recv_sem=sems[1].at[j, i], device_id=(x, y, 1 - c), device_id_type=MESH)
            for i in range(npc) for j, chip in enumerate(chips)]

    def start(g_refs, o_refs, sems):
        for cp in copies(o_refs, sems, True):
            cp.start()

    def wait(g_refs, o_refs, sems):
        for cp in copies(o_refs, sems, False):
            cp.wait_recv()
        for cp in copies(o_refs, sems, True):
            cp.wait_send()

    return _Comm(gathered, [S(g.shape, g.dtype) for g in gathered],
                 [pltpu.SemaphoreType.DMA((3, npc)), pltpu.SemaphoreType.DMA((3, npc))], start, wait,
                 alias={i: i for i in range(npc)})


def _comm_to_chips(parts):
    npc = len(parts)

    def copies(p_refs, o_refs, sems):
        x, y, c, chips = _place()
        return [pltpu.make_async_remote_copy(
            src_ref=p_refs[i].at[2 * chip[0] + chip[1]], dst_ref=o_refs[i].at[j], send_sem=sems[0].at[j, i],
            recv_sem=sems[1].at[j, i], device_id=(*chip, c), device_id_type=MESH)
            for i in range(npc) for j, chip in enumerate(chips)]

    def start(p_refs, o_refs, sems):
        for cp in copies(p_refs, o_refs, sems):
            cp.start()

    def wait(p_refs, o_refs, sems):
        for cp in copies(p_refs, o_refs, sems):
            cp.wait()

    return _Comm(parts, [S((3,) + p.shape[1:], p.dtype) for p in parts],
                 [pltpu.SemaphoreType.DMA((3, npc)), pltpu.SemaphoreType.DMA((3, npc))], start, wait)


def _dil_to_tok(scr, ref, d):
    n, C = ref.shape[1], ref.shape[2]
    for r in range(d):
        v = ref[r].astype(F32)
        for cb in range(C // LANES):
            scr.at[cb][pl.ds(r, n, stride=d), :] = v[:, cb * LANES:(cb + 1) * LANES]
    return jnp.concatenate([scr[cb] for cb in range(C // LANES)], axis=1)


def _tok_to_dil(scr, val, ref, d):
    n, C = ref.shape[1], ref.shape[2]
    for cb in range(C // LANES):
        scr[cb] = val[:, cb * LANES:(cb + 1) * LANES].astype(F32)
    for r in range(d):
        ref[r] = jnp.concatenate([scr.at[cb][pl.ds(r, n, stride=d), :] for cb in range(C // LANES)], axis=1).astype(ref.dtype)


def _rowwise(fn, rows, fulls, outs, accs=(), *, tm, name, cap=True, comm=None):
    rows = [r if isinstance(r, tuple) else (r, r.shape[1], 0) for r in rows]
    first = rows[0]
    T = (first[1] if isinstance(first[0], str) else first[0]).shape[0]
    widest = max([r[1].shape[1] if isinstance(r[0], str) else r[1] for r in rows] + [o[0] for o in outs])
    if cap:
        tm = min(tm, max(8, ROW_TILES_BYTES // (2 * (len(rows) + len(outs))) // (4 * widest) // 8 * 8))
    tm = T if T <= tm else _pick(T, tm, 8)
    nr, nf, no, na = len(rows), len(fulls), len(outs), len(accs)
    dil_in = [i for i, r in enumerate(rows) if isinstance(r[0], str) and r[2] > 1]
    dil_out = [i for i, o in enumerate(outs) if len(o) == 3 and o[2] > 1]
    scr_cols = [rows[i][1].shape[1] for i in dil_in] + [outs[i][0] for i in dil_out]

    def body(*refs):
        r, f = refs[:nr], refs[nr:nr + nf]
        o, a = refs[nr + nf:nr + nf + no], refs[nr + nf + no:nr + nf + no + na]
        scr = refs[nr + nf + no + na:]
        tiles = []
        for i, x in enumerate(r):
            if i in dil_in:
                tiles.append(_dil_to_tok(scr[dil_in.index(i)], x, rows[i][2]))
            else:
                tiles.append(x[...].astype(F32))
        ro, ra = fn(*tiles, *[x[...] for x in f])
        for i, (ref, val) in enumerate(zip(o, ro)):
            if i in dil_out:
                _tok_to_dil(scr[len(dil_in) + dil_out.index(i)], val, ref, outs[i][2])
            else:
                ref[...] = val.astype(ref.dtype)
        if na:
            @pl.when(pl.program_id(0) == 0)
            def _():
                for ref in a:
                    ref[...] = jnp.zeros_like(ref)
            for ref, val in zip(a, ra):
                ref[...] += val

    in_specs, args = [], []
    for i, rr in enumerate(rows):
        if isinstance(rr[0], str):
            arr, d = rr[1], rr[2]
            if d > 1:
                in_specs.append(pl.BlockSpec((d, tm // d, arr.shape[1]), lambda i: (0, i, 0)))
                args.append(arr.reshape(d, T // d, arr.shape[1]))
            else:
                in_specs.append(pl.BlockSpec((tm, arr.shape[1]), lambda i: (i, 0)))
                args.append(arr)
        else:
            in_specs.append(pl.BlockSpec((tm, rr[1]), functools.partial(lambda i, cb: (i, cb), cb=rr[2])))
            args.append(rr[0])
    in_specs += [pl.BlockSpec(f.shape, lambda i: (0, 0)) for f in fulls]
    out_specs, out_shape = [], []
    for i, oo in enumerate(outs):
        if i in dil_out:
            d = oo[2]
            out_specs.append(pl.BlockSpec((d, tm // d, oo[0]), lambda i: (0, i, 0)))
            out_shape.append(S((d, T // d, oo[0]), oo[1]))
        else:
            out_specs.append(pl.BlockSpec((tm, oo[0]), lambda i: (i, 0)))
            out_shape.append(S((T, oo[0]), oo[1]))
    out_specs += [pl.BlockSpec(sh, lambda i: (0, 0)) for sh in accs]
    out_shape += [S(sh, F32) for sh in accs]
    res, comm_res = _carrier_call(
        body, T // tm, in_specs, out_specs, out_shape, [pltpu.VMEM((c // LANES, tm, LANES), F32) for c in scr_cols],
        list(args) + list(fulls), comm, name)
    res = [x.reshape(T, x.shape[2]) if i in dil_out else x for i, x in enumerate(res)]
    return res if comm is None else (res, comm_res)


def _rmsnorm_fwd(h, w, name, dils=(), comm=None):
    D = h.shape[1]

    def fn(h, w):
        r = lax.rsqrt(jnp.mean(h * h, axis=-1, keepdims=True) + EPS)
        xn = h * r * w
        return [xn] * (1 + len(dils)), []
    return _rowwise(fn, [h], [w.reshape(1, -1)], [(D, BF16)] + [(D, BF16, d) for d in dils], tm=512, name=name, comm=comm)


def _rmsnorm_bwd(dxn, h, w, dres, name):
    def fn(dxn, h, dres, w):
        r = lax.rsqrt(jnp.mean(h * h, axis=-1, keepdims=True) + EPS)
        n = h * r
        dn = dxn * w
        dh = r * (dn - n * jnp.mean(dn * n, axis=-1, keepdims=True)) + dres
        return [dh], [jnp.sum(dxn * n, axis=0, keepdims=True)]
    D = h.shape[1]
    return _rowwise(fn, [dxn, h, dres], [w.reshape(1, -1)], [(D, F32)], [(1, D)], tm=512, name=name)


def _loss_head(h, w, tgt, name):
    D = h.shape[1]

    def fn(h, tgt, w):
        r = lax.rsqrt(jnp.mean(h * h, axis=-1, keepdims=True) + EPS)
        n = h * r
        e = n * w - tgt
        row_loss = 0.5 * jnp.mean(e * e, axis=-1, keepdims=True)
        dy = e * (1.0 / D)
        dn = dy * w
        dh = r * (dn - n * jnp.mean(dn * n, axis=-1, keepdims=True))
        return [dh], [jnp.sum(dy * n, axis=0, keepdims=True), jnp.broadcast_to(jnp.sum(row_loss, axis=0, keepdims=True), (1, LANES))]
    return _rowwise(fn, [h, tgt], [w.reshape(1, -1)], [(D, F32)], [(1, D), (1, LANES)], tm=256, name=name)


def _combine_fwd(os_, ls, name):
    def fn(o0, o1, o2, l0, l1, l2):
        m = jnp.maximum(jnp.maximum(l0, l1), l2)
        e0, e1, e2 = jnp.exp(l0 - m), jnp.exp(l1 - m), jnp.exp(l2 - m)
        attn = (e0 * o0 + e1 * o1 + e2 * o2) / (e0 + e1 + e2)
        return [attn, attn], []
    dil = [("dil", t, d) for t, (_, d) in zip(list(os_) + list(ls), DILATED_GROUPS * 2)]
    return _rowwise(fn, dil, [], [(ATTN_OUT_WIDTH, BF16), (ATTN_OUT_WIDTH, F32)], tm=512, name=name)


def _combine_bwd(dattn, attn, ls, head_ones, name):
    def fn(dattn, attn, l0, l1, l2, ones):
        m = jnp.maximum(jnp.maximum(l0, l1), l2)
        e0, e1, e2 = jnp.exp(l0 - m), jnp.exp(l1 - m), jnp.exp(l2 - m)
        inv = 1.0 / (e0 + e1 + e2)
        t = _dotx_r(dattn * attn, ones, parts=2)
        outs = []
        for e in (e0, e1, e2):
            al = e * inv
            outs += [al * dattn, al * t]
        return outs, []
    W = ATTN_OUT_WIDTH
    dil = [("dil", t, d) for t, (_, d) in zip(ls, DILATED_GROUPS)]
    outs = [(W, dt, d) for _, d in DILATED_GROUPS for dt in (BF16, F32)]
    return _rowwise(fn, [dattn, attn] + dil, [head_ones], outs, tm=512, name=name)


def _dt_fwd(dt_raw, dt_bias, name):
    def fn(raw, b):
        z = raw[:, :N_SSM_HEADS] + b
        return [jnp.maximum(z, 0.0) + jnp.log(1.0 + jnp.exp(-jnp.abs(z)))], []
    return _rowwise(fn, [dt_raw], [dt_bias.reshape(1, -1)], [(N_SSM_HEADS, F32)], tm=1024, name=name)[0]


def _dt_bwd(ddt_a, ddt_b, dt_raw, dt_bias, name):
    def fn(da, db, raw, b):
        g = (da + db) * jax.nn.sigmoid(raw[:, :N_SSM_HEADS] + b)
        pad = jnp.zeros((g.shape[0], LANES - N_SSM_HEADS), F32)
        return [jnp.concatenate([g, pad], axis=1)], [jnp.sum(g, axis=0, keepdims=True)]
    return _rowwise(fn, [ddt_a, ddt_b, dt_raw], [dt_bias.reshape(1, -1)], [(LANES, BF16)], [(1, N_SSM_HEADS)], tm=1024, name=name)


def _ssm_norm_fwd(y, z, w, name, comm=None):
    G = D_INNER // N_SSM_GROUPS

    def fn(y, z, w):
        yg = y * _silu(z)
        outs = []
        for g in range(N_SSM_GROUPS):
            t = yg[:, g * G:(g + 1) * G]
            outs.append(t * lax.rsqrt(jnp.mean(t * t, axis=-1, keepdims=True) + EPS))
        return [jnp.concatenate(outs, axis=1) * w], []
    res = _rowwise(fn, [y, z], [w.reshape(1, -1)], [(D_INNER, BF16)], tm=256, name=name, comm=comm)
    return res[0] if comm is None else (res[0][0], res[1])


def _ssm_norm_bwd(dssm, y, z, w, name):
    G = D_INNER // N_SSM_GROUPS

    def fn(dssm, y, z, w):
        sz = _silu(z)
        yg = y * sz
        dn = dssm * w
        ns, dygs = [], []
        for g in range(N_SSM_GROUPS):
            t = yg[:, g * G:(g + 1) * G]
            r = lax.rsqrt(jnp.mean(t * t, axis=-1, keepdims=True) + EPS)
            n = t * r
            d = dn[:, g * G:(g + 1) * G]
            dygs.append(r * (d - n * jnp.mean(d * n, axis=-1, keepdims=True)))
            ns.append(n)
        n, dyg = jnp.concatenate(ns, axis=1), jnp.concatenate(dygs, axis=1)
        return [dyg * sz, dyg * y * _dsilu(z)], [jnp.sum(dssm * n, axis=0, keepdims=True)]
    return _rowwise(fn, [dssm, y, z], [w.reshape(1, -1)], [(D_INNER, BF16), (D_INNER, BF16)], [(1, D_INNER)], tm=256, name=name)


def _gate_out_proj(a, sb, gl, h, w_out, name):
    def fn(a, sb, gl, h, w):
        g = jax.nn.sigmoid(gl)
        merged = (g[:, :D_MODEL] * a + g[:, D_MODEL:] * sb).astype(BF16)
        return [h + jnp.dot(merged, w, preferred_element_type=F32), merged], []
    return _rowwise(fn, [a, sb, gl, h], [w_out], [(D_MODEL, F32), (D_MODEL, BF16)], tm=512, cap=False, name=name)


def _d_out_proj_gate(dh, a, sb, gl, w_out, name):
    def fn(dh, a, sb, gl, w):
        dm = lax.dot_general(dh.astype(BF16), w, NT, preferred_element_type=F32)
        g = jax.nn.sigmoid(gl)
        g0, g1 = g[:, :D_MODEL], g[:, D_MODEL:]
        dgl = jnp.concatenate([dm * a * g0 * (1.0 - g0), dm * sb * g1 * (1.0 - g1)], axis=1)
        return [g0 * dm, g1 * dm, dgl], []
    return _rowwise(fn, [dh, a, sb, gl], [w_out], [(D_MODEL, BF16), (D_MODEL, BF16), (2 * D_MODEL, BF16)], tm=512, cap=False, name=name)


FFN_HALF = D_FF // 2


def _ffn_perm(w):
    h = FFN_HALF
    return jnp.concatenate([w[:, 0:h], w[:, D_FF:D_FF + h], w[:, h:D_FF], w[:, D_FF + h:]], axis=1)


def _ffn_unperm(w):
    h = FFN_HALF
    return jnp.concatenate([w[:, 0:h], w[:, 2 * h:3 * h], w[:, h:2 * h], w[:, 3 * h:]], axis=1)


def _swiglu_epilogue(res):
    return [res, _silu(res[:, :FFN_HALF]) * res[:, FFN_HALF:]]


def _dswiglu_epilogue(dact, u):
    u = u.astype(F32)
    gate, up = u[:, :FFN_HALF], u[:, FFN_HALF:]
    return [jnp.concatenate([dact * up * _dsilu(gate), dact * _silu(gate)], axis=1)]


def _adamw(w, g, m, v, name):
    c1 = 1.0 - ADAM_B1 ** ADAM_STEP
    c2 = 1.0 - ADAM_B2 ** ADAM_STEP

    def fn(w, g, m, v):
        m = ADAM_B1 * m + (1.0 - ADAM_B1) * g
        v = ADAM_B2 * v + (1.0 - ADAM_B2) * (g * g)
        delta = -ADAM_LR * ((m / c1) / (jnp.sqrt(v / c2) + ADAM_EPS) + ADAM_WD * w)
        return [delta, m, v], []
    C = w.shape[1]
    return _rowwise(fn, [w, g, m, v], [], [(C, F32)] * 3, tm=256, name=name)


def _bias_consts(dilation, n_steps):
    qi = np.arange(ATTN_BLOCK)[:, None]
    kj = np.arange(2 * ATTN_BLOCK)[None, :]
    steps = qi + ATTN_BLOCK - kj
    valid = (steps >= 0) & (steps <= n_steps)
    dist = jnp.asarray(np.clip(steps, 0, n_steps) * dilation, jnp.int32)
    max_exact = N_REL_BUCKETS // 2
    d_f = jnp.maximum(dist, 1).astype(F32)
    large = max_exact + (jnp.log(d_f / max_exact) / math.log(REL_MAX_DISTANCE / max_exact)
                         * (N_REL_BUCKETS - max_exact)).astype(jnp.int32)
    large = jnp.minimum(large, N_REL_BUCKETS - 1)
    bucket = jnp.where(dist < max_exact, dist, large).reshape(-1)
    onehot = (bucket[None, :] == jnp.arange(N_REL_BUCKETS)[:, None]).astype(F32)
    return onehot, jnp.asarray(valid.reshape(1, -1), F32)


def _bias_gather(rel_g_t, onehot, valid, name):
    def body(r_ref, oh_ref, v_ref, o_ref):
        b = jnp.dot(r_ref[...], oh_ref[...], preferred_element_type=F32, precision=lax.Precision.HIGHEST)
        o_ref[...] = jnp.where(v_ref[...] > 0.5, b, NEG)
    return pl.pallas_call(body, out_shape=S((HEADS_PER_GROUP, onehot.shape[1]), F32), compiler_params=_cparams(), name=name)(rel_g_t, onehot, valid)


def _bias_scatter(dbias, onehot, name):
    def body(d_ref, oh_ref, o_ref):
        o_ref[...] = lax.dot_general(d_ref[...], oh_ref[...], NT, preferred_element_type=F32, precision=lax.Precision.HIGHEST)
    return pl.pallas_call(body, out_shape=S((HEADS_PER_GROUP, N_REL_BUCKETS), F32), compiler_params=_cparams(), name=name)(dbias, onehot)


ATTN_QB_FWD, ATTN_QB_BWD = 8, 8


def _attn_tiles(T, d, qb):
    seg = T // d
    nqb = min(qb, seg // ATTN_BLOCK)
    tq = nqb * ATTN_BLOCK
    return seg, nqb, tq, seg // tq


def _attn_fwd(qkv, bias, d, name, comm=None):
    T = qkv.shape[0]
    seg, nqb, tq, ns = _attn_tiles(T, d, ATTN_QB_FWD)
    W = ATTN_OUT_WIDTH
    scale = HEAD_DIM ** -0.5

    def body(q_ref, kh_ref, kc_ref, vh_ref, vc_ref, b_ref, o_ref, l_ref, s_scr, p_scr):
        n = pl.program_id(1)
        qv = q_ref[...]
        kk = jnp.concatenate([kh_ref[...], kc_ref[...]], axis=0)
        vv = jnp.concatenate([vh_ref[...], vc_ref[...]], axis=0)
        col = lax.broadcasted_iota(jnp.int32, (ATTN_BLOCK, 2 * ATTN_BLOCK), 1)
        kill = jnp.logical_and(n == 0, col < ATTN_BLOCK)
        lo = lax.broadcasted_iota(jnp.int32, (1, LANES), 1) < HEAD_DIM
        zero = jnp.zeros((), BF16)
        for j in range(nqb):
            rows = slice(j * ATTN_BLOCK, (j + 1) * ATTN_BLOCK)
            keys = slice(j * ATTN_BLOCK, (j + 2) * ATTN_BLOCK)
            for hp in range(HEADS_PER_GROUP // 2):
                ps = slice(hp * LANES, (hp + 1) * LANES)
                q2 = (qv[rows, ps].astype(F32) * scale).astype(BF16)
                k2 = kk[keys, ps]
                s_scr[2 * hp] = lax.dot_general(q2, jnp.where(lo, k2, zero), NT, preferred_element_type=F32)
                s_scr[2 * hp + 1] = lax.dot_general(q2, jnp.where(lo, zero, k2), NT, preferred_element_type=F32)
            s = s_scr[...] + b_ref[...]
            if j == 0:
                s = jnp.where(kill[None], NEG, s)
            m = jnp.max(s, axis=-1, keepdims=True)
            p = jnp.exp(s - m)
            den = jnp.sum(p, axis=-1, keepdims=True)
            p_scr[...] = p.astype(BF16)
            inv = 1.0 / den
            lse = m + jnp.log(den)
            for hp in range(HEADS_PER_GROUP // 2):
                ps = slice(hp * LANES, (hp + 1) * LANES)
                v2 = vv[keys, ps]
                o2 = (jnp.dot(p_scr[2 * hp], jnp.where(lo, v2, zero), preferred_element_type=F32)
                      + jnp.dot(p_scr[2 * hp + 1], jnp.where(lo, zero, v2), preferred_element_type=F32))
                o_ref[rows, ps] = (o2 * jnp.where(lo, inv[2 * hp], inv[2 * hp + 1])).astype(o_ref.dtype)
                l_ref[rows, ps] = jnp.where(lo, lse[2 * hp], lse[2 * hp + 1])

    def cur(c):
        return pl.BlockSpec((tq, W), lambda r, n: (r * ns + n, c))

    def halo(c):
        return pl.BlockSpec((ATTN_BLOCK, W), lambda r, n: (jnp.maximum((r * ns + n) * nqb - 1, 0), c))

    res, comm_res = _carrier_call(
        body, (d, ns),
        [cur(0), halo(1), cur(1), halo(2), cur(2), pl.BlockSpec(bias.shape, lambda r, n: (0, 0, 0))],
        [cur(0), cur(0)], [S((T, W), BF16), S((T, W), F32)],
        [pltpu.VMEM((HEADS_PER_GROUP, ATTN_BLOCK, 2 * ATTN_BLOCK), F32), pltpu.VMEM((HEADS_PER_GROUP, ATTN_BLOCK, 2 * ATTN_BLOCK), BF16)],
        [qkv, qkv, qkv, qkv, qkv, bias], comm, name)
    return res if comm is None else (res, comm_res)


def _attn_bwd(qkv, bias, lse, do, dd, d, name):
    T = qkv.shape[0]
    seg, nqb, tq, ns = _attn_tiles(T, d, ATTN_QB_BWD)
    W = ATTN_OUT_WIDTH
    B = ATTN_BLOCK
    scale = HEAD_DIM ** -0.5

    def body(q_ref, kh_ref, kc_ref, vh_ref, vc_ref, b_ref, l_ref, do_ref, dd_ref, dq_ref, dk_ref, dv_ref, db_ref, pk_ref, pv_ref,
             s_scr, dp_scr, p_scr, ds_scr):
        r, n = pl.program_id(0), pl.program_id(1)

        @pl.when(jnp.logical_and(r == 0, n == 0))
        def _():
            db_ref[...] = jnp.zeros_like(db_ref)

        @pl.when(n == 0)
        def _():
            pk_ref[...] = jnp.zeros_like(pk_ref)
            pv_ref[...] = jnp.zeros_like(pv_ref)

        @pl.when(n < ns)
        def _():
            qv = q_ref[...]
            kk = jnp.concatenate([kh_ref[...], kc_ref[...]], axis=0)
            vv = jnp.concatenate([vh_ref[...], vc_ref[...]], axis=0)
            lse_v, do_v, dd_v = l_ref[...], do_ref[...], dd_ref[...]
            col = lax.broadcasted_iota(jnp.int32, (B, 2 * B), 1)
            kill = jnp.logical_and(n == 0, col < B)
            dqs = [[None] * (HEADS_PER_GROUP // 2) for _ in range(nqb)]
            dks = [[None] * (HEADS_PER_GROUP // 2) for _ in range(nqb)]
            dvs = [[None] * (HEADS_PER_GROUP // 2) for _ in range(nqb)]
            H, HP = HEADS_PER_GROUP, HEADS_PER_GROUP // 2
            do_b = do_v.astype(BF16)
            lo = lax.broadcasted_iota(jnp.int32, (1, LANES), 1) < HEAD_DIM
            zero = jnp.zeros((), BF16)
            first = lambda t: jnp.where(lo, t, zero)
            second = lambda t: jnp.where(lo, zero, t)
            for j in range(nqb):
                rows = slice(j * B, (j + 1) * B)
                keys = slice(j * B, (j + 2) * B)
                for hp in range(HP):
                    ps = slice(hp * LANES, (hp + 1) * LANES)
                    q2 = (qv[rows, ps].astype(F32) * scale).astype(BF16)
                    k2, v2, do2 = kk[keys, ps], vv[keys, ps], do_b[rows, ps]
                    s_scr[2 * hp] = lax.dot_general(q2, first(k2), NT, preferred_element_type=F32)
                    s_scr[2 * hp + 1] = lax.dot_general(q2, second(k2), NT, preferred_element_type=F32)
                    dp_scr[2 * hp] = lax.dot_general(do2, first(v2), NT, preferred_element_type=F32)
                    dp_scr[2 * hp + 1] = lax.dot_general(do2, second(v2), NT, preferred_element_type=F32)
                lse_h = jnp.stack([lse_v[rows, h * HEAD_DIM:h * HEAD_DIM + 1] for h in range(H)], axis=0)
                dd_h = jnp.stack([dd_v[rows, h * HEAD_DIM:h * HEAD_DIM + 1] for h in range(H)], axis=0)
                s = s_scr[...] + b_ref[...]
                if j == 0:
                    s = jnp.where(kill[None], NEG, s)
                p = jnp.exp(s - lse_h)
                ds = p * (dp_scr[...] - dd_h)
                db_ref[...] += ds
                p_scr[...] = p.astype(BF16)
                ds_scr[...] = ds.astype(BF16)
                for hp in range(HP):
                    ps = slice(hp * LANES, (hp + 1) * LANES)
                    q2 = (qv[rows, ps].astype(F32) * scale).astype(BF16)
                    k2, do2 = kk[keys, ps], do_b[rows, ps]
                    pa, pb, da, db_ = p_scr[2 * hp], p_scr[2 * hp + 1], ds_scr[2 * hp], ds_scr[2 * hp + 1]
                    dvs[j][hp] = (lax.dot_general(pa, first(do2), TN, preferred_element_type=F32)
                                  + lax.dot_general(pb, second(do2), TN, preferred_element_type=F32))
                    dqs[j][hp] = (jnp.dot(da, first(k2), preferred_element_type=F32)
                                  + jnp.dot(db_, second(k2), preferred_element_type=F32)) * scale
                    dks[j][hp] = (lax.dot_general(da, first(q2), TN, preferred_element_type=F32)
                                  + lax.dot_general(db_, second(q2), TN, preferred_element_type=F32))
            dq_ref[...] = jnp.concatenate([jnp.concatenate(dqs[j], axis=1) for j in range(nqb)], axis=0).astype(dq_ref.dtype)
            for parts, out_ref, pend in ((dks, dk_ref, pk_ref), (dvs, dv_ref, pv_ref)):
                full = [jnp.concatenate(parts[j], axis=1) for j in range(nqb)]
                if tq > B:
                    out_ref[:tq - B] = pend[:tq - B].astype(out_ref.dtype)
                out_ref[tq - B:] = (pend[tq - B:] + full[0][:B]).astype(out_ref.dtype)
                for j in range(nqb - 1):
                    pend[j * B:(j + 1) * B] = full[j][B:] + full[j + 1][:B]
                pend[tq - B:] = full[nqb - 1][B:]

        @pl.when(n == ns)
        def _():
            dk_ref[...] = pk_ref[...].astype(dk_ref.dtype)
            dv_ref[...] = pv_ref[...].astype(dv_ref.dtype)

    def cur(c):
        return pl.BlockSpec((tq, W), lambda r, n: (r * ns + jnp.minimum(n, ns - 1), c))

    def halo(c):
        return pl.BlockSpec((B, W), lambda r, n: (jnp.maximum((r * ns + jnp.minimum(n, ns - 1)) * nqb - 1, 0), c))

    late = pl.BlockSpec((tq, W), lambda r, n: (r * ns + jnp.clip(n - 1, 0, ns - 1), 0))
    bspec = pl.BlockSpec(bias.shape, lambda r, n: (0, 0, 0))
    return pl.pallas_call(
        body, grid=(d, ns + 1),
        in_specs=[cur(0), halo(1), cur(1), halo(2), cur(2), bspec, cur(0), cur(0), cur(0)],
        out_specs=[cur(0), late, late, bspec],
        out_shape=[S((T, W), BF16)] * 3 + [S(bias.shape, F32)],
        scratch_shapes=[pltpu.VMEM((tq, W), F32), pltpu.VMEM((tq, W), F32)]
                       + [pltpu.VMEM((HEADS_PER_GROUP, B, 2 * B), t) for t in (F32, F32, BF16, BF16)],
        compiler_params=_cparams(("arbitrary", "arbitrary")), name=name,
    )(qkv, qkv, qkv, qkv, qkv, bias, lse, do, dd)


CONV_TM, CONV_TC = 512, 1024


def _shift_down(x, halo8, s, row8):
    xr = pltpu.roll(x, s, 0)
    first = jnp.where(row8 < s, pltpu.roll(halo8, s, 0), xr[:8])
    return jnp.concatenate([first, xr[8:]], axis=0)


def _shift_up(x, halo8, s, row8):
    n = x.shape[0]
    xr = pltpu.roll(x, n - s, 0)
    last = jnp.where(row8 >= 8 - s, pltpu.roll(halo8, 8 - s, 0), xr[n - 8:])
    return jnp.concatenate([xr[:n - 8], last], axis=0)


def _conv_fwd(x, w, b, name, comm=None):
    T, C = x.shape
    tm, tc = min(CONV_TM, T), CONV_TC

    def body(x_ref, p_ref, w_ref, b_ref, u_ref, a_ref):
        ti = pl.program_id(1)
        xv = x_ref[...]
        p8 = jnp.where(ti == 0, 0.0, p_ref[...])
        wv = w_ref[...]
        row8 = lax.broadcasted_iota(jnp.int32, (8, tc), 0)
        u = xv * wv[3:4] + b_ref[...]
        for s in (1, 2, 3):
            u = u + _shift_down(xv, p8, s, row8) * wv[3 - s:4 - s]
        u_ref[...] = u
        a_ref[...] = _silu(u)

    cur = pl.BlockSpec((tm, tc), lambda cj, ti: (ti, cj))
    halo = pl.BlockSpec((8, tc), lambda cj, ti: (jnp.maximum(ti * (tm // 8) - 1, 0), cj))
    res, comm_res = _carrier_call(
        body, (C // tc, T // tm),
        [cur, halo, pl.BlockSpec((CONV_WIDTH, tc), lambda cj, ti: (0, cj)), pl.BlockSpec((1, tc), lambda cj, ti: (0, cj))],
        [cur, cur], [S((T, C), F32)] * 2, [], [x, x, w, b], comm, name)
    return res if comm is None else (res, comm_res)


def _conv_bwd(dact, u, x, w, name):
    T, C = x.shape
    tm, tc = min(CONV_TM, T), CONV_TC
    nt = T // tm

    def body(d_ref, dn_ref, u_ref, un_ref, x_ref, w_ref, dx_ref, dw_ref, db_ref):
        ti = pl.program_id(1)

        @pl.when(ti == 0)
        def _():
            dw_ref[...] = jnp.zeros_like(dw_ref)
            db_ref[...] = jnp.zeros_like(db_ref)

        du = d_ref[...] * _dsilu(u_ref[...])
        dun = jnp.where(ti == nt - 1, 0.0, dn_ref[...] * _dsilu(un_ref[...]))
        xv = x_ref[...]
        wv = w_ref[...]
        row8 = lax.broadcasted_iota(jnp.int32, (8, tc), 0)
        dx = du * wv[3:4]
        dws = [None] * CONV_WIDTH
        dws[3] = jnp.sum(du * xv, axis=0, keepdims=True)
        for s in (1, 2, 3):
            up = _shift_up(du, dun, s, row8)
            dx = dx + up * wv[3 - s:4 - s]
            dws[3 - s] = jnp.sum(up * xv, axis=0, keepdims=True)
        dx_ref[...] = dx.astype(dx_ref.dtype)
        dw_ref[...] += jnp.concatenate(dws, axis=0)
        db_ref[...] += jnp.sum(du, axis=0, keepdims=True)

    cur = pl.BlockSpec((tm, tc), lambda cj, ti: (ti, cj))
    nxt = pl.BlockSpec((8, tc), lambda cj, ti: (jnp.minimum((ti + 1) * (tm // 8), T // 8 - 1), cj))
    return pl.pallas_call(
        body, grid=(C // tc, nt),
        in_specs=[cur, nxt, cur, nxt, cur, pl.BlockSpec((CONV_WIDTH, tc), lambda cj, ti: (0, cj))],
        out_specs=[cur, pl.BlockSpec((CONV_WIDTH, tc), lambda cj, ti: (0, cj)), pl.BlockSpec((1, tc), lambda cj, ti: (0, cj))],
        out_shape=[S((T, C), BF16), S((CONV_WIDTH, C), F32), S((1, C), F32)],
        compiler_params=_cparams(("parallel", "arbitrary")), name=name)(dact, dact, u, u, x, w)


def _ssd_consts():
    i = np.arange(SSD_CHUNK)
    tril = (i[None, :] <= i[:, None]).astype(np.float32)
    trils = (i[None, :] < i[:, None]).astype(np.float32)
    head = np.repeat(np.arange(N_SSM_HEADS), D_INNER // N_SSM_HEADS)
    et = (head[None, :] == np.arange(N_SSM_HEADS)[:, None]).astype(np.float32)
    c = lambda a: jnp.asarray(a, BF16)
    return dict(tril=c(tril), triu=c(tril.T), trils=c(trils), et=c(et), e=c(et.T))


def _ssd_common(act_ref, dt_ref, dtT_ref, al_ref, alT_ref, tril_ref, triu_ref, et_ref):
    a_row = -jnp.exp(al_ref[...])
    a_col = -jnp.exp(alT_ref[...])
    dt, dtT = dt_ref[...], dtT_ref[...]
    la = _dotx_l(tril_ref[...], dt * a_row)
    laT = _dotx_r(dtT * a_col, triu_ref[...])
    et = et_ref[...]
    la_e = _dotx_r(la, et)
    dt_e = _dotx_r(dt, et, parts=2)
    x = act_ref[:, :D_INNER]
    xdt = x * dt_e
    la_q = la_e[SSD_CHUNK - 1:SSD_CHUNK, :]
    return a_row, a_col, dt, dtT, la, laT, la_e, dt_e, x, xdt, la_q


def _decay(la, laT, h, causal):
    seg = la[:, h:h + 1] - laT[h:h + 1, :]
    return jnp.exp(jnp.where(causal, seg, NEG))


def _ssd_fwd(act, dt, dtT, alog, dskip_e, cs, name, comm=None):
    T = act.shape[0]
    nc = T // SSD_CHUNK
    Q, G, GW = SSD_CHUNK, N_SSM_GROUPS, D_INNER // N_SSM_GROUPS

    def body(act_ref, dt_ref, dtT_ref, al_ref, alT_ref, dsk_ref, tril_ref, triu_ref, et_ref, y_ref, st_ref, scr):
        @pl.when(pl.program_id(0) == 0)
        def _():
            scr[...] = jnp.zeros_like(scr)
        st_ref[0] = scr[...]
        a_row, a_col, dtv, dtTv, la, laT, la_e, dt_e, x, xdt, la_q = _ssd_common(
            act_ref, dt_ref, dtT_ref, al_ref, alT_ref, tril_ref, triu_ref, et_ref)
        ela = jnp.exp(la_e)
        xdt_b = xdt.astype(BF16)
        xdte_b = (xdt * jnp.exp(la_q - la_e)).astype(BF16)
        ela_q = jnp.exp(la_q)
        causal = lax.broadcasted_iota(jnp.int32, (Q, Q), 0) >= lax.broadcasted_iota(jnp.int32, (Q, Q), 1)
        for g in range(G):
            gs = slice(g * GW, (g + 1) * GW)
            Bg = act_ref[:, D_INNER + g * D_STATE:D_INNER + (g + 1) * D_STATE].astype(BF16)
            Cg = act_ref[:, D_INNER + G * D_STATE + g * D_STATE:D_INNER + G * D_STATE + (g + 1) * D_STATE].astype(BF16)
            cb = lax.dot_general(Cg, Bg, NT, preferred_element_type=F32)
            st = scr[g]
            y_inter = jnp.dot(Cg, st.astype(BF16), preferred_element_type=F32) * ela[:, gs]
            ys = []
            for hh in range(HEADS_PER_GROUP):
                h = g * HEADS_PER_GROUP + hh
                m = (cb * _decay(la, laT, h, causal)).astype(BF16)
                ys.append(jnp.dot(m, xdt_b[:, h * HEAD_DIM:(h + 1) * HEAD_DIM], preferred_element_type=F32))
            y_ref[:, gs] = (jnp.concatenate(ys, axis=1) + y_inter + x[:, gs] * dsk_ref[:, gs]).astype(y_ref.dtype)
            scr[g] = st * ela_q[:, gs] + lax.dot_general(Bg, xdte_b[:, gs], TN, preferred_element_type=F32)

    full = lambda a: pl.BlockSpec(a.shape, lambda c: (0,) * a.ndim)
    al, alT = alog.reshape(1, -1), alog.reshape(-1, 1)
    res, comm_res = _carrier_call(
        body, nc,
        [pl.BlockSpec((Q, XBC_WIDTH), lambda c: (c, 0)), pl.BlockSpec((Q, N_SSM_HEADS), lambda c: (c, 0)),
         pl.BlockSpec((N_SSM_HEADS, Q), lambda c: (0, c)), full(al), full(alT), full(dskip_e),
         full(cs["tril"]), full(cs["triu"]), full(cs["et"])],
        [pl.BlockSpec((Q, D_INNER), lambda c: (c, 0)), pl.BlockSpec((1, G, D_STATE, GW), lambda c: (c, 0, 0, 0))],
        [S((T, D_INNER), BF16), S((nc, G, D_STATE, GW), F32)],
        [pltpu.VMEM((G, D_STATE, GW), F32)],
        [act, dt, dtT, al, alT, dskip_e, cs["tril"], cs["triu"], cs["et"]], comm, name)
    return res if comm is None else (res, comm_res)


def _ssd_bwd(dy, act, dt, dtT, alog, dskip_e, states, cs, name, comm=None):
    T = act.shape[0]
    nc = T // SSD_CHUNK
    Q, G, GW, H = SSD_CHUNK, N_SSM_GROUPS, D_INNER // N_SSM_GROUPS, N_SSM_HEADS

    def body(dy_ref, act_ref, dt_ref, dtT_ref, al_ref, alT_ref, dsk_ref, stp_ref, tril_ref, triu_ref, trils_ref,
             et_ref, e_ref, dact_ref, ddt_ref, ddtT_ref, da_ref, daT_ref, dsk_out_ref, dst, wbuf, ubuf, vbuf, sbuf, dm_scr, m_scr):
        @pl.when(pl.program_id(0) == 0)
        def _():
            dst[...] = jnp.zeros_like(dst)
            da_ref[...] = jnp.zeros_like(da_ref)
            daT_ref[...] = jnp.zeros_like(daT_ref)
            dsk_out_ref[...] = jnp.zeros_like(dsk_out_ref)
        a_row, a_col, dtv, dtTv, la, laT, la_e, dt_e, x, xdt, la_q = _ssd_common(
            act_ref, dt_ref, dtT_ref, al_ref, alT_ref, tril_ref, triu_ref, et_ref)
        dyv = dy_ref[...].astype(F32)
        ela = jnp.exp(la_e)
        e_end = jnp.exp(la_q - la_e)
        ela_q = jnp.exp(la_q)
        dye_b = (dyv * ela).astype(BF16)
        dy_b = dyv.astype(BF16)
        xdt_b = xdt.astype(BF16)
        xdte_b = (xdt * e_end).astype(BF16)
        ri = lax.broadcasted_iota(jnp.int32, (Q, Q), 0)
        ci = lax.broadcasted_iota(jnp.int32, (Q, Q), 1)
        causal = ri >= ci
        rows = []
        for g in range(G):
            gs = slice(g * GW, (g + 1) * GW)
            Bg = act_ref[:, D_INNER + g * D_STATE:D_INNER + (g + 1) * D_STATE].astype(BF16)
            Cg = act_ref[:, D_INNER + G * D_STATE + g * D_STATE:D_INNER + G * D_STATE + (g + 1) * D_STATE].astype(BF16)
            cb = lax.dot_general(Cg, Bg, NT, preferred_element_type=F32)
            stp = stp_ref[0, g]
            stp_b = stp.astype(BF16)
            dstv = dst[g]
            dst_b = dstv.astype(BF16)
            y_inter = jnp.dot(Cg, stp_b, preferred_element_type=F32) * ela[:, gs]
            wbuf[:, gs] = dyv[:, gs] * y_inter
            dxdt_state = jnp.dot(Bg, dst_b, preferred_element_type=F32) * e_end[:, gs]
            ubuf[:, gs] = dxdt_state * xdt[:, gs]
            dC = lax.dot_general(dye_b[:, gs], stp_b, NT, preferred_element_type=F32)
            dB = lax.dot_general(xdte_b[:, gs], dst_b, NT, preferred_element_type=F32)
            sbuf[:, gs] = jnp.broadcast_to(jnp.sum(dstv * stp, axis=0, keepdims=True), (8, GW))
            dst[g] = dstv * ela_q[:, gs] + lax.dot_general(Cg, dye_b[:, gs], TN, preferred_element_type=F32)
            for hh in range(HEADS_PER_GROUP):
                hs = slice((g * HEADS_PER_GROUP + hh) * HEAD_DIM, (g * HEADS_PER_GROUP + hh + 1) * HEAD_DIM)
                dm_scr[hh] = lax.dot_general(dy_b[:, hs], xdt_b[:, hs], NT, preferred_element_type=F32)
            dG = jnp.zeros((Q, Q), F32)
            for hh in range(HEADS_PER_GROUP):
                L = _decay(la, laT, g * HEADS_PER_GROUP + hh, causal)
                M = cb * L
                dM = dm_scr[hh]
                dG = dG + dM * L
                W = dM * M
                rows.append(jnp.sum(W.T, axis=0, keepdims=True) - jnp.sum(W, axis=0, keepdims=True))
                m_scr[hh] = M.astype(BF16)
            dxs = []
            for hh in range(HEADS_PER_GROUP):
                hs = slice((g * HEADS_PER_GROUP + hh) * HEAD_DIM, (g * HEADS_PER_GROUP + hh + 1) * HEAD_DIM)
                dxs.append(lax.dot_general(m_scr[hh], dy_b[:, hs], TN, preferred_element_type=F32))
            dG_b = dG.astype(BF16)
            dC = dC + jnp.dot(dG_b, Bg, preferred_element_type=F32)
            dB = dB + lax.dot_general(dG_b, Cg, TN, preferred_element_type=F32)
            dxdt = jnp.concatenate(dxs, axis=1) + dxdt_state
            vbuf[:, gs] = dxdt * x[:, gs]
            dact_ref[:, gs] = dxdt * dt_e[:, gs] + dyv[:, gs] * dsk_ref[:, gs]
            dact_ref[:, D_INNER + g * D_STATE:D_INNER + (g + 1) * D_STATE] = dB
            dact_ref[:, D_INNER + G * D_STATE + g * D_STATE:D_INNER + G * D_STATE + (g + 1) * D_STATE] = dC
        e = e_ref[...]
        w = _dotx_r(wbuf[...], e, parts=2)
        u = _dotx_r(ubuf[...], e, parts=2)
        vx = _dotx_r(vbuf[...], e, parts=2)
        dsk = _dotx_r(jnp.broadcast_to(jnp.sum(dyv * x, axis=0, keepdims=True), (8, D_INNER)), e, parts=2)[0:1]
        s0 =_dotx_r(sbuf[...], e, parts=2)[0:1] * jnp.exp(la[Q - 1:Q, :])
        ddelta = _dotx_l(triu_ref[...], w) + _dotx_l(trils_ref[...], u) + s0
        ddt_ref[...] = ddelta * a_row + vx
        ddeltaT = _dotx_r(jnp.concatenate(rows, axis=0), tril_ref[...])
        ddtT_ref[...] = ddeltaT * a_col
        da_ref[...] += jnp.sum(ddelta * dtv, axis=0, keepdims=True)
        daT_ref[...] += jnp.sum(ddeltaT * dtTv, axis=1, keepdims=True)
        dsk_out_ref[...] += dsk

    rev = lambda c: nc - 1 - c
    full = lambda a: pl.BlockSpec(a.shape, lambda c: (0,) * a.ndim)
    al, alT = alog.reshape(1, -1), alog.reshape(-1, 1)
    consts = [cs[k] for k in ("tril", "triu", "trils", "et", "e")]
    res, comm_res = _carrier_call(
        body, nc,
        [pl.BlockSpec((Q, D_INNER), lambda c: (rev(c), 0)), pl.BlockSpec((Q, XBC_WIDTH), lambda c: (rev(c), 0)),
         pl.BlockSpec((Q, H), lambda c: (rev(c), 0)), pl.BlockSpec((H, Q), lambda c: (0, rev(c))),
         full(al), full(alT), full(dskip_e), pl.BlockSpec((1, G, D_STATE, GW), lambda c: (rev(c), 0, 0, 0))]
        + [full(a) for a in consts],
        [pl.BlockSpec((Q, XBC_WIDTH), lambda c: (rev(c), 0)), pl.BlockSpec((Q, H), lambda c: (rev(c), 0)),
         pl.BlockSpec((H, Q), lambda c: (0, rev(c))), pl.BlockSpec((1, H), lambda c: (0, 0)),
         pl.BlockSpec((H, 1), lambda c: (0, 0)), pl.BlockSpec((1, H), lambda c: (0, 0))],
        [S((T, XBC_WIDTH), F32), S((T, H), F32), S((H, T), F32), S((1, H), F32), S((H, 1), F32), S((1, H), F32)],
        [pltpu.VMEM((G, D_STATE, GW), F32), pltpu.VMEM((Q, D_INNER), F32), pltpu.VMEM((Q, D_INNER), F32),
         pltpu.VMEM((Q, D_INNER), F32), pltpu.VMEM((8, D_INNER), F32),
         pltpu.VMEM((HEADS_PER_GROUP, Q, Q), F32), pltpu.VMEM((HEADS_PER_GROUP, Q, Q), BF16)],
        [dy, act, dt, dtT, al, alT, dskip_e, states] + consts, comm, name)
    return res if comm is None else (res, comm_res)


def _a_log_grad(da, daT_row, alog, name):
    def body(a_ref, b_ref, al_ref, o_ref):
        o_ref[...] = (a_ref[...] + b_ref[...]) * (-jnp.exp(al_ref[...]))
    return pl.pallas_call(body, out_shape=S((1, N_SSM_HEADS), F32), name=name)(da, daT_row, alog.reshape(1, -1))


def _place():
    x, y, c = lax.axis_index("x"), lax.axis_index("y"), lax.axis_index("c")
    return x, y, c, [(1 - x, y), (x, 1 - y), (1 - x, 1 - y)]


def _to_sibling(to_sib, name):
    npc = len(to_sib)

    def body(*refs):
        s_refs, o_refs, send_sems, recv_sems = refs[:npc], refs[npc:2 * npc], refs[2 * npc], refs[2 * npc + 1]
        x, y, c, _ = _place()
        cps = [pltpu.make_async_remote_copy(
            src_ref=s_refs[i], dst_ref=o_refs[i], send_sem=send_sems.at[i], recv_sem=recv_sems.at[i],
            device_id=(x, y, 1 - c), device_id_type=MESH) for i in range(npc)]
        for cp in cps:
            cp.start()
        for cp in cps:
            cp.wait()

    anys = pl.BlockSpec(memory_space=pl.ANY)
    return pl.pallas_call(
        body, in_specs=[anys] * npc, out_specs=[anys] * npc, out_shape=[S(s.shape, s.dtype) for s in to_sib],
        scratch_shapes=[pltpu.SemaphoreType.DMA((npc,)), pltpu.SemaphoreType.DMA((npc,))],
        name=name)(*to_sib)


def _rs_begin(pieces, name):
    c = lax.axis_index("c")
    by_core = [p.reshape(4, 2, p.shape[1], p.shape[2]) for p in pieces]
    to_sib = [lax.dynamic_index_in_dim(p, 1 - c, axis=1, keepdims=False).astype(BF16) for p in by_core]
    keep = [lax.dynamic_index_in_dim(p, c, axis=1, keepdims=False) for p in by_core]
    from_sib = _to_sibling(to_sib, name + "_d2d")

    def add1(a, b):
        s = a + b
        return [s, s], []

    parts, parts_b = [], []
    for i, (k, f) in enumerate(zip(keep, from_sib)):
        _, r, C = k.shape
        p, pb = _rowwise(add1, [k.reshape(4 * r, C), f.reshape(4 * r, C)], [], [(C, F32), (C, BF16)], tm=2048, name=f"{name}_add1_{i}")
        parts.append(p.reshape(4, r, C))
        parts_b.append(pb.reshape(4, r, C))
    return parts, parts_b


def _rs_finish(parts, got, name):
    x, y = lax.axis_index("x"), lax.axis_index("y")

    def add2(a, b, c_, d_):
        return [((a + b) + c_) + d_], []

    outs = []
    for i, (p, g) in enumerate(zip(parts, got)):
        own = lax.dynamic_index_in_dim(p, 2 * x + y, axis=0, keepdims=False)
        outs.append(_rowwise(add2, [own, g[0], g[1], g[2]], [], [(p.shape[2], F32)], tm=2048, name=f"{name}_add2_{i}")[0])
    return outs


class _GradReduce:
    EARLY = ("w_ffn_in", "w_ffn_out", "w_out", "w_attn_branch", "w_ssm_branch")

    def __init__(self):
        self.out, self.keys, self.parts, self.parts_b = {}, [], [], []

    def _begin(self, l, names, gr, name):
        parts, parts_b = _rs_begin([_shard(nm, gr[nm]) for nm in names], name)
        self.keys += [(l, nm) for nm in names]
        self.parts += parts
        self.parts_b += parts_b

    def carry_fn(self, l):
        if l != 0:
            return None

        def fn(gr):
            self._begin(0, self.EARLY, gr, "rs_early_l0")
            return _comm_to_chips(self.parts_b)
        return fn

    def carry2_fn(self, l):
        if l != 0:
            return None

        def fn(gr):
            self.parts2, parts2_b = _rs_begin([_shard("w_in", gr["w_in"])], "rs_w_in_l0")
            return _comm_to_chips(parts2_b)
        return fn

    def done(self, l, gr, carried):
        if l == DEPTH - 1:
            self._begin(l, BIG, gr, f"rs_l{l}")
            return
        for key, o in zip(self.keys, _rs_finish(self.parts, carried[0], "rs_carried")):
            self.out[key] = o
        self.out[(0, "w_in")] = _rs_finish(self.parts2, carried[1], "rs_w_in_l0")[0]


def _all_reduce_small(v, name):
    R, C = v.shape

    def body(x_ref, out_ref, buf, send_sems, recv_sems):
        x, y, c, chips = _place()
        me, sib = (x, y, c), (x, y, 1 - c)

        def rows(dev):
            return buf.at[4 * dev[0] + 2 * dev[1] + dev[2]]

        def copy(k, block, to, src=None):
            return pltpu.make_async_remote_copy(
                src_ref=rows(block) if src is None else src, dst_ref=rows(block),
                send_sem=send_sems.at[k], recv_sem=recv_sems.at[k], device_id=to, device_id_type=MESH)

        buf[4 * x + 2 * y + c] = x_ref[...]
        first = [copy(0, me, sib, src=x_ref)] + [copy(1 + j, me, (*chip, c), src=x_ref) for j, chip in enumerate(chips)]
        for cp in first:
            cp.start()
        passed = [copy(4 + j, (*chip, c), sib) for j, chip in enumerate(chips)]
        for j, chip in enumerate(chips):
            copy(1 + j, (*chip, c), me).wait_recv()
            passed[j].start()
        copy(0, sib, me).wait_recv()
        for j, chip in enumerate(chips):
            copy(4 + j, (*chip, 1 - c), me).wait_recv()
        for cp in first + passed:
            cp.wait_send()
        acc = buf[0]
        for j in range(1, N_DEV):
            acc = acc + buf[j]
        out_ref[...] = acc

    vm = pl.BlockSpec(memory_space=pltpu.VMEM)
    return pl.pallas_call(
        body, in_specs=[vm], out_specs=vm, out_shape=S((R, C), F32),
        scratch_shapes=[pltpu.VMEM((N_DEV, R, C), F32), pltpu.SemaphoreType.DMA((7,)), pltpu.SemaphoreType.DMA((7,))],
        compiler_params=pltpu.CompilerParams(vmem_limit_bytes=VMEM_LIMIT), name=name)(v)


SEG = (("q", 0, 1536), ("k", 1536, 1536), ("v", 3072, 1536), ("z", 4608, 2048), ("xbc", 6656, 3072), ("dt", 9728, 32), ("gl", 9760, 2048))


def _split_w_in(w_in_full):
    out = {}
    for nm, off, n in SEG:
        w = w_in_full[:, off:off + n]
        if nm == "dt":
            w = jnp.pad(w, ((0, 0), (0, LANES - n)))
        out[nm] = w
    W = ATTN_OUT_WIDTH
    out["qkv"] = [jnp.concatenate([out[s][:, g * W:(g + 1) * W] for s in ("q", "k", "v")], axis=1) for g in range(N_DIL)]
    out["qkv_t"] = [[out[s][:, g * W:(g + 1) * W] for s in ("q", "k", "v")] for g in range(N_DIL)]
    return out


def _layer_fwd(h, p, W, biases, cs, l, carry=None, late=None, early=None):
    T = h.shape[0]
    nm = lambda s: f"{s}_l{l}"
    sv = {"h_in": h}
    dils = [d for _, d in DILATED_GROUPS[1:]]
    if early is None:
        xns = _rmsnorm_fwd(h, p["norm1_w"], nm("norm1"), dils=dils)
    else:
        xns, spread = _rmsnorm_fwd(h, p["norm1_w"], nm("norm1"), dils=dils, comm=early[0])
        W["w_in"] = _split_w_in(early[1](_run_comm(_comm_gather_pass(spread), nm("pass_w_in")))["w_in"])
    xn = xns[0]
    wi = W["w_in"]
    z = _mm(xn, wi["z"], out_dtype=BF16, name=nm("proj_z"))
    xbc = _mm(xn, wi["xbc"], name=nm("proj_xbc"))
    dt_raw = _mm(xn, wi["dt"], name=nm("proj_dt"))
    gl = _mm(xn, wi["gl"], out_dtype=BF16, name=nm("proj_gl"))
    conv_args = (xbc, p["conv_w"], p["conv_b"].reshape(1, -1), nm("conv"))
    if late is None:
        u_conv, act = _conv_fwd(*conv_args)
    else:
        (u_conv, act), spread = _conv_fwd(*conv_args, comm=late[0])
    os_, ls, qkvs = [], [], []
    for g, (window, dil) in enumerate(DILATED_GROUPS):
        qkv = _mm(xns[g], wi["qkv"][g], out_dtype=BF16, name=nm(f"proj_qkv_g{g}"))
        if late is not None and g == 0:
            (o, lse), arrived = _attn_fwd(qkv, biases[g], dil, nm(f"attn_fwd_g{g}"), comm=_comm_gather_pass(spread))
            W.update(late[1](arrived))
        else:
            o, lse = _attn_fwd(qkv, biases[g], dil, nm(f"attn_fwd_g{g}"))
        os_.append(o)
        ls.append(lse)
        qkvs.append(qkv)
    attn_b, attn_f = _combine_fwd(os_, ls, nm("combine"))
    dt = _dt_fwd(dt_raw, p["dt_bias"], nm("dt"))
    dtT = dt.T
    dskip_e = jnp.repeat(p["d_skip"], D_INNER // N_SSM_HEADS).reshape(1, -1)
    carried = None
    if carry is None:
        y, states = _ssd_fwd(act, dt, dtT, p["a_log"], dskip_e, cs, nm("ssd_fwd"))
        ssm = _ssm_norm_fwd(y, z, p["ssm_norm_w"], nm("ssm_norm"))
    else:
        (y, states), spread = _ssd_fwd(act, dt, dtT, p["a_log"], dskip_e, cs, nm("ssd_fwd"), comm=carry)
        ssm, carried = _ssm_norm_fwd(y, z, p["ssm_norm_w"], nm("ssm_norm"), comm=_comm_gather_pass(spread))
    a_br = _mm(attn_b, W["w_attn_branch"], out_dtype=BF16, name=nm("attn_branch"))
    s_br = _mm(ssm, W["w_ssm_branch"], out_dtype=BF16, name=nm("ssm_branch"))
    h_mid, merged = _gate_out_proj(a_br, s_br, gl, h, W["w_out"], nm("gate_out_proj"))
    xn2 = _rmsnorm_fwd(h_mid, p["norm2_w"], nm("norm2"))[0]
    W["w_ffn_in_p"] = _ffn_perm(W["w_ffn_in"])
    u_ffn, ffn_act = _mm(xn2, W["w_ffn_in_p"], tm=512, tn=D_FF, epilogue=_swiglu_epilogue, outs=[(2 * D_FF, BF16), (D_FF, BF16)],
                         name=nm("ffn_in_swiglu"))
    h_out = _mm(ffn_act, W["w_ffn_out"], acc=h_mid, name=nm("ffn_out"))
    sv.update(xn=xn, xns=xns, qkvs=qkvs, z=z, xbc=xbc, dt_raw=dt_raw, gl=gl, ls=ls, attn_b=attn_b, attn_f=attn_f, u_conv=u_conv,
              act=act, dt=dt, dtT=dtT, dskip_e=dskip_e, y=y, states=states, ssm=ssm, a_br=a_br, s_br=s_br, merged=merged,
              h_mid=h_mid, xn2=xn2, u_ffn=u_ffn, ffn_act=ffn_act)
    return h_out, sv, carried


def _layer_bwd(dh, sv, p, W, biases, cs, head_ones, l, carry_fn=None, carry2_fn=None):
    T = dh.shape[0]
    nm = lambda s: f"{s}_l{l}"
    gr = {}
    du = _mm(dh, W["w_ffn_out"], tb=True, tm=512, tn=FFN_HALF, extras=[sv["u_ffn"]], epilogue=_dswiglu_epilogue, outs=[(2 * D_FF, BF16)],
             name=nm("d_ffn_act_swiglu"))
    gr["w_ffn_out"] = _mm(sv["ffn_act"], dh, ta=True, name=nm("g_ffn_out"))
    dxn2 = _mm(du, W["w_ffn_in_p"], tb=True, out_dtype=BF16, name=nm("d_xn2"))
    gr["w_ffn_in"] = _ffn_unperm(_mm(sv["xn2"], du, ta=True, name=nm("g_ffn_in")))
    dh_mid, gr["norm2_w"] = _rmsnorm_bwd(dxn2, sv["h_mid"], p["norm2_w"], dh, nm("d_norm2"))
    gr["w_out"] = _mm(sv["merged"], dh_mid, ta=True, name=nm("g_out"))
    d_a, d_s, dgl = _d_out_proj_gate(dh_mid, sv["a_br"], sv["s_br"], sv["gl"], W["w_out"], nm("d_out_proj_gate"))
    dattn = _mm(d_a, W["w_attn_branch"], tb=True, out_dtype=BF16, name=nm("d_attn"))
    gr["w_attn_branch"] = _mm(sv["attn_b"], d_a, ta=True, name=nm("g_attn_branch"))
    dssm = _mm(d_s, W["w_ssm_branch"], tb=True, out_dtype=BF16, name=nm("d_ssm"))
    gr["w_ssm_branch"] = _mm(sv["ssm"], d_s, ta=True, name=nm("g_ssm_branch"))
    dy, dz, gr["ssm_norm_w"] = _ssm_norm_bwd(dssm, sv["y"], sv["z"], p["ssm_norm_w"], nm("d_ssm_norm"))
    ssd_args = (dy, sv["act"], sv["dt"], sv["dtT"], p["a_log"], sv["dskip_e"], sv["states"], cs, nm("ssd_bwd"))
    carried = None
    if carry_fn is None:
        dact_c, ddt_a, ddt_bT, da, daT, dskip = _ssd_bwd(*ssd_args)
    else:
        (dact_c, ddt_a, ddt_bT, da, daT, dskip), carried = _ssd_bwd(*ssd_args, comm=carry_fn(gr))
    gr["a_log"] = _a_log_grad(da, daT.T, p["a_log"], nm("g_a_log")).reshape(-1)
    gr["d_skip"] = dskip.reshape(-1)
    ddt_raw, ddt_bias = _dt_bwd(ddt_a, ddt_bT.T, sv["dt_raw"], p["dt_bias"], nm("d_dt"))
    gr["dt_bias"] = ddt_bias.reshape(-1)
    dxbc, gr["conv_w"], dconv_b = _conv_bwd(dact_c, sv["u_conv"], sv["xbc"], p["conv_w"], nm("d_conv"))
    gr["conv_b"] = dconv_b.reshape(-1)
    outs = _combine_bwd(dattn, sv["attn_f"], sv["ls"], head_ones, nm("d_combine"))
    wi = W["w_in"]
    dbias, dxn = [], None
    gqkv = [[None] * N_DIL for _ in range(3)]
    for g, (window, dil) in enumerate(DILATED_GROUPS):
        dq, dk, dv, db = _attn_bwd(sv["qkvs"][g], biases[g], sv["ls"][g], outs[2 * g], outs[2 * g + 1], dil, nm(f"attn_bwd_g{g}"))
        dbias.append(db)
        dxn = _mm_dil([dq, dk, dv], wi["qkv_t"][g], dil, dxn, nm(f"d_xn_qkv_g{g}"))
        for i, dseg in enumerate((dq, dk, dv)):
            gqkv[i][g] = _mm(sv["xns"][g], dseg, ta=True, name=nm(f"g_in_{'qkv'[i]}_g{g}"))
    parts = (("z", dz), ("xbc", dxbc), ("dt", ddt_raw), ("gl", dgl))
    gws = gqkv[0] + gqkv[1] + gqkv[2]
    for sname, dseg in parts:
        gw = _mm(sv["xn"], dseg, ta=True, name=nm("g_in_" + sname))
        gws.append(gw[:, :N_SSM_HEADS] if sname == "dt" else gw)
    gr["w_in"] = jnp.concatenate(gws, axis=1)
    carried2 = None
    for sname, dseg in parts:
        if carry2_fn is not None and sname == "xbc":
            dxn, carried2 = _mm(dseg, wi[sname], tb=True, acc=dxn, name=nm("d_xn_" + sname), comm=carry2_fn(gr))
        else:
            dxn = _mm(dseg, wi[sname], tb=True, acc=dxn, name=nm("d_xn_" + sname))
    dh_in, gr["norm1_w"] = _rmsnorm_bwd(dxn, sv["h_in"], p["norm1_w"], dh_mid, nm("d_norm1"))
    return dh_in, gr, dbias, (carried, carried2)


def _step_local(x, tgt, small, Wfull, rel_bias, final_norm_w, prefetch=None, grad_reduce=None, late0=None, early0=None):
    cs = _ssd_consts()
    head = np.repeat(np.arange(HEADS_PER_GROUP), HEAD_DIM)
    head_ones = jnp.asarray(head[:, None] == head[None, :], BF16)
    biases, onehots = [], []
    for g, (window, dil) in enumerate(DILATED_GROUPS):
        onehot, valid = _bias_consts(dil, window // dil)
        rel_g_t = rel_bias[:, g * HEADS_PER_GROUP:(g + 1) * HEADS_PER_GROUP].T
        b = _bias_gather(rel_g_t, onehot, valid, f"bias_gather_g{g}")
        biases.append(b.reshape(HEADS_PER_GROUP, ATTN_BLOCK, 2 * ATTN_BLOCK))
        onehots.append(onehot)
    h, saved, carried = x, [], None
    Wfull = list(Wfull)
    for l in range(DEPTH):
        W = dict(prefetch[1](carried) if Wfull[l] is None else Wfull[l])
        if "w_in" in W:
            W["w_in"] = _split_w_in(W["w_in"])
        Wfull[l] = W
        first = prefetch is not None and l == 0
        h, sv, carried = _layer_fwd(h, small[l], W, biases, cs, l, prefetch[0] if first else None,
                                    late0 if l == 0 else None, early0 if l == 0 else None)
        saved.append(sv)
    dh, g_final, loss = _loss_head(h, final_norm_w, tgt, "loss_head")
    grads = [None] * DEPTH
    dbias_tot = [None] * N_DIL
    for l in reversed(range(DEPTH)):
        carry_fn = grad_reduce.carry_fn(l) if grad_reduce is not None else None
        carry2_fn = grad_reduce.carry2_fn(l) if grad_reduce is not None else None
        dh, grads[l], dbias, carried = _layer_bwd(dh, saved[l], small[l], Wfull[l], biases, cs, head_ones, l, carry_fn, carry2_fn)
        if grad_reduce is not None:
            grad_reduce.done(l, grads[l], carried)
        for g in range(N_DIL):
            dbias_tot[g] = dbias[g] if dbias_tot[g] is None else dbias_tot[g] + dbias[g]
    d_rel = jnp.concatenate(
        [_bias_scatter(dbias_tot[g].reshape(HEADS_PER_GROUP, -1), onehots[g], f"bias_scatter_g{g}").T for g in range(N_DIL)], axis=1)
    return loss, dh, grads, d_rel, g_final


def _unshard(nm, g):
    _, rows, cols = g.shape
    if nm in COL_SHARDED:
        return g.transpose(1, 0, 2).reshape(rows, N_DEV * cols)
    return g.reshape(N_DEV * rows, cols)


def _shard(nm, w):
    rows, cols = w.shape
    if nm in COL_SHARDED:
        return w.reshape(rows, N_DEV, cols // N_DEV).transpose(1, 0, 2)
    return w.reshape(N_DEV, rows // N_DEV, cols)


SMALL_LAYER = (("norm1_w", 1024), ("conv_w", 12288), ("conv_b", 3072), ("dt_bias", 32), ("a_log", 32), ("d_skip", 32),
               ("ssm_norm_w", 2048), ("norm2_w", 1024))
SMALL_GLOBAL = (("rel_bias", 768), ("final_norm_w", 1024), ("loss", 1))


def _pad128(v):
    n = v.shape[0]
    return jnp.pad(v, (0, -n % LANES))


def _pack_small(per_layer, glob):
    parts = [_pad128(per_layer[l][nm].reshape(-1)) for l in range(DEPTH) for nm, _ in SMALL_LAYER]
    parts += [_pad128(glob[nm].reshape(-1)) for nm, _ in SMALL_GLOBAL]
    flat = jnp.concatenate(parts)
    flat = jnp.pad(flat, (0, -flat.shape[0] % (8 * LANES)))
    return flat.reshape(-1, LANES)


def _unpack_small(packed):
    flat = packed.reshape(-1)
    per_layer, glob, off = [dict() for _ in range(DEPTH)], {}, 0
    for l in range(DEPTH):
        for nm, n in SMALL_LAYER:
            per_layer[l][nm] = flat[off:off + n]
            off += n + (-n % LANES)
    for nm, n in SMALL_GLOBAL:
        glob[nm] = flat[off:off + n]
        off += n + (-n % LANES)
    return per_layer, glob


def kernel(x, norm1_w, w_in, conv_w, conv_b, dt_bias, a_log, d_skip, ssm_norm_w, w_attn_branch, w_ssm_branch, w_out, norm2_w, w_ffn_in, w_ffn_out, rel_bias, final_norm_w, loss_target, m_norm1_w, m_w_in, m_conv_w, m_conv_b, m_dt_bias, m_a_log, m_d_skip, m_ssm_norm_w, m_w_attn_branch, m_w_ssm_branch, m_w_out, m_norm2_w, m_w_ffn_in, m_w_ffn_out, m_rel_bias, m_final_norm_w, v_norm1_w, v_w_in, v_conv_w, v_conv_b, v_dt_bias, v_a_log, v_d_skip, v_ssm_norm_w, v_w_attn_branch, v_w_ssm_branch, v_w_out, v_norm2_w, v_w_ffn_in, v_w_ffn_out, v_rel_bias, v_final_norm_w):
    big = dict(w_in=w_in, w_attn_branch=w_attn_branch, w_ssm_branch=w_ssm_branch, w_out=w_out, w_ffn_in=w_ffn_in, w_ffn_out=w_ffn_out)
    big_m = dict(w_in=m_w_in, w_attn_branch=m_w_attn_branch, w_ssm_branch=m_w_ssm_branch, w_out=m_w_out, w_ffn_in=m_w_ffn_in, w_ffn_out=m_w_ffn_out)
    big_v = dict(w_in=v_w_in, w_attn_branch=v_w_attn_branch, w_ssm_branch=v_w_ssm_branch, w_out=v_w_out, w_ffn_in=v_w_ffn_in, w_ffn_out=v_w_ffn_out)
    sm = dict(norm1_w=norm1_w, conv_w=conv_w, conv_b=conv_b, dt_bias=dt_bias, a_log=a_log, d_skip=d_skip, ssm_norm_w=ssm_norm_w, norm2_w=norm2_w)
    sm_m = dict(norm1_w=m_norm1_w, conv_w=m_conv_w, conv_b=m_conv_b, dt_bias=m_dt_bias, a_log=m_a_log, d_skip=m_d_skip, ssm_norm_w=m_ssm_norm_w, norm2_w=m_norm2_w)
    sm_v = dict(norm1_w=v_norm1_w, conv_w=v_conv_w, conv_b=v_conv_b, dt_bias=v_dt_bias, a_log=v_a_log, d_skip=v_d_skip, ssm_norm_w=v_ssm_norm_w, norm2_w=v_norm2_w)
    me = 4 * lax.axis_index("x") + 2 * lax.axis_index("y") + lax.axis_index("c")

    def full_weights(gathered, names=BIG):
        return {nm: _unshard(nm, g) for nm, g in zip(names, gathered)}

    later = [nm for nm in BIG if nm != "w_in"]
    Wfull = [{}, None]
    early0 = (_comm_gather_spread([big["w_in"][0].astype(BF16)]), functools.partial(full_weights, names=["w_in"]))
    late0 = (_comm_gather_spread([big[nm][0].astype(BF16) for nm in later]), functools.partial(full_weights, names=later))
    prefetch = (_comm_gather_spread([big[nm][DEPTH - 1].astype(BF16) for nm in BIG]), full_weights)

    conv_full = []
    for l in range(DEPTH):
        z = jnp.zeros((N_DEV, CONV_WIDTH, XBC_WIDTH // N_DEV), F32)
        conv_full.append(lax.dynamic_update_index_in_dim(z, conv_w[l], me, axis=0))
    cw = jnp.stack(conv_full).reshape(-1, LANES)
    cw = _all_reduce_small(cw, "gather_conv_w").reshape(DEPTH, N_DEV, CONV_WIDTH, XBC_WIDTH // N_DEV)
    cw = cw.transpose(0, 2, 1, 3).reshape(DEPTH, CONV_WIDTH, XBC_WIDTH)

    small = [{nm: (cw[l] if nm == "conv_w" else a[l]) for nm, a in sm.items()} for l in range(DEPTH)]
    grad_reduce = _GradReduce()
    loss, dx, grads, d_rel, g_final = _step_local(x[0], loss_target[0], small, Wfull, rel_bias, final_norm_w, prefetch, grad_reduce, late0, early0)
    g_big = {nm: jnp.stack([grad_reduce.out[(l, nm)] for l in range(DEPTH)]) for nm in BIG}

    per_layer = [{nm: grads[l][nm] for nm, _ in SMALL_LAYER} for l in range(DEPTH)]
    packet = _pack_small(per_layer, dict(rel_bias=d_rel, final_norm_w=g_final, loss=loss[0, :1]))
    per_layer, glob = _unpack_small(_all_reduce_small(packet, "all_reduce_small"))
    g_small = {nm: jnp.stack([per_layer[l][nm] for l in range(DEPTH)]) for nm, _ in SMALL_LAYER}
    cwg = g_small["conv_w"].reshape(DEPTH, CONV_WIDTH, N_DEV, XBC_WIDTH // N_DEV)
    g_small["conv_w"] = lax.dynamic_index_in_dim(cwg, me, axis=2, keepdims=False)
    for nm in sm:
        g_small[nm] = g_small[nm].reshape(sm[nm].shape)
    g_rel = glob["rel_bias"].reshape(rel_bias.shape)
    g_fin = glob["final_norm_w"]
    loss_out = glob["loss"][0]

    def adam(w, g, m, v, name):
        shp = w.shape
        two = lambda a: a.reshape(-1, shp[-1]) if a.ndim > 1 else a.reshape(1, -1)
        d, nm_, nv = _adamw(two(w), two(g), two(m), two(v), name)
        return d.reshape(shp), nm_.reshape(shp), nv.reshape(shp)

    order = ["norm1_w", "w_in", "conv_w", "conv_b", "dt_bias", "a_log", "d_skip", "ssm_norm_w", "w_attn_branch", "w_ssm_branch",
             "w_out", "norm2_w", "w_ffn_in", "w_ffn_out", "rel_bias", "final_norm_w"]
    allw = {**big, **sm, "rel_bias": rel_bias, "final_norm_w": final_norm_w}
    allm = {**big_m, **sm_m, "rel_bias": m_rel_bias, "final_norm_w": m_final_norm_w}
    allv = {**big_v, **sm_v, "rel_bias": v_rel_bias, "final_norm_w": v_final_norm_w}
    allg = {**g_big, **g_small, "rel_bias": g_rel, "final_norm_w": g_fin}
    deltas, new_m, new_v = [], [], []
    for nm in order:
        d, a, b = adam(allw[nm], allg[nm], allm[nm], allv[nm], "adamw_" + nm)
        deltas.append(d)
        new_m.append(a)
        new_v.append(b)
    return (loss_out, dx[None], *[allg[nm] for nm in order], *deltas, *new_m, *new_v)
```

```python
import functools
import math

import numpy as np
import jax
import jax.numpy as jnp
from jax import lax
from jax.experimental import pallas as pl
from jax.experimental.pallas import tpu as pltpu

F32, BF16 = jnp.float32, jnp.bfloat16
S = jax.ShapeDtypeStruct
MESH = pl.DeviceIdType.MESH

D_MODEL = 1024
DEPTH = 2
HEAD_DIM = 64
DILATED_GROUPS = ((128, 1), (512, 4), (2048, 16))
N_DIL = 3
HEADS_PER_GROUP = 8
ATTN_WIDTH = 1536
ATTN_OUT_WIDTH = 512
ATTN_BLOCK = 128
N_REL_BUCKETS = 32
REL_MAX_DISTANCE = 2048
D_INNER = 2048
N_SSM_HEADS = 32
N_SSM_GROUPS = 4
D_STATE = 128
CONV_WIDTH = 4
SSD_CHUNK = 128
XBC_WIDTH = 3072
D_FF = 2816
EPS = 1e-6
ADAM_LR, ADAM_B1, ADAM_B2, ADAM_EPS, ADAM_WD, ADAM_STEP = 0.001, 0.9, 0.999, 1e-08, 0.01, 10

N_DEV = 8
LANES = 128
VMEM_LIMIT = 56 * 1024 * 1024
MM_VMEM_BYTES = 40 * 1024 * 1024
ROW_TILES_BYTES = 36 * 1024 * 1024
NEG = -1e30
BIG = ("w_in", "w_attn_branch", "w_ssm_branch", "w_out", "w_ffn_in", "w_ffn_out")
COL_SHARDED = ("w_in", "w_attn_branch", "w_ffn_in")

NT = (((1,), (1,)), ((), ()))
TN = (((0,), (0,)), ((), ()))


def _cparams(sem=None):
    return pltpu.CompilerParams(dimension_semantics=sem, vmem_limit_bytes=VMEM_LIMIT)


def _pick(n, target, mult=LANES):
    best = None
    for t in range(mult, min(n, target) + 1, mult):
        if n % t == 0:
            best = t
    return best or n


def _silu(x):
    return x * jax.nn.sigmoid(x)


def _dsilu(x):
    s = jax.nn.sigmoid(x)
    return s * (1.0 + x * (1.0 - s))


def _split2(x):
    hi = x.astype(BF16)
    lo = (x - hi.astype(F32)).astype(BF16)
    return hi, lo


def _split3(x):
    x1 = x.astype(BF16)
    r1 = x - x1.astype(F32)
    x2 = r1.astype(BF16)
    x3 = (r1 - x2.astype(F32)).astype(BF16)
    return x1, x2, x3


def _dotx_r(x, m, parts=3):
    xs = _split3(x) if parts == 3 else _split2(x)
    out = jnp.dot(xs[0], m, preferred_element_type=F32)
    for xi in xs[1:]:
        out = out + jnp.dot(xi, m, preferred_element_type=F32)
    return out


def _dotx_l(m, x, parts=3):
    xs = _split3(x) if parts == 3 else _split2(x)
    out = jnp.dot(m, xs[0], preferred_element_type=F32)
    for xi in xs[1:]:
        out = out + jnp.dot(m, xi, preferred_element_type=F32)
    return out


def _mm(a, b, *, ta=False, tb=False, out_dtype=F32, acc=None, name, tm=None, tn=1536, tk=1536, extras=(), epilogue=None, outs=None, comm=None):
    M, K = (a.shape[1], a.shape[0]) if ta else a.shape
    N = b.shape[0] if tb else b.shape[1]
    tn, tk = _pick(N, tn), _pick(K, tk)
    if tm is None:
        def vmem(t, k):
            out_b = jnp.dtype(out_dtype).itemsize
            return (2 * t * k * a.dtype.itemsize + 2 * k * tn * b.dtype.itemsize + 2 * t * tn * out_b
                    + (t * tn * 4 if K > k else 0) + (2 * t * tn * acc.dtype.itemsize if acc is not None else 0))
        tm = _pick(M, 1536)
        while K % (2 * tk) == 0 and vmem(tm, 2 * tk) <= MM_VMEM_BYTES:
            tk *= 2
        while M % (2 * tm) == 0 and vmem(2 * tm, tk) <= MM_VMEM_BYTES:
            tm *= 2
    else:
        tm = _pick(M, tm)
    nk = K // tk
    dims = (((0 if ta else 1,), (1 if tb else 0,)), ((), ()))
    has_acc = acc is not None
    outs = [(N, out_dtype)] if outs is None else outs
    ne, no = len(extras), len(outs)

    def body(*refs):
        a_ref, b_ref = refs[:2]
        c_ref = refs[2] if has_acc else None
        e_refs = refs[2 + has_acc:2 + has_acc + ne]
        o_refs = refs[2 + has_acc + ne:2 + has_acc + ne + no]
        acc_ref = refs[-1]
        k = pl.program_id(2)
        part = lax.dot_general(a_ref[...].astype(BF16), b_ref[...].astype(BF16), dims, preferred_element_type=F32)

        def finish(res):
            tiles = [res] if epilogue is None else epilogue(res, *[e[...] for e in e_refs])
            for o_ref, t in zip(o_refs, tiles):
                o_ref[...] = t.astype(o_ref.dtype)

        if nk == 1:
            finish(part + c_ref[...].astype(F32) if has_acc else part)
        else:
            @pl.when(k == 0)
            def _():
                acc_ref[...] = part + c_ref[...].astype(F32) if has_acc else part

            @pl.when(jnp.logical_and(k > 0, k < nk - 1))
            def _():
                acc_ref[...] += part

            @pl.when(k == nk - 1)
            def _():
                finish(acc_ref[...] + part)

    def cspec(cols):
        return pl.BlockSpec((tm, cols * tn // N), lambda i, j, k: (i, j))

    a_spec = pl.BlockSpec((tk, tm), lambda i, j, k: (k, i)) if ta else pl.BlockSpec((tm, tk), lambda i, j, k: (i, k))
    b_spec = pl.BlockSpec((tn, tk), lambda i, j, k: (j, k)) if tb else pl.BlockSpec((tk, tn), lambda i, j, k: (k, j))
    in_specs, args = [a_spec, b_spec], [a, b]
    if has_acc:
        in_specs.append(cspec(N))
        args.append(acc)
    in_specs += [cspec(e.shape[1]) for e in extras]
    args += list(extras)
    res, comm_res = _carrier_call(
        body, (M // tm, N // tn, nk), in_specs, [cspec(c) for c, _ in outs], [S((M, c), dt) for c, dt in outs],
        [pltpu.VMEM((tm, tn), F32)] if nk > 1 else [], args, comm, name)
    res = res[0] if len(outs) == 1 else res
    return res if comm is None else (res, comm_res)


def _mm_dil(a_list, b_list, d, acc, name, tm=1024):
    T, K = a_list[0].shape
    N = b_list[0].shape[0]
    tm, tn = min(tm, T), _pick(N, 1024)
    na = len(a_list)
    has_acc = acc is not None

    def body(*refs):
        a_refs, b_refs, rest = refs[:na], refs[na:2 * na], refs[2 * na:]
        c_ref = rest[0] if has_acc else None
        o_ref, scr = rest[-2], rest[-1]
        out = c_ref[...] if has_acc else None
        for a_ref, b_ref in zip(a_refs, b_refs):
            a_tok = _dil_to_tok(scr, a_ref, d).astype(BF16) if d > 1 else a_ref[...]
            part = lax.dot_general(a_tok, b_ref[...], NT, preferred_element_type=F32)
            out = part if out is None else out + part
        o_ref[...] = out

    if d > 1:
        a_spec = pl.BlockSpec((d, tm // d, K), lambda i, j: (0, i, 0))
        a_args = [a.reshape(d, T // d, K) for a in a_list]
    else:
        a_spec = pl.BlockSpec((tm, K), lambda i, j: (i, 0))
        a_args = list(a_list)
    o_spec = pl.BlockSpec((tm, tn), lambda i, j: (i, j))
    in_specs = [a_spec] * na + [pl.BlockSpec((tn, K), lambda i, j: (j, 0))] * na + ([o_spec] if has_acc else [])
    return pl.pallas_call(
        body, grid=(T // tm, N // tn), in_specs=in_specs, out_specs=o_spec, out_shape=S((T, N), F32),
        scratch_shapes=[pltpu.VMEM((K // LANES, tm, LANES), F32)],
        compiler_params=_cparams(("parallel", "parallel")), name=name)(*a_args, *b_list, *([acc] if has_acc else []))


class _Comm:
    def __init__(self, ins, outs, sems, start, wait, alias=None):
        self.ins, self.outs, self.sems, self.start, self.wait, self.alias = list(ins), list(outs), list(sems), start, wait, alias or {}


def _carrier_call(body, grid, in_specs, out_specs, out_shape, scratch_shapes, args, comm, name):
    grid = (grid,) if isinstance(grid, int) else tuple(grid)
    seq = ("arbitrary",) * len(grid)
    ni, no, ns = len(in_specs), len(out_specs), len(scratch_shapes)
    if comm is None:
        res = pl.pallas_call(body, grid=grid, in_specs=in_specs, out_specs=out_specs, out_shape=out_shape,
                             scratch_shapes=scratch_shapes, compiler_params=_cparams(seq), name=name)(*args)
        return list(res), []
    ci, co = len(comm.ins), len(comm.outs)

    def wrapped(*refs):
        ins, cins = refs[:ni], refs[ni:ni + ci]
        outs, couts = refs[ni + ci:ni + ci + no], refs[ni + ci + no:ni + ci + no + co]
        scr, csems = refs[ni + ci + no + co:ni + ci + no + co + ns], refs[ni + ci + no + co + ns:]
        ids = [pl.program_id(i) for i in range(len(grid))]
        first = functools.reduce(jnp.logical_and, [i == 0 for i in ids])
        last = functools.reduce(jnp.logical_and, [i == g - 1 for i, g in zip(ids, grid)])

        @pl.when(first)
        def _():
            comm.start(cins, couts, csems)

        body(*ins, *outs, *scr)

        @pl.when(last)
        def _():
            comm.wait(cins, couts, csems)

    anys = pl.BlockSpec(memory_space=pl.ANY)
    res = pl.pallas_call(
        wrapped, grid=grid, in_specs=list(in_specs) + [anys] * ci, out_specs=list(out_specs) + [anys] * co,
        out_shape=list(out_shape) + comm.outs, scratch_shapes=list(scratch_shapes) + comm.sems,
        input_output_aliases={ni + a: no + b for a, b in comm.alias.items()},
        compiler_params=_cparams(seq), name=name)(*args, *comm.ins)
    return list(res[:no]), list(res[no:])


def _run_comm(comm, name):
    ci, co = len(comm.ins), len(comm.outs)

    def body(*refs):
        cins, couts, csems = refs[:ci], refs[ci:ci + co], refs[ci + co:]
        comm.start(cins, couts, csems)
        comm.wait(cins, couts, csems)

    anys = pl.BlockSpec(memory_space=pl.ANY)
    return pl.pallas_call(body, in_specs=[anys] * ci, out_specs=[anys] * co, out_shape=comm.outs, scratch_shapes=comm.sems,
                          input_output_aliases=dict(comm.alias), name=name)(*comm.ins)


def _dev_index(dev):
    return 4 * dev[0] + 2 * dev[1] + dev[2]


def _comm_gather_spread(shards):
    npc = len(shards)

    def copies(x_refs, o_refs, sems):
        x, y, c, chips = _place()
        me = (x, y, c)
        peers = [(x, y, 1 - c)] + [(*chip, c) for chip in chips]
        return [[pltpu.make_async_remote_copy(src_ref=x_refs[i], dst_ref=o_refs[i].at[_dev_index(me)], send_sem=sems[0].at[k, i],
                                              recv_sem=sems[1].at[k, i], device_id=peer, device_id_type=MESH)
                 for k, peer in enumerate(peers)] for i in range(npc)], peers, me

    def local(x_refs, o_refs, sems, me):
        return [pltpu.make_async_copy(x_refs[i], o_refs[i].at[_dev_index(me)], sems[2].at[i]) for i in range(npc)]

    def start(x_refs, o_refs, sems):
        cps, _, me = copies(x_refs, o_refs, sems)
        for cp in local(x_refs, o_refs, sems, me):
            cp.start()
        for row in cps:
            for cp in row:
                cp.start()

    def wait(x_refs, o_refs, sems):
        cps, peers, me = copies(x_refs, o_refs, sems)
        for i in range(npc):
            for k, peer in enumerate(peers):
                pltpu.make_async_remote_copy(src_ref=x_refs[i], dst_ref=o_refs[i].at[_dev_index(peer)], send_sem=sems[0].at[k, i],
                                             recv_sem=sems[1].at[k, i], device_id=peer, device_id_type=MESH).wait_recv()
        for row in cps:
            for cp in row:
                cp.wait_send()
        for cp in local(x_refs, o_refs, sems, me):
            cp.wait()

    return _Comm(shards, [S((N_DEV,) + s.shape, s.dtype) for s in shards],
                 [pltpu.SemaphoreType.DMA((4, npc)), pltpu.SemaphoreType.DMA((4, npc)), pltpu.SemaphoreType.DMA((npc,))], start, wait)


def _comm_gather_pass(gathered):
    npc = len(gathered)

    def copies(o_refs, sems, sent):
        x, y, c, chips = _place()
        return [pltpu.make_async_remote_copy(
            src_ref=o_refs[i].at[_dev_index((*chip, c))], dst_ref=o_refs[i].at[_dev_index((*chip, c if sent else 1 - c))],
            send_sem=sems[0].at[j, i], recv_sem=sems[1].at[j, i], device_id=(x, y, 1 - c), device_id_type=MESH)
            for i in range(npc) for j, chip in enumerate(chips)]

    def start(g_refs, o_refs, sems):
        for cp in copies(o_refs, sems, True):
            cp.start()

    def wait(g_refs, o_refs, sems):
        for cp in copies(o_refs, sems, False):
            cp.wait_recv()
        for cp in copies(o_refs, sems, True):
            cp.wait_send()

    return _Comm(gathered, [S(g.shape, g.dtype) for g in gathered],
                 [pltpu.SemaphoreType.DMA((3, npc)), pltpu.SemaphoreType.DMA((3, npc))], start, wait,
                 alias={i: i for i in range(npc)})


def _comm_to_chips(parts):
    npc = len(parts)

    def copies(p_refs, o_refs, sems):
        x, y, c, chips = _place()
        return [pltpu.make_async_remote_copy(
            src_ref=p_refs[i].at[2 * chip[0] + chip[1]], dst_ref=o_refs[i].at[j], send_sem=sems[0].at[j, i],
            recv_sem=sems[1].at[j, i], device_id=(*chip, c), device_id_type=MESH)
            for i in range(npc) for j, chip in enumerate(chips)]

    def start(p_refs, o_refs, sems):
        for cp in copies(p_refs, o_refs, sems):
            cp.start()

    def wait(p_refs, o_refs, sems):
        for cp in copies(p_refs, o_refs, sems):
            cp.wait()

    return _Comm(parts, [S((3,) + p.shape[1:], p.dtype) for p in parts],
                 [pltpu.SemaphoreType.DMA((3, npc)), pltpu.SemaphoreType.DMA((3, npc))], start, wait)


def _dil_to_tok(scr, ref, d):
    n, C = ref.shape[1], ref.shape[2]
    for r in range(d):
        v = ref[r].astype(F32)
        for cb in range(C // LANES):
            scr.at[cb][pl.ds(r, n, stride=d), :] = v[:, cb * LANES:(cb + 1) * LANES]
    return jnp.concatenate([scr[cb] for cb in range(C // LANES)], axis=1)


def _tok_to_dil(scr, val, ref, d):
    n, C = ref.shape[1], ref.shape[2]
    for cb in range(C // LANES):
        scr[cb] = val[:, cb * LANES:(cb + 1) * LANES].astype(F32)
    for r in range(d):
        ref[r] = jnp.concatenate([scr.at[cb][pl.ds(r, n, stride=d), :] for cb in range(C // LANES)], axis=1).astype(ref.dtype)


def _rowwise(fn, rows, fulls, outs, accs=(), *, tm, name, cap=True, comm=None):
    rows = [r if isinstance(r, tuple) else (r, r.shape[1], 0) for r in rows]
    first = rows[0]
    T = (first[1] if isinstance(first[0], str) else first[0]).shape[0]
    widest = max([r[1].shape[1] if isinstance(r[0], str) else r[1] for r in rows] + [o[0] for o in outs])
    if cap:
        tm = min(tm, max(8, ROW_TILES_BYTES // (2 * (len(rows) + len(outs))) // (4 * widest) // 8 * 8))
    tm = T if T <= tm else _pick(T, tm, 8)
    nr, nf, no, na = len(rows), len(fulls), len(outs), len(accs)
    dil_in = [i for i, r in enumerate(rows) if isinstance(r[0], str) and r[2] > 1]
    dil_out = [i for i, o in enumerate(outs) if len(o) == 3 and o[2] > 1]
    scr_cols = [rows[i][1].shape[1] for i in dil_in] + [outs[i][0] for i in dil_out]

    def body(*refs):
        r, f = refs[:nr], refs[nr:nr + nf]
        o, a = refs[nr + nf:nr + nf + no], refs[nr + nf + no:nr + nf + no + na]
        scr = refs[nr + nf + no + na:]
        tiles = []
        for i, x in enumerate(r):
            if i in dil_in:
                tiles.append(_dil_to_tok(scr[dil_in.index(i)], x, rows[i][2]))
            else:
                tiles.append(x[...].astype(F32))
        ro, ra = fn(*tiles, *[x[...] for x in f])
        for i, (ref, val) in enumerate(zip(o, ro)):
            if i in dil_out:
                _tok_to_dil(scr[len(dil_in) + dil_out.index(i)], val, ref, outs[i][2])
            else:
                ref[...] = val.astype(ref.dtype)
        if na:
            @pl.when(pl.program_id(0) == 0)
            def _():
                for ref in a:
                    ref[...] = jnp.zeros_like(ref)
            for ref, val in zip(a, ra):
                ref[...] += val

    in_specs, args = [], []
    for i, rr in enumerate(rows):
        if isinstance(rr[0], str):
            arr, d = rr[1], rr[2]
            if d > 1:
                in_specs.append(pl.BlockSpec((d, tm // d, arr.shape[1]), lambda i: (0, i, 0)))
                args.append(arr.reshape(d, T // d, arr.shape[1]))
            else:
                in_specs.append(pl.BlockSpec((tm, arr.shape[1]), lambda i: (i, 0)))
                args.append(arr)
        else:
            in_specs.append(pl.BlockSpec((tm, rr[1]), functools.partial(lambda i, cb: (i, cb), cb=rr[2])))
            args.append(rr[0])
    in_specs += [pl.BlockSpec(f.shape, lambda i: (0, 0)) for f in fulls]
    out_specs, out_shape = [], []
    for i, oo in enumerate(outs):
        if i in dil_out:
            d = oo[2]
            out_specs.append(pl.BlockSpec((d, tm // d, oo[0]), lambda i: (0, i, 0)))
            out_shape.append(S((d, T // d, oo[0]), oo[1]))
        else:
            out_specs.append(pl.BlockSpec((tm, oo[0]), lambda i: (i, 0)))
            out_shape.append(S((T, oo[0]), oo[1]))
    out_specs += [pl.BlockSpec(sh, lambda i: (0, 0)) for sh in accs]
    out_shape += [S(sh, F32) for sh in accs]
    res, comm_res = _carrier_call(
        body, T // tm, in_specs, out_specs, out_shape, [pltpu.VMEM((c // LANES, tm, LANES), F32) for c in scr_cols],
        list(args) + list(fulls), comm, name)
    res = [x.reshape(T, x.shape[2]) if i in dil_out else x for i, x in enumerate(res)]
    return res if comm is None else (res, comm_res)


def _rmsnorm_fwd(h, w, name, dils=(), comm=None):
    D = h.shape[1]

    def fn(h, w):
        r = lax.rsqrt(jnp.mean(h * h, axis=-1, keepdims=True) + EPS)
        xn = h * r * w
        return [xn] * (1 + len(dils)), []
    return _rowwise(fn, [h], [w.reshape(1, -1)], [(D, BF16)] + [(D, BF16, d) for d in dils], tm=512, name=name, comm=comm)


def _rmsnorm_bwd(dxn, h, w, dres, name):
    def fn(dxn, h, dres, w):
        r = lax.rsqrt(jnp.mean(h * h, axis=-1, keepdims=True) + EPS)
        n = h * r
        dn = dxn * w
        dh = r * (dn - n * jnp.mean(dn * n, axis=-1, keepdims=True)) + dres
        return [dh], [jnp.sum(dxn * n, axis=0, keepdims=True)]
    D = h.shape[1]
    return _rowwise(fn, [dxn, h, dres], [w.reshape(1, -1)], [(D, F32)], [(1, D)], tm=512, name=name)


def _loss_head(h, w, tgt, name):
    D = h.shape[1]

    def fn(h, tgt, w):
        r = lax.rsqrt(jnp.mean(h * h, axis=-1, keepdims=True) + EPS)
        n = h * r
        e = n * w - tgt
        row_loss = 0.5 * jnp.mean(e * e, axis=-1, keepdims=True)
        dy = e * (1.0 / D)
        dn = dy * w
        dh = r * (dn - n * jnp.mean(dn * n, axis=-1, keepdims=True))
        return [dh], [jnp.sum(dy * n, axis=0, keepdims=True), jnp.broadcast_to(jnp.sum(row_loss, axis=0, keepdims=True), (1, LANES))]
    return _rowwise(fn, [h, tgt], [w.reshape(1, -1)], [(D, F32)], [(1, D), (1, LANES)], tm=256, name=name)


def _combine_fwd(os_, ls, name):
    def fn(o0, o1, o2, l0, l1, l2):
        m = jnp.maximum(jnp.maximum(l0, l1), l2)
        e0, e1, e2 = jnp.exp(l0 - m), jnp.exp(l1 - m), jnp.exp(l2 - m)
        attn = (e0 * o0 + e1 * o1 + e2 * o2) / (e0 + e1 + e2)
        return [attn, attn], []
    dil = [("dil", t, d) for t, (_, d) in zip(list(os_) + list(ls), DILATED_GROUPS * 2)]
    return _rowwise(fn, dil, [], [(ATTN_OUT_WIDTH, BF16), (ATTN_OUT_WIDTH, F32)], tm=512, name=name)


def _combine_bwd(dattn, attn, ls, head_ones, name):
    def fn(dattn, attn, l0, l1, l2, ones):
        m = jnp.maximum(jnp.maximum(l0, l1), l2)
        e0, e1, e2 = jnp.exp(l0 - m), jnp.exp(l1 - m), jnp.exp(l2 - m)
        inv = 1.0 / (e0 + e1 + e2)
        t = _dotx_r(dattn * attn, ones, parts=2)
        outs = []
        for e in (e0, e1, e2):
            al = e * inv
            outs += [al * dattn, al * t]
        return outs, []
    W = ATTN_OUT_WIDTH
    dil = [("dil", t, d) for t, (_, d) in zip(ls, DILATED_GROUPS)]
    outs = [(W, dt, d) for _, d in DILATED_GROUPS for dt in (BF16, F32)]
    return _rowwise(fn, [dattn, attn] + dil, [head_ones], outs, tm=512, name=name)


def _dt_fwd(dt_raw, dt_bias, name):
    def fn(raw, b):
        z = raw[:, :N_SSM_HEADS] + b
        return [jnp.maximum(z, 0.0) + jnp.log(1.0 + jnp.exp(-jnp.abs(z)))], []
    return _rowwise(fn, [dt_raw], [dt_bias.reshape(1, -1)], [(N_SSM_HEADS, F32)], tm=1024, name=name)[0]


def _dt_bwd(ddt_a, ddt_b, dt_raw, dt_bias, name):
    def fn(da, db, raw, b):
        g = (da + db) * jax.nn.sigmoid(raw[:, :N_SSM_HEADS] + b)
        pad = jnp.zeros((g.shape[0], LANES - N_SSM_HEADS), F32)
        return [jnp.concatenate([g, pad], axis=1)], [jnp.sum(g, axis=0, keepdims=True)]
    return _rowwise(fn, [ddt_a, ddt_b, dt_raw], [dt_bias.reshape(1, -1)], [(LANES, BF16)], [(1, N_SSM_HEADS)], tm=1024, name=name)


def _ssm_norm_fwd(y, z, w, name, comm=None):
    G = D_INNER // N_SSM_GROUPS

    def fn(y, z, w):
        yg = y * _silu(z)
        outs = []
        for g in range(N_SSM_GROUPS):
            t = yg[:, g * G:(g + 1) * G]
            outs.append(t * lax.rsqrt(jnp.mean(t * t, axis=-1, keepdims=True) + EPS))
        return [jnp.concatenate(outs, axis=1) * w], []
    res = _rowwise(fn, [y, z], [w.reshape(1, -1)], [(D_INNER, BF16)], tm=256, name=name, comm=comm)
    return res[0] if comm is None else (res[0][0], res[1])


def _ssm_norm_bwd(dssm, y, z, w, name):
    G = D_INNER // N_SSM_GROUPS

    def fn(dssm, y, z, w):
        sz = _silu(z)
        yg = y * sz
        dn = dssm * w
        ns, dygs = [], []
        for g in range(N_SSM_GROUPS):
            t = yg[:, g * G:(g + 1) * G]
            r = lax.rsqrt(jnp.mean(t * t, axis=-1, keepdims=True) + EPS)
            n = t * r
            d = dn[:, g * G:(g + 1) * G]
            dygs.append(r * (d - n * jnp.mean(d * n, axis=-1, keepdims=True)))
            ns.append(n)
        n, dyg = jnp.concatenate(ns, axis=1), jnp.concatenate(dygs, axis=1)
        return [dyg * sz, dyg * y * _dsilu(z)], [jnp.sum(dssm * n, axis=0, keepdims=True)]
    return _rowwise(fn, [dssm, y, z], [w.reshape(1, -1)], [(D_INNER, BF16), (D_INNER, BF16)], [(1, D_INNER)], tm=256, name=name)


def _gate_out_proj(a, sb, gl, h, w_out, name):
    def fn(a, sb, gl, h, w):
        g = jax.nn.sigmoid(gl)
        merged = (g[:, :D_MODEL] * a + g[:, D_MODEL:] * sb).astype(BF16)
        return [h + jnp.dot(merged, w, preferred_element_type=F32), merged], []
    return _rowwise(fn, [a, sb, gl, h], [w_out], [(D_MODEL, F32), (D_MODEL, BF16)], tm=512, cap=False, name=name)


def _d_out_proj_gate(dh, a, sb, gl, w_out, name):
    def fn(dh, a, sb, gl, w):
        dm = lax.dot_general(dh.astype(BF16), w, NT, preferred_element_type=F32)
        g = jax.nn.sigmoid(gl)
        g0, g1 = g[:, :D_MODEL], g[:, D_MODEL:]
        dgl = jnp.concatenate([dm * a * g0 * (1.0 - g0), dm * sb * g1 * (1.0 - g1)], axis=1)
        return [g0 * dm, g1 * dm, dgl], []
    return _rowwise(fn, [dh, a, sb, gl], [w_out], [(D_MODEL, BF16), (D_MODEL, BF16), (2 * D_MODEL, BF16)], tm=512, cap=False, name=name)


FFN_HALF = D_FF // 2


def _ffn_perm(w):
    h = FFN_HALF
    return jnp.concatenate([w[:, 0:h], w[:, D_FF:D_FF + h], w[:, h:D_FF], w[:, D_FF + h:]], axis=1)


def _ffn_unperm(w):
    h = FFN_HALF
    return jnp.concatenate([w[:, 0:h], w[:, 2 * h:3 * h], w[:, h:2 * h], w[:, 3 * h:]], axis=1)


def _swiglu_epilogue(res):
    return [res, _silu(res[:, :FFN_HALF]) * res[:, FFN_HALF:]]


def _dswiglu_epilogue(dact, u):
    u = u.astype(F32)
    gate, up = u[:, :FFN_HALF], u[:, FFN_HALF:]
    return [jnp.concatenate([dact * up * _dsilu(gate), dact * _silu(gate)], axis=1)]


def _adamw(w, g, m, v, name):
    c1 = 1.0 - ADAM_B1 ** ADAM_STEP
    c2 = 1.0 - ADAM_B2 ** ADAM_STEP

    def fn(w, g, m, v):
        m = ADAM_B1 * m + (1.0 - ADAM_B1) * g
        v = ADAM_B2 * v + (1.0 - ADAM_B2) * (g * g)
        delta = -ADAM_LR * ((m / c1) / (jnp.sqrt(v / c2) + ADAM_EPS) + ADAM_WD * w)
        return [delta, m, v], []
    C = w.shape[1]
    return _rowwise(fn, [w, g, m, v], [], [(C, F32)] * 3, tm=256, name=name)


def _bias_consts(dilation, n_steps):
    qi = np.arange(ATTN_BLOCK)[:, None]
    kj = np.arange(2 * ATTN_BLOCK)[None, :]
    steps = qi + ATTN_BLOCK - kj
    valid = (steps >= 0) & (steps <= n_steps)
    dist = jnp.asarray(np.clip(steps, 0, n_steps) * dilation, jnp.int32)
    max_exact = N_REL_BUCKETS // 2
    d_f = jnp.maximum(dist, 1).astype(F32)
    large = max_exact + (jnp.log(d_f / max_exact) / math.log(REL_MAX_DISTANCE / max_exact)
                         * (N_REL_BUCKETS - max_exact)).astype(jnp.int32)
    large = jnp.minimum(large, N_REL_BUCKETS - 1)
    bucket = jnp.where(dist < max_exact, dist, large).reshape(-1)
    onehot = (bucket[None, :] == jnp.arange(N_REL_BUCKETS)[:, None]).astype(F32)
    return onehot, jnp.asarray(valid.reshape(1, -1), F32)


def _bias_gather(rel_g_t, onehot, valid, name):
    def body(r_ref, oh_ref, v_ref, o_ref):
        b = jnp.dot(r_ref[...], oh_ref[...], preferred_element_type=F32, precision=lax.Precision.HIGHEST)
        o_ref[...] = jnp.where(v_ref[...] > 0.5, b, NEG)
    return pl.pallas_call(body, out_shape=S((HEADS_PER_GROUP, onehot.shape[1]), F32), compiler_params=_cparams(), name=name)(rel_g_t, onehot, valid)


def _bias_scatter(dbias, onehot, name):
    def body(d_ref, oh_ref, o_ref):
        o_ref[...] = lax.dot_general(d_ref[...], oh_ref[...], NT, preferred_element_type=F32, precision=lax.Precision.HIGHEST)
    return pl.pallas_call(body, out_shape=S((HEADS_PER_GROUP, N_REL_BUCKETS), F32), compiler_params=_cparams(), name=name)(dbias, onehot)


ATTN_QB_FWD, ATTN_QB_BWD = 4, 4


def _attn_tiles(T, d, qb):
    seg = T // d
    nqb = min(qb, seg // ATTN_BLOCK)
    tq = nqb * ATTN_BLOCK
    return seg, nqb, tq, seg // tq


def _attn_fwd(qkv, bias, d, name, comm=None):
    T = qkv.shape[0]
    seg, nqb, tq, ns = _attn_tiles(T, d, ATTN_QB_FWD)
    W = ATTN_OUT_WIDTH
    scale = HEAD_DIM ** -0.5

    def body(q_ref, kh_ref, kc_ref, vh_ref, vc_ref, b_ref, o_ref, l_ref, s_scr, p_scr):
        n = pl.program_id(1)
        qv = q_ref[...]
        kk = jnp.concatenate([kh_ref[...], kc_ref[...]], axis=0)
        vv = jnp.concatenate([vh_ref[...], vc_ref[...]], axis=0)
        col = lax.broadcasted_iota(jnp.int32, (ATTN_BLOCK, 2 * ATTN_BLOCK), 1)
        kill = jnp.logical_and(n == 0, col < ATTN_BLOCK)
        lo = lax.broadcasted_iota(jnp.int32, (1, LANES), 1) < HEAD_DIM
        zero = jnp.zeros((), BF16)
        for j in range(nqb):
            rows = slice(j * ATTN_BLOCK, (j + 1) * ATTN_BLOCK)
            keys = slice(j * ATTN_BLOCK, (j + 2) * ATTN_BLOCK)
            for hp in range(HEADS_PER_GROUP // 2):
                ps = slice(hp * LANES, (hp + 1) * LANES)
                q2 = (qv[rows, ps].astype(F32) * scale).astype(BF16)
                k2 = kk[keys, ps]
                s_scr[2 * hp] = lax.dot_general(q2, jnp.where(lo, k2, zero), NT, preferred_element_type=F32)
                s_scr[2 * hp + 1] = lax.dot_general(q2, jnp.where(lo, zero, k2), NT, preferred_element_type=F32)
            s = s_scr[...] + b_ref[...]
            if j == 0:
                s = jnp.where(kill[None], NEG, s)
            m = jnp.max(s, axis=-1, keepdims=True)
            p = jnp.exp(s - m)
            den = jnp.sum(p, axis=-1, keepdims=True)
            p_scr[...] = p.astype(BF16)
            inv = 1.0 / den
            lse = m + jnp.log(den)
            for hp in range(HEADS_PER_GROUP // 2):
                ps = slice(hp * LANES, (hp + 1) * LANES)
                v2 = vv[keys, ps]
                o2 = (jnp.dot(p_scr[2 * hp], jnp.where(lo, v2, zero), preferred_element_type=F32)
                      + jnp.dot(p_scr[2 * hp + 1], jnp.where(lo, zero, v2), preferred_element_type=F32))
                o_ref[rows, ps] = (o2 * jnp.where(lo, inv[2 * hp], inv[2 * hp + 1])).astype(o_ref.dtype)
                l_ref[rows, ps] = jnp.where(lo, lse[2 * hp], lse[2 * hp + 1])

    def cur(c):
        return pl.BlockSpec((tq, W), lambda r, n: (r * ns + n, c))

    def halo(c):
        return pl.BlockSpec((ATTN_BLOCK, W), lambda r, n: (jnp.maximum((r * ns + n) * nqb - 1, 0), c))

    res, comm_res = _carrier_call(
        body, (d, ns),
        [cur(0), halo(1), cur(1), halo(2), cur(2), pl.BlockSpec(bias.shape, lambda r, n: (0, 0, 0))],
        [cur(0), cur(0)], [S((T, W), BF16), S((T, W), F32)],
        [pltpu.VMEM((HEADS_PER_GROUP, ATTN_BLOCK, 2 * ATTN_BLOCK), F32), pltpu.VMEM((HEADS_PER_GROUP, ATTN_BLOCK, 2 * ATTN_BLOCK), BF16)],
        [qkv, qkv, qkv, qkv, qkv, bias], comm, name)
    return res if comm is None else (res, comm_res)


def _attn_bwd(qkv, bias, lse, do, dd, d, name):
    T = qkv.shape[0]
    seg, nqb, tq, ns = _attn_tiles(T, d, ATTN_QB_BWD)
    W = ATTN_OUT_WIDTH
    B = ATTN_BLOCK
    scale = HEAD_DIM ** -0.5

    def body(q_ref, kh_ref, kc_ref, vh_ref, vc_ref, b_ref, l_ref, do_ref, dd_ref, dq_ref, dk_ref, dv_ref, db_ref, pk_ref, pv_ref,
             s_scr, dp_scr, p_scr, ds_scr):
        r, n = pl.program_id(0), pl.program_id(1)

        @pl.when(jnp.logical_and(r == 0, n == 0))
        def _():
            db_ref[...] = jnp.zeros_like(db_ref)

        @pl.when(n == 0)
        def _():
            pk_ref[...] = jnp.zeros_like(pk_ref)
            pv_ref[...] = jnp.zeros_like(pv_ref)

        @pl.when(n < ns)
        def _():
            qv = q_ref[...]
            kk = jnp.concatenate([kh_ref[...], kc_ref[...]], axis=0)
            vv = jnp.concatenate([vh_ref[...], vc_ref[...]], axis=0)
            lse_v, do_v, dd_v = l_ref[...], do_ref[...], dd_ref[...]
            col = lax.broadcasted_iota(jnp.int32, (B, 2 * B), 1)
            kill = jnp.logical_and(n == 0, col < B)
            dqs = [[None] * (HEADS_PER_GROUP // 2) for _ in range(nqb)]
            dks = [[None] * (HEADS_PER_GROUP // 2) for _ in range(nqb)]
            dvs = [[None] * (HEADS_PER_GROUP // 2) for _ in range(nqb)]
            H, HP = HEADS_PER_GROUP, HEADS_PER_GROUP // 2
            do_b = do_v.astype(BF16)
            lo = lax.broadcasted_iota(jnp.int32, (1, LANES), 1) < HEAD_DIM
            zero = jnp.zeros((), BF16)
            first = lambda t: jnp.where(lo, t, zero)
            second = lambda t: jnp.where(lo, zero, t)
            for j in range(nqb):
                rows = slice(j * B, (j + 1) * B)
                keys = slice(j * B, (j + 2) * B)
                for hp in range(HP):
                    ps = slice(hp * LANES, (hp + 1) * LANES)
                    q2 = (qv[rows, ps].astype(F32) * scale).astype(BF16)
                    k2, v2, do2 = kk[keys, ps], vv[keys, ps], do_b[rows, ps]
                    s_scr[2 * hp] = lax.dot_general(q2, first(k2), NT, preferred_element_type=F32)
                    s_scr[2 * hp + 1] = lax.dot_general(q2, second(k2), NT, preferred_element_type=F32)
                    dp_scr[2 * hp] = lax.dot_general(do2, first(v2), NT, preferred_element_type=F32)
                    dp_scr[2 * hp + 1] = lax.dot_general(do2, second(v2), NT, preferred_element_type=F32)
                lse_h = jnp.stack([lse_v[rows, h * HEAD_DIM:h * HEAD_DIM + 1] for h in range(H)], axis=0)
                dd_h = jnp.stack([dd_v[rows, h * HEAD_DIM:h * HEAD_DIM + 1] for h in range(H)], axis=0)
                s = s_scr[...] + b_ref[...]
                if j == 0:
                    s = jnp.where(kill[None], NEG, s)
                p = jnp.exp(s - lse_h)
                ds = p * (dp_scr[...] - dd_h)
                db_ref[...] += ds
                p_scr[...] = p.astype(BF16)
                ds_scr[...] = ds.astype(BF16)
                for hp in range(HP):
                    ps = slice(hp * LANES, (hp + 1) * LANES)
                    q2 = (qv[rows, ps].astype(F32) * scale).astype(BF16)
                    k2, do2 = kk[keys, ps], do_b[rows, ps]
                    pa, pb, da, db_ = p_scr[2 * hp], p_scr[2 * hp + 1], ds_scr[2 * hp], ds_scr[2 * hp + 1]
                    dvs[j][hp] = (lax.dot_general(pa, first(do2), TN, preferred_element_type=F32)
                                  + lax.dot_general(pb, second(do2), TN, preferred_element_type=F32))
                    dqs[j][hp] = (jnp.dot(da, first(k2), preferred_element_type=F32)
                                  + jnp.dot(db_, second(k2), preferred_element_type=F32)) * scale
                    dks[j][hp] = (lax.dot_general(da, first(q2), TN, preferred_element_type=F32)
                                  + lax.dot_general(db_, second(q2), TN, preferred_element_type=F32))
            dq_ref[...] = jnp.concatenate([jnp.concatenate(dqs[j], axis=1) for j in range(nqb)], axis=0).astype(dq_ref.dtype)
            for parts, out_ref, pend in ((dks, dk_ref, pk_ref), (dvs, dv_ref, pv_ref)):
                full = [jnp.concatenate(parts[j], axis=1) for j in range(nqb)]
                if tq > B:
                    out_ref[:tq - B] = pend[:tq - B].astype(out_ref.dtype)
                out_ref[tq - B:] = (pend[tq - B:] + full[0][:B]).astype(out_ref.dtype)
                for j in range(nqb - 1):
                    pend[j * B:(j + 1) * B] = full[j][B:] + full[j + 1][:B]
                pend[tq - B:] = full[nqb - 1][B:]

        @pl.when(n == ns)
        def _():
            dk_ref[...] = pk_ref[...].astype(dk_ref.dtype)
            dv_ref[...] = pv_ref[...].astype(dv_ref.dtype)

    def cur(c):
        return pl.BlockSpec((tq, W), lambda r, n: (r * ns + jnp.minimum(n, ns - 1), c))

    def halo(c):
        return pl.BlockSpec((B, W), lambda r, n: (jnp.maximum((r * ns + jnp.minimum(n, ns - 1)) * nqb - 1, 0), c))

    late = pl.BlockSpec((tq, W), lambda r, n: (r * ns + jnp.clip(n - 1, 0, ns - 1), 0))
    bspec = pl.BlockSpec(bias.shape, lambda r, n: (0, 0, 0))
    return pl.pallas_call(
        body, grid=(d, ns + 1),
        in_specs=[cur(0), halo(1), cur(1), halo(2), cur(2), bspec, cur(0), cur(0), cur(0)],
        out_specs=[cur(0), late, late, bspec],
        out_shape=[S((T, W), BF16)] * 3 + [S(bias.shape, F32)],
        scratch_shapes=[pltpu.VMEM((tq, W), F32), pltpu.VMEM((tq, W), F32)]
                       + [pltpu.VMEM((HEADS_PER_GROUP, B, 2 * B), t) for t in (F32, F32, BF16, BF16)],
        compiler_params=_cparams(("arbitrary", "arbitrary")), name=name,
    )(qkv, qkv, qkv, qkv, qkv, bias, lse, do, dd)


CONV_TM, CONV_TC = 512, 1024


def _shift_down(x, halo8, s, row8):
    xr = pltpu.roll(x, s, 0)
    first = jnp.where(row8 < s, pltpu.roll(halo8, s, 0), xr[:8])
    return jnp.concatenate([first, xr[8:]], axis=0)


def _shift_up(x, halo8, s, row8):
    n = x.shape[0]
    xr = pltpu.roll(x, n - s, 0)
    last = jnp.where(row8 >= 8 - s, pltpu.roll(halo8, 8 - s, 0), xr[n - 8:])
    return jnp.concatenate([xr[:n - 8], last], axis=0)


def _conv_fwd(x, w, b, name, comm=None):
    T, C = x.shape
    tm, tc = min(CONV_TM, T), CONV_TC

    def body(x_ref, p_ref, w_ref, b_ref, u_ref, a_ref):
        ti = pl.program_id(1)
        row8 = lax.broadcasted_iota(jnp.int32, (8, LANES), 0)
        for c in range(tc // LANES):
            cs = slice(c * LANES, (c + 1) * LANES)
            xv = x_ref[:, cs]
            p8 = jnp.where(ti == 0, 0.0, p_ref[:, cs])
            wv = w_ref[:, cs]
            u = xv * wv[3:4] + b_ref[:, cs]
            for s in (1, 2, 3):
                u = u + _shift_down(xv, p8, s, row8) * wv[3 - s:4 - s]
            u_ref[:, cs] = u
            a_ref[:, cs] = _silu(u)

    cur = pl.BlockSpec((tm, tc), lambda cj, ti: (ti, cj))
    halo = pl.BlockSpec((8, tc), lambda cj, ti: (jnp.maximum(ti * (tm // 8) - 1, 0), cj))
    res, comm_res = _carrier_call(
        body, (C // tc, T // tm),
        [cur, halo, pl.BlockSpec((CONV_WIDTH, tc), lambda cj, ti: (0, cj)), pl.BlockSpec((1, tc), lambda cj, ti: (0, cj))],
        [cur, cur], [S((T, C), F32)] * 2, [], [x, x, w, b], comm, name)
    return res if comm is None else (res, comm_res)


def _conv_bwd(dact, u, x, w, name):
    T, C = x.shape
    tm, tc = min(CONV_TM, T), CONV_TC
    nt = T // tm

    def body(d_ref, dn_ref, u_ref, un_ref, x_ref, w_ref, dx_ref, dw_ref, db_ref):
        ti = pl.program_id(1)

        @pl.when(ti == 0)
        def _():
            dw_ref[...] = jnp.zeros_like(dw_ref)
            db_ref[...] = jnp.zeros_like(db_ref)

        row8 = lax.broadcasted_iota(jnp.int32, (8, LANES), 0)
        for c in range(tc // LANES):
            cs = slice(c * LANES, (c + 1) * LANES)
            du = d_ref[:, cs] * _dsilu(u_ref[:, cs])
            dun = jnp.where(ti == nt - 1, 0.0, dn_ref[:, cs] * _dsilu(un_ref[:, cs]))
            xv = x_ref[:, cs]
            wv = w_ref[:, cs]
            dx = du * wv[3:4]
            dws = [None] * CONV_WIDTH
            dws[3] = jnp.sum(du * xv, axis=0, keepdims=True)
            for s in (1, 2, 3):
                up = _shift_up(du, dun, s, row8)
                dx = dx + up * wv[3 - s:4 - s]
                dws[3 - s] = jnp.sum(up * xv, axis=0, keepdims=True)
            dx_ref[:, cs] = dx.astype(dx_ref.dtype)
            dw_ref[:, cs] += jnp.concatenate(dws, axis=0)
            db_ref[:, cs] += jnp.sum(du, axis=0, keepdims=True)

    cur = pl.BlockSpec((tm, tc), lambda cj, ti: (ti, cj))
    nxt = pl.BlockSpec((8, tc), lambda cj, ti: (jnp.minimum((ti + 1) * (tm // 8), T // 8 - 1), cj))
    return pl.pallas_call(
        body, grid=(C // tc, nt),
        in_specs=[cur, nxt, cur, nxt, cur, pl.BlockSpec((CONV_WIDTH, tc), lambda cj, ti: (0, cj))],
        out_specs=[cur, pl.BlockSpec((CONV_WIDTH, tc), lambda cj, ti: (0, cj)), pl.BlockSpec((1, tc), lambda cj, ti: (0, cj))],
        out_shape=[S((T, C), BF16), S((CONV_WIDTH, C), F32), S((1, C), F32)],
        compiler_params=_cparams(("parallel", "arbitrary")), name=name)(dact, dact, u, u, x, w)


def _ssd_consts():
    i = np.arange(SSD_CHUNK)
    tril = (i[None, :] <= i[:, None]).astype(np.float32)
    trils = (i[None, :] < i[:, None]).astype(np.float32)
    head = np.repeat(np.arange(N_SSM_HEADS), D_INNER // N_SSM_HEADS)
    et = (head[None, :] == np.arange(N_SSM_HEADS)[:, None]).astype(np.float32)
    c = lambda a: jnp.asarray(a, BF16)
    return dict(tril=c(tril), triu=c(tril.T), trils=c(trils), et=c(et), e=c(et.T))


def _ssd_common(act_ref, dt_ref, dtT_ref, al_ref, alT_ref, tril_ref, triu_ref, et_ref):
    a_row = -jnp.exp(al_ref[...])
    a_col = -jnp.exp(alT_ref[...])
    dt, dtT = dt_ref[...], dtT_ref[...]
    la = _dotx_l(tril_ref[...], dt * a_row)
    laT = _dotx_r(dtT * a_col, triu_ref[...])
    et = et_ref[...]
    la_e = _dotx_r(la, et)
    dt_e = _dotx_r(dt, et, parts=2)
    x = act_ref[:, :D_INNER]
    xdt = x * dt_e
    la_q = la_e[SSD_CHUNK - 1:SSD_CHUNK, :]
    return a_row, a_col, dt, dtT, la, laT, la_e, dt_e, x, xdt, la_q


def _decay(la, laT, h, causal):
    seg = la[:, h:h + 1] - laT[h:h + 1, :]
    return jnp.exp(jnp.where(causal, seg, NEG))


def _ssd_fwd(act, dt, dtT, alog, dskip_e, cs, name, comm=None):
    T = act.shape[0]
    nc = T // SSD_CHUNK
    Q, G, GW = SSD_CHUNK, N_SSM_GROUPS, D_INNER // N_SSM_GROUPS

    def body(act_ref, dt_ref, dtT_ref, al_ref, alT_ref, dsk_ref, tril_ref, triu_ref, et_ref, y_ref, st_ref, scr):
        @pl.when(pl.program_id(0) == 0)
        def _():
            scr[...] = jnp.zeros_like(scr)
        st_ref[0] = scr[...]
        a_row, a_col, dtv, dtTv, la, laT, la_e, dt_e, x, xdt, la_q = _ssd_common(
            act_ref, dt_ref, dtT_ref, al_ref, alT_ref, tril_ref, triu_ref, et_ref)
        ela = jnp.exp(la_e)
        xdt_b = xdt.astype(BF16)
        xdte_b = (xdt * jnp.exp(la_q - la_e)).astype(BF16)
        ela_q = jnp.exp(la_q)
        causal = lax.broadcasted_iota(jnp.int32, (Q, Q), 0) >= lax.broadcasted_iota(jnp.int32, (Q, Q), 1)
        for g in range(G):
            gs = slice(g * GW, (g + 1) * GW)
            Bg = act_ref[:, D_INNER + g * D_STATE:D_INNER + (g + 1) * D_STATE].astype(BF16)
            Cg = act_ref[:, D_INNER + G * D_STATE + g * D_STATE:D_INNER + G * D_STATE + (g + 1) * D_STATE].astype(BF16)
            cb = lax.dot_general(Cg, Bg, NT, preferred_element_type=F32)
            st = scr[g]
            y_inter = jnp.dot(Cg, st.astype(BF16), preferred_element_type=F32) * ela[:, gs]
            ys = []
            for hh in range(HEADS_PER_GROUP):
                h = g * HEADS_PER_GROUP + hh
                m = (cb * _decay(la, laT, h, causal)).astype(BF16)
                ys.append(jnp.dot(m, xdt_b[:, h * HEAD_DIM:(h + 1) * HEAD_DIM], preferred_element_type=F32))
            y_ref[:, gs] = (jnp.concatenate(ys, axis=1) + y_inter + x[:, gs] * dsk_ref[:, gs]).astype(y_ref.dtype)
            scr[g] = st * ela_q[:, gs] + lax.dot_general(Bg, xdte_b[:, gs], TN, preferred_element_type=F32)

    full = lambda a: pl.BlockSpec(a.shape, lambda c: (0,) * a.ndim)
    al, alT = alog.reshape(1, -1), alog.reshape(-1, 1)
    res, comm_res = _carrier_call(
        body, nc,
        [pl.BlockSpec((Q, XBC_WIDTH), lambda c: (c, 0)), pl.BlockSpec((Q, N_SSM_HEADS), lambda c: (c, 0)),
         pl.BlockSpec((N_SSM_HEADS, Q), lambda c: (0, c)), full(al), full(alT), full(dskip_e),
         full(cs["tril"]), full(cs["triu"]), full(cs["et"])],
        [pl.BlockSpec((Q, D_INNER), lambda c: (c, 0)), pl.BlockSpec((1, G, D_STATE, GW), lambda c: (c, 0, 0, 0))],
        [S((T, D_INNER), BF16), S((nc, G, D_STATE, GW), F32)],
        [pltpu.VMEM((G, D_STATE, GW), F32)],
        [act, dt, dtT, al, alT, dskip_e, cs["tril"], cs["triu"], cs["et"]], comm, name)
    return res if comm is None else (res, comm_res)


def _ssd_bwd(dy, act, dt, dtT, alog, dskip_e, states, cs, name, comm=None):
    T = act.shape[0]
    nc = T // SSD_CHUNK
    Q, G, GW, H = SSD_CHUNK, N_SSM_GROUPS, D_INNER // N_SSM_GROUPS, N_SSM_HEADS

    def body(dy_ref, act_ref, dt_ref, dtT_ref, al_ref, alT_ref, dsk_ref, stp_ref, tril_ref, triu_ref, trils_ref,
             et_ref, e_ref, dact_ref, ddt_ref, ddtT_ref, da_ref, daT_ref, dsk_out_ref, dst, wbuf, ubuf, vbuf, sbuf, dm_scr, m_scr):
        @pl.when(pl.program_id(0) == 0)
        def _():
            dst[...] = jnp.zeros_like(dst)
            da_ref[...] = jnp.zeros_like(da_ref)
            daT_ref[...] = jnp.zeros_like(daT_ref)
            dsk_out_ref[...] = jnp.zeros_like(dsk_out_ref)
        a_row, a_col, dtv, dtTv, la, laT, la_e, dt_e, x, xdt, la_q = _ssd_common(
            act_ref, dt_ref, dtT_ref, al_ref, alT_ref, tril_ref, triu_ref, et_ref)
        dyv = dy_ref[...].astype(F32)
        ela = jnp.exp(la_e)
        e_end = jnp.exp(la_q - la_e)
        ela_q = jnp.exp(la_q)
        dye_b = (dyv * ela).astype(BF16)
        dy_b = dyv.astype(BF16)
        xdt_b = xdt.astype(BF16)
        xdte_b = (xdt * e_end).astype(BF16)
        ri = lax.broadcasted_iota(jnp.int32, (Q, Q), 0)
        ci = lax.broadcasted_iota(jnp.int32, (Q, Q), 1)
        causal = ri >= ci
        rows = []
        for g in range(G):
            gs = slice(g * GW, (g + 1) * GW)
            Bg = act_ref[:, D_INNER + g * D_STATE:D_INNER + (g + 1) * D_STATE].astype(BF16)
            Cg = act_ref[:, D_INNER + G * D_STATE + g * D_STATE:D_INNER + G * D_STATE + (g + 1) * D_STATE].astype(BF16)
            cb = lax.dot_general(Cg, Bg, NT, preferred_element_type=F32)
            stp = stp_ref[0, g]
            stp_b = stp.astype(BF16)
            dstv = dst[g]
            dst_b = dstv.astype(BF16)
            y_inter = jnp.dot(Cg, stp_b, preferred_element_type=F32) * ela[:, gs]
            wbuf[:, gs] = dyv[:, gs] * y_inter
            dxdt_state = jnp.dot(Bg, dst_b, preferred_element_type=F32) * e_end[:, gs]
            ubuf[:, gs] = dxdt_state * xdt[:, gs]
            dC = lax.dot_general(dye_b[:, gs], stp_b, NT, preferred_element_type=F32)
            dB = lax.dot_general(xdte_b[:, gs], dst_b, NT, preferred_element_type=F32)
            sbuf[:, gs] = jnp.broadcast_to(jnp.sum(dstv * stp, axis=0, keepdims=True), (8, GW))
            dst[g] = dstv * ela_q[:, gs] + lax.dot_general(Cg, dye_b[:, gs], TN, preferred_element_type=F32)
            for hh in range(HEADS_PER_GROUP):
                hs = slice((g * HEADS_PER_GROUP + hh) * HEAD_DIM, (g * HEADS_PER_GROUP + hh + 1) * HEAD_DIM)
                dm_scr[hh] = lax.dot_general(dy_b[:, hs], xdt_b[:, hs], NT, preferred_element_type=F32)
            dG = jnp.zeros((Q, Q), F32)
            for hh in range(HEADS_PER_GROUP):
                L = _decay(la, laT, g * HEADS_PER_GROUP + hh, causal)
                M = cb * L
                dM = dm_scr[hh]
                dG = dG + dM * L
                W = dM * M
                rows.append(jnp.sum(W.T, axis=0, keepdims=True) - jnp.sum(W, axis=0, keepdims=True))
                m_scr[hh] = M.astype(BF16)
            dxs = []
            for hh in range(HEADS_PER_GROUP):
                hs = slice((g * HEADS_PER_GROUP + hh) * HEAD_DIM, (g * HEADS_PER_GROUP + hh + 1) * HEAD_DIM)
                dxs.append(lax.dot_general(m_scr[hh], dy_b[:, hs], TN, preferred_element_type=F32))
            dG_b = dG.astype(BF16)
            dC = dC + jnp.dot(dG_b, Bg, preferred_element_type=F32)
            dB = dB + lax.dot_general(dG_b, Cg, TN, preferred_element_type=F32)
            dxdt = jnp.concatenate(dxs, axis=1) + dxdt_state
            vbuf[:, gs] = dxdt * x[:, gs]
            dact_ref[:, gs] = dxdt * dt_e[:, gs] + dyv[:, gs] * dsk_ref[:, gs]
            dact_ref[:, D_INNER + g * D_STATE:D_INNER + (g + 1) * D_STATE] = dB
            dact_ref[:, D_INNER + G * D_STATE + g * D_STATE:D_INNER + G * D_STATE + (g + 1) * D_STATE] = dC
        e = e_ref[...]
        w = _dotx_r(wbuf[...], e, parts=2)
        u = _dotx_r(ubuf[...], e, parts=2)
        vx = _dotx_r(vbuf[...], e, parts=2)
        dsk = _dotx_r(jnp.broadcast_to(jnp.sum(dyv * x, axis=0, keepdims=True), (8, D_INNER)), e, parts=2)[0:1]
        s0 =_dotx_r(sbuf[...], e, parts=2)[0:1] * jnp.exp(la[Q - 1:Q, :])
        ddelta = _dotx_l(triu_ref[...], w) + _dotx_l(trils_ref[...], u) + s0
        ddt_ref[...] = ddelta * a_row + vx
        ddeltaT = _dotx_r(jnp.concatenate(rows, axis=0), tril_ref[...])
        ddtT_ref[...] = ddeltaT * a_col
        da_ref[...] += jnp.sum(ddelta * dtv, axis=0, keepdims=True)
        daT_ref[...] += jnp.sum(ddeltaT * dtTv, axis=1, keepdims=True)
        dsk_out_ref[...] += dsk

    rev = lambda c: nc - 1 - c
    full = lambda a: pl.BlockSpec(a.shape, lambda c: (0,) * a.ndim)
    al, alT = alog.reshape(1, -1), alog.reshape(-1, 1)
    consts = [cs[k] for k in ("tril", "triu", "trils", "et", "e")]
    res, comm_res = _carrier_call(
        body, nc,
        [pl.BlockSpec((Q, D_INNER), lambda c: (rev(c), 0)), pl.BlockSpec((Q, XBC_WIDTH), lambda c: (rev(c), 0)),
         pl.BlockSpec((Q, H), lambda c: (rev(c), 0)), pl.BlockSpec((H, Q), lambda c: (0, rev(c))),
         full(al), full(alT), full(dskip_e), pl.BlockSpec((1, G, D_STATE, GW), lambda c: (rev(c), 0, 0, 0))]
        + [full(a) for a in consts],
        [pl.BlockSpec((Q, XBC_WIDTH), lambda c: (rev(c), 0)), pl.BlockSpec((Q, H), lambda c: (rev(c), 0)),
         pl.BlockSpec((H, Q), lambda c: (0, rev(c))), pl.BlockSpec((1, H), lambda c: (0, 0)),
         pl.BlockSpec((H, 1), lambda c: (0, 0)), pl.BlockSpec((1, H), lambda c: (0, 0))],
        [S((T, XBC_WIDTH), F32), S((T, H), F32), S((H, T), F32), S((1, H), F32), S((H, 1), F32), S((1, H), F32)],
        [pltpu.VMEM((G, D_STATE, GW), F32), pltpu.VMEM((Q, D_INNER), F32), pltpu.VMEM((Q, D_INNER), F32),
         pltpu.VMEM((Q, D_INNER), F32), pltpu.VMEM((8, D_INNER), F32),
         pltpu.VMEM((HEADS_PER_GROUP, Q, Q), F32), pltpu.VMEM((HEADS_PER_GROUP, Q, Q), BF16)],
        [dy, act, dt, dtT, al, alT, dskip_e, states] + consts, comm, name)
    return res if comm is None else (res, comm_res)


def _a_log_grad(da, daT_row, alog, name):
    def body(a_ref, b_ref, al_ref, o_ref):
        o_ref[...] = (a_ref[...] + b_ref[...]) * (-jnp.exp(al_ref[...]))
    return pl.pallas_call(body, out_shape=S((1, N_SSM_HEADS), F32), name=name)(da, daT_row, alog.reshape(1, -1))


def _place():
    x, y, c = lax.axis_index("x"), lax.axis_index("y"), lax.axis_index("c")
    return x, y, c, [(1 - x, y), (x, 1 - y), (1 - x, 1 - y)]


def _to_sibling(to_sib, name):
    npc = len(to_sib)

    def body(*refs):
        s_refs, o_refs, send_sems, recv_sems = refs[:npc], refs[npc:2 * npc], refs[2 * npc], refs[2 * npc + 1]
        x, y, c, _ = _place()
        cps = [pltpu.make_async_remote_copy(
            src_ref=s_refs[i], dst_ref=o_refs[i], send_sem=send_sems.at[i], recv_sem=recv_sems.at[i],
            device_id=(x, y, 1 - c), device_id_type=MESH) for i in range(npc)]
        for cp in cps:
            cp.start()
        for cp in cps:
            cp.wait()

    anys = pl.BlockSpec(memory_space=pl.ANY)
    return pl.pallas_call(
        body, in_specs=[anys] * npc, out_specs=[anys] * npc, out_shape=[S(s.shape, s.dtype) for s in to_sib],
        scratch_shapes=[pltpu.SemaphoreType.DMA((npc,)), pltpu.SemaphoreType.DMA((npc,))],
        name=name)(*to_sib)


def _rs_begin(pieces, name):
    c = lax.axis_index("c")
    by_core = [p.reshape(4, 2, p.shape[1], p.shape[2]) for p in pieces]
    to_sib = [lax.dynamic_index_in_dim(p, 1 - c, axis=1, keepdims=False).astype(BF16) for p in by_core]
    keep = [lax.dynamic_index_in_dim(p, c, axis=1, keepdims=False) for p in by_core]
    from_sib = _to_sibling(to_sib, name + "_d2d")

    def add1(a, b):
        s = a + b
        return [s, s], []

    parts, parts_b = [], []
    for i, (k, f) in enumerate(zip(keep, from_sib)):
        _, r, C = k.shape
        p, pb = _rowwise(add1, [k.reshape(4 * r, C), f.reshape(4 * r, C)], [], [(C, F32), (C, BF16)], tm=2048, name=f"{name}_add1_{i}")
        parts.append(p.reshape(4, r, C))
        parts_b.append(pb.reshape(4, r, C))
    return parts, parts_b


def _rs_finish(parts, got, name):
    x, y = lax.axis_index("x"), lax.axis_index("y")

    def add2(a, b, c_, d_):
        return [((a + b) + c_) + d_], []

    outs = []
    for i, (p, g) in enumerate(zip(parts, got)):
        own = lax.dynamic_index_in_dim(p, 2 * x + y, axis=0, keepdims=False)
        outs.append(_rowwise(add2, [own, g[0], g[1], g[2]], [], [(p.shape[2], F32)], tm=2048, name=f"{name}_add2_{i}")[0])
    return outs


class _GradReduce:
    EARLY = ("w_ffn_in", "w_ffn_out", "w_out", "w_attn_branch", "w_ssm_branch")

    def __init__(self):
        self.out, self.keys, self.parts, self.parts_b = {}, [], [], []

    def _begin(self, l, names, gr, name):
        parts, parts_b = _rs_begin([_shard(nm, gr[nm]) for nm in names], name)
        self.keys += [(l, nm) for nm in names]
        self.parts += parts
        self.parts_b += parts_b

    def carry_fn(self, l):
        if l != 0:
            return None

        def fn(gr):
            self._begin(0, self.EARLY, gr, "rs_early_l0")
            return _comm_to_chips(self.parts_b)
        return fn

    def carry2_fn(self, l):
        if l != 0:
            return None

        def fn(gr):
            self.parts2, parts2_b = _rs_begin([_shard("w_in", gr["w_in"])], "rs_w_in_l0")
            return _comm_to_chips(parts2_b)
        return fn

    def done(self, l, gr, carried):
        if l == DEPTH - 1:
            self._begin(l, BIG, gr, f"rs_l{l}")
            return
        for key, o in zip(self.keys, _rs_finish(self.parts, carried[0], "rs_carried")):
            self.out[key] = o
        self.out[(0, "w_in")] = _rs_finish(self.parts2, carried[1], "rs_w_in_l0")[0]


def _all_reduce_small(v, name):
    R, C = v.shape

    def body(x_ref, out_ref, buf, send_sems, recv_sems):
        x, y, c, chips = _place()
        me, sib = (x, y, c), (x, y, 1 - c)

        def rows(dev):
            return buf.at[4 * dev[0] + 2 * dev[1] + dev[2]]

        def copy(k, block, to, src=None):
            return pltpu.make_async_remote_copy(
                src_ref=rows(block) if src is None else src, dst_ref=rows(block),
                send_sem=send_sems.at[k], recv_sem=recv_sems.at[k], device_id=to, device_id_type=MESH)

        buf[4 * x + 2 * y + c] = x_ref[...]
        first = [copy(0, me, sib, src=x_ref)] + [copy(1 + j, me, (*chip, c), src=x_ref) for j, chip in enumerate(chips)]
        for cp in first:
            cp.start()
        passed = [copy(4 + j, (*chip, c), sib) for j, chip in enumerate(chips)]
        for j, chip in enumerate(chips):
            copy(1 + j, (*chip, c), me).wait_recv()
            passed[j].start()
        copy(0, sib, me).wait_recv()
        for j, chip in enumerate(chips):
            copy(4 + j, (*chip, 1 - c), me).wait_recv()
        for cp in first + passed:
            cp.wait_send()
        acc = buf[0]
        for j in range(1, N_DEV):
            acc = acc + buf[j]
        out_ref[...] = acc

    vm = pl.BlockSpec(memory_space=pltpu.VMEM)
    return pl.pallas_call(
        body, in_specs=[vm], out_specs=vm, out_shape=S((R, C), F32),
        scratch_shapes=[pltpu.VMEM((N_DEV, R, C), F32), pltpu.SemaphoreType.DMA((7,)), pltpu.SemaphoreType.DMA((7,))],
        compiler_params=pltpu.CompilerParams(vmem_limit_bytes=VMEM_LIMIT), name=name)(v)


SEG = (("q", 0, 1536), ("k", 1536, 1536), ("v", 3072, 1536), ("z", 4608, 2048), ("xbc", 6656, 3072), ("dt", 9728, 32), ("gl", 9760, 2048))


def _split_w_in(w_in_full):
    out = {}
    for nm, off, n in SEG:
        w = w_in_full[:, off:off + n]
        if nm == "dt":
            w = jnp.pad(w, ((0, 0), (0, LANES - n)))
        out[nm] = w
    W = ATTN_OUT_WIDTH
    out["qkv"] = [jnp.concatenate([out[s][:, g * W:(g + 1) * W] for s in ("q", "k", "v")], axis=1) for g in range(N_DIL)]
    out["qkv_t"] = [[out[s][:, g * W:(g + 1) * W] for s in ("q", "k", "v")] for g in range(N_DIL)]
    return out


def _layer_fwd(h, p, W, biases, cs, l, carry=None, late=None, early=None):
    T = h.shape[0]
    nm = lambda s: f"{s}_l{l}"
    sv = {"h_in": h}
    dils = [d for _, d in DILATED_GROUPS[1:]]
    if early is None:
        xns = _rmsnorm_fwd(h, p["norm1_w"], nm("norm1"), dils=dils)
    else:
        xns, spread = _rmsnorm_fwd(h, p["norm1_w"], nm("norm1"), dils=dils, comm=early[0])
        W["w_in"] = _split_w_in(early[1](_run_comm(_comm_gather_pass(spread), nm("pass_w_in")))["w_in"])
    xn = xns[0]
    wi = W["w_in"]
    z = _mm(xn, wi["z"], out_dtype=BF16, name=nm("proj_z"))
    xbc = _mm(xn, wi["xbc"], name=nm("proj_xbc"))
    dt_raw = _mm(xn, wi["dt"], name=nm("proj_dt"))
    gl = _mm(xn, wi["gl"], out_dtype=BF16, name=nm("proj_gl"))
    conv_args = (xbc, p["conv_w"], p["conv_b"].reshape(1, -1), nm("conv"))
    if late is None:
        u_conv, act = _conv_fwd(*conv_args)
    else:
        (u_conv, act), spread = _conv_fwd(*conv_args, comm=late[0])
    os_, ls, qkvs = [], [], []
    for g, (window, dil) in enumerate(DILATED_GROUPS):
        qkv = _mm(xns[g], wi["qkv"][g], out_dtype=BF16, name=nm(f"proj_qkv_g{g}"))
        if late is not None and g == 0:
            (o, lse), arrived = _attn_fwd(qkv, biases[g], dil, nm(f"attn_fwd_g{g}"), comm=_comm_gather_pass(spread))
            W.update(late[1](arrived))
        else:
            o, lse = _attn_fwd(qkv, biases[g], dil, nm(f"attn_fwd_g{g}"))
        os_.append(o)
        ls.append(lse)
        qkvs.append(qkv)
    attn_b, attn_f = _combine_fwd(os_, ls, nm("combine"))
    dt = _dt_fwd(dt_raw, p["dt_bias"], nm("dt"))
    dtT = dt.T
    dskip_e = jnp.repeat(p["d_skip"], D_INNER // N_SSM_HEADS).reshape(1, -1)
    carried = None
    if carry is None:
        y, states = _ssd_fwd(act, dt, dtT, p["a_log"], dskip_e, cs, nm("ssd_fwd"))
        ssm = _ssm_norm_fwd(y, z, p["ssm_norm_w"], nm("ssm_norm"))
    else:
        (y, states), spread = _ssd_fwd(act, dt, dtT, p["a_log"], dskip_e, cs, nm("ssd_fwd"), comm=carry)
        ssm, carried = _ssm_norm_fwd(y, z, p["ssm_norm_w"], nm("ssm_norm"), comm=_comm_gather_pass(spread))
    a_br = _mm(attn_b, W["w_attn_branch"], out_dtype=BF16, name=nm("attn_branch"))
    s_br = _mm(ssm, W["w_ssm_branch"], out_dtype=BF16, name=nm("ssm_branch"))
    h_mid, merged = _gate_out_proj(a_br, s_br, gl, h, W["w_out"], nm("gate_out_proj"))
    xn2 = _rmsnorm_fwd(h_mid, p["norm2_w"], nm("norm2"))[0]
    W["w_ffn_in_p"] = _ffn_perm(W["w_ffn_in"])
    u_ffn, ffn_act = _mm(xn2, W["w_ffn_in_p"], tm=512, tn=D_FF, epilogue=_swiglu_epilogue, outs=[(2 * D_FF, BF16), (D_FF, BF16)],
                         name=nm("ffn_in_swiglu"))
    h_out = _mm(ffn_act, W["w_ffn_out"], acc=h_mid, name=nm("ffn_out"))
    sv.update(xn=xn, xns=xns, qkvs=qkvs, z=z, xbc=xbc, dt_raw=dt_raw, gl=gl, ls=ls, attn_b=attn_b, attn_f=attn_f, u_conv=u_conv,
              act=act, dt=dt, dtT=dtT, dskip_e=dskip_e, y=y, states=states, ssm=ssm, a_br=a_br, s_br=s_br, merged=merged,
              h_mid=h_mid, xn2=xn2, u_ffn=u_ffn, ffn_act=ffn_act)
    return h_out, sv, carried


def _layer_bwd(dh, sv, p, W, biases, cs, head_ones, l, carry_fn=None, carry2_fn=None):
    T = dh.shape[0]
    nm = lambda s: f"{s}_l{l}"
    gr = {}
    du = _mm(dh, W["w_ffn_out"], tb=True, tm=512, tn=FFN_HALF, extras=[sv["u_ffn"]], epilogue=_dswiglu_epilogue, outs=[(2 * D_FF, BF16)],
             name=nm("d_ffn_act_swiglu"))
    gr["w_ffn_out"] = _mm(sv["ffn_act"], dh, ta=True, name=nm("g_ffn_out"))
    dxn2 = _mm(du, W["w_ffn_in_p"], tb=True, out_dtype=BF16, name=nm("d_xn2"))
    gr["w_ffn_in"] = _ffn_unperm(_mm(sv["xn2"], du, ta=True, name=nm("g_ffn_in")))
    dh_mid, gr["norm2_w"] = _rmsnorm_bwd(dxn2, sv["h_mid"], p["norm2_w"], dh, nm("d_norm2"))
    gr["w_out"] = _mm(sv["merged"], dh_mid, ta=True, name=nm("g_out"))
    d_a, d_s, dgl = _d_out_proj_gate(dh_mid, sv["a_br"], sv["s_br"], sv["gl"], W["w_out"], nm("d_out_proj_gate"))
    dattn = _mm(d_a, W["w_attn_branch"], tb=True, out_dtype=BF16, name=nm("d_attn"))
    gr["w_attn_branch"] = _mm(sv["attn_b"], d_a, ta=True, name=nm("g_attn_branch"))
    dssm = _mm(d_s, W["w_ssm_branch"], tb=True, out_dtype=BF16, name=nm("d_ssm"))
    gr["w_ssm_branch"] = _mm(sv["ssm"], d_s, ta=True, name=nm("g_ssm_branch"))
    dy, dz, gr["ssm_norm_w"] = _ssm_norm_bwd(dssm, sv["y"], sv["z"], p["ssm_norm_w"], nm("d_ssm_norm"))
    ssd_args = (dy, sv["act"], sv["dt"], sv["dtT"], p["a_log"], sv["dskip_e"], sv["states"], cs, nm("ssd_bwd"))
    carried = None
    if carry_fn is None:
        dact_c, ddt_a, ddt_bT, da, daT, dskip = _ssd_bwd(*ssd_args)
    else:
        (dact_c, ddt_a, ddt_bT, da, daT, dskip), carried = _ssd_bwd(*ssd_args, comm=carry_fn(gr))
    gr["a_log"] = _a_log_grad(da, daT.T, p["a_log"], nm("g_a_log")).reshape(-1)
    gr["d_skip"] = dskip.reshape(-1)
    ddt_raw, ddt_bias = _dt_bwd(ddt_a, ddt_bT.T, sv["dt_raw"], p["dt_bias"], nm("d_dt"))
    gr["dt_bias"] = ddt_bias.reshape(-1)
    dxbc, gr["conv_w"], dconv_b = _conv_bwd(dact_c, sv["u_conv"], sv["xbc"], p["conv_w"], nm("d_conv"))
    gr["conv_b"] = dconv_b.reshape(-1)
    outs = _combine_bwd(dattn, sv["attn_f"], sv["ls"], head_ones, nm("d_combine"))
    wi = W["w_in"]
    dbias, dxn = [], None
    gqkv = [[None] * N_DIL for _ in range(3)]
    for g, (window, dil) in enumerate(DILATED_GROUPS):
        dq, dk, dv, db = _attn_bwd(sv["qkvs"][g], biases[g], sv["ls"][g], outs[2 * g], outs[2 * g + 1], dil, nm(f"attn_bwd_g{g}"))
        dbias.append(db)
        dxn = _mm_dil([dq, dk, dv], wi["qkv_t"][g], dil, dxn, nm(f"d_xn_qkv_g{g}"))
        for i, dseg in enumerate((dq, dk, dv)):
            gqkv[i][g] = _mm(sv["xns"][g], dseg, ta=True, name=nm(f"g_in_{'qkv'[i]}_g{g}"))
    parts = (("z", dz), ("xbc", dxbc), ("dt", ddt_raw), ("gl", dgl))
    gws = gqkv[0] + gqkv[1] + gqkv[2]
    for sname, dseg in parts:
        gw = _mm(sv["xn"], dseg, ta=True, name=nm("g_in_" + sname))
        gws.append(gw[:, :N_SSM_HEADS] if sname == "dt" else gw)
    gr["w_in"] = jnp.concatenate(gws, axis=1)
    carried2 = None
    for sname, dseg in parts:
        if carry2_fn is not None and sname == "xbc":
            dxn, carried2 = _mm(dseg, wi[sname], tb=True, acc=dxn, name=nm("d_xn_" + sname), comm=carry2_fn(gr))
        else:
            dxn = _mm(dseg, wi[sname], tb=True, acc=dxn, name=nm("d_xn_" + sname))
    dh_in, gr["norm1_w"] = _rmsnorm_bwd(dxn, sv["h_in"], p["norm1_w"], dh_mid, nm("d_norm1"))
    return dh_in, gr, dbias, (carried, carried2)


def _step_local(x, tgt, small, Wfull, rel_bias, final_norm_w, prefetch=None, grad_reduce=None, late0=None, early0=None):
    cs = _ssd_consts()
    head = np.repeat(np.arange(HEADS_PER_GROUP), HEAD_DIM)
    head_ones = jnp.asarray(head[:, None] == head[None, :], BF16)
    biases, onehots = [], []
    for g, (window, dil) in enumerate(DILATED_GROUPS):
        onehot, valid = _bias_consts(dil, window // dil)
        rel_g_t = rel_bias[:, g * HEADS_PER_GROUP:(g + 1) * HEADS_PER_GROUP].T
        b = _bias_gather(rel_g_t, onehot, valid, f"bias_gather_g{g}")
        biases.append(b.reshape(HEADS_PER_GROUP, ATTN_BLOCK, 2 * ATTN_BLOCK))
        onehots.append(onehot)
    h, saved, carried = x, [], None
    Wfull = list(Wfull)
    for l in range(DEPTH):
        W = dict(prefetch[1](carried) if Wfull[l] is None else Wfull[l])
        if "w_in" in W:
            W["w_in"] = _split_w_in(W["w_in"])
        Wfull[l] = W
        first = prefetch is not None and l == 0
        h, sv, carried = _layer_fwd(h, small[l], W, biases, cs, l, prefetch[0] if first else None,
                                    late0 if l == 0 else None, early0 if l == 0 else None)
        saved.append(sv)
    dh, g_final, loss = _loss_head(h, final_norm_w, tgt, "loss_head")
    grads = [None] * DEPTH
    dbias_tot = [None] * N_DIL
    for l in reversed(range(DEPTH)):
        carry_fn = grad_reduce.carry_fn(l) if grad_reduce is not None else None
        carry2_fn = grad_reduce.carry2_fn(l) if grad_reduce is not None else None
        dh, grads[l], dbias, carried = _layer_bwd(dh, saved[l], small[l], Wfull[l], biases, cs, head_ones, l, carry_fn, carry2_fn)
        if grad_reduce is not None:
            grad_reduce.done(l, grads[l], carried)
        for g in range(N_DIL):
            dbias_tot[g] = dbias[g] if dbias_tot[g] is None else dbias_tot[g] + dbias[g]
    d_rel = jnp.concatenate(
        [_bias_scatter(dbias_tot[g].reshape(HEADS_PER_GROUP, -1), onehots[g], f"bias_scatter_g{g}").T for g in range(N_DIL)], axis=1)
    return loss, dh, grads, d_rel, g_final


def _unshard(nm, g):
    _, rows, cols = g.shape
    if nm in COL_SHARDED:
        return g.transpose(1, 0, 2).reshape(rows, N_DEV * cols)
    return g.reshape(N_DEV * rows, cols)


def _shard(nm, w):
    rows, cols = w.shape
    if nm in COL_SHARDED:
        return w.reshape(rows, N_DEV, cols // N_DEV).transpose(1, 0, 2)
    return w.reshape(N_DEV, rows // N_DEV, cols)


SMALL_LAYER = (("norm1_w", 1024), ("conv_w", 12288), ("conv_b", 3072), ("dt_bias", 32), ("a_log", 32), ("d_skip", 32),
               ("ssm_norm_w", 2048), ("norm2_w", 1024))
SMALL_GLOBAL = (("rel_bias", 768), ("final_norm_w", 1024), ("loss", 1))


def _pad128(v):
    n = v.shape[0]
    return jnp.pad(v, (0, -n % LANES))


def _pack_small(per_layer, glob):
    parts = [_pad128(per_layer[l][nm].reshape(-1)) for l in range(DEPTH) for nm, _ in SMALL_LAYER]
    parts += [_pad128(glob[nm].reshape(-1)) for nm, _ in SMALL_GLOBAL]
    flat = jnp.concatenate(parts)
    flat = jnp.pad(flat, (0, -flat.shape[0] % (8 * LANES)))
    return flat.reshape(-1, LANES)


def _unpack_small(packed):
    flat = packed.reshape(-1)
    per_layer, glob, off = [dict() for _ in range(DEPTH)], {}, 0
    for l in range(DEPTH):
        for nm, n in SMALL_LAYER:
            per_layer[l][nm] = flat[off:off + n]
            off += n + (-n % LANES)
    for nm, n in SMALL_GLOBAL:
        glob[nm] = flat[off:off + n]
        off += n + (-n % LANES)
    return per_layer, glob


def kernel(x, norm1_w, w_in, conv_w, conv_b, dt_bias, a_log, d_skip, ssm_norm_w, w_attn_branch, w_ssm_branch, w_out, norm2_w, w_ffn_in, w_ffn_out, rel_bias, final_norm_w, loss_target, m_norm1_w, m_w_in, m_conv_w, m_conv_b, m_dt_bias, m_a_log, m_d_skip, m_ssm_norm_w, m_w_attn_branch, m_w_ssm_branch, m_w_out, m_norm2_w, m_w_ffn_in, m_w_ffn_out, m_rel_bias, m_final_norm_w, v_norm1_w, v_w_in, v_conv_w, v_conv_b, v_dt_bias, v_a_log, v_d_skip, v_ssm_norm_w, v_w_attn_branch, v_w_ssm_branch, v_w_out, v_norm2_w, v_w_ffn_in, v_w_ffn_out, v_rel_bias, v_final_norm_w):
    big = dict(w_in=w_in, w_attn_branch=w_attn_branch, w_ssm_branch=w_ssm_branch, w_out=w_out, w_ffn_in=w_ffn_in, w_ffn_out=w_ffn_out)
    big_m = dict(w_in=m_w_in, w_attn_branch=m_w_attn_branch, w_ssm_branch=m_w_ssm_branch, w_out=m_w_out, w_ffn_in=m_w_ffn_in, w_ffn_out=m_w_ffn_out)
    big_v = dict(w_in=v_w_in, w_attn_branch=v_w_attn_branch, w_ssm_branch=v_w_ssm_branch, w_out=v_w_out, w_ffn_in=v_w_ffn_in, w_ffn_out=v_w_ffn_out)
    sm = dict(norm1_w=norm1_w, conv_w=conv_w, conv_b=conv_b, dt_bias=dt_bias, a_log=a_log, d_skip=d_skip, ssm_norm_w=ssm_norm_w, norm2_w=norm2_w)
    sm_m = dict(norm1_w=m_norm1_w, conv_w=m_conv_w, conv_b=m_conv_b, dt_bias=m_dt_bias, a_log=m_a_log, d_skip=m_d_skip, ssm_norm_w=m_ssm_norm_w, norm2_w=m_norm2_w)
    sm_v = dict(norm1_w=v_norm1_w, conv_w=v_conv_w, conv_b=v_conv_b, dt_bias=v_dt_bias, a_log=v_a_log, d_skip=v_d_skip, ssm_norm_w=v_ssm_norm_w, norm2_w=v_norm2_w)
    me = 4 * lax.axis_index("x") + 2 * lax.axis_index("y") + lax.axis_index("c")

    def full_weights(gathered, names=BIG):
        return {nm: _unshard(nm, g) for nm, g in zip(names, gathered)}

    later = [nm for nm in BIG if nm != "w_in"]
    Wfull = [{}, None]
    early0 = (_comm_gather_spread([big["w_in"][0].astype(BF16)]), functools.partial(full_weights, names=["w_in"]))
    late0 = (_comm_gather_spread([big[nm][0].astype(BF16) for nm in later]), functools.partial(full_weights, names=later))
    prefetch = (_comm_gather_spread([big[nm][DEPTH - 1].astype(BF16) for nm in BIG]), full_weights)

    conv_full = []
    for l in range(DEPTH):
        z = jnp.zeros((N_DEV, CONV_WIDTH, XBC_WIDTH // N_DEV), F32)
        conv_full.append(lax.dynamic_update_index_in_dim(z, conv_w[l], me, axis=0))
    cw = jnp.stack(conv_full).reshape(-1, LANES)
    cw = _all_reduce_small(cw, "gather_conv_w").reshape(DEPTH, N_DEV, CONV_WIDTH, XBC_WIDTH // N_DEV)
    cw = cw.transpose(0, 2, 1, 3).reshape(DEPTH, CONV_WIDTH, XBC_WIDTH)

    small = [{nm: (cw[l] if nm == "conv_w" else a[l]) for nm, a in sm.items()} for l in range(DEPTH)]
    grad_reduce = _GradReduce()
    loss, dx, grads, d_rel, g_final = _step_local(x[0], loss_target[0], small, Wfull, rel_bias, final_norm_w, prefetch, grad_reduce, late0, early0)
    g_big = {nm: jnp.stack([grad_reduce.out[(l, nm)] for l in range(DEPTH)]) for nm in BIG}

    per_layer = [{nm: grads[l][nm] for nm, _ in SMALL_LAYER} for l in range(DEPTH)]
    packet = _pack_small(per_layer, dict(rel_bias=d_rel, final_norm_w=g_final, loss=loss[0, :1]))
    per_layer, glob = _unpack_small(_all_reduce_small(packet, "all_reduce_small"))
    g_small = {nm: jnp.stack([per_layer[l][nm] for l in range(DEPTH)]) for nm, _ in SMALL_LAYER}
    cwg = g_small["conv_w"].reshape(DEPTH, CONV_WIDTH, N_DEV, XBC_WIDTH // N_DEV)
    g_small["conv_w"] = lax.dynamic_index_in_dim(cwg, me, axis=2, keepdims=False)
    for nm in sm:
        g_small[nm] = g_small[nm].reshape(sm[nm].shape)
    g_rel = glob["rel_bias"].reshape(rel_bias.shape)
    g_fin = glob["final_norm_w"]
    loss_out = glob["loss"][0]

    def adam(w, g, m, v, name):
        shp = w.shape
        two = lambda a: a.reshape(-1, shp[-1]) if a.ndim > 1 else a.reshape(1, -1)
        d, nm_, nv = _adamw(two(w), two(g), two(m), two(v), name)
        return d.reshape(shp), nm_.reshape(shp), nv.reshape(shp)

    order = ["norm1_w", "w_in", "conv_w", "conv_b", "dt_bias", "a_log", "d_skip", "ssm_norm_w", "w_attn_branch", "w_ssm_branch",
             "w_out", "norm2_w", "w_ffn_in", "w_ffn_out", "rel_bias", "final_norm_w"]
    allw = {**big, **sm, "rel_bias": rel_bias, "final_norm_w": final_norm_w}
    allm = {**big_m, **sm_m, "rel_bias": m_rel_bias, "final_norm_w": m_final_norm_w}
    allv = {**big_v, **sm_v, "rel_bias": v_rel_bias, "final_norm_w": v_final_norm_w}
    allg = {**g_big, **g_small, "rel_bias": g_rel, "final_norm_w": g_fin}
    deltas, new_m, new_v = [], [], []
    for nm in order:
        d, a, b = adam(allw[nm], allg[nm], allm[nm], allv[nm], "adamw_" + nm)
        deltas.append(d)
        new_m.append(a)
        new_v.append(b)
    return (loss_out, dx[None], *[allg[nm] for nm in order], *deltas, *new_m, *new_v)
```
